```python
import math
import jax, jax.numpy as jnp
from jax import lax
import numpy as np

D_MODEL = 1024
BATCH = 8
SEQ = 4096
DEPTH = 1

HEAD_DIM = 64
FOX_HEADS = D_MODEL // (2 * HEAD_DIM)
FOX_WIDTH = FOX_HEADS * HEAD_DIM
MLA_HEADS = D_MODEL // (2 * HEAD_DIM)
MLA_NOPE_DIM = 64
MLA_ROPE_DIM = 32
MLA_QK_DIM = MLA_NOPE_DIM + MLA_ROPE_DIM
MLA_V_DIM = 64
MLA_WIDTH = MLA_HEADS * MLA_V_DIM
MIX_WIDTH = FOX_WIDTH + MLA_WIDTH
Q_LORA_RANK = 3 * D_MODEL // 8
KV_LORA_RANK = D_MODEL // 4
D_FF = 4 * D_MODEL
BLOCK_Q = 128
ROPE_THETA = 10000.0
EPS = 1e-6

OFF_FQ = 0
OFF_FK = OFF_FQ + FOX_WIDTH
OFF_FV = OFF_FK + FOX_WIDTH
OFF_FF = OFF_FV + FOX_WIDTH
OFF_CQ = OFF_FF + FOX_HEADS
OFF_CKV = OFF_CQ + Q_LORA_RANK
OFF_KR = OFF_CKV + KV_LORA_RANK
IN_COLS = OFF_KR + MLA_ROPE_DIM

kernel_name = "hymba_fox_mla_hybrid_block"


def rmsnorm(x, g):
    x32 = x.astype(jnp.float32)
    y = x32 * lax.rsqrt(jnp.mean(x32 * x32, axis=-1, keepdims=True) + EPS)
    return (y * g.astype(jnp.float32)).astype(x.dtype)


def rope_cos_sin(positions):
    inv_freq = ROPE_THETA ** (-jnp.arange(0, MLA_ROPE_DIM, 2, dtype=jnp.float32) / MLA_ROPE_DIM)
    ang = positions.astype(jnp.float32)[..., None] * inv_freq
    return jnp.cos(ang), jnp.sin(ang)


def apply_rope(x, cos, sin):
    half = x.shape[-1] // 2
    x1 = x[..., :half].astype(jnp.float32)
    x2 = x[..., half:].astype(jnp.float32)
    return jnp.concatenate([x1 * cos - x2 * sin, x2 * cos + x1 * sin], axis=-1).astype(x.dtype)


def causal_block_attention(q, k, v, scale, log_decay_cum=None):
    b, h, s, dk = q.shape
    nb = s // BLOCK_Q
    q_blocks = jnp.moveaxis(q.reshape(b, h, nb, BLOCK_Q, dk), 2, 0)
    starts = jnp.arange(nb, dtype=jnp.int32) * BLOCK_Q
    key_pos = jnp.arange(s, dtype=jnp.int32)

    def attend(q_blk, start, bias):
        logits = jnp.einsum('bhqd,bhkd->bhqk', q_blk, k).astype(jnp.float32) * scale
        if bias is not None:
            logits = logits + bias
        q_pos = start + jnp.arange(BLOCK_Q, dtype=jnp.int32)
        mask = key_pos[None, :] <= q_pos[:, None]
        logits = jnp.where(mask, logits, -jnp.inf)
        p = jax.nn.softmax(logits, axis=-1).astype(v.dtype)
        return jnp.einsum('bhqk,bhkd->bhqd', p, v)

    if log_decay_cum is None:
        out = lax.map(lambda xs: attend(xs[0], xs[1], None), (q_blocks, starts))
    else:
        F = log_decay_cum
        fq_blocks = jnp.moveaxis(F.reshape(b, h, nb, BLOCK_Q), 2, 0)
        out = lax.map(
            lambda xs: attend(xs[0], xs[1], xs[2][..., :, None] - F[:, :, None, :]),
            (q_blocks, starts, fq_blocks))
    return jnp.moveaxis(out, 0, 2).reshape(b, h, s, -1)


def _fwd_setup_inputs(seed: int = 0) -> dict:
    key = jax.random.key(seed)
    ks = jax.random.split(key, 16)
    f32 = jnp.float32

    def w(k, shape, fan_in):
        return jax.random.normal(k, shape, f32) * (fan_in ** -0.5)

    def gain(k, shape):
        return 1.0 + 0.02 * jax.random.normal(k, shape, f32)

    x = jax.random.normal(ks[0], (BATCH, SEQ, D_MODEL), f32)
    offsets = jax.random.randint(ks[1], (BATCH, 1), 0, 64, dtype=jnp.int32)
    positions = jnp.arange(SEQ, dtype=jnp.int32)[None, :] + offsets
    return {
        "x": x,
        "positions": positions,
        "attn_norm_g": gain(ks[2], (DEPTH, D_MODEL)),
        "w_in": w(ks[3], (DEPTH, D_MODEL, IN_COLS), D_MODEL),
        "b_forget": 3.0 + 0.5 * jax.random.normal(ks[4], (DEPTH, FOX_HEADS), f32),
        "q_norm_g": gain(ks[5], (DEPTH, Q_LORA_RANK)),
        "w_uq": w(ks[6], (DEPTH, Q_LORA_RANK, MLA_HEADS * MLA_QK_DIM), Q_LORA_RANK),
        "kv_norm_g": gain(ks[7], (DEPTH, KV_LORA_RANK)),
        "w_ukv": w(ks[8], (DEPTH, KV_LORA_RANK, MLA_HEADS * (MLA_NOPE_DIM + MLA_V_DIM)), KV_LORA_RANK),
        "fox_out_g": gain(ks[9], (DEPTH, FOX_WIDTH)),
        "mla_out_g": gain(ks[10], (DEPTH, MLA_WIDTH)),
        "w_o": w(ks[11], (DEPTH, MIX_WIDTH, D_MODEL), MIX_WIDTH),
        "mlp_norm_g": gain(ks[12], (DEPTH, D_MODEL)),
        "w_up": w(ks[13], (DEPTH, D_MODEL, D_FF), D_MODEL),
        "w_down": w(ks[14], (DEPTH, D_FF, D_MODEL), D_FF),
        "final_norm_g": gain(ks[15], (D_MODEL,)),
    }


def _fwd_reference(x, positions, attn_norm_g, w_in, b_forget, q_norm_g, w_uq, kv_norm_g, w_ukv,
              fox_out_g, mla_out_g, w_o, mlp_norm_g, w_up, w_down, final_norm_g):
    b, s, _ = x.shape
    cos, sin = rope_cos_sin(positions)
    fox_scale = 1.0 / math.sqrt(HEAD_DIM)
    mla_scale = 1.0 / math.sqrt(MLA_QK_DIM)

    for l in range(DEPTH):
        h = rmsnorm(x, attn_norm_g[l])
        proj = jnp.einsum('bsd,dc->bsc', h, w_in[l])

        fq = proj[..., OFF_FQ:OFF_FK].reshape(b, s, FOX_HEADS, HEAD_DIM).transpose(0, 2, 1, 3)
        fk = proj[..., OFF_FK:OFF_FV].reshape(b, s, FOX_HEADS, HEAD_DIM).transpose(0, 2, 1, 3)
        fv = proj[..., OFF_FV:OFF_FF].reshape(b, s, FOX_HEADS, HEAD_DIM).transpose(0, 2, 1, 3)
        f_logit = proj[..., OFF_FF:OFF_CQ].astype(jnp.float32) + b_forget[l].astype(jnp.float32)
        log_f = jax.nn.log_sigmoid(f_logit)
        F = jnp.cumsum(log_f, axis=1).transpose(0, 2, 1)
        fox = causal_block_attention(fq, fk, fv, fox_scale, F)
        fox = fox.transpose(0, 2, 1, 3).reshape(b, s, FOX_WIDTH)

        c_q = rmsnorm(proj[..., OFF_CQ:OFF_CKV], q_norm_g[l])
        c_kv = rmsnorm(proj[..., OFF_CKV:OFF_KR], kv_norm_g[l])
        k_rope = apply_rope(proj[..., OFF_KR:IN_COLS], cos, sin)
        q = jnp.einsum('bsr,rc->bsc', c_q, w_uq[l]).reshape(b, s, MLA_HEADS, MLA_QK_DIM)
        q_nope = q[..., :MLA_NOPE_DIM]
        q_rope = apply_rope(q[..., MLA_NOPE_DIM:], cos[:, :, None, :], sin[:, :, None, :])
        kv = jnp.einsum('bsr,rc->bsc', c_kv, w_ukv[l]).reshape(b, s, MLA_HEADS, MLA_NOPE_DIM + MLA_V_DIM)
        k_nope = kv[..., :MLA_NOPE_DIM]
        mv = kv[..., MLA_NOPE_DIM:]
        mq = jnp.concatenate([q_nope, q_rope], axis=-1).transpose(0, 2, 1, 3)
        mk = jnp.concatenate(
            [k_nope, jnp.broadcast_to(k_rope[:, :, None, :], (b, s, MLA_HEADS, MLA_ROPE_DIM))],
            axis=-1).transpose(0, 2, 1, 3)
        mv = mv.transpose(0, 2, 1, 3)
        mla = causal_block_attention(mq, mk, mv, mla_scale)
        mla = mla.transpose(0, 2, 1, 3).reshape(b, s, MLA_WIDTH)

        mixed = jnp.concatenate([rmsnorm(fox, fox_out_g[l]), rmsnorm(mla, mla_out_g[l])], axis=-1)
        x = x + jnp.einsum('bsc,cd->bsd', mixed, w_o[l])

        h = rmsnorm(x, mlp_norm_g[l])
        u = jnp.einsum('bsd,df->bsf', h, w_up[l])
        x = x + jnp.einsum('bsf,fd->bsd', jnp.square(jax.nn.relu(u)), w_down[l])

    return rmsnorm(x, final_norm_g)


import jax as _jax
import jax.numpy as _jnp

TWIN_FORMAT = 'train_step'
FWD_PARAMS = ['x', 'positions', 'attn_norm_g', 'w_in', 'b_forget', 'q_norm_g', 'w_uq', 'kv_norm_g', 'w_ukv', 'fox_out_g', 'mla_out_g', 'w_o', 'mlp_norm_g', 'w_up', 'w_down', 'final_norm_g']
TWIN_WEIGHTS = ['attn_norm_g', 'w_in', 'b_forget', 'q_norm_g', 'w_uq', 'kv_norm_g', 'w_ukv', 'fox_out_g', 'mla_out_g', 'w_o', 'mlp_norm_g', 'w_up', 'w_down', 'final_norm_g']
TWIN_DIFF_INPUT = 'x'
TWIN_INPUTS = ['x', 'positions', 'attn_norm_g', 'w_in', 'b_forget', 'q_norm_g', 'w_uq', 'kv_norm_g', 'w_ukv', 'fox_out_g', 'mla_out_g', 'w_o', 'mlp_norm_g', 'w_up', 'w_down', 'final_norm_g', 'loss_target', 'm_attn_norm_g', 'm_w_in', 'm_b_forget', 'm_q_norm_g', 'm_w_uq', 'm_kv_norm_g', 'm_w_ukv', 'm_fox_out_g', 'm_mla_out_g', 'm_w_o', 'm_mlp_norm_g', 'm_w_up', 'm_w_down', 'm_final_norm_g', 'v_attn_norm_g', 'v_w_in', 'v_b_forget', 'v_q_norm_g', 'v_w_uq', 'v_kv_norm_g', 'v_w_ukv', 'v_fox_out_g', 'v_mla_out_g', 'v_w_o', 'v_mlp_norm_g', 'v_w_up', 'v_w_down', 'v_final_norm_g']
TWIN_OUTPUTS = ['loss', 'grad_x', 'grad_attn_norm_g', 'grad_w_in', 'grad_b_forget', 'grad_q_norm_g', 'grad_w_uq', 'grad_kv_norm_g', 'grad_w_ukv', 'grad_fox_out_g', 'grad_mla_out_g', 'grad_w_o', 'grad_mlp_norm_g', 'grad_w_up', 'grad_w_down', 'grad_final_norm_g', 'delta_attn_norm_g', 'delta_w_in', 'delta_b_forget', 'delta_q_norm_g', 'delta_w_uq', 'delta_kv_norm_g', 'delta_w_ukv', 'delta_fox_out_g', 'delta_mla_out_g', 'delta_w_o', 'delta_mlp_norm_g', 'delta_w_up', 'delta_w_down', 'delta_final_norm_g', 'new_m_attn_norm_g', 'new_m_w_in', 'new_m_b_forget', 'new_m_q_norm_g', 'new_m_w_uq', 'new_m_kv_norm_g', 'new_m_w_ukv', 'new_m_fox_out_g', 'new_m_mla_out_g', 'new_m_w_o', 'new_m_mlp_norm_g', 'new_m_w_up', 'new_m_w_down', 'new_m_final_norm_g', 'new_v_attn_norm_g', 'new_v_w_in', 'new_v_b_forget', 'new_v_q_norm_g', 'new_v_w_uq', 'new_v_kv_norm_g', 'new_v_w_ukv', 'new_v_fox_out_g', 'new_v_mla_out_g', 'new_v_w_o', 'new_v_mlp_norm_g', 'new_v_w_up', 'new_v_w_down', 'new_v_final_norm_g']
TWIN_LEAF_KINDS = {'loss': 'loss', 'grad_x': 'grad_x', 'grad_attn_norm_g': 'grad_w', 'grad_w_in': 'grad_w', 'grad_b_forget': 'grad_w', 'grad_q_norm_g': 'grad_w', 'grad_w_uq': 'grad_w', 'grad_kv_norm_g': 'grad_w', 'grad_w_ukv': 'grad_w', 'grad_fox_out_g': 'grad_w', 'grad_mla_out_g': 'grad_w', 'grad_w_o': 'grad_w', 'grad_mlp_norm_g': 'grad_w', 'grad_w_up': 'grad_w', 'grad_w_down': 'grad_w', 'grad_final_norm_g': 'grad_w', 'delta_attn_norm_g': 'delta_w', 'delta_w_in': 'delta_w', 'delta_b_forget': 'delta_w', 'delta_q_norm_g': 'delta_w', 'delta_w_uq': 'delta_w', 'delta_kv_norm_g': 'delta_w', 'delta_w_ukv': 'delta_w', 'delta_fox_out_g': 'delta_w', 'delta_mla_out_g': 'delta_w', 'delta_w_o': 'delta_w', 'delta_mlp_norm_g': 'delta_w', 'delta_w_up': 'delta_w', 'delta_w_down': 'delta_w', 'delta_final_norm_g': 'delta_w', 'new_m_attn_norm_g': 'new_m', 'new_m_w_in': 'new_m', 'new_m_b_forget': 'new_m', 'new_m_q_norm_g': 'new_m', 'new_m_w_uq': 'new_m', 'new_m_kv_norm_g': 'new_m', 'new_m_w_ukv': 'new_m', 'new_m_fox_out_g': 'new_m', 'new_m_mla_out_g': 'new_m', 'new_m_w_o': 'new_m', 'new_m_mlp_norm_g': 'new_m', 'new_m_w_up': 'new_m', 'new_m_w_down': 'new_m', 'new_m_final_norm_g': 'new_m', 'new_v_attn_norm_g': 'new_v', 'new_v_w_in': 'new_v', 'new_v_b_forget': 'new_v', 'new_v_q_norm_g': 'new_v', 'new_v_w_uq': 'new_v', 'new_v_kv_norm_g': 'new_v', 'new_v_w_ukv': 'new_v', 'new_v_fox_out_g': 'new_v', 'new_v_mla_out_g': 'new_v', 'new_v_w_o': 'new_v', 'new_v_mlp_norm_g': 'new_v', 'new_v_w_up': 'new_v', 'new_v_w_down': 'new_v', 'new_v_final_norm_g': 'new_v'}


def _forward(args):
    return _fwd_reference(*[args[k] for k in FWD_PARAMS])


def _output_shape():
    out = _jax.eval_shape(lambda: _forward(_fwd_setup_inputs(0)))
    return out.shape, out.dtype

N_MICROBATCH = 1
ADAM_LR = 0.001
ADAM_B1 = 0.9
ADAM_B2 = 0.999
ADAM_EPS = 1e-08
ADAM_WD = 0.01
ADAM_STEP = 10
PER_EXAMPLE_BATCH_AXIS = {'x': 0, 'positions': 0, 'loss_target': 0}
SHARED_INPUTS = []
_WEIGHT_DTYPES = {'attn_norm_g': _jnp.float32, 'w_in': _jnp.float32, 'b_forget': _jnp.float32, 'q_norm_g': _jnp.float32, 'w_uq': _jnp.float32, 'kv_norm_g': _jnp.float32, 'w_ukv': _jnp.float32, 'fox_out_g': _jnp.float32, 'mla_out_g': _jnp.float32, 'w_o': _jnp.float32, 'mlp_norm_g': _jnp.float32, 'w_up': _jnp.float32, 'w_down': _jnp.float32, 'final_norm_g': _jnp.float32}
MOMENT_SCALE = {'attn_norm_g': 2.329287e-01, 'w_in': 1.429075e-01, 'b_forget': 5.660617e-01, 'q_norm_g': 1.700574e-01, 'w_uq': 1.142870e-01, 'kv_norm_g': 3.855734e-01, 'w_ukv': 1.353720e-01, 'fox_out_g': 1.405042e-01, 'mla_out_g': 1.396310e-01, 'w_o': 1.357460e-01, 'mlp_norm_g': 1.342373e-01, 'w_up': 6.724687e-02, 'w_down': 1.273264e-01, 'final_norm_g': 3.219512e+01}


def _to_microbatches(a, axis):
    t = _jnp.moveaxis(a, axis, 0)
    t = t.reshape((N_MICROBATCH, t.shape[0] // N_MICROBATCH) + t.shape[1:])
    return _jnp.moveaxis(t, 1, axis + 1)


def setup_inputs(seed: int = 0) -> dict:
    inp = _fwd_setup_inputs(seed)
    key = _jax.random.fold_in(_jax.random.key(seed), 7919)
    shape, _ = _output_shape()
    out = dict(inp)
    out["loss_target"] = _jax.random.normal(_jax.random.fold_in(key, 0), shape, _jnp.float32)
    for i, name in enumerate(TWIN_WEIGHTS):
        w = inp[name].astype(_jnp.float32)
        if MOMENT_SCALE is None:
            s = _jnp.sqrt(_jnp.mean(_jnp.square(w)) + 1e-30)
        else:
            s = MOMENT_SCALE[name]
        km, kv = _jax.random.split(_jax.random.fold_in(key, i + 1))
        out[name] = w
        out["m_" + name] = s * _jax.random.normal(km, w.shape, _jnp.float32)
        out["v_" + name] = (s * s) * _jax.random.uniform(kv, w.shape, _jnp.float32, 0.5, 1.5)
    if N_MICROBATCH > 1:
        for name, axis in PER_EXAMPLE_BATCH_AXIS.items():
            out[name] = _to_microbatches(out[name], axis)
    return {'x': out['x'], 'positions': out['positions'], 'attn_norm_g': out['attn_norm_g'], 'w_in': out['w_in'], 'b_forget': out['b_forget'], 'q_norm_g': out['q_norm_g'], 'w_uq': out['w_uq'], 'kv_norm_g': out['kv_norm_g'], 'w_ukv': out['w_ukv'], 'fox_out_g': out['fox_out_g'], 'mla_out_g': out['mla_out_g'], 'w_o': out['w_o'], 'mlp_norm_g': out['mlp_norm_g'], 'w_up': out['w_up'], 'w_down': out['w_down'], 'final_norm_g': out['final_norm_g'], 'loss_target': out['loss_target'], 'm_attn_norm_g': out['m_attn_norm_g'], 'm_w_in': out['m_w_in'], 'm_b_forget': out['m_b_forget'], 'm_q_norm_g': out['m_q_norm_g'], 'm_w_uq': out['m_w_uq'], 'm_kv_norm_g': out['m_kv_norm_g'], 'm_w_ukv': out['m_w_ukv'], 'm_fox_out_g': out['m_fox_out_g'], 'm_mla_out_g': out['m_mla_out_g'], 'm_w_o': out['m_w_o'], 'm_mlp_norm_g': out['m_mlp_norm_g'], 'm_w_up': out['m_w_up'], 'm_w_down': out['m_w_down'], 'm_final_norm_g': out['m_final_norm_g'], 'v_attn_norm_g': out['v_attn_norm_g'], 'v_w_in': out['v_w_in'], 'v_b_forget': out['v_b_forget'], 'v_q_norm_g': out['v_q_norm_g'], 'v_w_uq': out['v_w_uq'], 'v_kv_norm_g': out['v_kv_norm_g'], 'v_w_ukv': out['v_w_ukv'], 'v_fox_out_g': out['v_fox_out_g'], 'v_mla_out_g': out['v_mla_out_g'], 'v_w_o': out['v_w_o'], 'v_mlp_norm_g': out['v_mlp_norm_g'], 'v_w_up': out['v_w_up'], 'v_w_down': out['v_w_down'], 'v_final_norm_g': out['v_final_norm_g']}


def _loss(weights, diff, rest, loss_target):
    with _jax.named_scope("forward"):
        args = {**rest, TWIN_DIFF_INPUT: diff, **{k: w.astype(_WEIGHT_DTYPES[k]) for k, w in weights.items()}}
        y = _forward(args)
    with _jax.named_scope("loss_head"):
        err = _jnp.square(y.astype(_jnp.float32) - loss_target)
        return 0.5 * _jnp.sum(_jnp.mean(err, axis=-1)) if err.ndim else 0.5 * err


def _adamw(w, g, m, v):
    m = ADAM_B1 * m + (1.0 - ADAM_B1) * g
    v = ADAM_B2 * v + (1.0 - ADAM_B2) * _jnp.square(g)
    m_hat = m / (1.0 - ADAM_B1 ** ADAM_STEP)
    v_hat = v / (1.0 - ADAM_B2 ** ADAM_STEP)
    delta = -ADAM_LR * (m_hat / (_jnp.sqrt(v_hat) + ADAM_EPS) + ADAM_WD * w)
    return delta, m, v


def reference(x, positions, attn_norm_g, w_in, b_forget, q_norm_g, w_uq, kv_norm_g, w_ukv, fox_out_g, mla_out_g, w_o, mlp_norm_g, w_up, w_down, final_norm_g, loss_target, m_attn_norm_g, m_w_in, m_b_forget, m_q_norm_g, m_w_uq, m_kv_norm_g, m_w_ukv, m_fox_out_g, m_mla_out_g, m_w_o, m_mlp_norm_g, m_w_up, m_w_down, m_final_norm_g, v_attn_norm_g, v_w_in, v_b_forget, v_q_norm_g, v_w_uq, v_kv_norm_g, v_w_ukv, v_fox_out_g, v_mla_out_g, v_w_o, v_mlp_norm_g, v_w_up, v_w_down, v_final_norm_g):
    given = dict(x=x, positions=positions, attn_norm_g=attn_norm_g, w_in=w_in, b_forget=b_forget, q_norm_g=q_norm_g, w_uq=w_uq, kv_norm_g=kv_norm_g, w_ukv=w_ukv, fox_out_g=fox_out_g, mla_out_g=mla_out_g, w_o=w_o, mlp_norm_g=mlp_norm_g, w_up=w_up, w_down=w_down, final_norm_g=final_norm_g, loss_target=loss_target, m_attn_norm_g=m_attn_norm_g, m_w_in=m_w_in, m_b_forget=m_b_forget, m_q_norm_g=m_q_norm_g, m_w_uq=m_w_uq, m_kv_norm_g=m_kv_norm_g, m_w_ukv=m_w_ukv, m_fox_out_g=m_fox_out_g, m_mla_out_g=m_mla_out_g, m_w_o=m_w_o, m_mlp_norm_g=m_mlp_norm_g, m_w_up=m_w_up, m_w_down=m_w_down, m_final_norm_g=m_final_norm_g, v_attn_norm_g=v_attn_norm_g, v_w_in=v_w_in, v_b_forget=v_b_forget, v_q_norm_g=v_q_norm_g, v_w_uq=v_w_uq, v_kv_norm_g=v_kv_norm_g, v_w_ukv=v_w_ukv, v_fox_out_g=v_fox_out_g, v_mla_out_g=v_mla_out_g, v_w_o=v_w_o, v_mlp_norm_g=v_mlp_norm_g, v_w_up=v_w_up, v_w_down=v_w_down, v_final_norm_g=v_final_norm_g)
    weights = {n: given[n] for n in TWIN_WEIGHTS}
    shared = {n: given[n] for n in SHARED_INPUTS}
    per_example = {n: given[n] for n in ['x', 'positions']}
    grad_fn = _jax.value_and_grad(_loss, argnums=(0, 1))

    def one_microbatch(ex, loss_target):
        ex = dict(ex)
        diff = ex.pop(TWIN_DIFF_INPUT)
        return grad_fn(weights, diff, {**shared, **ex}, loss_target)

    if N_MICROBATCH == 1:
        loss, (grad_w, grad_x) = one_microbatch(per_example, given["loss_target"])
    else:
        def body(carry, xs):
            loss_sum, grad_sum = carry
            l_k, (gw_k, gx_k) = one_microbatch(xs[0], xs[1])
            with _jax.named_scope("update"):
                return (loss_sum + l_k, _jax.tree.map(_jnp.add, grad_sum, gw_k)), gx_k

        init = (_jnp.zeros((), _jnp.float32), _jax.tree.map(_jnp.zeros_like, weights))
        (loss, grad_w), grad_x = _jax.lax.scan(body, init, (per_example, given["loss_target"]))
    with _jax.named_scope("update"):
        delta_w, new_m, new_v = {}, {}, {}
        for n in TWIN_WEIGHTS:
            delta_w[n], new_m[n], new_v[n] = _adamw(weights[n], grad_w[n], given["m_" + n], given["v_" + n])
    return (loss, grad_x, *[grad_w[n] for n in TWIN_WEIGHTS], *[delta_w[n] for n in TWIN_WEIGHTS],
            *[new_m[n] for n in TWIN_WEIGHTS], *[new_v[n] for n in TWIN_WEIGHTS])
```

```python
import functools
import math

import numpy as np
import jax
import jax.numpy as jnp
from jax import lax
from jax.experimental import pallas as pl
from jax.experimental.pallas import tpu as pltpu

F32 = jnp.float32
BF16 = jnp.bfloat16
MESH = pl.DeviceIdType.MESH

D_MODEL = 1024
HEADS = 8
HEAD_DIM = 64
FOX_WIDTH = 512
MLA_WIDTH = 512
NOPE = 64
ROPE = 32
QK_DIM = 96
Q_RANK = 384
KV_RANK = 256
D_FF = 4096
IN_COLS = 2216
ROPE_THETA = 10000.0
EPS = 1e-6
FOX_SCALE = 1.0 / math.sqrt(HEAD_DIM)
MLA_SCALE = 1.0 / math.sqrt(QK_DIM)
ADAM_LR = 0.001
ADAM_B1 = 0.9
ADAM_B2 = 0.999
ADAM_EPS = 1e-08
ADAM_WD = 0.01
ADAM_STEP = 10

N_DEV = 8
LANES = 128
REST_COLS = 896
VMEM_LIMIT = 56 * 1024 * 1024

SHARD_SIZES = (D_MODEL * 277, Q_RANK * QK_DIM, KV_RANK * 128, 128 * D_MODEL, D_MODEL * 512, 512 * D_MODEL)
FLAT = sum(SHARD_SIZES)
FLAT_ROWS = FLAT // LANES
FLAT_TILE = FLAT_ROWS // 3
SMALL_NAMES = ("attn_norm_g", "b_forget", "q_norm_g", "kv_norm_g", "fox_out_g", "mla_out_g", "mlp_norm_g", "final_norm_g")
SMALL_SIZES = (1024, 8, 384, 256, 512, 512, 1024, 1024)
SMALL_ROWS = 40
LOSS_SLOT = sum(SMALL_SIZES)


def _cparams(*sem):
    return pltpu.CompilerParams(dimension_semantics=sem or None, vmem_limit_bytes=VMEM_LIMIT)


def _row_tile(t):
    return 512 if t >= 2048 else 128


def _dot(a, b):
    return jnp.dot(a, b, preferred_element_type=F32)


def _dot_nt(a, b):
    return lax.dot_general(a, b, (((1,), (1,)), ((), ())), preferred_element_type=F32)


def _dot_tn(a, b):
    return lax.dot_general(a, b, (((0,), (0,)), ((), ())), preferred_element_type=F32)


def _rms(x, g):
    r = lax.rsqrt(jnp.mean(x * x, axis=-1, keepdims=True) + EPS)
    return x * r * g, r


def _rms_bwd(x, g, r, dy):
    xh = x * r
    gdy = dy * g
    dx = r * (gdy - xh * jnp.mean(gdy * xh, axis=-1, keepdims=True))
    return dx, jnp.sum(dy * xh, axis=0, keepdims=True)


def _lane():
    return lax.broadcasted_iota(jnp.int32, (1, LANES), 1)


def _rot(x):
    lane = _lane()
    half = NOPE + ROPE // 2
    first = jnp.logical_and(lane >= NOPE, lane < half)
    second = jnp.logical_and(lane >= half, lane < NOPE + ROPE)
    return jnp.where(first, -pltpu.roll(x, LANES - ROPE // 2, 1), jnp.where(second, pltpu.roll(x, ROPE // 2, 1), 0.0))


def _rope(x, cos, sin):
    return x * cos + _rot(x) * sin


def _rope_bwd(dy, cos, sin):
    return dy * cos - _rot(dy * sin)


def _my_index():
    return 4 * lax.axis_index("x") + 2 * lax.axis_index("y") + lax.axis_index("c")


def _all_gather(block):
    rows, cols = block.shape

    def body(x_ref, out_ref, send_sems, recv_sems, local_sem):
        x, y, c = lax.axis_index("x"), lax.axis_index("y"), lax.axis_index("c")
        me, sibling = (x, y, c), (x, y, 1 - c)
        chips = [(1 - x, y), (x, 1 - y), (1 - x, 1 - y)]

        def slot(px, py, pc):
            return out_ref.at[4 * px + 2 * py + pc]

        def copy(k, blk, to, src=None):
            return pltpu.make_async_remote_copy(
                src_ref=slot(*blk) if src is None else src, dst_ref=slot(*blk),
                send_sem=send_sems.at[k], recv_sem=recv_sems.at[k], device_id=to, device_id_type=MESH)

        mine = pltpu.make_async_copy(x_ref, slot(*me), local_sem)
        mine.start()
        first = [copy(0, me, sibling, src=x_ref)]
        first += [copy(1 + j, me, (*chip, c), src=x_ref) for j, chip in enumerate(chips)]
        for cp in first:
            cp.start()
        passed = [copy(4 + j, (*chip, c), sibling) for j, chip in enumerate(chips)]
        for j, chip in enumerate(chips):
            copy(1 + j, (*chip, c), me).wait_recv()
            passed[j].start()
        copy(0, sibling, me).wait_recv()
        for j, chip in enumerate(chips):
            copy(4 + j, (*chip, 1 - c), me).wait_recv()
        for cp in first + passed:
            cp.wait_send()
        mine.wait()

    return pl.pallas_call(
        body, name="all_gather_weights",
        out_shape=jax.ShapeDtypeStruct((N_DEV, rows, cols), block.dtype),
        in_specs=[pl.BlockSpec(memory_space=pl.ANY)],
        out_specs=pl.BlockSpec(memory_space=pl.ANY),
        scratch_shapes=[pltpu.SemaphoreType.DMA((7,)), pltpu.SemaphoreType.DMA((7,)), pltpu.SemaphoreType.DMA],
    )(block)


def _sibling_exchange(send):
    def body(s_ref, land_ref, send_sem, recv_sem):
        x, y, c = lax.axis_index("x"), lax.axis_index("y"), lax.axis_index("c")
        cp = pltpu.make_async_remote_copy(src_ref=s_ref, dst_ref=land_ref, send_sem=send_sem, recv_sem=recv_sem,
                                          device_id=(x, y, 1 - c), device_id_type=MESH)
        cp.start()
        cp.wait()

    return pl.pallas_call(
        body, name="rs_sibling_exchange",
        out_shape=jax.ShapeDtypeStruct(send.shape, send.dtype),
        in_specs=[pl.BlockSpec(memory_space=pl.ANY)],
        out_specs=pl.BlockSpec(memory_space=pl.ANY),
        scratch_shapes=[pltpu.SemaphoreType.DMA, pltpu.SemaphoreType.DMA],
    )(send)


def _chip_exchange(part):
    _, rows, cols = part.shape

    def body(p_ref, land_ref, send_sems, recv_sems):
        x, y, c = lax.axis_index("x"), lax.axis_index("y"), lax.axis_index("c")
        chips = [(1 - x, y), (x, 1 - y), (1 - x, 1 - y)]
        cps = [pltpu.make_async_remote_copy(
            src_ref=p_ref.at[2 * cx + cy], dst_ref=land_ref.at[k], send_sem=send_sems.at[k], recv_sem=recv_sems.at[k],
            device_id=(cx, cy, c), device_id_type=MESH) for k, (cx, cy) in enumerate(chips)]
        for cp in cps:
            cp.start()
        for cp in cps:
            cp.wait()

    return pl.pallas_call(
        body, name="rs_chip_exchange",
        out_shape=jax.ShapeDtypeStruct((3, rows, cols), part.dtype),
        in_specs=[pl.BlockSpec(memory_space=pl.ANY)],
        out_specs=pl.BlockSpec(memory_space=pl.ANY),
        scratch_shapes=[pltpu.SemaphoreType.DMA((3,)), pltpu.SemaphoreType.DMA((3,))],
    )(part)


def _small_all_reduce(vec):
    def body(v_ref, out_ref, land_ref, send_sems, recv_sems):
        x, y, c = lax.axis_index("x"), lax.axis_index("y"), lax.axis_index("c")
        me = 4 * x + 2 * y + c
        land_ref[me] = v_ref[...]
        cps = []
        for k in range(1, N_DEV):
            px, py, pc = x ^ (k >> 2), y ^ ((k >> 1) & 1), c ^ (k & 1)
            cps.append(pltpu.make_async_remote_copy(
                src_ref=v_ref, dst_ref=land_ref.at[me], send_sem=send_sems.at[k - 1], recv_sem=recv_sems.at[k - 1],
                device_id=(px, py, pc), device_id_type=MESH))
        for cp in cps:
            cp.start()
        for cp in cps:
            cp.wait()
        acc = land_ref[0]
        for d in range(1, N_DEV):
            acc = acc + land_ref[d]
        out_ref[...] = acc

    return pl.pallas_call(
        body, name="small_all_reduce",
        out_shape=jax.ShapeDtypeStruct(vec.shape, F32),
        in_specs=[pl.BlockSpec(memory_space=pltpu.VMEM)],
        out_specs=pl.BlockSpec(memory_space=pltpu.VMEM),
        scratch_shapes=[pltpu.VMEM((N_DEV,) + vec.shape, F32),
                        pltpu.SemaphoreType.DMA((N_DEV - 1,)), pltpu.SemaphoreType.DMA((N_DEV - 1,))],
    )(vec)


def _flat_spec(tile, lead=()):
    n = len(lead)
    return pl.BlockSpec(lead + (tile, LANES), lambda i: (0,) * n + (i, 0))


def _sibling_sum(keep, got):
    _, rows, _ = keep.shape
    tile = FLAT_TILE if rows == FLAT_ROWS else rows

    def body(k_ref, g_ref, f_ref, b_ref):
        s = k_ref[...] + g_ref[...].astype(F32)
        f_ref[...] = s
        b_ref[...] = s.astype(BF16)

    return pl.pallas_call(
        body, name="rs_sibling_sum", grid=(rows // tile,),
        out_shape=(jax.ShapeDtypeStruct(keep.shape, F32), jax.ShapeDtypeStruct(keep.shape, BF16)),
        in_specs=[_flat_spec(tile, (4,)), _flat_spec(tile, (4,))],
        out_specs=(_flat_spec(tile, (4,)), _flat_spec(tile, (4,))),
        compiler_params=_cparams("parallel"),
    )(keep, got)


def _adamw_math(w, g, m, v):
    m2 = ADAM_B1 * m + (1.0 - ADAM_B1) * g
    v2 = ADAM_B2 * v + (1.0 - ADAM_B2) * (g * g)
    m_hat = m2 / (1.0 - ADAM_B1 ** ADAM_STEP)
    v_hat = v2 / (1.0 - ADAM_B2 ** ADAM_STEP)
    delta = -ADAM_LR * (m_hat / (jnp.sqrt(v_hat) + ADAM_EPS) + ADAM_WD * w)
    return delta, m2, v2


def _adamw_sharded(own, got, w, m, v):
    rows = own.shape[0]
    tile = FLAT_TILE if rows == FLAT_ROWS else rows

    def body(o_ref, r_ref, w_ref, m_ref, v_ref, g_out, d_out, m_out, v_out):
        g = o_ref[...]
        for k in range(3):
            g = g + r_ref[k].astype(F32)
        d, m2, v2 = _adamw_math(w_ref[...], g, m_ref[...], v_ref[...])
        g_out[...] = g
        d_out[...] = d
        m_out[...] = m2
        v_out[...] = v2

    shp = jax.ShapeDtypeStruct(own.shape, F32)
    return pl.pallas_call(
        body, name="adamw_sharded", grid=(rows // tile,),
        out_shape=(shp, shp, shp, shp),
        in_specs=[_flat_spec(tile), _flat_spec(tile, (3,)), _flat_spec(tile), _flat_spec(tile), _flat_spec(tile)],
        out_specs=(_flat_spec(tile),) * 4,
        compiler_params=_cparams("parallel"),
    )(own, got, w, m, v)


def _adamw_small(g, w, m, v):
    def body(g_ref, w_ref, m_ref, v_ref, d_out, m_out, v_out):
        d, m2, v2 = _adamw_math(w_ref[...], g_ref[...], m_ref[...], v_ref[...])
        d_out[...] = d
        m_out[...] = m2
        v_out[...] = v2

    shp = jax.ShapeDtypeStruct(g.shape, F32)
    return pl.pallas_call(body, name="adamw_small", out_shape=(shp, shp, shp))(g, w, m, v)


def _rope_tables(pos_col):
    t = pos_col.shape[0]
    inv = (np.float32(ROPE_THETA) ** (-np.arange(0, ROPE, 2, dtype=np.float32) / np.float32(ROPE))).astype(np.float32)
    freq = np.zeros((1, LANES), np.float32)
    freq[0, NOPE:NOPE + ROPE // 2] = inv
    freq[0, NOPE + ROPE // 2:NOPE + ROPE] = inv
    tm = _row_tile(t)

    def body(p_ref, f_ref, c_ref, s_ref):
        ang = p_ref[...].astype(F32) * f_ref[...]
        c_ref[...] = jnp.cos(ang)
        s_ref[...] = jnp.sin(ang)

    shp = jax.ShapeDtypeStruct((t, LANES), F32)
    return pl.pallas_call(
        body, name="rope_tables", grid=(t // tm,), out_shape=(shp, shp),
        in_specs=[pl.BlockSpec((tm, 1), lambda i: (i, 0)), pl.BlockSpec((1, LANES), lambda i: (0, 0))],
        out_specs=(pl.BlockSpec((tm, LANES), lambda i: (i, 0)),) * 2,
        compiler_params=_cparams("parallel"),
    )(pos_col, jnp.asarray(freq))


def _in_proj(x, g, w_qkv, w_rest):
    t = x.shape[0]
    tm = _row_tile(t)

    def body(x_ref, g_ref, wq_ref, wr_ref, h_ref, fq_ref, fk_ref, fv_ref, r_ref):
        h, _ = _rms(x_ref[...], g_ref[...])
        hb = h.astype(BF16)
        h_ref[...] = hb
        for n, ref in enumerate((fq_ref, fk_ref, fv_ref)):
            ref[...] = _dot(hb, wq_ref[:, n * FOX_WIDTH:(n + 1) * FOX_WIDTH]).astype(BF16)
        r_ref[...] = _dot(hb, wr_ref[...])

    row = lambda n: pl.BlockSpec((tm, n), lambda i: (i, 0))
    full = lambda a: pl.BlockSpec(a.shape, lambda i: (0,) * a.ndim)
    return pl.pallas_call(
        body, name="in_proj", grid=(t // tm,),
        out_shape=(jax.ShapeDtypeStruct((t, D_MODEL), BF16),) + (jax.ShapeDtypeStruct((t, FOX_WIDTH), BF16),) * 3
        + (jax.ShapeDtypeStruct((t, REST_COLS), F32),),
        in_specs=[row(D_MODEL), full(g), full(w_qkv), full(w_rest)],
        out_specs=(row(D_MODEL), row(FOX_WIDTH), row(FOX_WIDTH), row(FOX_WIDTH), row(REST_COLS)),
        compiler_params=_cparams("parallel"),
    )(x, g, w_qkv, w_rest)


def _log_sigmoid(z):
    return jnp.minimum(z, 0.0) - jnp.log(1.0 + jnp.exp(-jnp.abs(z)))


def _split3(v):
    hi = v.astype(BF16)
    r1 = v - hi.astype(F32)
    mid = r1.astype(BF16)
    lo = (r1 - mid.astype(F32)).astype(BF16)
    return hi, mid, lo


def _scan_tile(t):
    return 256 if t >= 256 else t


def _forget_cumsum(rest, b128):
    t = rest.shape[0]
    tb = _scan_tile(t)

    def body(r_ref, b_ref, f_ref, carry):
        @pl.when(pl.program_id(0) == 0)
        def _():
            carry[...] = jnp.zeros_like(carry)
        lf = _log_sigmoid(r_ref[...] + b_ref[...])
        tri = (lax.broadcasted_iota(jnp.int32, (tb, tb), 0) >= lax.broadcasted_iota(jnp.int32, (tb, tb), 1)).astype(BF16)
        hi, mid, lo = _split3(lf)
        f_ref[...] = (_dot(tri, hi) + _dot(tri, mid)) + _dot(tri, lo) + carry[...]
        carry[...] = f_ref[tb - 1:tb, :]

    return pl.pallas_call(
        body, name="forget_cumsum", grid=(t // tb,),
        out_shape=jax.ShapeDtypeStruct((t, LANES), F32),
        in_specs=[pl.BlockSpec((tb, LANES), lambda i: (i, 0)), pl.BlockSpec((1, LANES), lambda i: (0, 0))],
        out_specs=pl.BlockSpec((tb, LANES), lambda i: (i, 0)),
        scratch_shapes=[pltpu.VMEM((1, LANES), F32)],
        compiler_params=_cparams("arbitrary"),
    )(rest, b128)


def _forget_bwd(rest, b128, d_fq, d_fk):
    t = rest.shape[0]
    tb = _scan_tile(t)
    nb = t // tb

    def body(r_ref, b_ref, dfq_ref, dfk_ref, dz_ref, db_ref, carry):
        @pl.when(pl.program_id(0) == 0)
        def _():
            carry[...] = jnp.zeros_like(carry)
            db_ref[...] = jnp.zeros_like(db_ref)
        tri = (lax.broadcasted_iota(jnp.int32, (tb, tb), 0) <= lax.broadcasted_iota(jnp.int32, (tb, tb), 1)).astype(BF16)
        df = dfq_ref[...] + dfk_ref[...]
        hi, mid, lo = _split3(df)
        dlf = (_dot(tri, hi) + _dot(tri, mid)) + _dot(tri, lo) + carry[...]
        z = r_ref[...] + b_ref[...]
        dz = dlf / (1.0 + jnp.exp(z))
        dz_ref[...] = dz
        db_ref[...] += jnp.sum(dz, axis=0, keepdims=True)
        carry[...] = carry[...] + jnp.sum(df, axis=0, keepdims=True)

    rev = lambda i: (nb - 1 - i, 0)
    return pl.pallas_call(
        body, name="forget_bwd", grid=(nb,),
        out_shape=(jax.ShapeDtypeStruct((t, LANES), F32), jax.ShapeDtypeStruct((1, LANES), F32)),
        in_specs=[pl.BlockSpec((tb, LANES), rev), pl.BlockSpec((1, LANES), lambda i: (0, 0)), pl.BlockSpec((tb, LANES), rev),
                  pl.BlockSpec((tb, LANES), rev)],
        out_specs=(pl.BlockSpec((tb, LANES), rev), pl.BlockSpec((1, LANES), lambda i: (0, 0))),
        scratch_shapes=[pltpu.VMEM((1, LANES), F32)],
        compiler_params=_cparams("arbitrary"),
    )(rest, b128, d_fq, d_fk)


def _mla_prep(rest, gq, gkv, wq, wkv, cos, sin):
    t = rest.shape[0]
    tm = _row_tile(t)

    def body(r_ref, gq_ref, gkv_ref, wq_ref, wkv_ref, c_ref, s_ref, q_ref, k_ref, kv_ref, cq_ref, ckv_ref):
        cos_, sin_ = c_ref[...], s_ref[...]
        cq, _ = _rms(r_ref[:, 128:512], gq_ref[...])
        ckv, _ = _rms(r_ref[:, 512:768], gkv_ref[...])
        cqb, ckvb = cq.astype(BF16), ckv.astype(BF16)
        cq_ref[...] = cqb
        ckv_ref[...] = ckvb
        k_rope = _rope(r_ref[:, 768:896], cos_, sin_)
        lo = _lane() < NOPE
        for h in range(HEADS):
            q_ref[h] = _rope(_dot(cqb, wq_ref[h]), cos_, sin_).astype(BF16)
            kv = _dot(ckvb, wkv_ref[h])
            kv_ref[h] = kv.astype(BF16)
            k_ref[h] = (jnp.where(lo, kv, 0.0) + k_rope).astype(BF16)

    row = lambda n: pl.BlockSpec((tm, n), lambda i: (i, 0))
    full = lambda a: pl.BlockSpec(a.shape, lambda i: (0,) * a.ndim)
    heads = pl.BlockSpec((HEADS, tm, LANES), lambda i: (0, i, 0))
    hshape = jax.ShapeDtypeStruct((HEADS, t, LANES), BF16)
    return pl.pallas_call(
        body, name="mla_prep", grid=(t // tm,),
        out_shape=(hshape, hshape, hshape, jax.ShapeDtypeStruct((t, Q_RANK), BF16), jax.ShapeDtypeStruct((t, KV_RANK), BF16)),
        in_specs=[row(REST_COLS), full(gq), full(gkv), full(wq), full(wkv), row(LANES), row(LANES)],
        out_specs=(heads, heads, heads, row(Q_RANK), row(KV_RANK)),
        compiler_params=_cparams("parallel"),
    )(rest, gq, gkv, wq, wkv, cos, sin)


def _pair_specs(fox, t, tq, blocked_q):
    if fox:
        blk = pl.BlockSpec((tq, LANES), lambda p, i: (i, p))
        whole = pl.BlockSpec((t, LANES), lambda p, i: (0, p))
    else:
        blk = pl.BlockSpec((2, tq, LANES), lambda p, i: (p, i, 0))
        whole = pl.BlockSpec((2, t, LANES), lambda p, i: (p, 0, 0))
    return [blk, whole, whole] if blocked_q else [whole, blk, blk]


def _attn_fwd(fox, q, k, v, f_col=None, f_row=None):
    t = q.shape[0] if fox else q.shape[1]
    tq = _row_tile(t)
    nq = t // tq
    scale = FOX_SCALE if fox else MLA_SCALE

    def body(*refs):
        if fox:
            q_ref, k_ref, v_ref, fc_ref, fr_ref, o_ref, lse_ref, m_sc, l_sc, acc_sc = refs
        else:
            q_ref, k_ref, v_ref, o_ref, lse_ref, m_sc, l_sc, acc_sc = refs
        i = pl.program_id(1)
        lo = _lane() < HEAD_DIM
        causal = lax.broadcasted_iota(jnp.int32, (tq, tq), 0) >= lax.broadcasted_iota(jnp.int32, (tq, tq), 1)
        outs = []
        for hh in range(2):
            hmask = lo if hh == 0 else jnp.logical_not(lo)
            if fox:
                qh = jnp.where(hmask, q_ref[...], jnp.zeros((), BF16))
            else:
                qh = q_ref[hh]

            def rows_of(ref, j, hh=hh):
                sl = pl.ds(pl.multiple_of(j * tq, tq), tq)
                return ref[sl, :] if fox else ref[hh, sl, :]

            m_sc[...] = jnp.full_like(m_sc, -jnp.inf)
            l_sc[...] = jnp.zeros_like(l_sc)
            acc_sc[...] = jnp.zeros_like(acc_sc)

            def step(j, masked, hh=hh, qh=qh, rows_of=rows_of):
                s = _dot_nt(qh, rows_of(k_ref, j)) * scale
                if fox:
                    s = s + (fc_ref[hh] - fr_ref[hh, j])
                if masked:
                    s = jnp.where(causal, s, -jnp.inf)
                m_prev = m_sc[...]
                m_new = jnp.maximum(m_prev, jnp.max(s, axis=1, keepdims=True))
                alpha = jnp.exp(m_prev - m_new)
                p = jnp.exp(s - m_new)
                l_sc[...] = alpha * l_sc[...] + jnp.sum(p, axis=1, keepdims=True)
                acc_sc[...] = alpha * acc_sc[...] + _dot(p.astype(BF16), rows_of(v_ref, j))
                m_sc[...] = m_new

            def loop_body(j, carry, step=step):
                step(j, False)
                return carry

            lax.fori_loop(0, i, loop_body, 0)
            step(i, True)
            l = l_sc[...]
            outs.append(acc_sc[...] / l)
            lse_ref[hh] = m_sc[...] + jnp.log(l)
        if fox:
            o_ref[...] = jnp.where(lo, outs[0], outs[1])
        else:
            o_ref[...] = jnp.where(lo, pltpu.roll(outs[0], HEAD_DIM, 1), outs[1])

    stat = pl.BlockSpec((2, tq, 1), lambda p, i: (p, i, 0))
    in_specs = _pair_specs(fox, t, tq, True)
    args = [q, k, v]
    if fox:
        in_specs += [stat, pl.BlockSpec((2, nq, 1, tq), lambda p, i: (p, 0, 0, 0))]
        args += [f_col, f_row]
    return pl.pallas_call(
        body, name="fox_attn_fwd" if fox else "mla_attn_fwd", grid=(HEADS // 2, nq),
        out_shape=(jax.ShapeDtypeStruct((t, 4 * LANES), F32), jax.ShapeDtypeStruct((HEADS, t, 1), F32)),
        in_specs=in_specs,
        out_specs=(pl.BlockSpec((tq, LANES), lambda p, i: (i, p)), stat),
        scratch_shapes=[pltpu.VMEM((tq, 1), F32), pltpu.VMEM((tq, 1), F32), pltpu.VMEM((tq, LANES), F32)],
        compiler_params=_cparams("parallel", "arbitrary"),
    )(*args)


def _head_do(fox, hh, do2, lo):
    if fox:
        return jnp.where(lo if hh == 0 else jnp.logical_not(lo), do2, 0.0)
    return jnp.where(lo, 0.0, pltpu.roll(do2, HEAD_DIM, 1) if hh == 0 else do2)


def _attn_bwd_dq(fox, q, k, v, do, o, lse, f_col=None, f_row=None):
    t = q.shape[0] if fox else q.shape[1]
    tq = _row_tile(t)
    nq = t // tq
    scale = FOX_SCALE if fox else MLA_SCALE

    def body(*refs):
        if fox:
            q_ref, k_ref, v_ref, fc_ref, fr_ref, do_ref, o_ref, lse_ref, dq_ref, dl_ref, df_ref, acc_sc, df_sc = refs
        else:
            q_ref, k_ref, v_ref, do_ref, o_ref, lse_ref, dq_ref, dl_ref, acc_sc = refs
        i = pl.program_id(1)
        lo = _lane() < HEAD_DIM
        causal = lax.broadcasted_iota(jnp.int32, (tq, tq), 0) >= lax.broadcasted_iota(jnp.int32, (tq, tq), 1)
        do2 = do_ref[...]
        prod = do2 * o_ref[...]
        dqs = []
        for hh in range(2):
            hmask = lo if hh == 0 else jnp.logical_not(lo)
            delta = jnp.sum(jnp.where(hmask, prod, 0.0), axis=1, keepdims=True)
            dl_ref[hh] = delta
            dob = _head_do(fox, hh, do2, lo).astype(BF16)
            lse_h = lse_ref[hh]
            if fox:
                qh = jnp.where(hmask, q_ref[...], jnp.zeros((), BF16))
            else:
                qh = q_ref[hh]

            def rows_of(ref, j, hh=hh):
                sl = pl.ds(pl.multiple_of(j * tq, tq), tq)
                return ref[sl, :] if fox else ref[hh, sl, :]

            acc_sc[...] = jnp.zeros_like(acc_sc)
            if fox:
                df_sc[...] = jnp.zeros_like(df_sc)

            def step(j, masked, hh=hh, qh=qh, rows_of=rows_of, dob=dob, lse_h=lse_h, delta=delta):
                kj = rows_of(k_ref, j)
                s = _dot_nt(qh, kj) * scale
                if fox:
                    s = s + (fc_ref[hh] - fr_ref[hh, j])
                if masked:
                    s = jnp.where(causal, s, -jnp.inf)
                p = jnp.exp(s - lse_h)
                dp = _dot_nt(dob, rows_of(v_ref, j))
                ds = p * (dp - delta)
                acc_sc[...] += _dot(ds.astype(BF16), kj)
                if fox:
                    df_sc[...] += jnp.sum(ds, axis=1, keepdims=True)

            def loop_body(j, carry, step=step):
                step(j, False)
                return carry

            lax.fori_loop(0, i, loop_body, 0)
            step(i, True)
            dqs.append(acc_sc[...] * scale)
            if fox:
                df_ref[hh] = df_sc[...]
        if fox:
            dq_ref[...] = jnp.where(lo, dqs[0], dqs[1]).astype(BF16)
        else:
            dq_ref[0] = dqs[0]
            dq_ref[1] = dqs[1]

    stat = pl.BlockSpec((2, tq, 1), lambda p, i: (p, i, 0))
    pair = pl.BlockSpec((tq, LANES), lambda p, i: (i, p))
    in_specs = _pair_specs(fox, t, tq, True)
    args = [q, k, v]
    if fox:
        in_specs += [stat, pl.BlockSpec((2, nq, 1, tq), lambda p, i: (p, 0, 0, 0))]
        args += [f_col, f_row]
    in_specs += [pair, pair, stat]
    args += [do, o, lse]
    if fox:
        dq_shape, dq_spec = jax.ShapeDtypeStruct((t, 4 * LANES), BF16), pair
    else:
        dq_shape, dq_spec = jax.ShapeDtypeStruct((HEADS, t, LANES), F32), pl.BlockSpec((2, tq, LANES), lambda p, i: (p, i, 0))
    return pl.pallas_call(
        body, name="fox_attn_bwd_dq" if fox else "mla_attn_bwd_dq", grid=(HEADS // 2, nq),
        out_shape=(dq_shape,) + (jax.ShapeDtypeStruct((HEADS, t, 1), F32),) * (2 if fox else 1),
        in_specs=in_specs, out_specs=(dq_spec,) + (stat,) * (2 if fox else 1),
        scratch_shapes=[pltpu.VMEM((tq, LANES), F32)] + ([pltpu.VMEM((tq, 1), F32)] if fox else []),
        compiler_params=_cparams("parallel", "arbitrary"),
    )(*args)


def _attn_bwd_dkv(fox, q, k, v, do, lse_row, delta_row, f_col=None, f_row=None):
    t = q.shape[0] if fox else q.shape[1]
    tq = _row_tile(t)
    nq = t // tq
    scale = FOX_SCALE if fox else MLA_SCALE

    def body(*refs):
        if fox:
            q_ref, k_ref, v_ref, fc_ref, fr_ref, do_ref, lse_ref, dl_ref, dk_ref, dv_ref, df_ref, dk_sc, dv_sc, df_sc = refs
        else:
            q_ref, k_ref, v_ref, do_ref, lse_ref, dl_ref, dkv_ref, dkr_ref, dk_sc, dv_sc = refs
        j = pl.program_id(1)
        lane = _lane()
        lo = lane < HEAD_DIM
        causal = lax.broadcasted_iota(jnp.int32, (tq, tq), 1) >= lax.broadcasted_iota(jnp.int32, (tq, tq), 0)
        if fox:
            dk_sc[...] = jnp.zeros_like(dk_sc)
            dv_sc[...] = jnp.zeros_like(dv_sc)
        else:
            dkr = jnp.zeros((tq, LANES), F32)
        for hh in range(2):
            hmask = lo if hh == 0 else jnp.logical_not(lo)
            kj = k_ref[...] if fox else k_ref[hh]
            vj = v_ref[...] if fox else v_ref[hh]
            if fox:
                df_sc[...] = jnp.zeros_like(df_sc)
            else:
                dk_sc[...] = jnp.zeros_like(dk_sc)
                dv_sc[...] = jnp.zeros_like(dv_sc)

            def step(i, masked, hh=hh, hmask=hmask, kj=kj, vj=vj):
                sl = pl.ds(pl.multiple_of(i * tq, tq), tq)
                if fox:
                    qi = jnp.where(hmask, q_ref[sl, :], jnp.zeros((), BF16))
                else:
                    qi = q_ref[hh, sl, :]
                dob = _head_do(fox, hh, do_ref[sl, :], lo).astype(BF16)
                st = _dot_nt(kj, qi) * scale
                if fox:
                    st = st + (fr_ref[hh, i] - fc_ref[hh])
                if masked:
                    st = jnp.where(causal, st, -jnp.inf)
                pt = jnp.exp(st - lse_ref[hh, i])
                dpt = _dot_nt(vj, dob)
                dst = pt * (dpt - dl_ref[hh, i])
                dv_sc[...] += _dot(pt.astype(BF16), dob)
                dk_sc[...] += _dot(dst.astype(BF16), qi)
                if fox:
                    df_sc[...] += jnp.sum(dst, axis=1, keepdims=True)

            def loop_body(i, carry, step=step):
                step(i, False)
                return carry

            step(j, True)
            lax.fori_loop(j + 1, nq, loop_body, 0)
            if fox:
                df_ref[hh] = -df_sc[...]
            else:
                dk = dk_sc[...] * scale
                dkv_ref[hh] = jnp.where(lo, dk, dv_sc[...])
                dkr = dkr + jnp.where(jnp.logical_and(lane >= NOPE, lane < NOPE + ROPE), dk, 0.0)
        if fox:
            dk_ref[...] = (dk_sc[...] * scale).astype(BF16)
            dv_ref[...] = dv_sc[...].astype(BF16)
        else:
            dkr_ref[0] = dkr

    stat = pl.BlockSpec((2, tq, 1), lambda p, j: (p, j, 0))
    rows4 = pl.BlockSpec((2, nq, 1, tq), lambda p, j: (p, 0, 0, 0))
    pair = pl.BlockSpec((tq, LANES), lambda p, j: (j, p))
    in_specs = _pair_specs(fox, t, tq, False)
    args = [q, k, v]
    if fox:
        in_specs += [stat, rows4]
        args += [f_col, f_row]
    in_specs += [pl.BlockSpec((t, LANES), lambda p, j: (0, p)), rows4, rows4]
    args += [do, lse_row, delta_row]
    acc = pltpu.VMEM((tq, LANES), F32)
    if fox:
        out_shape = (jax.ShapeDtypeStruct((t, 4 * LANES), BF16), jax.ShapeDtypeStruct((t, 4 * LANES), BF16),
                     jax.ShapeDtypeStruct((HEADS, t, 1), F32))
        out_specs = (pair, pair, stat)
        scratch = [acc, acc, pltpu.VMEM((tq, 1), F32)]
    else:
        out_shape = (jax.ShapeDtypeStruct((HEADS, t, LANES), F32), jax.ShapeDtypeStruct((HEADS // 2, t, LANES), F32))
        out_specs = (pl.BlockSpec((2, tq, LANES), lambda p, j: (p, j, 0)), pl.BlockSpec((1, tq, LANES), lambda p, j: (p, j, 0)))
        scratch = [acc, acc]
    return pl.pallas_call(
        body, name="fox_attn_bwd_dkv" if fox else "mla_attn_bwd_dkv", grid=(HEADS // 2, nq),
        out_shape=out_shape, in_specs=in_specs, out_specs=out_specs, scratch_shapes=scratch,
        compiler_params=_cparams("parallel", "arbitrary"),
    )(*args)


def _attn_out(x, fox_o, mla_o, gf, gm, w_o):
    t = x.shape[0]
    tm = _row_tile(t)

    def body(x_ref, f_ref, m_ref, gf_ref, gm_ref, w_ref, x1_ref, mix_ref):
        nf, _ = _rms(f_ref[...], gf_ref[...])
        nm, _ = _rms(m_ref[...], gm_ref[...])
        nfb, nmb = nf.astype(BF16), nm.astype(BF16)
        mix_ref[:, :FOX_WIDTH] = nfb
        mix_ref[:, FOX_WIDTH:] = nmb
        x1_ref[...] = x_ref[...] + _dot(nfb, w_ref[:FOX_WIDTH, :]) + _dot(nmb, w_ref[FOX_WIDTH:, :])

    row = lambda n: pl.BlockSpec((tm, n), lambda i: (i, 0))
    full = lambda a: pl.BlockSpec(a.shape, lambda i: (0,) * a.ndim)
    return pl.pallas_call(
        body, name="attn_out", grid=(t // tm,),
        out_shape=(jax.ShapeDtypeStruct((t, D_MODEL), F32), jax.ShapeDtypeStruct((t, D_MODEL), BF16)),
        in_specs=[row(D_MODEL), row(FOX_WIDTH), row(MLA_WIDTH), full(gf), full(gm), full(w_o)],
        out_specs=(row(D_MODEL), row(D_MODEL)),
        compiler_params=_cparams("parallel"),
    )(x, fox_o, mla_o, gf, gm, w_o)


def _mlp_fwd(x1, g_mlp, w_up, w_down, g_fin, target):
    t = x1.shape[0]
    tm = _row_tile(t)
    nf = w_up.shape[0]
    tf = w_up.shape[2]

    def body(x_ref, g_ref, wu_ref, wd_ref, gf_ref, t_ref, u_ref, h_ref, dx_ref, loss_ref, dg_ref, acc_sc, h_sc):
        i, f = pl.program_id(0), pl.program_id(1)

        @pl.when(jnp.logical_and(i == 0, f == 0))
        def _():
            loss_ref[...] = jnp.zeros_like(loss_ref)
            dg_ref[...] = jnp.zeros_like(dg_ref)

        @pl.when(f == 0)
        def _():
            h, _ = _rms(x_ref[...], g_ref[...])
            h_sc[...] = h.astype(BF16)
            h_ref[...] = h_sc[...]
            acc_sc[...] = jnp.zeros_like(acc_sc)

        u = _dot(h_sc[...], wu_ref[...])
        u_ref[...] = u
        r = jnp.maximum(u, 0.0)
        acc_sc[...] += _dot((r * r).astype(BF16), wd_ref[...])

        @pl.when(f == nf - 1)
        def _():
            x2 = x_ref[...] + acc_sc[...]
            y, r2 = _rms(x2, gf_ref[...])
            err = y - t_ref[...]
            loss_ref[...] += 0.5 * jnp.sum(jnp.mean(err * err, axis=-1, keepdims=True))
            dx, dg = _rms_bwd(x2, gf_ref[...], r2, err * (1.0 / D_MODEL))
            dx_ref[...] = dx
            dg_ref[...] += dg

    row = lambda n: pl.BlockSpec((tm, n), lambda i, f: (i, 0))
    vec = pl.BlockSpec((1, D_MODEL), lambda i, f: (0, 0))
    return pl.pallas_call(
        body, name="mlp_fwd", grid=(t // tm, nf),
        out_shape=(jax.ShapeDtypeStruct((t, D_FF), F32), jax.ShapeDtypeStruct((t, D_MODEL), BF16),
                   jax.ShapeDtypeStruct((t, D_MODEL), F32), jax.ShapeDtypeStruct((8, LANES), F32),
                   jax.ShapeDtypeStruct((1, D_MODEL), F32)),
        in_specs=[row(D_MODEL), vec, pl.BlockSpec((None, D_MODEL, tf), lambda i, f: (f, 0, 0)),
                  pl.BlockSpec((None, tf, D_MODEL), lambda i, f: (f, 0, 0)), vec, row(D_MODEL)],
        out_specs=(pl.BlockSpec((tm, tf), lambda i, f: (i, f)), row(D_MODEL), row(D_MODEL),
                   pl.BlockSpec((8, LANES), lambda i, f: (0, 0)), vec),
        scratch_shapes=[pltpu.VMEM((tm, D_MODEL), F32), pltpu.VMEM((tm, D_MODEL), BF16)],
        compiler_params=_cparams("arbitrary", "arbitrary"),
    )(x1, g_mlp, w_up, w_down, g_fin, target)


def _mlp_bwd(dx2, u, x1, g_mlp, w_up, w_down):
    t = x1.shape[0]
    tm = _row_tile(t)
    nf = w_up.shape[0]
    tf = w_up.shape[2]

    def body(dx_ref, u_ref, x_ref, g_ref, wu_ref, wd_ref, du_ref, a_ref, dx1_ref, dg_ref, acc_sc):
        i, f = pl.program_id(0), pl.program_id(1)

        @pl.when(jnp.logical_and(i == 0, f == 0))
        def _():
            dg_ref[...] = jnp.zeros_like(dg_ref)

        @pl.when(f == 0)
        def _():
            acc_sc[...] = jnp.zeros_like(acc_sc)

        r = jnp.maximum(u_ref[...], 0.0)
        a_ref[...] = (r * r).astype(BF16)
        da = _dot_nt(dx_ref[...].astype(BF16), wd_ref[...])
        du = (da * (2.0 * r)).astype(BF16)
        du_ref[...] = du
        acc_sc[...] += _dot_nt(du, wu_ref[...])

        @pl.when(f == nf - 1)
        def _():
            x = x_ref[...]
            _, r1 = _rms(x, g_ref[...])
            dx, dg = _rms_bwd(x, g_ref[...], r1, acc_sc[...])
            dx1_ref[...] = dx_ref[...] + dx
            dg_ref[...] += dg

    row = lambda n: pl.BlockSpec((tm, n), lambda i, f: (i, 0))
    vec = pl.BlockSpec((1, D_MODEL), lambda i, f: (0, 0))
    blk = pl.BlockSpec((tm, tf), lambda i, f: (i, f))
    return pl.pallas_call(
        body, name="mlp_bwd", grid=(t // tm, nf),
        out_shape=(jax.ShapeDtypeStruct((t, D_FF), BF16), jax.ShapeDtypeStruct((t, D_FF), BF16),
                   jax.ShapeDtypeStruct((t, D_MODEL), F32), jax.ShapeDtypeStruct((1, D_MODEL), F32)),
        in_specs=[row(D_MODEL), blk, row(D_MODEL), vec, pl.BlockSpec((None, D_MODEL, tf), lambda i, f: (f, 0, 0)),
                  pl.BlockSpec((None, tf, D_MODEL), lambda i, f: (f, 0, 0))],
        out_specs=(blk, blk, row(D_MODEL), vec),
        scratch_shapes=[pltpu.VMEM((tm, D_MODEL), F32)],
        compiler_params=_cparams("arbitrary", "arbitrary"),
    )(dx2, u, x1, g_mlp, w_up, w_down)


def _matmul_tn(name, a, b, batch_out=None):
    batched = b.ndim == 3
    t, m = a.shape[-2:]
    n = b.shape[-1]
    tk = _row_tile(t)
    bm = min(m, 512)
    bn = (n if n <= 1024 else 512) if batch_out is None else n // batch_out
    lead = b.shape[0] if batched else 1

    def body(a_ref, b_ref, o_ref, acc_sc):
        kk = pl.program_id(3)

        @pl.when(kk == 0)
        def _():
            acc_sc[...] = jnp.zeros_like(acc_sc)

        acc_sc[...] += _dot_tn(a_ref[...].astype(BF16), b_ref[...].astype(BF16))

        @pl.when(kk == pl.num_programs(3) - 1)
        def _():
            o_ref[...] = acc_sc[...]

    if batched:
        a_spec = pl.BlockSpec((tk, bm), lambda h, i, j, kk: (kk, i))
        b_spec = pl.BlockSpec((None, tk, bn), lambda h, i, j, kk: (h, kk, j))
        o_spec = pl.BlockSpec((None, bm, bn), lambda h, i, j, kk: (h, i, j))
        o_shape = (lead, m, n)
    else:
        a_spec = pl.BlockSpec((tk, bm), lambda h, i, j, kk: (kk, i))
        b_spec = pl.BlockSpec((tk, bn), lambda h, i, j, kk: (kk, j))
        if batch_out is None:
            o_spec = pl.BlockSpec((bm, bn), lambda h, i, j, kk: (i, j))
            o_shape = (m, n)
        else:
            o_spec = pl.BlockSpec((None, bm, bn), lambda h, i, j, kk: (j, i, 0))
            o_shape = (batch_out, m, bn)
    return pl.pallas_call(
        body, name=name, grid=(lead, m // bm, n // bn, t // tk),
        out_shape=jax.ShapeDtypeStruct(o_shape, F32),
        in_specs=[a_spec, b_spec], out_specs=o_spec,
        scratch_shapes=[pltpu.VMEM((bm, bn), F32)],
        compiler_params=_cparams("parallel", "parallel", "parallel", "arbitrary"),
    )(a, b)


def _attn_out_bwd(dx1, fox_o, mla_o, gf, gm, w_o):
    t = dx1.shape[0]
    tm = _row_tile(t)

    def body(dx_ref, f_ref, m_ref, gf_ref, gm_ref, w_ref, df_ref, dm_ref, dgf_ref, dgm_ref):
        @pl.when(pl.program_id(0) == 0)
        def _():
            dgf_ref[...] = jnp.zeros_like(dgf_ref)
            dgm_ref[...] = jnp.zeros_like(dgm_ref)
        dxb = dx_ref[...].astype(BF16)
        for o_ref, g_ref, lo_row, d_ref, dg_ref in ((f_ref, gf_ref, 0, df_ref, dgf_ref), (m_ref, gm_ref, FOX_WIDTH, dm_ref, dgm_ref)):
            dn = _dot_nt(dxb, w_ref[lo_row:lo_row + FOX_WIDTH, :])
            o = o_ref[...]
            _, r = _rms(o, g_ref[...])
            d, dg = _rms_bwd(o, g_ref[...], r, dn)
            d_ref[...] = d
            dg_ref[...] += dg

    row = lambda n: pl.BlockSpec((tm, n), lambda i: (i, 0))
    full = lambda a: pl.BlockSpec(a.shape, lambda i: (0,) * a.ndim)
    vec = pl.BlockSpec((1, FOX_WIDTH), lambda i: (0, 0))
    o_shape = jax.ShapeDtypeStruct((t, FOX_WIDTH), F32)
    g_shape = jax.ShapeDtypeStruct((1, FOX_WIDTH), F32)
    return pl.pallas_call(
        body, name="attn_out_bwd", grid=(t // tm,),
        out_shape=(o_shape, o_shape, g_shape, g_shape),
        in_specs=[row(D_MODEL), row(FOX_WIDTH), row(MLA_WIDTH), full(gf), full(gm), full(w_o)],
        out_specs=(row(FOX_WIDTH), row(MLA_WIDTH), vec, vec),
        compiler_params=_cparams("arbitrary"),
    )(dx1, fox_o, mla_o, gf, gm, w_o)


def _mla_prep_bwd(dq, dkv, dkr, dz, rest, gq, gkv, wq, wkv, cos, sin):
    t = rest.shape[0]
    tm = _row_tile(t)

    def body(dq_ref, dkv_ref, dkr_ref, dz_ref, r_ref, gq_ref, gkv_ref, wq_ref, wkv_ref, c_ref, s_ref,
             dr_ref, dqp_ref, dkvb_ref, dgq_ref, dgkv_ref):
        @pl.when(pl.program_id(0) == 0)
        def _():
            dgq_ref[...] = jnp.zeros_like(dgq_ref)
            dgkv_ref[...] = jnp.zeros_like(dgkv_ref)
        cos_, sin_ = c_ref[...], s_ref[...]
        dcq = jnp.zeros((tm, Q_RANK), F32)
        dckv = jnp.zeros((tm, KV_RANK), F32)
        for h in range(HEADS):
            dqp = _rope_bwd(dq_ref[h], cos_, sin_).astype(BF16)
            dqp_ref[h] = dqp
            dcq = dcq + _dot_nt(dqp, wq_ref[h])
            dkvb = dkv_ref[h].astype(BF16)
            dkvb_ref[h] = dkvb
            dckv = dckv + _dot_nt(dkvb, wkv_ref[h])
        dkrope = dkr_ref[0]
        for pr in range(1, HEADS // 2):
            dkrope = dkrope + dkr_ref[pr]
        cq = r_ref[:, 128:512]
        _, rq = _rms(cq, gq_ref[...])
        d_cq, dgq = _rms_bwd(cq, gq_ref[...], rq, dcq)
        ckv = r_ref[:, 512:768]
        _, rkv = _rms(ckv, gkv_ref[...])
        d_ckv, dgkv = _rms_bwd(ckv, gkv_ref[...], rkv, dckv)
        dgq_ref[...] += dgq
        dgkv_ref[...] += dgkv
        dr_ref[:, 0:128] = dz_ref[...].astype(BF16)
        dr_ref[:, 128:512] = d_cq.astype(BF16)
        dr_ref[:, 512:768] = d_ckv.astype(BF16)
        dr_ref[:, 768:896] = _rope_bwd(dkrope, cos_, sin_).astype(BF16)

    row = lambda n: pl.BlockSpec((tm, n), lambda i: (i, 0))
    full = lambda a: pl.BlockSpec(a.shape, lambda i: (0,) * a.ndim)
    heads = pl.BlockSpec((HEADS, tm, LANES), lambda i: (0, i, 0))
    hshape = jax.ShapeDtypeStruct((HEADS, t, LANES), BF16)
    return pl.pallas_call(
        body, name="mla_prep_bwd", grid=(t // tm,),
        out_shape=(jax.ShapeDtypeStruct((t, REST_COLS), BF16), hshape, hshape,
                   jax.ShapeDtypeStruct((1, Q_RANK), F32), jax.ShapeDtypeStruct((1, KV_RANK), F32)),
        in_specs=[heads, heads, pl.BlockSpec((HEADS // 2, tm, LANES), lambda i: (0, i, 0)), row(LANES), row(REST_COLS),
                  full(gq), full(gkv), full(wq), full(wkv), row(LANES), row(LANES)],
        out_specs=(row(REST_COLS), heads, heads, pl.BlockSpec((1, Q_RANK), lambda i: (0, 0)),
                   pl.BlockSpec((1, KV_RANK), lambda i: (0, 0))),
        compiler_params=_cparams("arbitrary"),
    )(dq, dkv, dkr, dz, rest, gq, gkv, wq, wkv, cos, sin)


def _in_proj_bwd(x, g, dx1, dfq, dfk, dfv, drest, w_qkv, w_rest):
    t = x.shape[0]
    tm = _row_tile(t)

    def body(x_ref, g_ref, dx1_ref, dq_ref, dk_ref, dv_ref, dr_ref, wq_ref, wr_ref, dx_ref, dg_ref):
        @pl.when(pl.program_id(0) == 0)
        def _():
            dg_ref[...] = jnp.zeros_like(dg_ref)
        dh = _dot_nt(dr_ref[...], wr_ref[...])
        for n, ref in enumerate((dq_ref, dk_ref, dv_ref)):
            dh = dh + _dot_nt(ref[...], wq_ref[:, n * FOX_WIDTH:(n + 1) * FOX_WIDTH])
        xv = x_ref[...]
        _, r = _rms(xv, g_ref[...])
        dx, dg = _rms_bwd(xv, g_ref[...], r, dh)
        dx_ref[...] = dx1_ref[...] + dx
        dg_ref[...] += dg

    row = lambda n: pl.BlockSpec((tm, n), lambda i: (i, 0))
    full = lambda a: pl.BlockSpec(a.shape, lambda i: (0,) * a.ndim)
    vec = pl.BlockSpec((1, D_MODEL), lambda i: (0, 0))
    return pl.pallas_call(
        body, name="in_proj_bwd", grid=(t // tm,),
        out_shape=(jax.ShapeDtypeStruct((t, D_MODEL), F32), jax.ShapeDtypeStruct((1, D_MODEL), F32)),
        in_specs=[row(D_MODEL), full(g), row(D_MODEL), row(FOX_WIDTH), row(FOX_WIDTH), row(FOX_WIDTH), row(REST_COLS),
                  full(w_qkv), full(w_rest)],
        out_specs=(row(D_MODEL), vec),
        compiler_params=_cparams("arbitrary"),
    )(x, g, dx1, dfq, dfk, dfv, drest, w_qkv, w_rest)


def _flatten_shards(w_in, w_uq, w_ukv, w_o, w_up, w_down):
    return jnp.concatenate([a.reshape(-1) for a in (w_in, w_uq, w_ukv, w_o, w_up, w_down)]).reshape(FLAT_ROWS, LANES)


def _unflatten_shards(flat, lead=()):
    flat = flat.reshape(lead + (FLAT,))
    shapes = ((D_MODEL, 277), (Q_RANK, QK_DIM), (KV_RANK, 128), (128, D_MODEL), (D_MODEL, 512), (512, D_MODEL))
    out, off = [], 0
    for size, shp in zip(SHARD_SIZES, shapes):
        out.append(flat[..., off:off + size].reshape(lead + shp))
        off += size
    return out


def _pad_cols(a, n):
    return jnp.pad(a, ((0, 0),) * (a.ndim - 1) + ((0, n - a.shape[-1]),))


def _rows4(a, tq):
    h, t, _ = a.shape
    return a.reshape(h, t // tq, 1, tq)


def _pack_small(parts):
    flat = jnp.concatenate([p.reshape(-1) for p in parts])
    return jnp.pad(flat, (0, SMALL_ROWS * LANES - flat.shape[0])).reshape(SMALL_ROWS, LANES)


def _unpack_small(vec, shapes):
    flat = vec.reshape(-1)
    out, off = [], 0
    for size, shp in zip(SMALL_SIZES, shapes):
        out.append(flat[off:off + size].reshape(shp))
        off += size
    return out


def kernel(x, positions, attn_norm_g, w_in, b_forget, q_norm_g, w_uq, kv_norm_g, w_ukv, fox_out_g, mla_out_g, w_o, mlp_norm_g, w_up, w_down, final_norm_g, loss_target, m_attn_norm_g, m_w_in, m_b_forget, m_q_norm_g, m_w_uq, m_kv_norm_g, m_w_ukv, m_fox_out_g, m_mla_out_g, m_w_o, m_mlp_norm_g, m_w_up, m_w_down, m_final_norm_g, v_attn_norm_g, v_w_in, v_b_forget, v_q_norm_g, v_w_uq, v_kv_norm_g, v_w_ukv, v_fox_out_g, v_mla_out_g, v_w_o, v_mlp_norm_g, v_w_up, v_w_down, v_final_norm_g):
    t = x.shape[1]
    tq = _row_tile(t)
    xs = x[0]
    target = loss_target[0]
    me = _my_index()

    w_flat = _flatten_shards(w_in[0], w_uq[0], w_ukv[0], w_o[0], w_up[0], w_down[0])
    gathered = _all_gather(w_flat.astype(BF16))
    g_in, g_uq, g_ukv, g_o, g_up, g_down = _unflatten_shards(gathered, (N_DEV,))
    win = jnp.transpose(g_in, (1, 0, 2)).reshape(D_MODEL, IN_COLS)
    w_qkv = win[:, :3 * FOX_WIDTH]
    w_rest = jnp.concatenate([
        _pad_cols(win[:, 1536:1544], LANES), win[:, 1544:2184],
        jnp.pad(win[:, 2184:2216], ((0, 0), (NOPE, LANES - NOPE - ROPE)))], axis=1)
    wq = _pad_cols(g_uq, LANES)
    wkv = g_ukv
    wo = g_o.reshape(D_MODEL, D_MODEL)

    cos, sin = _rope_tables(positions.reshape(t, 1))
    h1, fq, fk, fv, rest = _in_proj(xs, attn_norm_g, w_qkv, w_rest)
    b128 = _pad_cols(b_forget, LANES)
    f_cum = _forget_cumsum(rest, b128)
    f_col = jnp.transpose(f_cum[:, :HEADS]).reshape(HEADS, t, 1)
    f_row = _rows4(f_col, tq)
    fox_o, fox_lse = _attn_fwd(True, fq, fk, fv, f_col, f_row)
    mq, mk, mkv, cqn, ckvn = _mla_prep(rest, q_norm_g, kv_norm_g, wq, wkv, cos, sin)
    mla_o, mla_lse = _attn_fwd(False, mq, mk, mkv)
    x1, mixed = _attn_out(xs, fox_o, mla_o, fox_out_g, mla_out_g, wo)
    u, h2, dx2, loss8, d_gfin = _mlp_fwd(x1, mlp_norm_g, g_up, g_down, final_norm_g.reshape(1, D_MODEL), target)

    du, act, dx1, d_gmlp = _mlp_bwd(dx2, u, x1, mlp_norm_g, g_up, g_down)
    dw_down = _matmul_tn("dw_down", act, dx2)
    dw_up = _matmul_tn("dw_up", h2, du, batch_out=N_DEV)
    dfox_o, dmla_o, d_gfox, d_gmla = _attn_out_bwd(dx1, fox_o, mla_o, fox_out_g, mla_out_g, wo)
    dw_o = _matmul_tn("dw_o", mixed, dx1)

    dfq, fox_delta, d_fq_col = _attn_bwd_dq(True, fq, fk, fv, dfox_o, fox_o, fox_lse, f_col, f_row)
    dfk, dfv, d_fcol = _attn_bwd_dkv(True, fq, fk, fv, dfox_o, _rows4(fox_lse, tq), _rows4(fox_delta, tq), f_col, f_row)
    as_rows = lambda a: _pad_cols(jnp.transpose(a.reshape(HEADS, t)), LANES)
    dz, d_b = _forget_bwd(rest, b128, as_rows(d_fq_col), as_rows(d_fcol))

    dmq, mla_delta = _attn_bwd_dq(False, mq, mk, mkv, dmla_o, mla_o, mla_lse)
    dmkv, dmkr = _attn_bwd_dkv(False, mq, mk, mkv, dmla_o, _rows4(mla_lse, tq), _rows4(mla_delta, tq))
    drest, dqp, dkvb, d_gq, d_gkv = _mla_prep_bwd(dmq, dmkv, dmkr, dz, rest, q_norm_g, kv_norm_g, wq, wkv, cos, sin)
    dw_uq = _matmul_tn("dw_uq", cqn, dqp)
    dw_ukv = _matmul_tn("dw_ukv", ckvn, dkvb)
    grad_x, d_gattn = _in_proj_bwd(xs, attn_norm_g, dx1, dfq, dfk, dfv, drest, w_qkv, w_rest)
    dw_q = _matmul_tn("dw_in_q", h1, dfq)
    dw_k = _matmul_tn("dw_in_k", h1, dfk)
    dw_v = _matmul_tn("dw_in_v", h1, dfv)
    dw_r = _matmul_tn("dw_in_rest", h1, drest)

    dw_in = jnp.concatenate([dw_q, dw_k, dw_v, dw_r[:, 0:8], dw_r[:, 128:768], dw_r[:, 768 + NOPE:768 + NOPE + ROPE]], axis=1)
    dw_in = jnp.transpose(dw_in.reshape(D_MODEL, N_DEV, 277), (1, 0, 2))
    g_full = jnp.concatenate([
        dw_in.reshape(N_DEV, -1), dw_uq[:, :, :QK_DIM].reshape(N_DEV, -1), dw_ukv.reshape(N_DEV, -1),
        dw_o.reshape(N_DEV, -1), dw_up.reshape(N_DEV, -1), dw_down.reshape(N_DEV, -1)], axis=1)
    g4 = g_full.reshape(4, 2, FLAT_ROWS, LANES)
    my_c = lax.axis_index("c")
    keep = lax.dynamic_index_in_dim(g4, my_c, axis=1, keepdims=False)
    send = lax.dynamic_index_in_dim(g4, 1 - my_c, axis=1, keepdims=False).astype(BF16)
    got = _sibling_exchange(send)
    chip_f32, chip_bf16 = _sibling_sum(keep, got)
    others = _chip_exchange(chip_bf16)
    own = lax.dynamic_index_in_dim(chip_f32, 2 * lax.axis_index("x") + lax.axis_index("y"), axis=0, keepdims=False)
    sharded = (w_in, w_uq, w_ukv, w_o, w_up, w_down)
    moments_m = (m_w_in, m_w_uq, m_w_ukv, m_w_o, m_w_up, m_w_down)
    moments_v = (v_w_in, v_w_uq, v_w_ukv, v_w_o, v_w_up, v_w_down)
    flat3 = [_flatten_shards(*[a[0] for a in group]) for group in (sharded, moments_m, moments_v)]
    big = _adamw_sharded(own, others, *flat3)
    big_g, big_d, big_m, big_v = [[a[None] for a in _unflatten_shards(f)] for f in big]

    small_w = (attn_norm_g, b_forget, q_norm_g, kv_norm_g, fox_out_g, mla_out_g, mlp_norm_g, final_norm_g)
    small_m = (m_attn_norm_g, m_b_forget, m_q_norm_g, m_kv_norm_g, m_fox_out_g, m_mla_out_g, m_mlp_norm_g, m_final_norm_g)
    small_v = (v_attn_norm_g, v_b_forget, v_q_norm_g, v_kv_norm_g, v_fox_out_g, v_mla_out_g, v_mlp_norm_g, v_final_norm_g)
    partial = _pack_small([d_gattn, d_b[:, :HEADS], d_gq, d_gkv, d_gfox, d_gmla, d_gmlp, d_gfin, loss8[0, :1]])
    total = _small_all_reduce(partial)
    loss = total.reshape(-1)[LOSS_SLOT]
    small_g_vec = total * (lax.broadcasted_iota(jnp.int32, total.shape, 0) * LANES
                           + lax.broadcasted_iota(jnp.int32, total.shape, 1) < LOSS_SLOT).astype(F32)
    sd, sm, sv = _adamw_small(small_g_vec, _pack_small(small_w), _pack_small(small_m), _pack_small(small_v))
    shapes = [a.shape for a in small_w]
    s_g, s_d, s_m, s_v = [_unpack_small(vv, shapes) for vv in (small_g_vec, sd, sm, sv)]

    def ordered(small, bigs):
        ga, bf, gq_, gkv_, gfo, gml, gmlp_, gfin_ = small
        bin_, buq, bukv, bo, bup, bdown = bigs
        return [ga, bin_, bf, gq_, buq, gkv_, bukv, gfo, gml, bo, gmlp_, bup, bdown, gfin_]

    return (loss, grad_x[None], *ordered(s_g, big_g), *ordered(s_d, big_d), *ordered(s_m, big_m), *ordered(s_v, big_v))
```

```python
import functools
import math

import numpy as np
import jax
import jax.numpy as jnp
from jax import lax
from jax.experimental import pallas as pl
from jax.experimental.pallas import tpu as pltpu

F32 = jnp.float32
BF16 = jnp.bfloat16
MESH = pl.DeviceIdType.MESH

D_MODEL = 1024
HEADS = 8
HEAD_DIM = 64
FOX_WIDTH = 512
MLA_WIDTH = 512
NOPE = 64
ROPE = 32
QK_DIM = 96
Q_RANK = 384
KV_RANK = 256
D_FF = 4096
IN_COLS = 2216
ROPE_THETA = 10000.0
EPS = 1e-6
FOX_SCALE = 1.0 / math.sqrt(HEAD_DIM)
MLA_SCALE = 1.0 / math.sqrt(QK_DIM)
ADAM_LR = 0.001
ADAM_B1 = 0.9
ADAM_B2 = 0.999
ADAM_EPS = 1e-08
ADAM_WD = 0.01
ADAM_STEP = 10

N_DEV = 8
LANES = 128
REST_COLS = 896
REST_CQ = LANES
REST_CKV = REST_CQ + Q_RANK
REST_KR = REST_CKV + KV_RANK
LOG2E = 1.4426950408889634
VMEM_LIMIT = 56 * 1024 * 1024

SHARD_SIZES = (D_MODEL * 277, Q_RANK * QK_DIM, KV_RANK * 128, 128 * D_MODEL, D_MODEL * 512, 512 * D_MODEL)
FLAT = sum(SHARD_SIZES)
FLAT_ROWS = FLAT // LANES
FLAT_TILE = FLAT_ROWS // 3
SMALL_NAMES = ("attn_norm_g", "b_forget", "q_norm_g", "kv_norm_g", "fox_out_g", "mla_out_g", "mlp_norm_g", "final_norm_g")
SMALL_SIZES = (1024, 8, 384, 256, 512, 512, 1024, 1024)
SMALL_ROWS = 40
LOSS_SLOT = sum(SMALL_SIZES)


def _cparams(*sem):
    return pltpu.CompilerParams(dimension_semantics=sem or None, vmem_limit_bytes=VMEM_LIMIT)


def _row_tile(t):
    return 512 if t >= 2048 else 128


def _dot(a, b):
    return jnp.dot(a, b, preferred_element_type=F32)


def _dot_nt(a, b):
    return lax.dot_general(a, b, (((1,), (1,)), ((), ())), preferred_element_type=F32)


def _dot_tn(a, b):
    return lax.dot_general(a, b, (((0,), (0,)), ((), ())), preferred_element_type=F32)


def _rms(x, g):
    r = lax.rsqrt(jnp.mean(x * x, axis=-1, keepdims=True) + EPS)
    return x * r * g, r


def _rms_bwd(x, g, r, dy):
    xh = x * r
    gdy = dy * g
    dx = r * (gdy - xh * jnp.mean(gdy * xh, axis=-1, keepdims=True))
    return dx, jnp.sum(dy * xh, axis=0, keepdims=True)


def _lane():
    return lax.broadcasted_iota(jnp.int32, (1, LANES), 1)


def _rot(x):
    lane = _lane()
    half = NOPE + ROPE // 2
    first = jnp.logical_and(lane >= NOPE, lane < half)
    second = jnp.logical_and(lane >= half, lane < NOPE + ROPE)
    return jnp.where(first, -pltpu.roll(x, LANES - ROPE // 2, 1), jnp.where(second, pltpu.roll(x, ROPE // 2, 1), 0.0))


def _rope(x, cos, sin):
    return x * cos + _rot(x) * sin


def _rope_bwd(dy, cos, sin):
    return dy * cos - _rot(dy * sin)


def _my_index():
    return 4 * lax.axis_index("x") + 2 * lax.axis_index("y") + lax.axis_index("c")


def _all_gather(block):
    rows, cols = block.shape

    def body(x_ref, out_ref, send_sems, recv_sems, local_sem):
        x, y, c = lax.axis_index("x"), lax.axis_index("y"), lax.axis_index("c")
        me, sibling = (x, y, c), (x, y, 1 - c)
        chips = [(1 - x, y), (x, 1 - y), (1 - x, 1 - y)]

        def slot(px, py, pc):
            return out_ref.at[4 * px + 2 * py + pc]

        def copy(k, blk, to, src=None):
            return pltpu.make_async_remote_copy(
                src_ref=slot(*blk) if src is None else src, dst_ref=slot(*blk),
                send_sem=send_sems.at[k], recv_sem=recv_sems.at[k], device_id=to, device_id_type=MESH)

        mine = pltpu.make_async_copy(x_ref, slot(*me), local_sem)
        mine.start()
        first = [copy(0, me, sibling, src=x_ref)]
        first += [copy(1 + j, me, (*chip, c), src=x_ref) for j, chip in enumerate(chips)]
        for cp in first:
            cp.start()
        passed = [copy(4 + j, (*chip, c), sibling) for j, chip in enumerate(chips)]
        for j, chip in enumerate(chips):
            copy(1 + j, (*chip, c), me).wait_recv()
            passed[j].start()
        copy(0, sibling, me).wait_recv()
        for j, chip in enumerate(chips):
            copy(4 + j, (*chip, 1 - c), me).wait_recv()
        for cp in first + passed:
            cp.wait_send()
        mine.wait()

    return pl.pallas_call(
        body, name="all_gather_weights",
        out_shape=jax.ShapeDtypeStruct((N_DEV, rows, cols), block.dtype),
        in_specs=[pl.BlockSpec(memory_space=pl.ANY)],
        out_specs=pl.BlockSpec(memory_space=pl.ANY),
        scratch_shapes=[pltpu.SemaphoreType.DMA((7,)), pltpu.SemaphoreType.DMA((7,)), pltpu.SemaphoreType.DMA],
    )(block)


def _sibling_exchange(send):
    def body(s_ref, land_ref, send_sem, recv_sem):
        x, y, c = lax.axis_index("x"), lax.axis_index("y"), lax.axis_index("c")
        cp = pltpu.make_async_remote_copy(src_ref=s_ref, dst_ref=land_ref, send_sem=send_sem, recv_sem=recv_sem,
                                          device_id=(x, y, 1 - c), device_id_type=MESH)
        cp.start()
        cp.wait()

    return pl.pallas_call(
        body, name="rs_sibling_exchange",
        out_shape=jax.ShapeDtypeStruct(send.shape, send.dtype),
        in_specs=[pl.BlockSpec(memory_space=pl.ANY)],
        out_specs=pl.BlockSpec(memory_space=pl.ANY),
        scratch_shapes=[pltpu.SemaphoreType.DMA, pltpu.SemaphoreType.DMA],
    )(send)


def _chip_exchange(part):
    _, rows, cols = part.shape

    def body(p_ref, land_ref, send_sems, recv_sems):
        x, y, c = lax.axis_index("x"), lax.axis_index("y"), lax.axis_index("c")
        chips = [(1 - x, y), (x, 1 - y), (1 - x, 1 - y)]
        cps = [pltpu.make_async_remote_copy(
            src_ref=p_ref.at[2 * cx + cy], dst_ref=land_ref.at[k], send_sem=send_sems.at[k], recv_sem=recv_sems.at[k],
            device_id=(cx, cy, c), device_id_type=MESH) for k, (cx, cy) in enumerate(chips)]
        for cp in cps:
            cp.start()
        for cp in cps:
            cp.wait()

    return pl.pallas_call(
        body, name="rs_chip_exchange",
        out_shape=jax.ShapeDtypeStruct((3, rows, cols), part.dtype),
        in_specs=[pl.BlockSpec(memory_space=pl.ANY)],
        out_specs=pl.BlockSpec(memory_space=pl.ANY),
        scratch_shapes=[pltpu.SemaphoreType.DMA((3,)), pltpu.SemaphoreType.DMA((3,))],
    )(part)


def _small_all_reduce(vec):
    def body(v_ref, out_ref, land_ref, send_sems, recv_sems):
        x, y, c = lax.axis_index("x"), lax.axis_index("y"), lax.axis_index("c")
        me = 4 * x + 2 * y + c
        land_ref[me] = v_ref[...]
        cps = []
        for k in range(1, N_DEV):
            px, py, pc = x ^ (k >> 2), y ^ ((k >> 1) & 1), c ^ (k & 1)
            cps.append(pltpu.make_async_remote_copy(
                src_ref=v_ref, dst_ref=land_ref.at[me], send_sem=send_sems.at[k - 1], recv_sem=recv_sems.at[k - 1],
                device_id=(px, py, pc), device_id_type=MESH))
        for cp in cps:
            cp.start()
        for cp in cps:
            cp.wait()
        acc = land_ref[0]
        for d in range(1, N_DEV):
            acc = acc + land_ref[d]
        out_ref[...] = acc

    return pl.pallas_call(
        body, name="small_all_reduce",
        out_shape=jax.ShapeDtypeStruct(vec.shape, F32),
        in_specs=[pl.BlockSpec(memory_space=pltpu.VMEM)],
        out_specs=pl.BlockSpec(memory_space=pltpu.VMEM),
        scratch_shapes=[pltpu.VMEM((N_DEV,) + vec.shape, F32),
                        pltpu.SemaphoreType.DMA((N_DEV - 1,)), pltpu.SemaphoreType.DMA((N_DEV - 1,))],
    )(vec)


def _flat_spec(tile, lead=()):
    n = len(lead)
    return pl.BlockSpec(lead + (tile, LANES), lambda i: (0,) * n + (i, 0))


def _sibling_sum(keep, got):
    _, rows, _ = keep.shape
    tile = FLAT_TILE if rows == FLAT_ROWS else rows

    def body(k_ref, g_ref, f_ref, b_ref):
        s = k_ref[...] + g_ref[...].astype(F32)
        f_ref[...] = s
        b_ref[...] = s.astype(BF16)

    return pl.pallas_call(
        body, name="rs_sibling_sum", grid=(rows // tile,),
        out_shape=(jax.ShapeDtypeStruct(keep.shape, F32), jax.ShapeDtypeStruct(keep.shape, BF16)),
        in_specs=[_flat_spec(tile, (4,)), _flat_spec(tile, (4,))],
        out_specs=(_flat_spec(tile, (4,)), _flat_spec(tile, (4,))),
        compiler_params=_cparams("parallel"),
    )(keep, got)


def _adamw_math(w, g, m, v):
    m2 = ADAM_B1 * m + (1.0 - ADAM_B1) * g
    v2 = ADAM_B2 * v + (1.0 - ADAM_B2) * (g * g)
    m_hat = m2 / (1.0 - ADAM_B1 ** ADAM_STEP)
    v_hat = v2 / (1.0 - ADAM_B2 ** ADAM_STEP)
    delta = -ADAM_LR * (m_hat / (jnp.sqrt(v_hat) + ADAM_EPS) + ADAM_WD * w)
    return delta, m2, v2


def _adamw_sharded(own, got, w, m, v):
    rows = own.shape[0]
    tile = FLAT_TILE if rows == FLAT_ROWS else rows

    def body(o_ref, r_ref, w_ref, m_ref, v_ref, g_out, d_out, m_out, v_out):
        g = o_ref[...]
        for k in range(3):
            g = g + r_ref[k].astype(F32)
        d, m2, v2 = _adamw_math(w_ref[...], g, m_ref[...], v_ref[...])
        g_out[...] = g
        d_out[...] = d
        m_out[...] = m2
        v_out[...] = v2

    shp = jax.ShapeDtypeStruct(own.shape, F32)
    return pl.pallas_call(
        body, name="adamw_sharded", grid=(rows // tile,),
        out_shape=(shp, shp, shp, shp),
        in_specs=[_flat_spec(tile), _flat_spec(tile, (3,)), _flat_spec(tile), _flat_spec(tile), _flat_spec(tile)],
        out_specs=(_flat_spec(tile),) * 4,
        compiler_params=_cparams("parallel"),
    )(own, got, w, m, v)


def _adamw_small(g, w, m, v):
    def body(g_ref, w_ref, m_ref, v_ref, d_out, m_out, v_out):
        d, m2, v2 = _adamw_math(w_ref[...], g_ref[...], m_ref[...], v_ref[...])
        d_out[...] = d
        m_out[...] = m2
        v_out[...] = v2

    shp = jax.ShapeDtypeStruct(g.shape, F32)
    return pl.pallas_call(body, name="adamw_small", out_shape=(shp, shp, shp))(g, w, m, v)


def _rope_tables(pos_col):
    t = pos_col.shape[0]
    inv = (np.float32(ROPE_THETA) ** (-np.arange(0, ROPE, 2, dtype=np.float32) / np.float32(ROPE))).astype(np.float32)
    freq = np.zeros((1, LANES), np.float32)
    freq[0, NOPE:NOPE + ROPE // 2] = inv
    freq[0, NOPE + ROPE // 2:NOPE + ROPE] = inv
    tm = _row_tile(t)

    def body(p_ref, f_ref, c_ref, s_ref):
        ang = p_ref[...].astype(F32) * f_ref[...]
        c_ref[...] = jnp.cos(ang)
        s_ref[...] = jnp.sin(ang)

    shp = jax.ShapeDtypeStruct((t, LANES), F32)
    return pl.pallas_call(
        body, name="rope_tables", grid=(t // tm,), out_shape=(shp, shp),
        in_specs=[pl.BlockSpec((tm, 1), lambda i: (i, 0)), pl.BlockSpec((1, LANES), lambda i: (0, 0))],
        out_specs=(pl.BlockSpec((tm, LANES), lambda i: (i, 0)),) * 2,
        compiler_params=_cparams("parallel"),
    )(pos_col, jnp.asarray(freq))


def _in_proj(x, g, w_qkv, w_rest):
    t = x.shape[0]
    tm = _row_tile(t)

    def body(x_ref, g_ref, wq_ref, wr_ref, h_ref, fq_ref, fk_ref, fv_ref, r_ref):
        h, _ = _rms(x_ref[...], g_ref[...])
        hb = h.astype(BF16)
        h_ref[...] = hb
        for n, ref in enumerate((fq_ref, fk_ref, fv_ref)):
            ref[...] = _dot(hb, wq_ref[:, n * FOX_WIDTH:(n + 1) * FOX_WIDTH]).astype(BF16)
        r_ref[...] = _dot(hb, wr_ref[...])

    row = lambda n: pl.BlockSpec((tm, n), lambda i: (i, 0))
    full = lambda a: pl.BlockSpec(a.shape, lambda i: (0,) * a.ndim)
    return pl.pallas_call(
        body, name="in_proj", grid=(t // tm,),
        out_shape=(jax.ShapeDtypeStruct((t, D_MODEL), BF16),) + (jax.ShapeDtypeStruct((t, FOX_WIDTH), BF16),) * 3
        + (jax.ShapeDtypeStruct((t, REST_COLS), F32),),
        in_specs=[row(D_MODEL), full(g), full(w_qkv), full(w_rest)],
        out_specs=(row(D_MODEL), row(FOX_WIDTH), row(FOX_WIDTH), row(FOX_WIDTH), row(REST_COLS)),
        compiler_params=_cparams("parallel"),
    )(x, g, w_qkv, w_rest)


def _log_sigmoid(z):
    return jnp.minimum(z, 0.0) - jnp.log(1.0 + jnp.exp(-jnp.abs(z)))


def _split3(v):
    hi = v.astype(BF16)
    r1 = v - hi.astype(F32)
    mid = r1.astype(BF16)
    lo = (r1 - mid.astype(F32)).astype(BF16)
    return hi, mid, lo


def _scan_tile(t):
    return 256 if t >= 256 else t


def _forget_cumsum(rest, b128):
    t = rest.shape[0]
    tb = _scan_tile(t)

    def body(r_ref, b_ref, row_ref, rep_ref, f_sc, carry):
        @pl.when(pl.program_id(0) == 0)
        def _():
            carry[...] = jnp.zeros_like(carry)
        lf = _log_sigmoid(r_ref[...] + b_ref[...])
        tri = (lax.broadcasted_iota(jnp.int32, (tb, tb), 0) >= lax.broadcasted_iota(jnp.int32, (tb, tb), 1)).astype(BF16)
        hi, mid, lo = _split3(lf)
        f_sc[...] = (_dot(tri, hi) + _dot(tri, mid)) + _dot(tri, lo) + carry[...]
        carry[...] = f_sc[tb - 1:tb, :]
        f2 = f_sc[...] * LOG2E
        row_ref[...] = jnp.transpose(f2)[0:HEADS, :]
        lane = _lane()
        for h in range(HEADS):
            col = jnp.sum(jnp.where(lane == h, f2, 0.0), axis=1, keepdims=True)
            rep_ref[h] = jnp.broadcast_to(col, (tb, LANES))

    return pl.pallas_call(
        body, name="forget_cumsum", grid=(t // tb,),
        out_shape=(jax.ShapeDtypeStruct((HEADS, t), F32), jax.ShapeDtypeStruct((HEADS, t, LANES), F32)),
        in_specs=[pl.BlockSpec((tb, LANES), lambda i: (i, 0)), pl.BlockSpec((1, LANES), lambda i: (0, 0))],
        out_specs=(pl.BlockSpec((HEADS, tb), lambda i: (0, i)), pl.BlockSpec((HEADS, tb, LANES), lambda i: (0, i, 0))),
        scratch_shapes=[pltpu.VMEM((tb, LANES), F32), pltpu.VMEM((1, LANES), F32)],
        compiler_params=_cparams("arbitrary"),
    )(rest, b128)


def _forget_bwd(rest, b128, d_fq, d_fk):
    t = rest.shape[0]
    tb = _scan_tile(t)
    nb = t // tb

    def body(r_ref, b_ref, dfq_ref, dfk_ref, dz_ref, db_ref, carry):
        @pl.when(pl.program_id(0) == 0)
        def _():
            carry[...] = jnp.zeros_like(carry)
            db_ref[...] = jnp.zeros_like(db_ref)
        tri = (lax.broadcasted_iota(jnp.int32, (tb, tb), 0) <= lax.broadcasted_iota(jnp.int32, (tb, tb), 1)).astype(BF16)
        lane = _lane()
        df = jnp.zeros((tb, LANES), F32)
        for h in range(HEADS):
            df = df + jnp.where(lane == h, dfq_ref[h] + dfk_ref[h], 0.0)
        hi, mid, lo = _split3(df)
        dlf = (_dot(tri, hi) + _dot(tri, mid)) + _dot(tri, lo) + carry[...]
        z = r_ref[...] + b_ref[...]
        dz = dlf / (1.0 + jnp.exp(z))
        dz_ref[...] = dz
        db_ref[...] += jnp.sum(dz, axis=0, keepdims=True)
        carry[...] = carry[...] + jnp.sum(df, axis=0, keepdims=True)

    rev = lambda i: (nb - 1 - i, 0)
    rev3 = pl.BlockSpec((HEADS, tb, LANES), lambda i: (0, nb - 1 - i, 0))
    return pl.pallas_call(
        body, name="forget_bwd", grid=(nb,),
        out_shape=(jax.ShapeDtypeStruct((t, LANES), F32), jax.ShapeDtypeStruct((1, LANES), F32)),
        in_specs=[pl.BlockSpec((tb, LANES), rev), pl.BlockSpec((1, LANES), lambda i: (0, 0)), rev3, rev3],
        out_specs=(pl.BlockSpec((tb, LANES), rev), pl.BlockSpec((1, LANES), lambda i: (0, 0))),
        scratch_shapes=[pltpu.VMEM((1, LANES), F32)],
        compiler_params=_cparams("arbitrary"),
    )(rest, b128, d_fq, d_fk)


def _mla_prep(rest, gq, gkv, wq, wkv, cos, sin):
    t = rest.shape[0]
    tm = _row_tile(t)

    def body(r_ref, gq_ref, gkv_ref, wq_ref, wkv_ref, c_ref, s_ref, q_ref, k_ref, kv_ref, cq_ref, ckv_ref):
        cos_, sin_ = c_ref[...], s_ref[...]
        cq, _ = _rms(r_ref[:, REST_CQ:REST_CKV], gq_ref[...])
        ckv, _ = _rms(r_ref[:, REST_CKV:REST_KR], gkv_ref[...])
        cqb, ckvb = cq.astype(BF16), ckv.astype(BF16)
        cq_ref[...] = cqb
        ckv_ref[...] = ckvb
        k_rope = _rope(r_ref[:, REST_KR:REST_COLS], cos_, sin_)
        lo = _lane() < NOPE
        for h in range(HEADS):
            q_ref[h] = _rope(_dot(cqb, wq_ref[h]), cos_, sin_).astype(BF16)
            kv = _dot(ckvb, wkv_ref[h])
            kv_ref[h] = kv.astype(BF16)
            k_ref[h] = (jnp.where(lo, kv, 0.0) + k_rope).astype(BF16)

    row = lambda n: pl.BlockSpec((tm, n), lambda i: (i, 0))
    full = lambda a: pl.BlockSpec(a.shape, lambda i: (0,) * a.ndim)
    heads = pl.BlockSpec((HEADS, tm, LANES), lambda i: (0, i, 0))
    hshape = jax.ShapeDtypeStruct((HEADS, t, LANES), BF16)
    return pl.pallas_call(
        body, name="mla_prep", grid=(t // tm,),
        out_shape=(hshape, hshape, hshape, jax.ShapeDtypeStruct((t, Q_RANK), BF16), jax.ShapeDtypeStruct((t, KV_RANK), BF16)),
        in_specs=[row(REST_COLS), full(gq), full(gkv), full(wq), full(wkv), row(LANES), row(LANES)],
        out_specs=(heads, heads, heads, row(Q_RANK), row(KV_RANK)),
        compiler_params=_cparams("parallel"),
    )(rest, gq, gkv, wq, wkv, cos, sin)


def _pair_specs(fox, t, tq, blocked_q):
    if fox:
        blk = pl.BlockSpec((tq, LANES), lambda p, i: (i, p))
        whole = pl.BlockSpec((t, LANES), lambda p, i: (0, p))
    else:
        blk = pl.BlockSpec((2, tq, LANES), lambda p, i: (p, i, 0))
        whole = pl.BlockSpec((2, t, LANES), lambda p, i: (p, 0, 0))
    return [blk, whole, whole] if blocked_q else [whole, blk, blk]


def _tile_lanes(x, n):
    return jnp.tile(x, (1, n)) if n > 1 else x


def _pair_heads(fox, q_ref, lo):
    hi = jnp.logical_not(lo)
    if fox:
        zero = jnp.zeros((), BF16)
        return [jnp.where(lo, q_ref[...], zero), jnp.where(hi, q_ref[...], zero)], [hi, lo]
    return [q_ref[0], q_ref[1]], [lo, lo]


def _stat_rows(x):
    return jnp.transpose(x)[0:8, :]


def _attn_fwd(fox, q, k, v, f2_rows=None):
    t = q.shape[0] if fox else q.shape[1]
    tq = _row_tile(t)
    nq = t // tq
    nrep = tq // LANES
    c2 = (FOX_SCALE if fox else MLA_SCALE) * LOG2E

    def body(*refs):
        if fox:
            q_ref, k_ref, v_ref, fr_ref, o_ref, lse_ref, lset_ref, m_sc, acc_sc = refs
        else:
            q_ref, k_ref, v_ref, o_ref, lse_ref, lset_ref, m_sc, acc_sc = refs
        i = pl.program_id(1)
        lo = _lane() < HEAD_DIM
        causal = lax.broadcasted_iota(jnp.int32, (tq, tq), 0) >= lax.broadcasted_iota(jnp.int32, (tq, tq), 1)
        qs, sum_lanes = _pair_heads(fox, q_ref, lo)
        one = jnp.ones((), BF16)
        m_sc[...] = jnp.full_like(m_sc, -jnp.inf)
        acc_sc[...] = jnp.zeros_like(acc_sc)

        def rows_of(ref, j, hh):
            sl = pl.ds(pl.multiple_of(j * tq, tq), tq)
            return ref[sl, :] if fox else ref[hh, sl, :]

        def step(j, masked):
            for hh in range(2):
                s = _dot_nt(qs[hh], rows_of(k_ref, j, hh)) * c2
                if fox:
                    s = s - fr_ref[hh, j]
                if masked:
                    s = jnp.where(causal, s, -jnp.inf)
                m_prev = m_sc[hh]
                m_new = jnp.maximum(m_prev, jnp.max(s, axis=1, keepdims=True))
                p = jnp.exp2(s - _tile_lanes(m_new, nrep))
                vj = jnp.where(sum_lanes[hh], one, rows_of(v_ref, j, hh))
                acc_sc[hh] = jnp.exp2(m_prev - m_new) * acc_sc[hh] + _dot(p.astype(BF16), vj)
                m_sc[hh] = m_new

        def loop_body(j, carry):
            step(j, False)
            return carry

        lax.fori_loop(0, i, loop_body, 0)
        step(i, True)
        outs = []
        for hh in range(2):
            acc = acc_sc[hh]
            swapped = pltpu.roll(acc, HEAD_DIM, 1)
            outs.append(acc / swapped)
            lse2 = m_sc[hh] + jnp.log(jnp.where(sum_lanes[hh], acc, swapped)) * LOG2E
            lse_ref[hh] = lse2
            lset_ref[hh, 0] = _stat_rows(lse2)
        if fox:
            o_ref[...] = jnp.where(lo, outs[0], outs[1])
        else:
            o_ref[...] = jnp.where(lo, pltpu.roll(outs[0], HEAD_DIM, 1), outs[1])

    stat = pl.BlockSpec((2, tq, LANES), lambda p, i: (p, i, 0))
    stat_rows = pl.BlockSpec((2, 1, 8, tq), lambda p, i: (p, i, 0, 0))
    in_specs = _pair_specs(fox, t, tq, True)
    args = [q, k, v]
    if fox:
        in_specs += [pl.BlockSpec((2, nq, 1, tq), lambda p, i: (p, 0, 0, 0))]
        args += [f2_rows]
    return pl.pallas_call(
        body, name="fox_attn_fwd" if fox else "mla_attn_fwd", grid=(HEADS // 2, nq),
        out_shape=(jax.ShapeDtypeStruct((t, 4 * LANES), F32), jax.ShapeDtypeStruct((HEADS, t, LANES), F32),
                   jax.ShapeDtypeStruct((HEADS, nq, 8, tq), F32)),
        in_specs=in_specs,
        out_specs=(pl.BlockSpec((tq, LANES), lambda p, i: (i, p)), stat, stat_rows),
        scratch_shapes=[pltpu.VMEM((2, tq, LANES), F32), pltpu.VMEM((2, tq, LANES), F32)],
        compiler_params=_cparams("parallel", "arbitrary"),
    )(*args)


def _head_do(fox, hh, do2, lo):
    if fox:
        return jnp.where(lo if hh == 0 else jnp.logical_not(lo), do2, 0.0)
    return jnp.where(lo, 0.0, pltpu.roll(do2, HEAD_DIM, 1) if hh == 0 else do2)


def _attn_bwd_dq(fox, q, k, v, do, o, lse, f2_rows=None):
    t = q.shape[0] if fox else q.shape[1]
    tq = _row_tile(t)
    nq = t // tq
    nrep = tq // LANES
    scale = FOX_SCALE if fox else MLA_SCALE
    c2 = scale * LOG2E

    def body(*refs):
        if fox:
            q_ref, k_ref, v_ref, fr_ref, do_ref, o_ref, lse_ref, dq_ref, dlt_ref, df_ref, acc_sc = refs
        else:
            q_ref, k_ref, v_ref, do_ref, o_ref, lse_ref, dq_ref, dlt_ref, acc_sc = refs
        i = pl.program_id(1)
        lo = _lane() < HEAD_DIM
        causal = lax.broadcasted_iota(jnp.int32, (tq, tq), 0) >= lax.broadcasted_iota(jnp.int32, (tq, tq), 1)
        qs, sum_lanes = _pair_heads(fox, q_ref, lo)
        one = jnp.ones((), BF16)
        do2 = do_ref[...]
        prod = do2 * o_ref[...]
        dobs, deltas = [], []
        for hh in range(2):
            hmask = lo if hh == 0 else jnp.logical_not(lo)
            delta = jnp.broadcast_to(jnp.sum(jnp.where(hmask, prod, 0.0), axis=1, keepdims=True), (tq, LANES))
            dlt_ref[hh, 0] = _stat_rows(delta)
            deltas.append(delta)
            dobs.append(_head_do(fox, hh, do2, lo).astype(BF16))
        acc_sc[...] = jnp.zeros_like(acc_sc)

        def rows_of(ref, j, hh):
            sl = pl.ds(pl.multiple_of(j * tq, tq), tq)
            return ref[sl, :] if fox else ref[hh, sl, :]

        def step(j, masked):
            for hh in range(2):
                kj = rows_of(k_ref, j, hh)
                s = _dot_nt(qs[hh], kj) * c2
                if fox:
                    s = s - fr_ref[hh, j]
                if masked:
                    s = jnp.where(causal, s, -jnp.inf)
                p = jnp.exp2(s - _tile_lanes(lse_ref[hh], nrep))
                dp = _dot_nt(dobs[hh], rows_of(v_ref, j, hh))
                ds = p * (dp - _tile_lanes(deltas[hh], nrep))
                if fox:
                    kj = jnp.where(sum_lanes[hh], one, kj)
                acc_sc[hh] += _dot(ds.astype(BF16), kj)

        def loop_body(j, carry):
            step(j, False)
            return carry

        lax.fori_loop(0, i, loop_body, 0)
        step(i, True)
        if fox:
            dq_ref[...] = (jnp.where(lo, acc_sc[0], acc_sc[1]) * scale).astype(BF16)
            for hh in range(2):
                acc = acc_sc[hh]
                df_ref[hh] = jnp.where(sum_lanes[hh], acc, pltpu.roll(acc, HEAD_DIM, 1))
        else:
            dq_ref[0] = acc_sc[0] * scale
            dq_ref[1] = acc_sc[1] * scale

    stat = pl.BlockSpec((2, tq, LANES), lambda p, i: (p, i, 0))
    stat_rows = pl.BlockSpec((2, 1, 8, tq), lambda p, i: (p, i, 0, 0))
    pair = pl.BlockSpec((tq, LANES), lambda p, i: (i, p))
    in_specs = _pair_specs(fox, t, tq, True)
    args = [q, k, v]
    if fox:
        in_specs += [pl.BlockSpec((2, nq, 1, tq), lambda p, i: (p, 0, 0, 0))]
        args += [f2_rows]
    in_specs += [pair, pair, stat]
    args += [do, o, lse]
    rows_shape = jax.ShapeDtypeStruct((HEADS, nq, 8, tq), F32)
    if fox:
        out_shape = (jax.ShapeDtypeStruct((t, 4 * LANES), BF16), rows_shape, jax.ShapeDtypeStruct((HEADS, t, LANES), F32))
        out_specs = (pair, stat_rows, stat)
    else:
        out_shape = (jax.ShapeDtypeStruct((HEADS, t, LANES), F32), rows_shape)
        out_specs = (stat, stat_rows)
    return pl.pallas_call(
        body, name="fox_attn_bwd_dq" if fox else "mla_attn_bwd_dq", grid=(HEADS // 2, nq),
        out_shape=out_shape, in_specs=in_specs, out_specs=out_specs,
        scratch_shapes=[pltpu.VMEM((2, tq, LANES), F32)],
        compiler_params=_cparams("parallel", "arbitrary"),
    )(*args)


def _attn_bwd_dkv(fox, q, k, v, do, lse_rows, delta_rows, f2_rep=None):
    t = q.shape[0] if fox else q.shape[1]
    tq = _row_tile(t)
    nq = t // tq
    nrep = tq // LANES
    scale = FOX_SCALE if fox else MLA_SCALE
    c2 = scale * LOG2E

    def body(*refs):
        if fox:
            q_ref, k_ref, v_ref, f_ref, do_ref, lse_ref, dl_ref, dk_ref, dv_ref, df_ref, dk_sc, dv_sc = refs
        else:
            q_ref, k_ref, v_ref, do_ref, lse_ref, dl_ref, dkv_ref, dkr_ref, dk_sc, dv_sc = refs
        j = pl.program_id(1)
        lane = _lane()
        lo = lane < HEAD_DIM
        hi = jnp.logical_not(lo)
        causal = lax.broadcasted_iota(jnp.int32, (tq, tq), 1) >= lax.broadcasted_iota(jnp.int32, (tq, tq), 0)
        zero, one = jnp.zeros((), BF16), jnp.ones((), BF16)
        dk_sc[...] = jnp.zeros_like(dk_sc)
        dv_sc[...] = jnp.zeros_like(dv_sc)

        def step(i, masked):
            sl = pl.ds(pl.multiple_of(i * tq, tq), tq)
            do_i = do_ref[sl, :]
            for hh in range(2):
                kj = k_ref[...] if fox else k_ref[hh]
                vj = v_ref[...] if fox else v_ref[hh]
                qi = jnp.where(lo if hh == 0 else hi, q_ref[sl, :], zero) if fox else q_ref[hh, sl, :]
                dob = _head_do(fox, hh, do_i, lo).astype(BF16)
                st = _dot_nt(kj, qi) * c2
                if fox:
                    st = st - _tile_lanes(f_ref[hh], nrep)
                if masked:
                    st = jnp.where(causal, st, -jnp.inf)
                pt = jnp.exp2(st - lse_ref[hh, i, 0:1, :])
                dpt = _dot_nt(vj, dob)
                dst = pt * (dpt - dl_ref[hh, i, 0:1, :])
                dv_sc[hh] += _dot(pt.astype(BF16), dob)
                if fox:
                    qi = jnp.where(hi if hh == 0 else lo, one, qi)
                dk_sc[hh] += _dot(dst.astype(BF16), qi)

        def loop_body(i, carry):
            step(i, False)
            return carry

        step(j, True)
        lax.fori_loop(j + 1, nq, loop_body, 0)
        if fox:
            dk_ref[...] = (jnp.where(lo, dk_sc[0], dk_sc[1]) * scale).astype(BF16)
            dv_ref[...] = (dv_sc[0] + dv_sc[1]).astype(BF16)
            for hh in range(2):
                dk = dk_sc[hh]
                df_ref[hh] = -jnp.where(hi if hh == 0 else lo, dk, pltpu.roll(dk, HEAD_DIM, 1))
        else:
            rope_lanes = jnp.logical_and(lane >= NOPE, lane < NOPE + ROPE)
            dkr = jnp.zeros((tq, LANES), F32)
            for hh in range(2):
                dk = dk_sc[hh] * scale
                dkv_ref[hh] = jnp.where(lo, dk, dv_sc[hh])
                dkr = dkr + jnp.where(rope_lanes, dk, 0.0)
            dkr_ref[0] = dkr

    stat = pl.BlockSpec((2, tq, LANES), lambda p, j: (p, j, 0))
    rows4 = pl.BlockSpec((2, nq, 8, tq), lambda p, j: (p, 0, 0, 0))
    pair = pl.BlockSpec((tq, LANES), lambda p, j: (j, p))
    in_specs = _pair_specs(fox, t, tq, False)
    args = [q, k, v]
    if fox:
        in_specs += [stat]
        args += [f2_rep]
    in_specs += [pl.BlockSpec((t, LANES), lambda p, j: (0, p)), rows4, rows4]
    args += [do, lse_rows, delta_rows]
    if fox:
        out_shape = (jax.ShapeDtypeStruct((t, 4 * LANES), BF16), jax.ShapeDtypeStruct((t, 4 * LANES), BF16),
                     jax.ShapeDtypeStruct((HEADS, t, LANES), F32))
        out_specs = (pair, pair, stat)
    else:
        out_shape = (jax.ShapeDtypeStruct((HEADS, t, LANES), F32), jax.ShapeDtypeStruct((HEADS // 2, t, LANES), F32))
        out_specs = (stat, pl.BlockSpec((1, tq, LANES), lambda p, j: (p, j, 0)))
    return pl.pallas_call(
        body, name="fox_attn_bwd_dkv" if fox else "mla_attn_bwd_dkv", grid=(HEADS // 2, nq),
        out_shape=out_shape, in_specs=in_specs, out_specs=out_specs,
        scratch_shapes=[pltpu.VMEM((2, tq, LANES), F32), pltpu.VMEM((2, tq, LANES), F32)],
        compiler_params=_cparams("parallel", "arbitrary"),
    )(*args)


def _attn_out(x, fox_o, mla_o, gf, gm, w_o):
    t = x.shape[0]
    tm = _row_tile(t)

    def body(x_ref, f_ref, m_ref, gf_ref, gm_ref, w_ref, x1_ref, mix_ref):
        nf, _ = _rms(f_ref[...], gf_ref[...])
        nm, _ = _rms(m_ref[...], gm_ref[...])
        nfb, nmb = nf.astype(BF16), nm.astype(BF16)
        mix_ref[:, :FOX_WIDTH] = nfb
        mix_ref[:, FOX_WIDTH:] = nmb
        x1_ref[...] = x_ref[...] + _dot(nfb, w_ref[:FOX_WIDTH, :]) + _dot(nmb, w_ref[FOX_WIDTH:, :])

    row = lambda n: pl.BlockSpec((tm, n), lambda i: (i, 0))
    full = lambda a: pl.BlockSpec(a.shape, lambda i: (0,) * a.ndim)
    return pl.pallas_call(
        body, name="attn_out", grid=(t // tm,),
        out_shape=(jax.ShapeDtypeStruct((t, D_MODEL), F32), jax.ShapeDtypeStruct((t, D_MODEL), BF16)),
        in_specs=[row(D_MODEL), row(FOX_WIDTH), row(MLA_WIDTH), full(gf), full(gm), full(w_o)],
        out_specs=(row(D_MODEL), row(D_MODEL)),
        compiler_params=_cparams("parallel"),
    )(x, fox_o, mla_o, gf, gm, w_o)


def _mlp_fwd(x1, g_mlp, w_up, w_down, g_fin, target):
    t = x1.shape[0]
    tm = _row_tile(t)
    nf = w_up.shape[0]
    tf = w_up.shape[2]

    def body(x_ref, g_ref, wu_ref, wd_ref, gf_ref, t_ref, u_ref, h_ref, dx_ref, loss_ref, dg_ref, acc_sc, h_sc):
        i, f = pl.program_id(0), pl.program_id(1)

        @pl.when(jnp.logical_and(i == 0, f == 0))
        def _():
            loss_ref[...] = jnp.zeros_like(loss_ref)
            dg_ref[...] = jnp.zeros_like(dg_ref)

        @pl.when(f == 0)
        def _():
            h, _ = _rms(x_ref[...], g_ref[...])
            h_sc[...] = h.astype(BF16)
            h_ref[...] = h_sc[...]
            acc_sc[...] = jnp.zeros_like(acc_sc)

        u = _dot(h_sc[...], wu_ref[...])
        u_ref[...] = u
        r = jnp.maximum(u, 0.0)
        acc_sc[...] += _dot((r * r).astype(BF16), wd_ref[...])

        @pl.when(f == nf - 1)
        def _():
            x2 = x_ref[...] + acc_sc[...]
            y, r2 = _rms(x2, gf_ref[...])
            err = y - t_ref[...]
            loss_ref[...] += 0.5 * jnp.sum(jnp.mean(err * err, axis=-1, keepdims=True))
            dx, dg = _rms_bwd(x2, gf_ref[...], r2, err * (1.0 / D_MODEL))
            dx_ref[...] = dx
            dg_ref[...] += dg

    row = lambda n: pl.BlockSpec((tm, n), lambda i, f: (i, 0))
    vec = pl.BlockSpec((1, D_MODEL), lambda i, f: (0, 0))
    return pl.pallas_call(
        body, name="mlp_fwd", grid=(t // tm, nf),
        out_shape=(jax.ShapeDtypeStruct((t, D_FF), F32), jax.ShapeDtypeStruct((t, D_MODEL), BF16),
                   jax.ShapeDtypeStruct((t, D_MODEL), F32), jax.ShapeDtypeStruct((8, LANES), F32),
                   jax.ShapeDtypeStruct((1, D_MODEL), F32)),
        in_specs=[row(D_MODEL), vec, pl.BlockSpec((None, D_MODEL, tf), lambda i, f: (f, 0, 0)),
                  pl.BlockSpec((None, tf, D_MODEL), lambda i, f: (f, 0, 0)), vec, row(D_MODEL)],
        out_specs=(pl.BlockSpec((tm, tf), lambda i, f: (i, f)), row(D_MODEL), row(D_MODEL),
                   pl.BlockSpec((8, LANES), lambda i, f: (0, 0)), vec),
        scratch_shapes=[pltpu.VMEM((tm, D_MODEL), F32), pltpu.VMEM((tm, D_MODEL), BF16)],
        compiler_params=_cparams("arbitrary", "arbitrary"),
    )(x1, g_mlp, w_up, w_down, g_fin, target)


def _mlp_bwd(dx2, u, x1, g_mlp, w_up, w_down):
    t = x1.shape[0]
    tm = _row_tile(t)
    nf = w_up.shape[0]
    tf = w_up.shape[2]

    def body(dx_ref, u_ref, x_ref, g_ref, wu_ref, wd_ref, du_ref, a_ref, dx1_ref, dg_ref, acc_sc):
        i, f = pl.program_id(0), pl.program_id(1)

        @pl.when(jnp.logical_and(i == 0, f == 0))
        def _():
            dg_ref[...] = jnp.zeros_like(dg_ref)

        @pl.when(f == 0)
        def _():
            acc_sc[...] = jnp.zeros_like(acc_sc)

        r = jnp.maximum(u_ref[...], 0.0)
        a_ref[...] = (r * r).astype(BF16)
        da = _dot_nt(dx_ref[...].astype(BF16), wd_ref[...])
        du = (da * (2.0 * r)).astype(BF16)
        du_ref[...] = du
        acc_sc[...] += _dot_nt(du, wu_ref[...])

        @pl.when(f == nf - 1)
        def _():
            x = x_ref[...]
            _, r1 = _rms(x, g_ref[...])
            dx, dg = _rms_bwd(x, g_ref[...], r1, acc_sc[...])
            dx1_ref[...] = dx_ref[...] + dx
            dg_ref[...] += dg

    row = lambda n: pl.BlockSpec((tm, n), lambda i, f: (i, 0))
    vec = pl.BlockSpec((1, D_MODEL), lambda i, f: (0, 0))
    blk = pl.BlockSpec((tm, tf), lambda i, f: (i, f))
    return pl.pallas_call(
        body, name="mlp_bwd", grid=(t // tm, nf),
        out_shape=(jax.ShapeDtypeStruct((t, D_FF), BF16), jax.ShapeDtypeStruct((t, D_FF), BF16),
                   jax.ShapeDtypeStruct((t, D_MODEL), F32), jax.ShapeDtypeStruct((1, D_MODEL), F32)),
        in_specs=[row(D_MODEL), blk, row(D_MODEL), vec, pl.BlockSpec((None, D_MODEL, tf), lambda i, f: (f, 0, 0)),
                  pl.BlockSpec((None, tf, D_MODEL), lambda i, f: (f, 0, 0))],
        out_specs=(blk, blk, row(D_MODEL), vec),
        scratch_shapes=[pltpu.VMEM((tm, D_MODEL), F32)],
        compiler_params=_cparams("arbitrary", "arbitrary"),
    )(dx2, u, x1, g_mlp, w_up, w_down)


def _matmul_tn(name, a, b, batch_out=None):
    batched = b.ndim == 3
    t, m = a.shape[-2:]
    n = b.shape[-1]
    tk = _row_tile(t)
    bm = min(m, 512)
    bn = (n if n <= 1024 else 512) if batch_out is None else n // batch_out
    lead = b.shape[0] if batched else 1

    def body(a_ref, b_ref, o_ref, acc_sc):
        kk = pl.program_id(3)

        @pl.when(kk == 0)
        def _():
            acc_sc[...] = jnp.zeros_like(acc_sc)

        acc_sc[...] += _dot_tn(a_ref[...].astype(BF16), b_ref[...].astype(BF16))

        @pl.when(kk == pl.num_programs(3) - 1)
        def _():
            o_ref[...] = acc_sc[...]

    if batched:
        a_spec = pl.BlockSpec((tk, bm), lambda h, i, j, kk: (kk, i))
        b_spec = pl.BlockSpec((None, tk, bn), lambda h, i, j, kk: (h, kk, j))
        o_spec = pl.BlockSpec((None, bm, bn), lambda h, i, j, kk: (h, i, j))
        o_shape = (lead, m, n)
    else:
        a_spec = pl.BlockSpec((tk, bm), lambda h, i, j, kk: (kk, i))
        b_spec = pl.BlockSpec((tk, bn), lambda h, i, j, kk: (kk, j))
        if batch_out is None:
            o_spec = pl.BlockSpec((bm, bn), lambda h, i, j, kk: (i, j))
            o_shape = (m, n)
        else:
            o_spec = pl.BlockSpec((None, bm, bn), lambda h, i, j, kk: (j, i, 0))
            o_shape = (batch_out, m, bn)
    return pl.pallas_call(
        body, name=name, grid=(lead, m // bm, n // bn, t // tk),
        out_shape=jax.ShapeDtypeStruct(o_shape, F32),
        in_specs=[a_spec, b_spec], out_specs=o_spec,
        scratch_shapes=[pltpu.VMEM((bm, bn), F32)],
        compiler_params=_cparams("parallel", "parallel", "parallel", "arbitrary"),
    )(a, b)


def _attn_out_bwd(dx1, fox_o, mla_o, gf, gm, w_o):
    t = dx1.shape[0]
    tm = _row_tile(t)

    def body(dx_ref, f_ref, m_ref, gf_ref, gm_ref, w_ref, df_ref, dm_ref, dgf_ref, dgm_ref):
        @pl.when(pl.program_id(0) == 0)
        def _():
            dgf_ref[...] = jnp.zeros_like(dgf_ref)
            dgm_ref[...] = jnp.zeros_like(dgm_ref)
        dxb = dx_ref[...].astype(BF16)
        for o_ref, g_ref, lo_row, d_ref, dg_ref in ((f_ref, gf_ref, 0, df_ref, dgf_ref), (m_ref, gm_ref, FOX_WIDTH, dm_ref, dgm_ref)):
            dn = _dot_nt(dxb, w_ref[lo_row:lo_row + FOX_WIDTH, :])
            o = o_ref[...]
            _, r = _rms(o, g_ref[...])
            d, dg = _rms_bwd(o, g_ref[...], r, dn)
            d_ref[...] = d
            dg_ref[...] += dg

    row = lambda n: pl.BlockSpec((tm, n), lambda i: (i, 0))
    full = lambda a: pl.BlockSpec(a.shape, lambda i: (0,) * a.ndim)
    vec = pl.BlockSpec((1, FOX_WIDTH), lambda i: (0, 0))
    o_shape = jax.ShapeDtypeStruct((t, FOX_WIDTH), F32)
    g_shape = jax.ShapeDtypeStruct((1, FOX_WIDTH), F32)
    return pl.pallas_call(
        body, name="attn_out_bwd", grid=(t // tm,),
        out_shape=(o_shape, o_shape, g_shape, g_shape),
        in_specs=[row(D_MODEL), row(FOX_WIDTH), row(MLA_WIDTH), full(gf), full(gm), full(w_o)],
        out_specs=(row(FOX_WIDTH), row(MLA_WIDTH), vec, vec),
        compiler_params=_cparams("arbitrary"),
    )(dx1, fox_o, mla_o, gf, gm, w_o)


def _mla_prep_bwd(dq, dkv, dkr, dz, rest, gq, gkv, wq, wkv, cos, sin):
    t = rest.shape[0]
    tm = _row_tile(t)

    def body(dq_ref, dkv_ref, dkr_ref, dz_ref, r_ref, gq_ref, gkv_ref, wq_ref, wkv_ref, c_ref, s_ref,
             dr_ref, dqp_ref, dkvb_ref, dgq_ref, dgkv_ref):
        @pl.when(pl.program_id(0) == 0)
        def _():
            dgq_ref[...] = jnp.zeros_like(dgq_ref)
            dgkv_ref[...] = jnp.zeros_like(dgkv_ref)
        cos_, sin_ = c_ref[...], s_ref[...]
        dcq = jnp.zeros((tm, Q_RANK), F32)
        dckv = jnp.zeros((tm, KV_RANK), F32)
        for h in range(HEADS):
            dqp = _rope_bwd(dq_ref[h], cos_, sin_).astype(BF16)
            dqp_ref[h] = dqp
            dcq = dcq + _dot_nt(dqp, wq_ref[h])
            dkvb = dkv_ref[h].astype(BF16)
            dkvb_ref[h] = dkvb
            dckv = dckv + _dot_nt(dkvb, wkv_ref[h])
        dkrope = dkr_ref[0]
        for pr in range(1, HEADS // 2):
            dkrope = dkrope + dkr_ref[pr]
        cq = r_ref[:, REST_CQ:REST_CKV]
        _, rq = _rms(cq, gq_ref[...])
        d_cq, dgq = _rms_bwd(cq, gq_ref[...], rq, dcq)
        ckv = r_ref[:, REST_CKV:REST_KR]
        _, rkv = _rms(ckv, gkv_ref[...])
        d_ckv, dgkv = _rms_bwd(ckv, gkv_ref[...], rkv, dckv)
        dgq_ref[...] += dgq
        dgkv_ref[...] += dgkv
        dr_ref[:, 0:REST_CQ] = dz_ref[...].astype(BF16)
        dr_ref[:, REST_CQ:REST_CKV] = d_cq.astype(BF16)
        dr_ref[:, REST_CKV:REST_KR] = d_ckv.astype(BF16)
        dr_ref[:, REST_KR:REST_COLS] = _rope_bwd(dkrope, cos_, sin_).astype(BF16)

    row = lambda n: pl.BlockSpec((tm, n), lambda i: (i, 0))
    full = lambda a: pl.BlockSpec(a.shape, lambda i: (0,) * a.ndim)
    heads = pl.BlockSpec((HEADS, tm, LANES), lambda i: (0, i, 0))
    hshape = jax.ShapeDtypeStruct((HEADS, t, LANES), BF16)
    return pl.pallas_call(
        body, name="mla_prep_bwd", grid=(t // tm,),
        out_shape=(jax.ShapeDtypeStruct((t, REST_COLS), BF16), hshape, hshape,
                   jax.ShapeDtypeStruct((1, Q_RANK), F32), jax.ShapeDtypeStruct((1, KV_RANK), F32)),
        in_specs=[heads, heads, pl.BlockSpec((HEADS // 2, tm, LANES), lambda i: (0, i, 0)), row(LANES), row(REST_COLS),
                  full(gq), full(gkv), full(wq), full(wkv), row(LANES), row(LANES)],
        out_specs=(row(REST_COLS), heads, heads, pl.BlockSpec((1, Q_RANK), lambda i: (0, 0)),
                   pl.BlockSpec((1, KV_RANK), lambda i: (0, 0))),
        compiler_params=_cparams("arbitrary"),
    )(dq, dkv, dkr, dz, rest, gq, gkv, wq, wkv, cos, sin)


def _in_proj_bwd(x, g, dx1, dfq, dfk, dfv, drest, w_qkv, w_rest):
    t = x.shape[0]
    tm = _row_tile(t)

    def body(x_ref, g_ref, dx1_ref, dq_ref, dk_ref, dv_ref, dr_ref, wq_ref, wr_ref, dx_ref, dg_ref):
        @pl.when(pl.program_id(0) == 0)
        def _():
            dg_ref[...] = jnp.zeros_like(dg_ref)
        dh = _dot_nt(dr_ref[...], wr_ref[...])
        for n, ref in enumerate((dq_ref, dk_ref, dv_ref)):
            dh = dh + _dot_nt(ref[...], wq_ref[:, n * FOX_WIDTH:(n + 1) * FOX_WIDTH])
        xv = x_ref[...]
        _, r = _rms(xv, g_ref[...])
        dx, dg = _rms_bwd(xv, g_ref[...], r, dh)
        dx_ref[...] = dx1_ref[...] + dx
        dg_ref[...] += dg

    row = lambda n: pl.BlockSpec((tm, n), lambda i: (i, 0))
    full = lambda a: pl.BlockSpec(a.shape, lambda i: (0,) * a.ndim)
    vec = pl.BlockSpec((1, D_MODEL), lambda i: (0, 0))
    return pl.pallas_call(
        body, name="in_proj_bwd", grid=(t // tm,),
        out_shape=(jax.ShapeDtypeStruct((t, D_MODEL), F32), jax.ShapeDtypeStruct((1, D_MODEL), F32)),
        in_specs=[row(D_MODEL), full(g), row(D_MODEL), row(FOX_WIDTH), row(FOX_WIDTH), row(FOX_WIDTH), row(REST_COLS),
                  full(w_qkv), full(w_rest)],
        out_specs=(row(D_MODEL), vec),
        compiler_params=_cparams("arbitrary"),
    )(x, g, dx1, dfq, dfk, dfv, drest, w_qkv, w_rest)


def _flatten_shards(w_in, w_uq, w_ukv, w_o, w_up, w_down):
    return jnp.concatenate([a.reshape(-1) for a in (w_in, w_uq, w_ukv, w_o, w_up, w_down)]).reshape(FLAT_ROWS, LANES)


def _unflatten_shards(flat, lead=()):
    flat = flat.reshape(lead + (FLAT,))
    shapes = ((D_MODEL, 277), (Q_RANK, QK_DIM), (KV_RANK, 128), (128, D_MODEL), (D_MODEL, 512), (512, D_MODEL))
    out, off = [], 0
    for size, shp in zip(SHARD_SIZES, shapes):
        out.append(flat[..., off:off + size].reshape(lead + shp))
        off += size
    return out


def _pad_cols(a, n):
    return jnp.pad(a, ((0, 0),) * (a.ndim - 1) + ((0, n - a.shape[-1]),))


def _pack_small(parts):
    flat = jnp.concatenate([p.reshape(-1) for p in parts])
    return jnp.pad(flat, (0, SMALL_ROWS * LANES - flat.shape[0])).reshape(SMALL_ROWS, LANES)


def _unpack_small(vec, shapes):
    flat = vec.reshape(-1)
    out, off = [], 0
    for size, shp in zip(SMALL_SIZES, shapes):
        out.append(flat[off:off + size].reshape(shp))
        off += size
    return out


def kernel(x, positions, attn_norm_g, w_in, b_forget, q_norm_g, w_uq, kv_norm_g, w_ukv, fox_out_g, mla_out_g, w_o, mlp_norm_g, w_up, w_down, final_norm_g, loss_target, m_attn_norm_g, m_w_in, m_b_forget, m_q_norm_g, m_w_uq, m_kv_norm_g, m_w_ukv, m_fox_out_g, m_mla_out_g, m_w_o, m_mlp_norm_g, m_w_up, m_w_down, m_final_norm_g, v_attn_norm_g, v_w_in, v_b_forget, v_q_norm_g, v_w_uq, v_kv_norm_g, v_w_ukv, v_fox_out_g, v_mla_out_g, v_w_o, v_mlp_norm_g, v_w_up, v_w_down, v_final_norm_g):
    t = x.shape[1]
    tq = _row_tile(t)
    xs = x[0]
    target = loss_target[0]
    me = _my_index()

    w_flat = _flatten_shards(w_in[0], w_uq[0], w_ukv[0], w_o[0], w_up[0], w_down[0])
    gathered = _all_gather(w_flat.astype(BF16))
    g_in, g_uq, g_ukv, g_o, g_up, g_down = _unflatten_shards(gathered, (N_DEV,))
    win = jnp.transpose(g_in, (1, 0, 2)).reshape(D_MODEL, IN_COLS)
    w_qkv = win[:, :3 * FOX_WIDTH]
    w_rest = jnp.concatenate([
        _pad_cols(win[:, 1536:1544], LANES), win[:, 1544:2184],
        jnp.pad(win[:, 2184:2216], ((0, 0), (NOPE, LANES - NOPE - ROPE)))], axis=1)
    wq = _pad_cols(g_uq, LANES)
    wkv = g_ukv
    wo = g_o.reshape(D_MODEL, D_MODEL)

    cos, sin = _rope_tables(positions.reshape(t, 1))
    h1, fq, fk, fv, rest = _in_proj(xs, attn_norm_g, w_qkv, w_rest)
    b128 = _pad_cols(b_forget, LANES)
    f2_rows, f2_rep = _forget_cumsum(rest, b128)
    f2_rows = f2_rows.reshape(HEADS, t // tq, 1, tq)
    fox_o, fox_lse, fox_lse_rows = _attn_fwd(True, fq, fk, fv, f2_rows)
    mq, mk, mkv, cqn, ckvn = _mla_prep(rest, q_norm_g, kv_norm_g, wq, wkv, cos, sin)
    mla_o, mla_lse, mla_lse_rows = _attn_fwd(False, mq, mk, mkv)
    x1, mixed = _attn_out(xs, fox_o, mla_o, fox_out_g, mla_out_g, wo)
    u, h2, dx2, loss8, d_gfin = _mlp_fwd(x1, mlp_norm_g, g_up, g_down, final_norm_g.reshape(1, D_MODEL), target)

    du, act, dx1, d_gmlp = _mlp_bwd(dx2, u, x1, mlp_norm_g, g_up, g_down)
    dw_down = _matmul_tn("dw_down", act, dx2)
    dw_up = _matmul_tn("dw_up", h2, du, batch_out=N_DEV)
    dfox_o, dmla_o, d_gfox, d_gmla = _attn_out_bwd(dx1, fox_o, mla_o, fox_out_g, mla_out_g, wo)
    dw_o = _matmul_tn("dw_o", mixed, dx1)

    dfq, fox_delta_rows, d_fq = _attn_bwd_dq(True, fq, fk, fv, dfox_o, fox_o, fox_lse, f2_rows)
    dfk, dfv, d_fk = _attn_bwd_dkv(True, fq, fk, fv, dfox_o, fox_lse_rows, fox_delta_rows, f2_rep)
    dz, d_b = _forget_bwd(rest, b128, d_fq, d_fk)

    dmq, mla_delta_rows = _attn_bwd_dq(False, mq, mk, mkv, dmla_o, mla_o, mla_lse)
    dmkv, dmkr = _attn_bwd_dkv(False, mq, mk, mkv, dmla_o, mla_lse_rows, mla_delta_rows)
    drest, dqp, dkvb, d_gq, d_gkv = _mla_prep_bwd(dmq, dmkv, dmkr, dz, rest, q_norm_g, kv_norm_g, wq, wkv, cos, sin)
    dw_uq = _matmul_tn("dw_uq", cqn, dqp)
    dw_ukv = _matmul_tn("dw_ukv", ckvn, dkvb)
    grad_x, d_gattn = _in_proj_bwd(xs, attn_norm_g, dx1, dfq, dfk, dfv, drest, w_qkv, w_rest)
    dw_q = _matmul_tn("dw_in_q", h1, dfq)
    dw_k = _matmul_tn("dw_in_k", h1, dfk)
    dw_v = _matmul_tn("dw_in_v", h1, dfv)
    dw_r = _matmul_tn("dw_in_rest", h1, drest)

    dw_in = jnp.concatenate([dw_q, dw_k, dw_v, dw_r[:, 0:HEADS], dw_r[:, REST_CQ:REST_KR],
                             dw_r[:, REST_KR + NOPE:REST_KR + NOPE + ROPE]], axis=1)
    dw_in = jnp.transpose(dw_in.reshape(D_MODEL, N_DEV, 277), (1, 0, 2))
    g_full = jnp.concatenate([
        dw_in.reshape(N_DEV, -1), dw_uq[:, :, :QK_DIM].reshape(N_DEV, -1), dw_ukv.reshape(N_DEV, -1),
        dw_o.reshape(N_DEV, -1), dw_up.reshape(N_DEV, -1), dw_down.reshape(N_DEV, -1)], axis=1)
    g4 = g_full.reshape(4, 2, FLAT_ROWS, LANES)
    my_c = lax.axis_index("c")
    keep = lax.dynamic_index_in_dim(g4, my_c, axis=1, keepdims=False)
    send = lax.dynamic_index_in_dim(g4, 1 - my_c, axis=1, keepdims=False).astype(BF16)
    got = _sibling_exchange(send)
    chip_f32, chip_bf16 = _sibling_sum(keep, got)
    others = _chip_exchange(chip_bf16)
    own = lax.dynamic_index_in_dim(chip_f32, 2 * lax.axis_index("x") + lax.axis_index("y"), axis=0, keepdims=False)
    sharded = (w_in, w_uq, w_ukv, w_o, w_up, w_down)
    moments_m = (m_w_in, m_w_uq, m_w_ukv, m_w_o, m_w_up, m_w_down)
    moments_v = (v_w_in, v_w_uq, v_w_ukv, v_w_o, v_w_up, v_w_down)
    flat3 = [_flatten_shards(*[a[0] for a in group]) for group in (sharded, moments_m, moments_v)]
    big = _adamw_sharded(own, others, *flat3)
    big_g, big_d, big_m, big_v = [[a[None] for a in _unflatten_shards(f)] for f in big]

    small_w = (attn_norm_g, b_forget, q_norm_g, kv_norm_g, fox_out_g, mla_out_g, mlp_norm_g, final_norm_g)
    small_m = (m_attn_norm_g, m_b_forget, m_q_norm_g, m_kv_norm_g, m_fox_out_g, m_mla_out_g, m_mlp_norm_g, m_final_norm_g)
    small_v = (v_attn_norm_g, v_b_forget, v_q_norm_g, v_kv_norm_g, v_fox_out_g, v_mla_out_g, v_mlp_norm_g, v_final_norm_g)
    partial = _pack_small([d_gattn, d_b[:, :HEADS], d_gq, d_gkv, d_gfox, d_gmla, d_gmlp, d_gfin, loss8[0, :1]])
    total = _small_all_reduce(partial)
    loss = total.reshape(-1)[LOSS_SLOT]
    small_g_vec = total * (lax.broadcasted_iota(jnp.int32, total.shape, 0) * LANES
                           + lax.broadcasted_iota(jnp.int32, total.shape, 1) < LOSS_SLOT).astype(F32)
    sd, sm, sv = _adamw_small(small_g_vec, _pack_small(small_w), _pack_small(small_m), _pack_small(small_v))
    shapes = [a.shape for a in small_w]
    s_g, s_d, s_m, s_v = [_unpack_small(vv, shapes) for vv in (small_g_vec, sd, sm, sv)]

    def ordered(small, bigs):
        ga, bf, gq_, gkv_, gfo, gml, gmlp_, gfin_ = small
        bin_, buq, bukv, bo, bup, bdown = bigs
        return [ga, bin_, bf, gq_, buq, gkv_, bukv, gfo, gml, bo, gmlp_, bup, bdown, gfin_]

    return (loss, grad_x[None], *ordered(s_g, big_g), *ordered(s_d, big_d), *ordered(s_m, big_m), *ordered(s_v, big_v))
```

```python
import functools
import math

import numpy as np
import jax
import jax.numpy as jnp
from jax import lax
from jax.experimental import pallas as pl
from jax.experimental.pallas import tpu as pltpu

F32 = jnp.float32
BF16 = jnp.bfloat16
MESH = pl.DeviceIdType.MESH

D_MODEL = 1024
HEADS = 8
HEAD_DIM = 64
FOX_WIDTH = 512
MLA_WIDTH = 512
NOPE = 64
ROPE = 32
QK_DIM = 96
Q_RANK = 384
KV_RANK = 256
D_FF = 4096
IN_COLS = 2216
ROPE_THETA = 10000.0
EPS = 1e-6
FOX_SCALE = 1.0 / math.sqrt(HEAD_DIM)
MLA_SCALE = 1.0 / math.sqrt(QK_DIM)
ADAM_LR = 0.001
ADAM_B1 = 0.9
ADAM_B2 = 0.999
ADAM_EPS = 1e-08
ADAM_WD = 0.01
ADAM_STEP = 10

N_DEV = 8
LANES = 128
REST_COLS = 896
REST_CQ = LANES
REST_CKV = REST_CQ + Q_RANK
REST_KR = REST_CKV + KV_RANK
LOG2E = 1.4426950408889634
VMEM_LIMIT = 56 * 1024 * 1024

IN_SHARD = IN_COLS // N_DEV
SMALL_SIZES = (1024, 8, 384, 256, 512, 512, 1024, 1024)
SMALL_ROWS = 16
LOSS_ROW = len(SMALL_SIZES)


def _cparams(*sem):
    return pltpu.CompilerParams(dimension_semantics=sem or None, vmem_limit_bytes=VMEM_LIMIT)


def _row_tile(t):
    return 512 if t >= 2048 else 128


def _dot(a, b):
    return jnp.dot(a, b, preferred_element_type=F32)


def _dot_nt(a, b):
    return lax.dot_general(a, b, (((1,), (1,)), ((), ())), preferred_element_type=F32)


def _dot_tn(a, b):
    return lax.dot_general(a, b, (((0,), (0,)), ((), ())), preferred_element_type=F32)


def _rms(x, g):
    r = lax.rsqrt(jnp.mean(x * x, axis=-1, keepdims=True) + EPS)
    return x * r * g, r


def _rms_bwd(x, g, r, dy):
    xh = x * r
    gdy = dy * g
    dx = r * (gdy - xh * jnp.mean(gdy * xh, axis=-1, keepdims=True))
    return dx, jnp.sum(dy * xh, axis=0, keepdims=True)


def _lane():
    return lax.broadcasted_iota(jnp.int32, (1, LANES), 1)


def _rot(x):
    lane = _lane()
    half = NOPE + ROPE // 2
    first = jnp.logical_and(lane >= NOPE, lane < half)
    second = jnp.logical_and(lane >= half, lane < NOPE + ROPE)
    return jnp.where(first, -pltpu.roll(x, LANES - ROPE // 2, 1), jnp.where(second, pltpu.roll(x, ROPE // 2, 1), 0.0))


def _rope(x, cos, sin):
    return x * cos + _rot(x) * sin


def _rope_bwd(dy, cos, sin):
    return dy * cos - _rot(dy * sin)


def _remote(src, dst, send_sem, recv_sem, to):
    return pltpu.make_async_remote_copy(src_ref=src, dst_ref=dst, send_sem=send_sem, recv_sem=recv_sem,
                                        device_id=to, device_id_type=MESH)


def _hbm_specs(n):
    return [pl.BlockSpec(memory_space=pl.ANY)] * n


def _all_gather(blocks):
    n = len(blocks)

    def body(*refs):
        x_refs, out_refs = refs[:n], refs[n:2 * n]
        send_sems, recv_sems, local_sems = refs[2 * n:]
        x, y, c = lax.axis_index("x"), lax.axis_index("y"), lax.axis_index("c")
        me, sibling = (x, y, c), (x, y, 1 - c)
        chips = [(1 - x, y), (x, 1 - y), (1 - x, 1 - y)]

        def slot(a, px, py, pc):
            return out_refs[a].at[4 * px + 2 * py + pc]

        def copy(a, k, blk, to, src=None):
            return _remote(slot(a, *blk) if src is None else src, slot(a, *blk),
                           send_sems.at[7 * a + k], recv_sems.at[7 * a + k], to)

        mine = [pltpu.make_async_copy(x_refs[a], slot(a, *me), local_sems.at[a]) for a in range(n)]
        first, passed = [], []
        for a in range(n):
            mine[a].start()
            first.append(copy(a, 0, me, sibling, src=x_refs[a]))
            first += [copy(a, 1 + j, me, (*chip, c), src=x_refs[a]) for j, chip in enumerate(chips)]
        for cp in first:
            cp.start()
        for a in range(n):
            for j, chip in enumerate(chips):
                copy(a, 1 + j, (*chip, c), me).wait_recv()
                passed.append(copy(a, 4 + j, (*chip, c), sibling))
                passed[-1].start()
        for a in range(n):
            copy(a, 0, sibling, me).wait_recv()
            for j, chip in enumerate(chips):
                copy(a, 4 + j, (*chip, 1 - c), me).wait_recv()
        for cp in first + passed:
            cp.wait_send()
        for cp in mine:
            cp.wait()

    return pl.pallas_call(
        body, name="all_gather_weights",
        out_shape=[jax.ShapeDtypeStruct((N_DEV,) + b.shape, b.dtype) for b in blocks],
        in_specs=_hbm_specs(n), out_specs=_hbm_specs(n),
        scratch_shapes=[pltpu.SemaphoreType.DMA((7 * n,)), pltpu.SemaphoreType.DMA((7 * n,)), pltpu.SemaphoreType.DMA((n,))],
    )(*blocks)


def _rs_sibling_exchange(grads):
    n = len(grads)

    def body(*refs):
        g_refs, land = refs[:n], refs[n:2 * n]
        send_sems, recv_sems = refs[2 * n:]
        x, y, c = lax.axis_index("x"), lax.axis_index("y"), lax.axis_index("c")
        cps = [_remote(g_refs[a].at[2 * q + 1 - c], land[a].at[q], send_sems.at[4 * a + q], recv_sems.at[4 * a + q], (x, y, 1 - c))
               for a in range(n) for q in range(4)]
        for cp in cps:
            cp.start()
        for cp in cps:
            cp.wait()

    return pl.pallas_call(
        body, name="rs_sibling_exchange",
        out_shape=[jax.ShapeDtypeStruct((4,) + g.shape[1:], g.dtype) for g in grads],
        in_specs=_hbm_specs(n), out_specs=_hbm_specs(n),
        scratch_shapes=[pltpu.SemaphoreType.DMA((4 * n,)), pltpu.SemaphoreType.DMA((4 * n,))],
    )(*grads)


def _rs_chip_exchange(parts):
    n = len(parts)

    def body(*refs):
        p_refs, land = refs[:n], refs[n:2 * n]
        send_sems, recv_sems = refs[2 * n:]
        x, y, c = lax.axis_index("x"), lax.axis_index("y"), lax.axis_index("c")
        chips = [(1 - x, y), (x, 1 - y), (1 - x, 1 - y)]
        cps = [_remote(p_refs[a].at[2 * cx + cy], land[a].at[k], send_sems.at[3 * a + k], recv_sems.at[3 * a + k], (cx, cy, c))
               for a in range(n) for k, (cx, cy) in enumerate(chips)]
        for cp in cps:
            cp.start()
        for cp in cps:
            cp.wait()

    return pl.pallas_call(
        body, name="rs_chip_exchange",
        out_shape=[jax.ShapeDtypeStruct((3,) + p.shape[1:], p.dtype) for p in parts],
        in_specs=_hbm_specs(n), out_specs=_hbm_specs(n),
        scratch_shapes=[pltpu.SemaphoreType.DMA((3 * n,)), pltpu.SemaphoreType.DMA((3 * n,))],
    )(*parts)


def _small_all_reduce(parts, loss8):
    n = len(parts)

    def body(*refs):
        p_refs, loss_ref, out_ref, pack, land, send_sems, recv_sems = refs[:n], *refs[n:]
        x, y, c = lax.axis_index("x"), lax.axis_index("y"), lax.axis_index("c")
        me = 4 * x + 2 * y + c
        pack[...] = jnp.zeros_like(pack)
        for r, ref in enumerate(p_refs):
            pack[r:r + 1, 0:ref.shape[1]] = ref[...]
        pack[LOSS_ROW:LOSS_ROW + 1, 0:LANES] = loss_ref[0:1, :]
        land[me] = pack[...]
        cps = []
        for k in range(1, N_DEV):
            peer = (x ^ (k >> 2), y ^ ((k >> 1) & 1), c ^ (k & 1))
            cps.append(_remote(pack, land.at[me], send_sems.at[k - 1], recv_sems.at[k - 1], peer))
        for cp in cps:
            cp.start()
        for cp in cps:
            cp.wait()
        acc = land[0]
        for d in range(1, N_DEV):
            acc = acc + land[d]
        out_ref[...] = acc

    vmem = pl.BlockSpec(memory_space=pltpu.VMEM)
    return pl.pallas_call(
        body, name="small_all_reduce",
        out_shape=jax.ShapeDtypeStruct((SMALL_ROWS, D_MODEL), F32),
        in_specs=[vmem] * (n + 1), out_specs=vmem,
        scratch_shapes=[pltpu.VMEM((SMALL_ROWS, D_MODEL), F32), pltpu.VMEM((N_DEV, SMALL_ROWS, D_MODEL), F32),
                        pltpu.SemaphoreType.DMA((N_DEV - 1,)), pltpu.SemaphoreType.DMA((N_DEV - 1,))],
    )(*parts, loss8)


def _rs_sibling_sum(name, grad, got, core):
    _, rows, cols = grad.shape

    def body(c_ref, g_ref, l_ref, f_ref, b_ref):
        s = g_ref[...] + l_ref[...]
        f_ref[...] = s
        b_ref[...] = s.astype(BF16)

    by_chip = pl.BlockSpec((None, rows, cols), lambda q, c_ref: (q, 0, 0))
    return pl.pallas_call(
        body, name=name,
        grid_spec=pltpu.PrefetchScalarGridSpec(
            num_scalar_prefetch=1, grid=(4,),
            in_specs=[pl.BlockSpec((None, rows, cols), lambda q, c_ref: (2 * q + c_ref[0], 0, 0)), by_chip],
            out_specs=[by_chip, by_chip]),
        out_shape=(jax.ShapeDtypeStruct((4, rows, cols), F32), jax.ShapeDtypeStruct((4, rows, cols), BF16)),
        compiler_params=_cparams("parallel"),
    )(core, grad, got)


def _adamw_math(w, g, m, v):
    m2 = ADAM_B1 * m + (1.0 - ADAM_B1) * g
    v2 = ADAM_B2 * v + (1.0 - ADAM_B2) * (g * g)
    m_hat = m2 / (1.0 - ADAM_B1 ** ADAM_STEP)
    v_hat = v2 / (1.0 - ADAM_B2 ** ADAM_STEP)
    delta = -ADAM_LR * (m_hat / (jnp.sqrt(v_hat) + ADAM_EPS) + ADAM_WD * w)
    return delta, m2, v2


def _update_tile(rows):
    return 256 if rows % 256 == 0 else rows


def _rs_final_sum(name, chip_sums, got, chip):
    _, rows, cols = chip_sums.shape

    def body(q_ref, o_ref, r_ref, g_out):
        g = o_ref[...]
        for k in range(3):
            g = g + r_ref[k].astype(F32)
        g_out[...] = g

    return pl.pallas_call(
        body, name=name,
        grid_spec=pltpu.PrefetchScalarGridSpec(
            num_scalar_prefetch=1, grid=(1,),
            in_specs=[pl.BlockSpec((None, rows, cols), lambda i, q_ref: (q_ref[0], 0, 0)),
                      pl.BlockSpec((3, rows, cols), lambda i, q_ref: (0, 0, 0))],
            out_specs=pl.BlockSpec((rows, cols), lambda i, q_ref: (0, 0))),
        out_shape=jax.ShapeDtypeStruct((rows, cols), F32),
    )(chip, chip_sums, got)


def _adamw_sharded(name, w, m, v, chip_sums, got, chip):
    _, rows, cols = w.shape
    tr = _update_tile(rows)

    def body(q_ref, o_ref, r_ref, w_ref, m_ref, v_ref, g_out, d_out, m_out, v_out):
        g = o_ref[:, 0:cols]
        for k in range(3):
            g = g + r_ref[k, :, 0:cols].astype(F32)
        d, m2, v2 = _adamw_math(w_ref[0], g, m_ref[0], v_ref[0])
        g_out[0] = g
        d_out[0] = d
        m_out[0] = m2
        v_out[0] = v2

    own = pl.BlockSpec((1, tr, cols), lambda i, q_ref: (0, i, 0))
    shp = jax.ShapeDtypeStruct(w.shape, F32)
    wide = chip_sums.shape[2]
    return pl.pallas_call(
        body, name=name,
        grid_spec=pltpu.PrefetchScalarGridSpec(
            num_scalar_prefetch=1, grid=(rows // tr,),
            in_specs=[pl.BlockSpec((None, tr, wide), lambda i, q_ref: (q_ref[0], i, 0)),
                      pl.BlockSpec((3, tr, wide), lambda i, q_ref: (0, i, 0)), own, own, own],
            out_specs=[own] * 4),
        out_shape=(shp,) * 4,
        compiler_params=_cparams("parallel"),
    )(chip, chip_sums, got, w, m, v)


def _adamw_given(name, g, w, m, v):
    _, rows, cols = w.shape
    tr = _update_tile(rows)

    def body(g_ref, w_ref, m_ref, v_ref, g_out, d_out, m_out, v_out):
        g = g_ref[...]
        d, m2, v2 = _adamw_math(w_ref[0], g, m_ref[0], v_ref[0])
        g_out[0] = g
        d_out[0] = d
        m_out[0] = m2
        v_out[0] = v2

    own = pl.BlockSpec((1, tr, cols), lambda i: (0, i, 0))
    shp = jax.ShapeDtypeStruct(w.shape, F32)
    return pl.pallas_call(
        body, name=name, grid=(rows // tr,), out_shape=(shp,) * 4,
        in_specs=[pl.BlockSpec((tr, cols), lambda i: (i, 0)), own, own, own], out_specs=[own] * 4,
        compiler_params=_cparams("parallel"),
    )(g, w, m, v)


def _adamw_small(total, ws, ms, vs):
    n = len(ws)

    def body(*refs):
        t_ref = refs[0]
        w_refs, m_refs, v_refs = refs[1:1 + n], refs[1 + n:1 + 2 * n], refs[1 + 2 * n:1 + 3 * n]
        outs = refs[1 + 3 * n:]
        outs[0][...] = t_ref[LOSS_ROW:LOSS_ROW + 1, 0:1]
        for r in range(n):
            g = t_ref[r:r + 1, 0:w_refs[r].shape[1]]
            d, m2, v2 = _adamw_math(w_refs[r][...], g, m_refs[r][...], v_refs[r][...])
            for k, val in enumerate((g, d, m2, v2)):
                outs[1 + 4 * r + k][...] = val

    vmem = pl.BlockSpec(memory_space=pltpu.VMEM)
    out_shape = [jax.ShapeDtypeStruct((1, 1), F32)]
    for w in ws:
        out_shape += [jax.ShapeDtypeStruct(w.shape, F32)] * 4
    return pl.pallas_call(
        body, name="adamw_small", out_shape=out_shape,
        in_specs=[vmem] * (1 + 3 * n), out_specs=[vmem] * len(out_shape),
    )(total, *ws, *ms, *vs)


def _rope_tables(pos_col):
    t = pos_col.shape[0]
    inv = (np.float32(ROPE_THETA) ** (-np.arange(0, ROPE, 2, dtype=np.float32) / np.float32(ROPE))).astype(np.float32)
    freq = np.zeros((1, LANES), np.float32)
    freq[0, NOPE:NOPE + ROPE // 2] = inv
    freq[0, NOPE + ROPE // 2:NOPE + ROPE] = inv
    tm = _row_tile(t)

    def body(p_ref, f_ref, c_ref, s_ref):
        ang = p_ref[...].astype(F32) * f_ref[...]
        c_ref[...] = jnp.cos(ang)
        s_ref[...] = jnp.sin(ang)

    shp = jax.ShapeDtypeStruct((t, LANES), F32)
    return pl.pallas_call(
        body, name="rope_tables", grid=(t // tm,), out_shape=(shp, shp),
        in_specs=[pl.BlockSpec((tm, 1), lambda i: (i, 0)), pl.BlockSpec((1, LANES), lambda i: (0, 0))],
        out_specs=(pl.BlockSpec((tm, LANES), lambda i: (i, 0)),) * 2,
        compiler_params=_cparams("parallel"),
    )(pos_col, jnp.asarray(freq))


def _in_proj(x, g, w_qkv, w_rest):
    t = x.shape[0]
    tm = _row_tile(t)

    def body(x_ref, g_ref, wq_ref, wr_ref, h_ref, fq_ref, fk_ref, fv_ref, r_ref):
        h, _ = _rms(x_ref[...], g_ref[...])
        hb = h.astype(BF16)
        h_ref[...] = hb
        for n, ref in enumerate((fq_ref, fk_ref, fv_ref)):
            ref[...] = _dot_nt(hb, wq_ref[n * FOX_WIDTH:(n + 1) * FOX_WIDTH, :]).astype(BF16)
        r_ref[...] = _dot_nt(hb, wr_ref[...])

    row = lambda n: pl.BlockSpec((tm, n), lambda i: (i, 0))
    full = lambda a: pl.BlockSpec(a.shape, lambda i: (0,) * a.ndim)
    return pl.pallas_call(
        body, name="in_proj", grid=(t // tm,),
        out_shape=(jax.ShapeDtypeStruct((t, D_MODEL), BF16),) + (jax.ShapeDtypeStruct((t, FOX_WIDTH), BF16),) * 3
        + (jax.ShapeDtypeStruct((t, REST_COLS), F32),),
        in_specs=[row(D_MODEL), full(g), full(w_qkv), full(w_rest)],
        out_specs=(row(D_MODEL), row(FOX_WIDTH), row(FOX_WIDTH), row(FOX_WIDTH), row(REST_COLS)),
        compiler_params=_cparams("parallel"),
    )(x, g, w_qkv, w_rest)


def _log_sigmoid(z):
    return jnp.minimum(z, 0.0) - jnp.log(1.0 + jnp.exp(-jnp.abs(z)))


def _split3(v):
    hi = v.astype(BF16)
    r1 = v - hi.astype(F32)
    mid = r1.astype(BF16)
    lo = (r1 - mid.astype(F32)).astype(BF16)
    return hi, mid, lo


def _scan_tile(t):
    return 256 if t >= 256 else t


def _forget_cumsum(rest, b128):
    t = rest.shape[0]
    tb = _scan_tile(t)

    def body(r_ref, b_ref, row_ref, rep_ref, f_sc, carry):
        @pl.when(pl.program_id(0) == 0)
        def _():
            carry[...] = jnp.zeros_like(carry)
        lf = _log_sigmoid(r_ref[...] + b_ref[...])
        tri = (lax.broadcasted_iota(jnp.int32, (tb, tb), 0) >= lax.broadcasted_iota(jnp.int32, (tb, tb), 1)).astype(BF16)
        hi, mid, lo = _split3(lf)
        f_sc[...] = (_dot(tri, hi) + _dot(tri, mid)) + _dot(tri, lo) + carry[...]
        carry[...] = f_sc[tb - 1:tb, :]
        f2 = f_sc[...] * LOG2E
        row_ref[...] = jnp.transpose(f2)[0:HEADS, :]
        lane = _lane()
        for h in range(HEADS):
            col = jnp.sum(jnp.where(lane == h, f2, 0.0), axis=1, keepdims=True)
            rep_ref[h] = jnp.broadcast_to(col, (tb, LANES))

    return pl.pallas_call(
        body, name="forget_cumsum", grid=(t // tb,),
        out_shape=(jax.ShapeDtypeStruct((HEADS, t), F32), jax.ShapeDtypeStruct((HEADS, t, LANES), F32)),
        in_specs=[pl.BlockSpec((tb, LANES), lambda i: (i, 0)), pl.BlockSpec((1, LANES), lambda i: (0, 0))],
        out_specs=(pl.BlockSpec((HEADS, tb), lambda i: (0, i)), pl.BlockSpec((HEADS, tb, LANES), lambda i: (0, i, 0))),
        scratch_shapes=[pltpu.VMEM((tb, LANES), F32), pltpu.VMEM((1, LANES), F32)],
        compiler_params=_cparams("arbitrary"),
    )(rest, b128)


def _forget_bwd(rest, b128, d_fq, d_fk):
    t = rest.shape[0]
    tb = _scan_tile(t)
    nb = t // tb

    def body(r_ref, b_ref, dfq_ref, dfk_ref, dz_ref, db_ref, carry):
        @pl.when(pl.program_id(0) == 0)
        def _():
            carry[...] = jnp.zeros_like(carry)
            db_ref[...] = jnp.zeros_like(db_ref)
        tri = (lax.broadcasted_iota(jnp.int32, (tb, tb), 0) <= lax.broadcasted_iota(jnp.int32, (tb, tb), 1)).astype(BF16)
        lane = _lane()
        df = jnp.zeros((tb, LANES), F32)
        for h in range(HEADS):
            df = df + jnp.where(lane == h, dfq_ref[h] + dfk_ref[h], 0.0)
        hi, mid, lo = _split3(df)
        dlf = (_dot(tri, hi) + _dot(tri, mid)) + _dot(tri, lo) + carry[...]
        z = r_ref[...] + b_ref[...]
        dz = dlf / (1.0 + jnp.exp(z))
        dz_ref[...] = dz
        db_ref[...] += jnp.sum(dz, axis=0, keepdims=True)
        carry[...] = carry[...] + jnp.sum(df, axis=0, keepdims=True)

    rev = lambda i: (nb - 1 - i, 0)
    rev3 = pl.BlockSpec((HEADS, tb, LANES), lambda i: (0, nb - 1 - i, 0))
    return pl.pallas_call(
        body, name="forget_bwd", grid=(nb,),
        out_shape=(jax.ShapeDtypeStruct((t, LANES), F32), jax.ShapeDtypeStruct((1, LANES), F32)),
        in_specs=[pl.BlockSpec((tb, LANES), rev), pl.BlockSpec((1, LANES), lambda i: (0, 0)), rev3, rev3],
        out_specs=(pl.BlockSpec((tb, LANES), rev), pl.BlockSpec((1, LANES), lambda i: (0, 0))),
        scratch_shapes=[pltpu.VMEM((1, LANES), F32)],
        compiler_params=_cparams("arbitrary"),
    )(rest, b128, d_fq, d_fk)


def _mla_prep(rest, gq, gkv, wq, wkv, cos, sin):
    t = rest.shape[0]
    tm = _row_tile(t)

    def body(r_ref, gq_ref, gkv_ref, wq_ref, wkv_ref, c_ref, s_ref, q_ref, k_ref, kv_ref, cq_ref, ckv_ref):
        cos_, sin_ = c_ref[...], s_ref[...]
        cq, _ = _rms(r_ref[:, REST_CQ:REST_CKV], gq_ref[...])
        ckv, _ = _rms(r_ref[:, REST_CKV:REST_KR], gkv_ref[...])
        cqb, ckvb = cq.astype(BF16), ckv.astype(BF16)
        cq_ref[...] = cqb
        ckv_ref[...] = ckvb
        k_rope = _rope(r_ref[:, REST_KR:REST_COLS], cos_, sin_)
        lo = _lane() < NOPE
        for h in range(HEADS):
            q_ref[h] = _rope(_dot(cqb, wq_ref[h]), cos_, sin_).astype(BF16)
            kv = _dot(ckvb, wkv_ref[h])
            kv_ref[h] = kv.astype(BF16)
            k_ref[h] = (jnp.where(lo, kv, 0.0) + k_rope).astype(BF16)

    row = lambda n: pl.BlockSpec((tm, n), lambda i: (i, 0))
    full = lambda a: pl.BlockSpec(a.shape, lambda i: (0,) * a.ndim)
    heads = pl.BlockSpec((HEADS, tm, LANES), lambda i: (0, i, 0))
    hshape = jax.ShapeDtypeStruct((HEADS, t, LANES), BF16)
    return pl.pallas_call(
        body, name="mla_prep", grid=(t // tm,),
        out_shape=(hshape, hshape, hshape, jax.ShapeDtypeStruct((t, Q_RANK), BF16), jax.ShapeDtypeStruct((t, KV_RANK), BF16)),
        in_specs=[row(REST_COLS), full(gq), full(gkv), full(wq), full(wkv), row(LANES), row(LANES)],
        out_specs=(heads, heads, heads, row(Q_RANK), row(KV_RANK)),
        compiler_params=_cparams("parallel"),
    )(rest, gq, gkv, wq, wkv, cos, sin)


def _pair_specs(fox, t, tq, blocked_q):
    if fox:
        blk = pl.BlockSpec((tq, LANES), lambda p, i: (i, p))
        whole = pl.BlockSpec((t, LANES), lambda p, i: (0, p))
    else:
        blk = pl.BlockSpec((2, tq, LANES), lambda p, i: (p, i, 0))
        whole = pl.BlockSpec((2, t, LANES), lambda p, i: (p, 0, 0))
    return [blk, whole, whole] if blocked_q else [whole, blk, blk]


def _tile_lanes(x, n):
    return jnp.tile(x, (1, n)) if n > 1 else x


def _pair_heads(fox, q_ref, lo):
    hi = jnp.logical_not(lo)
    if fox:
        zero = jnp.zeros((), BF16)
        return [jnp.where(lo, q_ref[...], zero), jnp.where(hi, q_ref[...], zero)], [hi, lo]
    return [q_ref[0], q_ref[1]], [lo, lo]


def _stat_rows(x):
    return jnp.transpose(x)[0:8, :]


def _attn_fwd(fox, q, k, v, f2_rows=None):
    t = q.shape[0] if fox else q.shape[1]
    tq = _row_tile(t)
    nq = t // tq
    nrep = tq // LANES
    c2 = (FOX_SCALE if fox else MLA_SCALE) * LOG2E

    def body(*refs):
        if fox:
            q_ref, k_ref, v_ref, fr_ref, o_ref, lse_ref, lset_ref, m_sc, acc_sc = refs
        else:
            q_ref, k_ref, v_ref, o_ref, lse_ref, lset_ref, m_sc, acc_sc = refs
        i = pl.program_id(1)
        lo = _lane() < HEAD_DIM
        causal = lax.broadcasted_iota(jnp.int32, (tq, tq), 0) >= lax.broadcasted_iota(jnp.int32, (tq, tq), 1)
        qs, sum_lanes = _pair_heads(fox, q_ref, lo)
        one = jnp.ones((), BF16)
        m_sc[...] = jnp.full_like(m_sc, -jnp.inf)
        acc_sc[...] = jnp.zeros_like(acc_sc)

        def rows_of(ref, j, hh):
            sl = pl.ds(pl.multiple_of(j * tq, tq), tq)
            return ref[sl, :] if fox else ref[hh, sl, :]

        def step(j, masked):
            for hh in range(2):
                s = _dot_nt(qs[hh], rows_of(k_ref, j, hh)) * c2
                if fox:
                    s = s - fr_ref[hh, j]
                if masked:
                    s = jnp.where(causal, s, -jnp.inf)
                m_prev = m_sc[hh]
                m_new = jnp.maximum(m_prev, jnp.max(s, axis=1, keepdims=True))
                p = jnp.exp2(s - _tile_lanes(m_new, nrep))
                vj = jnp.where(sum_lanes[hh], one, rows_of(v_ref, j, hh))
                acc_sc[hh] = jnp.exp2(m_prev - m_new) * acc_sc[hh] + _dot(p.astype(BF16), vj)
                m_sc[hh] = m_new

        def loop_body(j, carry):
            step(j, False)
            return carry

        lax.fori_loop(0, i, loop_body, 0)
        step(i, True)
        outs = []
        for hh in range(2):
            acc = acc_sc[hh]
            swapped = pltpu.roll(acc, HEAD_DIM, 1)
            outs.append(acc / swapped)
            lse2 = m_sc[hh] + jnp.log(jnp.where(sum_lanes[hh], acc, swapped)) * LOG2E
            lse_ref[hh] = lse2
            lset_ref[hh, 0] = _stat_rows(lse2)
        if fox:
            o_ref[...] = jnp.where(lo, outs[0], outs[1])
        else:
            o_ref[...] = jnp.where(lo, pltpu.roll(outs[0], HEAD_DIM, 1), outs[1])

    stat = pl.BlockSpec((2, tq, LANES), lambda p, i: (p, i, 0))
    stat_rows = pl.BlockSpec((2, 1, 8, tq), lambda p, i: (p, i, 0, 0))
    in_specs = _pair_specs(fox, t, tq, True)
    args = [q, k, v]
    if fox:
        in_specs += [pl.BlockSpec((2, nq, 1, tq), lambda p, i: (p, 0, 0, 0))]
        args += [f2_rows]
    return pl.pallas_call(
        body, name="fox_attn_fwd" if fox else "mla_attn_fwd", grid=(HEADS // 2, nq),
        out_shape=(jax.ShapeDtypeStruct((t, 4 * LANES), F32), jax.ShapeDtypeStruct((HEADS, t, LANES), F32),
                   jax.ShapeDtypeStruct((HEADS, nq, 8, tq), F32)),
        in_specs=in_specs,
        out_specs=(pl.BlockSpec((tq, LANES), lambda p, i: (i, p)), stat, stat_rows),
        scratch_shapes=[pltpu.VMEM((2, tq, LANES), F32), pltpu.VMEM((2, tq, LANES), F32)],
        compiler_params=_cparams("parallel", "arbitrary"),
    )(*args)


def _head_do(fox, hh, do2, lo):
    if fox:
        return jnp.where(lo if hh == 0 else jnp.logical_not(lo), do2, 0.0)
    return jnp.where(lo, 0.0, pltpu.roll(do2, HEAD_DIM, 1) if hh == 0 else do2)


def _attn_bwd_dq(fox, q, k, v, do, o, lse, f2_rows=None):
    t = q.shape[0] if fox else q.shape[1]
    tq = _row_tile(t)
    nq = t // tq
    nrep = tq // LANES
    scale = FOX_SCALE if fox else MLA_SCALE
    c2 = scale * LOG2E

    def body(*refs):
        if fox:
            q_ref, k_ref, v_ref, fr_ref, do_ref, o_ref, lse_ref, dq_ref, dlt_ref, df_ref, acc_sc = refs
        else:
            q_ref, k_ref, v_ref, do_ref, o_ref, lse_ref, dq_ref, dlt_ref, acc_sc = refs
        i = pl.program_id(1)
        lo = _lane() < HEAD_DIM
        causal = lax.broadcasted_iota(jnp.int32, (tq, tq), 0) >= lax.broadcasted_iota(jnp.int32, (tq, tq), 1)
        qs, sum_lanes = _pair_heads(fox, q_ref, lo)
        one = jnp.ones((), BF16)
        do2 = do_ref[...]
        prod = do2 * o_ref[...]
        dobs, deltas = [], []
        for hh in range(2):
            hmask = lo if hh == 0 else jnp.logical_not(lo)
            delta = jnp.broadcast_to(jnp.sum(jnp.where(hmask, prod, 0.0), axis=1, keepdims=True), (tq, LANES))
            dlt_ref[hh, 0] = _stat_rows(delta)
            deltas.append(delta)
            dobs.append(_head_do(fox, hh, do2, lo).astype(BF16))
        acc_sc[...] = jnp.zeros_like(acc_sc)

        def rows_of(ref, j, hh):
            sl = pl.ds(pl.multiple_of(j * tq, tq), tq)
            return ref[sl, :] if fox else ref[hh, sl, :]

        def step(j, masked):
            for hh in range(2):
                kj = rows_of(k_ref, j, hh)
                s = _dot_nt(qs[hh], kj) * c2
                if fox:
                    s = s - fr_ref[hh, j]
                if masked:
                    s = jnp.where(causal, s, -jnp.inf)
                p = jnp.exp2(s - _tile_lanes(lse_ref[hh], nrep))
                dp = _dot_nt(dobs[hh], rows_of(v_ref, j, hh))
                ds = p * (dp - _tile_lanes(deltas[hh], nrep))
                if fox:
                    kj = jnp.where(sum_lanes[hh], one, kj)
                acc_sc[hh] += _dot(ds.astype(BF16), kj)

        def loop_body(j, carry):
            step(j, False)
            return carry

        lax.fori_loop(0, i, loop_body, 0)
        step(i, True)
        if fox:
            dq_ref[...] = (jnp.where(lo, acc_sc[0], acc_sc[1]) * scale).astype(BF16)
            for hh in range(2):
                acc = acc_sc[hh]
                df_ref[hh] = jnp.where(sum_lanes[hh], acc, pltpu.roll(acc, HEAD_DIM, 1))
        else:
            dq_ref[0] = acc_sc[0] * scale
            dq_ref[1] = acc_sc[1] * scale

    stat = pl.BlockSpec((2, tq, LANES), lambda p, i: (p, i, 0))
    stat_rows = pl.BlockSpec((2, 1, 8, tq), lambda p, i: (p, i, 0, 0))
    pair = pl.BlockSpec((tq, LANES), lambda p, i: (i, p))
    in_specs = _pair_specs(fox, t, tq, True)
    args = [q, k, v]
    if fox:
        in_specs += [pl.BlockSpec((2, nq, 1, tq), lambda p, i: (p, 0, 0, 0))]
        args += [f2_rows]
    in_specs += [pair, pair, stat]
    args += [do, o, lse]
    rows_shape = jax.ShapeDtypeStruct((HEADS, nq, 8, tq), F32)
    if fox:
        out_shape = (jax.ShapeDtypeStruct((t, 4 * LANES), BF16), rows_shape, jax.ShapeDtypeStruct((HEADS, t, LANES), F32))
        out_specs = (pair, stat_rows, stat)
    else:
        out_shape = (jax.ShapeDtypeStruct((HEADS, t, LANES), F32), rows_shape)
        out_specs = (stat, stat_rows)
    return pl.pallas_call(
        body, name="fox_attn_bwd_dq" if fox else "mla_attn_bwd_dq", grid=(HEADS // 2, nq),
        out_shape=out_shape, in_specs=in_specs, out_specs=out_specs,
        scratch_shapes=[pltpu.VMEM((2, tq, LANES), F32)],
        compiler_params=_cparams("parallel", "arbitrary"),
    )(*args)


def _attn_bwd_dkv(fox, q, k, v, do, lse_rows, delta_rows, f2_rep=None):
    t = q.shape[0] if fox else q.shape[1]
    tq = _row_tile(t)
    nq = t // tq
    nrep = tq // LANES
    scale = FOX_SCALE if fox else MLA_SCALE
    c2 = scale * LOG2E

    def body(*refs):
        if fox:
            q_ref, k_ref, v_ref, f_ref, do_ref, lse_ref, dl_ref, dk_ref, dv_ref, df_ref, dk_sc, dv_sc = refs
        else:
            q_ref, k_ref, v_ref, do_ref, lse_ref, dl_ref, dkv_ref, dkr_ref, dk_sc, dv_sc = refs
        j = pl.program_id(1)
        lane = _lane()
        lo = lane < HEAD_DIM
        hi = jnp.logical_not(lo)
        causal = lax.broadcasted_iota(jnp.int32, (tq, tq), 1) >= lax.broadcasted_iota(jnp.int32, (tq, tq), 0)
        zero, one = jnp.zeros((), BF16), jnp.ones((), BF16)
        dk_sc[...] = jnp.zeros_like(dk_sc)
        dv_sc[...] = jnp.zeros_like(dv_sc)

        def step(i, masked):
            sl = pl.ds(pl.multiple_of(i * tq, tq), tq)
            do_i = do_ref[sl, :]
            for hh in range(2):
                kj = k_ref[...] if fox else k_ref[hh]
                vj = v_ref[...] if fox else v_ref[hh]
                qi = jnp.where(lo if hh == 0 else hi, q_ref[sl, :], zero) if fox else q_ref[hh, sl, :]
                dob = _head_do(fox, hh, do_i, lo).astype(BF16)
                st = _dot_nt(kj, qi) * c2
                if fox:
                    st = st - _tile_lanes(f_ref[hh], nrep)
                if masked:
                    st = jnp.where(causal, st, -jnp.inf)
                pt = jnp.exp2(st - lse_ref[hh, i, 0:1, :])
                dpt = _dot_nt(vj, dob)
                dst = pt * (dpt - dl_ref[hh, i, 0:1, :])
                dv_sc[hh] += _dot(pt.astype(BF16), dob)
                if fox:
                    qi = jnp.where(hi if hh == 0 else lo, one, qi)
                dk_sc[hh] += _dot(dst.astype(BF16), qi)

        def loop_body(i, carry):
            step(i, False)
            return carry

        step(j, True)
        lax.fori_loop(j + 1, nq, loop_body, 0)
        if fox:
            dk_ref[...] = (jnp.where(lo, dk_sc[0], dk_sc[1]) * scale).astype(BF16)
            dv_ref[...] = (dv_sc[0] + dv_sc[1]).astype(BF16)
            for hh in range(2):
                dk = dk_sc[hh]
                df_ref[hh] = -jnp.where(hi if hh == 0 else lo, dk, pltpu.roll(dk, HEAD_DIM, 1))
        else:
            rope_lanes = jnp.logical_and(lane >= NOPE, lane < NOPE + ROPE)
            dkr = jnp.zeros((tq, LANES), F32)
            for hh in range(2):
                dk = dk_sc[hh] * scale
                dkv_ref[hh] = jnp.where(lo, dk, dv_sc[hh])
                dkr = dkr + jnp.where(rope_lanes, dk, 0.0)
            dkr_ref[0] = dkr

    stat = pl.BlockSpec((2, tq, LANES), lambda p, j: (p, j, 0))
    rows4 = pl.BlockSpec((2, nq, 8, tq), lambda p, j: (p, 0, 0, 0))
    pair = pl.BlockSpec((tq, LANES), lambda p, j: (j, p))
    in_specs = _pair_specs(fox, t, tq, False)
    args = [q, k, v]
    if fox:
        in_specs += [stat]
        args += [f2_rep]
    in_specs += [pl.BlockSpec((t, LANES), lambda p, j: (0, p)), rows4, rows4]
    args += [do, lse_rows, delta_rows]
    if fox:
        out_shape = (jax.ShapeDtypeStruct((t, 4 * LANES), BF16), jax.ShapeDtypeStruct((t, 4 * LANES), BF16),
                     jax.ShapeDtypeStruct((HEADS, t, LANES), F32))
        out_specs = (pair, pair, stat)
    else:
        out_shape = (jax.ShapeDtypeStruct((HEADS, t, LANES), F32), jax.ShapeDtypeStruct((HEADS // 2, t, LANES), F32))
        out_specs = (stat, pl.BlockSpec((1, tq, LANES), lambda p, j: (p, j, 0)))
    return pl.pallas_call(
        body, name="fox_attn_bwd_dkv" if fox else "mla_attn_bwd_dkv", grid=(HEADS // 2, nq),
        out_shape=out_shape, in_specs=in_specs, out_specs=out_specs,
        scratch_shapes=[pltpu.VMEM((2, tq, LANES), F32), pltpu.VMEM((2, tq, LANES), F32)],
        compiler_params=_cparams("parallel", "arbitrary"),
    )(*args)


def _attn_out(x, fox_o, mla_o, gf, gm, w_o):
    t = x.shape[0]
    tm = _row_tile(t)

    def body(x_ref, f_ref, m_ref, gf_ref, gm_ref, w_ref, x1_ref, mix_ref):
        nf, _ = _rms(f_ref[...], gf_ref[...])
        nm, _ = _rms(m_ref[...], gm_ref[...])
        nfb, nmb = nf.astype(BF16), nm.astype(BF16)
        mix_ref[:, :FOX_WIDTH] = nfb
        mix_ref[:, FOX_WIDTH:] = nmb
        x1_ref[...] = x_ref[...] + _dot(nfb, w_ref[:FOX_WIDTH, :]) + _dot(nmb, w_ref[FOX_WIDTH:, :])

    row = lambda n: pl.BlockSpec((tm, n), lambda i: (i, 0))
    full = lambda a: pl.BlockSpec(a.shape, lambda i: (0,) * a.ndim)
    return pl.pallas_call(
        body, name="attn_out", grid=(t // tm,),
        out_shape=(jax.ShapeDtypeStruct((t, D_MODEL), F32), jax.ShapeDtypeStruct((t, D_MODEL), BF16)),
        in_specs=[row(D_MODEL), row(FOX_WIDTH), row(MLA_WIDTH), full(gf), full(gm), full(w_o)],
        out_specs=(row(D_MODEL), row(D_MODEL)),
        compiler_params=_cparams("parallel"),
    )(x, fox_o, mla_o, gf, gm, w_o)


def _mlp_fwd(x1, g_mlp, w_up, w_down, g_fin, target):
    t = x1.shape[0]
    tm = _row_tile(t)
    nf = w_up.shape[0]
    tf = w_up.shape[2]

    def body(x_ref, g_ref, wu_ref, wd_ref, gf_ref, t_ref, u_ref, h_ref, dx_ref, loss_ref, dg_ref, acc_sc, h_sc):
        i, f = pl.program_id(0), pl.program_id(1)

        @pl.when(jnp.logical_and(i == 0, f == 0))
        def _():
            loss_ref[...] = jnp.zeros_like(loss_ref)
            dg_ref[...] = jnp.zeros_like(dg_ref)

        @pl.when(f == 0)
        def _():
            h, _ = _rms(x_ref[...], g_ref[...])
            h_sc[...] = h.astype(BF16)
            h_ref[...] = h_sc[...]
            acc_sc[...] = jnp.zeros_like(acc_sc)

        u = _dot(h_sc[...], wu_ref[...])
        u_ref[...] = u
        r = jnp.maximum(u, 0.0)
        acc_sc[...] += _dot((r * r).astype(BF16), wd_ref[...])

        @pl.when(f == nf - 1)
        def _():
            x2 = x_ref[...] + acc_sc[...]
            y, r2 = _rms(x2, gf_ref[...])
            err = y - t_ref[...]
            loss_ref[...] += 0.5 * jnp.sum(jnp.mean(err * err, axis=-1, keepdims=True))
            dx, dg = _rms_bwd(x2, gf_ref[...], r2, err * (1.0 / D_MODEL))
            dx_ref[...] = dx
            dg_ref[...] += dg

    row = lambda n: pl.BlockSpec((tm, n), lambda i, f: (i, 0))
    vec = pl.BlockSpec((1, D_MODEL), lambda i, f: (0, 0))
    return pl.pallas_call(
        body, name="mlp_fwd", grid=(t // tm, nf),
        out_shape=(jax.ShapeDtypeStruct((t, D_FF), F32), jax.ShapeDtypeStruct((t, D_MODEL), BF16),
                   jax.ShapeDtypeStruct((t, D_MODEL), F32), jax.ShapeDtypeStruct((8, LANES), F32),
                   jax.ShapeDtypeStruct((1, D_MODEL), F32)),
        in_specs=[row(D_MODEL), vec, pl.BlockSpec((None, D_MODEL, tf), lambda i, f: (f, 0, 0)),
                  pl.BlockSpec((None, tf, D_MODEL), lambda i, f: (f, 0, 0)), vec, row(D_MODEL)],
        out_specs=(pl.BlockSpec((tm, tf), lambda i, f: (i, f)), row(D_MODEL), row(D_MODEL),
                   pl.BlockSpec((8, LANES), lambda i, f: (0, 0)), vec),
        scratch_shapes=[pltpu.VMEM((tm, D_MODEL), F32), pltpu.VMEM((tm, D_MODEL), BF16)],
        compiler_params=_cparams("arbitrary", "arbitrary"),
    )(x1, g_mlp, w_up, w_down, g_fin, target)


def _mlp_bwd(dx2, u, x1, g_mlp, w_up, w_down):
    t = x1.shape[0]
    tm = _row_tile(t)
    nf = w_up.shape[0]
    tf = w_up.shape[2]

    def body(dx_ref, u_ref, x_ref, g_ref, wu_ref, wd_ref, du_ref, a_ref, dx1_ref, dg_ref, acc_sc):
        i, f = pl.program_id(0), pl.program_id(1)

        @pl.when(jnp.logical_and(i == 0, f == 0))
        def _():
            dg_ref[...] = jnp.zeros_like(dg_ref)

        @pl.when(f == 0)
        def _():
            acc_sc[...] = jnp.zeros_like(acc_sc)

        r = jnp.maximum(u_ref[...], 0.0)
        a_ref[...] = (r * r).astype(BF16)
        da = _dot_nt(dx_ref[...].astype(BF16), wd_ref[...])
        du = (da * (2.0 * r)).astype(BF16)
        du_ref[...] = du
        acc_sc[...] += _dot_nt(du, wu_ref[...])

        @pl.when(f == nf - 1)
        def _():
            x = x_ref[...]
            _, r1 = _rms(x, g_ref[...])
            dx, dg = _rms_bwd(x, g_ref[...], r1, acc_sc[...])
            dx1_ref[...] = dx_ref[...] + dx
            dg_ref[...] += dg

    row = lambda n: pl.BlockSpec((tm, n), lambda i, f: (i, 0))
    vec = pl.BlockSpec((1, D_MODEL), lambda i, f: (0, 0))
    blk = pl.BlockSpec((tm, tf), lambda i, f: (i, f))
    return pl.pallas_call(
        body, name="mlp_bwd", grid=(t // tm, nf),
        out_shape=(jax.ShapeDtypeStruct((t, D_FF), BF16), jax.ShapeDtypeStruct((t, D_FF), BF16),
                   jax.ShapeDtypeStruct((t, D_MODEL), F32), jax.ShapeDtypeStruct((1, D_MODEL), F32)),
        in_specs=[row(D_MODEL), blk, row(D_MODEL), vec, pl.BlockSpec((None, D_MODEL, tf), lambda i, f: (f, 0, 0)),
                  pl.BlockSpec((None, tf, D_MODEL), lambda i, f: (f, 0, 0))],
        out_specs=(blk, blk, row(D_MODEL), vec),
        scratch_shapes=[pltpu.VMEM((tm, D_MODEL), F32)],
        compiler_params=_cparams("arbitrary", "arbitrary"),
    )(dx2, u, x1, g_mlp, w_up, w_down)


def _matmul_tn(name, a, b, batch_out=None):
    batched = b.ndim == 3
    t, m = a.shape[-2:]
    n = b.shape[-1]
    tk = _row_tile(t)
    bm = m if m <= 1024 else 512
    bn = (n if n <= 1024 else 512) if batch_out is None else n // batch_out
    lead = b.shape[0] if batched else 1

    def body(a_ref, b_ref, o_ref, acc_sc):
        kk = pl.program_id(3)

        @pl.when(kk == 0)
        def _():
            acc_sc[...] = jnp.zeros_like(acc_sc)

        acc_sc[...] += _dot_tn(a_ref[...].astype(BF16), b_ref[...].astype(BF16))

        @pl.when(kk == pl.num_programs(3) - 1)
        def _():
            o_ref[...] = acc_sc[...]

    if batched:
        a_spec = pl.BlockSpec((tk, bm), lambda h, i, j, kk: (kk, i))
        b_spec = pl.BlockSpec((None, tk, bn), lambda h, i, j, kk: (h, kk, j))
        o_spec = pl.BlockSpec((None, bm, bn), lambda h, i, j, kk: (h, i, j))
        o_shape = (lead, m, n)
    else:
        a_spec = pl.BlockSpec((tk, bm), lambda h, i, j, kk: (kk, i))
        b_spec = pl.BlockSpec((tk, bn), lambda h, i, j, kk: (kk, j))
        if batch_out is None:
            o_spec = pl.BlockSpec((bm, bn), lambda h, i, j, kk: (i, j))
            o_shape = (m, n)
        else:
            o_spec = pl.BlockSpec((None, bm, bn), lambda h, i, j, kk: (j, i, 0))
            o_shape = (batch_out, m, bn)
    return pl.pallas_call(
        body, name=name, grid=(lead, m // bm, n // bn, t // tk),
        out_shape=jax.ShapeDtypeStruct(o_shape, F32),
        in_specs=[a_spec, b_spec], out_specs=o_spec,
        scratch_shapes=[pltpu.VMEM((bm, bn), F32)],
        compiler_params=_cparams("parallel", "parallel", "parallel", "arbitrary"),
    )(a, b)


def _attn_out_bwd(dx1, fox_o, mla_o, gf, gm, w_o):
    t = dx1.shape[0]
    tm = _row_tile(t)

    def body(dx_ref, f_ref, m_ref, gf_ref, gm_ref, w_ref, df_ref, dm_ref, dgf_ref, dgm_ref):
        @pl.when(pl.program_id(0) == 0)
        def _():
            dgf_ref[...] = jnp.zeros_like(dgf_ref)
            dgm_ref[...] = jnp.zeros_like(dgm_ref)
        dxb = dx_ref[...].astype(BF16)
        for o_ref, g_ref, lo_row, d_ref, dg_ref in ((f_ref, gf_ref, 0, df_ref, dgf_ref), (m_ref, gm_ref, FOX_WIDTH, dm_ref, dgm_ref)):
            dn = _dot_nt(dxb, w_ref[lo_row:lo_row + FOX_WIDTH, :])
            o = o_ref[...]
            _, r = _rms(o, g_ref[...])
            d, dg = _rms_bwd(o, g_ref[...], r, dn)
            d_ref[...] = d
            dg_ref[...] += dg

    row = lambda n: pl.BlockSpec((tm, n), lambda i: (i, 0))
    full = lambda a: pl.BlockSpec(a.shape, lambda i: (0,) * a.ndim)
    vec = pl.BlockSpec((1, FOX_WIDTH), lambda i: (0, 0))
    o_shape = jax.ShapeDtypeStruct((t, FOX_WIDTH), F32)
    g_shape = jax.ShapeDtypeStruct((1, FOX_WIDTH), F32)
    return pl.pallas_call(
        body, name="attn_out_bwd", grid=(t // tm,),
        out_shape=(o_shape, o_shape, g_shape, g_shape),
        in_specs=[row(D_MODEL), row(FOX_WIDTH), row(MLA_WIDTH), full(gf), full(gm), full(w_o)],
        out_specs=(row(FOX_WIDTH), row(MLA_WIDTH), vec, vec),
        compiler_params=_cparams("arbitrary"),
    )(dx1, fox_o, mla_o, gf, gm, w_o)


def _mla_prep_bwd(dq, dkv, dkr, dz, rest, gq, gkv, wq, wkv, cos, sin):
    t = rest.shape[0]
    tm = _row_tile(t)

    def body(dq_ref, dkv_ref, dkr_ref, dz_ref, r_ref, gq_ref, gkv_ref, wq_ref, wkv_ref, c_ref, s_ref,
             dr_ref, dqp_ref, dkvb_ref, dgq_ref, dgkv_ref):
        @pl.when(pl.program_id(0) == 0)
        def _():
            dgq_ref[...] = jnp.zeros_like(dgq_ref)
            dgkv_ref[...] = jnp.zeros_like(dgkv_ref)
        cos_, sin_ = c_ref[...], s_ref[...]
        dcq = jnp.zeros((tm, Q_RANK), F32)
        dckv = jnp.zeros((tm, KV_RANK), F32)
        for h in range(HEADS):
            dqp = _rope_bwd(dq_ref[h], cos_, sin_).astype(BF16)
            dqp_ref[h] = dqp
            dcq = dcq + _dot_nt(dqp, wq_ref[h])
            dkvb = dkv_ref[h].astype(BF16)
            dkvb_ref[h] = dkvb
            dckv = dckv + _dot_nt(dkvb, wkv_ref[h])
        dkrope = dkr_ref[0]
        for pr in range(1, HEADS // 2):
            dkrope = dkrope + dkr_ref[pr]
        cq = r_ref[:, REST_CQ:REST_CKV]
        _, rq = _rms(cq, gq_ref[...])
        d_cq, dgq = _rms_bwd(cq, gq_ref[...], rq, dcq)
        ckv = r_ref[:, REST_CKV:REST_KR]
        _, rkv = _rms(ckv, gkv_ref[...])
        d_ckv, dgkv = _rms_bwd(ckv, gkv_ref[...], rkv, dckv)
        dgq_ref[...] += dgq
        dgkv_ref[...] += dgkv
        dr_ref[:, 0:REST_CQ] = dz_ref[...].astype(BF16)
        dr_ref[:, REST_CQ:REST_CKV] = d_cq.astype(BF16)
        dr_ref[:, REST_CKV:REST_KR] = d_ckv.astype(BF16)
        dr_ref[:, REST_KR:REST_COLS] = _rope_bwd(dkrope, cos_, sin_).astype(BF16)

    row = lambda n: pl.BlockSpec((tm, n), lambda i: (i, 0))
    full = lambda a: pl.BlockSpec(a.shape, lambda i: (0,) * a.ndim)
    heads = pl.BlockSpec((HEADS, tm, LANES), lambda i: (0, i, 0))
    hshape = jax.ShapeDtypeStruct((HEADS, t, LANES), BF16)
    return pl.pallas_call(
        body, name="mla_prep_bwd", grid=(t // tm,),
        out_shape=(jax.ShapeDtypeStruct((t, REST_COLS), BF16), hshape, hshape,
                   jax.ShapeDtypeStruct((1, Q_RANK), F32), jax.ShapeDtypeStruct((1, KV_RANK), F32)),
        in_specs=[heads, heads, pl.BlockSpec((HEADS // 2, tm, LANES), lambda i: (0, i, 0)), row(LANES), row(REST_COLS),
                  full(gq), full(gkv), full(wq), full(wkv), row(LANES), row(LANES)],
        out_specs=(row(REST_COLS), heads, heads, pl.BlockSpec((1, Q_RANK), lambda i: (0, 0)),
                   pl.BlockSpec((1, KV_RANK), lambda i: (0, 0))),
        compiler_params=_cparams("arbitrary"),
    )(dq, dkv, dkr, dz, rest, gq, gkv, wq, wkv, cos, sin)


def _in_proj_bwd(x, g, dx1, dfq, dfk, dfv, drest, w_qkv, w_rest):
    t = x.shape[0]
    tm = _row_tile(t)

    def body(x_ref, g_ref, dx1_ref, dq_ref, dk_ref, dv_ref, dr_ref, wq_ref, wr_ref, dx_ref, dg_ref):
        @pl.when(pl.program_id(0) == 0)
        def _():
            dg_ref[...] = jnp.zeros_like(dg_ref)
        dh = _dot(dr_ref[...], wr_ref[...])
        for n, ref in enumerate((dq_ref, dk_ref, dv_ref)):
            dh = dh + _dot(ref[...], wq_ref[n * FOX_WIDTH:(n + 1) * FOX_WIDTH, :])
        xv = x_ref[...]
        _, r = _rms(xv, g_ref[...])
        dx, dg = _rms_bwd(xv, g_ref[...], r, dh)
        dx_ref[...] = dx1_ref[...] + dx
        dg_ref[...] += dg

    row = lambda n: pl.BlockSpec((tm, n), lambda i: (i, 0))
    full = lambda a: pl.BlockSpec(a.shape, lambda i: (0,) * a.ndim)
    vec = pl.BlockSpec((1, D_MODEL), lambda i: (0, 0))
    return pl.pallas_call(
        body, name="in_proj_bwd", grid=(t // tm,),
        out_shape=(jax.ShapeDtypeStruct((t, D_MODEL), F32), jax.ShapeDtypeStruct((1, D_MODEL), F32)),
        in_specs=[row(D_MODEL), full(g), row(D_MODEL), row(FOX_WIDTH), row(FOX_WIDTH), row(FOX_WIDTH), row(REST_COLS),
                  full(w_qkv), full(w_rest)],
        out_specs=(row(D_MODEL), vec),
        compiler_params=_cparams("arbitrary"),
    )(x, g, dx1, dfq, dfk, dfv, drest, w_qkv, w_rest)


def _pad_cols(a, n):
    return jnp.pad(a, ((0, 0),) * (a.ndim - 1) + ((0, n - a.shape[-1]),))


def kernel(x, positions, attn_norm_g, w_in, b_forget, q_norm_g, w_uq, kv_norm_g, w_ukv, fox_out_g, mla_out_g, w_o, mlp_norm_g, w_up, w_down, final_norm_g, loss_target, m_attn_norm_g, m_w_in, m_b_forget, m_q_norm_g, m_w_uq, m_kv_norm_g, m_w_ukv, m_fox_out_g, m_mla_out_g, m_w_o, m_mlp_norm_g, m_w_up, m_w_down, m_final_norm_g, v_attn_norm_g, v_w_in, v_b_forget, v_q_norm_g, v_w_uq, v_kv_norm_g, v_w_ukv, v_fox_out_g, v_mla_out_g, v_w_o, v_mlp_norm_g, v_w_up, v_w_down, v_final_norm_g):
    t = x.shape[1]
    tq = _row_tile(t)
    xs = x[0]
    target = loss_target[0]

    shards = [jnp.transpose(w_in[0]), _pad_cols(w_uq[0], LANES), w_ukv[0], w_o[0], w_up[0], w_down[0]]
    g_in, wq, wkv, g_o, g_up, g_down = _all_gather([s.astype(BF16) for s in shards])
    win = g_in.reshape(IN_COLS, D_MODEL)
    off_ff, off_cq, off_kr = 3 * FOX_WIDTH, 3 * FOX_WIDTH + HEADS, IN_COLS - ROPE
    zeros = lambda n: jnp.zeros((n, D_MODEL), BF16)
    w_qkv = win[:off_ff]
    w_rest = jnp.concatenate([
        win[off_ff:off_cq], zeros(REST_CQ - HEADS), win[off_cq:off_kr],
        zeros(NOPE), win[off_kr:], zeros(LANES - NOPE - ROPE)], axis=0)
    wo = g_o.reshape(D_MODEL, D_MODEL)

    cos, sin = _rope_tables(positions.reshape(t, 1))
    h1, fq, fk, fv, rest = _in_proj(xs, attn_norm_g, w_qkv, w_rest)
    b128 = _pad_cols(b_forget, LANES)
    f2_rows, f2_rep = _forget_cumsum(rest, b128)
    f2_rows = f2_rows.reshape(HEADS, t // tq, 1, tq)
    fox_o, fox_lse, fox_lse_rows = _attn_fwd(True, fq, fk, fv, f2_rows)
    mq, mk, mkv, cqn, ckvn = _mla_prep(rest, q_norm_g, kv_norm_g, wq, wkv, cos, sin)
    mla_o, mla_lse, mla_lse_rows = _attn_fwd(False, mq, mk, mkv)
    x1, mixed = _attn_out(xs, fox_o, mla_o, fox_out_g, mla_out_g, wo)
    u, h2, dx2, loss8, d_gfin = _mlp_fwd(x1, mlp_norm_g, g_up, g_down, final_norm_g.reshape(1, D_MODEL), target)

    du, act, dx1, d_gmlp = _mlp_bwd(dx2, u, x1, mlp_norm_g, g_up, g_down)
    dw_down = _matmul_tn("dw_down", act, dx2)
    dw_up = _matmul_tn("dw_up", h2, du, batch_out=N_DEV)
    dfox_o, dmla_o, d_gfox, d_gmla = _attn_out_bwd(dx1, fox_o, mla_o, fox_out_g, mla_out_g, wo)
    dw_o = _matmul_tn("dw_o", mixed, dx1)

    dfq, fox_delta_rows, d_fq = _attn_bwd_dq(True, fq, fk, fv, dfox_o, fox_o, fox_lse, f2_rows)
    dfk, dfv, d_fk = _attn_bwd_dkv(True, fq, fk, fv, dfox_o, fox_lse_rows, fox_delta_rows, f2_rep)
    dz, d_b = _forget_bwd(rest, b128, d_fq, d_fk)

    dmq, mla_delta_rows = _attn_bwd_dq(False, mq, mk, mkv, dmla_o, mla_o, mla_lse)
    dmkv, dmkr = _attn_bwd_dkv(False, mq, mk, mkv, dmla_o, mla_lse_rows, mla_delta_rows)
    drest, dqp, dkvb, d_gq, d_gkv = _mla_prep_bwd(dmq, dmkv, dmkr, dz, rest, q_norm_g, kv_norm_g, wq, wkv, cos, sin)
    dw_uq = _matmul_tn("dw_uq", cqn, dqp)
    dw_ukv = _matmul_tn("dw_ukv", ckvn, dkvb)
    grad_x, d_gattn = _in_proj_bwd(xs, attn_norm_g, dx1, dfq, dfk, dfv, drest, w_qkv, w_rest)
    dw_q = _matmul_tn("dw_in_q", dfq, h1)
    dw_k = _matmul_tn("dw_in_k", dfk, h1)
    dw_v = _matmul_tn("dw_in_v", dfv, h1)
    dw_r = _matmul_tn("dw_in_rest", drest, h1)

    dw_in = jnp.concatenate([dw_q, dw_k, dw_v, dw_r[0:HEADS], dw_r[REST_CQ:REST_KR],
                             dw_r[REST_KR + NOPE:REST_KR + NOPE + ROPE]], axis=0)
    names = ("w_in", "w_uq", "w_ukv", "w_o", "w_up", "w_down")
    grads = [dw_in.reshape(N_DEV, IN_SHARD, D_MODEL), dw_uq, dw_ukv, dw_o.reshape(N_DEV, -1, D_MODEL), dw_up,
             dw_down.reshape(N_DEV, -1, D_MODEL)]
    core = lax.axis_index("c").astype(jnp.int32).reshape(1)
    chip = (2 * lax.axis_index("x") + lax.axis_index("y")).astype(jnp.int32).reshape(1)
    got = _rs_sibling_exchange(grads)
    sums = [_rs_sibling_sum("rs_sibling_sum_" + nm, g, l, core) for nm, g, l in zip(names, grads, got)]
    others = _rs_chip_exchange([s[1] for s in sums])
    sharded = (w_in, w_uq, w_ukv, w_o, w_up, w_down)
    moments_m = (m_w_in, m_w_uq, m_w_ukv, m_w_o, m_w_up, m_w_down)
    moments_v = (v_w_in, v_w_uq, v_w_ukv, v_w_o, v_w_up, v_w_down)
    g_in_t = _rs_final_sum("rs_final_sum_w_in", sums[0][0], others[0], chip)
    big = [_adamw_given("adamw_w_in", jnp.transpose(g_in_t), w_in, m_w_in, v_w_in)]
    for a in range(1, len(names)):
        big.append(_adamw_sharded("adamw_" + names[a], sharded[a], moments_m[a], moments_v[a], sums[a][0], others[a], chip))
    big_g, big_d, big_m, big_v = [[b[k] for b in big] for k in range(4)]

    as_row = lambda a: a.reshape(1, -1)
    small_w = (attn_norm_g, b_forget, q_norm_g, kv_norm_g, fox_out_g, mla_out_g, mlp_norm_g, final_norm_g)
    small_m = (m_attn_norm_g, m_b_forget, m_q_norm_g, m_kv_norm_g, m_fox_out_g, m_mla_out_g, m_mlp_norm_g, m_final_norm_g)
    small_v = (v_attn_norm_g, v_b_forget, v_q_norm_g, v_kv_norm_g, v_fox_out_g, v_mla_out_g, v_mlp_norm_g, v_final_norm_g)
    total = _small_all_reduce([d_gattn, d_b, d_gq, d_gkv, d_gfox, d_gmla, d_gmlp, d_gfin], loss8)
    small = _adamw_small(total, [as_row(a) for a in small_w], [as_row(a) for a in small_m], [as_row(a) for a in small_v])
    loss = small[0].reshape(())
    s_g, s_d, s_m, s_v = [[small[1 + 4 * r + k].reshape(small_w[r].shape) for r in range(len(small_w))] for k in range(4)]

    def ordered(small_, bigs):
        ga, bf, gq_, gkv_, gfo, gml, gmlp_, gfin_ = small_
        bin_, buq, bukv, bo, bup, bdown = bigs
        return [ga, bin_, bf, gq_, buq, gkv_, bukv, gfo, gml, bo, gmlp_, bup, bdown, gfin_]

    return (loss, grad_x[None], *ordered(s_g, big_g), *ordered(s_d, big_d), *ordered(s_m, big_m), *ordered(s_v, big_v))
```

```python
import math
from typing import Callable, NamedTuple

import numpy as np
import jax
import jax.numpy as jnp
from jax import lax
from jax.experimental import pallas as pl
from jax.experimental.pallas import tpu as pltpu

F32 = jnp.float32
BF16 = jnp.bfloat16
MESH = pl.DeviceIdType.MESH

D_MODEL = 1024
HEADS = 8
HEAD_DIM = 64
FOX_WIDTH = 512
MLA_WIDTH = 512
NOPE = 64
ROPE = 32
QK_DIM = 96
Q_RANK = 384
KV_RANK = 256
D_FF = 4096
IN_COLS = 2216
ROPE_THETA = 10000.0
EPS = 1e-6
FOX_SCALE = 1.0 / math.sqrt(HEAD_DIM)
MLA_SCALE = 1.0 / math.sqrt(QK_DIM)
ADAM_LR = 0.001
ADAM_B1 = 0.9
ADAM_B2 = 0.999
ADAM_EPS = 1e-08
ADAM_WD = 0.01
ADAM_STEP = 10

N_DEV = 8
LANES = 128
REST_COLS = 896
REST_CQ = LANES
REST_CKV = REST_CQ + Q_RANK
REST_KR = REST_CKV + KV_RANK
LOG2E = 1.4426950408889634
VMEM_LIMIT = 56 * 1024 * 1024

IN_SHARD = IN_COLS // N_DEV
SMALL_SIZES = (1024, 8, 384, 256, 512, 512, 1024, 1024)
SMALL_ROWS = 16
LOSS_ROW = len(SMALL_SIZES)


def _cparams(*sem):
    return pltpu.CompilerParams(dimension_semantics=sem or None, vmem_limit_bytes=VMEM_LIMIT)


def _row_tile(t):
    return 512 if t >= 2048 else 128


def _dot(a, b):
    return jnp.dot(a, b, preferred_element_type=F32)


def _dot_nt(a, b):
    return lax.dot_general(a, b, (((1,), (1,)), ((), ())), preferred_element_type=F32)


def _dot_tn(a, b):
    return lax.dot_general(a, b, (((0,), (0,)), ((), ())), preferred_element_type=F32)


def _rms(x, g):
    r = lax.rsqrt(jnp.mean(x * x, axis=-1, keepdims=True) + EPS)
    return x * r * g, r


def _rms_bwd(x, g, r, dy):
    xh = x * r
    gdy = dy * g
    dx = r * (gdy - xh * jnp.mean(gdy * xh, axis=-1, keepdims=True))
    return dx, jnp.sum(dy * xh, axis=0, keepdims=True)


def _lane():
    return lax.broadcasted_iota(jnp.int32, (1, LANES), 1)


def _rot(x):
    lane = _lane()
    half = NOPE + ROPE // 2
    first = jnp.logical_and(lane >= NOPE, lane < half)
    second = jnp.logical_and(lane >= half, lane < NOPE + ROPE)
    return jnp.where(first, -pltpu.roll(x, LANES - ROPE // 2, 1), jnp.where(second, pltpu.roll(x, ROPE // 2, 1), 0.0))


def _rope(x, cos, sin):
    return x * cos + _rot(x) * sin


def _rope_bwd(dy, cos, sin):
    return dy * cos - _rot(dy * sin)


def _remote(src, dst, send_sem, recv_sem, to):
    return pltpu.make_async_remote_copy(src_ref=src, dst_ref=dst, send_sem=send_sem, recv_sem=recv_sem,
                                        device_id=to, device_id_type=MESH)


def _hbm_specs(n):
    return [pl.BlockSpec(memory_space=pl.ANY)] * n


def _all_gather(blocks):
    n = len(blocks)

    def body(*refs):
        x_refs, out_refs = refs[:n], refs[n:2 * n]
        send_sems, recv_sems, local_sems = refs[2 * n:]
        x, y, c = lax.axis_index("x"), lax.axis_index("y"), lax.axis_index("c")
        me, sibling = (x, y, c), (x, y, 1 - c)
        chips = [(1 - x, y), (x, 1 - y), (1 - x, 1 - y)]

        def slot(a, px, py, pc):
            return out_refs[a].at[4 * px + 2 * py + pc]

        def copy(a, k, blk, to, src=None):
            return _remote(slot(a, *blk) if src is None else src, slot(a, *blk),
                           send_sems.at[7 * a + k], recv_sems.at[7 * a + k], to)

        mine = [pltpu.make_async_copy(x_refs[a], slot(a, *me), local_sems.at[a]) for a in range(n)]
        first, passed = [], []
        for a in range(n):
            mine[a].start()
            first.append(copy(a, 0, me, sibling, src=x_refs[a]))
            first += [copy(a, 1 + j, me, (*chip, c), src=x_refs[a]) for j, chip in enumerate(chips)]
        for cp in first:
            cp.start()
        for a in range(n):
            for j, chip in enumerate(chips):
                copy(a, 1 + j, (*chip, c), me).wait_recv()
                passed.append(copy(a, 4 + j, (*chip, c), sibling))
                passed[-1].start()
        for a in range(n):
            copy(a, 0, sibling, me).wait_recv()
            for j, chip in enumerate(chips):
                copy(a, 4 + j, (*chip, 1 - c), me).wait_recv()
        for cp in first + passed:
            cp.wait_send()
        for cp in mine:
            cp.wait()

    return pl.pallas_call(
        body, name="all_gather_weights",
        out_shape=[jax.ShapeDtypeStruct((N_DEV,) + b.shape, b.dtype) for b in blocks],
        in_specs=_hbm_specs(n), out_specs=_hbm_specs(n),
        scratch_shapes=[pltpu.SemaphoreType.DMA((7 * n,)), pltpu.SemaphoreType.DMA((7 * n,)), pltpu.SemaphoreType.DMA((n,))],
    )(*blocks)


def _symmetric_comm(inputs, out_shape, aliases, per_array, copies):
    def start(in_refs, out_refs, sems):
        for cp in copies(in_refs, out_refs, *sems):
            cp.start()

    def finish(in_refs, out_refs, sems):
        for cp in copies(in_refs, out_refs, *sems):
            cp.wait()

    n_sems = per_array * len(inputs)
    return _Comm(tuple(inputs), tuple(out_shape), aliases,
                 (pltpu.SemaphoreType.DMA((n_sems,)), pltpu.SemaphoreType.DMA((n_sems,))), start, finish)


def _ag_direct(shards):
    def copies(in_refs, out_refs, send_sems, recv_sems):
        x, y, c = lax.axis_index("x"), lax.axis_index("y"), lax.axis_index("c")
        peers = [(x, y, 1 - c), (1 - x, y, c), (x, 1 - y, c), (1 - x, 1 - y, c)]
        cps = []
        for a in range(len(shards)):
            mine = out_refs[a].at[4 * x + 2 * y + c]
            cps.append(pltpu.make_async_copy(in_refs[a], mine, send_sems.at[5 * a]))
            cps += [_remote(in_refs[a], mine, send_sems.at[5 * a + k], recv_sems.at[5 * a + k], peer)
                    for k, peer in enumerate(peers, start=1)]
        return cps

    return _symmetric_comm(shards, [jax.ShapeDtypeStruct((N_DEV,) + s.shape, s.dtype) for s in shards], {}, 5, copies)


def _ag_forward(gathered):
    def copies(in_refs, out_refs, send_sems, recv_sems):
        x, y, c = lax.axis_index("x"), lax.axis_index("y"), lax.axis_index("c")
        chips = [(1 - x, y), (x, 1 - y), (1 - x, 1 - y)]
        return [_remote(out_refs[a].at[4 * cx + 2 * cy + c], out_refs[a].at[4 * cx + 2 * cy + c],
                        send_sems.at[3 * a + j], recv_sems.at[3 * a + j], (x, y, 1 - c))
                for a in range(len(gathered)) for j, (cx, cy) in enumerate(chips)]

    shapes = [jax.ShapeDtypeStruct(g.shape, g.dtype) for g in gathered]
    return _symmetric_comm(gathered, shapes, {a: a for a in range(len(gathered))}, 3, copies)


def _rs_to_sibling(grads):
    def copies(in_refs, out_refs, send_sems, recv_sems):
        x, y, c = lax.axis_index("x"), lax.axis_index("y"), lax.axis_index("c")
        return [_remote(in_refs[a].at[2 * q + 1 - c], out_refs[a].at[q], send_sems.at[4 * a + q], recv_sems.at[4 * a + q], (x, y, 1 - c))
                for a in range(len(grads)) for q in range(4)]

    return _symmetric_comm(grads, [jax.ShapeDtypeStruct((4,) + g.shape[1:], g.dtype) for g in grads], {}, 4, copies)


def _rs_to_chips(parts):
    def copies(in_refs, out_refs, send_sems, recv_sems):
        x, y, c = lax.axis_index("x"), lax.axis_index("y"), lax.axis_index("c")
        chips = [(1 - x, y), (x, 1 - y), (1 - x, 1 - y)]
        return [_remote(in_refs[a].at[2 * cx + cy], out_refs[a].at[k], send_sems.at[3 * a + k], recv_sems.at[3 * a + k], (cx, cy, c))
                for a in range(len(parts)) for k, (cx, cy) in enumerate(chips)]

    return _symmetric_comm(parts, [jax.ShapeDtypeStruct((3,) + p.shape[1:], p.dtype) for p in parts], {}, 3, copies)


def _comm_call(name, comm):
    n_in, n_out = len(comm.inputs), len(comm.out_shape)

    def body(*refs):
        ins, outs, sems = refs[:n_in], refs[n_in:n_in + n_out], refs[n_in + n_out:]
        comm.start(ins, outs, sems)
        comm.finish(ins, outs, sems)

    return pl.pallas_call(
        body, name=name, out_shape=list(comm.out_shape), in_specs=_hbm_specs(n_in), out_specs=_hbm_specs(n_out),
        scratch_shapes=list(comm.scratch), input_output_aliases=dict(comm.aliases),
    )(*comm.inputs)


def _small_all_reduce(parts, loss8):
    n = len(parts)

    def body(*refs):
        p_refs, loss_ref, out_ref, pack, land, send_sems, recv_sems = refs[:n], *refs[n:]
        x, y, c = lax.axis_index("x"), lax.axis_index("y"), lax.axis_index("c")
        me = 4 * x + 2 * y + c
        pack[...] = jnp.zeros_like(pack)
        for r, ref in enumerate(p_refs):
            pack[r:r + 1, 0:ref.shape[1]] = ref[...]
        pack[LOSS_ROW:LOSS_ROW + 1, 0:LANES] = loss_ref[0:1, :]
        land[me] = pack[...]
        cps = []
        for k in range(1, N_DEV):
            peer = (x ^ (k >> 2), y ^ ((k >> 1) & 1), c ^ (k & 1))
            cps.append(_remote(pack, land.at[me], send_sems.at[k - 1], recv_sems.at[k - 1], peer))
        for cp in cps:
            cp.start()
        for cp in cps:
            cp.wait()
        acc = land[0]
        for d in range(1, N_DEV):
            acc = acc + land[d]
        out_ref[...] = acc

    vmem = pl.BlockSpec(memory_space=pltpu.VMEM)
    return pl.pallas_call(
        body, name="small_all_reduce",
        out_shape=jax.ShapeDtypeStruct((SMALL_ROWS, D_MODEL), F32),
        in_specs=[vmem] * (n + 1), out_specs=vmem,
        scratch_shapes=[pltpu.VMEM((SMALL_ROWS, D_MODEL), F32), pltpu.VMEM((N_DEV, SMALL_ROWS, D_MODEL), F32),
                        pltpu.SemaphoreType.DMA((N_DEV - 1,)), pltpu.SemaphoreType.DMA((N_DEV - 1,))],
    )(*parts, loss8)


def _rs_sibling_sum(name, grad, got, core):
    _, rows, cols = grad.shape

    def body(c_ref, g_ref, l_ref, f_ref, b_ref):
        s = g_ref[...] + l_ref[...]
        f_ref[...] = s
        b_ref[...] = s.astype(BF16)

    by_chip = pl.BlockSpec((None, rows, cols), lambda q, c_ref: (q, 0, 0))
    return pl.pallas_call(
        body, name=name,
        grid_spec=pltpu.PrefetchScalarGridSpec(
            num_scalar_prefetch=1, grid=(4,),
            in_specs=[pl.BlockSpec((None, rows, cols), lambda q, c_ref: (2 * q + c_ref[0], 0, 0)), by_chip],
            out_specs=[by_chip, by_chip]),
        out_shape=(jax.ShapeDtypeStruct((4, rows, cols), F32), jax.ShapeDtypeStruct((4, rows, cols), BF16)),
        compiler_params=_cparams("parallel"),
    )(core, grad, got)


def _adamw_math(w, g, m, v):
    m2 = ADAM_B1 * m + (1.0 - ADAM_B1) * g
    v2 = ADAM_B2 * v + (1.0 - ADAM_B2) * (g * g)
    m_hat = m2 / (1.0 - ADAM_B1 ** ADAM_STEP)
    v_hat = v2 / (1.0 - ADAM_B2 ** ADAM_STEP)
    delta = -ADAM_LR * (m_hat / (jnp.sqrt(v_hat) + ADAM_EPS) + ADAM_WD * w)
    return delta, m2, v2


def _update_tile(rows):
    return 256 if rows % 256 == 0 else rows


def _rs_final_sum(name, chip_sums, got, chip):
    _, rows, cols = chip_sums.shape

    def body(q_ref, o_ref, r_ref, g_out):
        g = o_ref[...]
        for k in range(3):
            g = g + r_ref[k].astype(F32)
        g_out[...] = g

    return pl.pallas_call(
        body, name=name,
        grid_spec=pltpu.PrefetchScalarGridSpec(
            num_scalar_prefetch=1, grid=(1,),
            in_specs=[pl.BlockSpec((None, rows, cols), lambda i, q_ref: (q_ref[0], 0, 0)),
                      pl.BlockSpec((3, rows, cols), lambda i, q_ref: (0, 0, 0))],
            out_specs=pl.BlockSpec((rows, cols), lambda i, q_ref: (0, 0))),
        out_shape=jax.ShapeDtypeStruct((rows, cols), F32),
    )(chip, chip_sums, got)


def _adamw_sharded(name, w, m, v, chip_sums, got, chip):
    _, rows, cols = w.shape
    tr = _update_tile(rows)

    def body(q_ref, o_ref, r_ref, w_ref, m_ref, v_ref, g_out, d_out, m_out, v_out):
        g = o_ref[:, 0:cols]
        for k in range(3):
            g = g + r_ref[k, :, 0:cols].astype(F32)
        d, m2, v2 = _adamw_math(w_ref[0], g, m_ref[0], v_ref[0])
        g_out[0] = g
        d_out[0] = d
        m_out[0] = m2
        v_out[0] = v2

    own = pl.BlockSpec((1, tr, cols), lambda i, q_ref: (0, i, 0))
    shp = jax.ShapeDtypeStruct(w.shape, F32)
    wide = chip_sums.shape[2]
    return pl.pallas_call(
        body, name=name,
        grid_spec=pltpu.PrefetchScalarGridSpec(
            num_scalar_prefetch=1, grid=(rows // tr,),
            in_specs=[pl.BlockSpec((None, tr, wide), lambda i, q_ref: (q_ref[0], i, 0)),
                      pl.BlockSpec((3, tr, wide), lambda i, q_ref: (0, i, 0)), own, own, own],
            out_specs=[own] * 4),
        out_shape=(shp,) * 4,
        compiler_params=_cparams("parallel"),
    )(chip, chip_sums, got, w, m, v)


def _adamw_given(name, g, w, m, v):
    _, rows, cols = w.shape
    tr = _update_tile(rows)

    def body(g_ref, w_ref, m_ref, v_ref, g_out, d_out, m_out, v_out):
        g = g_ref[...]
        d, m2, v2 = _adamw_math(w_ref[0], g, m_ref[0], v_ref[0])
        g_out[0] = g
        d_out[0] = d
        m_out[0] = m2
        v_out[0] = v2

    own = pl.BlockSpec((1, tr, cols), lambda i: (0, i, 0))
    shp = jax.ShapeDtypeStruct(w.shape, F32)
    return pl.pallas_call(
        body, name=name, grid=(rows // tr,), out_shape=(shp,) * 4,
        in_specs=[pl.BlockSpec((tr, cols), lambda i: (i, 0)), own, own, own], out_specs=[own] * 4,
        compiler_params=_cparams("parallel"),
    )(g, w, m, v)


def _adamw_small(total, ws, ms, vs):
    n = len(ws)

    def body(*refs):
        t_ref = refs[0]
        w_refs, m_refs, v_refs = refs[1:1 + n], refs[1 + n:1 + 2 * n], refs[1 + 2 * n:1 + 3 * n]
        outs = refs[1 + 3 * n:]
        outs[0][...] = t_ref[LOSS_ROW:LOSS_ROW + 1, 0:1]
        for r in range(n):
            g = t_ref[r:r + 1, 0:w_refs[r].shape[1]]
            d, m2, v2 = _adamw_math(w_refs[r][...], g, m_refs[r][...], v_refs[r][...])
            for k, val in enumerate((g, d, m2, v2)):
                outs[1 + 4 * r + k][...] = val

    vmem = pl.BlockSpec(memory_space=pltpu.VMEM)
    out_shape = [jax.ShapeDtypeStruct((1, 1), F32)]
    for w in ws:
        out_shape += [jax.ShapeDtypeStruct(w.shape, F32)] * 4
    return pl.pallas_call(
        body, name="adamw_small", out_shape=out_shape,
        in_specs=[vmem] * (1 + 3 * n), out_specs=[vmem] * len(out_shape),
    )(total, *ws, *ms, *vs)


def _rope_tables(pos_col):
    t = pos_col.shape[0]
    inv = (np.float32(ROPE_THETA) ** (-np.arange(0, ROPE, 2, dtype=np.float32) / np.float32(ROPE))).astype(np.float32)
    freq = np.zeros((1, LANES), np.float32)
    freq[0, NOPE:NOPE + ROPE // 2] = inv
    freq[0, NOPE + ROPE // 2:NOPE + ROPE] = inv
    tm = _row_tile(t)

    def body(p_ref, f_ref, c_ref, s_ref):
        ang = p_ref[...].astype(F32) * f_ref[...]
        c_ref[...] = jnp.cos(ang)
        s_ref[...] = jnp.sin(ang)

    shp = jax.ShapeDtypeStruct((t, LANES), F32)
    return pl.pallas_call(
        body, name="rope_tables", grid=(t // tm,), out_shape=(shp, shp),
        in_specs=[pl.BlockSpec((tm, 1), lambda i: (i, 0)), pl.BlockSpec((1, LANES), lambda i: (0, 0))],
        out_specs=(pl.BlockSpec((tm, LANES), lambda i: (i, 0)),) * 2,
        compiler_params=_cparams("parallel"),
    )(pos_col, jnp.asarray(freq))


def _in_proj(x, g, w_qkv, w_rest):
    t = x.shape[0]
    tm = _row_tile(t)

    def body(x_ref, g_ref, wq_ref, wr_ref, h_ref, fq_ref, fk_ref, fv_ref, r_ref):
        h, _ = _rms(x_ref[...], g_ref[...])
        hb = h.astype(BF16)
        h_ref[...] = hb
        for n, ref in enumerate((fq_ref, fk_ref, fv_ref)):
            ref[...] = _dot_nt(hb, wq_ref[n * FOX_WIDTH:(n + 1) * FOX_WIDTH, :]).astype(BF16)
        r_ref[...] = _dot_nt(hb, wr_ref[...])

    row = lambda n: pl.BlockSpec((tm, n), lambda i: (i, 0))
    full = lambda a: pl.BlockSpec(a.shape, lambda i: (0,) * a.ndim)
    return pl.pallas_call(
        body, name="in_proj", grid=(t // tm,),
        out_shape=(jax.ShapeDtypeStruct((t, D_MODEL), BF16),) + (jax.ShapeDtypeStruct((t, FOX_WIDTH), BF16),) * 3
        + (jax.ShapeDtypeStruct((t, REST_COLS), F32),),
        in_specs=[row(D_MODEL), full(g), full(w_qkv), full(w_rest)],
        out_specs=(row(D_MODEL), row(FOX_WIDTH), row(FOX_WIDTH), row(FOX_WIDTH), row(REST_COLS)),
        compiler_params=_cparams("parallel"),
    )(x, g, w_qkv, w_rest)


def _log_sigmoid(z):
    return jnp.minimum(z, 0.0) - jnp.log(1.0 + jnp.exp(-jnp.abs(z)))


def _split3(v):
    hi = v.astype(BF16)
    r1 = v - hi.astype(F32)
    mid = r1.astype(BF16)
    lo = (r1 - mid.astype(F32)).astype(BF16)
    return hi, mid, lo


def _scan_tile(t):
    return 256 if t >= 256 else t


def _forget_cumsum(rest, b128):
    t = rest.shape[0]
    tb = _scan_tile(t)

    def body(r_ref, b_ref, row_ref, rep_ref, f_sc, carry):
        @pl.when(pl.program_id(0) == 0)
        def _():
            carry[...] = jnp.zeros_like(carry)
        lf = _log_sigmoid(r_ref[...] + b_ref[...])
        tri = (lax.broadcasted_iota(jnp.int32, (tb, tb), 0) >= lax.broadcasted_iota(jnp.int32, (tb, tb), 1)).astype(BF16)
        hi, mid, lo = _split3(lf)
        f_sc[...] = (_dot(tri, hi) + _dot(tri, mid)) + _dot(tri, lo) + carry[...]
        carry[...] = f_sc[tb - 1:tb, :]
        f2 = f_sc[...] * LOG2E
        row_ref[...] = jnp.transpose(f2)[0:HEADS, :]
        lane = _lane()
        for h in range(HEADS):
            col = jnp.sum(jnp.where(lane == h, f2, 0.0), axis=1, keepdims=True)
            rep_ref[h] = jnp.broadcast_to(col, (tb, LANES))

    return pl.pallas_call(
        body, name="forget_cumsum", grid=(t // tb,),
        out_shape=(jax.ShapeDtypeStruct((HEADS, t), F32), jax.ShapeDtypeStruct((HEADS, t, LANES), F32)),
        in_specs=[pl.BlockSpec((tb, LANES), lambda i: (i, 0)), pl.BlockSpec((1, LANES), lambda i: (0, 0))],
        out_specs=(pl.BlockSpec((HEADS, tb), lambda i: (0, i)), pl.BlockSpec((HEADS, tb, LANES), lambda i: (0, i, 0))),
        scratch_shapes=[pltpu.VMEM((tb, LANES), F32), pltpu.VMEM((1, LANES), F32)],
        compiler_params=_cparams("arbitrary"),
    )(rest, b128)


def _forget_bwd(rest, b128, d_fq, d_fk):
    t = rest.shape[0]
    tb = _scan_tile(t)
    nb = t // tb

    def body(r_ref, b_ref, dfq_ref, dfk_ref, dz_ref, db_ref, carry):
        @pl.when(pl.program_id(0) == 0)
        def _():
            carry[...] = jnp.zeros_like(carry)
            db_ref[...] = jnp.zeros_like(db_ref)
        tri = (lax.broadcasted_iota(jnp.int32, (tb, tb), 0) <= lax.broadcasted_iota(jnp.int32, (tb, tb), 1)).astype(BF16)
        lane = _lane()
        df = jnp.zeros((tb, LANES), F32)
        for h in range(HEADS):
            df = df + jnp.where(lane == h, dfq_ref[h] + dfk_ref[h], 0.0)
        hi, mid, lo = _split3(df)
        dlf = (_dot(tri, hi) + _dot(tri, mid)) + _dot(tri, lo) + carry[...]
        z = r_ref[...] + b_ref[...]
        dz = dlf / (1.0 + jnp.exp(z))
        dz_ref[...] = dz
        db_ref[...] += jnp.sum(dz, axis=0, keepdims=True)
        carry[...] = carry[...] + jnp.sum(df, axis=0, keepdims=True)

    rev = lambda i: (nb - 1 - i, 0)
    rev3 = pl.BlockSpec((HEADS, tb, LANES), lambda i: (0, nb - 1 - i, 0))
    return pl.pallas_call(
        body, name="forget_bwd", grid=(nb,),
        out_shape=(jax.ShapeDtypeStruct((t, LANES), F32), jax.ShapeDtypeStruct((1, LANES), F32)),
        in_specs=[pl.BlockSpec((tb, LANES), rev), pl.BlockSpec((1, LANES), lambda i: (0, 0)), rev3, rev3],
        out_specs=(pl.BlockSpec((tb, LANES), rev), pl.BlockSpec((1, LANES), lambda i: (0, 0))),
        scratch_shapes=[pltpu.VMEM((1, LANES), F32)],
        compiler_params=_cparams("arbitrary"),
    )(rest, b128, d_fq, d_fk)


def _mla_prep(rest, gq, gkv, wq, wkv, cos, sin):
    t = rest.shape[0]
    tm = _row_tile(t)

    def body(r_ref, gq_ref, gkv_ref, wq_ref, wkv_ref, c_ref, s_ref, q_ref, k_ref, kv_ref, cq_ref, ckv_ref):
        cos_, sin_ = c_ref[...], s_ref[...]
        cq, _ = _rms(r_ref[:, REST_CQ:REST_CKV], gq_ref[...])
        ckv, _ = _rms(r_ref[:, REST_CKV:REST_KR], gkv_ref[...])
        cqb, ckvb = cq.astype(BF16), ckv.astype(BF16)
        cq_ref[...] = cqb
        ckv_ref[...] = ckvb
        k_rope = _rope(r_ref[:, REST_KR:REST_COLS], cos_, sin_)
        lo = _lane() < NOPE
        for h in range(HEADS):
            q_ref[h] = _rope(_dot(cqb, wq_ref[h]), cos_, sin_).astype(BF16)
            kv = _dot(ckvb, wkv_ref[h])
            kv_ref[h] = kv.astype(BF16)
            k_ref[h] = (jnp.where(lo, kv, 0.0) + k_rope).astype(BF16)

    row = lambda n: pl.BlockSpec((tm, n), lambda i: (i, 0))
    full = lambda a: pl.BlockSpec(a.shape, lambda i: (0,) * a.ndim)
    heads = pl.BlockSpec((HEADS, tm, LANES), lambda i: (0, i, 0))
    hshape = jax.ShapeDtypeStruct((HEADS, t, LANES), BF16)
    return pl.pallas_call(
        body, name="mla_prep", grid=(t // tm,),
        out_shape=(hshape, hshape, hshape, jax.ShapeDtypeStruct((t, Q_RANK), BF16), jax.ShapeDtypeStruct((t, KV_RANK), BF16)),
        in_specs=[row(REST_COLS), full(gq), full(gkv), full(wq), full(wkv), row(LANES), row(LANES)],
        out_specs=(heads, heads, heads, row(Q_RANK), row(KV_RANK)),
        compiler_params=_cparams("parallel"),
    )(rest, gq, gkv, wq, wkv, cos, sin)


def _pair_specs(fox, t, tq, blocked_q):
    if fox:
        blk = pl.BlockSpec((tq, LANES), lambda p, i: (i, p))
        whole = pl.BlockSpec((t, LANES), lambda p, i: (0, p))
    else:
        blk = pl.BlockSpec((2, tq, LANES), lambda p, i: (p, i, 0))
        whole = pl.BlockSpec((2, t, LANES), lambda p, i: (p, 0, 0))
    return [blk, whole, whole] if blocked_q else [whole, blk, blk]


def _tile_lanes(x, n):
    return jnp.tile(x, (1, n)) if n > 1 else x


class _Comm(NamedTuple):
    inputs: tuple
    out_shape: tuple
    aliases: dict
    scratch: tuple
    start: Callable
    finish: Callable


def _hosted_call(name, main, grid, args, in_specs, out_shape, out_specs, scratch, comm):
    n_in, n_out, n_scr = len(args), len(out_shape), len(scratch)
    c_in = list(comm.inputs) if comm else []
    c_out = list(comm.out_shape) if comm else []

    def body(*refs):
        bounds = [0, n_in, len(c_in), n_out, len(c_out), n_scr]
        starts = [sum(bounds[:k + 1]) for k in range(len(bounds))]
        ins, cins, outs, couts, scr = [refs[a:b] for a, b in zip(starts[:-1], starts[1:])]
        sems = refs[starts[-1]:]
        if comm:
            @pl.when(jnp.logical_and(pl.program_id(0) == 0, pl.program_id(1) == 0))
            def _():
                comm.start(cins, couts, sems)
        main(ins, outs, scr)
        if comm:
            @pl.when(jnp.logical_and(pl.program_id(0) == grid[0] - 1, pl.program_id(1) == grid[1] - 1))
            def _():
                comm.finish(cins, couts, sems)

    res = pl.pallas_call(
        body, name=name, grid=grid,
        out_shape=list(out_shape) + c_out,
        in_specs=list(in_specs) + _hbm_specs(len(c_in)),
        out_specs=list(out_specs) + _hbm_specs(len(c_out)),
        scratch_shapes=list(scratch) + (list(comm.scratch) if comm else []),
        input_output_aliases={n_in + i: n_out + o for i, o in comm.aliases.items()} if comm else {},
        compiler_params=_cparams("arbitrary", "arbitrary"),
    )(*args, *c_in)
    return res[:n_out], res[n_out:]


def _pair_heads(fox, q_ref, lo):
    hi = jnp.logical_not(lo)
    if fox:
        zero = jnp.zeros((), BF16)
        return [jnp.where(lo, q_ref[...], zero), jnp.where(hi, q_ref[...], zero)], [hi, lo]
    return [q_ref[0], q_ref[1]], [lo, lo]


def _stat_rows(x):
    return jnp.transpose(x)[0:8, :]


def _attn_fwd(fox, q, k, v, f2_rows=None, comm=None):
    t = q.shape[0] if fox else q.shape[1]
    tq = _row_tile(t)
    nq = t // tq
    nrep = tq // LANES
    c2 = (FOX_SCALE if fox else MLA_SCALE) * LOG2E

    def main(ins, outs, scr):
        q_ref, k_ref, v_ref = ins[:3]
        fr_ref = ins[3] if fox else None
        o_ref, lse_ref, lset_ref = outs
        m_sc, acc_sc = scr
        i = pl.program_id(1)
        lo = _lane() < HEAD_DIM
        causal = lax.broadcasted_iota(jnp.int32, (tq, tq), 0) >= lax.broadcasted_iota(jnp.int32, (tq, tq), 1)
        qs, sum_lanes = _pair_heads(fox, q_ref, lo)
        one = jnp.ones((), BF16)
        m_sc[...] = jnp.full_like(m_sc, -jnp.inf)
        acc_sc[...] = jnp.zeros_like(acc_sc)

        def rows_of(ref, j, hh):
            sl = pl.ds(pl.multiple_of(j * tq, tq), tq)
            return ref[sl, :] if fox else ref[hh, sl, :]

        def step(j, masked):
            for hh in range(2):
                s = _dot_nt(qs[hh], rows_of(k_ref, j, hh)) * c2
                if fox:
                    s = s - fr_ref[hh, j]
                if masked:
                    s = jnp.where(causal, s, -jnp.inf)
                m_prev = m_sc[hh]
                m_new = jnp.maximum(m_prev, jnp.max(s, axis=1, keepdims=True))
                p = jnp.exp2(s - _tile_lanes(m_new, nrep))
                vj = jnp.where(sum_lanes[hh], one, rows_of(v_ref, j, hh))
                acc_sc[hh] = jnp.exp2(m_prev - m_new) * acc_sc[hh] + _dot(p.astype(BF16), vj)
                m_sc[hh] = m_new

        def loop_body(j, carry):
            step(j, False)
            return carry

        lax.fori_loop(0, i, loop_body, 0)
        step(i, True)
        outs = []
        for hh in range(2):
            acc = acc_sc[hh]
            swapped = pltpu.roll(acc, HEAD_DIM, 1)
            outs.append(acc / swapped)
            lse2 = m_sc[hh] + jnp.log(jnp.where(sum_lanes[hh], acc, swapped)) * LOG2E
            lse_ref[hh] = lse2
            lset_ref[hh, 0] = _stat_rows(lse2)
        if fox:
            o_ref[...] = jnp.where(lo, outs[0], outs[1])
        else:
            o_ref[...] = jnp.where(lo, pltpu.roll(outs[0], HEAD_DIM, 1), outs[1])

    stat = pl.BlockSpec((2, tq, LANES), lambda p, i: (p, i, 0))
    stat_rows = pl.BlockSpec((2, 1, 8, tq), lambda p, i: (p, i, 0, 0))
    in_specs = _pair_specs(fox, t, tq, True)
    args = [q, k, v]
    if fox:
        in_specs += [pl.BlockSpec((2, nq, 1, tq), lambda p, i: (p, 0, 0, 0))]
        args += [f2_rows]
    return _hosted_call(
        "fox_attn_fwd" if fox else "mla_attn_fwd", main, (HEADS // 2, nq), args, in_specs,
        (jax.ShapeDtypeStruct((t, 4 * LANES), F32), jax.ShapeDtypeStruct((HEADS, t, LANES), F32),
         jax.ShapeDtypeStruct((HEADS, nq, 8, tq), F32)),
        (pl.BlockSpec((tq, LANES), lambda p, i: (i, p)), stat, stat_rows),
        [pltpu.VMEM((2, tq, LANES), F32), pltpu.VMEM((2, tq, LANES), F32)], comm)


def _head_do(fox, hh, do2, lo):
    if fox:
        return jnp.where(lo if hh == 0 else jnp.logical_not(lo), do2, 0.0)
    return jnp.where(lo, 0.0, pltpu.roll(do2, HEAD_DIM, 1) if hh == 0 else do2)


def _attn_bwd_dq(fox, q, k, v, do, o, lse, f2_rows=None, comm=None):
    t = q.shape[0] if fox else q.shape[1]
    tq = _row_tile(t)
    nq = t // tq
    nrep = tq // LANES
    scale = FOX_SCALE if fox else MLA_SCALE
    c2 = scale * LOG2E

    def main(ins, outs, scr):
        if fox:
            q_ref, k_ref, v_ref, fr_ref, do_ref, o_ref, lse_ref = ins
            dq_ref, dlt_ref, df_ref = outs
        else:
            q_ref, k_ref, v_ref, do_ref, o_ref, lse_ref = ins
            dq_ref, dlt_ref = outs
        acc_sc, = scr
        i = pl.program_id(1)
        lo = _lane() < HEAD_DIM
        causal = lax.broadcasted_iota(jnp.int32, (tq, tq), 0) >= lax.broadcasted_iota(jnp.int32, (tq, tq), 1)
        qs, sum_lanes = _pair_heads(fox, q_ref, lo)
        one = jnp.ones((), BF16)
        do2 = do_ref[...]
        prod = do2 * o_ref[...]
        dobs, deltas = [], []
        for hh in range(2):
            hmask = lo if hh == 0 else jnp.logical_not(lo)
            delta = jnp.broadcast_to(jnp.sum(jnp.where(hmask, prod, 0.0), axis=1, keepdims=True), (tq, LANES))
            dlt_ref[hh, 0] = _stat_rows(delta)
            deltas.append(delta)
            dobs.append(_head_do(fox, hh, do2, lo).astype(BF16))
        acc_sc[...] = jnp.zeros_like(acc_sc)

        def rows_of(ref, j, hh):
            sl = pl.ds(pl.multiple_of(j * tq, tq), tq)
            return ref[sl, :] if fox else ref[hh, sl, :]

        def step(j, masked):
            for hh in range(2):
                kj = rows_of(k_ref, j, hh)
                s = _dot_nt(qs[hh], kj) * c2
                if fox:
                    s = s - fr_ref[hh, j]
                if masked:
                    s = jnp.where(causal, s, -jnp.inf)
                p = jnp.exp2(s - _tile_lanes(lse_ref[hh], nrep))
                dp = _dot_nt(dobs[hh], rows_of(v_ref, j, hh))
                ds = p * (dp - _tile_lanes(deltas[hh], nrep))
                if fox:
                    kj = jnp.where(sum_lanes[hh], one, kj)
                acc_sc[hh] += _dot(ds.astype(BF16), kj)

        def loop_body(j, carry):
            step(j, False)
            return carry

        lax.fori_loop(0, i, loop_body, 0)
        step(i, True)
        if fox:
            dq_ref[...] = (jnp.where(lo, acc_sc[0], acc_sc[1]) * scale).astype(BF16)
            for hh in range(2):
                acc = acc_sc[hh]
                df_ref[hh] = jnp.where(sum_lanes[hh], acc, pltpu.roll(acc, HEAD_DIM, 1))
        else:
            dq_ref[0] = acc_sc[0] * scale
            dq_ref[1] = acc_sc[1] * scale

    stat = pl.BlockSpec((2, tq, LANES), lambda p, i: (p, i, 0))
    stat_rows = pl.BlockSpec((2, 1, 8, tq), lambda p, i: (p, i, 0, 0))
    pair = pl.BlockSpec((tq, LANES), lambda p, i: (i, p))
    in_specs = _pair_specs(fox, t, tq, True)
    args = [q, k, v]
    if fox:
        in_specs += [pl.BlockSpec((2, nq, 1, tq), lambda p, i: (p, 0, 0, 0))]
        args += [f2_rows]
    in_specs += [pair, pair, stat]
    args += [do, o, lse]
    rows_shape = jax.ShapeDtypeStruct((HEADS, nq, 8, tq), F32)
    if fox:
        out_shape = (jax.ShapeDtypeStruct((t, 4 * LANES), BF16), rows_shape, jax.ShapeDtypeStruct((HEADS, t, LANES), F32))
        out_specs = (pair, stat_rows, stat)
    else:
        out_shape = (jax.ShapeDtypeStruct((HEADS, t, LANES), F32), rows_shape)
        out_specs = (stat, stat_rows)
    return _hosted_call("fox_attn_bwd_dq" if fox else "mla_attn_bwd_dq", main, (HEADS // 2, nq), args, in_specs,
                        out_shape, out_specs, [pltpu.VMEM((2, tq, LANES), F32)], comm)


def _attn_bwd_dkv(fox, q, k, v, do, lse_rows, delta_rows, f2_rep=None, comm=None):
    t = q.shape[0] if fox else q.shape[1]
    tq = _row_tile(t)
    nq = t // tq
    nrep = tq // LANES
    scale = FOX_SCALE if fox else MLA_SCALE
    c2 = scale * LOG2E

    def main(ins, outs, scr):
        if fox:
            q_ref, k_ref, v_ref, f_ref, do_ref, lse_ref, dl_ref = ins
            dk_ref, dv_ref, df_ref = outs
        else:
            q_ref, k_ref, v_ref, do_ref, lse_ref, dl_ref = ins
            dkv_ref, dkr_ref = outs
        dk_sc, dv_sc = scr
        j = pl.program_id(1)
        lane = _lane()
        lo = lane < HEAD_DIM
        hi = jnp.logical_not(lo)
        causal = lax.broadcasted_iota(jnp.int32, (tq, tq), 1) >= lax.broadcasted_iota(jnp.int32, (tq, tq), 0)
        zero, one = jnp.zeros((), BF16), jnp.ones((), BF16)
        dk_sc[...] = jnp.zeros_like(dk_sc)
        dv_sc[...] = jnp.zeros_like(dv_sc)

        def step(i, masked):
            sl = pl.ds(pl.multiple_of(i * tq, tq), tq)
            do_i = do_ref[sl, :]
            for hh in range(2):
                kj = k_ref[...] if fox else k_ref[hh]
                vj = v_ref[...] if fox else v_ref[hh]
                qi = jnp.where(lo if hh == 0 else hi, q_ref[sl, :], zero) if fox else q_ref[hh, sl, :]
                dob = _head_do(fox, hh, do_i, lo).astype(BF16)
                st = _dot_nt(kj, qi) * c2
                if fox:
                    st = st - _tile_lanes(f_ref[hh], nrep)
                if masked:
                    st = jnp.where(causal, st, -jnp.inf)
                pt = jnp.exp2(st - lse_ref[hh, i, 0:1, :])
                dpt = _dot_nt(vj, dob)
                dst = pt * (dpt - dl_ref[hh, i, 0:1, :])
                dv_sc[hh] += _dot(pt.astype(BF16), dob)
                if fox:
                    qi = jnp.where(hi if hh == 0 else lo, one, qi)
                dk_sc[hh] += _dot(dst.astype(BF16), qi)

        def loop_body(i, carry):
            step(i, False)
            return carry

        step(j, True)
        lax.fori_loop(j + 1, nq, loop_body, 0)
        if fox:
            dk_ref[...] = (jnp.where(lo, dk_sc[0], dk_sc[1]) * scale).astype(BF16)
            dv_ref[...] = (dv_sc[0] + dv_sc[1]).astype(BF16)
            for hh in range(2):
                dk = dk_sc[hh]
                df_ref[hh] = -jnp.where(hi if hh == 0 else lo, dk, pltpu.roll(dk, HEAD_DIM, 1))
        else:
            rope_lanes = jnp.logical_and(lane >= NOPE, lane < NOPE + ROPE)
            dkr = jnp.zeros((tq, LANES), F32)
            for hh in range(2):
                dk = dk_sc[hh] * scale
                dkv_ref[hh] = jnp.where(lo, dk, dv_sc[hh])
                dkr = dkr + jnp.where(rope_lanes, dk, 0.0)
            dkr_ref[0] = dkr

    stat = pl.BlockSpec((2, tq, LANES), lambda p, j: (p, j, 0))
    rows4 = pl.BlockSpec((2, nq, 8, tq), lambda p, j: (p, 0, 0, 0))
    pair = pl.BlockSpec((tq, LANES), lambda p, j: (j, p))
    in_specs = _pair_specs(fox, t, tq, False)
    args = [q, k, v]
    if fox:
        in_specs += [stat]
        args += [f2_rep]
    in_specs += [pl.BlockSpec((t, LANES), lambda p, j: (0, p)), rows4, rows4]
    args += [do, lse_rows, delta_rows]
    if fox:
        out_shape = (jax.ShapeDtypeStruct((t, 4 * LANES), BF16), jax.ShapeDtypeStruct((t, 4 * LANES), BF16),
                     jax.ShapeDtypeStruct((HEADS, t, LANES), F32))
        out_specs = (pair, pair, stat)
    else:
        out_shape = (jax.ShapeDtypeStruct((HEADS, t, LANES), F32), jax.ShapeDtypeStruct((HEADS // 2, t, LANES), F32))
        out_specs = (stat, pl.BlockSpec((1, tq, LANES), lambda p, j: (p, j, 0)))
    return _hosted_call("fox_attn_bwd_dkv" if fox else "mla_attn_bwd_dkv", main, (HEADS // 2, nq), args, in_specs,
                        out_shape, out_specs, [pltpu.VMEM((2, tq, LANES), F32), pltpu.VMEM((2, tq, LANES), F32)], comm)


def _attn_out(x, fox_o, mla_o, gf, gm, w_o):
    t = x.shape[0]
    tm = _row_tile(t)

    def body(x_ref, f_ref, m_ref, gf_ref, gm_ref, w_ref, x1_ref, mix_ref):
        nf, _ = _rms(f_ref[...], gf_ref[...])
        nm, _ = _rms(m_ref[...], gm_ref[...])
        nfb, nmb = nf.astype(BF16), nm.astype(BF16)
        mix_ref[:, :FOX_WIDTH] = nfb
        mix_ref[:, FOX_WIDTH:] = nmb
        x1_ref[...] = x_ref[...] + _dot(nfb, w_ref[:FOX_WIDTH, :]) + _dot(nmb, w_ref[FOX_WIDTH:, :])

    row = lambda n: pl.BlockSpec((tm, n), lambda i: (i, 0))
    full = lambda a: pl.BlockSpec(a.shape, lambda i: (0,) * a.ndim)
    return pl.pallas_call(
        body, name="attn_out", grid=(t // tm,),
        out_shape=(jax.ShapeDtypeStruct((t, D_MODEL), F32), jax.ShapeDtypeStruct((t, D_MODEL), BF16)),
        in_specs=[row(D_MODEL), row(FOX_WIDTH), row(MLA_WIDTH), full(gf), full(gm), full(w_o)],
        out_specs=(row(D_MODEL), row(D_MODEL)),
        compiler_params=_cparams("parallel"),
    )(x, fox_o, mla_o, gf, gm, w_o)


def _mlp_fwd(x1, g_mlp, w_up, w_down, g_fin, target):
    t = x1.shape[0]
    tm = _row_tile(t)
    nf = w_up.shape[0]
    tf = w_up.shape[2]

    def body(x_ref, g_ref, wu_ref, wd_ref, gf_ref, t_ref, u_ref, h_ref, dx_ref, loss_ref, dg_ref, acc_sc, h_sc):
        i, f = pl.program_id(0), pl.program_id(1)

        @pl.when(jnp.logical_and(i == 0, f == 0))
        def _():
            loss_ref[...] = jnp.zeros_like(loss_ref)
            dg_ref[...] = jnp.zeros_like(dg_ref)

        @pl.when(f == 0)
        def _():
            h, _ = _rms(x_ref[...], g_ref[...])
            h_sc[...] = h.astype(BF16)
            h_ref[...] = h_sc[...]
            acc_sc[...] = jnp.zeros_like(acc_sc)

        u = _dot(h_sc[...], wu_ref[...])
        u_ref[...] = u
        r = jnp.maximum(u, 0.0)
        acc_sc[...] += _dot((r * r).astype(BF16), wd_ref[...])

        @pl.when(f == nf - 1)
        def _():
            x2 = x_ref[...] + acc_sc[...]
            y, r2 = _rms(x2, gf_ref[...])
            err = y - t_ref[...]
            loss_ref[...] += 0.5 * jnp.sum(jnp.mean(err * err, axis=-1, keepdims=True))
            dx, dg = _rms_bwd(x2, gf_ref[...], r2, err * (1.0 / D_MODEL))
            dx_ref[...] = dx
            dg_ref[...] += dg

    row = lambda n: pl.BlockSpec((tm, n), lambda i, f: (i, 0))
    vec = pl.BlockSpec((1, D_MODEL), lambda i, f: (0, 0))
    return pl.pallas_call(
        body, name="mlp_fwd", grid=(t // tm, nf),
        out_shape=(jax.ShapeDtypeStruct((t, D_FF), F32), jax.ShapeDtypeStruct((t, D_MODEL), BF16),
                   jax.ShapeDtypeStruct((t, D_MODEL), F32), jax.ShapeDtypeStruct((8, LANES), F32),
                   jax.ShapeDtypeStruct((1, D_MODEL), F32)),
        in_specs=[row(D_MODEL), vec, pl.BlockSpec((None, D_MODEL, tf), lambda i, f: (f, 0, 0)),
                  pl.BlockSpec((None, tf, D_MODEL), lambda i, f: (f, 0, 0)), vec, row(D_MODEL)],
        out_specs=(pl.BlockSpec((tm, tf), lambda i, f: (i, f)), row(D_MODEL), row(D_MODEL),
                   pl.BlockSpec((8, LANES), lambda i, f: (0, 0)), vec),
        scratch_shapes=[pltpu.VMEM((tm, D_MODEL), F32), pltpu.VMEM((tm, D_MODEL), BF16)],
        compiler_params=_cparams("arbitrary", "arbitrary"),
    )(x1, g_mlp, w_up, w_down, g_fin, target)


def _mlp_bwd(dx2, u, x1, g_mlp, w_up, w_down):
    t = x1.shape[0]
    tm = _row_tile(t)
    nf = w_up.shape[0]
    tf = w_up.shape[2]

    def body(dx_ref, u_ref, x_ref, g_ref, wu_ref, wd_ref, du_ref, a_ref, dx1_ref, dg_ref, acc_sc):
        i, f = pl.program_id(0), pl.program_id(1)

        @pl.when(jnp.logical_and(i == 0, f == 0))
        def _():
            dg_ref[...] = jnp.zeros_like(dg_ref)

        @pl.when(f == 0)
        def _():
            acc_sc[...] = jnp.zeros_like(acc_sc)

        r = jnp.maximum(u_ref[...], 0.0)
        a_ref[...] = (r * r).astype(BF16)
        da = _dot_nt(dx_ref[...].astype(BF16), wd_ref[...])
        du = (da * (2.0 * r)).astype(BF16)
        du_ref[...] = du
        acc_sc[...] += _dot_nt(du, wu_ref[...])

        @pl.when(f == nf - 1)
        def _():
            x = x_ref[...]
            _, r1 = _rms(x, g_ref[...])
            dx, dg = _rms_bwd(x, g_ref[...], r1, acc_sc[...])
            dx1_ref[...] = dx_ref[...] + dx
            dg_ref[...] += dg

    row = lambda n: pl.BlockSpec((tm, n), lambda i, f: (i, 0))
    vec = pl.BlockSpec((1, D_MODEL), lambda i, f: (0, 0))
    blk = pl.BlockSpec((tm, tf), lambda i, f: (i, f))
    return pl.pallas_call(
        body, name="mlp_bwd", grid=(t // tm, nf),
        out_shape=(jax.ShapeDtypeStruct((t, D_FF), BF16), jax.ShapeDtypeStruct((t, D_FF), BF16),
                   jax.ShapeDtypeStruct((t, D_MODEL), F32), jax.ShapeDtypeStruct((1, D_MODEL), F32)),
        in_specs=[row(D_MODEL), blk, row(D_MODEL), vec, pl.BlockSpec((None, D_MODEL, tf), lambda i, f: (f, 0, 0)),
                  pl.BlockSpec((None, tf, D_MODEL), lambda i, f: (f, 0, 0))],
        out_specs=(blk, blk, row(D_MODEL), vec),
        scratch_shapes=[pltpu.VMEM((tm, D_MODEL), F32)],
        compiler_params=_cparams("arbitrary", "arbitrary"),
    )(dx2, u, x1, g_mlp, w_up, w_down)


def _matmul_tn(name, a, b, batch_out=None):
    batched = b.ndim == 3
    t, m = a.shape[-2:]
    n = b.shape[-1]
    tk = _row_tile(t)
    bm = m if m <= 1024 else 512
    bn = (n if n <= 1024 else 512) if batch_out is None else n // batch_out
    lead = b.shape[0] if batched else 1

    def body(a_ref, b_ref, o_ref, acc_sc):
        kk = pl.program_id(3)

        @pl.when(kk == 0)
        def _():
            acc_sc[...] = jnp.zeros_like(acc_sc)

        acc_sc[...] += _dot_tn(a_ref[...].astype(BF16), b_ref[...].astype(BF16))

        @pl.when(kk == pl.num_programs(3) - 1)
        def _():
            o_ref[...] = acc_sc[...]

    if batched:
        a_spec = pl.BlockSpec((tk, bm), lambda h, i, j, kk: (kk, i))
        b_spec = pl.BlockSpec((None, tk, bn), lambda h, i, j, kk: (h, kk, j))
        o_spec = pl.BlockSpec((None, bm, bn), lambda h, i, j, kk: (h, i, j))
        o_shape = (lead, m, n)
    else:
        a_spec = pl.BlockSpec((tk, bm), lambda h, i, j, kk: (kk, i))
        b_spec = pl.BlockSpec((tk, bn), lambda h, i, j, kk: (kk, j))
        if batch_out is None:
            o_spec = pl.BlockSpec((bm, bn), lambda h, i, j, kk: (i, j))
            o_shape = (m, n)
        else:
            o_spec = pl.BlockSpec((None, bm, bn), lambda h, i, j, kk: (j, i, 0))
            o_shape = (batch_out, m, bn)
    return pl.pallas_call(
        body, name=name, grid=(lead, m // bm, n // bn, t // tk),
        out_shape=jax.ShapeDtypeStruct(o_shape, F32),
        in_specs=[a_spec, b_spec], out_specs=o_spec,
        scratch_shapes=[pltpu.VMEM((bm, bn), F32)],
        compiler_params=_cparams("parallel", "parallel", "parallel", "arbitrary"),
    )(a, b)


def _attn_out_bwd(dx1, fox_o, mla_o, gf, gm, w_o):
    t = dx1.shape[0]
    tm = _row_tile(t)

    def body(dx_ref, f_ref, m_ref, gf_ref, gm_ref, w_ref, df_ref, dm_ref, dgf_ref, dgm_ref):
        @pl.when(pl.program_id(0) == 0)
        def _():
            dgf_ref[...] = jnp.zeros_like(dgf_ref)
            dgm_ref[...] = jnp.zeros_like(dgm_ref)
        dxb = dx_ref[...].astype(BF16)
        for o_ref, g_ref, lo_row, d_ref, dg_ref in ((f_ref, gf_ref, 0, df_ref, dgf_ref), (m_ref, gm_ref, FOX_WIDTH, dm_ref, dgm_ref)):
            dn = _dot_nt(dxb, w_ref[lo_row:lo_row + FOX_WIDTH, :])
            o = o_ref[...]
            _, r = _rms(o, g_ref[...])
            d, dg = _rms_bwd(o, g_ref[...], r, dn)
            d_ref[...] = d
            dg_ref[...] += dg

    row = lambda n: pl.BlockSpec((tm, n), lambda i: (i, 0))
    full = lambda a: pl.BlockSpec(a.shape, lambda i: (0,) * a.ndim)
    vec = pl.BlockSpec((1, FOX_WIDTH), lambda i: (0, 0))
    o_shape = jax.ShapeDtypeStruct((t, FOX_WIDTH), F32)
    g_shape = jax.ShapeDtypeStruct((1, FOX_WIDTH), F32)
    return pl.pallas_call(
        body, name="attn_out_bwd", grid=(t // tm,),
        out_shape=(o_shape, o_shape, g_shape, g_shape),
        in_specs=[row(D_MODEL), row(FOX_WIDTH), row(MLA_WIDTH), full(gf), full(gm), full(w_o)],
        out_specs=(row(FOX_WIDTH), row(MLA_WIDTH), vec, vec),
        compiler_params=_cparams("arbitrary"),
    )(dx1, fox_o, mla_o, gf, gm, w_o)


def _mla_prep_bwd(dq, dkv, dkr, dz, rest, gq, gkv, wq, wkv, cos, sin):
    t = rest.shape[0]
    tm = _row_tile(t)

    def body(dq_ref, dkv_ref, dkr_ref, dz_ref, r_ref, gq_ref, gkv_ref, wq_ref, wkv_ref, c_ref, s_ref,
             dr_ref, dqp_ref, dkvb_ref, dgq_ref, dgkv_ref):
        @pl.when(pl.program_id(0) == 0)
        def _():
            dgq_ref[...] = jnp.zeros_like(dgq_ref)
            dgkv_ref[...] = jnp.zeros_like(dgkv_ref)
        cos_, sin_ = c_ref[...], s_ref[...]
        dcq = jnp.zeros((tm, Q_RANK), F32)
        dckv = jnp.zeros((tm, KV_RANK), F32)
        for h in range(HEADS):
            dqp = _rope_bwd(dq_ref[h], cos_, sin_).astype(BF16)
            dqp_ref[h] = dqp
            dcq = dcq + _dot_nt(dqp, wq_ref[h])
            dkvb = dkv_ref[h].astype(BF16)
            dkvb_ref[h] = dkvb
            dckv = dckv + _dot_nt(dkvb, wkv_ref[h])
        dkrope = dkr_ref[0]
        for pr in range(1, HEADS // 2):
            dkrope = dkrope + dkr_ref[pr]
        cq = r_ref[:, REST_CQ:REST_CKV]
        _, rq = _rms(cq, gq_ref[...])
        d_cq, dgq = _rms_bwd(cq, gq_ref[...], rq, dcq)
        ckv = r_ref[:, REST_CKV:REST_KR]
        _, rkv = _rms(ckv, gkv_ref[...])
        d_ckv, dgkv = _rms_bwd(ckv, gkv_ref[...], rkv, dckv)
        dgq_ref[...] += dgq
        dgkv_ref[...] += dgkv
        dr_ref[:, 0:REST_CQ] = dz_ref[...].astype(BF16)
        dr_ref[:, REST_CQ:REST_CKV] = d_cq.astype(BF16)
        dr_ref[:, REST_CKV:REST_KR] = d_ckv.astype(BF16)
        dr_ref[:, REST_KR:REST_COLS] = _rope_bwd(dkrope, cos_, sin_).astype(BF16)

    row = lambda n: pl.BlockSpec((tm, n), lambda i: (i, 0))
    full = lambda a: pl.BlockSpec(a.shape, lambda i: (0,) * a.ndim)
    heads = pl.BlockSpec((HEADS, tm, LANES), lambda i: (0, i, 0))
    hshape = jax.ShapeDtypeStruct((HEADS, t, LANES), BF16)
    return pl.pallas_call(
        body, name="mla_prep_bwd", grid=(t // tm,),
        out_shape=(jax.ShapeDtypeStruct((t, REST_COLS), BF16), hshape, hshape,
                   jax.ShapeDtypeStruct((1, Q_RANK), F32), jax.ShapeDtypeStruct((1, KV_RANK), F32)),
        in_specs=[heads, heads, pl.BlockSpec((HEADS // 2, tm, LANES), lambda i: (0, i, 0)), row(LANES), row(REST_COLS),
                  full(gq), full(gkv), full(wq), full(wkv), row(LANES), row(LANES)],
        out_specs=(row(REST_COLS), heads, heads, pl.BlockSpec((1, Q_RANK), lambda i: (0, 0)),
                   pl.BlockSpec((1, KV_RANK), lambda i: (0, 0))),
        compiler_params=_cparams("arbitrary"),
    )(dq, dkv, dkr, dz, rest, gq, gkv, wq, wkv, cos, sin)


def _in_proj_bwd(x, g, dx1, dfq, dfk, dfv, drest, w_qkv, w_rest):
    t = x.shape[0]
    tm = _row_tile(t)

    def body(x_ref, g_ref, dx1_ref, dq_ref, dk_ref, dv_ref, dr_ref, wq_ref, wr_ref, dx_ref, dg_ref):
        @pl.when(pl.program_id(0) == 0)
        def _():
            dg_ref[...] = jnp.zeros_like(dg_ref)
        dh = _dot(dr_ref[...], wr_ref[...])
        for n, ref in enumerate((dq_ref, dk_ref, dv_ref)):
            dh = dh + _dot(ref[...], wq_ref[n * FOX_WIDTH:(n + 1) * FOX_WIDTH, :])
        xv = x_ref[...]
        _, r = _rms(xv, g_ref[...])
        dx, dg = _rms_bwd(xv, g_ref[...], r, dh)
        dx_ref[...] = dx1_ref[...] + dx
        dg_ref[...] += dg

    row = lambda n: pl.BlockSpec((tm, n), lambda i: (i, 0))
    full = lambda a: pl.BlockSpec(a.shape, lambda i: (0,) * a.ndim)
    vec = pl.BlockSpec((1, D_MODEL), lambda i: (0, 0))
    return pl.pallas_call(
        body, name="in_proj_bwd", grid=(t // tm,),
        out_shape=(jax.ShapeDtypeStruct((t, D_MODEL), F32), jax.ShapeDtypeStruct((1, D_MODEL), F32)),
        in_specs=[row(D_MODEL), full(g), row(D_MODEL), row(FOX_WIDTH), row(FOX_WIDTH), row(FOX_WIDTH), row(REST_COLS),
                  full(w_qkv), full(w_rest)],
        out_specs=(row(D_MODEL), vec),
        compiler_params=_cparams("arbitrary"),
    )(x, g, dx1, dfq, dfk, dfv, drest, w_qkv, w_rest)


def _pad_cols(a, n):
    return jnp.pad(a, ((0, 0),) * (a.ndim - 1) + ((0, n - a.shape[-1]),))


def kernel(x, positions, attn_norm_g, w_in, b_forget, q_norm_g, w_uq, kv_norm_g, w_ukv, fox_out_g, mla_out_g, w_o, mlp_norm_g, w_up, w_down, final_norm_g, loss_target, m_attn_norm_g, m_w_in, m_b_forget, m_q_norm_g, m_w_uq, m_kv_norm_g, m_w_ukv, m_fox_out_g, m_mla_out_g, m_w_o, m_mlp_norm_g, m_w_up, m_w_down, m_final_norm_g, v_attn_norm_g, v_w_in, v_b_forget, v_q_norm_g, v_w_uq, v_kv_norm_g, v_w_ukv, v_fox_out_g, v_mla_out_g, v_w_o, v_mlp_norm_g, v_w_up, v_w_down, v_final_norm_g):
    t = x.shape[1]
    tq = _row_tile(t)
    xs = x[0]
    target = loss_target[0]

    early = [jnp.transpose(w_in[0]), _pad_cols(w_uq[0], LANES), w_ukv[0]]
    late = [w_o[0].astype(BF16), w_up[0].astype(BF16), w_down[0].astype(BF16)]
    g_in, wq, wkv = _all_gather([s.astype(BF16) for s in early])
    win = g_in.reshape(IN_COLS, D_MODEL)
    off_ff, off_cq, off_kr = 3 * FOX_WIDTH, 3 * FOX_WIDTH + HEADS, IN_COLS - ROPE
    zeros = lambda n: jnp.zeros((n, D_MODEL), BF16)
    w_qkv = win[:off_ff]
    w_rest = jnp.concatenate([
        win[off_ff:off_cq], zeros(REST_CQ - HEADS), win[off_cq:off_kr],
        zeros(NOPE), win[off_kr:], zeros(LANES - NOPE - ROPE)], axis=0)

    cos, sin = _rope_tables(positions.reshape(t, 1))
    h1, fq, fk, fv, rest = _in_proj(xs, attn_norm_g, w_qkv, w_rest)
    b128 = _pad_cols(b_forget, LANES)
    f2_rows, f2_rep = _forget_cumsum(rest, b128)
    f2_rows = f2_rows.reshape(HEADS, t // tq, 1, tq)
    (fox_o, fox_lse, fox_lse_rows), partly = _attn_fwd(True, fq, fk, fv, f2_rows, comm=_ag_direct(late))
    mq, mk, mkv, cqn, ckvn = _mla_prep(rest, q_norm_g, kv_norm_g, wq, wkv, cos, sin)
    (mla_o, mla_lse, mla_lse_rows), (g_o, g_up, g_down) = _attn_fwd(False, mq, mk, mkv, comm=_ag_forward(partly))
    wo = g_o.reshape(D_MODEL, D_MODEL)
    x1, mixed = _attn_out(xs, fox_o, mla_o, fox_out_g, mla_out_g, wo)
    u, h2, dx2, loss8, d_gfin = _mlp_fwd(x1, mlp_norm_g, g_up, g_down, final_norm_g.reshape(1, D_MODEL), target)

    du, act, dx1, d_gmlp = _mlp_bwd(dx2, u, x1, mlp_norm_g, g_up, g_down)
    dw_down = _matmul_tn("dw_down", act, dx2)
    dw_up = _matmul_tn("dw_up", h2, du, batch_out=N_DEV)
    dfox_o, dmla_o, d_gfox, d_gmla = _attn_out_bwd(dx1, fox_o, mla_o, fox_out_g, mla_out_g, wo)
    dw_o = _matmul_tn("dw_o", mixed, dx1)

    core = lax.axis_index("c").astype(jnp.int32).reshape(1)
    chip = (2 * lax.axis_index("x") + lax.axis_index("y")).astype(jnp.int32).reshape(1)
    names = ("w_in", "w_uq", "w_ukv", "w_o", "w_up", "w_down")
    grads_b = [dw_o.reshape(N_DEV, -1, D_MODEL), dw_up, dw_down.reshape(N_DEV, -1, D_MODEL)]
    (dfq, fox_delta_rows, d_fq), got_b = _attn_bwd_dq(True, fq, fk, fv, dfox_o, fox_o, fox_lse, f2_rows,
                                                      comm=_rs_to_sibling(grads_b))
    sums_b = [_rs_sibling_sum("rs_sibling_sum_" + nm, g, l, core) for nm, g, l in zip(names[3:], grads_b, got_b)]
    (dfk, dfv, d_fk), others_b = _attn_bwd_dkv(True, fq, fk, fv, dfox_o, fox_lse_rows, fox_delta_rows, f2_rep,
                                               comm=_rs_to_chips([s[1] for s in sums_b]))
    dz, d_b = _forget_bwd(rest, b128, d_fq, d_fk)

    (dmq, mla_delta_rows), _ = _attn_bwd_dq(False, mq, mk, mkv, dmla_o, mla_o, mla_lse)
    (dmkv, dmkr), _ = _attn_bwd_dkv(False, mq, mk, mkv, dmla_o, mla_lse_rows, mla_delta_rows)
    drest, dqp, dkvb, d_gq, d_gkv = _mla_prep_bwd(dmq, dmkv, dmkr, dz, rest, q_norm_g, kv_norm_g, wq, wkv, cos, sin)
    dw_uq = _matmul_tn("dw_uq", cqn, dqp)
    dw_ukv = _matmul_tn("dw_ukv", ckvn, dkvb)
    grad_x, d_gattn = _in_proj_bwd(xs, attn_norm_g, dx1, dfq, dfk, dfv, drest, w_qkv, w_rest)
    dw_q = _matmul_tn("dw_in_q", dfq, h1)
    dw_k = _matmul_tn("dw_in_k", dfk, h1)
    dw_v = _matmul_tn("dw_in_v", dfv, h1)
    dw_r = _matmul_tn("dw_in_rest", drest, h1)

    dw_in = jnp.concatenate([dw_q, dw_k, dw_v, dw_r[0:HEADS], dw_r[REST_CQ:REST_KR],
                             dw_r[REST_KR + NOPE:REST_KR + NOPE + ROPE]], axis=0)
    grads_a = [dw_in.reshape(N_DEV, IN_SHARD, D_MODEL), dw_uq, dw_ukv]
    got_a = _comm_call("rs_sibling_exchange", _rs_to_sibling(grads_a))
    sums_a = [_rs_sibling_sum("rs_sibling_sum_" + nm, g, l, core) for nm, g, l in zip(names[:3], grads_a, got_a)]
    others_a = _comm_call("rs_chip_exchange", _rs_to_chips([s[1] for s in sums_a]))
    sums, others = sums_a + sums_b, list(others_a) + list(others_b)
    sharded = (w_in, w_uq, w_ukv, w_o, w_up, w_down)
    moments_m = (m_w_in, m_w_uq, m_w_ukv, m_w_o, m_w_up, m_w_down)
    moments_v = (v_w_in, v_w_uq, v_w_ukv, v_w_o, v_w_up, v_w_down)
    g_in_t = _rs_final_sum("rs_final_sum_w_in", sums[0][0], others[0], chip)
    big = [_adamw_given("adamw_w_in", jnp.transpose(g_in_t), w_in, m_w_in, v_w_in)]
    for a in range(1, len(names)):
        big.append(_adamw_sharded("adamw_" + names[a], sharded[a], moments_m[a], moments_v[a], sums[a][0], others[a], chip))
    big_g, big_d, big_m, big_v = [[b[k] for b in big] for k in range(4)]

    as_row = lambda a: a.reshape(1, -1)
    small_w = (attn_norm_g, b_forget, q_norm_g, kv_norm_g, fox_out_g, mla_out_g, mlp_norm_g, final_norm_g)
    small_m = (m_attn_norm_g, m_b_forget, m_q_norm_g, m_kv_norm_g, m_fox_out_g, m_mla_out_g, m_mlp_norm_g, m_final_norm_g)
    small_v = (v_attn_norm_g, v_b_forget, v_q_norm_g, v_kv_norm_g, v_fox_out_g, v_mla_out_g, v_mlp_norm_g, v_final_norm_g)
    total = _small_all_reduce([d_gattn, d_b, d_gq, d_gkv, d_gfox, d_gmla, d_gmlp, d_gfin], loss8)
    small = _adamw_small(total, [as_row(a) for a in small_w], [as_row(a) for a in small_m], [as_row(a) for a in small_v])
    loss = small[0].reshape(())
    s_g, s_d, s_m, s_v = [[small[1 + 4 * r + k].reshape(small_w[r].shape) for r in range(len(small_w))] for k in range(4)]

    def ordered(small_, bigs):
        ga, bf, gq_, gkv_, gfo, gml, gmlp_, gfin_ = small_
        bin_, buq, bukv, bo, bup, bdown = bigs
        return [ga, bin_, bf, gq_, buq, gkv_, bukv, gfo, gml, bo, gmlp_, bup, bdown, gfin_]

    return (loss, grad_x[None], *ordered(s_g, big_g), *ordered(s_d, big_d), *ordered(s_m, big_m), *ordered(s_v, big_v))
```

```python
import math
from typing import Callable, NamedTuple

import numpy as np
import jax
import jax.numpy as jnp
from jax import lax
from jax.experimental import pallas as pl
from jax.experimental.pallas import tpu as pltpu

F32 = jnp.float32
BF16 = jnp.bfloat16
MESH = pl.DeviceIdType.MESH

D_MODEL = 1024
HEADS = 8
HEAD_DIM = 64
FOX_WIDTH = 512
MLA_WIDTH = 512
NOPE = 64
ROPE = 32
QK_DIM = 96
Q_RANK = 384
KV_RANK = 256
D_FF = 4096
IN_COLS = 2216
ROPE_THETA = 10000.0
EPS = 1e-6
FOX_SCALE = 1.0 / math.sqrt(HEAD_DIM)
MLA_SCALE = 1.0 / math.sqrt(QK_DIM)
ADAM_LR = 0.001
ADAM_B1 = 0.9
ADAM_B2 = 0.999
ADAM_EPS = 1e-08
ADAM_WD = 0.01
ADAM_STEP = 10

N_DEV = 8
LANES = 128
REST_COLS = 896
REST_CQ = LANES
REST_CKV = REST_CQ + Q_RANK
REST_KR = REST_CKV + KV_RANK
LOG2E = 1.4426950408889634
VMEM_LIMIT = 56 * 1024 * 1024

IN_SHARD = IN_COLS // N_DEV
SMALL_SIZES = (1024, 8, 384, 256, 512, 512, 1024, 1024)
SMALL_ROWS = 16
LOSS_ROW = len(SMALL_SIZES)


def _cparams(*sem):
    return pltpu.CompilerParams(dimension_semantics=sem or None, vmem_limit_bytes=VMEM_LIMIT)


def _row_tile(t):
    return 512 if t >= 2048 else 128


def _dot(a, b):
    return jnp.dot(a, b, preferred_element_type=F32)


def _dot_nt(a, b):
    return lax.dot_general(a, b, (((1,), (1,)), ((), ())), preferred_element_type=F32)


def _dot_tn(a, b):
    return lax.dot_general(a, b, (((0,), (0,)), ((), ())), preferred_element_type=F32)


def _rms(x, g):
    r = lax.rsqrt(jnp.mean(x * x, axis=-1, keepdims=True) + EPS)
    return x * r * g, r


def _rms_bwd(x, g, r, dy):
    xh = x * r
    gdy = dy * g
    dx = r * (gdy - xh * jnp.mean(gdy * xh, axis=-1, keepdims=True))
    return dx, jnp.sum(dy * xh, axis=0, keepdims=True)


def _lane():
    return lax.broadcasted_iota(jnp.int32, (1, LANES), 1)


def _rot(x):
    lane = _lane()
    half = NOPE + ROPE // 2
    first = jnp.logical_and(lane >= NOPE, lane < half)
    second = jnp.logical_and(lane >= half, lane < NOPE + ROPE)
    return jnp.where(first, -pltpu.roll(x, LANES - ROPE // 2, 1), jnp.where(second, pltpu.roll(x, ROPE // 2, 1), 0.0))


def _rope(x, cos, sin):
    return x * cos + _rot(x) * sin


def _rope_bwd(dy, cos, sin):
    return dy * cos - _rot(dy * sin)


def _remote(src, dst, send_sem, recv_sem, to):
    return pltpu.make_async_remote_copy(src_ref=src, dst_ref=dst, send_sem=send_sem, recv_sem=recv_sem,
                                        device_id=to, device_id_type=MESH)


def _hbm_specs(n):
    return [pl.BlockSpec(memory_space=pl.ANY)] * n


def _all_gather(blocks):
    n = len(blocks)

    def body(*refs):
        x_refs, out_refs = refs[:n], refs[n:2 * n]
        send_sems, recv_sems, local_sems = refs[2 * n:]
        x, y, c = lax.axis_index("x"), lax.axis_index("y"), lax.axis_index("c")
        me, sibling = (x, y, c), (x, y, 1 - c)
        chips = [(1 - x, y), (x, 1 - y), (1 - x, 1 - y)]

        def slot(a, px, py, pc):
            return out_refs[a].at[4 * px + 2 * py + pc]

        def copy(a, k, blk, to, src=None):
            return _remote(slot(a, *blk) if src is None else src, slot(a, *blk),
                           send_sems.at[7 * a + k], recv_sems.at[7 * a + k], to)

        mine = [pltpu.make_async_copy(x_refs[a], slot(a, *me), local_sems.at[a]) for a in range(n)]
        first, passed = [], []
        for a in range(n):
            mine[a].start()
            first.append(copy(a, 0, me, sibling, src=x_refs[a]))
            first += [copy(a, 1 + j, me, (*chip, c), src=x_refs[a]) for j, chip in enumerate(chips)]
        for cp in first:
            cp.start()
        for a in range(n):
            for j, chip in enumerate(chips):
                copy(a, 1 + j, (*chip, c), me).wait_recv()
                passed.append(copy(a, 4 + j, (*chip, c), sibling))
                passed[-1].start()
        for a in range(n):
            copy(a, 0, sibling, me).wait_recv()
            for j, chip in enumerate(chips):
                copy(a, 4 + j, (*chip, 1 - c), me).wait_recv()
        for cp in first + passed:
            cp.wait_send()
        for cp in mine:
            cp.wait()

    return pl.pallas_call(
        body, name="all_gather_weights",
        out_shape=[jax.ShapeDtypeStruct((N_DEV,) + b.shape, b.dtype) for b in blocks],
        in_specs=_hbm_specs(n), out_specs=_hbm_specs(n),
        scratch_shapes=[pltpu.SemaphoreType.DMA((7 * n,)), pltpu.SemaphoreType.DMA((7 * n,)), pltpu.SemaphoreType.DMA((n,))],
    )(*blocks)


def _symmetric_comm(inputs, out_shape, aliases, per_array, copies):
    def start(in_refs, out_refs, sems):
        for cp in copies(in_refs, out_refs, *sems):
            cp.start()

    def finish(in_refs, out_refs, sems):
        for cp in copies(in_refs, out_refs, *sems):
            cp.wait()

    n_sems = per_array * len(inputs)
    return _Comm(tuple(inputs), tuple(out_shape), aliases,
                 (pltpu.SemaphoreType.DMA((n_sems,)), pltpu.SemaphoreType.DMA((n_sems,))), start, finish)


def _ag_direct(shards):
    def copies(in_refs, out_refs, send_sems, recv_sems):
        x, y, c = lax.axis_index("x"), lax.axis_index("y"), lax.axis_index("c")
        peers = [(x, y, 1 - c), (1 - x, y, c), (x, 1 - y, c), (1 - x, 1 - y, c)]
        cps = []
        for a in range(len(shards)):
            mine = out_refs[a].at[4 * x + 2 * y + c]
            cps.append(pltpu.make_async_copy(in_refs[a], mine, send_sems.at[5 * a]))
            cps += [_remote(in_refs[a], mine, send_sems.at[5 * a + k], recv_sems.at[5 * a + k], peer)
                    for k, peer in enumerate(peers, start=1)]
        return cps

    return _symmetric_comm(shards, [jax.ShapeDtypeStruct((N_DEV,) + s.shape, s.dtype) for s in shards], {}, 5, copies)


def _ag_forward(gathered):
    def copies(in_refs, out_refs, send_sems, recv_sems):
        x, y, c = lax.axis_index("x"), lax.axis_index("y"), lax.axis_index("c")
        chips = [(1 - x, y), (x, 1 - y), (1 - x, 1 - y)]
        return [_remote(out_refs[a].at[4 * cx + 2 * cy + c], out_refs[a].at[4 * cx + 2 * cy + c],
                        send_sems.at[3 * a + j], recv_sems.at[3 * a + j], (x, y, 1 - c))
                for a in range(len(gathered)) for j, (cx, cy) in enumerate(chips)]

    shapes = [jax.ShapeDtypeStruct(g.shape, g.dtype) for g in gathered]
    return _symmetric_comm(gathered, shapes, {a: a for a in range(len(gathered))}, 3, copies)


def _rs_to_sibling(grads):
    def copies(in_refs, out_refs, send_sems, recv_sems):
        x, y, c = lax.axis_index("x"), lax.axis_index("y"), lax.axis_index("c")
        return [_remote(in_refs[a].at[2 * q + 1 - c], out_refs[a].at[q], send_sems.at[4 * a + q], recv_sems.at[4 * a + q], (x, y, 1 - c))
                for a in range(len(grads)) for q in range(4)]

    return _symmetric_comm(grads, [jax.ShapeDtypeStruct((4,) + g.shape[1:], g.dtype) for g in grads], {}, 4, copies)


def _rs_to_chips(parts):
    def copies(in_refs, out_refs, send_sems, recv_sems):
        x, y, c = lax.axis_index("x"), lax.axis_index("y"), lax.axis_index("c")
        chips = [(1 - x, y), (x, 1 - y), (1 - x, 1 - y)]
        return [_remote(in_refs[a].at[2 * cx + cy], out_refs[a].at[k], send_sems.at[3 * a + k], recv_sems.at[3 * a + k], (cx, cy, c))
                for a in range(len(parts)) for k, (cx, cy) in enumerate(chips)]

    return _symmetric_comm(parts, [jax.ShapeDtypeStruct((3,) + p.shape[1:], p.dtype) for p in parts], {}, 3, copies)


def _comm_call(name, comm):
    n_in, n_out = len(comm.inputs), len(comm.out_shape)

    def body(*refs):
        ins, outs, sems = refs[:n_in], refs[n_in:n_in + n_out], refs[n_in + n_out:]
        comm.start(ins, outs, sems)
        comm.finish(ins, outs, sems)

    return pl.pallas_call(
        body, name=name, out_shape=list(comm.out_shape), in_specs=_hbm_specs(n_in), out_specs=_hbm_specs(n_out),
        scratch_shapes=list(comm.scratch), input_output_aliases=dict(comm.aliases),
    )(*comm.inputs)


def _small_all_reduce(parts, loss8):
    n = len(parts)

    def body(*refs):
        p_refs, loss_ref, out_ref, pack, land, send_sems, recv_sems = refs[:n], *refs[n:]
        x, y, c = lax.axis_index("x"), lax.axis_index("y"), lax.axis_index("c")
        me = 4 * x + 2 * y + c
        pack[...] = jnp.zeros_like(pack)
        for r, ref in enumerate(p_refs):
            pack[r:r + 1, 0:ref.shape[1]] = ref[...]
        pack[LOSS_ROW:LOSS_ROW + 1, 0:LANES] = loss_ref[0:1, :]
        land[me] = pack[...]
        cps = []
        for k in range(1, N_DEV):
            peer = (x ^ (k >> 2), y ^ ((k >> 1) & 1), c ^ (k & 1))
            cps.append(_remote(pack, land.at[me], send_sems.at[k - 1], recv_sems.at[k - 1], peer))
        for cp in cps:
            cp.start()
        for cp in cps:
            cp.wait()
        acc = land[0]
        for d in range(1, N_DEV):
            acc = acc + land[d]
        out_ref[...] = acc

    vmem = pl.BlockSpec(memory_space=pltpu.VMEM)
    return pl.pallas_call(
        body, name="small_all_reduce",
        out_shape=jax.ShapeDtypeStruct((SMALL_ROWS, D_MODEL), F32),
        in_specs=[vmem] * (n + 1), out_specs=vmem,
        scratch_shapes=[pltpu.VMEM((SMALL_ROWS, D_MODEL), F32), pltpu.VMEM((N_DEV, SMALL_ROWS, D_MODEL), F32),
                        pltpu.SemaphoreType.DMA((N_DEV - 1,)), pltpu.SemaphoreType.DMA((N_DEV - 1,))],
    )(*parts, loss8)


def _rs_sibling_sum(name, grad, got, core):
    _, rows, cols = grad.shape

    def body(c_ref, g_ref, l_ref, f_ref, b_ref):
        s = g_ref[...] + l_ref[...]
        f_ref[...] = s
        b_ref[...] = s.astype(BF16)

    by_chip = pl.BlockSpec((None, rows, cols), lambda q, c_ref: (q, 0, 0))
    return pl.pallas_call(
        body, name=name,
        grid_spec=pltpu.PrefetchScalarGridSpec(
            num_scalar_prefetch=1, grid=(4,),
            in_specs=[pl.BlockSpec((None, rows, cols), lambda q, c_ref: (2 * q + c_ref[0], 0, 0)), by_chip],
            out_specs=[by_chip, by_chip]),
        out_shape=(jax.ShapeDtypeStruct((4, rows, cols), F32), jax.ShapeDtypeStruct((4, rows, cols), BF16)),
        compiler_params=_cparams("parallel"),
    )(core, grad, got)


def _adamw_math(w, g, m, v):
    m2 = ADAM_B1 * m + (1.0 - ADAM_B1) * g
    v2 = ADAM_B2 * v + (1.0 - ADAM_B2) * (g * g)
    m_hat = m2 / (1.0 - ADAM_B1 ** ADAM_STEP)
    v_hat = v2 / (1.0 - ADAM_B2 ** ADAM_STEP)
    delta = -ADAM_LR * (m_hat / (jnp.sqrt(v_hat) + ADAM_EPS) + ADAM_WD * w)
    return delta, m2, v2


def _update_tile(rows):
    return 256 if rows % 256 == 0 else rows


def _rs_final_sum(name, chip_sums, got, chip):
    _, rows, cols = chip_sums.shape

    def body(q_ref, o_ref, r_ref, g_out):
        g = o_ref[...]
        for k in range(3):
            g = g + r_ref[k].astype(F32)
        g_out[...] = g

    return pl.pallas_call(
        body, name=name,
        grid_spec=pltpu.PrefetchScalarGridSpec(
            num_scalar_prefetch=1, grid=(1,),
            in_specs=[pl.BlockSpec((None, rows, cols), lambda i, q_ref: (q_ref[0], 0, 0)),
                      pl.BlockSpec((3, rows, cols), lambda i, q_ref: (0, 0, 0))],
            out_specs=pl.BlockSpec((rows, cols), lambda i, q_ref: (0, 0))),
        out_shape=jax.ShapeDtypeStruct((rows, cols), F32),
    )(chip, chip_sums, got)


def _adamw_sharded(name, w, m, v, chip_sums, got, chip):
    _, rows, cols = w.shape
    tr = _update_tile(rows)

    def body(q_ref, o_ref, r_ref, w_ref, m_ref, v_ref, g_out, d_out, m_out, v_out):
        g = o_ref[:, 0:cols]
        for k in range(3):
            g = g + r_ref[k, :, 0:cols].astype(F32)
        d, m2, v2 = _adamw_math(w_ref[0], g, m_ref[0], v_ref[0])
        g_out[0] = g
        d_out[0] = d
        m_out[0] = m2
        v_out[0] = v2

    own = pl.BlockSpec((1, tr, cols), lambda i, q_ref: (0, i, 0))
    shp = jax.ShapeDtypeStruct(w.shape, F32)
    wide = chip_sums.shape[2]
    return pl.pallas_call(
        body, name=name,
        grid_spec=pltpu.PrefetchScalarGridSpec(
            num_scalar_prefetch=1, grid=(rows // tr,),
            in_specs=[pl.BlockSpec((None, tr, wide), lambda i, q_ref: (q_ref[0], i, 0)),
                      pl.BlockSpec((3, tr, wide), lambda i, q_ref: (0, i, 0)), own, own, own],
            out_specs=[own] * 4),
        out_shape=(shp,) * 4,
        compiler_params=_cparams("parallel"),
    )(chip, chip_sums, got, w, m, v)


def _adamw_given(name, g, w, m, v):
    _, rows, cols = w.shape
    tr = _update_tile(rows)

    def body(g_ref, w_ref, m_ref, v_ref, g_out, d_out, m_out, v_out):
        g = g_ref[...]
        d, m2, v2 = _adamw_math(w_ref[0], g, m_ref[0], v_ref[0])
        g_out[0] = g
        d_out[0] = d
        m_out[0] = m2
        v_out[0] = v2

    own = pl.BlockSpec((1, tr, cols), lambda i: (0, i, 0))
    shp = jax.ShapeDtypeStruct(w.shape, F32)
    return pl.pallas_call(
        body, name=name, grid=(rows // tr,), out_shape=(shp,) * 4,
        in_specs=[pl.BlockSpec((tr, cols), lambda i: (i, 0)), own, own, own], out_specs=[own] * 4,
        compiler_params=_cparams("parallel"),
    )(g, w, m, v)


def _adamw_small(total, ws, ms, vs):
    n = len(ws)

    def body(*refs):
        t_ref = refs[0]
        w_refs, m_refs, v_refs = refs[1:1 + n], refs[1 + n:1 + 2 * n], refs[1 + 2 * n:1 + 3 * n]
        outs = refs[1 + 3 * n:]
        outs[0][...] = t_ref[LOSS_ROW:LOSS_ROW + 1, 0:1]
        for r in range(n):
            g = t_ref[r:r + 1, 0:w_refs[r].shape[1]]
            d, m2, v2 = _adamw_math(w_refs[r][...], g, m_refs[r][...], v_refs[r][...])
            for k, val in enumerate((g, d, m2, v2)):
                outs[1 + 4 * r + k][...] = val

    vmem = pl.BlockSpec(memory_space=pltpu.VMEM)
    out_shape = [jax.ShapeDtypeStruct((1, 1), F32)]
    for w in ws:
        out_shape += [jax.ShapeDtypeStruct(w.shape, F32)] * 4
    return pl.pallas_call(
        body, name="adamw_small", out_shape=out_shape,
        in_specs=[vmem] * (1 + 3 * n), out_specs=[vmem] * len(out_shape),
    )(total, *ws, *ms, *vs)


def _rope_tables(pos_col):
    t = pos_col.shape[0]
    inv = (np.float32(ROPE_THETA) ** (-np.arange(0, ROPE, 2, dtype=np.float32) / np.float32(ROPE))).astype(np.float32)
    freq = np.zeros((1, LANES), np.float32)
    freq[0, NOPE:NOPE + ROPE // 2] = inv
    freq[0, NOPE + ROPE // 2:NOPE + ROPE] = inv
    tm = _row_tile(t)

    def body(p_ref, f_ref, c_ref, s_ref):
        ang = p_ref[...].astype(F32) * f_ref[...]
        c_ref[...] = jnp.cos(ang)
        s_ref[...] = jnp.sin(ang)

    shp = jax.ShapeDtypeStruct((t, LANES), F32)
    return pl.pallas_call(
        body, name="rope_tables", grid=(t // tm,), out_shape=(shp, shp),
        in_specs=[pl.BlockSpec((tm, 1), lambda i: (i, 0)), pl.BlockSpec((1, LANES), lambda i: (0, 0))],
        out_specs=(pl.BlockSpec((tm, LANES), lambda i: (i, 0)),) * 2,
        compiler_params=_cparams("parallel"),
    )(pos_col, jnp.asarray(freq))


def _in_proj(x, g, w_qkv, w_rest):
    t = x.shape[0]
    tm = _row_tile(t)

    def body(x_ref, g_ref, wq_ref, wr_ref, h_ref, fq_ref, fk_ref, fv_ref, r_ref):
        h, _ = _rms(x_ref[...], g_ref[...])
        hb = h.astype(BF16)
        h_ref[...] = hb
        for n, ref in enumerate((fq_ref, fk_ref, fv_ref)):
            ref[...] = _dot_nt(hb, wq_ref[n * FOX_WIDTH:(n + 1) * FOX_WIDTH, :]).astype(BF16)
        r_ref[...] = _dot_nt(hb, wr_ref[...])

    row = lambda n: pl.BlockSpec((tm, n), lambda i: (i, 0))
    full = lambda a: pl.BlockSpec(a.shape, lambda i: (0,) * a.ndim)
    return pl.pallas_call(
        body, name="in_proj", grid=(t // tm,),
        out_shape=(jax.ShapeDtypeStruct((t, D_MODEL), BF16),) + (jax.ShapeDtypeStruct((t, FOX_WIDTH), BF16),) * 3
        + (jax.ShapeDtypeStruct((t, REST_COLS), F32),),
        in_specs=[row(D_MODEL), full(g), full(w_qkv), full(w_rest)],
        out_specs=(row(D_MODEL), row(FOX_WIDTH), row(FOX_WIDTH), row(FOX_WIDTH), row(REST_COLS)),
        compiler_params=_cparams("parallel"),
    )(x, g, w_qkv, w_rest)


def _log_sigmoid(z):
    return jnp.minimum(z, 0.0) - jnp.log(1.0 + jnp.exp(-jnp.abs(z)))


def _split3(v):
    hi = v.astype(BF16)
    r1 = v - hi.astype(F32)
    mid = r1.astype(BF16)
    lo = (r1 - mid.astype(F32)).astype(BF16)
    return hi, mid, lo


def _scan_tile(t):
    return 256 if t >= 256 else t


def _forget_cumsum(rest, b128):
    t = rest.shape[0]
    tb = _scan_tile(t)

    def body(r_ref, b_ref, row_ref, rep_ref, f_sc, carry):
        @pl.when(pl.program_id(0) == 0)
        def _():
            carry[...] = jnp.zeros_like(carry)
        lf = _log_sigmoid(r_ref[...] + b_ref[...])
        tri = (lax.broadcasted_iota(jnp.int32, (tb, tb), 0) >= lax.broadcasted_iota(jnp.int32, (tb, tb), 1)).astype(BF16)
        hi, mid, lo = _split3(lf)
        f_sc[...] = (_dot(tri, hi) + _dot(tri, mid)) + _dot(tri, lo) + carry[...]
        carry[...] = f_sc[tb - 1:tb, :]
        f2 = f_sc[...] * LOG2E
        row_ref[...] = jnp.transpose(f2)[0:HEADS, :]
        lane = _lane()
        for h in range(HEADS):
            col = jnp.sum(jnp.where(lane == h, f2, 0.0), axis=1, keepdims=True)
            rep_ref[h] = jnp.broadcast_to(col, (tb, LANES))

    return pl.pallas_call(
        body, name="forget_cumsum", grid=(t // tb,),
        out_shape=(jax.ShapeDtypeStruct((HEADS, t), F32), jax.ShapeDtypeStruct((HEADS, t, LANES), F32)),
        in_specs=[pl.BlockSpec((tb, LANES), lambda i: (i, 0)), pl.BlockSpec((1, LANES), lambda i: (0, 0))],
        out_specs=(pl.BlockSpec((HEADS, tb), lambda i: (0, i)), pl.BlockSpec((HEADS, tb, LANES), lambda i: (0, i, 0))),
        scratch_shapes=[pltpu.VMEM((tb, LANES), F32), pltpu.VMEM((1, LANES), F32)],
        compiler_params=_cparams("arbitrary"),
    )(rest, b128)


def _forget_bwd(rest, b128, d_fq, d_fk):
    t = rest.shape[0]
    tb = _scan_tile(t)
    nb = t // tb

    def body(r_ref, b_ref, dfq_ref, dfk_ref, dz_ref, db_ref, carry):
        @pl.when(pl.program_id(0) == 0)
        def _():
            carry[...] = jnp.zeros_like(carry)
            db_ref[...] = jnp.zeros_like(db_ref)
        tri = (lax.broadcasted_iota(jnp.int32, (tb, tb), 0) <= lax.broadcasted_iota(jnp.int32, (tb, tb), 1)).astype(BF16)
        lane = _lane()
        df = jnp.zeros((tb, LANES), F32)
        for h in range(HEADS):
            df = df + jnp.where(lane == h, dfq_ref[h] + dfk_ref[h], 0.0)
        hi, mid, lo = _split3(df)
        dlf = (_dot(tri, hi) + _dot(tri, mid)) + _dot(tri, lo) + carry[...]
        z = r_ref[...] + b_ref[...]
        dz = dlf / (1.0 + jnp.exp(z))
        dz_ref[...] = dz
        db_ref[...] += jnp.sum(dz, axis=0, keepdims=True)
        carry[...] = carry[...] + jnp.sum(df, axis=0, keepdims=True)

    rev = lambda i: (nb - 1 - i, 0)
    rev3 = pl.BlockSpec((HEADS, tb, LANES), lambda i: (0, nb - 1 - i, 0))
    return pl.pallas_call(
        body, name="forget_bwd", grid=(nb,),
        out_shape=(jax.ShapeDtypeStruct((t, LANES), F32), jax.ShapeDtypeStruct((1, LANES), F32)),
        in_specs=[pl.BlockSpec((tb, LANES), rev), pl.BlockSpec((1, LANES), lambda i: (0, 0)), rev3, rev3],
        out_specs=(pl.BlockSpec((tb, LANES), rev), pl.BlockSpec((1, LANES), lambda i: (0, 0))),
        scratch_shapes=[pltpu.VMEM((1, LANES), F32)],
        compiler_params=_cparams("arbitrary"),
    )(rest, b128, d_fq, d_fk)


def _mla_prep(rest, gq, gkv, wq, wkv, cos, sin):
    t = rest.shape[0]
    tm = _row_tile(t)

    def body(r_ref, gq_ref, gkv_ref, wq_ref, wkv_ref, c_ref, s_ref, q_ref, k_ref, kv_ref, cq_ref, ckv_ref):
        cos_, sin_ = c_ref[...], s_ref[...]
        cq, _ = _rms(r_ref[:, REST_CQ:REST_CKV], gq_ref[...])
        ckv, _ = _rms(r_ref[:, REST_CKV:REST_KR], gkv_ref[...])
        cqb, ckvb = cq.astype(BF16), ckv.astype(BF16)
        cq_ref[...] = cqb
        ckv_ref[...] = ckvb
        k_rope = _rope(r_ref[:, REST_KR:REST_COLS], cos_, sin_)
        lo = _lane() < NOPE
        for h in range(HEADS):
            q_ref[h] = _rope(_dot(cqb, wq_ref[h]), cos_, sin_).astype(BF16)
            kv = _dot(ckvb, wkv_ref[h])
            kv_ref[h] = kv.astype(BF16)
            k_ref[h] = (jnp.where(lo, kv, 0.0) + k_rope).astype(BF16)

    row = lambda n: pl.BlockSpec((tm, n), lambda i: (i, 0))
    full = lambda a: pl.BlockSpec(a.shape, lambda i: (0,) * a.ndim)
    heads = pl.BlockSpec((HEADS, tm, LANES), lambda i: (0, i, 0))
    hshape = jax.ShapeDtypeStruct((HEADS, t, LANES), BF16)
    return pl.pallas_call(
        body, name="mla_prep", grid=(t // tm,),
        out_shape=(hshape, hshape, hshape, jax.ShapeDtypeStruct((t, Q_RANK), BF16), jax.ShapeDtypeStruct((t, KV_RANK), BF16)),
        in_specs=[row(REST_COLS), full(gq), full(gkv), full(wq), full(wkv), row(LANES), row(LANES)],
        out_specs=(heads, heads, heads, row(Q_RANK), row(KV_RANK)),
        compiler_params=_cparams("parallel"),
    )(rest, gq, gkv, wq, wkv, cos, sin)


def _pair_specs(fox, t, tq, blocked_q):
    if fox:
        blk = pl.BlockSpec((tq, LANES), lambda p, i: (i, p))
        whole = pl.BlockSpec((t, LANES), lambda p, i: (0, p))
    else:
        blk = pl.BlockSpec((2, tq, LANES), lambda p, i: (p, i, 0))
        whole = pl.BlockSpec((2, t, LANES), lambda p, i: (p, 0, 0))
    return [blk, whole, whole] if blocked_q else [whole, blk, blk]


def _tile_lanes(x, n):
    return jnp.tile(x, (1, n)) if n > 1 else x


class _Comm(NamedTuple):
    inputs: tuple
    out_shape: tuple
    aliases: dict
    scratch: tuple
    start: Callable
    finish: Callable


def _hosted_call(name, main, grid, args, in_specs, out_shape, out_specs, scratch, comm):
    n_in, n_out, n_scr = len(args), len(out_shape), len(scratch)
    c_in = list(comm.inputs) if comm else []
    c_out = list(comm.out_shape) if comm else []

    def body(*refs):
        bounds = [0, n_in, len(c_in), n_out, len(c_out), n_scr]
        starts = [sum(bounds[:k + 1]) for k in range(len(bounds))]
        ins, cins, outs, couts, scr = [refs[a:b] for a, b in zip(starts[:-1], starts[1:])]
        sems = refs[starts[-1]:]
        if comm:
            @pl.when(jnp.logical_and(pl.program_id(0) == 0, pl.program_id(1) == 0))
            def _():
                comm.start(cins, couts, sems)
        main(ins, outs, scr)
        if comm:
            @pl.when(jnp.logical_and(pl.program_id(0) == grid[0] - 1, pl.program_id(1) == grid[1] - 1))
            def _():
                comm.finish(cins, couts, sems)

    res = pl.pallas_call(
        body, name=name, grid=grid,
        out_shape=list(out_shape) + c_out,
        in_specs=list(in_specs) + _hbm_specs(len(c_in)),
        out_specs=list(out_specs) + _hbm_specs(len(c_out)),
        scratch_shapes=list(scratch) + (list(comm.scratch) if comm else []),
        input_output_aliases={n_in + i: n_out + o for i, o in comm.aliases.items()} if comm else {},
        compiler_params=_cparams("arbitrary", "arbitrary"),
    )(*args, *c_in)
    return res[:n_out], res[n_out:]


def _pair_heads(fox, q_ref, lo):
    hi = jnp.logical_not(lo)
    if fox:
        zero = jnp.zeros((), BF16)
        return [jnp.where(lo, q_ref[...], zero), jnp.where(hi, q_ref[...], zero)], [hi, lo]
    return [q_ref[0], q_ref[1]], [lo, lo]


def _stat_rows(x):
    return jnp.transpose(x)[0:8, :]


def _attn_fwd(fox, q, k, v, f2_rows=None, comm=None):
    t = q.shape[0] if fox else q.shape[1]
    tq = _row_tile(t)
    nq = t // tq
    nrep = tq // LANES
    c2 = (FOX_SCALE if fox else MLA_SCALE) * LOG2E

    def main(ins, outs, scr):
        q_ref, k_ref, v_ref = ins[:3]
        fr_ref = ins[3] if fox else None
        o_ref, lse_ref, lset_ref = outs
        m_sc, acc_sc = scr
        i = pl.program_id(1)
        lo = _lane() < HEAD_DIM
        causal = lax.broadcasted_iota(jnp.int32, (tq, tq), 0) >= lax.broadcasted_iota(jnp.int32, (tq, tq), 1)
        qs, sum_lanes = _pair_heads(fox, q_ref, lo)
        one = jnp.ones((), BF16)
        m_sc[...] = jnp.full_like(m_sc, -jnp.inf)
        acc_sc[...] = jnp.zeros_like(acc_sc)

        def rows_of(ref, j, hh):
            sl = pl.ds(pl.multiple_of(j * tq, tq), tq)
            return ref[sl, :] if fox else ref[hh, sl, :]

        def step(j, masked):
            for hh in range(2):
                s = _dot_nt(qs[hh], rows_of(k_ref, j, hh)) * c2
                if fox:
                    s = s - fr_ref[hh, j]
                if masked:
                    s = jnp.where(causal, s, -jnp.inf)
                m_prev = m_sc[hh]
                m_new = jnp.maximum(m_prev, jnp.max(s, axis=1, keepdims=True))
                p = jnp.exp2(s - _tile_lanes(m_new, nrep))
                vj = jnp.where(sum_lanes[hh], one, rows_of(v_ref, j, hh))
                acc_sc[hh] = jnp.exp2(m_prev - m_new) * acc_sc[hh] + _dot(p.astype(BF16), vj)
                m_sc[hh] = m_new

        def loop_body(j, carry):
            step(j, False)
            return carry

        lax.fori_loop(0, i, loop_body, 0)
        step(i, True)
        outs = []
        for hh in range(2):
            acc = acc_sc[hh]
            swapped = pltpu.roll(acc, HEAD_DIM, 1)
            outs.append(acc / swapped)
            lse2 = m_sc[hh] + jnp.log(jnp.where(sum_lanes[hh], acc, swapped)) * LOG2E
            lse_ref[hh] = lse2
            lset_ref[hh, 0] = _stat_rows(lse2)
        if fox:
            o_ref[...] = jnp.where(lo, outs[0], outs[1])
        else:
            o_ref[...] = jnp.where(lo, pltpu.roll(outs[0], HEAD_DIM, 1), outs[1])

    stat = pl.BlockSpec((2, tq, LANES), lambda p, i: (p, i, 0))
    stat_rows = pl.BlockSpec((2, 1, 8, tq), lambda p, i: (p, i, 0, 0))
    in_specs = _pair_specs(fox, t, tq, True)
    args = [q, k, v]
    if fox:
        in_specs += [pl.BlockSpec((2, nq, 1, tq), lambda p, i: (p, 0, 0, 0))]
        args += [f2_rows]
    return _hosted_call(
        "fox_attn_fwd" if fox else "mla_attn_fwd", main, (HEADS // 2, nq), args, in_specs,
        (jax.ShapeDtypeStruct((t, 4 * LANES), F32), jax.ShapeDtypeStruct((HEADS, t, LANES), F32),
         jax.ShapeDtypeStruct((HEADS, nq, 8, tq), F32)),
        (pl.BlockSpec((tq, LANES), lambda p, i: (i, p)), stat, stat_rows),
        [pltpu.VMEM((2, tq, LANES), F32), pltpu.VMEM((2, tq, LANES), F32)], comm)


def _head_do(fox, hh, do2, lo):
    if fox:
        return jnp.where(lo if hh == 0 else jnp.logical_not(lo), do2, 0.0)
    return jnp.where(lo, 0.0, pltpu.roll(do2, HEAD_DIM, 1) if hh == 0 else do2)


def _attn_bwd_dq(fox, q, k, v, do, o, lse, f2_rows=None, comm=None):
    t = q.shape[0] if fox else q.shape[1]
    tq = _row_tile(t)
    nq = t // tq
    nrep = tq // LANES
    scale = FOX_SCALE if fox else MLA_SCALE
    c2 = scale * LOG2E

    def main(ins, outs, scr):
        if fox:
            q_ref, k_ref, v_ref, fr_ref, do_ref, o_ref, lse_ref = ins
            dq_ref, dlt_ref, df_ref = outs
        else:
            q_ref, k_ref, v_ref, do_ref, o_ref, lse_ref = ins
            dq_ref, dlt_ref = outs
        acc_sc, = scr
        i = pl.program_id(1)
        lo = _lane() < HEAD_DIM
        causal = lax.broadcasted_iota(jnp.int32, (tq, tq), 0) >= lax.broadcasted_iota(jnp.int32, (tq, tq), 1)
        qs, sum_lanes = _pair_heads(fox, q_ref, lo)
        one = jnp.ones((), BF16)
        do2 = do_ref[...]
        prod = do2 * o_ref[...]
        dobs, deltas = [], []
        for hh in range(2):
            hmask = lo if hh == 0 else jnp.logical_not(lo)
            delta = jnp.broadcast_to(jnp.sum(jnp.where(hmask, prod, 0.0), axis=1, keepdims=True), (tq, LANES))
            dlt_ref[hh, 0] = _stat_rows(delta)
            deltas.append(delta)
            dobs.append(_head_do(fox, hh, do2, lo).astype(BF16))
        acc_sc[...] = jnp.zeros_like(acc_sc)

        def rows_of(ref, j, hh):
            sl = pl.ds(pl.multiple_of(j * tq, tq), tq)
            return ref[sl, :] if fox else ref[hh, sl, :]

        def step(j, masked):
            for hh in range(2):
                kj = rows_of(k_ref, j, hh)
                s = _dot_nt(qs[hh], kj) * c2
                if fox:
                    s = s - fr_ref[hh, j]
                if masked:
                    s = jnp.where(causal, s, -jnp.inf)
                p = jnp.exp2(s - _tile_lanes(lse_ref[hh], nrep))
                dp = _dot_nt(dobs[hh], rows_of(v_ref, j, hh))
                ds = p * (dp - _tile_lanes(deltas[hh], nrep))
                if fox:
                    kj = jnp.where(sum_lanes[hh], one, kj)
                acc_sc[hh] += _dot(ds.astype(BF16), kj)

        def loop_body(j, carry):
            step(j, False)
            return carry

        lax.fori_loop(0, i, loop_body, 0)
        step(i, True)
        if fox:
            dq_ref[...] = (jnp.where(lo, acc_sc[0], acc_sc[1]) * scale).astype(BF16)
            for hh in range(2):
                acc = acc_sc[hh]
                df_ref[hh] = jnp.where(sum_lanes[hh], acc, pltpu.roll(acc, HEAD_DIM, 1))
        else:
            dq_ref[0] = acc_sc[0] * scale
            dq_ref[1] = acc_sc[1] * scale

    stat = pl.BlockSpec((2, tq, LANES), lambda p, i: (p, i, 0))
    stat_rows = pl.BlockSpec((2, 1, 8, tq), lambda p, i: (p, i, 0, 0))
    pair = pl.BlockSpec((tq, LANES), lambda p, i: (i, p))
    in_specs = _pair_specs(fox, t, tq, True)
    args = [q, k, v]
    if fox:
        in_specs += [pl.BlockSpec((2, nq, 1, tq), lambda p, i: (p, 0, 0, 0))]
        args += [f2_rows]
    in_specs += [pair, pair, stat]
    args += [do, o, lse]
    rows_shape = jax.ShapeDtypeStruct((HEADS, nq, 8, tq), F32)
    if fox:
        out_shape = (jax.ShapeDtypeStruct((t, 4 * LANES), BF16), rows_shape, jax.ShapeDtypeStruct((HEADS, t, LANES), F32))
        out_specs = (pair, stat_rows, stat)
    else:
        out_shape = (jax.ShapeDtypeStruct((HEADS, t, LANES), F32), rows_shape)
        out_specs = (stat, stat_rows)
    return _hosted_call("fox_attn_bwd_dq" if fox else "mla_attn_bwd_dq", main, (HEADS // 2, nq), args, in_specs,
                        out_shape, out_specs, [pltpu.VMEM((2, tq, LANES), F32)], comm)


def _attn_bwd_dkv(fox, q, k, v, do, lse_rows, delta_rows, f2_rep=None, comm=None):
    t = q.shape[0] if fox else q.shape[1]
    tq = _row_tile(t)
    nq = t // tq
    nrep = tq // LANES
    scale = FOX_SCALE if fox else MLA_SCALE
    c2 = scale * LOG2E

    def main(ins, outs, scr):
        if fox:
            q_ref, k_ref, v_ref, f_ref, do_ref, lse_ref, dl_ref = ins
            dk_ref, dv_ref, df_ref = outs
        else:
            q_ref, k_ref, v_ref, do_ref, lse_ref, dl_ref = ins
            dkv_ref, dkr_ref = outs
        dk_sc, dv_sc = scr
        j = pl.program_id(1)
        lane = _lane()
        lo = lane < HEAD_DIM
        hi = jnp.logical_not(lo)
        causal = lax.broadcasted_iota(jnp.int32, (tq, tq), 1) >= lax.broadcasted_iota(jnp.int32, (tq, tq), 0)
        zero, one = jnp.zeros((), BF16), jnp.ones((), BF16)
        dk_sc[...] = jnp.zeros_like(dk_sc)
        dv_sc[...] = jnp.zeros_like(dv_sc)

        def step(i, masked):
            sl = pl.ds(pl.multiple_of(i * tq, tq), tq)
            do_i = do_ref[sl, :]
            for hh in range(2):
                kj = k_ref[...] if fox else k_ref[hh]
                vj = v_ref[...] if fox else v_ref[hh]
                qi = jnp.where(lo if hh == 0 else hi, q_ref[sl, :], zero) if fox else q_ref[hh, sl, :]
                dob = _head_do(fox, hh, do_i, lo).astype(BF16)
                st = _dot_nt(kj, qi) * c2
                if fox:
                    st = st - _tile_lanes(f_ref[hh], nrep)
                if masked:
                    st = jnp.where(causal, st, -jnp.inf)
                pt = jnp.exp2(st - lse_ref[hh, i, 0:1, :])
                dpt = _dot_nt(vj, dob)
                dst = pt * (dpt - dl_ref[hh, i, 0:1, :])
                dv_sc[hh] += _dot(pt.astype(BF16), dob)
                if fox:
                    qi = jnp.where(hi if hh == 0 else lo, one, qi)
                dk_sc[hh] += _dot(dst.astype(BF16), qi)

        def loop_body(i, carry):
            step(i, False)
            return carry

        step(j, True)
        lax.fori_loop(j + 1, nq, loop_body, 0)
        if fox:
            dk_ref[...] = (jnp.where(lo, dk_sc[0], dk_sc[1]) * scale).astype(BF16)
            dv_ref[...] = (dv_sc[0] + dv_sc[1]).astype(BF16)
            for hh in range(2):
                dk = dk_sc[hh]
                df_ref[hh] = -jnp.where(hi if hh == 0 else lo, dk, pltpu.roll(dk, HEAD_DIM, 1))
        else:
            rope_lanes = jnp.logical_and(lane >= NOPE, lane < NOPE + ROPE)
            dkr = jnp.zeros((tq, LANES), F32)
            for hh in range(2):
                dk = dk_sc[hh] * scale
                dkv_ref[hh] = jnp.where(lo, dk, dv_sc[hh])
                dkr = dkr + jnp.where(rope_lanes, dk, 0.0)
            dkr_ref[0] = dkr

    stat = pl.BlockSpec((2, tq, LANES), lambda p, j: (p, j, 0))
    rows4 = pl.BlockSpec((2, nq, 8, tq), lambda p, j: (p, 0, 0, 0))
    pair = pl.BlockSpec((tq, LANES), lambda p, j: (j, p))
    in_specs = _pair_specs(fox, t, tq, False)
    args = [q, k, v]
    if fox:
        in_specs += [stat]
        args += [f2_rep]
    in_specs += [pl.BlockSpec((t, LANES), lambda p, j: (0, p)), rows4, rows4]
    args += [do, lse_rows, delta_rows]
    if fox:
        out_shape = (jax.ShapeDtypeStruct((t, 4 * LANES), BF16), jax.ShapeDtypeStruct((t, 4 * LANES), BF16),
                     jax.ShapeDtypeStruct((HEADS, t, LANES), F32))
        out_specs = (pair, pair, stat)
    else:
        out_shape = (jax.ShapeDtypeStruct((HEADS, t, LANES), F32), jax.ShapeDtypeStruct((HEADS // 2, t, LANES), F32))
        out_specs = (stat, pl.BlockSpec((1, tq, LANES), lambda p, j: (p, j, 0)))
    return _hosted_call("fox_attn_bwd_dkv" if fox else "mla_attn_bwd_dkv", main, (HEADS // 2, nq), args, in_specs,
                        out_shape, out_specs, [pltpu.VMEM((2, tq, LANES), F32), pltpu.VMEM((2, tq, LANES), F32)], comm)


def _attn_out(x, fox_o, mla_o, gf, gm, w_o):
    t = x.shape[0]
    tm = _row_tile(t)

    def body(x_ref, f_ref, m_ref, gf_ref, gm_ref, w_ref, x1_ref, mix_ref):
        nf, _ = _rms(f_ref[...], gf_ref[...])
        nm, _ = _rms(m_ref[...], gm_ref[...])
        nfb, nmb = nf.astype(BF16), nm.astype(BF16)
        mix_ref[:, :FOX_WIDTH] = nfb
        mix_ref[:, FOX_WIDTH:] = nmb
        x1_ref[...] = x_ref[...] + _dot(nfb, w_ref[:FOX_WIDTH, :]) + _dot(nmb, w_ref[FOX_WIDTH:, :])

    row = lambda n: pl.BlockSpec((tm, n), lambda i: (i, 0))
    full = lambda a: pl.BlockSpec(a.shape, lambda i: (0,) * a.ndim)
    return pl.pallas_call(
        body, name="attn_out", grid=(t // tm,),
        out_shape=(jax.ShapeDtypeStruct((t, D_MODEL), F32), jax.ShapeDtypeStruct((t, D_MODEL), BF16)),
        in_specs=[row(D_MODEL), row(FOX_WIDTH), row(MLA_WIDTH), full(gf), full(gm), full(w_o)],
        out_specs=(row(D_MODEL), row(D_MODEL)),
        compiler_params=_cparams("parallel"),
    )(x, fox_o, mla_o, gf, gm, w_o)


def _mlp_tile(t):
    return 256 if t >= 2048 else 128


def _resident(a):
    return pl.BlockSpec(a.shape, lambda i: (0,) * a.ndim, pipeline_mode=pl.Buffered(1))


FF_CHUNK = 512


def _mlp_fwd(x1, g_mlp, w_up, w_down, g_fin, target):
    t = x1.shape[0]
    tm = _mlp_tile(t)

    def body(x_ref, g_ref, wu_ref, wd_ref, gf_ref, t_ref, u_ref, h_ref, dx_ref, loss_ref, dg_ref, a_sc):
        @pl.when(pl.program_id(0) == 0)
        def _():
            loss_ref[...] = jnp.zeros_like(loss_ref)
            dg_ref[...] = jnp.zeros_like(dg_ref)

        x = x_ref[...]
        h, _ = _rms(x, g_ref[...])
        hb = h.astype(BF16)
        h_ref[...] = hb
        for f in range(D_FF // FF_CHUNK):
            sl = slice(f * FF_CHUNK, (f + 1) * FF_CHUNK)
            u = _dot(hb, wu_ref[:, sl])
            u_ref[:, sl] = u
            r = jnp.maximum(u, 0.0)
            a_sc[:, sl] = (r * r).astype(BF16)
        x2 = x + _dot(a_sc[...], wd_ref[...])
        y, r2 = _rms(x2, gf_ref[...])
        err = y - t_ref[...]
        loss_ref[...] += 0.5 * jnp.sum(jnp.mean(err * err, axis=-1, keepdims=True))
        dx, dg = _rms_bwd(x2, gf_ref[...], r2, err * (1.0 / D_MODEL))
        dx_ref[...] = dx
        dg_ref[...] += dg

    row = lambda n: pl.BlockSpec((tm, n), lambda i: (i, 0))
    vec = pl.BlockSpec((1, D_MODEL), lambda i: (0, 0))
    return pl.pallas_call(
        body, name="mlp_fwd", grid=(t // tm,),
        out_shape=(jax.ShapeDtypeStruct((t, D_FF), F32), jax.ShapeDtypeStruct((t, D_MODEL), BF16),
                   jax.ShapeDtypeStruct((t, D_MODEL), F32), jax.ShapeDtypeStruct((8, LANES), F32),
                   jax.ShapeDtypeStruct((1, D_MODEL), F32)),
        in_specs=[row(D_MODEL), vec, _resident(w_up), _resident(w_down), vec, row(D_MODEL)],
        out_specs=(row(D_FF), row(D_MODEL), row(D_MODEL), pl.BlockSpec((8, LANES), lambda i: (0, 0)), vec),
        scratch_shapes=[pltpu.VMEM((tm, D_FF), BF16)],
        compiler_params=_cparams("arbitrary"),
    )(x1, g_mlp, w_up, w_down, g_fin, target)


def _mlp_bwd(dx2, u, x1, g_mlp, w_up, w_down):
    t = x1.shape[0]
    tm = _mlp_tile(t)

    def body(dx_ref, u_ref, x_ref, g_ref, wu_ref, wd_ref, du_ref, a_ref, dx1_ref, dg_ref):
        @pl.when(pl.program_id(0) == 0)
        def _():
            dg_ref[...] = jnp.zeros_like(dg_ref)

        dx2 = dx_ref[...]
        dxb = dx2.astype(BF16)
        for f in range(D_FF // FF_CHUNK):
            sl = slice(f * FF_CHUNK, (f + 1) * FF_CHUNK)
            r = jnp.maximum(u_ref[:, sl], 0.0)
            a_ref[:, sl] = (r * r).astype(BF16)
            da = _dot_nt(dxb, wd_ref[sl, :])
            du_ref[:, sl] = (da * (2.0 * r)).astype(BF16)
        dh = _dot_nt(du_ref[...], wu_ref[...])
        x = x_ref[...]
        _, r1 = _rms(x, g_ref[...])
        dx, dg = _rms_bwd(x, g_ref[...], r1, dh)
        dx1_ref[...] = dx2 + dx
        dg_ref[...] += dg

    row = lambda n: pl.BlockSpec((tm, n), lambda i: (i, 0))
    vec = pl.BlockSpec((1, D_MODEL), lambda i: (0, 0))
    return pl.pallas_call(
        body, name="mlp_bwd", grid=(t // tm,),
        out_shape=(jax.ShapeDtypeStruct((t, D_FF), BF16), jax.ShapeDtypeStruct((t, D_FF), BF16),
                   jax.ShapeDtypeStruct((t, D_MODEL), F32), jax.ShapeDtypeStruct((1, D_MODEL), F32)),
        in_specs=[row(D_MODEL), row(D_FF), row(D_MODEL), vec, _resident(w_up), _resident(w_down)],
        out_specs=(row(D_FF), row(D_FF), row(D_MODEL), vec),
        compiler_params=_cparams("arbitrary"),
    )(dx2, u, x1, g_mlp, w_up, w_down)


def _matmul_tn(name, a, b, blocks=None):
    t, m = a.shape
    n = b.shape[1]
    tk = min(t, 2048)
    steps = t // tk
    bm = m if m <= 1024 else 512
    bn = n if n <= 1024 else 512
    width = bn if blocks is None else n // blocks
    per = bn // width

    def body(a_ref, b_ref, o_ref, acc_sc):
        kk = pl.program_id(2)

        @pl.when(kk == 0)
        def _():
            acc_sc[...] = jnp.zeros_like(acc_sc)

        acc_sc[...] += _dot_tn(a_ref[...].astype(BF16), b_ref[...].astype(BF16))

        @pl.when(kk == steps - 1)
        def _():
            if blocks is None:
                o_ref[...] = acc_sc[...]
            else:
                for s in range(per):
                    o_ref[s] = acc_sc[:, s * width:(s + 1) * width]

    if blocks is None:
        o_spec = pl.BlockSpec((bm, bn), lambda i, j, kk: (i, j))
        o_shape = (m, n)
    else:
        o_spec = pl.BlockSpec((per, bm, width), lambda i, j, kk: (j, i, 0))
        o_shape = (blocks, m, width)
    return pl.pallas_call(
        body, name=name, grid=(m // bm, n // bn, steps),
        out_shape=jax.ShapeDtypeStruct(o_shape, F32),
        in_specs=[pl.BlockSpec((tk, bm), lambda i, j, kk: (kk, i)), pl.BlockSpec((tk, bn), lambda i, j, kk: (kk, j))],
        out_specs=o_spec,
        scratch_shapes=[pltpu.VMEM((bm, bn), F32)],
        compiler_params=_cparams("parallel", "parallel", "arbitrary"),
    )(a, b)


def _attn_out_bwd(dx1, fox_o, mla_o, gf, gm, w_o):
    t = dx1.shape[0]
    tm = _row_tile(t)

    def body(dx_ref, f_ref, m_ref, gf_ref, gm_ref, w_ref, df_ref, dm_ref, dgf_ref, dgm_ref):
        @pl.when(pl.program_id(0) == 0)
        def _():
            dgf_ref[...] = jnp.zeros_like(dgf_ref)
            dgm_ref[...] = jnp.zeros_like(dgm_ref)
        dxb = dx_ref[...].astype(BF16)
        for o_ref, g_ref, lo_row, d_ref, dg_ref in ((f_ref, gf_ref, 0, df_ref, dgf_ref), (m_ref, gm_ref, FOX_WIDTH, dm_ref, dgm_ref)):
            dn = _dot_nt(dxb, w_ref[lo_row:lo_row + FOX_WIDTH, :])
            o = o_ref[...]
            _, r = _rms(o, g_ref[...])
            d, dg = _rms_bwd(o, g_ref[...], r, dn)
            d_ref[...] = d
            dg_ref[...] += dg

    row = lambda n: pl.BlockSpec((tm, n), lambda i: (i, 0))
    full = lambda a: pl.BlockSpec(a.shape, lambda i: (0,) * a.ndim)
    vec = pl.BlockSpec((1, FOX_WIDTH), lambda i: (0, 0))
    o_shape = jax.ShapeDtypeStruct((t, FOX_WIDTH), F32)
    g_shape = jax.ShapeDtypeStruct((1, FOX_WIDTH), F32)
    return pl.pallas_call(
        body, name="attn_out_bwd", grid=(t // tm,),
        out_shape=(o_shape, o_shape, g_shape, g_shape),
        in_specs=[row(D_MODEL), row(FOX_WIDTH), row(MLA_WIDTH), full(gf), full(gm), full(w_o)],
        out_specs=(row(FOX_WIDTH), row(MLA_WIDTH), vec, vec),
        compiler_params=_cparams("arbitrary"),
    )(dx1, fox_o, mla_o, gf, gm, w_o)


def _mla_prep_bwd(dq, dkv, dkr, dz, rest, gq, gkv, wq, wkv, cos, sin):
    t = rest.shape[0]
    tm = _row_tile(t)

    def body(dq_ref, dkv_ref, dkr_ref, dz_ref, r_ref, gq_ref, gkv_ref, wq_ref, wkv_ref, c_ref, s_ref,
             dr_ref, dqp_ref, dkvb_ref, dgq_ref, dgkv_ref):
        @pl.when(pl.program_id(0) == 0)
        def _():
            dgq_ref[...] = jnp.zeros_like(dgq_ref)
            dgkv_ref[...] = jnp.zeros_like(dgkv_ref)
        cos_, sin_ = c_ref[...], s_ref[...]
        dcq = jnp.zeros((tm, Q_RANK), F32)
        dckv = jnp.zeros((tm, KV_RANK), F32)
        for h in range(HEADS):
            dqp = _rope_bwd(dq_ref[h], cos_, sin_).astype(BF16)
            dqp_ref[:, h * LANES:(h + 1) * LANES] = dqp
            dcq = dcq + _dot_nt(dqp, wq_ref[h])
            dkvb = dkv_ref[h].astype(BF16)
            dkvb_ref[:, h * LANES:(h + 1) * LANES] = dkvb
            dckv = dckv + _dot_nt(dkvb, wkv_ref[h])
        dkrope = dkr_ref[0]
        for pr in range(1, HEADS // 2):
            dkrope = dkrope + dkr_ref[pr]
        cq = r_ref[:, REST_CQ:REST_CKV]
        _, rq = _rms(cq, gq_ref[...])
        d_cq, dgq = _rms_bwd(cq, gq_ref[...], rq, dcq)
        ckv = r_ref[:, REST_CKV:REST_KR]
        _, rkv = _rms(ckv, gkv_ref[...])
        d_ckv, dgkv = _rms_bwd(ckv, gkv_ref[...], rkv, dckv)
        dgq_ref[...] += dgq
        dgkv_ref[...] += dgkv
        dr_ref[:, 0:REST_CQ] = dz_ref[...].astype(BF16)
        dr_ref[:, REST_CQ:REST_CKV] = d_cq.astype(BF16)
        dr_ref[:, REST_CKV:REST_KR] = d_ckv.astype(BF16)
        dr_ref[:, REST_KR:REST_COLS] = _rope_bwd(dkrope, cos_, sin_).astype(BF16)

    row = lambda n: pl.BlockSpec((tm, n), lambda i: (i, 0))
    full = lambda a: pl.BlockSpec(a.shape, lambda i: (0,) * a.ndim)
    heads = pl.BlockSpec((HEADS, tm, LANES), lambda i: (0, i, 0))
    hshape = jax.ShapeDtypeStruct((t, HEADS * LANES), BF16)
    return pl.pallas_call(
        body, name="mla_prep_bwd", grid=(t // tm,),
        out_shape=(jax.ShapeDtypeStruct((t, REST_COLS), BF16), hshape, hshape,
                   jax.ShapeDtypeStruct((1, Q_RANK), F32), jax.ShapeDtypeStruct((1, KV_RANK), F32)),
        in_specs=[heads, heads, pl.BlockSpec((HEADS // 2, tm, LANES), lambda i: (0, i, 0)), row(LANES), row(REST_COLS),
                  full(gq), full(gkv), full(wq), full(wkv), row(LANES), row(LANES)],
        out_specs=(row(REST_COLS), row(HEADS * LANES), row(HEADS * LANES), pl.BlockSpec((1, Q_RANK), lambda i: (0, 0)),
                   pl.BlockSpec((1, KV_RANK), lambda i: (0, 0))),
        compiler_params=_cparams("arbitrary"),
    )(dq, dkv, dkr, dz, rest, gq, gkv, wq, wkv, cos, sin)


def _in_proj_bwd(x, g, dx1, dfq, dfk, dfv, drest, w_qkv, w_rest):
    t = x.shape[0]
    tm = _row_tile(t)

    def body(x_ref, g_ref, dx1_ref, dq_ref, dk_ref, dv_ref, dr_ref, wq_ref, wr_ref, dx_ref, dg_ref):
        @pl.when(pl.program_id(0) == 0)
        def _():
            dg_ref[...] = jnp.zeros_like(dg_ref)
        dh = _dot(dr_ref[...], wr_ref[...])
        for n, ref in enumerate((dq_ref, dk_ref, dv_ref)):
            dh = dh + _dot(ref[...], wq_ref[n * FOX_WIDTH:(n + 1) * FOX_WIDTH, :])
        xv = x_ref[...]
        _, r = _rms(xv, g_ref[...])
        dx, dg = _rms_bwd(xv, g_ref[...], r, dh)
        dx_ref[...] = dx1_ref[...] + dx
        dg_ref[...] += dg

    row = lambda n: pl.BlockSpec((tm, n), lambda i: (i, 0))
    full = lambda a: pl.BlockSpec(a.shape, lambda i: (0,) * a.ndim)
    vec = pl.BlockSpec((1, D_MODEL), lambda i: (0, 0))
    return pl.pallas_call(
        body, name="in_proj_bwd", grid=(t // tm,),
        out_shape=(jax.ShapeDtypeStruct((t, D_MODEL), F32), jax.ShapeDtypeStruct((1, D_MODEL), F32)),
        in_specs=[row(D_MODEL), full(g), row(D_MODEL), row(FOX_WIDTH), row(FOX_WIDTH), row(FOX_WIDTH), row(REST_COLS),
                  full(w_qkv), full(w_rest)],
        out_specs=(row(D_MODEL), vec),
        compiler_params=_cparams("arbitrary"),
    )(x, g, dx1, dfq, dfk, dfv, drest, w_qkv, w_rest)


def _pad_cols(a, n):
    return jnp.pad(a, ((0, 0),) * (a.ndim - 1) + ((0, n - a.shape[-1]),))


def kernel(x, positions, attn_norm_g, w_in, b_forget, q_norm_g, w_uq, kv_norm_g, w_ukv, fox_out_g, mla_out_g, w_o, mlp_norm_g, w_up, w_down, final_norm_g, loss_target, m_attn_norm_g, m_w_in, m_b_forget, m_q_norm_g, m_w_uq, m_kv_norm_g, m_w_ukv, m_fox_out_g, m_mla_out_g, m_w_o, m_mlp_norm_g, m_w_up, m_w_down, m_final_norm_g, v_attn_norm_g, v_w_in, v_b_forget, v_q_norm_g, v_w_uq, v_kv_norm_g, v_w_ukv, v_fox_out_g, v_mla_out_g, v_w_o, v_mlp_norm_g, v_w_up, v_w_down, v_final_norm_g):
    t = x.shape[1]
    tq = _row_tile(t)
    xs = x[0]
    target = loss_target[0]

    early = [jnp.transpose(w_in[0]), _pad_cols(w_uq[0], LANES), w_ukv[0]]
    late = [w_o[0].astype(BF16), w_up[0].astype(BF16), w_down[0].astype(BF16)]
    g_in, wq, wkv = _all_gather([s.astype(BF16) for s in early])
    win = g_in.reshape(IN_COLS, D_MODEL)
    off_ff, off_cq, off_kr = 3 * FOX_WIDTH, 3 * FOX_WIDTH + HEADS, IN_COLS - ROPE
    zeros = lambda n: jnp.zeros((n, D_MODEL), BF16)
    w_qkv = win[:off_ff]
    w_rest = jnp.concatenate([
        win[off_ff:off_cq], zeros(REST_CQ - HEADS), win[off_cq:off_kr],
        zeros(NOPE), win[off_kr:], zeros(LANES - NOPE - ROPE)], axis=0)

    cos, sin = _rope_tables(positions.reshape(t, 1))
    h1, fq, fk, fv, rest = _in_proj(xs, attn_norm_g, w_qkv, w_rest)
    b128 = _pad_cols(b_forget, LANES)
    f2_rows, f2_rep = _forget_cumsum(rest, b128)
    f2_rows = f2_rows.reshape(HEADS, t // tq, 1, tq)
    (fox_o, fox_lse, fox_lse_rows), partly = _attn_fwd(True, fq, fk, fv, f2_rows, comm=_ag_direct(late))
    mq, mk, mkv, cqn, ckvn = _mla_prep(rest, q_norm_g, kv_norm_g, wq, wkv, cos, sin)
    (mla_o, mla_lse, mla_lse_rows), (g_o, g_up, g_down) = _attn_fwd(False, mq, mk, mkv, comm=_ag_forward(partly))
    wo = g_o.reshape(D_MODEL, D_MODEL)
    x1, mixed = _attn_out(xs, fox_o, mla_o, fox_out_g, mla_out_g, wo)
    wup = jnp.transpose(g_up, (1, 0, 2)).reshape(D_MODEL, D_FF)
    wdown = g_down.reshape(D_FF, D_MODEL)
    u, h2, dx2, loss8, d_gfin = _mlp_fwd(x1, mlp_norm_g, wup, wdown, final_norm_g.reshape(1, D_MODEL), target)

    du, act, dx1, d_gmlp = _mlp_bwd(dx2, u, x1, mlp_norm_g, wup, wdown)
    dw_down = _matmul_tn("dw_down", act, dx2)
    dw_up = _matmul_tn("dw_up", h2, du, blocks=N_DEV)
    dfox_o, dmla_o, d_gfox, d_gmla = _attn_out_bwd(dx1, fox_o, mla_o, fox_out_g, mla_out_g, wo)
    dw_o = _matmul_tn("dw_o", mixed, dx1)

    core = lax.axis_index("c").astype(jnp.int32).reshape(1)
    chip = (2 * lax.axis_index("x") + lax.axis_index("y")).astype(jnp.int32).reshape(1)
    names = ("w_in", "w_uq", "w_ukv", "w_o", "w_up", "w_down")
    grads_b = [dw_o.reshape(N_DEV, -1, D_MODEL), dw_up, dw_down.reshape(N_DEV, -1, D_MODEL)]
    (dfq, fox_delta_rows, d_fq), got_b = _attn_bwd_dq(True, fq, fk, fv, dfox_o, fox_o, fox_lse, f2_rows,
                                                      comm=_rs_to_sibling(grads_b))
    sums_b = [_rs_sibling_sum("rs_sibling_sum_" + nm, g, l, core) for nm, g, l in zip(names[3:], grads_b, got_b)]
    (dfk, dfv, d_fk), others_b = _attn_bwd_dkv(True, fq, fk, fv, dfox_o, fox_lse_rows, fox_delta_rows, f2_rep,
                                               comm=_rs_to_chips([s[1] for s in sums_b]))
    dz, d_b = _forget_bwd(rest, b128, d_fq, d_fk)

    (dmq, mla_delta_rows), _ = _attn_bwd_dq(False, mq, mk, mkv, dmla_o, mla_o, mla_lse)
    (dmkv, dmkr), _ = _attn_bwd_dkv(False, mq, mk, mkv, dmla_o, mla_lse_rows, mla_delta_rows)
    drest, dqp, dkvb, d_gq, d_gkv = _mla_prep_bwd(dmq, dmkv, dmkr, dz, rest, q_norm_g, kv_norm_g, wq, wkv, cos, sin)
    dw_uq = _matmul_tn("dw_uq", cqn, dqp, blocks=HEADS)
    dw_ukv = _matmul_tn("dw_ukv", ckvn, dkvb, blocks=HEADS)
    grad_x, d_gattn = _in_proj_bwd(xs, attn_norm_g, dx1, dfq, dfk, dfv, drest, w_qkv, w_rest)
    dw_q = _matmul_tn("dw_in_q", dfq, h1)
    dw_k = _matmul_tn("dw_in_k", dfk, h1)
    dw_v = _matmul_tn("dw_in_v", dfv, h1)
    dw_r = _matmul_tn("dw_in_rest", drest, h1)

    dw_in = jnp.concatenate([dw_q, dw_k, dw_v, dw_r[0:HEADS], dw_r[REST_CQ:REST_KR],
                             dw_r[REST_KR + NOPE:REST_KR + NOPE + ROPE]], axis=0)
    grads_a = [dw_in.reshape(N_DEV, IN_SHARD, D_MODEL), dw_uq, dw_ukv]
    got_a = _comm_call("rs_sibling_exchange", _rs_to_sibling(grads_a))
    sums_a = [_rs_sibling_sum("rs_sibling_sum_" + nm, g, l, core) for nm, g, l in zip(names[:3], grads_a, got_a)]
    others_a = _comm_call("rs_chip_exchange", _rs_to_chips([s[1] for s in sums_a]))
    sums, others = sums_a + sums_b, list(others_a) + list(others_b)
    sharded = (w_in, w_uq, w_ukv, w_o, w_up, w_down)
    moments_m = (m_w_in, m_w_uq, m_w_ukv, m_w_o, m_w_up, m_w_down)
    moments_v = (v_w_in, v_w_uq, v_w_ukv, v_w_o, v_w_up, v_w_down)
    g_in_t = _rs_final_sum("rs_final_sum_w_in", sums[0][0], others[0], chip)
    big = [_adamw_given("adamw_w_in", jnp.transpose(g_in_t), w_in, m_w_in, v_w_in)]
    for a in range(1, len(names)):
        big.append(_adamw_sharded("adamw_" + names[a], sharded[a], moments_m[a], moments_v[a], sums[a][0], others[a], chip))
    big_g, big_d, big_m, big_v = [[b[k] for b in big] for k in range(4)]

    as_row = lambda a: a.reshape(1, -1)
    small_w = (attn_norm_g, b_forget, q_norm_g, kv_norm_g, fox_out_g, mla_out_g, mlp_norm_g, final_norm_g)
    small_m = (m_attn_norm_g, m_b_forget, m_q_norm_g, m_kv_norm_g, m_fox_out_g, m_mla_out_g, m_mlp_norm_g, m_final_norm_g)
    small_v = (v_attn_norm_g, v_b_forget, v_q_norm_g, v_kv_norm_g, v_fox_out_g, v_mla_out_g, v_mlp_norm_g, v_final_norm_g)
    total = _small_all_reduce([d_gattn, d_b, d_gq, d_gkv, d_gfox, d_gmla, d_gmlp, d_gfin], loss8)
    small = _adamw_small(total, [as_row(a) for a in small_w], [as_row(a) for a in small_m], [as_row(a) for a in small_v])
    loss = small[0].reshape(())
    s_g, s_d, s_m, s_v = [[small[1 + 4 * r + k].reshape(small_w[r].shape) for r in range(len(small_w))] for k in range(4)]

    def ordered(small_, bigs):
        ga, bf, gq_, gkv_, gfo, gml, gmlp_, gfin_ = small_
        bin_, buq, bukv, bo, bup, bdown = bigs
        return [ga, bin_, bf, gq_, buq, gkv_, bukv, gfo, gml, bo, gmlp_, bup, bdown, gfin_]

    return (loss, grad_x[None], *ordered(s_g, big_g), *ordered(s_d, big_d), *ordered(s_m, big_m), *ordered(s_v, big_v))
```

```python
import math
from typing import Callable, NamedTuple

import numpy as np
import jax
import jax.numpy as jnp
from jax import lax
from jax.experimental import pallas as pl
from jax.experimental.pallas import tpu as pltpu

F32 = jnp.float32
BF16 = jnp.bfloat16
MESH = pl.DeviceIdType.MESH

D_MODEL = 1024
HEADS = 8
HEAD_DIM = 64
FOX_WIDTH = 512
MLA_WIDTH = 512
NOPE = 64
ROPE = 32
QK_DIM = 96
Q_RANK = 384
KV_RANK = 256
D_FF = 4096
IN_COLS = 2216
ROPE_THETA = 10000.0
EPS = 1e-6
FOX_SCALE = 1.0 / math.sqrt(HEAD_DIM)
MLA_SCALE = 1.0 / math.sqrt(QK_DIM)
ADAM_LR = 0.001
ADAM_B1 = 0.9
ADAM_B2 = 0.999
ADAM_EPS = 1e-08
ADAM_WD = 0.01
ADAM_STEP = 10

N_DEV = 8
LANES = 128
REST_COLS = 896
REST_CQ = LANES
REST_CKV = REST_CQ + Q_RANK
REST_KR = REST_CKV + KV_RANK
LOG2E = 1.4426950408889634
VMEM_LIMIT = 56 * 1024 * 1024

IN_SHARD = IN_COLS // N_DEV
SMALL_SIZES = (1024, 8, 384, 256, 512, 512, 1024, 1024)
SMALL_ROWS = 16
LOSS_ROW = len(SMALL_SIZES)


def _cparams(*sem):
    return pltpu.CompilerParams(dimension_semantics=sem or None, vmem_limit_bytes=VMEM_LIMIT)


def _row_tile(t):
    return 512 if t >= 2048 else 128


def _dot(a, b):
    return jnp.dot(a, b, preferred_element_type=F32)


def _dot_nt(a, b):
    return lax.dot_general(a, b, (((1,), (1,)), ((), ())), preferred_element_type=F32)


def _dot_tn(a, b):
    return lax.dot_general(a, b, (((0,), (0,)), ((), ())), preferred_element_type=F32)


def _rms(x, g):
    r = lax.rsqrt(jnp.mean(x * x, axis=-1, keepdims=True) + EPS)
    return x * r * g, r


def _rms_bwd(x, g, r, dy):
    xh = x * r
    gdy = dy * g
    dx = r * (gdy - xh * jnp.mean(gdy * xh, axis=-1, keepdims=True))
    return dx, jnp.sum(dy * xh, axis=0, keepdims=True)


def _lane():
    return lax.broadcasted_iota(jnp.int32, (1, LANES), 1)


def _rot(x):
    lane = _lane()
    half = NOPE + ROPE // 2
    first = jnp.logical_and(lane >= NOPE, lane < half)
    second = jnp.logical_and(lane >= half, lane < NOPE + ROPE)
    return jnp.where(first, -pltpu.roll(x, LANES - ROPE // 2, 1), jnp.where(second, pltpu.roll(x, ROPE // 2, 1), 0.0))


def _rope(x, cos, sin):
    return x * cos + _rot(x) * sin


def _rope_bwd(dy, cos, sin):
    return dy * cos - _rot(dy * sin)


def _remote(src, dst, send_sem, recv_sem, to):
    return pltpu.make_async_remote_copy(src_ref=src, dst_ref=dst, send_sem=send_sem, recv_sem=recv_sem,
                                        device_id=to, device_id_type=MESH)


def _hbm_specs(n):
    return [pl.BlockSpec(memory_space=pl.ANY)] * n


def _all_gather(blocks):
    n = len(blocks)

    def body(*refs):
        x_refs, out_refs = refs[:n], refs[n:2 * n]
        send_sems, recv_sems, local_sems = refs[2 * n:]
        x, y, c = lax.axis_index("x"), lax.axis_index("y"), lax.axis_index("c")
        me, sibling = (x, y, c), (x, y, 1 - c)
        chips = [(1 - x, y), (x, 1 - y), (1 - x, 1 - y)]

        def slot(a, px, py, pc):
            return out_refs[a].at[4 * px + 2 * py + pc]

        def copy(a, k, blk, to, src=None):
            return _remote(slot(a, *blk) if src is None else src, slot(a, *blk),
                           send_sems.at[7 * a + k], recv_sems.at[7 * a + k], to)

        mine = [pltpu.make_async_copy(x_refs[a], slot(a, *me), local_sems.at[a]) for a in range(n)]
        first, passed = [], []
        for a in range(n):
            mine[a].start()
            first.append(copy(a, 0, me, sibling, src=x_refs[a]))
            first += [copy(a, 1 + j, me, (*chip, c), src=x_refs[a]) for j, chip in enumerate(chips)]
        for cp in first:
            cp.start()
        for a in range(n):
            for j, chip in enumerate(chips):
                copy(a, 1 + j, (*chip, c), me).wait_recv()
                passed.append(copy(a, 4 + j, (*chip, c), sibling))
                passed[-1].start()
        for a in range(n):
            copy(a, 0, sibling, me).wait_recv()
            for j, chip in enumerate(chips):
                copy(a, 4 + j, (*chip, 1 - c), me).wait_recv()
        for cp in first + passed:
            cp.wait_send()
        for cp in mine:
            cp.wait()

    return pl.pallas_call(
        body, name="all_gather_weights",
        out_shape=[jax.ShapeDtypeStruct((N_DEV,) + b.shape, b.dtype) for b in blocks],
        in_specs=_hbm_specs(n), out_specs=_hbm_specs(n),
        scratch_shapes=[pltpu.SemaphoreType.DMA((7 * n,)), pltpu.SemaphoreType.DMA((7 * n,)), pltpu.SemaphoreType.DMA((n,))],
    )(*blocks)


def _symmetric_comm(inputs, out_shape, aliases, per_array, copies):
    def start(in_refs, out_refs, sems):
        for cp in copies(in_refs, out_refs, *sems):
            cp.start()

    def finish(in_refs, out_refs, sems):
        for cp in copies(in_refs, out_refs, *sems):
            cp.wait()

    n_sems = per_array * len(inputs)
    return _Comm(tuple(inputs), tuple(out_shape), aliases,
                 (pltpu.SemaphoreType.DMA((n_sems,)), pltpu.SemaphoreType.DMA((n_sems,))), start, finish)


def _ag_direct(shards):
    def copies(in_refs, out_refs, send_sems, recv_sems):
        x, y, c = lax.axis_index("x"), lax.axis_index("y"), lax.axis_index("c")
        peers = [(x, y, 1 - c), (1 - x, y, c), (x, 1 - y, c), (1 - x, 1 - y, c)]
        cps = []
        for a in range(len(shards)):
            mine = out_refs[a].at[4 * x + 2 * y + c]
            cps.append(pltpu.make_async_copy(in_refs[a], mine, send_sems.at[5 * a]))
            cps += [_remote(in_refs[a], mine, send_sems.at[5 * a + k], recv_sems.at[5 * a + k], peer)
                    for k, peer in enumerate(peers, start=1)]
        return cps

    return _symmetric_comm(shards, [jax.ShapeDtypeStruct((N_DEV,) + s.shape, s.dtype) for s in shards], {}, 5, copies)


def _ag_forward(gathered):
    def copies(in_refs, out_refs, send_sems, recv_sems):
        x, y, c = lax.axis_index("x"), lax.axis_index("y"), lax.axis_index("c")
        chips = [(1 - x, y), (x, 1 - y), (1 - x, 1 - y)]
        return [_remote(out_refs[a].at[4 * cx + 2 * cy + c], out_refs[a].at[4 * cx + 2 * cy + c],
                        send_sems.at[3 * a + j], recv_sems.at[3 * a + j], (x, y, 1 - c))
                for a in range(len(gathered)) for j, (cx, cy) in enumerate(chips)]

    shapes = [jax.ShapeDtypeStruct(g.shape, g.dtype) for g in gathered]
    return _symmetric_comm(gathered, shapes, {a: a for a in range(len(gathered))}, 3, copies)


def _rs_to_sibling(grads):
    def copies(in_refs, out_refs, send_sems, recv_sems):
        x, y, c = lax.axis_index("x"), lax.axis_index("y"), lax.axis_index("c")
        return [_remote(in_refs[a].at[2 * q + 1 - c], out_refs[a].at[q], send_sems.at[4 * a + q], recv_sems.at[4 * a + q], (x, y, 1 - c))
                for a in range(len(grads)) for q in range(4)]

    return _symmetric_comm(grads, [jax.ShapeDtypeStruct((4,) + g.shape[1:], g.dtype) for g in grads], {}, 4, copies)


def _rs_to_chips(parts):
    def copies(in_refs, out_refs, send_sems, recv_sems):
        x, y, c = lax.axis_index("x"), lax.axis_index("y"), lax.axis_index("c")
        chips = [(1 - x, y), (x, 1 - y), (1 - x, 1 - y)]
        return [_remote(in_refs[a].at[2 * cx + cy], out_refs[a].at[k], send_sems.at[3 * a + k], recv_sems.at[3 * a + k], (cx, cy, c))
                for a in range(len(parts)) for k, (cx, cy) in enumerate(chips)]

    return _symmetric_comm(parts, [jax.ShapeDtypeStruct((3,) + p.shape[1:], p.dtype) for p in parts], {}, 3, copies)


def _comm_call(name, comm):
    n_in, n_out = len(comm.inputs), len(comm.out_shape)

    def body(*refs):
        ins, outs, sems = refs[:n_in], refs[n_in:n_in + n_out], refs[n_in + n_out:]
        comm.start(ins, outs, sems)
        comm.finish(ins, outs, sems)

    return pl.pallas_call(
        body, name=name, out_shape=list(comm.out_shape), in_specs=_hbm_specs(n_in), out_specs=_hbm_specs(n_out),
        scratch_shapes=list(comm.scratch), input_output_aliases=dict(comm.aliases),
    )(*comm.inputs)


def _small_all_reduce(parts, loss8):
    n = len(parts)

    def body(*refs):
        p_refs, loss_ref, out_ref, pack, land, send_sems, recv_sems = refs[:n], *refs[n:]
        x, y, c = lax.axis_index("x"), lax.axis_index("y"), lax.axis_index("c")
        me = 4 * x + 2 * y + c
        pack[...] = jnp.zeros_like(pack)
        for r, ref in enumerate(p_refs):
            pack[r:r + 1, 0:ref.shape[1]] = ref[...]
        pack[LOSS_ROW:LOSS_ROW + 1, 0:LANES] = loss_ref[0:1, :]
        land[me] = pack[...]
        cps = []
        for k in range(1, N_DEV):
            peer = (x ^ (k >> 2), y ^ ((k >> 1) & 1), c ^ (k & 1))
            cps.append(_remote(pack, land.at[me], send_sems.at[k - 1], recv_sems.at[k - 1], peer))
        for cp in cps:
            cp.start()
        for cp in cps:
            cp.wait()
        acc = land[0]
        for d in range(1, N_DEV):
            acc = acc + land[d]
        out_ref[...] = acc

    vmem = pl.BlockSpec(memory_space=pltpu.VMEM)
    return pl.pallas_call(
        body, name="small_all_reduce",
        out_shape=jax.ShapeDtypeStruct((SMALL_ROWS, D_MODEL), F32),
        in_specs=[vmem] * (n + 1), out_specs=vmem,
        scratch_shapes=[pltpu.VMEM((SMALL_ROWS, D_MODEL), F32), pltpu.VMEM((N_DEV, SMALL_ROWS, D_MODEL), F32),
                        pltpu.SemaphoreType.DMA((N_DEV - 1,)), pltpu.SemaphoreType.DMA((N_DEV - 1,))],
    )(*parts, loss8)


def _rs_sibling_sum(name, grad, got, core):
    _, rows, cols = grad.shape

    def body(c_ref, g_ref, l_ref, f_ref, b_ref):
        s = g_ref[...] + l_ref[...]
        f_ref[...] = s
        b_ref[...] = s.astype(BF16)

    by_chip = pl.BlockSpec((None, rows, cols), lambda q, c_ref: (q, 0, 0))
    return pl.pallas_call(
        body, name=name,
        grid_spec=pltpu.PrefetchScalarGridSpec(
            num_scalar_prefetch=1, grid=(4,),
            in_specs=[pl.BlockSpec((None, rows, cols), lambda q, c_ref: (2 * q + c_ref[0], 0, 0)), by_chip],
            out_specs=[by_chip, by_chip]),
        out_shape=(jax.ShapeDtypeStruct((4, rows, cols), F32), jax.ShapeDtypeStruct((4, rows, cols), BF16)),
        compiler_params=_cparams("parallel"),
    )(core, grad, got)


def _adamw_math(w, g, m, v):
    m2 = ADAM_B1 * m + (1.0 - ADAM_B1) * g
    v2 = ADAM_B2 * v + (1.0 - ADAM_B2) * (g * g)
    m_hat = m2 / (1.0 - ADAM_B1 ** ADAM_STEP)
    v_hat = v2 / (1.0 - ADAM_B2 ** ADAM_STEP)
    delta = -ADAM_LR * (m_hat / (jnp.sqrt(v_hat) + ADAM_EPS) + ADAM_WD * w)
    return delta, m2, v2


def _update_tile(rows):
    return 256 if rows % 256 == 0 else rows


def _rs_final_sum(name, chip_sums, got, chip):
    _, rows, cols = chip_sums.shape

    def body(q_ref, o_ref, r_ref, g_out):
        g = o_ref[...]
        for k in range(3):
            g = g + r_ref[k].astype(F32)
        g_out[...] = g

    return pl.pallas_call(
        body, name=name,
        grid_spec=pltpu.PrefetchScalarGridSpec(
            num_scalar_prefetch=1, grid=(1,),
            in_specs=[pl.BlockSpec((None, rows, cols), lambda i, q_ref: (q_ref[0], 0, 0)),
                      pl.BlockSpec((3, rows, cols), lambda i, q_ref: (0, 0, 0))],
            out_specs=pl.BlockSpec((rows, cols), lambda i, q_ref: (0, 0))),
        out_shape=jax.ShapeDtypeStruct((rows, cols), F32),
    )(chip, chip_sums, got)


def _adamw_sharded(name, w, m, v, chip_sums, got, chip):
    _, rows, cols = w.shape
    tr = _update_tile(rows)

    def body(q_ref, o_ref, r_ref, w_ref, m_ref, v_ref, g_out, d_out, m_out, v_out):
        g = o_ref[:, 0:cols]
        for k in range(3):
            g = g + r_ref[k, :, 0:cols].astype(F32)
        d, m2, v2 = _adamw_math(w_ref[0], g, m_ref[0], v_ref[0])
        g_out[0] = g
        d_out[0] = d
        m_out[0] = m2
        v_out[0] = v2

    own = pl.BlockSpec((1, tr, cols), lambda i, q_ref: (0, i, 0))
    shp = jax.ShapeDtypeStruct(w.shape, F32)
    wide = chip_sums.shape[2]
    return pl.pallas_call(
        body, name=name,
        grid_spec=pltpu.PrefetchScalarGridSpec(
            num_scalar_prefetch=1, grid=(rows // tr,),
            in_specs=[pl.BlockSpec((None, tr, wide), lambda i, q_ref: (q_ref[0], i, 0)),
                      pl.BlockSpec((3, tr, wide), lambda i, q_ref: (0, i, 0)), own, own, own],
            out_specs=[own] * 4),
        out_shape=(shp,) * 4,
        compiler_params=_cparams("parallel"),
    )(chip, chip_sums, got, w, m, v)


def _adamw_given(name, g, w, m, v):
    _, rows, cols = w.shape
    tr = _update_tile(rows)

    def body(g_ref, w_ref, m_ref, v_ref, g_out, d_out, m_out, v_out):
        g = g_ref[...]
        d, m2, v2 = _adamw_math(w_ref[0], g, m_ref[0], v_ref[0])
        g_out[0] = g
        d_out[0] = d
        m_out[0] = m2
        v_out[0] = v2

    own = pl.BlockSpec((1, tr, cols), lambda i: (0, i, 0))
    shp = jax.ShapeDtypeStruct(w.shape, F32)
    return pl.pallas_call(
        body, name=name, grid=(rows // tr,), out_shape=(shp,) * 4,
        in_specs=[pl.BlockSpec((tr, cols), lambda i: (i, 0)), own, own, own], out_specs=[own] * 4,
        compiler_params=_cparams("parallel"),
    )(g, w, m, v)


def _adamw_small(total, ws, ms, vs):
    n = len(ws)

    def body(*refs):
        t_ref = refs[0]
        w_refs, m_refs, v_refs = refs[1:1 + n], refs[1 + n:1 + 2 * n], refs[1 + 2 * n:1 + 3 * n]
        outs = refs[1 + 3 * n:]
        outs[0][...] = t_ref[LOSS_ROW:LOSS_ROW + 1, 0:1]
        for r in range(n):
            g = t_ref[r:r + 1, 0:w_refs[r].shape[1]]
            d, m2, v2 = _adamw_math(w_refs[r][...], g, m_refs[r][...], v_refs[r][...])
            for k, val in enumerate((g, d, m2, v2)):
                outs[1 + 4 * r + k][...] = val

    vmem = pl.BlockSpec(memory_space=pltpu.VMEM)
    out_shape = [jax.ShapeDtypeStruct((1, 1), F32)]
    for w in ws:
        out_shape += [jax.ShapeDtypeStruct(w.shape, F32)] * 4
    return pl.pallas_call(
        body, name="adamw_small", out_shape=out_shape,
        in_specs=[vmem] * (1 + 3 * n), out_specs=[vmem] * len(out_shape),
    )(total, *ws, *ms, *vs)


def _rope_tables(pos_col):
    t = pos_col.shape[0]
    inv = (np.float32(ROPE_THETA) ** (-np.arange(0, ROPE, 2, dtype=np.float32) / np.float32(ROPE))).astype(np.float32)
    freq = np.zeros((1, LANES), np.float32)
    freq[0, NOPE:NOPE + ROPE // 2] = inv
    freq[0, NOPE + ROPE // 2:NOPE + ROPE] = inv
    tm = _row_tile(t)

    def body(p_ref, f_ref, c_ref, s_ref):
        ang = p_ref[...].astype(F32) * f_ref[...]
        c_ref[...] = jnp.cos(ang)
        s_ref[...] = jnp.sin(ang)

    shp = jax.ShapeDtypeStruct((t, LANES), F32)
    return pl.pallas_call(
        body, name="rope_tables", grid=(t // tm,), out_shape=(shp, shp),
        in_specs=[pl.BlockSpec((tm, 1), lambda i: (i, 0)), pl.BlockSpec((1, LANES), lambda i: (0, 0))],
        out_specs=(pl.BlockSpec((tm, LANES), lambda i: (i, 0)),) * 2,
        compiler_params=_cparams("parallel"),
    )(pos_col, jnp.asarray(freq))


def _in_proj(x, g, w_qkv, w_rest):
    t = x.shape[0]
    tm = _row_tile(t)

    def body(x_ref, g_ref, wq_ref, wr_ref, h_ref, fq_ref, fk_ref, fv_ref, r_ref):
        h, _ = _rms(x_ref[...], g_ref[...])
        hb = h.astype(BF16)
        h_ref[...] = hb
        for n, ref in enumerate((fq_ref, fk_ref, fv_ref)):
            ref[...] = _dot_nt(hb, wq_ref[n * FOX_WIDTH:(n + 1) * FOX_WIDTH, :]).astype(BF16)
        r_ref[...] = _dot_nt(hb, wr_ref[...])

    row = lambda n: pl.BlockSpec((tm, n), lambda i: (i, 0))
    full = lambda a: pl.BlockSpec(a.shape, lambda i: (0,) * a.ndim)
    return pl.pallas_call(
        body, name="in_proj", grid=(t // tm,),
        out_shape=(jax.ShapeDtypeStruct((t, D_MODEL), BF16),) + (jax.ShapeDtypeStruct((t, FOX_WIDTH), BF16),) * 3
        + (jax.ShapeDtypeStruct((t, REST_COLS), F32),),
        in_specs=[row(D_MODEL), full(g), full(w_qkv), full(w_rest)],
        out_specs=(row(D_MODEL), row(FOX_WIDTH), row(FOX_WIDTH), row(FOX_WIDTH), row(REST_COLS)),
        compiler_params=_cparams("parallel"),
    )(x, g, w_qkv, w_rest)


def _log_sigmoid(z):
    return jnp.minimum(z, 0.0) - jnp.log(1.0 + jnp.exp(-jnp.abs(z)))


def _split3(v):
    hi = v.astype(BF16)
    r1 = v - hi.astype(F32)
    mid = r1.astype(BF16)
    lo = (r1 - mid.astype(F32)).astype(BF16)
    return hi, mid, lo


def _scan_tile(t):
    return 256 if t >= 256 else t


def _forget_cumsum(rest, b128):
    t = rest.shape[0]
    tb = _scan_tile(t)

    def body(r_ref, b_ref, row_ref, rep_ref, f_sc, carry):
        @pl.when(pl.program_id(0) == 0)
        def _():
            carry[...] = jnp.zeros_like(carry)
        lf = _log_sigmoid(r_ref[...] + b_ref[...])
        tri = (lax.broadcasted_iota(jnp.int32, (tb, tb), 0) >= lax.broadcasted_iota(jnp.int32, (tb, tb), 1)).astype(BF16)
        hi, mid, lo = _split3(lf)
        f_sc[...] = (_dot(tri, hi) + _dot(tri, mid)) + _dot(tri, lo) + carry[...]
        carry[...] = f_sc[tb - 1:tb, :]
        f2 = f_sc[...] * LOG2E
        row_ref[...] = jnp.transpose(f2)[0:HEADS, :]
        lane = _lane()
        for h in range(HEADS):
            col = jnp.sum(jnp.where(lane == h, f2, 0.0), axis=1, keepdims=True)
            rep_ref[h] = jnp.broadcast_to(col, (tb, LANES))

    return pl.pallas_call(
        body, name="forget_cumsum", grid=(t // tb,),
        out_shape=(jax.ShapeDtypeStruct((HEADS, t), F32), jax.ShapeDtypeStruct((HEADS, t, LANES), F32)),
        in_specs=[pl.BlockSpec((tb, LANES), lambda i: (i, 0)), pl.BlockSpec((1, LANES), lambda i: (0, 0))],
        out_specs=(pl.BlockSpec((HEADS, tb), lambda i: (0, i)), pl.BlockSpec((HEADS, tb, LANES), lambda i: (0, i, 0))),
        scratch_shapes=[pltpu.VMEM((tb, LANES), F32), pltpu.VMEM((1, LANES), F32)],
        compiler_params=_cparams("arbitrary"),
    )(rest, b128)


def _forget_bwd(rest, b128, d_fq, d_fk):
    t = rest.shape[0]
    tb = _scan_tile(t)
    nb = t // tb

    def body(r_ref, b_ref, dfq_ref, dfk_ref, dz_ref, db_ref, carry):
        @pl.when(pl.program_id(0) == 0)
        def _():
            carry[...] = jnp.zeros_like(carry)
            db_ref[...] = jnp.zeros_like(db_ref)
        tri = (lax.broadcasted_iota(jnp.int32, (tb, tb), 0) <= lax.broadcasted_iota(jnp.int32, (tb, tb), 1)).astype(BF16)
        lane = _lane()
        df = jnp.zeros((tb, LANES), F32)
        for h in range(HEADS):
            df = df + jnp.where(lane == h, dfq_ref[h] + dfk_ref[h], 0.0)
        hi, mid, lo = _split3(df)
        dlf = (_dot(tri, hi) + _dot(tri, mid)) + _dot(tri, lo) + carry[...]
        z = r_ref[...] + b_ref[...]
        dz = dlf / (1.0 + jnp.exp(z))
        dz_ref[...] = dz
        db_ref[...] += jnp.sum(dz, axis=0, keepdims=True)
        carry[...] = carry[...] + jnp.sum(df, axis=0, keepdims=True)

    rev = lambda i: (nb - 1 - i, 0)
    rev3 = pl.BlockSpec((HEADS, tb, LANES), lambda i: (0, nb - 1 - i, 0))
    return pl.pallas_call(
        body, name="forget_bwd", grid=(nb,),
        out_shape=(jax.ShapeDtypeStruct((t, LANES), F32), jax.ShapeDtypeStruct((1, LANES), F32)),
        in_specs=[pl.BlockSpec((tb, LANES), rev), pl.BlockSpec((1, LANES), lambda i: (0, 0)), rev3, rev3],
        out_specs=(pl.BlockSpec((tb, LANES), rev), pl.BlockSpec((1, LANES), lambda i: (0, 0))),
        scratch_shapes=[pltpu.VMEM((1, LANES), F32)],
        compiler_params=_cparams("arbitrary"),
    )(rest, b128, d_fq, d_fk)


def _mla_prep(rest, gq, gkv, wq, wkv, cos, sin):
    t = rest.shape[0]
    tm = _row_tile(t)

    def body(r_ref, gq_ref, gkv_ref, wq_ref, wkv_ref, c_ref, s_ref, q_ref, k_ref, kv_ref, cq_ref, ckv_ref):
        cos_, sin_ = c_ref[...], s_ref[...]
        cq, _ = _rms(r_ref[:, REST_CQ:REST_CKV], gq_ref[...])
        ckv, _ = _rms(r_ref[:, REST_CKV:REST_KR], gkv_ref[...])
        cqb, ckvb = cq.astype(BF16), ckv.astype(BF16)
        cq_ref[...] = cqb
        ckv_ref[...] = ckvb
        k_rope = _rope(r_ref[:, REST_KR:REST_COLS], cos_, sin_)
        lo = _lane() < NOPE
        for h in range(HEADS):
            q_ref[h] = _rope(_dot(cqb, wq_ref[h]), cos_, sin_).astype(BF16)
            kv = _dot(ckvb, wkv_ref[h])
            kv_ref[h] = kv.astype(BF16)
            k_ref[h] = (jnp.where(lo, kv, 0.0) + k_rope).astype(BF16)

    row = lambda n: pl.BlockSpec((tm, n), lambda i: (i, 0))
    full = lambda a: pl.BlockSpec(a.shape, lambda i: (0,) * a.ndim)
    heads = pl.BlockSpec((HEADS, tm, LANES), lambda i: (0, i, 0))
    hshape = jax.ShapeDtypeStruct((HEADS, t, LANES), BF16)
    return pl.pallas_call(
        body, name="mla_prep", grid=(t // tm,),
        out_shape=(hshape, hshape, hshape, jax.ShapeDtypeStruct((t, Q_RANK), BF16), jax.ShapeDtypeStruct((t, KV_RANK), BF16)),
        in_specs=[row(REST_COLS), full(gq), full(gkv), full(wq), full(wkv), row(LANES), row(LANES)],
        out_specs=(heads, heads, heads, row(Q_RANK), row(KV_RANK)),
        compiler_params=_cparams("parallel"),
    )(rest, gq, gkv, wq, wkv, cos, sin)


def _pair_specs(fox, t, tq, blocked_q):
    if fox:
        blk = pl.BlockSpec((tq, LANES), lambda p, i: (i, p))
        whole = pl.BlockSpec((t, LANES), lambda p, i: (0, p))
    else:
        blk = pl.BlockSpec((2, tq, LANES), lambda p, i: (p, i, 0))
        whole = pl.BlockSpec((2, t, LANES), lambda p, i: (p, 0, 0))
    return [blk, whole, whole] if blocked_q else [whole, blk, blk]


def _tile_lanes(x, n):
    return jnp.tile(x, (1, n)) if n > 1 else x


class _Comm(NamedTuple):
    inputs: tuple
    out_shape: tuple
    aliases: dict
    scratch: tuple
    start: Callable
    finish: Callable


def _hosted_call(name, main, grid, args, in_specs, out_shape, out_specs, scratch, comm):
    n_in, n_out, n_scr = len(args), len(out_shape), len(scratch)
    c_in = list(comm.inputs) if comm else []
    c_out = list(comm.out_shape) if comm else []

    def body(*refs):
        bounds = [0, n_in, len(c_in), n_out, len(c_out), n_scr]
        starts = [sum(bounds[:k + 1]) for k in range(len(bounds))]
        ins, cins, outs, couts, scr = [refs[a:b] for a, b in zip(starts[:-1], starts[1:])]
        sems = refs[starts[-1]:]
        if comm:
            @pl.when(jnp.logical_and(pl.program_id(0) == 0, pl.program_id(1) == 0))
            def _():
                comm.start(cins, couts, sems)
        main(ins, outs, scr)
        if comm:
            @pl.when(jnp.logical_and(pl.program_id(0) == grid[0] - 1, pl.program_id(1) == grid[1] - 1))
            def _():
                comm.finish(cins, couts, sems)

    res = pl.pallas_call(
        body, name=name, grid=grid,
        out_shape=list(out_shape) + c_out,
        in_specs=list(in_specs) + _hbm_specs(len(c_in)),
        out_specs=list(out_specs) + _hbm_specs(len(c_out)),
        scratch_shapes=list(scratch) + (list(comm.scratch) if comm else []),
        input_output_aliases={n_in + i: n_out + o for i, o in comm.aliases.items()} if comm else {},
        compiler_params=_cparams("arbitrary", "arbitrary"),
    )(*args, *c_in)
    return res[:n_out], res[n_out:]


def _pair_heads(fox, q_ref, lo):
    hi = jnp.logical_not(lo)
    if fox:
        zero = jnp.zeros((), BF16)
        return [jnp.where(lo, q_ref[...], zero), jnp.where(hi, q_ref[...], zero)], [hi, lo]
    return [q_ref[0], q_ref[1]], [lo, lo]


def _stat_rows(x):
    return jnp.transpose(x)[0:8, :]


def _attn_fwd(fox, q, k, v, f2_rows=None, comm=None):
    t = q.shape[0] if fox else q.shape[1]
    tq = _row_tile(t)
    nq = t // tq
    nrep = tq // LANES
    c2 = (FOX_SCALE if fox else MLA_SCALE) * LOG2E

    def main(ins, outs, scr):
        q_ref, k_ref, v_ref = ins[:3]
        fr_ref = ins[3] if fox else None
        o_ref, lset_ref = outs
        m_sc, acc_sc = scr
        i = pl.program_id(1)
        lo = _lane() < HEAD_DIM
        causal = lax.broadcasted_iota(jnp.int32, (tq, tq), 0) >= lax.broadcasted_iota(jnp.int32, (tq, tq), 1)
        qs, sum_lanes = _pair_heads(fox, q_ref, lo)
        one = jnp.ones((), BF16)
        m_sc[...] = jnp.full_like(m_sc, -jnp.inf)
        acc_sc[...] = jnp.zeros_like(acc_sc)

        def rows_of(ref, j, hh):
            sl = pl.ds(pl.multiple_of(j * tq, tq), tq)
            return ref[sl, :] if fox else ref[hh, sl, :]

        def step(j, masked):
            for hh in range(2):
                s = _dot_nt(qs[hh], rows_of(k_ref, j, hh)) * c2
                if fox:
                    s = s - fr_ref[hh, j]
                if masked:
                    s = jnp.where(causal, s, -jnp.inf)
                m_prev = m_sc[hh]
                m_new = jnp.maximum(m_prev, jnp.max(s, axis=1, keepdims=True))
                p = jnp.exp2(s - _tile_lanes(m_new, nrep))
                vj = jnp.where(sum_lanes[hh], one, rows_of(v_ref, j, hh))
                acc_sc[hh] = jnp.exp2(m_prev - m_new) * acc_sc[hh] + _dot(p.astype(BF16), vj)
                m_sc[hh] = m_new

        def loop_body(j, carry):
            step(j, False)
            return carry

        lax.fori_loop(0, i, loop_body, 0)
        step(i, True)
        outs = []
        for hh in range(2):
            acc = acc_sc[hh]
            swapped = pltpu.roll(acc, HEAD_DIM, 1)
            outs.append(acc / swapped)
            lse2 = m_sc[hh] + jnp.log(jnp.where(sum_lanes[hh], acc, swapped)) * LOG2E
            lset_ref[hh, 0] = _stat_rows(lse2)
        if fox:
            o_ref[...] = jnp.where(lo, outs[0], outs[1])
        else:
            o_ref[...] = jnp.where(lo, pltpu.roll(outs[0], HEAD_DIM, 1), outs[1])

    stat_rows = pl.BlockSpec((2, 1, 8, tq), lambda p, i: (p, i, 0, 0))
    in_specs = _pair_specs(fox, t, tq, True)
    args = [q, k, v]
    if fox:
        in_specs += [pl.BlockSpec((2, nq, 1, tq), lambda p, i: (p, 0, 0, 0))]
        args += [f2_rows]
    return _hosted_call(
        "fox_attn_fwd" if fox else "mla_attn_fwd", main, (HEADS // 2, nq), args, in_specs,
        (jax.ShapeDtypeStruct((t, 4 * LANES), F32), jax.ShapeDtypeStruct((HEADS, nq, 8, tq), F32)),
        (pl.BlockSpec((tq, LANES), lambda p, i: (i, p)), stat_rows),
        [pltpu.VMEM((2, tq, LANES), F32), pltpu.VMEM((2, tq, LANES), F32)], comm)


def _head_do(fox, hh, do2, lo):
    if fox:
        return jnp.where(lo if hh == 0 else jnp.logical_not(lo), do2, 0.0)
    return jnp.where(lo, 0.0, pltpu.roll(do2, HEAD_DIM, 1) if hh == 0 else do2)


def _attn_delta(name, do, o):
    t = do.shape[0]
    tq = _row_tile(t)

    def body(do_ref, o_ref, dl_ref):
        lo = _lane() < HEAD_DIM
        prod = do_ref[...] * o_ref[...]
        for hh in range(2):
            part = jnp.where(lo if hh == 0 else jnp.logical_not(lo), prod, 0.0)
            dl_ref[hh, 0] = _stat_rows(jnp.broadcast_to(jnp.sum(part, axis=1, keepdims=True), (tq, LANES)))

    pair = pl.BlockSpec((tq, LANES), lambda p, i: (i, p))
    return pl.pallas_call(
        body, name=name, grid=(HEADS // 2, t // tq),
        out_shape=jax.ShapeDtypeStruct((HEADS, t // tq, 8, tq), F32),
        in_specs=[pair, pair], out_specs=pl.BlockSpec((2, 1, 8, tq), lambda p, i: (p, i, 0, 0)),
        compiler_params=_cparams("parallel", "parallel"),
    )(do, o)


def _attn_bwd(fox, q, k, v, do, lse_rows, delta_rows, f2_rep=None, comm=None):
    t = q.shape[0] if fox else q.shape[1]
    tq = _row_tile(t)
    nq = t // tq
    nrep = tq // LANES
    scale = FOX_SCALE if fox else MLA_SCALE
    c2 = scale * LOG2E

    def main(ins, outs, scr):
        if fox:
            q_ref, k_ref, v_ref, f_ref, do_ref, lse_ref, dl_ref = ins
            dq_ref, dk_ref, dv_ref, dfq_ref, dfk_ref = outs
        else:
            q_ref, k_ref, v_ref, do_ref, lse_ref, dl_ref = ins
            dq_ref, dkv_ref, dkr_ref = outs
        dq_sc, dk_sc, dv_sc = scr
        j = pl.program_id(1)
        lane = _lane()
        lo = lane < HEAD_DIM
        hi = jnp.logical_not(lo)
        causal = lax.broadcasted_iota(jnp.int32, (tq, tq), 1) >= lax.broadcasted_iota(jnp.int32, (tq, tq), 0)
        zero, one = jnp.zeros((), BF16), jnp.ones((), BF16)

        @pl.when(j == 0)
        def _():
            dq_sc[...] = jnp.zeros_like(dq_sc)

        dk_sc[...] = jnp.zeros_like(dk_sc)
        dv_sc[...] = jnp.zeros_like(dv_sc)

        def step(i, masked):
            sl = pl.ds(pl.multiple_of(i * tq, tq), tq)
            do_i = do_ref[sl, :]
            for hh in range(2):
                kj = k_ref[...] if fox else k_ref[hh]
                vj = v_ref[...] if fox else v_ref[hh]
                qi = jnp.where(lo if hh == 0 else hi, q_ref[sl, :], zero) if fox else q_ref[hh, sl, :]
                dob = _head_do(fox, hh, do_i, lo).astype(BF16)
                st = _dot_nt(kj, qi) * c2
                if fox:
                    st = st - _tile_lanes(f_ref[hh], nrep)
                if masked:
                    st = jnp.where(causal, st, -jnp.inf)
                pt = jnp.exp2(st - lse_ref[hh, i, 0:1, :])
                dpt = _dot_nt(vj, dob)
                dst = (pt * (dpt - dl_ref[hh, i, 0:1, :])).astype(BF16)
                dv_sc[hh] += _dot(pt.astype(BF16), dob)
                if fox:
                    other = hi if hh == 0 else lo
                    qi = jnp.where(other, one, qi)
                    kj = jnp.where(other, one, kj)
                dk_sc[hh] += _dot(dst, qi)
                dq_sc[hh, sl, :] += _dot_tn(dst, kj)

        def loop_body(i, carry):
            step(i, False)
            return carry

        step(j, True)
        lax.fori_loop(j + 1, nq, loop_body, 0)
        if fox:
            dk_ref[...] = (jnp.where(lo, dk_sc[0], dk_sc[1]) * scale).astype(BF16)
            dv_ref[...] = (dv_sc[0] + dv_sc[1]).astype(BF16)
            for hh in range(2):
                dk = dk_sc[hh]
                dfk_ref[hh] = -jnp.where(hi if hh == 0 else lo, dk, pltpu.roll(dk, HEAD_DIM, 1))
        else:
            rope_lanes = jnp.logical_and(lane >= NOPE, lane < NOPE + ROPE)
            dkr = jnp.zeros((tq, LANES), F32)
            for hh in range(2):
                dk = dk_sc[hh] * scale
                dkv_ref[hh] = jnp.where(lo, dk, dv_sc[hh])
                dkr = dkr + jnp.where(rope_lanes, dk, 0.0)
            dkr_ref[0] = dkr

        @pl.when(j == nq - 1)
        def _():
            for i in range(nq):
                rows = slice(i * tq, (i + 1) * tq)
                if fox:
                    dq_ref[rows, :] = (jnp.where(lo, dq_sc[0, rows, :], dq_sc[1, rows, :]) * scale).astype(BF16)
                    for hh in range(2):
                        acc = dq_sc[hh, rows, :]
                        dfq_ref[hh, rows, :] = jnp.where(hi if hh == 0 else lo, acc, pltpu.roll(acc, HEAD_DIM, 1))
                else:
                    for hh in range(2):
                        dq_ref[hh, rows, :] = dq_sc[hh, rows, :] * scale

    stat = pl.BlockSpec((2, tq, LANES), lambda p, j: (p, j, 0))
    stat_all = pl.BlockSpec((2, t, LANES), lambda p, j: (p, 0, 0))
    rows4 = pl.BlockSpec((2, nq, 8, tq), lambda p, j: (p, 0, 0, 0))
    pair = pl.BlockSpec((tq, LANES), lambda p, j: (j, p))
    pair_all = pl.BlockSpec((t, LANES), lambda p, j: (0, p))
    in_specs = _pair_specs(fox, t, tq, False)
    args = [q, k, v]
    if fox:
        in_specs += [stat]
        args += [f2_rep]
    in_specs += [pair_all, rows4, rows4]
    args += [do, lse_rows, delta_rows]
    heads_f32 = jax.ShapeDtypeStruct((HEADS, t, LANES), F32)
    if fox:
        wide = jax.ShapeDtypeStruct((t, 4 * LANES), BF16)
        out_shape = (wide, wide, wide, heads_f32, heads_f32)
        out_specs = (pair_all, pair, pair, stat_all, stat)
    else:
        out_shape = (heads_f32, heads_f32, jax.ShapeDtypeStruct((HEADS // 2, t, LANES), F32))
        out_specs = (stat_all, stat, pl.BlockSpec((1, tq, LANES), lambda p, j: (p, j, 0)))
    acc = pltpu.VMEM((2, tq, LANES), F32)
    return _hosted_call("fox_attn_bwd" if fox else "mla_attn_bwd", main, (HEADS // 2, nq), args, in_specs,
                        out_shape, out_specs, [pltpu.VMEM((2, t, LANES), F32), acc, acc], comm)


def _attn_out(x, fox_o, mla_o, gf, gm, w_o):
    t = x.shape[0]
    tm = _row_tile(t)

    def body(x_ref, f_ref, m_ref, gf_ref, gm_ref, w_ref, x1_ref, mix_ref):
        nf, _ = _rms(f_ref[...], gf_ref[...])
        nm, _ = _rms(m_ref[...], gm_ref[...])
        nfb, nmb = nf.astype(BF16), nm.astype(BF16)
        mix_ref[:, :FOX_WIDTH] = nfb
        mix_ref[:, FOX_WIDTH:] = nmb
        x1_ref[...] = x_ref[...] + _dot(nfb, w_ref[:FOX_WIDTH, :]) + _dot(nmb, w_ref[FOX_WIDTH:, :])

    row = lambda n: pl.BlockSpec((tm, n), lambda i: (i, 0))
    full = lambda a: pl.BlockSpec(a.shape, lambda i: (0,) * a.ndim)
    return pl.pallas_call(
        body, name="attn_out", grid=(t // tm,),
        out_shape=(jax.ShapeDtypeStruct((t, D_MODEL), F32), jax.ShapeDtypeStruct((t, D_MODEL), BF16)),
        in_specs=[row(D_MODEL), row(FOX_WIDTH), row(MLA_WIDTH), full(gf), full(gm), full(w_o)],
        out_specs=(row(D_MODEL), row(D_MODEL)),
        compiler_params=_cparams("parallel"),
    )(x, fox_o, mla_o, gf, gm, w_o)


def _mlp_tile(t):
    return 256 if t >= 2048 else 128


def _resident(a):
    return pl.BlockSpec(a.shape, lambda i: (0,) * a.ndim, pipeline_mode=pl.Buffered(1))


FF_CHUNK = 512


def _mlp_fwd(x1, g_mlp, w_up, w_down, g_fin, target):
    t = x1.shape[0]
    tm = _mlp_tile(t)

    def body(x_ref, g_ref, wu_ref, wd_ref, gf_ref, t_ref, u_ref, h_ref, dx_ref, loss_ref, dg_ref, a_sc):
        @pl.when(pl.program_id(0) == 0)
        def _():
            loss_ref[...] = jnp.zeros_like(loss_ref)
            dg_ref[...] = jnp.zeros_like(dg_ref)

        x = x_ref[...]
        h, _ = _rms(x, g_ref[...])
        hb = h.astype(BF16)
        h_ref[...] = hb
        for f in range(D_FF // FF_CHUNK):
            sl = slice(f * FF_CHUNK, (f + 1) * FF_CHUNK)
            u = _dot(hb, wu_ref[:, sl])
            u_ref[:, sl] = u
            r = jnp.maximum(u, 0.0)
            a_sc[:, sl] = (r * r).astype(BF16)
        x2 = x + _dot(a_sc[...], wd_ref[...])
        y, r2 = _rms(x2, gf_ref[...])
        err = y - t_ref[...]
        loss_ref[...] += 0.5 * jnp.sum(jnp.mean(err * err, axis=-1, keepdims=True))
        dx, dg = _rms_bwd(x2, gf_ref[...], r2, err * (1.0 / D_MODEL))
        dx_ref[...] = dx
        dg_ref[...] += dg

    row = lambda n: pl.BlockSpec((tm, n), lambda i: (i, 0))
    vec = pl.BlockSpec((1, D_MODEL), lambda i: (0, 0))
    return pl.pallas_call(
        body, name="mlp_fwd", grid=(t // tm,),
        out_shape=(jax.ShapeDtypeStruct((t, D_FF), F32), jax.ShapeDtypeStruct((t, D_MODEL), BF16),
                   jax.ShapeDtypeStruct((t, D_MODEL), F32), jax.ShapeDtypeStruct((8, LANES), F32),
                   jax.ShapeDtypeStruct((1, D_MODEL), F32)),
        in_specs=[row(D_MODEL), vec, _resident(w_up), _resident(w_down), vec, row(D_MODEL)],
        out_specs=(row(D_FF), row(D_MODEL), row(D_MODEL), pl.BlockSpec((8, LANES), lambda i: (0, 0)), vec),
        scratch_shapes=[pltpu.VMEM((tm, D_FF), BF16)],
        compiler_params=_cparams("arbitrary"),
    )(x1, g_mlp, w_up, w_down, g_fin, target)


def _mlp_bwd(dx2, u, x1, g_mlp, w_up, w_down):
    t = x1.shape[0]
    tm = _mlp_tile(t)

    def body(dx_ref, u_ref, x_ref, g_ref, wu_ref, wd_ref, du_ref, a_ref, dx1_ref, dg_ref):
        @pl.when(pl.program_id(0) == 0)
        def _():
            dg_ref[...] = jnp.zeros_like(dg_ref)

        dx2 = dx_ref[...]
        dxb = dx2.astype(BF16)
        for f in range(D_FF // FF_CHUNK):
            sl = slice(f * FF_CHUNK, (f + 1) * FF_CHUNK)
            r = jnp.maximum(u_ref[:, sl], 0.0)
            a_ref[:, sl] = (r * r).astype(BF16)
            da = _dot_nt(dxb, wd_ref[sl, :])
            du_ref[:, sl] = (da * (2.0 * r)).astype(BF16)
        dh = _dot_nt(du_ref[...], wu_ref[...])
        x = x_ref[...]
        _, r1 = _rms(x, g_ref[...])
        dx, dg = _rms_bwd(x, g_ref[...], r1, dh)
        dx1_ref[...] = dx2 + dx
        dg_ref[...] += dg

    row = lambda n: pl.BlockSpec((tm, n), lambda i: (i, 0))
    vec = pl.BlockSpec((1, D_MODEL), lambda i: (0, 0))
    return pl.pallas_call(
        body, name="mlp_bwd", grid=(t // tm,),
        out_shape=(jax.ShapeDtypeStruct((t, D_FF), BF16), jax.ShapeDtypeStruct((t, D_FF), BF16),
                   jax.ShapeDtypeStruct((t, D_MODEL), F32), jax.ShapeDtypeStruct((1, D_MODEL), F32)),
        in_specs=[row(D_MODEL), row(D_FF), row(D_MODEL), vec, _resident(w_up), _resident(w_down)],
        out_specs=(row(D_FF), row(D_FF), row(D_MODEL), vec),
        compiler_params=_cparams("arbitrary"),
    )(dx2, u, x1, g_mlp, w_up, w_down)


def _matmul_tn(name, a, b, blocks=None):
    t, m = a.shape
    n = b.shape[1]
    tk = min(t, 2048)
    steps = t // tk
    bm = m if m <= 1024 else 512
    bn = n if n <= 1024 else 512
    width = bn if blocks is None else n // blocks
    per = bn // width

    def body(a_ref, b_ref, o_ref, acc_sc):
        kk = pl.program_id(2)

        @pl.when(kk == 0)
        def _():
            acc_sc[...] = jnp.zeros_like(acc_sc)

        acc_sc[...] += _dot_tn(a_ref[...].astype(BF16), b_ref[...].astype(BF16))

        @pl.when(kk == steps - 1)
        def _():
            if blocks is None:
                o_ref[...] = acc_sc[...]
            else:
                for s in range(per):
                    o_ref[s] = acc_sc[:, s * width:(s + 1) * width]

    if blocks is None:
        o_spec = pl.BlockSpec((bm, bn), lambda i, j, kk: (i, j))
        o_shape = (m, n)
    else:
        o_spec = pl.BlockSpec((per, bm, width), lambda i, j, kk: (j, i, 0))
        o_shape = (blocks, m, width)
    return pl.pallas_call(
        body, name=name, grid=(m // bm, n // bn, steps),
        out_shape=jax.ShapeDtypeStruct(o_shape, F32),
        in_specs=[pl.BlockSpec((tk, bm), lambda i, j, kk: (kk, i)), pl.BlockSpec((tk, bn), lambda i, j, kk: (kk, j))],
        out_specs=o_spec,
        scratch_shapes=[pltpu.VMEM((bm, bn), F32)],
        compiler_params=_cparams("parallel", "parallel", "arbitrary"),
    )(a, b)


def _attn_out_bwd(dx1, fox_o, mla_o, gf, gm, w_o):
    t = dx1.shape[0]
    tm = _row_tile(t)

    def body(dx_ref, f_ref, m_ref, gf_ref, gm_ref, w_ref, df_ref, dm_ref, dgf_ref, dgm_ref):
        @pl.when(pl.program_id(0) == 0)
        def _():
            dgf_ref[...] = jnp.zeros_like(dgf_ref)
            dgm_ref[...] = jnp.zeros_like(dgm_ref)
        dxb = dx_ref[...].astype(BF16)
        for o_ref, g_ref, lo_row, d_ref, dg_ref in ((f_ref, gf_ref, 0, df_ref, dgf_ref), (m_ref, gm_ref, FOX_WIDTH, dm_ref, dgm_ref)):
            dn = _dot_nt(dxb, w_ref[lo_row:lo_row + FOX_WIDTH, :])
            o = o_ref[...]
            _, r = _rms(o, g_ref[...])
            d, dg = _rms_bwd(o, g_ref[...], r, dn)
            d_ref[...] = d
            dg_ref[...] += dg

    row = lambda n: pl.BlockSpec((tm, n), lambda i: (i, 0))
    full = lambda a: pl.BlockSpec(a.shape, lambda i: (0,) * a.ndim)
    vec = pl.BlockSpec((1, FOX_WIDTH), lambda i: (0, 0))
    o_shape = jax.ShapeDtypeStruct((t, FOX_WIDTH), F32)
    g_shape = jax.ShapeDtypeStruct((1, FOX_WIDTH), F32)
    return pl.pallas_call(
        body, name="attn_out_bwd", grid=(t // tm,),
        out_shape=(o_shape, o_shape, g_shape, g_shape),
        in_specs=[row(D_MODEL), row(FOX_WIDTH), row(MLA_WIDTH), full(gf), full(gm), full(w_o)],
        out_specs=(row(FOX_WIDTH), row(MLA_WIDTH), vec, vec),
        compiler_params=_cparams("arbitrary"),
    )(dx1, fox_o, mla_o, gf, gm, w_o)


def _mla_prep_bwd(dq, dkv, dkr, dz, rest, gq, gkv, wq, wkv, cos, sin):
    t = rest.shape[0]
    tm = _row_tile(t)

    def body(dq_ref, dkv_ref, dkr_ref, dz_ref, r_ref, gq_ref, gkv_ref, wq_ref, wkv_ref, c_ref, s_ref,
             dr_ref, dqp_ref, dkvb_ref, dgq_ref, dgkv_ref):
        @pl.when(pl.program_id(0) == 0)
        def _():
            dgq_ref[...] = jnp.zeros_like(dgq_ref)
            dgkv_ref[...] = jnp.zeros_like(dgkv_ref)
        cos_, sin_ = c_ref[...], s_ref[...]
        dcq = jnp.zeros((tm, Q_RANK), F32)
        dckv = jnp.zeros((tm, KV_RANK), F32)
        for h in range(HEADS):
            dqp = _rope_bwd(dq_ref[h], cos_, sin_).astype(BF16)
            dqp_ref[:, h * LANES:(h + 1) * LANES] = dqp
            dcq = dcq + _dot_nt(dqp, wq_ref[h])
            dkvb = dkv_ref[h].astype(BF16)
            dkvb_ref[:, h * LANES:(h + 1) * LANES] = dkvb
            dckv = dckv + _dot_nt(dkvb, wkv_ref[h])
        dkrope = dkr_ref[0]
        for pr in range(1, HEADS // 2):
            dkrope = dkrope + dkr_ref[pr]
        cq = r_ref[:, REST_CQ:REST_CKV]
        _, rq = _rms(cq, gq_ref[...])
        d_cq, dgq = _rms_bwd(cq, gq_ref[...], rq, dcq)
        ckv = r_ref[:, REST_CKV:REST_KR]
        _, rkv = _rms(ckv, gkv_ref[...])
        d_ckv, dgkv = _rms_bwd(ckv, gkv_ref[...], rkv, dckv)
        dgq_ref[...] += dgq
        dgkv_ref[...] += dgkv
        dr_ref[:, 0:REST_CQ] = dz_ref[...].astype(BF16)
        dr_ref[:, REST_CQ:REST_CKV] = d_cq.astype(BF16)
        dr_ref[:, REST_CKV:REST_KR] = d_ckv.astype(BF16)
        dr_ref[:, REST_KR:REST_COLS] = _rope_bwd(dkrope, cos_, sin_).astype(BF16)

    row = lambda n: pl.BlockSpec((tm, n), lambda i: (i, 0))
    full = lambda a: pl.BlockSpec(a.shape, lambda i: (0,) * a.ndim)
    heads = pl.BlockSpec((HEADS, tm, LANES), lambda i: (0, i, 0))
    hshape = jax.ShapeDtypeStruct((t, HEADS * LANES), BF16)
    return pl.pallas_call(
        body, name="mla_prep_bwd", grid=(t // tm,),
        out_shape=(jax.ShapeDtypeStruct((t, REST_COLS), BF16), hshape, hshape,
                   jax.ShapeDtypeStruct((1, Q_RANK), F32), jax.ShapeDtypeStruct((1, KV_RANK), F32)),
        in_specs=[heads, heads, pl.BlockSpec((HEADS // 2, tm, LANES), lambda i: (0, i, 0)), row(LANES), row(REST_COLS),
                  full(gq), full(gkv), full(wq), full(wkv), row(LANES), row(LANES)],
        out_specs=(row(REST_COLS), row(HEADS * LANES), row(HEADS * LANES), pl.BlockSpec((1, Q_RANK), lambda i: (0, 0)),
                   pl.BlockSpec((1, KV_RANK), lambda i: (0, 0))),
        compiler_params=_cparams("arbitrary"),
    )(dq, dkv, dkr, dz, rest, gq, gkv, wq, wkv, cos, sin)


def _in_proj_bwd(x, g, dx1, dfq, dfk, dfv, drest, w_qkv, w_rest):
    t = x.shape[0]
    tm = _row_tile(t)

    def body(x_ref, g_ref, dx1_ref, dq_ref, dk_ref, dv_ref, dr_ref, wq_ref, wr_ref, dx_ref, dg_ref):
        @pl.when(pl.program_id(0) == 0)
        def _():
            dg_ref[...] = jnp.zeros_like(dg_ref)
        dh = _dot(dr_ref[...], wr_ref[...])
        for n, ref in enumerate((dq_ref, dk_ref, dv_ref)):
            dh = dh + _dot(ref[...], wq_ref[n * FOX_WIDTH:(n + 1) * FOX_WIDTH, :])
        xv = x_ref[...]
        _, r = _rms(xv, g_ref[...])
        dx, dg = _rms_bwd(xv, g_ref[...], r, dh)
        dx_ref[...] = dx1_ref[...] + dx
        dg_ref[...] += dg

    row = lambda n: pl.BlockSpec((tm, n), lambda i: (i, 0))
    full = lambda a: pl.BlockSpec(a.shape, lambda i: (0,) * a.ndim)
    vec = pl.BlockSpec((1, D_MODEL), lambda i: (0, 0))
    return pl.pallas_call(
        body, name="in_proj_bwd", grid=(t // tm,),
        out_shape=(jax.ShapeDtypeStruct((t, D_MODEL), F32), jax.ShapeDtypeStruct((1, D_MODEL), F32)),
        in_specs=[row(D_MODEL), full(g), row(D_MODEL), row(FOX_WIDTH), row(FOX_WIDTH), row(FOX_WIDTH), row(REST_COLS),
                  full(w_qkv), full(w_rest)],
        out_specs=(row(D_MODEL), vec),
        compiler_params=_cparams("arbitrary"),
    )(x, g, dx1, dfq, dfk, dfv, drest, w_qkv, w_rest)


def _pad_cols(a, n):
    return jnp.pad(a, ((0, 0),) * (a.ndim - 1) + ((0, n - a.shape[-1]),))


def kernel(x, positions, attn_norm_g, w_in, b_forget, q_norm_g, w_uq, kv_norm_g, w_ukv, fox_out_g, mla_out_g, w_o, mlp_norm_g, w_up, w_down, final_norm_g, loss_target, m_attn_norm_g, m_w_in, m_b_forget, m_q_norm_g, m_w_uq, m_kv_norm_g, m_w_ukv, m_fox_out_g, m_mla_out_g, m_w_o, m_mlp_norm_g, m_w_up, m_w_down, m_final_norm_g, v_attn_norm_g, v_w_in, v_b_forget, v_q_norm_g, v_w_uq, v_kv_norm_g, v_w_ukv, v_fox_out_g, v_mla_out_g, v_w_o, v_mlp_norm_g, v_w_up, v_w_down, v_final_norm_g):
    t = x.shape[1]
    tq = _row_tile(t)
    xs = x[0]
    target = loss_target[0]

    early = [jnp.transpose(w_in[0]), _pad_cols(w_uq[0], LANES), w_ukv[0]]
    late = [w_o[0].astype(BF16), w_up[0].astype(BF16), w_down[0].astype(BF16)]
    g_in, wq, wkv = _all_gather([s.astype(BF16) for s in early])
    win = g_in.reshape(IN_COLS, D_MODEL)
    off_ff, off_cq, off_kr = 3 * FOX_WIDTH, 3 * FOX_WIDTH + HEADS, IN_COLS - ROPE
    zeros = lambda n: jnp.zeros((n, D_MODEL), BF16)
    w_qkv = win[:off_ff]
    w_rest = jnp.concatenate([
        win[off_ff:off_cq], zeros(REST_CQ - HEADS), win[off_cq:off_kr],
        zeros(NOPE), win[off_kr:], zeros(LANES - NOPE - ROPE)], axis=0)

    cos, sin = _rope_tables(positions.reshape(t, 1))
    h1, fq, fk, fv, rest = _in_proj(xs, attn_norm_g, w_qkv, w_rest)
    b128 = _pad_cols(b_forget, LANES)
    f2_rows, f2_rep = _forget_cumsum(rest, b128)
    f2_rows = f2_rows.reshape(HEADS, t // tq, 1, tq)
    (fox_o, fox_lse_rows), partly = _attn_fwd(True, fq, fk, fv, f2_rows, comm=_ag_direct(late))
    mq, mk, mkv, cqn, ckvn = _mla_prep(rest, q_norm_g, kv_norm_g, wq, wkv, cos, sin)
    (mla_o, mla_lse_rows), (g_o, g_up, g_down) = _attn_fwd(False, mq, mk, mkv, comm=_ag_forward(partly))
    wo = g_o.reshape(D_MODEL, D_MODEL)
    x1, mixed = _attn_out(xs, fox_o, mla_o, fox_out_g, mla_out_g, wo)
    wup = jnp.transpose(g_up, (1, 0, 2)).reshape(D_MODEL, D_FF)
    wdown = g_down.reshape(D_FF, D_MODEL)
    u, h2, dx2, loss8, d_gfin = _mlp_fwd(x1, mlp_norm_g, wup, wdown, final_norm_g.reshape(1, D_MODEL), target)

    du, act, dx1, d_gmlp = _mlp_bwd(dx2, u, x1, mlp_norm_g, wup, wdown)
    dw_down = _matmul_tn("dw_down", act, dx2)
    dw_up = _matmul_tn("dw_up", h2, du, blocks=N_DEV)
    dfox_o, dmla_o, d_gfox, d_gmla = _attn_out_bwd(dx1, fox_o, mla_o, fox_out_g, mla_out_g, wo)
    dw_o = _matmul_tn("dw_o", mixed, dx1)

    core = lax.axis_index("c").astype(jnp.int32).reshape(1)
    chip = (2 * lax.axis_index("x") + lax.axis_index("y")).astype(jnp.int32).reshape(1)
    names = ("w_in", "w_uq", "w_ukv", "w_o", "w_up", "w_down")
    grads_b = [dw_o.reshape(N_DEV, -1, D_MODEL), dw_up, dw_down.reshape(N_DEV, -1, D_MODEL)]
    (dfq, dfk, dfv, d_fq, d_fk), got_b = _attn_bwd(True, fq, fk, fv, dfox_o, fox_lse_rows, _attn_delta("fox_attn_delta", dfox_o, fox_o),
                                                   f2_rep, comm=_rs_to_sibling(grads_b))
    sums_b = [_rs_sibling_sum("rs_sibling_sum_" + nm, g, l, core) for nm, g, l in zip(names[3:], grads_b, got_b)]
    dz, d_b = _forget_bwd(rest, b128, d_fq, d_fk)
    (dmq, dmkv, dmkr), others_b = _attn_bwd(False, mq, mk, mkv, dmla_o, mla_lse_rows, _attn_delta("mla_attn_delta", dmla_o, mla_o),
                                            comm=_rs_to_chips([s[1] for s in sums_b]))
    drest, dqp, dkvb, d_gq, d_gkv = _mla_prep_bwd(dmq, dmkv, dmkr, dz, rest, q_norm_g, kv_norm_g, wq, wkv, cos, sin)
    dw_uq = _matmul_tn("dw_uq", cqn, dqp, blocks=HEADS)
    dw_ukv = _matmul_tn("dw_ukv", ckvn, dkvb, blocks=HEADS)
    grad_x, d_gattn = _in_proj_bwd(xs, attn_norm_g, dx1, dfq, dfk, dfv, drest, w_qkv, w_rest)
    dw_q = _matmul_tn("dw_in_q", dfq, h1)
    dw_k = _matmul_tn("dw_in_k", dfk, h1)
    dw_v = _matmul_tn("dw_in_v", dfv, h1)
    dw_r = _matmul_tn("dw_in_rest", drest, h1)

    dw_in = jnp.concatenate([dw_q, dw_k, dw_v, dw_r[0:HEADS], dw_r[REST_CQ:REST_KR],
                             dw_r[REST_KR + NOPE:REST_KR + NOPE + ROPE]], axis=0)
    grads_a = [dw_in.reshape(N_DEV, IN_SHARD, D_MODEL), dw_uq, dw_ukv]
    got_a = _comm_call("rs_sibling_exchange", _rs_to_sibling(grads_a))
    sums_a = [_rs_sibling_sum("rs_sibling_sum_" + nm, g, l, core) for nm, g, l in zip(names[:3], grads_a, got_a)]
    others_a = _comm_call("rs_chip_exchange", _rs_to_chips([s[1] for s in sums_a]))
    sums, others = sums_a + sums_b, list(others_a) + list(others_b)
    sharded = (w_in, w_uq, w_ukv, w_o, w_up, w_down)
    moments_m = (m_w_in, m_w_uq, m_w_ukv, m_w_o, m_w_up, m_w_down)
    moments_v = (v_w_in, v_w_uq, v_w_ukv, v_w_o, v_w_up, v_w_down)
    g_in_t = _rs_final_sum("rs_final_sum_w_in", sums[0][0], others[0], chip)
    big = [_adamw_given("adamw_w_in", jnp.transpose(g_in_t), w_in, m_w_in, v_w_in)]
    for a in range(1, len(names)):
        big.append(_adamw_sharded("adamw_" + names[a], sharded[a], moments_m[a], moments_v[a], sums[a][0], others[a], chip))
    big_g, big_d, big_m, big_v = [[b[k] for b in big] for k in range(4)]

    as_row = lambda a: a.reshape(1, -1)
    small_w = (attn_norm_g, b_forget, q_norm_g, kv_norm_g, fox_out_g, mla_out_g, mlp_norm_g, final_norm_g)
    small_m = (m_attn_norm_g, m_b_forget, m_q_norm_g, m_kv_norm_g, m_fox_out_g, m_mla_out_g, m_mlp_norm_g, m_final_norm_g)
    small_v = (v_attn_norm_g, v_b_forget, v_q_norm_g, v_kv_norm_g, v_fox_out_g, v_mla_out_g, v_mlp_norm_g, v_final_norm_g)
    total = _small_all_reduce([d_gattn, d_b, d_gq, d_gkv, d_gfox, d_gmla, d_gmlp, d_gfin], loss8)
    small = _adamw_small(total, [as_row(a) for a in small_w], [as_row(a) for a in small_m], [as_row(a) for a in small_v])
    loss = small[0].reshape(())
    s_g, s_d, s_m, s_v = [[small[1 + 4 * r + k].reshape(small_w[r].shape) for r in range(len(small_w))] for k in range(4)]

    def ordered(small_, bigs):
        ga, bf, gq_, gkv_, gfo, gml, gmlp_, gfin_ = small_
        bin_, buq, bukv, bo, bup, bdown = bigs
        return [ga, bin_, bf, gq_, buq, gkv_, bukv, gfo, gml, bo, gmlp_, bup, bdown, gfin_]

    return (loss, grad_x[None], *ordered(s_g, big_g), *ordered(s_d, big_d), *ordered(s_m, big_m), *ordered(s_v, big_v))
```

```python
import math
from typing import Callable, NamedTuple

import numpy as np
import jax
import jax.numpy as jnp
from jax import lax
from jax.experimental import pallas as pl
from jax.experimental.pallas import tpu as pltpu

F32 = jnp.float32
BF16 = jnp.bfloat16
MESH = pl.DeviceIdType.MESH

D_MODEL = 1024
HEADS = 8
HEAD_DIM = 64
FOX_WIDTH = 512
MLA_WIDTH = 512
NOPE = 64
ROPE = 32
QK_DIM = 96
Q_RANK = 384
KV_RANK = 256
D_FF = 4096
IN_COLS = 2216
ROPE_THETA = 10000.0
EPS = 1e-6
FOX_SCALE = 1.0 / math.sqrt(HEAD_DIM)
MLA_SCALE = 1.0 / math.sqrt(QK_DIM)
ADAM_LR = 0.001
ADAM_B1 = 0.9
ADAM_B2 = 0.999
ADAM_EPS = 1e-08
ADAM_WD = 0.01
ADAM_STEP = 10

N_DEV = 8
LANES = 128
REST_COLS = 896
REST_CQ = LANES
REST_CKV = REST_CQ + Q_RANK
REST_KR = REST_CKV + KV_RANK
LOG2E = 1.4426950408889634
VMEM_LIMIT = 56 * 1024 * 1024

IN_SHARD = IN_COLS // N_DEV
SMALL_SIZES = (1024, 8, 384, 256, 512, 512, 1024, 1024)
SMALL_ROWS = 16
LOSS_ROW = len(SMALL_SIZES)


def _cparams(*sem):
    return pltpu.CompilerParams(dimension_semantics=sem or None, vmem_limit_bytes=VMEM_LIMIT)


def _row_tile(t):
    return 512 if t >= 2048 else 128


def _dot(a, b):
    return jnp.dot(a, b, preferred_element_type=F32)


def _dot_nt(a, b):
    return lax.dot_general(a, b, (((1,), (1,)), ((), ())), preferred_element_type=F32)


def _dot_tn(a, b):
    return lax.dot_general(a, b, (((0,), (0,)), ((), ())), preferred_element_type=F32)


def _rms(x, g):
    r = lax.rsqrt(jnp.mean(x * x, axis=-1, keepdims=True) + EPS)
    return x * r * g, r


def _rms_bwd(x, g, r, dy):
    xh = x * r
    gdy = dy * g
    dx = r * (gdy - xh * jnp.mean(gdy * xh, axis=-1, keepdims=True))
    return dx, jnp.sum(dy * xh, axis=0, keepdims=True)


def _lane():
    return lax.broadcasted_iota(jnp.int32, (1, LANES), 1)


def _rot(x):
    lane = _lane()
    half = NOPE + ROPE // 2
    first = jnp.logical_and(lane >= NOPE, lane < half)
    second = jnp.logical_and(lane >= half, lane < NOPE + ROPE)
    return jnp.where(first, -pltpu.roll(x, LANES - ROPE // 2, 1), jnp.where(second, pltpu.roll(x, ROPE // 2, 1), 0.0))


def _rope(x, cos, sin):
    return x * cos + _rot(x) * sin


def _rope_bwd(dy, cos, sin):
    return dy * cos - _rot(dy * sin)


def _remote(src, dst, send_sem, recv_sem, to):
    return pltpu.make_async_remote_copy(src_ref=src, dst_ref=dst, send_sem=send_sem, recv_sem=recv_sem,
                                        device_id=to, device_id_type=MESH)


def _hbm_specs(n):
    return [pl.BlockSpec(memory_space=pl.ANY)] * n


def _all_gather(blocks):
    n = len(blocks)

    def body(*refs):
        x_refs, out_refs = refs[:n], refs[n:2 * n]
        send_sems, recv_sems, local_sems = refs[2 * n:]
        x, y, c = lax.axis_index("x"), lax.axis_index("y"), lax.axis_index("c")
        me, sibling = (x, y, c), (x, y, 1 - c)
        chips = [(1 - x, y), (x, 1 - y), (1 - x, 1 - y)]

        def slot(a, px, py, pc):
            return out_refs[a].at[4 * px + 2 * py + pc]

        def copy(a, k, blk, to, src=None):
            return _remote(slot(a, *blk) if src is None else src, slot(a, *blk),
                           send_sems.at[7 * a + k], recv_sems.at[7 * a + k], to)

        mine = [pltpu.make_async_copy(x_refs[a], slot(a, *me), local_sems.at[a]) for a in range(n)]
        first, passed = [], []
        for a in range(n):
            mine[a].start()
            first.append(copy(a, 0, me, sibling, src=x_refs[a]))
            first += [copy(a, 1 + j, me, (*chip, c), src=x_refs[a]) for j, chip in enumerate(chips)]
        for cp in first:
            cp.start()
        for a in range(n):
            for j, chip in enumerate(chips):
                copy(a, 1 + j, (*chip, c), me).wait_recv()
                passed.append(copy(a, 4 + j, (*chip, c), sibling))
                passed[-1].start()
        for a in range(n):
            copy(a, 0, sibling, me).wait_recv()
            for j, chip in enumerate(chips):
                copy(a, 4 + j, (*chip, 1 - c), me).wait_recv()
        for cp in first + passed:
            cp.wait_send()
        for cp in mine:
            cp.wait()

    return pl.pallas_call(
        body, name="all_gather_weights",
        out_shape=[jax.ShapeDtypeStruct((N_DEV,) + b.shape, b.dtype) for b in blocks],
        in_specs=_hbm_specs(n), out_specs=_hbm_specs(n),
        scratch_shapes=[pltpu.SemaphoreType.DMA((7 * n,)), pltpu.SemaphoreType.DMA((7 * n,)), pltpu.SemaphoreType.DMA((n,))],
    )(*blocks)


def _symmetric_comm(inputs, out_shape, aliases, per_array, copies):
    def start(in_refs, out_refs, sems):
        for cp in copies(in_refs, out_refs, *sems):
            cp.start()

    def finish(in_refs, out_refs, sems):
        for cp in copies(in_refs, out_refs, *sems):
            cp.wait()

    n_sems = per_array * len(inputs)
    return _Comm(tuple(inputs), tuple(out_shape), aliases,
                 (pltpu.SemaphoreType.DMA((n_sems,)), pltpu.SemaphoreType.DMA((n_sems,))), start, finish)


def _ag_direct(shards):
    def copies(in_refs, out_refs, send_sems, recv_sems):
        x, y, c = lax.axis_index("x"), lax.axis_index("y"), lax.axis_index("c")
        peers = [(x, y, 1 - c), (1 - x, y, c), (x, 1 - y, c), (1 - x, 1 - y, c)]
        cps = []
        for a in range(len(shards)):
            mine = out_refs[a].at[4 * x + 2 * y + c]
            cps.append(pltpu.make_async_copy(in_refs[a], mine, send_sems.at[5 * a]))
            cps += [_remote(in_refs[a], mine, send_sems.at[5 * a + k], recv_sems.at[5 * a + k], peer)
                    for k, peer in enumerate(peers, start=1)]
        return cps

    return _symmetric_comm(shards, [jax.ShapeDtypeStruct((N_DEV,) + s.shape, s.dtype) for s in shards], {}, 5, copies)


def _ag_forward(gathered):
    def copies(in_refs, out_refs, send_sems, recv_sems):
        x, y, c = lax.axis_index("x"), lax.axis_index("y"), lax.axis_index("c")
        chips = [(1 - x, y), (x, 1 - y), (1 - x, 1 - y)]
        return [_remote(out_refs[a].at[4 * cx + 2 * cy + c], out_refs[a].at[4 * cx + 2 * cy + c],
                        send_sems.at[3 * a + j], recv_sems.at[3 * a + j], (x, y, 1 - c))
                for a in range(len(gathered)) for j, (cx, cy) in enumerate(chips)]

    shapes = [jax.ShapeDtypeStruct(g.shape, g.dtype) for g in gathered]
    return _symmetric_comm(gathered, shapes, {a: a for a in range(len(gathered))}, 3, copies)


def _rs_to_sibling(grads):
    def copies(in_refs, out_refs, send_sems, recv_sems):
        x, y, c = lax.axis_index("x"), lax.axis_index("y"), lax.axis_index("c")
        return [_remote(in_refs[a].at[2 * q + 1 - c], out_refs[a].at[q], send_sems.at[4 * a + q], recv_sems.at[4 * a + q], (x, y, 1 - c))
                for a in range(len(grads)) for q in range(4)]

    return _symmetric_comm(grads, [jax.ShapeDtypeStruct((4,) + g.shape[1:], g.dtype) for g in grads], {}, 4, copies)


def _rs_to_chips(parts):
    def copies(in_refs, out_refs, send_sems, recv_sems):
        x, y, c = lax.axis_index("x"), lax.axis_index("y"), lax.axis_index("c")
        chips = [(1 - x, y), (x, 1 - y), (1 - x, 1 - y)]
        return [_remote(in_refs[a].at[2 * cx + cy], out_refs[a].at[k], send_sems.at[3 * a + k], recv_sems.at[3 * a + k], (cx, cy, c))
                for a in range(len(parts)) for k, (cx, cy) in enumerate(chips)]

    return _symmetric_comm(parts, [jax.ShapeDtypeStruct((3,) + p.shape[1:], p.dtype) for p in parts], {}, 3, copies)


def _comm_call(name, comm):
    n_in, n_out = len(comm.inputs), len(comm.out_shape)

    def body(*refs):
        ins, outs, sems = refs[:n_in], refs[n_in:n_in + n_out], refs[n_in + n_out:]
        comm.start(ins, outs, sems)
        comm.finish(ins, outs, sems)

    return pl.pallas_call(
        body, name=name, out_shape=list(comm.out_shape), in_specs=_hbm_specs(n_in), out_specs=_hbm_specs(n_out),
        scratch_shapes=list(comm.scratch), input_output_aliases=dict(comm.aliases),
    )(*comm.inputs)


def _small_all_reduce(parts, loss8):
    n = len(parts)

    def body(*refs):
        p_refs, loss_ref, out_ref, pack, land, send_sems, recv_sems = refs[:n], *refs[n:]
        x, y, c = lax.axis_index("x"), lax.axis_index("y"), lax.axis_index("c")
        me = 4 * x + 2 * y + c
        pack[...] = jnp.zeros_like(pack)
        for r, ref in enumerate(p_refs):
            pack[r:r + 1, 0:ref.shape[1]] = ref[...]
        pack[LOSS_ROW:LOSS_ROW + 1, 0:LANES] = loss_ref[0:1, :]
        land[me] = pack[...]
        cps = []
        for k in range(1, N_DEV):
            peer = (x ^ (k >> 2), y ^ ((k >> 1) & 1), c ^ (k & 1))
            cps.append(_remote(pack, land.at[me], send_sems.at[k - 1], recv_sems.at[k - 1], peer))
        for cp in cps:
            cp.start()
        for cp in cps:
            cp.wait()
        acc = land[0]
        for d in range(1, N_DEV):
            acc = acc + land[d]
        out_ref[...] = acc

    vmem = pl.BlockSpec(memory_space=pltpu.VMEM)
    return pl.pallas_call(
        body, name="small_all_reduce",
        out_shape=jax.ShapeDtypeStruct((SMALL_ROWS, D_MODEL), F32),
        in_specs=[vmem] * (n + 1), out_specs=vmem,
        scratch_shapes=[pltpu.VMEM((SMALL_ROWS, D_MODEL), F32), pltpu.VMEM((N_DEV, SMALL_ROWS, D_MODEL), F32),
                        pltpu.SemaphoreType.DMA((N_DEV - 1,)), pltpu.SemaphoreType.DMA((N_DEV - 1,))],
    )(*parts, loss8)


def _rs_sibling_sum(name, grad, got, core):
    _, rows, cols = grad.shape

    def body(c_ref, g_ref, l_ref, f_ref, b_ref):
        s = g_ref[...] + l_ref[...]
        f_ref[...] = s
        b_ref[...] = s.astype(BF16)

    by_chip = pl.BlockSpec((None, rows, cols), lambda q, c_ref: (q, 0, 0))
    return pl.pallas_call(
        body, name=name,
        grid_spec=pltpu.PrefetchScalarGridSpec(
            num_scalar_prefetch=1, grid=(4,),
            in_specs=[pl.BlockSpec((None, rows, cols), lambda q, c_ref: (2 * q + c_ref[0], 0, 0)), by_chip],
            out_specs=[by_chip, by_chip]),
        out_shape=(jax.ShapeDtypeStruct((4, rows, cols), F32), jax.ShapeDtypeStruct((4, rows, cols), BF16)),
        compiler_params=_cparams("parallel"),
    )(core, grad, got)


def _adamw_math(w, g, m, v):
    m2 = ADAM_B1 * m + (1.0 - ADAM_B1) * g
    v2 = ADAM_B2 * v + (1.0 - ADAM_B2) * (g * g)
    m_hat = m2 / (1.0 - ADAM_B1 ** ADAM_STEP)
    v_hat = v2 / (1.0 - ADAM_B2 ** ADAM_STEP)
    delta = -ADAM_LR * (m_hat / (jnp.sqrt(v_hat) + ADAM_EPS) + ADAM_WD * w)
    return delta, m2, v2


def _update_tile(rows):
    return 256 if rows % 256 == 0 else rows


def _rs_final_sum(name, chip_sums, got, chip):
    _, rows, cols = chip_sums.shape

    def body(q_ref, o_ref, r_ref, g_out):
        g = o_ref[...]
        for k in range(3):
            g = g + r_ref[k].astype(F32)
        g_out[...] = g

    return pl.pallas_call(
        body, name=name,
        grid_spec=pltpu.PrefetchScalarGridSpec(
            num_scalar_prefetch=1, grid=(1,),
            in_specs=[pl.BlockSpec((None, rows, cols), lambda i, q_ref: (q_ref[0], 0, 0)),
                      pl.BlockSpec((3, rows, cols), lambda i, q_ref: (0, 0, 0))],
            out_specs=pl.BlockSpec((rows, cols), lambda i, q_ref: (0, 0))),
        out_shape=jax.ShapeDtypeStruct((rows, cols), F32),
    )(chip, chip_sums, got)


def _adamw_sharded(name, w, m, v, chip_sums, got, chip):
    _, rows, cols = w.shape
    tr = _update_tile(rows)

    def body(q_ref, o_ref, r_ref, w_ref, m_ref, v_ref, g_out, d_out, m_out, v_out):
        g = o_ref[:, 0:cols]
        for k in range(3):
            g = g + r_ref[k, :, 0:cols].astype(F32)
        d, m2, v2 = _adamw_math(w_ref[0], g, m_ref[0], v_ref[0])
        g_out[0] = g
        d_out[0] = d
        m_out[0] = m2
        v_out[0] = v2

    own = pl.BlockSpec((1, tr, cols), lambda i, q_ref: (0, i, 0))
    shp = jax.ShapeDtypeStruct(w.shape, F32)
    wide = chip_sums.shape[2]
    return pl.pallas_call(
        body, name=name,
        grid_spec=pltpu.PrefetchScalarGridSpec(
            num_scalar_prefetch=1, grid=(rows // tr,),
            in_specs=[pl.BlockSpec((None, tr, wide), lambda i, q_ref: (q_ref[0], i, 0)),
                      pl.BlockSpec((3, tr, wide), lambda i, q_ref: (0, i, 0)), own, own, own],
            out_specs=[own] * 4),
        out_shape=(shp,) * 4,
        compiler_params=_cparams("parallel"),
    )(chip, chip_sums, got, w, m, v)


def _adamw_given(name, g, w, m, v):
    _, rows, cols = w.shape
    tr = _update_tile(rows)

    def body(g_ref, w_ref, m_ref, v_ref, g_out, d_out, m_out, v_out):
        g = g_ref[...]
        d, m2, v2 = _adamw_math(w_ref[0], g, m_ref[0], v_ref[0])
        g_out[0] = g
        d_out[0] = d
        m_out[0] = m2
        v_out[0] = v2

    own = pl.BlockSpec((1, tr, cols), lambda i: (0, i, 0))
    shp = jax.ShapeDtypeStruct(w.shape, F32)
    return pl.pallas_call(
        body, name=name, grid=(rows // tr,), out_shape=(shp,) * 4,
        in_specs=[pl.BlockSpec((tr, cols), lambda i: (i, 0)), own, own, own], out_specs=[own] * 4,
        compiler_params=_cparams("parallel"),
    )(g, w, m, v)


def _adamw_small(total, ws, ms, vs):
    n = len(ws)

    def body(*refs):
        t_ref = refs[0]
        w_refs, m_refs, v_refs = refs[1:1 + n], refs[1 + n:1 + 2 * n], refs[1 + 2 * n:1 + 3 * n]
        outs = refs[1 + 3 * n:]
        outs[0][...] = t_ref[LOSS_ROW:LOSS_ROW + 1, 0:1]
        for r in range(n):
            g = t_ref[r:r + 1, 0:w_refs[r].shape[1]]
            d, m2, v2 = _adamw_math(w_refs[r][...], g, m_refs[r][...], v_refs[r][...])
            for k, val in enumerate((g, d, m2, v2)):
                outs[1 + 4 * r + k][...] = val

    vmem = pl.BlockSpec(memory_space=pltpu.VMEM)
    out_shape = [jax.ShapeDtypeStruct((1, 1), F32)]
    for w in ws:
        out_shape += [jax.ShapeDtypeStruct(w.shape, F32)] * 4
    return pl.pallas_call(
        body, name="adamw_small", out_shape=out_shape,
        in_specs=[vmem] * (1 + 3 * n), out_specs=[vmem] * len(out_shape),
    )(total, *ws, *ms, *vs)


def _rope_tables(pos_col):
    t = pos_col.shape[0]
    inv = (np.float32(ROPE_THETA) ** (-np.arange(0, ROPE, 2, dtype=np.float32) / np.float32(ROPE))).astype(np.float32)
    freq = np.zeros((1, LANES), np.float32)
    freq[0, NOPE:NOPE + ROPE // 2] = inv
    freq[0, NOPE + ROPE // 2:NOPE + ROPE] = inv
    tm = _row_tile(t)

    def body(p_ref, f_ref, c_ref, s_ref):
        ang = p_ref[...].astype(F32) * f_ref[...]
        c_ref[...] = jnp.cos(ang)
        s_ref[...] = jnp.sin(ang)

    shp = jax.ShapeDtypeStruct((t, LANES), F32)
    return pl.pallas_call(
        body, name="rope_tables", grid=(t // tm,), out_shape=(shp, shp),
        in_specs=[pl.BlockSpec((tm, 1), lambda i: (i, 0)), pl.BlockSpec((1, LANES), lambda i: (0, 0))],
        out_specs=(pl.BlockSpec((tm, LANES), lambda i: (i, 0)),) * 2,
        compiler_params=_cparams("parallel"),
    )(pos_col, jnp.asarray(freq))


def _in_proj(x, g, w_qkv, w_rest):
    t = x.shape[0]
    tm = _row_tile(t)

    def body(x_ref, g_ref, wq_ref, wr_ref, h_ref, fq_ref, fk_ref, fv_ref, r_ref):
        h, _ = _rms(x_ref[...], g_ref[...])
        hb = h.astype(BF16)
        h_ref[...] = hb
        for n, ref in enumerate((fq_ref, fk_ref, fv_ref)):
            ref[...] = _dot_nt(hb, wq_ref[n * FOX_WIDTH:(n + 1) * FOX_WIDTH, :]).astype(BF16)
        r_ref[...] = _dot_nt(hb, wr_ref[...])

    row = lambda n: pl.BlockSpec((tm, n), lambda i: (i, 0))
    full = lambda a: pl.BlockSpec(a.shape, lambda i: (0,) * a.ndim)
    return pl.pallas_call(
        body, name="in_proj", grid=(t // tm,),
        out_shape=(jax.ShapeDtypeStruct((t, D_MODEL), BF16),) + (jax.ShapeDtypeStruct((t, FOX_WIDTH), BF16),) * 3
        + (jax.ShapeDtypeStruct((t, REST_COLS), F32),),
        in_specs=[row(D_MODEL), full(g), full(w_qkv), full(w_rest)],
        out_specs=(row(D_MODEL), row(FOX_WIDTH), row(FOX_WIDTH), row(FOX_WIDTH), row(REST_COLS)),
        compiler_params=_cparams("parallel"),
    )(x, g, w_qkv, w_rest)


def _log_sigmoid(z):
    return jnp.minimum(z, 0.0) - jnp.log(1.0 + jnp.exp(-jnp.abs(z)))


def _split3(v):
    hi = v.astype(BF16)
    r1 = v - hi.astype(F32)
    mid = r1.astype(BF16)
    lo = (r1 - mid.astype(F32)).astype(BF16)
    return hi, mid, lo


def _scan_tile(t):
    return 256 if t >= 256 else t


def _forget_cumsum(rest, b128):
    t = rest.shape[0]
    tb = _scan_tile(t)

    def body(r_ref, b_ref, row_ref, rep_ref, f_sc, carry):
        @pl.when(pl.program_id(0) == 0)
        def _():
            carry[...] = jnp.zeros_like(carry)
        lf = _log_sigmoid(r_ref[...] + b_ref[...])
        tri = (lax.broadcasted_iota(jnp.int32, (tb, tb), 0) >= lax.broadcasted_iota(jnp.int32, (tb, tb), 1)).astype(BF16)
        hi, mid, lo = _split3(lf)
        f_sc[...] = (_dot(tri, hi) + _dot(tri, mid)) + _dot(tri, lo) + carry[...]
        carry[...] = f_sc[tb - 1:tb, :]
        f2 = f_sc[...] * LOG2E
        row_ref[...] = jnp.transpose(f2)[0:HEADS, :]
        lane = _lane()
        for h in range(HEADS):
            col = jnp.sum(jnp.where(lane == h, f2, 0.0), axis=1, keepdims=True)
            rep_ref[h] = jnp.broadcast_to(col, (tb, LANES))

    return pl.pallas_call(
        body, name="forget_cumsum", grid=(t // tb,),
        out_shape=(jax.ShapeDtypeStruct((HEADS, t), F32), jax.ShapeDtypeStruct((HEADS, t, LANES), F32)),
        in_specs=[pl.BlockSpec((tb, LANES), lambda i: (i, 0)), pl.BlockSpec((1, LANES), lambda i: (0, 0))],
        out_specs=(pl.BlockSpec((HEADS, tb), lambda i: (0, i)), pl.BlockSpec((HEADS, tb, LANES), lambda i: (0, i, 0))),
        scratch_shapes=[pltpu.VMEM((tb, LANES), F32), pltpu.VMEM((1, LANES), F32)],
        compiler_params=_cparams("arbitrary"),
    )(rest, b128)


def _forget_bwd(rest, b128, d_fq, d_fk):
    t = rest.shape[0]
    tb = _scan_tile(t)
    nb = t // tb

    def body(r_ref, b_ref, dfq_ref, dfk_ref, dz_ref, db_ref, carry):
        @pl.when(pl.program_id(0) == 0)
        def _():
            carry[...] = jnp.zeros_like(carry)
            db_ref[...] = jnp.zeros_like(db_ref)
        tri = (lax.broadcasted_iota(jnp.int32, (tb, tb), 0) <= lax.broadcasted_iota(jnp.int32, (tb, tb), 1)).astype(BF16)
        lane = _lane()
        df = jnp.zeros((tb, LANES), F32)
        for h in range(HEADS):
            df = df + jnp.where(lane == h, dfq_ref[h] + dfk_ref[h], 0.0)
        hi, mid, lo = _split3(df)
        dlf = (_dot(tri, hi) + _dot(tri, mid)) + _dot(tri, lo) + carry[...]
        z = r_ref[...] + b_ref[...]
        dz = dlf / (1.0 + jnp.exp(z))
        dz_ref[...] = dz
        db_ref[...] += jnp.sum(dz, axis=0, keepdims=True)
        carry[...] = carry[...] + jnp.sum(df, axis=0, keepdims=True)

    rev = lambda i: (nb - 1 - i, 0)
    rev3 = pl.BlockSpec((HEADS, tb, LANES), lambda i: (0, nb - 1 - i, 0))
    return pl.pallas_call(
        body, name="forget_bwd", grid=(nb,),
        out_shape=(jax.ShapeDtypeStruct((t, LANES), F32), jax.ShapeDtypeStruct((1, LANES), F32)),
        in_specs=[pl.BlockSpec((tb, LANES), rev), pl.BlockSpec((1, LANES), lambda i: (0, 0)), rev3, rev3],
        out_specs=(pl.BlockSpec((tb, LANES), rev), pl.BlockSpec((1, LANES), lambda i: (0, 0))),
        scratch_shapes=[pltpu.VMEM((1, LANES), F32)],
        compiler_params=_cparams("arbitrary"),
    )(rest, b128, d_fq, d_fk)


def _mla_prep(rest, gq, gkv, wq, wkv, cos, sin):
    t = rest.shape[0]
    tm = _row_tile(t)

    def body(r_ref, gq_ref, gkv_ref, wq_ref, wkv_ref, c_ref, s_ref, q_ref, k_ref, kv_ref, cq_ref, ckv_ref):
        cos_, sin_ = c_ref[...], s_ref[...]
        cq, _ = _rms(r_ref[:, REST_CQ:REST_CKV], gq_ref[...])
        ckv, _ = _rms(r_ref[:, REST_CKV:REST_KR], gkv_ref[...])
        cqb, ckvb = cq.astype(BF16), ckv.astype(BF16)
        cq_ref[...] = cqb
        ckv_ref[...] = ckvb
        k_rope = _rope(r_ref[:, REST_KR:REST_COLS], cos_, sin_)
        lo = _lane() < NOPE
        for h in range(HEADS):
            q_ref[h] = _rope(_dot(cqb, wq_ref[h]), cos_, sin_).astype(BF16)
            kv = _dot(ckvb, wkv_ref[h])
            kv_ref[h] = kv.astype(BF16)
            k_ref[h] = (jnp.where(lo, kv, 0.0) + k_rope).astype(BF16)

    row = lambda n: pl.BlockSpec((tm, n), lambda i: (i, 0))
    full = lambda a: pl.BlockSpec(a.shape, lambda i: (0,) * a.ndim)
    heads = pl.BlockSpec((HEADS, tm, LANES), lambda i: (0, i, 0))
    hshape = jax.ShapeDtypeStruct((HEADS, t, LANES), BF16)
    return pl.pallas_call(
        body, name="mla_prep", grid=(t // tm,),
        out_shape=(hshape, hshape, hshape, jax.ShapeDtypeStruct((t, Q_RANK), BF16), jax.ShapeDtypeStruct((t, KV_RANK), BF16)),
        in_specs=[row(REST_COLS), full(gq), full(gkv), full(wq), full(wkv), row(LANES), row(LANES)],
        out_specs=(heads, heads, heads, row(Q_RANK), row(KV_RANK)),
        compiler_params=_cparams("parallel"),
    )(rest, gq, gkv, wq, wkv, cos, sin)


def _pair_specs(fox, t, tq, blocked_q):
    if fox:
        blk = pl.BlockSpec((tq, LANES), lambda p, i: (i, p))
        whole = pl.BlockSpec((t, LANES), lambda p, i: (0, p))
    else:
        blk = pl.BlockSpec((2, tq, LANES), lambda p, i: (p, i, 0))
        whole = pl.BlockSpec((2, t, LANES), lambda p, i: (p, 0, 0))
    return [blk, whole, whole] if blocked_q else [whole, blk, blk]


def _tile_lanes(x, n):
    return jnp.tile(x, (1, n)) if n > 1 else x


class _Comm(NamedTuple):
    inputs: tuple
    out_shape: tuple
    aliases: dict
    scratch: tuple
    start: Callable
    finish: Callable


def _hosted_call(name, main, grid, args, in_specs, out_shape, out_specs, scratch, comm):
    n_in, n_out, n_scr = len(args), len(out_shape), len(scratch)
    c_in = list(comm.inputs) if comm else []
    c_out = list(comm.out_shape) if comm else []

    def body(*refs):
        bounds = [0, n_in, len(c_in), n_out, len(c_out), n_scr]
        starts = [sum(bounds[:k + 1]) for k in range(len(bounds))]
        ins, cins, outs, couts, scr = [refs[a:b] for a, b in zip(starts[:-1], starts[1:])]
        sems = refs[starts[-1]:]
        if comm:
            @pl.when(jnp.logical_and(pl.program_id(0) == 0, pl.program_id(1) == 0))
            def _():
                comm.start(cins, couts, sems)
        main(ins, outs, scr)
        if comm:
            @pl.when(jnp.logical_and(pl.program_id(0) == grid[0] - 1, pl.program_id(1) == grid[1] - 1))
            def _():
                comm.finish(cins, couts, sems)

    res = pl.pallas_call(
        body, name=name, grid=grid,
        out_shape=list(out_shape) + c_out,
        in_specs=list(in_specs) + _hbm_specs(len(c_in)),
        out_specs=list(out_specs) + _hbm_specs(len(c_out)),
        scratch_shapes=list(scratch) + (list(comm.scratch) if comm else []),
        input_output_aliases={n_in + i: n_out + o for i, o in comm.aliases.items()} if comm else {},
        compiler_params=_cparams("arbitrary", "arbitrary"),
    )(*args, *c_in)
    return res[:n_out], res[n_out:]


def _stat_rows(x):
    return jnp.transpose(x)[0:8, :]


FWD_HEADS = 4


def _attn_fwd(fox, q, k, v, f2_rows=None, comm=None):
    t = q.shape[0] if fox else q.shape[1]
    tq = _row_tile(t)
    nq = t // tq
    nrep = tq // LANES
    c2 = (FOX_SCALE if fox else MLA_SCALE) * LOG2E
    nh = FWD_HEADS
    wide = (nh // 2) * LANES

    def main(ins, outs, scr):
        q_ref, k_ref, v_ref = ins[:3]
        fr_ref = ins[3] if fox else None
        o_ref, lset_ref = outs
        m_sc, acc_sc = scr
        i = pl.program_id(1)
        lo = _lane() < HEAD_DIM
        hi = jnp.logical_not(lo)
        causal = lax.broadcasted_iota(jnp.int32, (tq, tq), 0) >= lax.broadcasted_iota(jnp.int32, (tq, tq), 1)
        zero, one = jnp.zeros((), BF16), jnp.ones((), BF16)
        lanes_of = lambda h: slice((h // 2) * LANES, (h // 2 + 1) * LANES)
        if fox:
            qs = [jnp.where(lo if h % 2 == 0 else hi, q_ref[:, lanes_of(h)], zero) for h in range(nh)]
            sum_lanes = [hi if h % 2 == 0 else lo for h in range(nh)]
        else:
            qs = [q_ref[h] for h in range(nh)]
            sum_lanes = [lo] * nh
        m_sc[...] = jnp.full_like(m_sc, -jnp.inf)
        acc_sc[...] = jnp.zeros_like(acc_sc)

        def rows_of(ref, j, h):
            sl = pl.ds(pl.multiple_of(j * tq, tq), tq)
            return ref[sl, lanes_of(h)] if fox else ref[h, sl, :]

        def step(j, masked):
            for h in range(nh):
                s = _dot_nt(qs[h], rows_of(k_ref, j, h)) * c2
                if fox:
                    s = s - fr_ref[h, j]
                if masked:
                    s = jnp.where(causal, s, -jnp.inf)
                m_prev = m_sc[h]
                m_new = jnp.maximum(m_prev, jnp.max(s, axis=1, keepdims=True))
                p = jnp.exp2(s - _tile_lanes(m_new, nrep))
                vj = jnp.where(sum_lanes[h], one, rows_of(v_ref, j, h))
                acc_sc[h] = jnp.exp2(m_prev - m_new) * acc_sc[h] + _dot(p.astype(BF16), vj)
                m_sc[h] = m_new

        def loop_body(j, carry):
            step(j, False)
            return carry

        lax.fori_loop(0, i, loop_body, 0)
        step(i, True)
        res = []
        for h in range(nh):
            acc = acc_sc[h]
            swapped = pltpu.roll(acc, HEAD_DIM, 1)
            res.append(acc / swapped)
            lse2 = m_sc[h] + jnp.log(jnp.where(sum_lanes[h], acc, swapped)) * LOG2E
            lset_ref[h, 0] = _stat_rows(lse2)
        for pr in range(nh // 2):
            even = res[2 * pr] if fox else pltpu.roll(res[2 * pr], HEAD_DIM, 1)
            o_ref[:, pr * LANES:(pr + 1) * LANES] = jnp.where(lo, even, res[2 * pr + 1])

    if fox:
        in_specs = [pl.BlockSpec((tq, wide), lambda g, i: (i, g))] + [pl.BlockSpec((t, wide), lambda g, i: (0, g))] * 2
        in_specs += [pl.BlockSpec((nh, nq, 1, tq), lambda g, i: (g, 0, 0, 0))]
        args = [q, k, v, f2_rows]
    else:
        in_specs = [pl.BlockSpec((nh, tq, LANES), lambda g, i: (g, i, 0))] + [pl.BlockSpec((nh, t, LANES), lambda g, i: (g, 0, 0))] * 2
        args = [q, k, v]
    return _hosted_call(
        "fox_attn_fwd" if fox else "mla_attn_fwd", main, (HEADS // nh, nq), args, in_specs,
        (jax.ShapeDtypeStruct((t, 4 * LANES), F32), jax.ShapeDtypeStruct((HEADS, nq, 8, tq), F32)),
        (pl.BlockSpec((tq, wide), lambda g, i: (i, g)), pl.BlockSpec((nh, 1, 8, tq), lambda g, i: (g, i, 0, 0))),
        [pltpu.VMEM((nh, tq, LANES), F32), pltpu.VMEM((nh, tq, LANES), F32)], comm)


def _head_do(fox, hh, do2, lo):
    if fox:
        return jnp.where(lo if hh == 0 else jnp.logical_not(lo), do2, 0.0)
    return jnp.where(lo, 0.0, pltpu.roll(do2, HEAD_DIM, 1) if hh == 0 else do2)


def _attn_bwd(fox, q, k, v, do, lse_rows, delta_rows, f2_rep=None, comm=None):
    t = q.shape[0] if fox else q.shape[1]
    tq = _row_tile(t)
    nq = t // tq
    nrep = tq // LANES
    scale = FOX_SCALE if fox else MLA_SCALE
    c2 = scale * LOG2E

    def main(ins, outs, scr):
        if fox:
            q_ref, k_ref, v_ref, f_ref, do_ref, lse_ref, dl_ref = ins
            dq_ref, dk_ref, dv_ref, dfq_ref, dfk_ref = outs
        else:
            q_ref, k_ref, v_ref, do_ref, lse_ref, dl_ref = ins
            dq_ref, dkv_ref, dkr_ref = outs
        dq_sc, dk_sc, dv_sc = scr
        j = pl.program_id(1)
        lane = _lane()
        lo = lane < HEAD_DIM
        hi = jnp.logical_not(lo)
        causal = lax.broadcasted_iota(jnp.int32, (tq, tq), 1) >= lax.broadcasted_iota(jnp.int32, (tq, tq), 0)
        zero, one = jnp.zeros((), BF16), jnp.ones((), BF16)

        @pl.when(j == 0)
        def _():
            dq_sc[...] = jnp.zeros_like(dq_sc)

        dk_sc[...] = jnp.zeros_like(dk_sc)
        dv_sc[...] = jnp.zeros_like(dv_sc)

        def step(i, masked):
            sl = pl.ds(pl.multiple_of(i * tq, tq), tq)
            do_i = do_ref[sl, :]
            for hh in range(2):
                kj = k_ref[...] if fox else k_ref[hh]
                vj = v_ref[...] if fox else v_ref[hh]
                qi = jnp.where(lo if hh == 0 else hi, q_ref[sl, :], zero) if fox else q_ref[hh, sl, :]
                dob = _head_do(fox, hh, do_i, lo).astype(BF16)
                st = _dot_nt(kj, qi) * c2
                if fox:
                    st = st - _tile_lanes(f_ref[hh], nrep)
                if masked:
                    st = jnp.where(causal, st, -jnp.inf)
                pt = jnp.exp2(st - lse_ref[hh, i, 0:1, :])
                dpt = _dot_nt(vj, dob)
                dst = (pt * (dpt - dl_ref[hh, i, 0:1, :])).astype(BF16)
                dv_sc[hh] += _dot(pt.astype(BF16), dob)
                if fox:
                    other = hi if hh == 0 else lo
                    qi = jnp.where(other, one, qi)
                    kj = jnp.where(other, one, kj)
                dk_sc[hh] += _dot(dst, qi)
                dq_sc[hh, sl, :] += _dot_tn(dst, kj)

        def loop_body(i, carry):
            step(i, False)
            return carry

        step(j, True)
        lax.fori_loop(j + 1, nq, loop_body, 0)
        if fox:
            dk_ref[...] = (jnp.where(lo, dk_sc[0], dk_sc[1]) * scale).astype(BF16)
            dv_ref[...] = (dv_sc[0] + dv_sc[1]).astype(BF16)
            for hh in range(2):
                dk = dk_sc[hh]
                dfk_ref[hh] = -jnp.where(hi if hh == 0 else lo, dk, pltpu.roll(dk, HEAD_DIM, 1))
        else:
            rope_lanes = jnp.logical_and(lane >= NOPE, lane < NOPE + ROPE)
            dkr = jnp.zeros((tq, LANES), F32)
            for hh in range(2):
                dk = dk_sc[hh] * scale
                dkv_ref[hh] = jnp.where(lo, dk, dv_sc[hh])
                dkr = dkr + jnp.where(rope_lanes, dk, 0.0)
            dkr_ref[0] = dkr

        @pl.when(j == nq - 1)
        def _():
            for i in range(nq):
                rows = slice(i * tq, (i + 1) * tq)
                if fox:
                    dq_ref[rows, :] = (jnp.where(lo, dq_sc[0, rows, :], dq_sc[1, rows, :]) * scale).astype(BF16)
                    for hh in range(2):
                        acc = dq_sc[hh, rows, :]
                        dfq_ref[hh, rows, :] = jnp.where(hi if hh == 0 else lo, acc, pltpu.roll(acc, HEAD_DIM, 1))
                else:
                    for hh in range(2):
                        dq_ref[hh, rows, :] = dq_sc[hh, rows, :] * scale

    stat = pl.BlockSpec((2, tq, LANES), lambda p, j: (p, j, 0))
    stat_all = pl.BlockSpec((2, t, LANES), lambda p, j: (p, 0, 0))
    rows4 = pl.BlockSpec((2, nq, 8, tq), lambda p, j: (p, 0, 0, 0))
    pair = pl.BlockSpec((tq, LANES), lambda p, j: (j, p))
    pair_all = pl.BlockSpec((t, LANES), lambda p, j: (0, p))
    in_specs = _pair_specs(fox, t, tq, False)
    args = [q, k, v]
    if fox:
        in_specs += [stat]
        args += [f2_rep]
    in_specs += [pair_all, rows4, rows4]
    args += [do, lse_rows, delta_rows]
    heads_f32 = jax.ShapeDtypeStruct((HEADS, t, LANES), F32)
    if fox:
        wide = jax.ShapeDtypeStruct((t, 4 * LANES), BF16)
        out_shape = (wide, wide, wide, heads_f32, heads_f32)
        out_specs = (pair_all, pair, pair, stat_all, stat)
    else:
        out_shape = (heads_f32, heads_f32, jax.ShapeDtypeStruct((HEADS // 2, t, LANES), F32))
        out_specs = (stat_all, stat, pl.BlockSpec((1, tq, LANES), lambda p, j: (p, j, 0)))
    acc = pltpu.VMEM((2, tq, LANES), F32)
    return _hosted_call("fox_attn_bwd" if fox else "mla_attn_bwd", main, (HEADS // 2, nq), args, in_specs,
                        out_shape, out_specs, [pltpu.VMEM((2, t, LANES), F32), acc, acc], comm)


def _attn_out(x, fox_o, mla_o, gf, gm, w_o):
    t = x.shape[0]
    tm = _row_tile(t)

    def body(x_ref, f_ref, m_ref, gf_ref, gm_ref, w_ref, x1_ref, mix_ref):
        nf, _ = _rms(f_ref[...], gf_ref[...])
        nm, _ = _rms(m_ref[...], gm_ref[...])
        nfb, nmb = nf.astype(BF16), nm.astype(BF16)
        mix_ref[:, :FOX_WIDTH] = nfb
        mix_ref[:, FOX_WIDTH:] = nmb
        x1_ref[...] = x_ref[...] + _dot(nfb, w_ref[:FOX_WIDTH, :]) + _dot(nmb, w_ref[FOX_WIDTH:, :])

    row = lambda n: pl.BlockSpec((tm, n), lambda i: (i, 0))
    full = lambda a: pl.BlockSpec(a.shape, lambda i: (0,) * a.ndim)
    return pl.pallas_call(
        body, name="attn_out", grid=(t // tm,),
        out_shape=(jax.ShapeDtypeStruct((t, D_MODEL), F32), jax.ShapeDtypeStruct((t, D_MODEL), BF16)),
        in_specs=[row(D_MODEL), row(FOX_WIDTH), row(MLA_WIDTH), full(gf), full(gm), full(w_o)],
        out_specs=(row(D_MODEL), row(D_MODEL)),
        compiler_params=_cparams("parallel"),
    )(x, fox_o, mla_o, gf, gm, w_o)


def _mlp_tile(t):
    return 256 if t >= 2048 else 128


def _resident(a):
    return pl.BlockSpec(a.shape, lambda i: (0,) * a.ndim, pipeline_mode=pl.Buffered(1))


FF_CHUNK = 512


def _mlp_fwd(x1, g_mlp, w_up, w_down, g_fin, target):
    t = x1.shape[0]
    tm = _mlp_tile(t)

    def body(x_ref, g_ref, wu_ref, wd_ref, gf_ref, t_ref, u_ref, h_ref, dx_ref, loss_ref, dg_ref, a_sc):
        @pl.when(pl.program_id(0) == 0)
        def _():
            loss_ref[...] = jnp.zeros_like(loss_ref)
            dg_ref[...] = jnp.zeros_like(dg_ref)

        x = x_ref[...]
        h, _ = _rms(x, g_ref[...])
        hb = h.astype(BF16)
        h_ref[...] = hb
        for f in range(D_FF // FF_CHUNK):
            sl = slice(f * FF_CHUNK, (f + 1) * FF_CHUNK)
            u = _dot(hb, wu_ref[:, sl])
            u_ref[:, sl] = u
            r = jnp.maximum(u, 0.0)
            a_sc[:, sl] = (r * r).astype(BF16)
        x2 = x + _dot(a_sc[...], wd_ref[...])
        y, r2 = _rms(x2, gf_ref[...])
        err = y - t_ref[...]
        loss_ref[...] += 0.5 * jnp.sum(jnp.mean(err * err, axis=-1, keepdims=True))
        dx, dg = _rms_bwd(x2, gf_ref[...], r2, err * (1.0 / D_MODEL))
        dx_ref[...] = dx
        dg_ref[...] += dg

    row = lambda n: pl.BlockSpec((tm, n), lambda i: (i, 0))
    vec = pl.BlockSpec((1, D_MODEL), lambda i: (0, 0))
    return pl.pallas_call(
        body, name="mlp_fwd", grid=(t // tm,),
        out_shape=(jax.ShapeDtypeStruct((t, D_FF), F32), jax.ShapeDtypeStruct((t, D_MODEL), BF16),
                   jax.ShapeDtypeStruct((t, D_MODEL), F32), jax.ShapeDtypeStruct((8, LANES), F32),
                   jax.ShapeDtypeStruct((1, D_MODEL), F32)),
        in_specs=[row(D_MODEL), vec, _resident(w_up), _resident(w_down), vec, row(D_MODEL)],
        out_specs=(row(D_FF), row(D_MODEL), row(D_MODEL), pl.BlockSpec((8, LANES), lambda i: (0, 0)), vec),
        scratch_shapes=[pltpu.VMEM((tm, D_FF), BF16)],
        compiler_params=_cparams("arbitrary"),
    )(x1, g_mlp, w_up, w_down, g_fin, target)


def _mlp_bwd(dx2, u, x1, g_mlp, w_up, w_down):
    t = x1.shape[0]
    tm = _mlp_tile(t)

    def body(dx_ref, u_ref, x_ref, g_ref, wu_ref, wd_ref, du_ref, a_ref, dx1_ref, dg_ref):
        @pl.when(pl.program_id(0) == 0)
        def _():
            dg_ref[...] = jnp.zeros_like(dg_ref)

        dx2 = dx_ref[...]
        dxb = dx2.astype(BF16)
        for f in range(D_FF // FF_CHUNK):
            sl = slice(f * FF_CHUNK, (f + 1) * FF_CHUNK)
            r = jnp.maximum(u_ref[:, sl], 0.0)
            a_ref[:, sl] = (r * r).astype(BF16)
            da = _dot_nt(dxb, wd_ref[sl, :])
            du_ref[:, sl] = (da * (2.0 * r)).astype(BF16)
        dh = _dot_nt(du_ref[...], wu_ref[...])
        x = x_ref[...]
        _, r1 = _rms(x, g_ref[...])
        dx, dg = _rms_bwd(x, g_ref[...], r1, dh)
        dx1_ref[...] = dx2 + dx
        dg_ref[...] += dg

    row = lambda n: pl.BlockSpec((tm, n), lambda i: (i, 0))
    vec = pl.BlockSpec((1, D_MODEL), lambda i: (0, 0))
    return pl.pallas_call(
        body, name="mlp_bwd", grid=(t // tm,),
        out_shape=(jax.ShapeDtypeStruct((t, D_FF), BF16), jax.ShapeDtypeStruct((t, D_FF), BF16),
                   jax.ShapeDtypeStruct((t, D_MODEL), F32), jax.ShapeDtypeStruct((1, D_MODEL), F32)),
        in_specs=[row(D_MODEL), row(D_FF), row(D_MODEL), vec, _resident(w_up), _resident(w_down)],
        out_specs=(row(D_FF), row(D_FF), row(D_MODEL), vec),
        compiler_params=_cparams("arbitrary"),
    )(dx2, u, x1, g_mlp, w_up, w_down)


def _matmul_tn(name, a, b, blocks=None):
    t, m = a.shape
    n = b.shape[1]
    tk = min(t, 2048)
    steps = t // tk
    bm = m if m <= 1024 else 512
    bn = n if n <= 1024 else 512
    width = bn if blocks is None else n // blocks
    per = bn // width

    def body(a_ref, b_ref, o_ref, acc_sc):
        kk = pl.program_id(2)

        @pl.when(kk == 0)
        def _():
            acc_sc[...] = jnp.zeros_like(acc_sc)

        acc_sc[...] += _dot_tn(a_ref[...].astype(BF16), b_ref[...].astype(BF16))

        @pl.when(kk == steps - 1)
        def _():
            if blocks is None:
                o_ref[...] = acc_sc[...]
            else:
                for s in range(per):
                    o_ref[s] = acc_sc[:, s * width:(s + 1) * width]

    if blocks is None:
        o_spec = pl.BlockSpec((bm, bn), lambda i, j, kk: (i, j))
        o_shape = (m, n)
    else:
        o_spec = pl.BlockSpec((per, bm, width), lambda i, j, kk: (j, i, 0))
        o_shape = (blocks, m, width)
    return pl.pallas_call(
        body, name=name, grid=(m // bm, n // bn, steps),
        out_shape=jax.ShapeDtypeStruct(o_shape, F32),
        in_specs=[pl.BlockSpec((tk, bm), lambda i, j, kk: (kk, i)), pl.BlockSpec((tk, bn), lambda i, j, kk: (kk, j))],
        out_specs=o_spec,
        scratch_shapes=[pltpu.VMEM((bm, bn), F32)],
        compiler_params=_cparams("parallel", "parallel", "arbitrary"),
    )(a, b)


def _attn_out_bwd(dx1, fox_o, mla_o, gf, gm, w_o):
    t = dx1.shape[0]
    tm = _row_tile(t)

    def body(dx_ref, f_ref, m_ref, gf_ref, gm_ref, w_ref, df_ref, dm_ref, dlf_ref, dlm_ref, dgf_ref, dgm_ref):
        @pl.when(pl.program_id(0) == 0)
        def _():
            dgf_ref[...] = jnp.zeros_like(dgf_ref)
            dgm_ref[...] = jnp.zeros_like(dgm_ref)
        dxb = dx_ref[...].astype(BF16)
        lane = lax.broadcasted_iota(jnp.int32, (8, LANES), 1)
        picks = [(lane < HEAD_DIM).astype(BF16), (lane >= HEAD_DIM).astype(BF16)]
        for o_ref, g_ref, lo_row, d_ref, dl_ref, dg_ref in ((f_ref, gf_ref, 0, df_ref, dlf_ref, dgf_ref),
                                                             (m_ref, gm_ref, FOX_WIDTH, dm_ref, dlm_ref, dgm_ref)):
            dn = _dot_nt(dxb, w_ref[lo_row:lo_row + FOX_WIDTH, :])
            o = o_ref[...]
            _, r = _rms(o, g_ref[...])
            d, dg = _rms_bwd(o, g_ref[...], r, dn)
            d_ref[...] = d
            dg_ref[...] += dg
            prod = d * o
            for h in range(HEADS):
                parts = _split3(prod[:, (h // 2) * LANES:(h // 2 + 1) * LANES])
                dl_ref[h, 0] = (_dot_nt(picks[h % 2], parts[0]) + _dot_nt(picks[h % 2], parts[1])) + _dot_nt(picks[h % 2], parts[2])

    row = lambda n: pl.BlockSpec((tm, n), lambda i: (i, 0))
    full = lambda a: pl.BlockSpec(a.shape, lambda i: (0,) * a.ndim)
    vec = pl.BlockSpec((1, FOX_WIDTH), lambda i: (0, 0))
    rows = pl.BlockSpec((HEADS, 1, 8, tm), lambda i: (0, i, 0, 0))
    o_shape = jax.ShapeDtypeStruct((t, FOX_WIDTH), F32)
    g_shape = jax.ShapeDtypeStruct((1, FOX_WIDTH), F32)
    r_shape = jax.ShapeDtypeStruct((HEADS, t // tm, 8, tm), F32)
    return pl.pallas_call(
        body, name="attn_out_bwd", grid=(t // tm,),
        out_shape=(o_shape, o_shape, r_shape, r_shape, g_shape, g_shape),
        in_specs=[row(D_MODEL), row(FOX_WIDTH), row(MLA_WIDTH), full(gf), full(gm), full(w_o)],
        out_specs=(row(FOX_WIDTH), row(MLA_WIDTH), rows, rows, vec, vec),
        compiler_params=_cparams("arbitrary"),
    )(dx1, fox_o, mla_o, gf, gm, w_o)


def _mla_prep_bwd(dq, dkv, dkr, dz, rest, gq, gkv, wq, wkv, cos, sin):
    t = rest.shape[0]
    tm = _row_tile(t)

    def body(dq_ref, dkv_ref, dkr_ref, dz_ref, r_ref, gq_ref, gkv_ref, wq_ref, wkv_ref, c_ref, s_ref,
             dr_ref, dqp_ref, dkvb_ref, dgq_ref, dgkv_ref):
        @pl.when(pl.program_id(0) == 0)
        def _():
            dgq_ref[...] = jnp.zeros_like(dgq_ref)
            dgkv_ref[...] = jnp.zeros_like(dgkv_ref)
        cos_, sin_ = c_ref[...], s_ref[...]
        dcq = jnp.zeros((tm, Q_RANK), F32)
        dckv = jnp.zeros((tm, KV_RANK), F32)
        for h in range(HEADS):
            dqp = _rope_bwd(dq_ref[h], cos_, sin_).astype(BF16)
            dqp_ref[:, h * LANES:(h + 1) * LANES] = dqp
            dcq = dcq + _dot_nt(dqp, wq_ref[h])
            dkvb = dkv_ref[h].astype(BF16)
            dkvb_ref[:, h * LANES:(h + 1) * LANES] = dkvb
            dckv = dckv + _dot_nt(dkvb, wkv_ref[h])
        dkrope = dkr_ref[0]
        for pr in range(1, HEADS // 2):
            dkrope = dkrope + dkr_ref[pr]
        cq = r_ref[:, REST_CQ:REST_CKV]
        _, rq = _rms(cq, gq_ref[...])
        d_cq, dgq = _rms_bwd(cq, gq_ref[...], rq, dcq)
        ckv = r_ref[:, REST_CKV:REST_KR]
        _, rkv = _rms(ckv, gkv_ref[...])
        d_ckv, dgkv = _rms_bwd(ckv, gkv_ref[...], rkv, dckv)
        dgq_ref[...] += dgq
        dgkv_ref[...] += dgkv
        dr_ref[:, 0:REST_CQ] = dz_ref[...].astype(BF16)
        dr_ref[:, REST_CQ:REST_CKV] = d_cq.astype(BF16)
        dr_ref[:, REST_CKV:REST_KR] = d_ckv.astype(BF16)
        dr_ref[:, REST_KR:REST_COLS] = _rope_bwd(dkrope, cos_, sin_).astype(BF16)

    row = lambda n: pl.BlockSpec((tm, n), lambda i: (i, 0))
    full = lambda a: pl.BlockSpec(a.shape, lambda i: (0,) * a.ndim)
    heads = pl.BlockSpec((HEADS, tm, LANES), lambda i: (0, i, 0))
    hshape = jax.ShapeDtypeStruct((t, HEADS * LANES), BF16)
    return pl.pallas_call(
        body, name="mla_prep_bwd", grid=(t // tm,),
        out_shape=(jax.ShapeDtypeStruct((t, REST_COLS), BF16), hshape, hshape,
                   jax.ShapeDtypeStruct((1, Q_RANK), F32), jax.ShapeDtypeStruct((1, KV_RANK), F32)),
        in_specs=[heads, heads, pl.BlockSpec((HEADS // 2, tm, LANES), lambda i: (0, i, 0)), row(LANES), row(REST_COLS),
                  full(gq), full(gkv), full(wq), full(wkv), row(LANES), row(LANES)],
        out_specs=(row(REST_COLS), row(HEADS * LANES), row(HEADS * LANES), pl.BlockSpec((1, Q_RANK), lambda i: (0, 0)),
                   pl.BlockSpec((1, KV_RANK), lambda i: (0, 0))),
        compiler_params=_cparams("arbitrary"),
    )(dq, dkv, dkr, dz, rest, gq, gkv, wq, wkv, cos, sin)


def _in_proj_bwd(x, g, dx1, dfq, dfk, dfv, drest, w_qkv, w_rest):
    t = x.shape[0]
    tm = _row_tile(t)

    def body(x_ref, g_ref, dx1_ref, dq_ref, dk_ref, dv_ref, dr_ref, wq_ref, wr_ref, dx_ref, dg_ref):
        @pl.when(pl.program_id(0) == 0)
        def _():
            dg_ref[...] = jnp.zeros_like(dg_ref)
        dh = _dot(dr_ref[...], wr_ref[...])
        for n, ref in enumerate((dq_ref, dk_ref, dv_ref)):
            dh = dh + _dot(ref[...], wq_ref[n * FOX_WIDTH:(n + 1) * FOX_WIDTH, :])
        xv = x_ref[...]
        _, r = _rms(xv, g_ref[...])
        dx, dg = _rms_bwd(xv, g_ref[...], r, dh)
        dx_ref[...] = dx1_ref[...] + dx
        dg_ref[...] += dg

    row = lambda n: pl.BlockSpec((tm, n), lambda i: (i, 0))
    full = lambda a: pl.BlockSpec(a.shape, lambda i: (0,) * a.ndim)
    vec = pl.BlockSpec((1, D_MODEL), lambda i: (0, 0))
    return pl.pallas_call(
        body, name="in_proj_bwd", grid=(t // tm,),
        out_shape=(jax.ShapeDtypeStruct((t, D_MODEL), F32), jax.ShapeDtypeStruct((1, D_MODEL), F32)),
        in_specs=[row(D_MODEL), full(g), row(D_MODEL), row(FOX_WIDTH), row(FOX_WIDTH), row(FOX_WIDTH), row(REST_COLS),
                  full(w_qkv), full(w_rest)],
        out_specs=(row(D_MODEL), vec),
        compiler_params=_cparams("arbitrary"),
    )(x, g, dx1, dfq, dfk, dfv, drest, w_qkv, w_rest)


def _pad_cols(a, n):
    return jnp.pad(a, ((0, 0),) * (a.ndim - 1) + ((0, n - a.shape[-1]),))


def kernel(x, positions, attn_norm_g, w_in, b_forget, q_norm_g, w_uq, kv_norm_g, w_ukv, fox_out_g, mla_out_g, w_o, mlp_norm_g, w_up, w_down, final_norm_g, loss_target, m_attn_norm_g, m_w_in, m_b_forget, m_q_norm_g, m_w_uq, m_kv_norm_g, m_w_ukv, m_fox_out_g, m_mla_out_g, m_w_o, m_mlp_norm_g, m_w_up, m_w_down, m_final_norm_g, v_attn_norm_g, v_w_in, v_b_forget, v_q_norm_g, v_w_uq, v_kv_norm_g, v_w_ukv, v_fox_out_g, v_mla_out_g, v_w_o, v_mlp_norm_g, v_w_up, v_w_down, v_final_norm_g):
    t = x.shape[1]
    tq = _row_tile(t)
    xs = x[0]
    target = loss_target[0]

    early = [jnp.transpose(w_in[0]), _pad_cols(w_uq[0], LANES), w_ukv[0]]
    late = [w_o[0].astype(BF16), w_up[0].astype(BF16), w_down[0].astype(BF16)]
    g_in, wq, wkv = _all_gather([s.astype(BF16) for s in early])
    win = g_in.reshape(IN_COLS, D_MODEL)
    off_ff, off_cq, off_kr = 3 * FOX_WIDTH, 3 * FOX_WIDTH + HEADS, IN_COLS - ROPE
    zeros = lambda n: jnp.zeros((n, D_MODEL), BF16)
    w_qkv = win[:off_ff]
    w_rest = jnp.concatenate([
        win[off_ff:off_cq], zeros(REST_CQ - HEADS), win[off_cq:off_kr],
        zeros(NOPE), win[off_kr:], zeros(LANES - NOPE - ROPE)], axis=0)

    cos, sin = _rope_tables(positions.reshape(t, 1))
    h1, fq, fk, fv, rest = _in_proj(xs, attn_norm_g, w_qkv, w_rest)
    b128 = _pad_cols(b_forget, LANES)
    f2_rows, f2_rep = _forget_cumsum(rest, b128)
    f2_rows = f2_rows.reshape(HEADS, t // tq, 1, tq)
    (fox_o, fox_lse_rows), partly = _attn_fwd(True, fq, fk, fv, f2_rows, comm=_ag_direct(late))
    mq, mk, mkv, cqn, ckvn = _mla_prep(rest, q_norm_g, kv_norm_g, wq, wkv, cos, sin)
    (mla_o, mla_lse_rows), (g_o, g_up, g_down) = _attn_fwd(False, mq, mk, mkv, comm=_ag_forward(partly))
    wo = g_o.reshape(D_MODEL, D_MODEL)
    x1, mixed = _attn_out(xs, fox_o, mla_o, fox_out_g, mla_out_g, wo)
    wup = jnp.transpose(g_up, (1, 0, 2)).reshape(D_MODEL, D_FF)
    wdown = g_down.reshape(D_FF, D_MODEL)
    u, h2, dx2, loss8, d_gfin = _mlp_fwd(x1, mlp_norm_g, wup, wdown, final_norm_g.reshape(1, D_MODEL), target)

    du, act, dx1, d_gmlp = _mlp_bwd(dx2, u, x1, mlp_norm_g, wup, wdown)
    dw_down = _matmul_tn("dw_down", act, dx2)
    dw_up = _matmul_tn("dw_up", h2, du, blocks=N_DEV)
    dfox_o, dmla_o, fox_delta_rows, mla_delta_rows, d_gfox, d_gmla = _attn_out_bwd(dx1, fox_o, mla_o, fox_out_g, mla_out_g, wo)
    dw_o = _matmul_tn("dw_o", mixed, dx1)

    core = lax.axis_index("c").astype(jnp.int32).reshape(1)
    chip = (2 * lax.axis_index("x") + lax.axis_index("y")).astype(jnp.int32).reshape(1)
    names = ("w_in", "w_uq", "w_ukv", "w_o", "w_up", "w_down")
    grads_b = [dw_o.reshape(N_DEV, -1, D_MODEL), dw_up, dw_down.reshape(N_DEV, -1, D_MODEL)]
    (dfq, dfk, dfv, d_fq, d_fk), got_b = _attn_bwd(True, fq, fk, fv, dfox_o, fox_lse_rows, fox_delta_rows,
                                                   f2_rep, comm=_rs_to_sibling(grads_b))
    sums_b = [_rs_sibling_sum("rs_sibling_sum_" + nm, g, l, core) for nm, g, l in zip(names[3:], grads_b, got_b)]
    dz, d_b = _forget_bwd(rest, b128, d_fq, d_fk)
    (dmq, dmkv, dmkr), others_b = _attn_bwd(False, mq, mk, mkv, dmla_o, mla_lse_rows, mla_delta_rows,
                                            comm=_rs_to_chips([s[1] for s in sums_b]))
    drest, dqp, dkvb, d_gq, d_gkv = _mla_prep_bwd(dmq, dmkv, dmkr, dz, rest, q_norm_g, kv_norm_g, wq, wkv, cos, sin)
    dw_uq = _matmul_tn("dw_uq", cqn, dqp, blocks=HEADS)
    dw_ukv = _matmul_tn("dw_ukv", ckvn, dkvb, blocks=HEADS)
    grad_x, d_gattn = _in_proj_bwd(xs, attn_norm_g, dx1, dfq, dfk, dfv, drest, w_qkv, w_rest)
    dw_q = _matmul_tn("dw_in_q", dfq, h1)
    dw_k = _matmul_tn("dw_in_k", dfk, h1)
    dw_v = _matmul_tn("dw_in_v", dfv, h1)
    dw_r = _matmul_tn("dw_in_rest", drest, h1)

    dw_in = jnp.concatenate([dw_q, dw_k, dw_v, dw_r[0:HEADS], dw_r[REST_CQ:REST_KR],
                             dw_r[REST_KR + NOPE:REST_KR + NOPE + ROPE]], axis=0)
    grads_a = [dw_in.reshape(N_DEV, IN_SHARD, D_MODEL), dw_uq, dw_ukv]
    got_a = _comm_call("rs_sibling_exchange", _rs_to_sibling(grads_a))
    sums_a = [_rs_sibling_sum("rs_sibling_sum_" + nm, g, l, core) for nm, g, l in zip(names[:3], grads_a, got_a)]
    others_a = _comm_call("rs_chip_exchange", _rs_to_chips([s[1] for s in sums_a]))
    sums, others = sums_a + sums_b, list(others_a) + list(others_b)
    sharded = (w_in, w_uq, w_ukv, w_o, w_up, w_down)
    moments_m = (m_w_in, m_w_uq, m_w_ukv, m_w_o, m_w_up, m_w_down)
    moments_v = (v_w_in, v_w_uq, v_w_ukv, v_w_o, v_w_up, v_w_down)
    g_in_t = _rs_final_sum("rs_final_sum_w_in", sums[0][0], others[0], chip)
    big = [_adamw_given("adamw_w_in", jnp.transpose(g_in_t), w_in, m_w_in, v_w_in)]
    for a in range(1, len(names)):
        big.append(_adamw_sharded("adamw_" + names[a], sharded[a], moments_m[a], moments_v[a], sums[a][0], others[a], chip))
    big_g, big_d, big_m, big_v = [[b[k] for b in big] for k in range(4)]

    as_row = lambda a: a.reshape(1, -1)
    small_w = (attn_norm_g, b_forget, q_norm_g, kv_norm_g, fox_out_g, mla_out_g, mlp_norm_g, final_norm_g)
    small_m = (m_attn_norm_g, m_b_forget, m_q_norm_g, m_kv_norm_g, m_fox_out_g, m_mla_out_g, m_mlp_norm_g, m_final_norm_g)
    small_v = (v_attn_norm_g, v_b_forget, v_q_norm_g, v_kv_norm_g, v_fox_out_g, v_mla_out_g, v_mlp_norm_g, v_final_norm_g)
    total = _small_all_reduce([d_gattn, d_b, d_gq, d_gkv, d_gfox, d_gmla, d_gmlp, d_gfin], loss8)
    small = _adamw_small(total, [as_row(a) for a in small_w], [as_row(a) for a in small_m], [as_row(a) for a in small_v])
    loss = small[0].reshape(())
    s_g, s_d, s_m, s_v = [[small[1 + 4 * r + k].reshape(small_w[r].shape) for r in range(len(small_w))] for k in range(4)]

    def ordered(small_, bigs):
        ga, bf, gq_, gkv_, gfo, gml, gmlp_, gfin_ = small_
        bin_, buq, bukv, bo, bup, bdown = bigs
        return [ga, bin_, bf, gq_, buq, gkv_, bukv, gfo, gml, bo, gmlp_, bup, bdown, gfin_]

    return (loss, grad_x[None], *ordered(s_g, big_g), *ordered(s_d, big_d), *ordered(s_m, big_m), *ordered(s_v, big_v))
```

```python
import math
from typing import Callable, NamedTuple

import numpy as np
import jax
import jax.numpy as jnp
from jax import lax
from jax.experimental import pallas as pl
from jax.experimental.pallas import tpu as pltpu

F32 = jnp.float32
BF16 = jnp.bfloat16
MESH = pl.DeviceIdType.MESH

D_MODEL = 1024
HEADS = 8
HEAD_DIM = 64
FOX_WIDTH = 512
MLA_WIDTH = 512
NOPE = 64
ROPE = 32
QK_DIM = 96
Q_RANK = 384
KV_RANK = 256
D_FF = 4096
IN_COLS = 2216
ROPE_THETA = 10000.0
EPS = 1e-6
FOX_SCALE = 1.0 / math.sqrt(HEAD_DIM)
MLA_SCALE = 1.0 / math.sqrt(QK_DIM)
ADAM_LR = 0.001
ADAM_B1 = 0.9
ADAM_B2 = 0.999
ADAM_EPS = 1e-08
ADAM_WD = 0.01
ADAM_STEP = 10

N_DEV = 8
LANES = 128
REST_COLS = 896
REST_CQ = LANES
REST_CKV = REST_CQ + Q_RANK
REST_KR = REST_CKV + KV_RANK
LOG2E = 1.4426950408889634
VMEM_LIMIT = 56 * 1024 * 1024

IN_SHARD = IN_COLS // N_DEV
SMALL_SIZES = (1024, 8, 384, 256, 512, 512, 1024, 1024)
SMALL_ROWS = 16
LOSS_ROW = len(SMALL_SIZES)


def _cparams(*sem):
    return pltpu.CompilerParams(dimension_semantics=sem or None, vmem_limit_bytes=VMEM_LIMIT)


def _row_tile(t):
    return 512 if t >= 2048 else 128


def _dot(a, b):
    return jnp.dot(a, b, preferred_element_type=F32)


def _dot_nt(a, b):
    return lax.dot_general(a, b, (((1,), (1,)), ((), ())), preferred_element_type=F32)


def _dot_tn(a, b):
    return lax.dot_general(a, b, (((0,), (0,)), ((), ())), preferred_element_type=F32)


def _rms(x, g):
    r = lax.rsqrt(jnp.mean(x * x, axis=-1, keepdims=True) + EPS)
    return x * r * g, r


def _rms_bwd(x, g, r, dy):
    xh = x * r
    gdy = dy * g
    dx = r * (gdy - xh * jnp.mean(gdy * xh, axis=-1, keepdims=True))
    return dx, jnp.sum(dy * xh, axis=0, keepdims=True)


def _lane():
    return lax.broadcasted_iota(jnp.int32, (1, LANES), 1)


def _rot(x):
    lane = _lane()
    half = NOPE + ROPE // 2
    first = jnp.logical_and(lane >= NOPE, lane < half)
    second = jnp.logical_and(lane >= half, lane < NOPE + ROPE)
    return jnp.where(first, -pltpu.roll(x, LANES - ROPE // 2, 1), jnp.where(second, pltpu.roll(x, ROPE // 2, 1), 0.0))


def _rope(x, cos, sin):
    return x * cos + _rot(x) * sin


def _rope_bwd(dy, cos, sin):
    return dy * cos - _rot(dy * sin)


def _remote(src, dst, send_sem, recv_sem, to):
    return pltpu.make_async_remote_copy(src_ref=src, dst_ref=dst, send_sem=send_sem, recv_sem=recv_sem,
                                        device_id=to, device_id_type=MESH)


def _hbm_specs(n):
    return [pl.BlockSpec(memory_space=pl.ANY)] * n


def _all_gather(blocks):
    n = len(blocks)

    def body(*refs):
        x_refs, out_refs = refs[:n], refs[n:2 * n]
        send_sems, recv_sems, local_sems = refs[2 * n:]
        x, y, c = lax.axis_index("x"), lax.axis_index("y"), lax.axis_index("c")
        me, sibling = (x, y, c), (x, y, 1 - c)
        chips = [(1 - x, y), (x, 1 - y), (1 - x, 1 - y)]

        def slot(a, px, py, pc):
            return out_refs[a].at[4 * px + 2 * py + pc]

        def copy(a, k, blk, to, src=None):
            return _remote(slot(a, *blk) if src is None else src, slot(a, *blk),
                           send_sems.at[7 * a + k], recv_sems.at[7 * a + k], to)

        mine = [pltpu.make_async_copy(x_refs[a], slot(a, *me), local_sems.at[a]) for a in range(n)]
        first, passed = [], []
        for a in range(n):
            mine[a].start()
            first.append(copy(a, 0, me, sibling, src=x_refs[a]))
            first += [copy(a, 1 + j, me, (*chip, c), src=x_refs[a]) for j, chip in enumerate(chips)]
        for cp in first:
            cp.start()
        for a in range(n):
            for j, chip in enumerate(chips):
                copy(a, 1 + j, (*chip, c), me).wait_recv()
                passed.append(copy(a, 4 + j, (*chip, c), sibling))
                passed[-1].start()
        for a in range(n):
            copy(a, 0, sibling, me).wait_recv()
            for j, chip in enumerate(chips):
                copy(a, 4 + j, (*chip, 1 - c), me).wait_recv()
        for cp in first + passed:
            cp.wait_send()
        for cp in mine:
            cp.wait()

    return pl.pallas_call(
        body, name="all_gather_weights",
        out_shape=[jax.ShapeDtypeStruct((N_DEV,) + b.shape, b.dtype) for b in blocks],
        in_specs=_hbm_specs(n), out_specs=_hbm_specs(n),
        scratch_shapes=[pltpu.SemaphoreType.DMA((7 * n,)), pltpu.SemaphoreType.DMA((7 * n,)), pltpu.SemaphoreType.DMA((n,))],
    )(*blocks)


def _symmetric_comm(inputs, out_shape, aliases, per_array, copies):
    def start(in_refs, out_refs, sems):
        for cp in copies(in_refs, out_refs, *sems):
            cp.start()

    def finish(in_refs, out_refs, sems):
        for cp in copies(in_refs, out_refs, *sems):
            cp.wait()

    n_sems = per_array * len(inputs)
    return _Comm(tuple(inputs), tuple(out_shape), aliases,
                 (pltpu.SemaphoreType.DMA((n_sems,)), pltpu.SemaphoreType.DMA((n_sems,))), start, finish)


def _ag_direct(shards):
    def copies(in_refs, out_refs, send_sems, recv_sems):
        x, y, c = lax.axis_index("x"), lax.axis_index("y"), lax.axis_index("c")
        peers = [(x, y, 1 - c), (1 - x, y, c), (x, 1 - y, c), (1 - x, 1 - y, c)]
        cps = []
        for a in range(len(shards)):
            mine = out_refs[a].at[4 * x + 2 * y + c]
            cps.append(pltpu.make_async_copy(in_refs[a], mine, send_sems.at[5 * a]))
            cps += [_remote(in_refs[a], mine, send_sems.at[5 * a + k], recv_sems.at[5 * a + k], peer)
                    for k, peer in enumerate(peers, start=1)]
        return cps

    return _symmetric_comm(shards, [jax.ShapeDtypeStruct((N_DEV,) + s.shape, s.dtype) for s in shards], {}, 5, copies)


def _ag_forward(gathered):
    def copies(in_refs, out_refs, send_sems, recv_sems):
        x, y, c = lax.axis_index("x"), lax.axis_index("y"), lax.axis_index("c")
        chips = [(1 - x, y), (x, 1 - y), (1 - x, 1 - y)]
        return [_remote(out_refs[a].at[4 * cx + 2 * cy + c], out_refs[a].at[4 * cx + 2 * cy + c],
                        send_sems.at[3 * a + j], recv_sems.at[3 * a + j], (x, y, 1 - c))
                for a in range(len(gathered)) for j, (cx, cy) in enumerate(chips)]

    shapes = [jax.ShapeDtypeStruct(g.shape, g.dtype) for g in gathered]
    return _symmetric_comm(gathered, shapes, {a: a for a in range(len(gathered))}, 3, copies)


def _rs_to_sibling(grads):
    def copies(in_refs, out_refs, send_sems, recv_sems):
        x, y, c = lax.axis_index("x"), lax.axis_index("y"), lax.axis_index("c")
        return [_remote(in_refs[a].at[2 * q + 1 - c], out_refs[a].at[q], send_sems.at[4 * a + q], recv_sems.at[4 * a + q], (x, y, 1 - c))
                for a in range(len(grads)) for q in range(4)]

    return _symmetric_comm(grads, [jax.ShapeDtypeStruct((4,) + g.shape[1:], g.dtype) for g in grads], {}, 4, copies)


def _rs_to_chips(parts):
    def copies(in_refs, out_refs, send_sems, recv_sems):
        x, y, c = lax.axis_index("x"), lax.axis_index("y"), lax.axis_index("c")
        chips = [(1 - x, y), (x, 1 - y), (1 - x, 1 - y)]
        return [_remote(in_refs[a].at[2 * cx + cy], out_refs[a].at[k], send_sems.at[3 * a + k], recv_sems.at[3 * a + k], (cx, cy, c))
                for a in range(len(parts)) for k, (cx, cy) in enumerate(chips)]

    return _symmetric_comm(parts, [jax.ShapeDtypeStruct((3,) + p.shape[1:], p.dtype) for p in parts], {}, 3, copies)


def _comm_call(name, comm):
    n_in, n_out = len(comm.inputs), len(comm.out_shape)

    def body(*refs):
        ins, outs, sems = refs[:n_in], refs[n_in:n_in + n_out], refs[n_in + n_out:]
        comm.start(ins, outs, sems)
        comm.finish(ins, outs, sems)

    return pl.pallas_call(
        body, name=name, out_shape=list(comm.out_shape), in_specs=_hbm_specs(n_in), out_specs=_hbm_specs(n_out),
        scratch_shapes=list(comm.scratch), input_output_aliases=dict(comm.aliases),
    )(*comm.inputs)


def _small_all_reduce(parts, loss8):
    n = len(parts)

    def body(*refs):
        p_refs, loss_ref, out_ref, pack, land, send_sems, recv_sems = refs[:n], *refs[n:]
        x, y, c = lax.axis_index("x"), lax.axis_index("y"), lax.axis_index("c")
        me = 4 * x + 2 * y + c
        pack[...] = jnp.zeros_like(pack)
        for r, ref in enumerate(p_refs):
            pack[r:r + 1, 0:ref.shape[1]] = ref[...]
        pack[LOSS_ROW:LOSS_ROW + 1, 0:LANES] = loss_ref[0:1, :]
        land[me] = pack[...]
        cps = []
        for k in range(1, N_DEV):
            peer = (x ^ (k >> 2), y ^ ((k >> 1) & 1), c ^ (k & 1))
            cps.append(_remote(pack, land.at[me], send_sems.at[k - 1], recv_sems.at[k - 1], peer))
        for cp in cps:
            cp.start()
        for cp in cps:
            cp.wait()
        acc = land[0]
        for d in range(1, N_DEV):
            acc = acc + land[d]
        out_ref[...] = acc

    vmem = pl.BlockSpec(memory_space=pltpu.VMEM)
    return pl.pallas_call(
        body, name="small_all_reduce",
        out_shape=jax.ShapeDtypeStruct((SMALL_ROWS, D_MODEL), F32),
        in_specs=[vmem] * (n + 1), out_specs=vmem,
        scratch_shapes=[pltpu.VMEM((SMALL_ROWS, D_MODEL), F32), pltpu.VMEM((N_DEV, SMALL_ROWS, D_MODEL), F32),
                        pltpu.SemaphoreType.DMA((N_DEV - 1,)), pltpu.SemaphoreType.DMA((N_DEV - 1,))],
    )(*parts, loss8)


def _rs_sibling_sum(name, grad, got, core):
    _, rows, cols = grad.shape

    def body(c_ref, g_ref, l_ref, f_ref, b_ref):
        s = g_ref[...] + l_ref[...]
        f_ref[...] = s
        b_ref[...] = s.astype(BF16)

    by_chip = pl.BlockSpec((None, rows, cols), lambda q, c_ref: (q, 0, 0))
    return pl.pallas_call(
        body, name=name,
        grid_spec=pltpu.PrefetchScalarGridSpec(
            num_scalar_prefetch=1, grid=(4,),
            in_specs=[pl.BlockSpec((None, rows, cols), lambda q, c_ref: (2 * q + c_ref[0], 0, 0)), by_chip],
            out_specs=[by_chip, by_chip]),
        out_shape=(jax.ShapeDtypeStruct((4, rows, cols), F32), jax.ShapeDtypeStruct((4, rows, cols), BF16)),
        compiler_params=_cparams("parallel"),
    )(core, grad, got)


def _adamw_math(w, g, m, v):
    m2 = ADAM_B1 * m + (1.0 - ADAM_B1) * g
    v2 = ADAM_B2 * v + (1.0 - ADAM_B2) * (g * g)
    m_hat = m2 / (1.0 - ADAM_B1 ** ADAM_STEP)
    v_hat = v2 / (1.0 - ADAM_B2 ** ADAM_STEP)
    delta = -ADAM_LR * (m_hat / (jnp.sqrt(v_hat) + ADAM_EPS) + ADAM_WD * w)
    return delta, m2, v2


def _update_tile(rows):
    return 256 if rows % 256 == 0 else rows


def _rs_final_sum(name, chip_sums, got, chip):
    _, rows, cols = chip_sums.shape

    def body(q_ref, o_ref, r_ref, g_out):
        g = o_ref[...]
        for k in range(3):
            g = g + r_ref[k].astype(F32)
        g_out[...] = g

    return pl.pallas_call(
        body, name=name,
        grid_spec=pltpu.PrefetchScalarGridSpec(
            num_scalar_prefetch=1, grid=(1,),
            in_specs=[pl.BlockSpec((None, rows, cols), lambda i, q_ref: (q_ref[0], 0, 0)),
                      pl.BlockSpec((3, rows, cols), lambda i, q_ref: (0, 0, 0))],
            out_specs=pl.BlockSpec((rows, cols), lambda i, q_ref: (0, 0))),
        out_shape=jax.ShapeDtypeStruct((rows, cols), F32),
    )(chip, chip_sums, got)


def _adamw_sharded(name, w, m, v, chip_sums, got, chip):
    _, rows, cols = w.shape
    tr = _update_tile(rows)

    def body(q_ref, o_ref, r_ref, w_ref, m_ref, v_ref, g_out, d_out, m_out, v_out):
        g = o_ref[:, 0:cols]
        for k in range(3):
            g = g + r_ref[k, :, 0:cols].astype(F32)
        d, m2, v2 = _adamw_math(w_ref[0], g, m_ref[0], v_ref[0])
        g_out[0] = g
        d_out[0] = d
        m_out[0] = m2
        v_out[0] = v2

    own = pl.BlockSpec((1, tr, cols), lambda i, q_ref: (0, i, 0))
    shp = jax.ShapeDtypeStruct(w.shape, F32)
    wide = chip_sums.shape[2]
    return pl.pallas_call(
        body, name=name,
        grid_spec=pltpu.PrefetchScalarGridSpec(
            num_scalar_prefetch=1, grid=(rows // tr,),
            in_specs=[pl.BlockSpec((None, tr, wide), lambda i, q_ref: (q_ref[0], i, 0)),
                      pl.BlockSpec((3, tr, wide), lambda i, q_ref: (0, i, 0)), own, own, own],
            out_specs=[own] * 4),
        out_shape=(shp,) * 4,
        compiler_params=_cparams("parallel"),
    )(chip, chip_sums, got, w, m, v)


def _adamw_given(name, g, w, m, v):
    _, rows, cols = w.shape
    tr = _update_tile(rows)

    def body(g_ref, w_ref, m_ref, v_ref, g_out, d_out, m_out, v_out):
        g = g_ref[...]
        d, m2, v2 = _adamw_math(w_ref[0], g, m_ref[0], v_ref[0])
        g_out[0] = g
        d_out[0] = d
        m_out[0] = m2
        v_out[0] = v2

    own = pl.BlockSpec((1, tr, cols), lambda i: (0, i, 0))
    shp = jax.ShapeDtypeStruct(w.shape, F32)
    return pl.pallas_call(
        body, name=name, grid=(rows // tr,), out_shape=(shp,) * 4,
        in_specs=[pl.BlockSpec((tr, cols), lambda i: (i, 0)), own, own, own], out_specs=[own] * 4,
        compiler_params=_cparams("parallel"),
    )(g, w, m, v)


def _adamw_small(total, ws, ms, vs):
    n = len(ws)

    def body(*refs):
        t_ref = refs[0]
        w_refs, m_refs, v_refs = refs[1:1 + n], refs[1 + n:1 + 2 * n], refs[1 + 2 * n:1 + 3 * n]
        outs = refs[1 + 3 * n:]
        outs[0][...] = t_ref[LOSS_ROW:LOSS_ROW + 1, 0:1]
        for r in range(n):
            g = t_ref[r:r + 1, 0:w_refs[r].shape[1]]
            d, m2, v2 = _adamw_math(w_refs[r][...], g, m_refs[r][...], v_refs[r][...])
            for k, val in enumerate((g, d, m2, v2)):
                outs[1 + 4 * r + k][...] = val

    vmem = pl.BlockSpec(memory_space=pltpu.VMEM)
    out_shape = [jax.ShapeDtypeStruct((1, 1), F32)]
    for w in ws:
        out_shape += [jax.ShapeDtypeStruct(w.shape, F32)] * 4
    return pl.pallas_call(
        body, name="adamw_small", out_shape=out_shape,
        in_specs=[vmem] * (1 + 3 * n), out_specs=[vmem] * len(out_shape),
    )(total, *ws, *ms, *vs)


def _rope_tables(pos_col):
    t = pos_col.shape[0]
    inv = (np.float32(ROPE_THETA) ** (-np.arange(0, ROPE, 2, dtype=np.float32) / np.float32(ROPE))).astype(np.float32)
    freq = np.zeros((1, LANES), np.float32)
    freq[0, NOPE:NOPE + ROPE // 2] = inv
    freq[0, NOPE + ROPE // 2:NOPE + ROPE] = inv
    tm = _row_tile(t)

    def body(p_ref, f_ref, c_ref, s_ref):
        ang = p_ref[...].astype(F32) * f_ref[...]
        c_ref[...] = jnp.cos(ang)
        s_ref[...] = jnp.sin(ang)

    shp = jax.ShapeDtypeStruct((t, LANES), F32)
    return pl.pallas_call(
        body, name="rope_tables", grid=(t // tm,), out_shape=(shp, shp),
        in_specs=[pl.BlockSpec((tm, 1), lambda i: (i, 0)), pl.BlockSpec((1, LANES), lambda i: (0, 0))],
        out_specs=(pl.BlockSpec((tm, LANES), lambda i: (i, 0)),) * 2,
        compiler_params=_cparams("parallel"),
    )(pos_col, jnp.asarray(freq))


def _in_proj(x, g, w_qkv, w_rest):
    t = x.shape[0]
    tm = _row_tile(t)

    def body(x_ref, g_ref, wq_ref, wr_ref, h_ref, fq_ref, fk_ref, fv_ref, r_ref):
        h, _ = _rms(x_ref[...], g_ref[...])
        hb = h.astype(BF16)
        h_ref[...] = hb
        for n, ref in enumerate((fq_ref, fk_ref, fv_ref)):
            ref[...] = _dot_nt(hb, wq_ref[n * FOX_WIDTH:(n + 1) * FOX_WIDTH, :]).astype(BF16)
        r_ref[...] = _dot_nt(hb, wr_ref[...])

    row = lambda n: pl.BlockSpec((tm, n), lambda i: (i, 0))
    full = lambda a: pl.BlockSpec(a.shape, lambda i: (0,) * a.ndim)
    return pl.pallas_call(
        body, name="in_proj", grid=(t // tm,),
        out_shape=(jax.ShapeDtypeStruct((t, D_MODEL), BF16),) + (jax.ShapeDtypeStruct((t, FOX_WIDTH), BF16),) * 3
        + (jax.ShapeDtypeStruct((t, REST_COLS), F32),),
        in_specs=[row(D_MODEL), full(g), full(w_qkv), full(w_rest)],
        out_specs=(row(D_MODEL), row(FOX_WIDTH), row(FOX_WIDTH), row(FOX_WIDTH), row(REST_COLS)),
        compiler_params=_cparams("parallel"),
    )(x, g, w_qkv, w_rest)


def _log_sigmoid(z):
    return jnp.minimum(z, 0.0) - jnp.log(1.0 + jnp.exp(-jnp.abs(z)))


def _split3(v):
    hi = v.astype(BF16)
    r1 = v - hi.astype(F32)
    mid = r1.astype(BF16)
    lo = (r1 - mid.astype(F32)).astype(BF16)
    return hi, mid, lo


def _scan_tile(t):
    return 256 if t >= 256 else t


def _forget_cumsum(rest, b128):
    t = rest.shape[0]
    tb = _scan_tile(t)

    def body(r_ref, b_ref, row_ref, rep_ref, f_sc, carry):
        @pl.when(pl.program_id(0) == 0)
        def _():
            carry[...] = jnp.zeros_like(carry)
        lf = _log_sigmoid(r_ref[...] + b_ref[...])
        tri = (lax.broadcasted_iota(jnp.int32, (tb, tb), 0) >= lax.broadcasted_iota(jnp.int32, (tb, tb), 1)).astype(BF16)
        hi, mid, lo = _split3(lf)
        f_sc[...] = (_dot(tri, hi) + _dot(tri, mid)) + _dot(tri, lo) + carry[...]
        carry[...] = f_sc[tb - 1:tb, :]
        f2 = f_sc[...] * LOG2E
        row_ref[...] = jnp.transpose(f2)[0:HEADS, :]
        lane = _lane()
        for h in range(HEADS):
            col = jnp.sum(jnp.where(lane == h, f2, 0.0), axis=1, keepdims=True)
            rep_ref[h] = jnp.broadcast_to(col, (tb, LANES))

    return pl.pallas_call(
        body, name="forget_cumsum", grid=(t // tb,),
        out_shape=(jax.ShapeDtypeStruct((HEADS, t), F32), jax.ShapeDtypeStruct((HEADS, t, LANES), F32)),
        in_specs=[pl.BlockSpec((tb, LANES), lambda i: (i, 0)), pl.BlockSpec((1, LANES), lambda i: (0, 0))],
        out_specs=(pl.BlockSpec((HEADS, tb), lambda i: (0, i)), pl.BlockSpec((HEADS, tb, LANES), lambda i: (0, i, 0))),
        scratch_shapes=[pltpu.VMEM((tb, LANES), F32), pltpu.VMEM((1, LANES), F32)],
        compiler_params=_cparams("arbitrary"),
    )(rest, b128)


def _forget_bwd(rest, b128, d_fq, d_fk):
    t = rest.shape[0]
    tb = _scan_tile(t)
    nb = t // tb

    def body(r_ref, b_ref, dfq_ref, dfk_ref, dz_ref, db_ref, carry):
        @pl.when(pl.program_id(0) == 0)
        def _():
            carry[...] = jnp.zeros_like(carry)
            db_ref[...] = jnp.zeros_like(db_ref)
        tri = (lax.broadcasted_iota(jnp.int32, (tb, tb), 0) <= lax.broadcasted_iota(jnp.int32, (tb, tb), 1)).astype(BF16)
        lane = _lane()
        df = jnp.zeros((tb, LANES), F32)
        for h in range(HEADS):
            df = df + jnp.where(lane == h, dfq_ref[h] + dfk_ref[h], 0.0)
        hi, mid, lo = _split3(df)
        dlf = (_dot(tri, hi) + _dot(tri, mid)) + _dot(tri, lo) + carry[...]
        z = r_ref[...] + b_ref[...]
        dz = dlf / (1.0 + jnp.exp(z))
        dz_ref[...] = dz
        db_ref[...] += jnp.sum(dz, axis=0, keepdims=True)
        carry[...] = carry[...] + jnp.sum(df, axis=0, keepdims=True)

    rev = lambda i: (nb - 1 - i, 0)
    rev3 = pl.BlockSpec((HEADS, tb, LANES), lambda i: (0, nb - 1 - i, 0))
    return pl.pallas_call(
        body, name="forget_bwd", grid=(nb,),
        out_shape=(jax.ShapeDtypeStruct((t, LANES), F32), jax.ShapeDtypeStruct((1, LANES), F32)),
        in_specs=[pl.BlockSpec((tb, LANES), rev), pl.BlockSpec((1, LANES), lambda i: (0, 0)), rev3, rev3],
        out_specs=(pl.BlockSpec((tb, LANES), rev), pl.BlockSpec((1, LANES), lambda i: (0, 0))),
        scratch_shapes=[pltpu.VMEM((1, LANES), F32)],
        compiler_params=_cparams("arbitrary"),
    )(rest, b128, d_fq, d_fk)


def _mla_prep(rest, gq, gkv, wq, wkv, cos, sin):
    t = rest.shape[0]
    tm = _row_tile(t)

    def body(r_ref, gq_ref, gkv_ref, wq_ref, wkv_ref, c_ref, s_ref, q_ref, k_ref, kv_ref, cq_ref, ckv_ref):
        cos_, sin_ = c_ref[...], s_ref[...]
        cq, _ = _rms(r_ref[:, REST_CQ:REST_CKV], gq_ref[...])
        ckv, _ = _rms(r_ref[:, REST_CKV:REST_KR], gkv_ref[...])
        cqb, ckvb = cq.astype(BF16), ckv.astype(BF16)
        cq_ref[...] = cqb
        ckv_ref[...] = ckvb
        k_rope = _rope(r_ref[:, REST_KR:REST_COLS], cos_, sin_)
        lo = _lane() < NOPE
        for h in range(HEADS):
            q_ref[h] = _rope(_dot(cqb, wq_ref[h]), cos_, sin_).astype(BF16)
            kv = _dot(ckvb, wkv_ref[h])
            kv_ref[h] = kv.astype(BF16)
            k_ref[h] = (jnp.where(lo, kv, 0.0) + k_rope).astype(BF16)

    row = lambda n: pl.BlockSpec((tm, n), lambda i: (i, 0))
    full = lambda a: pl.BlockSpec(a.shape, lambda i: (0,) * a.ndim)
    heads = pl.BlockSpec((HEADS, tm, LANES), lambda i: (0, i, 0))
    hshape = jax.ShapeDtypeStruct((HEADS, t, LANES), BF16)
    return pl.pallas_call(
        body, name="mla_prep", grid=(t // tm,),
        out_shape=(hshape, hshape, hshape, jax.ShapeDtypeStruct((t, Q_RANK), BF16), jax.ShapeDtypeStruct((t, KV_RANK), BF16)),
        in_specs=[row(REST_COLS), full(gq), full(gkv), full(wq), full(wkv), row(LANES), row(LANES)],
        out_specs=(heads, heads, heads, row(Q_RANK), row(KV_RANK)),
        compiler_params=_cparams("parallel"),
    )(rest, gq, gkv, wq, wkv, cos, sin)


def _pair_specs(fox, t, tq, blocked_q):
    if fox:
        blk = pl.BlockSpec((tq, LANES), lambda p, i: (i, p))
        whole = pl.BlockSpec((t, LANES), lambda p, i: (0, p))
    else:
        blk = pl.BlockSpec((2, tq, LANES), lambda p, i: (p, i, 0))
        whole = pl.BlockSpec((2, t, LANES), lambda p, i: (p, 0, 0))
    return [blk, whole, whole] if blocked_q else [whole, blk, blk]


def _tile_lanes(x, n):
    return jnp.tile(x, (1, n)) if n > 1 else x


class _Comm(NamedTuple):
    inputs: tuple
    out_shape: tuple
    aliases: dict
    scratch: tuple
    start: Callable
    finish: Callable


def _hosted_call(name, main, grid, args, in_specs, out_shape, out_specs, scratch, comm):
    n_in, n_out, n_scr = len(args), len(out_shape), len(scratch)
    c_in = list(comm.inputs) if comm else []
    c_out = list(comm.out_shape) if comm else []

    def at_step(which):
        hit = pl.program_id(0) == which[0]
        for axis in range(1, len(grid)):
            hit = jnp.logical_and(hit, pl.program_id(axis) == which[axis])
        return hit

    def body(*refs):
        bounds = [0, n_in, len(c_in), n_out, len(c_out), n_scr]
        starts = [sum(bounds[:k + 1]) for k in range(len(bounds))]
        ins, cins, outs, couts, scr = [refs[a:b] for a, b in zip(starts[:-1], starts[1:])]
        sems = refs[starts[-1]:]
        if comm:
            @pl.when(at_step([0] * len(grid)))
            def _():
                comm.start(cins, couts, sems)
        main(ins, outs, scr)
        if comm:
            @pl.when(at_step([n - 1 for n in grid]))
            def _():
                comm.finish(cins, couts, sems)

    res = pl.pallas_call(
        body, name=name, grid=grid,
        out_shape=list(out_shape) + c_out,
        in_specs=list(in_specs) + _hbm_specs(len(c_in)),
        out_specs=list(out_specs) + _hbm_specs(len(c_out)),
        scratch_shapes=list(scratch) + (list(comm.scratch) if comm else []),
        input_output_aliases={n_in + i: n_out + o for i, o in comm.aliases.items()} if comm else {},
        compiler_params=_cparams(*(["arbitrary"] * len(grid))),
    )(*args, *c_in)
    return res[:n_out], res[n_out:]


def _stat_rows(x):
    return jnp.transpose(x)[0:8, :]


FWD_HEADS = 4


def _attn_fwd(fox, q, k, v, f2_rows=None, comm=None):
    t = q.shape[0] if fox else q.shape[1]
    tq = _row_tile(t)
    nq = t // tq
    nrep = tq // LANES
    c2 = (FOX_SCALE if fox else MLA_SCALE) * LOG2E
    nh = FWD_HEADS
    wide = (nh // 2) * LANES

    def main(ins, outs, scr):
        q_ref, k_ref, v_ref = ins[:3]
        fr_ref = ins[3] if fox else None
        o_ref, lset_ref = outs
        m_sc, acc_sc = scr
        i = pl.program_id(1)
        lo = _lane() < HEAD_DIM
        hi = jnp.logical_not(lo)
        causal = lax.broadcasted_iota(jnp.int32, (tq, tq), 0) >= lax.broadcasted_iota(jnp.int32, (tq, tq), 1)
        zero, one = jnp.zeros((), BF16), jnp.ones((), BF16)
        lanes_of = lambda h: slice((h // 2) * LANES, (h // 2 + 1) * LANES)
        if fox:
            qs = [jnp.where(lo if h % 2 == 0 else hi, q_ref[:, lanes_of(h)], zero) for h in range(nh)]
            sum_lanes = [hi if h % 2 == 0 else lo for h in range(nh)]
        else:
            qs = [q_ref[h] for h in range(nh)]
            sum_lanes = [lo] * nh
        m_sc[...] = jnp.full_like(m_sc, -jnp.inf)
        acc_sc[...] = jnp.zeros_like(acc_sc)

        def rows_of(ref, j, h):
            sl = pl.ds(pl.multiple_of(j * tq, tq), tq)
            return ref[sl, lanes_of(h)] if fox else ref[h, sl, :]

        def step(j, masked):
            for h in range(nh):
                s = _dot_nt(qs[h], rows_of(k_ref, j, h)) * c2
                if fox:
                    s = s - fr_ref[h, j]
                if masked:
                    s = jnp.where(causal, s, -jnp.inf)
                m_prev = m_sc[h]
                m_new = jnp.maximum(m_prev, jnp.max(s, axis=1, keepdims=True))
                p = jnp.exp2(s - _tile_lanes(m_new, nrep))
                vj = jnp.where(sum_lanes[h], one, rows_of(v_ref, j, h))
                acc_sc[h] = jnp.exp2(m_prev - m_new) * acc_sc[h] + _dot(p.astype(BF16), vj)
                m_sc[h] = m_new

        def loop_body(j, carry):
            step(j, False)
            return carry

        lax.fori_loop(0, i, loop_body, 0)
        step(i, True)
        res = []
        for h in range(nh):
            acc = acc_sc[h]
            swapped = pltpu.roll(acc, HEAD_DIM, 1)
            res.append(acc / swapped)
            lse2 = m_sc[h] + jnp.log(jnp.where(sum_lanes[h], acc, swapped)) * LOG2E
            lset_ref[h, 0] = _stat_rows(lse2)
        for pr in range(nh // 2):
            even = res[2 * pr] if fox else pltpu.roll(res[2 * pr], HEAD_DIM, 1)
            o_ref[:, pr * LANES:(pr + 1) * LANES] = jnp.where(lo, even, res[2 * pr + 1])

    if fox:
        in_specs = [pl.BlockSpec((tq, wide), lambda g, i: (i, g))] + [pl.BlockSpec((t, wide), lambda g, i: (0, g))] * 2
        in_specs += [pl.BlockSpec((nh, nq, 1, tq), lambda g, i: (g, 0, 0, 0))]
        args = [q, k, v, f2_rows]
    else:
        in_specs = [pl.BlockSpec((nh, tq, LANES), lambda g, i: (g, i, 0))] + [pl.BlockSpec((nh, t, LANES), lambda g, i: (g, 0, 0))] * 2
        args = [q, k, v]
    return _hosted_call(
        "fox_attn_fwd" if fox else "mla_attn_fwd", main, (HEADS // nh, nq), args, in_specs,
        (jax.ShapeDtypeStruct((t, 4 * LANES), F32), jax.ShapeDtypeStruct((HEADS, nq, 8, tq), F32)),
        (pl.BlockSpec((tq, wide), lambda g, i: (i, g)), pl.BlockSpec((nh, 1, 8, tq), lambda g, i: (g, i, 0, 0))),
        [pltpu.VMEM((nh, tq, LANES), F32), pltpu.VMEM((nh, tq, LANES), F32)], comm)


def _head_do(fox, hh, do2, lo):
    if fox:
        return jnp.where(lo if hh == 0 else jnp.logical_not(lo), do2, 0.0)
    return jnp.where(lo, 0.0, pltpu.roll(do2, HEAD_DIM, 1) if hh == 0 else do2)


def _attn_bwd(fox, q, k, v, do, lse_rows, delta_rows, f2_rep=None, comm=None):
    t = q.shape[0] if fox else q.shape[1]
    tq = _row_tile(t)
    nq = t // tq
    nrep = tq // LANES
    scale = FOX_SCALE if fox else MLA_SCALE
    c2 = scale * LOG2E

    def main(ins, outs, scr):
        if fox:
            q_ref, k_ref, v_ref, f_ref, do_ref, lse_ref, dl_ref = ins
            dq_ref, dk_ref, dv_ref, dfq_ref, dfk_ref = outs
        else:
            q_ref, k_ref, v_ref, do_ref, lse_ref, dl_ref = ins
            dq_ref, dkv_ref, dkr_ref = outs
        dq_sc, dk_sc, dv_sc = scr
        j = pl.program_id(1)
        lane = _lane()
        lo = lane < HEAD_DIM
        hi = jnp.logical_not(lo)
        causal = lax.broadcasted_iota(jnp.int32, (tq, tq), 1) >= lax.broadcasted_iota(jnp.int32, (tq, tq), 0)
        zero, one = jnp.zeros((), BF16), jnp.ones((), BF16)

        @pl.when(j == 0)
        def _():
            dq_sc[...] = jnp.zeros_like(dq_sc)

        dk_sc[...] = jnp.zeros_like(dk_sc)
        dv_sc[...] = jnp.zeros_like(dv_sc)

        def step(i, masked):
            sl = pl.ds(pl.multiple_of(i * tq, tq), tq)
            do_i = do_ref[sl, :]
            for hh in range(2):
                kj = k_ref[...] if fox else k_ref[hh]
                vj = v_ref[...] if fox else v_ref[hh]
                qi = jnp.where(lo if hh == 0 else hi, q_ref[sl, :], zero) if fox else q_ref[hh, sl, :]
                dob = _head_do(fox, hh, do_i, lo).astype(BF16)
                st = _dot_nt(kj, qi) * c2
                if fox:
                    st = st - _tile_lanes(f_ref[hh], nrep)
                if masked:
                    st = jnp.where(causal, st, -jnp.inf)
                pt = jnp.exp2(st - lse_ref[hh, i, 0:1, :])
                dpt = _dot_nt(vj, dob)
                dst = (pt * (dpt - dl_ref[hh, i, 0:1, :])).astype(BF16)
                dv_sc[hh] += _dot(pt.astype(BF16), dob)
                if fox:
                    other = hi if hh == 0 else lo
                    qi = jnp.where(other, one, qi)
                    kj = jnp.where(other, one, kj)
                dk_sc[hh] += _dot(dst, qi)
                dq_sc[hh, sl, :] += _dot_tn(dst, kj)

        def loop_body(i, carry):
            step(i, False)
            return carry

        step(j, True)
        lax.fori_loop(j + 1, nq, loop_body, 0)
        if fox:
            dk_ref[...] = (jnp.where(lo, dk_sc[0], dk_sc[1]) * scale).astype(BF16)
            dv_ref[...] = (dv_sc[0] + dv_sc[1]).astype(BF16)
            for hh in range(2):
                dk = dk_sc[hh]
                dfk_ref[hh] = -jnp.where(hi if hh == 0 else lo, dk, pltpu.roll(dk, HEAD_DIM, 1))
        else:
            rope_lanes = jnp.logical_and(lane >= NOPE, lane < NOPE + ROPE)
            dkr = jnp.zeros((tq, LANES), F32)
            for hh in range(2):
                dk = dk_sc[hh] * scale
                dkv_ref[hh] = jnp.where(lo, dk, dv_sc[hh])
                dkr = dkr + jnp.where(rope_lanes, dk, 0.0)
            dkr_ref[0] = dkr

        @pl.when(j == nq - 1)
        def _():
            for i in range(nq):
                rows = slice(i * tq, (i + 1) * tq)
                if fox:
                    dq_ref[rows, :] = (jnp.where(lo, dq_sc[0, rows, :], dq_sc[1, rows, :]) * scale).astype(BF16)
                    for hh in range(2):
                        acc = dq_sc[hh, rows, :]
                        dfq_ref[hh, rows, :] = jnp.where(hi if hh == 0 else lo, acc, pltpu.roll(acc, HEAD_DIM, 1))
                else:
                    for hh in range(2):
                        dq_ref[hh, rows, :] = dq_sc[hh, rows, :] * scale

    stat = pl.BlockSpec((2, tq, LANES), lambda p, j: (p, j, 0))
    stat_all = pl.BlockSpec((2, t, LANES), lambda p, j: (p, 0, 0))
    rows4 = pl.BlockSpec((2, nq, 8, tq), lambda p, j: (p, 0, 0, 0))
    pair = pl.BlockSpec((tq, LANES), lambda p, j: (j, p))
    pair_all = pl.BlockSpec((t, LANES), lambda p, j: (0, p))
    in_specs = _pair_specs(fox, t, tq, False)
    args = [q, k, v]
    if fox:
        in_specs += [stat]
        args += [f2_rep]
    in_specs += [pair_all, rows4, rows4]
    args += [do, lse_rows, delta_rows]
    heads_f32 = jax.ShapeDtypeStruct((HEADS, t, LANES), F32)
    if fox:
        wide = jax.ShapeDtypeStruct((t, 4 * LANES), BF16)
        out_shape = (wide, wide, wide, heads_f32, heads_f32)
        out_specs = (pair_all, pair, pair, stat_all, stat)
    else:
        out_shape = (heads_f32, heads_f32, jax.ShapeDtypeStruct((HEADS // 2, t, LANES), F32))
        out_specs = (stat_all, stat, pl.BlockSpec((1, tq, LANES), lambda p, j: (p, j, 0)))
    acc = pltpu.VMEM((2, tq, LANES), F32)
    return _hosted_call("fox_attn_bwd" if fox else "mla_attn_bwd", main, (HEADS // 2, nq), args, in_specs,
                        out_shape, out_specs, [pltpu.VMEM((2, t, LANES), F32), acc, acc], comm)


def _attn_out(x, fox_o, mla_o, gf, gm, w_o):
    t = x.shape[0]
    tm = _row_tile(t)

    def body(x_ref, f_ref, m_ref, gf_ref, gm_ref, w_ref, x1_ref, mix_ref):
        nf, _ = _rms(f_ref[...], gf_ref[...])
        nm, _ = _rms(m_ref[...], gm_ref[...])
        nfb, nmb = nf.astype(BF16), nm.astype(BF16)
        mix_ref[:, :FOX_WIDTH] = nfb
        mix_ref[:, FOX_WIDTH:] = nmb
        x1_ref[...] = x_ref[...] + _dot(nfb, w_ref[:FOX_WIDTH, :]) + _dot(nmb, w_ref[FOX_WIDTH:, :])

    row = lambda n: pl.BlockSpec((tm, n), lambda i: (i, 0))
    full = lambda a: pl.BlockSpec(a.shape, lambda i: (0,) * a.ndim)
    return pl.pallas_call(
        body, name="attn_out", grid=(t // tm,),
        out_shape=(jax.ShapeDtypeStruct((t, D_MODEL), F32), jax.ShapeDtypeStruct((t, D_MODEL), BF16)),
        in_specs=[row(D_MODEL), row(FOX_WIDTH), row(MLA_WIDTH), full(gf), full(gm), full(w_o)],
        out_specs=(row(D_MODEL), row(D_MODEL)),
        compiler_params=_cparams("parallel"),
    )(x, fox_o, mla_o, gf, gm, w_o)


def _mlp_tile(t):
    return 256 if t >= 2048 else 128


def _resident(a):
    return pl.BlockSpec(a.shape, lambda i: (0,) * a.ndim, pipeline_mode=pl.Buffered(1))


FF_CHUNK = 512


def _mlp_fwd(x1, g_mlp, w_up, w_down, g_fin, target):
    t = x1.shape[0]
    tm = _mlp_tile(t)

    def body(x_ref, g_ref, wu_ref, wd_ref, gf_ref, t_ref, u_ref, h_ref, dx_ref, dxb_ref, loss_ref, dg_ref, a_sc):
        @pl.when(pl.program_id(0) == 0)
        def _():
            loss_ref[...] = jnp.zeros_like(loss_ref)
            dg_ref[...] = jnp.zeros_like(dg_ref)

        x = x_ref[...]
        h, _ = _rms(x, g_ref[...])
        hb = h.astype(BF16)
        h_ref[...] = hb
        for f in range(D_FF // FF_CHUNK):
            sl = slice(f * FF_CHUNK, (f + 1) * FF_CHUNK)
            u = _dot(hb, wu_ref[:, sl])
            u_ref[:, sl] = u
            r = jnp.maximum(u, 0.0)
            a_sc[:, sl] = (r * r).astype(BF16)
        x2 = x + _dot(a_sc[...], wd_ref[...])
        y, r2 = _rms(x2, gf_ref[...])
        err = y - t_ref[...]
        loss_ref[...] += 0.5 * jnp.sum(jnp.mean(err * err, axis=-1, keepdims=True))
        dx, dg = _rms_bwd(x2, gf_ref[...], r2, err * (1.0 / D_MODEL))
        dx_ref[...] = dx
        dxb_ref[...] = dx.astype(BF16)
        dg_ref[...] += dg

    row = lambda n: pl.BlockSpec((tm, n), lambda i: (i, 0))
    vec = pl.BlockSpec((1, D_MODEL), lambda i: (0, 0))
    return pl.pallas_call(
        body, name="mlp_fwd", grid=(t // tm,),
        out_shape=(jax.ShapeDtypeStruct((t, D_FF), F32), jax.ShapeDtypeStruct((t, D_MODEL), BF16),
                   jax.ShapeDtypeStruct((t, D_MODEL), F32), jax.ShapeDtypeStruct((t, D_MODEL), BF16),
                   jax.ShapeDtypeStruct((8, LANES), F32), jax.ShapeDtypeStruct((1, D_MODEL), F32)),
        in_specs=[row(D_MODEL), vec, _resident(w_up), _resident(w_down), vec, row(D_MODEL)],
        out_specs=(row(D_FF), row(D_MODEL), row(D_MODEL), row(D_MODEL), pl.BlockSpec((8, LANES), lambda i: (0, 0)), vec),
        scratch_shapes=[pltpu.VMEM((tm, D_FF), BF16)],
        compiler_params=_cparams("arbitrary"),
    )(x1, g_mlp, w_up, w_down, g_fin, target)


def _mlp_bwd(dx2, u, x1, g_mlp, w_up, w_down):
    t = x1.shape[0]
    tm = _mlp_tile(t)

    def body(dx_ref, u_ref, x_ref, g_ref, wu_ref, wd_ref, du_ref, a_ref, dx1_ref, dx1b_ref, dg_ref):
        @pl.when(pl.program_id(0) == 0)
        def _():
            dg_ref[...] = jnp.zeros_like(dg_ref)

        dx2 = dx_ref[...]
        dxb = dx2.astype(BF16)
        for f in range(D_FF // FF_CHUNK):
            sl = slice(f * FF_CHUNK, (f + 1) * FF_CHUNK)
            r = jnp.maximum(u_ref[:, sl], 0.0)
            a_ref[:, sl] = (r * r).astype(BF16)
            da = _dot_nt(dxb, wd_ref[sl, :])
            du_ref[:, sl] = (da * (2.0 * r)).astype(BF16)
        dh = _dot_nt(du_ref[...], wu_ref[...])
        x = x_ref[...]
        _, r1 = _rms(x, g_ref[...])
        dx, dg = _rms_bwd(x, g_ref[...], r1, dh)
        dx1 = dx2 + dx
        dx1_ref[...] = dx1
        dx1b_ref[...] = dx1.astype(BF16)
        dg_ref[...] += dg

    row = lambda n: pl.BlockSpec((tm, n), lambda i: (i, 0))
    vec = pl.BlockSpec((1, D_MODEL), lambda i: (0, 0))
    return pl.pallas_call(
        body, name="mlp_bwd", grid=(t // tm,),
        out_shape=(jax.ShapeDtypeStruct((t, D_FF), BF16), jax.ShapeDtypeStruct((t, D_FF), BF16),
                   jax.ShapeDtypeStruct((t, D_MODEL), F32), jax.ShapeDtypeStruct((t, D_MODEL), BF16),
                   jax.ShapeDtypeStruct((1, D_MODEL), F32)),
        in_specs=[row(D_MODEL), row(D_FF), row(D_MODEL), vec, _resident(w_up), _resident(w_down)],
        out_specs=(row(D_FF), row(D_FF), row(D_MODEL), row(D_MODEL), vec),
        compiler_params=_cparams("arbitrary"),
    )(dx2, u, x1, g_mlp, w_up, w_down)


def _matmul_tn(name, a, b, blocks=None):
    t, m = a.shape
    n = b.shape[1]
    tk = t if a.dtype == BF16 and b.dtype == BF16 else min(t, 2048)
    steps = t // tk
    bm = m if m <= 1024 else 512
    bn = n if n <= 1024 else 512
    width = bn if blocks is None else n // blocks
    per = bn // width

    def body(a_ref, b_ref, o_ref, acc_sc):
        kk = pl.program_id(2)

        @pl.when(kk == 0)
        def _():
            acc_sc[...] = jnp.zeros_like(acc_sc)

        acc_sc[...] += _dot_tn(a_ref[...].astype(BF16), b_ref[...].astype(BF16))

        @pl.when(kk == steps - 1)
        def _():
            if blocks is None:
                o_ref[...] = acc_sc[...]
            else:
                for s in range(per):
                    o_ref[s] = acc_sc[:, s * width:(s + 1) * width]

    if blocks is None:
        o_spec = pl.BlockSpec((bm, bn), lambda i, j, kk: (i, j))
        o_shape = (m, n)
    else:
        o_spec = pl.BlockSpec((per, bm, width), lambda i, j, kk: (j, i, 0))
        o_shape = (blocks, m, width)
    return pl.pallas_call(
        body, name=name, grid=(m // bm, n // bn, steps),
        out_shape=jax.ShapeDtypeStruct(o_shape, F32),
        in_specs=[pl.BlockSpec((tk, bm), lambda i, j, kk: (kk, i)), pl.BlockSpec((tk, bn), lambda i, j, kk: (kk, j))],
        out_specs=o_spec,
        scratch_shapes=[pltpu.VMEM((bm, bn), F32)],
        compiler_params=_cparams("parallel", "parallel", "arbitrary"),
    )(a, b)


def _attn_out_bwd(dx1, fox_o, mla_o, gf, gm, w_o):
    t = dx1.shape[0]
    tm = _row_tile(t)

    def body(dx_ref, f_ref, m_ref, gf_ref, gm_ref, w_ref, df_ref, dm_ref, dlf_ref, dlm_ref, dgf_ref, dgm_ref):
        @pl.when(pl.program_id(0) == 0)
        def _():
            dgf_ref[...] = jnp.zeros_like(dgf_ref)
            dgm_ref[...] = jnp.zeros_like(dgm_ref)
        dxb = dx_ref[...].astype(BF16)
        lane = lax.broadcasted_iota(jnp.int32, (8, LANES), 1)
        picks = [(lane < HEAD_DIM).astype(BF16), (lane >= HEAD_DIM).astype(BF16)]
        for o_ref, g_ref, lo_row, d_ref, dl_ref, dg_ref in ((f_ref, gf_ref, 0, df_ref, dlf_ref, dgf_ref),
                                                             (m_ref, gm_ref, FOX_WIDTH, dm_ref, dlm_ref, dgm_ref)):
            dn = _dot_nt(dxb, w_ref[lo_row:lo_row + FOX_WIDTH, :])
            o = o_ref[...]
            _, r = _rms(o, g_ref[...])
            d, dg = _rms_bwd(o, g_ref[...], r, dn)
            d_ref[...] = d
            dg_ref[...] += dg
            prod = d * o
            for h in range(HEADS):
                parts = _split3(prod[:, (h // 2) * LANES:(h // 2 + 1) * LANES])
                dl_ref[h, 0] = (_dot_nt(picks[h % 2], parts[0]) + _dot_nt(picks[h % 2], parts[1])) + _dot_nt(picks[h % 2], parts[2])

    row = lambda n: pl.BlockSpec((tm, n), lambda i: (i, 0))
    full = lambda a: pl.BlockSpec(a.shape, lambda i: (0,) * a.ndim)
    vec = pl.BlockSpec((1, FOX_WIDTH), lambda i: (0, 0))
    rows = pl.BlockSpec((HEADS, 1, 8, tm), lambda i: (0, i, 0, 0))
    o_shape = jax.ShapeDtypeStruct((t, FOX_WIDTH), F32)
    g_shape = jax.ShapeDtypeStruct((1, FOX_WIDTH), F32)
    r_shape = jax.ShapeDtypeStruct((HEADS, t // tm, 8, tm), F32)
    return pl.pallas_call(
        body, name="attn_out_bwd", grid=(t // tm,),
        out_shape=(o_shape, o_shape, r_shape, r_shape, g_shape, g_shape),
        in_specs=[row(D_MODEL), row(FOX_WIDTH), row(MLA_WIDTH), full(gf), full(gm), full(w_o)],
        out_specs=(row(FOX_WIDTH), row(MLA_WIDTH), rows, rows, vec, vec),
        compiler_params=_cparams("arbitrary"),
    )(dx1, fox_o, mla_o, gf, gm, w_o)


def _mla_prep_bwd(dq, dkv, dkr, dz, rest, gq, gkv, wq, wkv, cos, sin):
    t = rest.shape[0]
    tm = _row_tile(t)

    def body(dq_ref, dkv_ref, dkr_ref, dz_ref, r_ref, gq_ref, gkv_ref, wq_ref, wkv_ref, c_ref, s_ref,
             dr_ref, dqp_ref, dkvb_ref, dgq_ref, dgkv_ref):
        @pl.when(pl.program_id(0) == 0)
        def _():
            dgq_ref[...] = jnp.zeros_like(dgq_ref)
            dgkv_ref[...] = jnp.zeros_like(dgkv_ref)
        cos_, sin_ = c_ref[...], s_ref[...]
        dcq = jnp.zeros((tm, Q_RANK), F32)
        dckv = jnp.zeros((tm, KV_RANK), F32)
        for h in range(HEADS):
            dqp = _rope_bwd(dq_ref[h], cos_, sin_).astype(BF16)
            dqp_ref[:, h * LANES:(h + 1) * LANES] = dqp
            dcq = dcq + _dot_nt(dqp, wq_ref[h])
            dkvb = dkv_ref[h].astype(BF16)
            dkvb_ref[:, h * LANES:(h + 1) * LANES] = dkvb
            dckv = dckv + _dot_nt(dkvb, wkv_ref[h])
        dkrope = dkr_ref[0]
        for pr in range(1, HEADS // 2):
            dkrope = dkrope + dkr_ref[pr]
        cq = r_ref[:, REST_CQ:REST_CKV]
        _, rq = _rms(cq, gq_ref[...])
        d_cq, dgq = _rms_bwd(cq, gq_ref[...], rq, dcq)
        ckv = r_ref[:, REST_CKV:REST_KR]
        _, rkv = _rms(ckv, gkv_ref[...])
        d_ckv, dgkv = _rms_bwd(ckv, gkv_ref[...], rkv, dckv)
        dgq_ref[...] += dgq
        dgkv_ref[...] += dgkv
        dr_ref[:, 0:REST_CQ] = dz_ref[...].astype(BF16)
        dr_ref[:, REST_CQ:REST_CKV] = d_cq.astype(BF16)
        dr_ref[:, REST_CKV:REST_KR] = d_ckv.astype(BF16)
        dr_ref[:, REST_KR:REST_COLS] = _rope_bwd(dkrope, cos_, sin_).astype(BF16)

    row = lambda n: pl.BlockSpec((tm, n), lambda i: (i, 0))
    full = lambda a: pl.BlockSpec(a.shape, lambda i: (0,) * a.ndim)
    heads = pl.BlockSpec((HEADS, tm, LANES), lambda i: (0, i, 0))
    hshape = jax.ShapeDtypeStruct((t, HEADS * LANES), BF16)
    return pl.pallas_call(
        body, name="mla_prep_bwd", grid=(t // tm,),
        out_shape=(jax.ShapeDtypeStruct((t, REST_COLS), BF16), hshape, hshape,
                   jax.ShapeDtypeStruct((1, Q_RANK), F32), jax.ShapeDtypeStruct((1, KV_RANK), F32)),
        in_specs=[heads, heads, pl.BlockSpec((HEADS // 2, tm, LANES), lambda i: (0, i, 0)), row(LANES), row(REST_COLS),
                  full(gq), full(gkv), full(wq), full(wkv), row(LANES), row(LANES)],
        out_specs=(row(REST_COLS), row(HEADS * LANES), row(HEADS * LANES), pl.BlockSpec((1, Q_RANK), lambda i: (0, 0)),
                   pl.BlockSpec((1, KV_RANK), lambda i: (0, 0))),
        compiler_params=_cparams("arbitrary"),
    )(dq, dkv, dkr, dz, rest, gq, gkv, wq, wkv, cos, sin)


def _in_proj_bwd(x, g, dx1, dfq, dfk, dfv, drest, w_qkv, w_rest, comm=None):
    t = x.shape[0]
    tm = _row_tile(t)

    def main(ins, outs, scr):
        x_ref, g_ref, dx1_ref, dq_ref, dk_ref, dv_ref, dr_ref, wq_ref, wr_ref = ins
        dx_ref, dg_ref = outs

        @pl.when(pl.program_id(0) == 0)
        def _():
            dg_ref[...] = jnp.zeros_like(dg_ref)
        dh = _dot(dr_ref[...], wr_ref[...])
        for n, ref in enumerate((dq_ref, dk_ref, dv_ref)):
            dh = dh + _dot(ref[...], wq_ref[n * FOX_WIDTH:(n + 1) * FOX_WIDTH, :])
        xv = x_ref[...]
        _, r = _rms(xv, g_ref[...])
        dx, dg = _rms_bwd(xv, g_ref[...], r, dh)
        dx_ref[...] = dx1_ref[...] + dx
        dg_ref[...] += dg

    row = lambda n: pl.BlockSpec((tm, n), lambda i: (i, 0))
    full = lambda a: pl.BlockSpec(a.shape, lambda i: (0,) * a.ndim)
    vec = pl.BlockSpec((1, D_MODEL), lambda i: (0, 0))
    return _hosted_call(
        "in_proj_bwd", main, (t // tm,), [x, g, dx1, dfq, dfk, dfv, drest, w_qkv, w_rest],
        [row(D_MODEL), full(g), row(D_MODEL), row(FOX_WIDTH), row(FOX_WIDTH), row(FOX_WIDTH), row(REST_COLS),
         full(w_qkv), full(w_rest)],
        (jax.ShapeDtypeStruct((t, D_MODEL), F32), jax.ShapeDtypeStruct((1, D_MODEL), F32)), (row(D_MODEL), vec), [], comm)


def _pad_cols(a, n):
    return jnp.pad(a, ((0, 0),) * (a.ndim - 1) + ((0, n - a.shape[-1]),))


def kernel(x, positions, attn_norm_g, w_in, b_forget, q_norm_g, w_uq, kv_norm_g, w_ukv, fox_out_g, mla_out_g, w_o, mlp_norm_g, w_up, w_down, final_norm_g, loss_target, m_attn_norm_g, m_w_in, m_b_forget, m_q_norm_g, m_w_uq, m_kv_norm_g, m_w_ukv, m_fox_out_g, m_mla_out_g, m_w_o, m_mlp_norm_g, m_w_up, m_w_down, m_final_norm_g, v_attn_norm_g, v_w_in, v_b_forget, v_q_norm_g, v_w_uq, v_kv_norm_g, v_w_ukv, v_fox_out_g, v_mla_out_g, v_w_o, v_mlp_norm_g, v_w_up, v_w_down, v_final_norm_g):
    t = x.shape[1]
    tq = _row_tile(t)
    xs = x[0]
    target = loss_target[0]

    early = [jnp.transpose(w_in[0]), _pad_cols(w_uq[0], LANES), w_ukv[0]]
    late = [w_o[0].astype(BF16), w_up[0].astype(BF16), w_down[0].astype(BF16)]
    g_in, wq, wkv = _all_gather([s.astype(BF16) for s in early])
    win = g_in.reshape(IN_COLS, D_MODEL)
    off_ff, off_cq, off_kr = 3 * FOX_WIDTH, 3 * FOX_WIDTH + HEADS, IN_COLS - ROPE
    zeros = lambda n: jnp.zeros((n, D_MODEL), BF16)
    w_qkv = win[:off_ff]
    w_rest = jnp.concatenate([
        win[off_ff:off_cq], zeros(REST_CQ - HEADS), win[off_cq:off_kr],
        zeros(NOPE), win[off_kr:], zeros(LANES - NOPE - ROPE)], axis=0)

    cos, sin = _rope_tables(positions.reshape(t, 1))
    h1, fq, fk, fv, rest = _in_proj(xs, attn_norm_g, w_qkv, w_rest)
    b128 = _pad_cols(b_forget, LANES)
    f2_rows, f2_rep = _forget_cumsum(rest, b128)
    f2_rows = f2_rows.reshape(HEADS, t // tq, 1, tq)
    (fox_o, fox_lse_rows), partly = _attn_fwd(True, fq, fk, fv, f2_rows, comm=_ag_direct(late))
    mq, mk, mkv, cqn, ckvn = _mla_prep(rest, q_norm_g, kv_norm_g, wq, wkv, cos, sin)
    (mla_o, mla_lse_rows), (g_o, g_up, g_down) = _attn_fwd(False, mq, mk, mkv, comm=_ag_forward(partly))
    wo = g_o.reshape(D_MODEL, D_MODEL)
    x1, mixed = _attn_out(xs, fox_o, mla_o, fox_out_g, mla_out_g, wo)
    wup = jnp.transpose(g_up, (1, 0, 2)).reshape(D_MODEL, D_FF)
    wdown = g_down.reshape(D_FF, D_MODEL)
    u, h2, dx2, dx2b, loss8, d_gfin = _mlp_fwd(x1, mlp_norm_g, wup, wdown, final_norm_g.reshape(1, D_MODEL), target)

    du, act, dx1, dx1b, d_gmlp = _mlp_bwd(dx2, u, x1, mlp_norm_g, wup, wdown)
    dw_down = _matmul_tn("dw_down", act, dx2b)
    dw_up = _matmul_tn("dw_up", h2, du, blocks=N_DEV)
    dfox_o, dmla_o, fox_delta_rows, mla_delta_rows, d_gfox, d_gmla = _attn_out_bwd(dx1, fox_o, mla_o, fox_out_g, mla_out_g, wo)
    dw_o = _matmul_tn("dw_o", mixed, dx1b)

    core = lax.axis_index("c").astype(jnp.int32).reshape(1)
    chip = (2 * lax.axis_index("x") + lax.axis_index("y")).astype(jnp.int32).reshape(1)
    names = ("w_in", "w_uq", "w_ukv", "w_o", "w_up", "w_down")
    grads_b = [dw_o.reshape(N_DEV, -1, D_MODEL), dw_up, dw_down.reshape(N_DEV, -1, D_MODEL)]
    (dfq, dfk, dfv, d_fq, d_fk), got_b = _attn_bwd(True, fq, fk, fv, dfox_o, fox_lse_rows, fox_delta_rows,
                                                   f2_rep, comm=_rs_to_sibling(grads_b))
    sums_b = [_rs_sibling_sum("rs_sibling_sum_" + nm, g, l, core) for nm, g, l in zip(names[3:], grads_b, got_b)]
    dz, d_b = _forget_bwd(rest, b128, d_fq, d_fk)
    (dmq, dmkv, dmkr), others_b = _attn_bwd(False, mq, mk, mkv, dmla_o, mla_lse_rows, mla_delta_rows,
                                            comm=_rs_to_chips([s[1] for s in sums_b]))
    drest, dqp, dkvb, d_gq, d_gkv = _mla_prep_bwd(dmq, dmkv, dmkr, dz, rest, q_norm_g, kv_norm_g, wq, wkv, cos, sin)
    dw_uq = _matmul_tn("dw_uq", cqn, dqp, blocks=HEADS)
    dw_ukv = _matmul_tn("dw_ukv", ckvn, dkvb, blocks=HEADS)
    dw_q = _matmul_tn("dw_in_q", dfq, h1)
    dw_k = _matmul_tn("dw_in_k", dfk, h1)
    dw_v = _matmul_tn("dw_in_v", dfv, h1)
    dw_r = _matmul_tn("dw_in_rest", drest, h1)

    dw_in = jnp.concatenate([dw_q, dw_k, dw_v, dw_r[0:HEADS], dw_r[REST_CQ:REST_KR],
                             dw_r[REST_KR + NOPE:REST_KR + NOPE + ROPE]], axis=0)
    grads_a = [dw_in.reshape(N_DEV, IN_SHARD, D_MODEL), dw_uq, dw_ukv]
    got_a = _comm_call("rs_sibling_exchange", _rs_to_sibling(grads_a))
    sums_a = [_rs_sibling_sum("rs_sibling_sum_" + nm, g, l, core) for nm, g, l in zip(names[:3], grads_a, got_a)]
    (grad_x, d_gattn), others_a = _in_proj_bwd(xs, attn_norm_g, dx1, dfq, dfk, dfv, drest, w_qkv, w_rest,
                                               comm=_rs_to_chips([s[1] for s in sums_a]))
    sums, others = sums_a + sums_b, list(others_a) + list(others_b)
    sharded = (w_in, w_uq, w_ukv, w_o, w_up, w_down)
    moments_m = (m_w_in, m_w_uq, m_w_ukv, m_w_o, m_w_up, m_w_down)
    moments_v = (v_w_in, v_w_uq, v_w_ukv, v_w_o, v_w_up, v_w_down)
    g_in_t = _rs_final_sum("rs_final_sum_w_in", sums[0][0], others[0], chip)
    big = [_adamw_given("adamw_w_in", jnp.transpose(g_in_t), w_in, m_w_in, v_w_in)]
    for a in range(1, len(names)):
        big.append(_adamw_sharded("adamw_" + names[a], sharded[a], moments_m[a], moments_v[a], sums[a][0], others[a], chip))
    big_g, big_d, big_m, big_v = [[b[k] for b in big] for k in range(4)]

    as_row = lambda a: a.reshape(1, -1)
    small_w = (attn_norm_g, b_forget, q_norm_g, kv_norm_g, fox_out_g, mla_out_g, mlp_norm_g, final_norm_g)
    small_m = (m_attn_norm_g, m_b_forget, m_q_norm_g, m_kv_norm_g, m_fox_out_g, m_mla_out_g, m_mlp_norm_g, m_final_norm_g)
    small_v = (v_attn_norm_g, v_b_forget, v_q_norm_g, v_kv_norm_g, v_fox_out_g, v_mla_out_g, v_mlp_norm_g, v_final_norm_g)
    total = _small_all_reduce([d_gattn, d_b, d_gq, d_gkv, d_gfox, d_gmla, d_gmlp, d_gfin], loss8)
    small = _adamw_small(total, [as_row(a) for a in small_w], [as_row(a) for a in small_m], [as_row(a) for a in small_v])
    loss = small[0].reshape(())
    s_g, s_d, s_m, s_v = [[small[1 + 4 * r + k].reshape(small_w[r].shape) for r in range(len(small_w))] for k in range(4)]

    def ordered(small_, bigs):
        ga, bf, gq_, gkv_, gfo, gml, gmlp_, gfin_ = small_
        bin_, buq, bukv, bo, bup, bdown = bigs
        return [ga, bin_, bf, gq_, buq, gkv_, bukv, gfo, gml, bo, gmlp_, bup, bdown, gfin_]

    return (loss, grad_x[None], *ordered(s_g, big_g), *ordered(s_d, big_d), *ordered(s_m, big_m), *ordered(s_v, big_v))
```

```python
import math
from typing import Callable, NamedTuple

import numpy as np
import jax
import jax.numpy as jnp
from jax import lax
from jax.experimental import pallas as pl
from jax.experimental.pallas import tpu as pltpu

F32 = jnp.float32
BF16 = jnp.bfloat16
MESH = pl.DeviceIdType.MESH

D_MODEL = 1024
HEADS = 8
HEAD_DIM = 64
FOX_WIDTH = 512
MLA_WIDTH = 512
NOPE = 64
ROPE = 32
QK_DIM = 96
Q_RANK = 384
KV_RANK = 256
D_FF = 4096
IN_COLS = 2216
ROPE_THETA = 10000.0
EPS = 1e-6
FOX_SCALE = 1.0 / math.sqrt(HEAD_DIM)
MLA_SCALE = 1.0 / math.sqrt(QK_DIM)
ADAM_LR = 0.001
ADAM_B1 = 0.9
ADAM_B2 = 0.999
ADAM_EPS = 1e-08
ADAM_WD = 0.01
ADAM_STEP = 10

N_DEV = 8
LANES = 128
REST_COLS = 896
REST_CQ = LANES
REST_CKV = REST_CQ + Q_RANK
REST_KR = REST_CKV + KV_RANK
LOG2E = 1.4426950408889634
VMEM_LIMIT = 56 * 1024 * 1024

IN_SHARD = IN_COLS // N_DEV
SMALL_SIZES = (1024, 8, 384, 256, 512, 512, 1024, 1024)
SMALL_ROWS = 16
LOSS_ROW = len(SMALL_SIZES)


def _cparams(*sem):
    return pltpu.CompilerParams(dimension_semantics=sem or None, vmem_limit_bytes=VMEM_LIMIT)


def _row_tile(t):
    return 512 if t >= 2048 else (256 if t >= 512 else 128)


def _dot(a, b):
    return jnp.dot(a, b, preferred_element_type=F32)


def _dot_nt(a, b):
    return lax.dot_general(a, b, (((1,), (1,)), ((), ())), preferred_element_type=F32)


def _dot_tn(a, b):
    return lax.dot_general(a, b, (((0,), (0,)), ((), ())), preferred_element_type=F32)


def _rms(x, g):
    r = lax.rsqrt(jnp.mean(x * x, axis=-1, keepdims=True) + EPS)
    return x * r * g, r


def _rms_bwd(x, g, r, dy):
    xh = x * r
    gdy = dy * g
    dx = r * (gdy - xh * jnp.mean(gdy * xh, axis=-1, keepdims=True))
    return dx, jnp.sum(dy * xh, axis=0, keepdims=True)


def _lane():
    return lax.broadcasted_iota(jnp.int32, (1, LANES), 1)


def _rot(x):
    lane = _lane()
    half = NOPE + ROPE // 2
    first = jnp.logical_and(lane >= NOPE, lane < half)
    second = jnp.logical_and(lane >= half, lane < NOPE + ROPE)
    return jnp.where(first, -pltpu.roll(x, LANES - ROPE // 2, 1), jnp.where(second, pltpu.roll(x, ROPE // 2, 1), 0.0))


def _rope(x, cos, sin):
    return x * cos + _rot(x) * sin


def _rope_bwd(dy, cos, sin):
    return dy * cos - _rot(dy * sin)


def _remote(src, dst, send_sem, recv_sem, to):
    return pltpu.make_async_remote_copy(src_ref=src, dst_ref=dst, send_sem=send_sem, recv_sem=recv_sem,
                                        device_id=to, device_id_type=MESH)


def _hbm_specs(n):
    return [pl.BlockSpec(memory_space=pl.ANY)] * n


def _all_gather(blocks):
    n = len(blocks)

    def body(*refs):
        x_refs, out_refs = refs[:n], refs[n:2 * n]
        send_sems, recv_sems, local_sems = refs[2 * n:]
        x, y, c = lax.axis_index("x"), lax.axis_index("y"), lax.axis_index("c")
        me, sibling = (x, y, c), (x, y, 1 - c)
        chips = [(1 - x, y), (x, 1 - y), (1 - x, 1 - y)]

        def slot(a, px, py, pc):
            return out_refs[a].at[4 * px + 2 * py + pc]

        def copy(a, k, blk, to, src=None):
            return _remote(slot(a, *blk) if src is None else src, slot(a, *blk),
                           send_sems.at[7 * a + k], recv_sems.at[7 * a + k], to)

        mine = [pltpu.make_async_copy(x_refs[a], slot(a, *me), local_sems.at[a]) for a in range(n)]
        first, passed = [], []
        for a in range(n):
            mine[a].start()
            first.append(copy(a, 0, me, sibling, src=x_refs[a]))
            first += [copy(a, 1 + j, me, (*chip, c), src=x_refs[a]) for j, chip in enumerate(chips)]
        for cp in first:
            cp.start()
        for a in range(n):
            for j, chip in enumerate(chips):
                copy(a, 1 + j, (*chip, c), me).wait_recv()
                passed.append(copy(a, 4 + j, (*chip, c), sibling))
                passed[-1].start()
        for a in range(n):
            copy(a, 0, sibling, me).wait_recv()
            for j, chip in enumerate(chips):
                copy(a, 4 + j, (*chip, 1 - c), me).wait_recv()
        for cp in first + passed:
            cp.wait_send()
        for cp in mine:
            cp.wait()

    return pl.pallas_call(
        body, name="all_gather_weights",
        out_shape=[jax.ShapeDtypeStruct((N_DEV,) + b.shape, b.dtype) for b in blocks],
        in_specs=_hbm_specs(n), out_specs=_hbm_specs(n),
        scratch_shapes=[pltpu.SemaphoreType.DMA((7 * n,)), pltpu.SemaphoreType.DMA((7 * n,)), pltpu.SemaphoreType.DMA((n,))],
    )(*blocks)


def _symmetric_comm(inputs, out_shape, aliases, per_array, copies):
    def start(in_refs, out_refs, sems):
        for cp in copies(in_refs, out_refs, *sems):
            cp.start()

    def finish(in_refs, out_refs, sems):
        for cp in copies(in_refs, out_refs, *sems):
            cp.wait()

    n_sems = per_array * len(inputs)
    return _Comm(tuple(inputs), tuple(out_shape), aliases,
                 (pltpu.SemaphoreType.DMA((n_sems,)), pltpu.SemaphoreType.DMA((n_sems,))), start, finish)


def _ag_direct(shards):
    def copies(in_refs, out_refs, send_sems, recv_sems):
        x, y, c = lax.axis_index("x"), lax.axis_index("y"), lax.axis_index("c")
        peers = [(x, y, 1 - c), (1 - x, y, c), (x, 1 - y, c), (1 - x, 1 - y, c)]
        cps = []
        for a in range(len(shards)):
            mine = out_refs[a].at[4 * x + 2 * y + c]
            cps.append(pltpu.make_async_copy(in_refs[a], mine, send_sems.at[5 * a]))
            cps += [_remote(in_refs[a], mine, send_sems.at[5 * a + k], recv_sems.at[5 * a + k], peer)
                    for k, peer in enumerate(peers, start=1)]
        return cps

    return _symmetric_comm(shards, [jax.ShapeDtypeStruct((N_DEV,) + s.shape, s.dtype) for s in shards], {}, 5, copies)


def _ag_forward(gathered):
    def copies(in_refs, out_refs, send_sems, recv_sems):
        x, y, c = lax.axis_index("x"), lax.axis_index("y"), lax.axis_index("c")
        chips = [(1 - x, y), (x, 1 - y), (1 - x, 1 - y)]
        return [_remote(out_refs[a].at[4 * cx + 2 * cy + c], out_refs[a].at[4 * cx + 2 * cy + c],
                        send_sems.at[3 * a + j], recv_sems.at[3 * a + j], (x, y, 1 - c))
                for a in range(len(gathered)) for j, (cx, cy) in enumerate(chips)]

    shapes = [jax.ShapeDtypeStruct(g.shape, g.dtype) for g in gathered]
    return _symmetric_comm(gathered, shapes, {a: a for a in range(len(gathered))}, 3, copies)


def _rs_to_sibling(grads):
    def copies(in_refs, out_refs, send_sems, recv_sems):
        x, y, c = lax.axis_index("x"), lax.axis_index("y"), lax.axis_index("c")
        return [_remote(in_refs[a].at[2 * q + 1 - c], out_refs[a].at[q], send_sems.at[4 * a + q], recv_sems.at[4 * a + q], (x, y, 1 - c))
                for a in range(len(grads)) for q in range(4)]

    return _symmetric_comm(grads, [jax.ShapeDtypeStruct((4,) + g.shape[1:], g.dtype) for g in grads], {}, 4, copies)


def _rs_to_chips(parts):
    def copies(in_refs, out_refs, send_sems, recv_sems):
        x, y, c = lax.axis_index("x"), lax.axis_index("y"), lax.axis_index("c")
        chips = [(1 - x, y), (x, 1 - y), (1 - x, 1 - y)]
        return [_remote(in_refs[a].at[2 * cx + cy], out_refs[a].at[k], send_sems.at[3 * a + k], recv_sems.at[3 * a + k], (cx, cy, c))
                for a in range(len(parts)) for k, (cx, cy) in enumerate(chips)]

    return _symmetric_comm(parts, [jax.ShapeDtypeStruct((3,) + p.shape[1:], p.dtype) for p in parts], {}, 3, copies)


def _comm_call(name, comm):
    n_in, n_out = len(comm.inputs), len(comm.out_shape)

    def body(*refs):
        ins, outs, sems = refs[:n_in], refs[n_in:n_in + n_out], refs[n_in + n_out:]
        comm.start(ins, outs, sems)
        comm.finish(ins, outs, sems)

    return pl.pallas_call(
        body, name=name, out_shape=list(comm.out_shape), in_specs=_hbm_specs(n_in), out_specs=_hbm_specs(n_out),
        scratch_shapes=list(comm.scratch), input_output_aliases=dict(comm.aliases),
    )(*comm.inputs)


def _small_all_reduce(parts, loss8):
    n = len(parts)

    def body(*refs):
        p_refs, loss_ref, out_ref, pack, land, send_sems, recv_sems = refs[:n], *refs[n:]
        x, y, c = lax.axis_index("x"), lax.axis_index("y"), lax.axis_index("c")
        me = 4 * x + 2 * y + c
        pack[...] = jnp.zeros_like(pack)
        for r, ref in enumerate(p_refs):
            pack[r:r + 1, 0:ref.shape[1]] = ref[...]
        pack[LOSS_ROW:LOSS_ROW + 1, 0:LANES] = loss_ref[0:1, :]
        land[me] = pack[...]
        cps = []
        for k in range(1, N_DEV):
            peer = (x ^ (k >> 2), y ^ ((k >> 1) & 1), c ^ (k & 1))
            cps.append(_remote(pack, land.at[me], send_sems.at[k - 1], recv_sems.at[k - 1], peer))
        for cp in cps:
            cp.start()
        for cp in cps:
            cp.wait()
        acc = land[0]
        for d in range(1, N_DEV):
            acc = acc + land[d]
        out_ref[...] = acc

    vmem = pl.BlockSpec(memory_space=pltpu.VMEM)
    return pl.pallas_call(
        body, name="small_all_reduce",
        out_shape=jax.ShapeDtypeStruct((SMALL_ROWS, D_MODEL), F32),
        in_specs=[vmem] * (n + 1), out_specs=vmem,
        scratch_shapes=[pltpu.VMEM((SMALL_ROWS, D_MODEL), F32), pltpu.VMEM((N_DEV, SMALL_ROWS, D_MODEL), F32),
                        pltpu.SemaphoreType.DMA((N_DEV - 1,)), pltpu.SemaphoreType.DMA((N_DEV - 1,))],
    )(*parts, loss8)


def _rs_sibling_sum(name, grad, got, core):
    _, rows, cols = grad.shape

    def body(c_ref, g_ref, l_ref, f_ref, b_ref):
        s = g_ref[...] + l_ref[...]
        f_ref[...] = s
        b_ref[...] = s.astype(BF16)

    by_chip = pl.BlockSpec((None, rows, cols), lambda q, c_ref: (q, 0, 0))
    return pl.pallas_call(
        body, name=name,
        grid_spec=pltpu.PrefetchScalarGridSpec(
            num_scalar_prefetch=1, grid=(4,),
            in_specs=[pl.BlockSpec((None, rows, cols), lambda q, c_ref: (2 * q + c_ref[0], 0, 0)), by_chip],
            out_specs=[by_chip, by_chip]),
        out_shape=(jax.ShapeDtypeStruct((4, rows, cols), F32), jax.ShapeDtypeStruct((4, rows, cols), BF16)),
        compiler_params=_cparams("parallel"),
    )(core, grad, got)


def _adamw_math(w, g, m, v):
    m2 = ADAM_B1 * m + (1.0 - ADAM_B1) * g
    v2 = ADAM_B2 * v + (1.0 - ADAM_B2) * (g * g)
    m_hat = m2 / (1.0 - ADAM_B1 ** ADAM_STEP)
    v_hat = v2 / (1.0 - ADAM_B2 ** ADAM_STEP)
    delta = -ADAM_LR * (m_hat / (jnp.sqrt(v_hat) + ADAM_EPS) + ADAM_WD * w)
    return delta, m2, v2


def _update_tile(rows):
    return 256 if rows % 256 == 0 else rows


def _rs_final_sum(name, chip_sums, got, chip):
    _, rows, cols = chip_sums.shape

    def body(q_ref, o_ref, r_ref, g_out):
        g = o_ref[...]
        for k in range(3):
            g = g + r_ref[k].astype(F32)
        g_out[...] = g

    return pl.pallas_call(
        body, name=name,
        grid_spec=pltpu.PrefetchScalarGridSpec(
            num_scalar_prefetch=1, grid=(1,),
            in_specs=[pl.BlockSpec((None, rows, cols), lambda i, q_ref: (q_ref[0], 0, 0)),
                      pl.BlockSpec((3, rows, cols), lambda i, q_ref: (0, 0, 0))],
            out_specs=pl.BlockSpec((rows, cols), lambda i, q_ref: (0, 0))),
        out_shape=jax.ShapeDtypeStruct((rows, cols), F32),
    )(chip, chip_sums, got)


def _adamw_sharded(name, w, m, v, chip_sums, got, chip):
    _, rows, cols = w.shape
    tr = _update_tile(rows)

    def body(q_ref, o_ref, r_ref, w_ref, m_ref, v_ref, g_out, d_out, m_out, v_out):
        g = o_ref[:, 0:cols]
        for k in range(3):
            g = g + r_ref[k, :, 0:cols].astype(F32)
        d, m2, v2 = _adamw_math(w_ref[0], g, m_ref[0], v_ref[0])
        g_out[0] = g
        d_out[0] = d
        m_out[0] = m2
        v_out[0] = v2

    own = pl.BlockSpec((1, tr, cols), lambda i, q_ref: (0, i, 0))
    shp = jax.ShapeDtypeStruct(w.shape, F32)
    wide = chip_sums.shape[2]
    return pl.pallas_call(
        body, name=name,
        grid_spec=pltpu.PrefetchScalarGridSpec(
            num_scalar_prefetch=1, grid=(rows // tr,),
            in_specs=[pl.BlockSpec((None, tr, wide), lambda i, q_ref: (q_ref[0], i, 0)),
                      pl.BlockSpec((3, tr, wide), lambda i, q_ref: (0, i, 0)), own, own, own],
            out_specs=[own] * 4),
        out_shape=(shp,) * 4,
        compiler_params=_cparams("parallel"),
    )(chip, chip_sums, got, w, m, v)


def _adamw_given(name, g, w, m, v):
    _, rows, cols = w.shape
    tr = _update_tile(rows)

    def body(g_ref, w_ref, m_ref, v_ref, g_out, d_out, m_out, v_out):
        g = g_ref[...]
        d, m2, v2 = _adamw_math(w_ref[0], g, m_ref[0], v_ref[0])
        g_out[0] = g
        d_out[0] = d
        m_out[0] = m2
        v_out[0] = v2

    own = pl.BlockSpec((1, tr, cols), lambda i: (0, i, 0))
    shp = jax.ShapeDtypeStruct(w.shape, F32)
    return pl.pallas_call(
        body, name=name, grid=(rows // tr,), out_shape=(shp,) * 4,
        in_specs=[pl.BlockSpec((tr, cols), lambda i: (i, 0)), own, own, own], out_specs=[own] * 4,
        compiler_params=_cparams("parallel"),
    )(g, w, m, v)


def _adamw_small(total, ws, ms, vs):
    n = len(ws)

    def body(*refs):
        t_ref = refs[0]
        w_refs, m_refs, v_refs = refs[1:1 + n], refs[1 + n:1 + 2 * n], refs[1 + 2 * n:1 + 3 * n]
        outs = refs[1 + 3 * n:]
        outs[0][...] = t_ref[LOSS_ROW:LOSS_ROW + 1, 0:1]
        for r in range(n):
            g = t_ref[r:r + 1, 0:w_refs[r].shape[1]]
            d, m2, v2 = _adamw_math(w_refs[r][...], g, m_refs[r][...], v_refs[r][...])
            for k, val in enumerate((g, d, m2, v2)):
                outs[1 + 4 * r + k][...] = val

    vmem = pl.BlockSpec(memory_space=pltpu.VMEM)
    out_shape = [jax.ShapeDtypeStruct((1, 1), F32)]
    for w in ws:
        out_shape += [jax.ShapeDtypeStruct(w.shape, F32)] * 4
    return pl.pallas_call(
        body, name="adamw_small", out_shape=out_shape,
        in_specs=[vmem] * (1 + 3 * n), out_specs=[vmem] * len(out_shape),
    )(total, *ws, *ms, *vs)


def _rope_tables(pos_col):
    t = pos_col.shape[0]
    inv = (np.float32(ROPE_THETA) ** (-np.arange(0, ROPE, 2, dtype=np.float32) / np.float32(ROPE))).astype(np.float32)
    freq = np.zeros((1, LANES), np.float32)
    freq[0, NOPE:NOPE + ROPE // 2] = inv
    freq[0, NOPE + ROPE // 2:NOPE + ROPE] = inv
    tm = _row_tile(t)

    def body(p_ref, f_ref, c_ref, s_ref):
        ang = p_ref[...].astype(F32) * f_ref[...]
        c_ref[...] = jnp.cos(ang)
        s_ref[...] = jnp.sin(ang)

    shp = jax.ShapeDtypeStruct((t, LANES), F32)
    return pl.pallas_call(
        body, name="rope_tables", grid=(t // tm,), out_shape=(shp, shp),
        in_specs=[pl.BlockSpec((tm, 1), lambda i: (i, 0)), pl.BlockSpec((1, LANES), lambda i: (0, 0))],
        out_specs=(pl.BlockSpec((tm, LANES), lambda i: (i, 0)),) * 2,
        compiler_params=_cparams("parallel"),
    )(pos_col, jnp.asarray(freq))


def _in_proj(x, g, w_qkv, w_rest):
    t = x.shape[0]
    tm = _row_tile(t)

    def body(x_ref, g_ref, wq_ref, wr_ref, h_ref, fq_ref, fk_ref, fv_ref, r_ref):
        h, _ = _rms(x_ref[...], g_ref[...])
        hb = h.astype(BF16)
        h_ref[...] = hb
        for n, ref in enumerate((fq_ref, fk_ref, fv_ref)):
            ref[...] = _dot_nt(hb, wq_ref[n * FOX_WIDTH:(n + 1) * FOX_WIDTH, :]).astype(BF16)
        r_ref[...] = _dot_nt(hb, wr_ref[...])

    row = lambda n: pl.BlockSpec((tm, n), lambda i: (i, 0))
    full = lambda a: pl.BlockSpec(a.shape, lambda i: (0,) * a.ndim)
    return pl.pallas_call(
        body, name="in_proj", grid=(t // tm,),
        out_shape=(jax.ShapeDtypeStruct((t, D_MODEL), BF16),) + (jax.ShapeDtypeStruct((t, FOX_WIDTH), BF16),) * 3
        + (jax.ShapeDtypeStruct((t, REST_COLS), F32),),
        in_specs=[row(D_MODEL), full(g), full(w_qkv), full(w_rest)],
        out_specs=(row(D_MODEL), row(FOX_WIDTH), row(FOX_WIDTH), row(FOX_WIDTH), row(REST_COLS)),
        compiler_params=_cparams("parallel"),
    )(x, g, w_qkv, w_rest)


def _log_sigmoid(z):
    return jnp.minimum(z, 0.0) - jnp.log(1.0 + jnp.exp(-jnp.abs(z)))


def _split3(v):
    hi = v.astype(BF16)
    r1 = v - hi.astype(F32)
    mid = r1.astype(BF16)
    lo = (r1 - mid.astype(F32)).astype(BF16)
    return hi, mid, lo


def _scan_tile(t):
    return 256 if t >= 256 else t


def _forget_cumsum(rest, b128):
    t = rest.shape[0]
    tb = _scan_tile(t)

    def body(r_ref, b_ref, row_ref, rep_ref, f_sc, carry):
        @pl.when(pl.program_id(0) == 0)
        def _():
            carry[...] = jnp.zeros_like(carry)
        lf = _log_sigmoid(r_ref[...] + b_ref[...])
        tri = (lax.broadcasted_iota(jnp.int32, (tb, tb), 0) >= lax.broadcasted_iota(jnp.int32, (tb, tb), 1)).astype(BF16)
        hi, mid, lo = _split3(lf)
        f_sc[...] = (_dot(tri, hi) + _dot(tri, mid)) + _dot(tri, lo) + carry[...]
        carry[...] = f_sc[tb - 1:tb, :]
        f2 = f_sc[...] * LOG2E
        row_ref[...] = jnp.transpose(f2)[0:HEADS, :]
        lane = _lane()
        for h in range(HEADS):
            col = jnp.sum(jnp.where(lane == h, f2, 0.0), axis=1, keepdims=True)
            rep_ref[h] = jnp.broadcast_to(col, (tb, LANES))

    return pl.pallas_call(
        body, name="forget_cumsum", grid=(t // tb,),
        out_shape=(jax.ShapeDtypeStruct((HEADS, t), F32), jax.ShapeDtypeStruct((HEADS, t, LANES), F32)),
        in_specs=[pl.BlockSpec((tb, LANES), lambda i: (i, 0)), pl.BlockSpec((1, LANES), lambda i: (0, 0))],
        out_specs=(pl.BlockSpec((HEADS, tb), lambda i: (0, i)), pl.BlockSpec((HEADS, tb, LANES), lambda i: (0, i, 0))),
        scratch_shapes=[pltpu.VMEM((tb, LANES), F32), pltpu.VMEM((1, LANES), F32)],
        compiler_params=_cparams("arbitrary"),
    )(rest, b128)


def _forget_bwd(rest, b128, d_fq, d_fk):
    t = rest.shape[0]
    tb = _scan_tile(t)
    nb = t // tb

    def body(r_ref, b_ref, dfq_ref, dfk_ref, dz_ref, db_ref, carry):
        @pl.when(pl.program_id(0) == 0)
        def _():
            carry[...] = jnp.zeros_like(carry)
            db_ref[...] = jnp.zeros_like(db_ref)
        tri = (lax.broadcasted_iota(jnp.int32, (tb, tb), 0) <= lax.broadcasted_iota(jnp.int32, (tb, tb), 1)).astype(BF16)
        lane = _lane()
        df = jnp.zeros((tb, LANES), F32)
        for h in range(HEADS):
            df = df + jnp.where(lane == h, dfq_ref[h] + dfk_ref[h], 0.0)
        hi, mid, lo = _split3(df)
        dlf = (_dot(tri, hi) + _dot(tri, mid)) + _dot(tri, lo) + carry[...]
        z = r_ref[...] + b_ref[...]
        dz = dlf / (1.0 + jnp.exp(z))
        dz_ref[...] = dz
        db_ref[...] += jnp.sum(dz, axis=0, keepdims=True)
        carry[...] = carry[...] + jnp.sum(df, axis=0, keepdims=True)

    rev = lambda i: (nb - 1 - i, 0)
    rev3 = pl.BlockSpec((HEADS, tb, LANES), lambda i: (0, nb - 1 - i, 0))
    return pl.pallas_call(
        body, name="forget_bwd", grid=(nb,),
        out_shape=(jax.ShapeDtypeStruct((t, LANES), F32), jax.ShapeDtypeStruct((1, LANES), F32)),
        in_specs=[pl.BlockSpec((tb, LANES), rev), pl.BlockSpec((1, LANES), lambda i: (0, 0)), rev3, rev3],
        out_specs=(pl.BlockSpec((tb, LANES), rev), pl.BlockSpec((1, LANES), lambda i: (0, 0))),
        scratch_shapes=[pltpu.VMEM((1, LANES), F32)],
        compiler_params=_cparams("arbitrary"),
    )(rest, b128, d_fq, d_fk)


def _mla_prep(rest, gq, gkv, wq, wkv, cos, sin):
    t = rest.shape[0]
    tm = _row_tile(t)

    def body(r_ref, gq_ref, gkv_ref, wq_ref, wkv_ref, c_ref, s_ref, q_ref, k_ref, kv_ref, cq_ref, ckv_ref):
        cos_, sin_ = c_ref[...], s_ref[...]
        cq, _ = _rms(r_ref[:, REST_CQ:REST_CKV], gq_ref[...])
        ckv, _ = _rms(r_ref[:, REST_CKV:REST_KR], gkv_ref[...])
        cqb, ckvb = cq.astype(BF16), ckv.astype(BF16)
        cq_ref[...] = cqb
        ckv_ref[...] = ckvb
        k_rope = _rope(r_ref[:, REST_KR:REST_COLS], cos_, sin_)
        lo = _lane() < NOPE
        for h in range(HEADS):
            q_ref[h] = _rope(_dot(cqb, wq_ref[h]), cos_, sin_).astype(BF16)
            kv = _dot(ckvb, wkv_ref[h])
            kv_ref[h] = kv.astype(BF16)
            k_ref[h] = (jnp.where(lo, kv, 0.0) + k_rope).astype(BF16)

    row = lambda n: pl.BlockSpec((tm, n), lambda i: (i, 0))
    full = lambda a: pl.BlockSpec(a.shape, lambda i: (0,) * a.ndim)
    heads = pl.BlockSpec((HEADS, tm, LANES), lambda i: (0, i, 0))
    hshape = jax.ShapeDtypeStruct((HEADS, t, LANES), BF16)
    return pl.pallas_call(
        body, name="mla_prep", grid=(t // tm,),
        out_shape=(hshape, hshape, hshape, jax.ShapeDtypeStruct((t, Q_RANK), BF16), jax.ShapeDtypeStruct((t, KV_RANK), BF16)),
        in_specs=[row(REST_COLS), full(gq), full(gkv), full(wq), full(wkv), row(LANES), row(LANES)],
        out_specs=(heads, heads, heads, row(Q_RANK), row(KV_RANK)),
        compiler_params=_cparams("parallel"),
    )(rest, gq, gkv, wq, wkv, cos, sin)


def _pair_specs(fox, t, tq, blocked_q):
    if fox:
        blk = pl.BlockSpec((tq, LANES), lambda p, i: (i, p))
        whole = pl.BlockSpec((t, LANES), lambda p, i: (0, p))
    else:
        blk = pl.BlockSpec((2, tq, LANES), lambda p, i: (p, i, 0))
        whole = pl.BlockSpec((2, t, LANES), lambda p, i: (p, 0, 0))
    return [blk, whole, whole] if blocked_q else [whole, blk, blk]


def _tile_lanes(x, n):
    return jnp.tile(x, (1, n)) if n > 1 else x


class _Comm(NamedTuple):
    inputs: tuple
    out_shape: tuple
    aliases: dict
    scratch: tuple
    start: Callable
    finish: Callable


def _hosted_call(name, main, grid, args, in_specs, out_shape, out_specs, scratch, comm):
    n_in, n_out, n_scr = len(args), len(out_shape), len(scratch)
    c_in = list(comm.inputs) if comm else []
    c_out = list(comm.out_shape) if comm else []

    def at_step(which):
        hit = pl.program_id(0) == which[0]
        for axis in range(1, len(grid)):
            hit = jnp.logical_and(hit, pl.program_id(axis) == which[axis])
        return hit

    def body(*refs):
        bounds = [0, n_in, len(c_in), n_out, len(c_out), n_scr]
        starts = [sum(bounds[:k + 1]) for k in range(len(bounds))]
        ins, cins, outs, couts, scr = [refs[a:b] for a, b in zip(starts[:-1], starts[1:])]
        sems = refs[starts[-1]:]
        if comm:
            @pl.when(at_step([0] * len(grid)))
            def _():
                comm.start(cins, couts, sems)
        main(ins, outs, scr)
        if comm:
            @pl.when(at_step([n - 1 for n in grid]))
            def _():
                comm.finish(cins, couts, sems)

    res = pl.pallas_call(
        body, name=name, grid=grid,
        out_shape=list(out_shape) + c_out,
        in_specs=list(in_specs) + _hbm_specs(len(c_in)),
        out_specs=list(out_specs) + _hbm_specs(len(c_out)),
        scratch_shapes=list(scratch) + (list(comm.scratch) if comm else []),
        input_output_aliases={n_in + i: n_out + o for i, o in comm.aliases.items()} if comm else {},
        compiler_params=_cparams(*(["arbitrary"] * len(grid))),
    )(*args, *c_in)
    return res[:n_out], res[n_out:]


def _stat_rows(x):
    return jnp.transpose(x)[0:8, :]


FWD_HEADS = 4


def _attn_fwd(fox, q, k, v, f2_rows=None, comm=None):
    t = q.shape[0] if fox else q.shape[1]
    tq = _row_tile(t)
    nq = t // tq
    c2 = (FOX_SCALE if fox else MLA_SCALE) * LOG2E
    nh = FWD_HEADS
    wide = (nh // 2) * LANES

    def main(ins, outs, scr):
        q_ref, k_ref, v_ref = ins[:3]
        fr_ref = ins[3] if fox else None
        o_ref, lset_ref = outs
        m_sc, acc_sc = scr
        i = pl.program_id(1)
        lo = _lane() < HEAD_DIM
        hi = jnp.logical_not(lo)
        zero, one = jnp.zeros((), BF16), jnp.ones((), BF16)
        lanes_of = lambda h: slice((h // 2) * LANES, (h // 2 + 1) * LANES)
        if fox:
            qs = [jnp.where(lo if h % 2 == 0 else hi, q_ref[:, lanes_of(h)], zero) for h in range(nh)]
            sum_lanes = [hi if h % 2 == 0 else lo for h in range(nh)]
        else:
            qs = [q_ref[h] for h in range(nh)]
            sum_lanes = [lo] * nh
        m_sc[...] = jnp.full_like(m_sc, -jnp.inf)
        acc_sc[...] = jnp.zeros_like(acc_sc)

        def block(j, r0, nr, c0, nc, seen_from):
            rows = slice(r0, r0 + nr)
            sl = pl.ds(pl.multiple_of(j * tq + c0, math.gcd(tq, c0) if c0 else tq), nc)
            if seen_from is not None:
                seen = (lax.broadcasted_iota(jnp.int32, (nr, nc), 1)
                        <= lax.broadcasted_iota(jnp.int32, (nr, nc), 0) + seen_from)
            for h in range(nh):
                kj, vj = (k_ref[sl, lanes_of(h)], v_ref[sl, lanes_of(h)]) if fox else (k_ref[h, sl, :], v_ref[h, sl, :])
                s = _dot_nt(qs[h][rows], kj) * c2
                if fox:
                    s = s - fr_ref[h, j, :, c0:c0 + nc]
                if seen_from is not None:
                    s = jnp.where(seen, s, -jnp.inf)
                m_prev = m_sc[h, rows]
                m_new = jnp.maximum(m_prev, jnp.max(s, axis=1, keepdims=True))
                p = jnp.exp2(s - _tile_lanes(m_new, nc // LANES))
                vj = jnp.where(sum_lanes[h], one, vj)
                acc_sc[h, rows] = jnp.exp2(m_prev - m_new) * acc_sc[h, rows] + _dot(p.astype(BF16), vj)
                m_sc[h, rows] = m_new

        def loop_body(j, carry):
            block(j, 0, tq, 0, tq, None)
            return carry

        lax.fori_loop(0, i, loop_body, 0)
        half = tq // 2
        if half % LANES == 0:
            block(i, 0, half, 0, half, 0)
            block(i, half, half, 0, tq, half)
        else:
            block(i, 0, tq, 0, tq, 0)
        res = []
        for h in range(nh):
            acc = acc_sc[h]
            swapped = pltpu.roll(acc, HEAD_DIM, 1)
            res.append(acc / swapped)
            lse2 = m_sc[h] + jnp.log(jnp.where(sum_lanes[h], acc, swapped)) * LOG2E
            lset_ref[h, 0] = _stat_rows(lse2)
        for pr in range(nh // 2):
            even = res[2 * pr] if fox else pltpu.roll(res[2 * pr], HEAD_DIM, 1)
            o_ref[:, pr * LANES:(pr + 1) * LANES] = jnp.where(lo, even, res[2 * pr + 1])

    if fox:
        in_specs = [pl.BlockSpec((tq, wide), lambda g, i: (i, g))] + [pl.BlockSpec((t, wide), lambda g, i: (0, g))] * 2
        in_specs += [pl.BlockSpec((nh, nq, 1, tq), lambda g, i: (g, 0, 0, 0))]
        args = [q, k, v, f2_rows]
    else:
        in_specs = [pl.BlockSpec((nh, tq, LANES), lambda g, i: (g, i, 0))] + [pl.BlockSpec((nh, t, LANES), lambda g, i: (g, 0, 0))] * 2
        args = [q, k, v]
    return _hosted_call(
        "fox_attn_fwd" if fox else "mla_attn_fwd", main, (HEADS // nh, nq), args, in_specs,
        (jax.ShapeDtypeStruct((t, 4 * LANES), F32), jax.ShapeDtypeStruct((HEADS, nq, 8, tq), F32)),
        (pl.BlockSpec((tq, wide), lambda g, i: (i, g)), pl.BlockSpec((nh, 1, 8, tq), lambda g, i: (g, i, 0, 0))),
        [pltpu.VMEM((nh, tq, LANES), F32), pltpu.VMEM((nh, tq, LANES), F32)], comm)


def _head_do(fox, hh, do2, lo):
    if fox:
        return jnp.where(lo if hh == 0 else jnp.logical_not(lo), do2, 0.0)
    return jnp.where(lo, 0.0, pltpu.roll(do2, HEAD_DIM, 1) if hh == 0 else do2)


def _attn_bwd(fox, q, k, v, do, lse_rows, delta_rows, f2_rep=None, comm=None):
    t = q.shape[0] if fox else q.shape[1]
    tq = _row_tile(t)
    nq = t // tq
    scale = FOX_SCALE if fox else MLA_SCALE
    c2 = scale * LOG2E

    def main(ins, outs, scr):
        if fox:
            q_ref, k_ref, v_ref, f_ref, do_ref, lse_ref, dl_ref = ins
            dq_ref, dk_ref, dv_ref, dfq_ref, dfk_ref = outs
        else:
            q_ref, k_ref, v_ref, do_ref, lse_ref, dl_ref = ins
            dq_ref, dkv_ref, dkr_ref = outs
        dq_sc, dk_sc, dv_sc = scr
        j = pl.program_id(1)
        lane = _lane()
        lo = lane < HEAD_DIM
        hi = jnp.logical_not(lo)
        zero, one = jnp.zeros((), BF16), jnp.ones((), BF16)

        @pl.when(j == 0)
        def _():
            dq_sc[...] = jnp.zeros_like(dq_sc)

        dk_sc[...] = jnp.zeros_like(dk_sc)
        dv_sc[...] = jnp.zeros_like(dv_sc)

        def block(i, r0, nr, c0, nc, masked):
            rows, cols = slice(r0, r0 + nr), slice(c0, c0 + nc)
            sl = pl.ds(pl.multiple_of(i * tq + c0, math.gcd(tq, c0) if c0 else tq), nc)
            do_i = do_ref[sl, :]
            if masked:
                seen = lax.broadcasted_iota(jnp.int32, (nr, nc), 1) >= lax.broadcasted_iota(jnp.int32, (nr, nc), 0)
            for hh in range(2):
                kj = k_ref[rows, :] if fox else k_ref[hh, rows, :]
                vj = v_ref[rows, :] if fox else v_ref[hh, rows, :]
                qi = jnp.where(lo if hh == 0 else hi, q_ref[sl, :], zero) if fox else q_ref[hh, sl, :]
                dob = _head_do(fox, hh, do_i, lo).astype(BF16)
                st = _dot_nt(kj, qi) * c2
                if fox:
                    st = st - _tile_lanes(f_ref[hh, rows, :], nc // LANES)
                if masked:
                    st = jnp.where(seen, st, -jnp.inf)
                pt = jnp.exp2(st - lse_ref[hh, i, 0:1, cols])
                dpt = _dot_nt(vj, dob)
                dst = (pt * (dpt - dl_ref[hh, i, 0:1, cols])).astype(BF16)
                dv_sc[hh, rows] += _dot(pt.astype(BF16), dob)
                if fox:
                    other = hi if hh == 0 else lo
                    qi = jnp.where(other, one, qi)
                    kj = jnp.where(other, one, kj)
                dk_sc[hh, rows] += _dot(dst, qi)
                dq_sc[hh, sl, :] += _dot_tn(dst, kj)

        def loop_body(i, carry):
            block(i, 0, tq, 0, tq, False)
            return carry

        half = tq // 2
        if half % LANES == 0:
            block(j, 0, half, 0, tq, True)
            block(j, half, half, half, half, True)
        else:
            block(j, 0, tq, 0, tq, True)
        lax.fori_loop(j + 1, nq, loop_body, 0)
        if fox:
            dk_ref[...] = (jnp.where(lo, dk_sc[0], dk_sc[1]) * scale).astype(BF16)
            dv_ref[...] = (dv_sc[0] + dv_sc[1]).astype(BF16)
            for hh in range(2):
                dk = dk_sc[hh]
                dfk_ref[hh] = -jnp.where(hi if hh == 0 else lo, dk, pltpu.roll(dk, HEAD_DIM, 1))
        else:
            rope_lanes = jnp.logical_and(lane >= NOPE, lane < NOPE + ROPE)
            dkr = jnp.zeros((tq, LANES), F32)
            for hh in range(2):
                dk = dk_sc[hh] * scale
                dkv_ref[hh] = jnp.where(lo, dk, dv_sc[hh])
                dkr = dkr + jnp.where(rope_lanes, dk, 0.0)
            dkr_ref[0] = dkr

        @pl.when(j == nq - 1)
        def _():
            for i in range(nq):
                rows = slice(i * tq, (i + 1) * tq)
                if fox:
                    dq_ref[rows, :] = (jnp.where(lo, dq_sc[0, rows, :], dq_sc[1, rows, :]) * scale).astype(BF16)
                    for hh in range(2):
                        acc = dq_sc[hh, rows, :]
                        dfq_ref[hh, rows, :] = jnp.where(hi if hh == 0 else lo, acc, pltpu.roll(acc, HEAD_DIM, 1))
                else:
                    for hh in range(2):
                        dq_ref[hh, rows, :] = dq_sc[hh, rows, :] * scale

    stat = pl.BlockSpec((2, tq, LANES), lambda p, j: (p, j, 0))
    stat_all = pl.BlockSpec((2, t, LANES), lambda p, j: (p, 0, 0))
    rows4 = pl.BlockSpec((2, nq, 8, tq), lambda p, j: (p, 0, 0, 0))
    pair = pl.BlockSpec((tq, LANES), lambda p, j: (j, p))
    pair_all = pl.BlockSpec((t, LANES), lambda p, j: (0, p))
    in_specs = _pair_specs(fox, t, tq, False)
    args = [q, k, v]
    if fox:
        in_specs += [stat]
        args += [f2_rep]
    in_specs += [pair_all, rows4, rows4]
    args += [do, lse_rows, delta_rows]
    heads_f32 = jax.ShapeDtypeStruct((HEADS, t, LANES), F32)
    if fox:
        wide = jax.ShapeDtypeStruct((t, 4 * LANES), BF16)
        out_shape = (wide, wide, wide, heads_f32, heads_f32)
        out_specs = (pair_all, pair, pair, stat_all, stat)
    else:
        out_shape = (heads_f32, heads_f32, jax.ShapeDtypeStruct((HEADS // 2, t, LANES), F32))
        out_specs = (stat_all, stat, pl.BlockSpec((1, tq, LANES), lambda p, j: (p, j, 0)))
    acc = pltpu.VMEM((2, tq, LANES), F32)
    return _hosted_call("fox_attn_bwd" if fox else "mla_attn_bwd", main, (HEADS // 2, nq), args, in_specs,
                        out_shape, out_specs, [pltpu.VMEM((2, t, LANES), F32), acc, acc], comm)


def _attn_out(x, fox_o, mla_o, gf, gm, w_o):
    t = x.shape[0]
    tm = _row_tile(t)

    def body(x_ref, f_ref, m_ref, gf_ref, gm_ref, w_ref, x1_ref, mix_ref):
        nf, _ = _rms(f_ref[...], gf_ref[...])
        nm, _ = _rms(m_ref[...], gm_ref[...])
        nfb, nmb = nf.astype(BF16), nm.astype(BF16)
        mix_ref[:, :FOX_WIDTH] = nfb
        mix_ref[:, FOX_WIDTH:] = nmb
        x1_ref[...] = x_ref[...] + _dot(nfb, w_ref[:FOX_WIDTH, :]) + _dot(nmb, w_ref[FOX_WIDTH:, :])

    row = lambda n: pl.BlockSpec((tm, n), lambda i: (i, 0))
    full = lambda a: pl.BlockSpec(a.shape, lambda i: (0,) * a.ndim)
    return pl.pallas_call(
        body, name="attn_out", grid=(t // tm,),
        out_shape=(jax.ShapeDtypeStruct((t, D_MODEL), F32), jax.ShapeDtypeStruct((t, D_MODEL), BF16)),
        in_specs=[row(D_MODEL), row(FOX_WIDTH), row(MLA_WIDTH), full(gf), full(gm), full(w_o)],
        out_specs=(row(D_MODEL), row(D_MODEL)),
        compiler_params=_cparams("parallel"),
    )(x, fox_o, mla_o, gf, gm, w_o)


def _mlp_tile(t):
    return 256 if t >= 2048 else 128


def _resident(a):
    return pl.BlockSpec(a.shape, lambda i: (0,) * a.ndim, pipeline_mode=pl.Buffered(1))


FF_CHUNK = 512


def _mlp_fwd(x1, g_mlp, w_up, w_down, g_fin, target):
    t = x1.shape[0]
    tm = _mlp_tile(t)

    def body(x_ref, g_ref, wu_ref, wd_ref, gf_ref, t_ref, u_ref, h_ref, dx_ref, dxb_ref, loss_ref, dg_ref, a_sc):
        @pl.when(pl.program_id(0) == 0)
        def _():
            loss_ref[...] = jnp.zeros_like(loss_ref)
            dg_ref[...] = jnp.zeros_like(dg_ref)

        x = x_ref[...]
        h, _ = _rms(x, g_ref[...])
        hb = h.astype(BF16)
        h_ref[...] = hb
        for f in range(D_FF // FF_CHUNK):
            sl = slice(f * FF_CHUNK, (f + 1) * FF_CHUNK)
            u = _dot(hb, wu_ref[:, sl])
            u_ref[:, sl] = u
            r = jnp.maximum(u, 0.0)
            a_sc[:, sl] = (r * r).astype(BF16)
        x2 = x + _dot(a_sc[...], wd_ref[...])
        y, r2 = _rms(x2, gf_ref[...])
        err = y - t_ref[...]
        loss_ref[...] += 0.5 * jnp.sum(jnp.mean(err * err, axis=-1, keepdims=True))
        dx, dg = _rms_bwd(x2, gf_ref[...], r2, err * (1.0 / D_MODEL))
        dx_ref[...] = dx
        dxb_ref[...] = dx.astype(BF16)
        dg_ref[...] += dg

    row = lambda n: pl.BlockSpec((tm, n), lambda i: (i, 0))
    vec = pl.BlockSpec((1, D_MODEL), lambda i: (0, 0))
    return pl.pallas_call(
        body, name="mlp_fwd", grid=(t // tm,),
        out_shape=(jax.ShapeDtypeStruct((t, D_FF), F32), jax.ShapeDtypeStruct((t, D_MODEL), BF16),
                   jax.ShapeDtypeStruct((t, D_MODEL), F32), jax.ShapeDtypeStruct((t, D_MODEL), BF16),
                   jax.ShapeDtypeStruct((8, LANES), F32), jax.ShapeDtypeStruct((1, D_MODEL), F32)),
        in_specs=[row(D_MODEL), vec, _resident(w_up), _resident(w_down), vec, row(D_MODEL)],
        out_specs=(row(D_FF), row(D_MODEL), row(D_MODEL), row(D_MODEL), pl.BlockSpec((8, LANES), lambda i: (0, 0)), vec),
        scratch_shapes=[pltpu.VMEM((tm, D_FF), BF16)],
        compiler_params=_cparams("arbitrary"),
    )(x1, g_mlp, w_up, w_down, g_fin, target)


def _mlp_bwd(dx2, u, x1, g_mlp, w_up, w_down):
    t = x1.shape[0]
    tm = _mlp_tile(t)

    def body(dx_ref, u_ref, x_ref, g_ref, wu_ref, wd_ref, du_ref, a_ref, dx1_ref, dx1b_ref, dg_ref):
        @pl.when(pl.program_id(0) == 0)
        def _():
            dg_ref[...] = jnp.zeros_like(dg_ref)

        dx2 = dx_ref[...]
        dxb = dx2.astype(BF16)
        for f in range(D_FF // FF_CHUNK):
            sl = slice(f * FF_CHUNK, (f + 1) * FF_CHUNK)
            r = jnp.maximum(u_ref[:, sl], 0.0)
            a_ref[:, sl] = (r * r).astype(BF16)
            da = _dot_nt(dxb, wd_ref[sl, :])
            du_ref[:, sl] = (da * (2.0 * r)).astype(BF16)
        dh = _dot_nt(du_ref[...], wu_ref[...])
        x = x_ref[...]
        _, r1 = _rms(x, g_ref[...])
        dx, dg = _rms_bwd(x, g_ref[...], r1, dh)
        dx1 = dx2 + dx
        dx1_ref[...] = dx1
        dx1b_ref[...] = dx1.astype(BF16)
        dg_ref[...] += dg

    row = lambda n: pl.BlockSpec((tm, n), lambda i: (i, 0))
    vec = pl.BlockSpec((1, D_MODEL), lambda i: (0, 0))
    return pl.pallas_call(
        body, name="mlp_bwd", grid=(t // tm,),
        out_shape=(jax.ShapeDtypeStruct((t, D_FF), BF16), jax.ShapeDtypeStruct((t, D_FF), BF16),
                   jax.ShapeDtypeStruct((t, D_MODEL), F32), jax.ShapeDtypeStruct((t, D_MODEL), BF16),
                   jax.ShapeDtypeStruct((1, D_MODEL), F32)),
        in_specs=[row(D_MODEL), row(D_FF), row(D_MODEL), vec, _resident(w_up), _resident(w_down)],
        out_specs=(row(D_FF), row(D_FF), row(D_MODEL), row(D_MODEL), vec),
        compiler_params=_cparams("arbitrary"),
    )(dx2, u, x1, g_mlp, w_up, w_down)


def _matmul_tn(name, a, b, blocks=None):
    t, m = a.shape
    n = b.shape[1]
    tk = t if a.dtype == BF16 and b.dtype == BF16 else min(t, 2048)
    steps = t // tk
    bm = m if m <= 1024 else 512
    bn = n if n <= 1024 else 512
    width = bn if blocks is None else n // blocks
    per = bn // width

    def body(a_ref, b_ref, o_ref, acc_sc):
        kk = pl.program_id(2)

        @pl.when(kk == 0)
        def _():
            acc_sc[...] = jnp.zeros_like(acc_sc)

        acc_sc[...] += _dot_tn(a_ref[...].astype(BF16), b_ref[...].astype(BF16))

        @pl.when(kk == steps - 1)
        def _():
            if blocks is None:
                o_ref[...] = acc_sc[...]
            else:
                for s in range(per):
                    o_ref[s] = acc_sc[:, s * width:(s + 1) * width]

    if blocks is None:
        o_spec = pl.BlockSpec((bm, bn), lambda i, j, kk: (i, j))
        o_shape = (m, n)
    else:
        o_spec = pl.BlockSpec((per, bm, width), lambda i, j, kk: (j, i, 0))
        o_shape = (blocks, m, width)
    return pl.pallas_call(
        body, name=name, grid=(m // bm, n // bn, steps),
        out_shape=jax.ShapeDtypeStruct(o_shape, F32),
        in_specs=[pl.BlockSpec((tk, bm), lambda i, j, kk: (kk, i)), pl.BlockSpec((tk, bn), lambda i, j, kk: (kk, j))],
        out_specs=o_spec,
        scratch_shapes=[pltpu.VMEM((bm, bn), F32)],
        compiler_params=_cparams("parallel", "parallel", "arbitrary"),
    )(a, b)


def _attn_out_bwd(dx1, fox_o, mla_o, gf, gm, w_o):
    t = dx1.shape[0]
    tm = _row_tile(t)

    def body(dx_ref, f_ref, m_ref, gf_ref, gm_ref, w_ref, df_ref, dm_ref, dlf_ref, dlm_ref, dgf_ref, dgm_ref):
        @pl.when(pl.program_id(0) == 0)
        def _():
            dgf_ref[...] = jnp.zeros_like(dgf_ref)
            dgm_ref[...] = jnp.zeros_like(dgm_ref)
        dxb = dx_ref[...].astype(BF16)
        lane = lax.broadcasted_iota(jnp.int32, (8, LANES), 1)
        picks = [(lane < HEAD_DIM).astype(BF16), (lane >= HEAD_DIM).astype(BF16)]
        for o_ref, g_ref, lo_row, d_ref, dl_ref, dg_ref in ((f_ref, gf_ref, 0, df_ref, dlf_ref, dgf_ref),
                                                             (m_ref, gm_ref, FOX_WIDTH, dm_ref, dlm_ref, dgm_ref)):
            dn = _dot_nt(dxb, w_ref[lo_row:lo_row + FOX_WIDTH, :])
            o = o_ref[...]
            _, r = _rms(o, g_ref[...])
            d, dg = _rms_bwd(o, g_ref[...], r, dn)
            d_ref[...] = d
            dg_ref[...] += dg
            prod = d * o
            for h in range(HEADS):
                parts = _split3(prod[:, (h // 2) * LANES:(h // 2 + 1) * LANES])
                dl_ref[h, 0] = (_dot_nt(picks[h % 2], parts[0]) + _dot_nt(picks[h % 2], parts[1])) + _dot_nt(picks[h % 2], parts[2])

    row = lambda n: pl.BlockSpec((tm, n), lambda i: (i, 0))
    full = lambda a: pl.BlockSpec(a.shape, lambda i: (0,) * a.ndim)
    vec = pl.BlockSpec((1, FOX_WIDTH), lambda i: (0, 0))
    rows = pl.BlockSpec((HEADS, 1, 8, tm), lambda i: (0, i, 0, 0))
    o_shape = jax.ShapeDtypeStruct((t, FOX_WIDTH), F32)
    g_shape = jax.ShapeDtypeStruct((1, FOX_WIDTH), F32)
    r_shape = jax.ShapeDtypeStruct((HEADS, t // tm, 8, tm), F32)
    return pl.pallas_call(
        body, name="attn_out_bwd", grid=(t // tm,),
        out_shape=(o_shape, o_shape, r_shape, r_shape, g_shape, g_shape),
        in_specs=[row(D_MODEL), row(FOX_WIDTH), row(MLA_WIDTH), full(gf), full(gm), full(w_o)],
        out_specs=(row(FOX_WIDTH), row(MLA_WIDTH), rows, rows, vec, vec),
        compiler_params=_cparams("arbitrary"),
    )(dx1, fox_o, mla_o, gf, gm, w_o)


def _mla_prep_bwd(dq, dkv, dkr, dz, rest, gq, gkv, wq, wkv, cos, sin):
    t = rest.shape[0]
    tm = _row_tile(t)

    def body(dq_ref, dkv_ref, dkr_ref, dz_ref, r_ref, gq_ref, gkv_ref, wq_ref, wkv_ref, c_ref, s_ref,
             dr_ref, dqp_ref, dkvb_ref, dgq_ref, dgkv_ref):
        @pl.when(pl.program_id(0) == 0)
        def _():
            dgq_ref[...] = jnp.zeros_like(dgq_ref)
            dgkv_ref[...] = jnp.zeros_like(dgkv_ref)
        cos_, sin_ = c_ref[...], s_ref[...]
        dcq = jnp.zeros((tm, Q_RANK), F32)
        dckv = jnp.zeros((tm, KV_RANK), F32)
        for h in range(HEADS):
            dqp = _rope_bwd(dq_ref[h], cos_, sin_).astype(BF16)
            dqp_ref[:, h * LANES:(h + 1) * LANES] = dqp
            dcq = dcq + _dot_nt(dqp, wq_ref[h])
            dkvb = dkv_ref[h].astype(BF16)
            dkvb_ref[:, h * LANES:(h + 1) * LANES] = dkvb
            dckv = dckv + _dot_nt(dkvb, wkv_ref[h])
        dkrope = dkr_ref[0]
        for pr in range(1, HEADS // 2):
            dkrope = dkrope + dkr_ref[pr]
        cq = r_ref[:, REST_CQ:REST_CKV]
        _, rq = _rms(cq, gq_ref[...])
        d_cq, dgq = _rms_bwd(cq, gq_ref[...], rq, dcq)
        ckv = r_ref[:, REST_CKV:REST_KR]
        _, rkv = _rms(ckv, gkv_ref[...])
        d_ckv, dgkv = _rms_bwd(ckv, gkv_ref[...], rkv, dckv)
        dgq_ref[...] += dgq
        dgkv_ref[...] += dgkv
        dr_ref[:, 0:REST_CQ] = dz_ref[...].astype(BF16)
        dr_ref[:, REST_CQ:REST_CKV] = d_cq.astype(BF16)
        dr_ref[:, REST_CKV:REST_KR] = d_ckv.astype(BF16)
        dr_ref[:, REST_KR:REST_COLS] = _rope_bwd(dkrope, cos_, sin_).astype(BF16)

    row = lambda n: pl.BlockSpec((tm, n), lambda i: (i, 0))
    full = lambda a: pl.BlockSpec(a.shape, lambda i: (0,) * a.ndim)
    heads = pl.BlockSpec((HEADS, tm, LANES), lambda i: (0, i, 0))
    hshape = jax.ShapeDtypeStruct((t, HEADS * LANES), BF16)
    return pl.pallas_call(
        body, name="mla_prep_bwd", grid=(t // tm,),
        out_shape=(jax.ShapeDtypeStruct((t, REST_COLS), BF16), hshape, hshape,
                   jax.ShapeDtypeStruct((1, Q_RANK), F32), jax.ShapeDtypeStruct((1, KV_RANK), F32)),
        in_specs=[heads, heads, pl.BlockSpec((HEADS // 2, tm, LANES), lambda i: (0, i, 0)), row(LANES), row(REST_COLS),
                  full(gq), full(gkv), full(wq), full(wkv), row(LANES), row(LANES)],
        out_specs=(row(REST_COLS), row(HEADS * LANES), row(HEADS * LANES), pl.BlockSpec((1, Q_RANK), lambda i: (0, 0)),
                   pl.BlockSpec((1, KV_RANK), lambda i: (0, 0))),
        compiler_params=_cparams("arbitrary"),
    )(dq, dkv, dkr, dz, rest, gq, gkv, wq, wkv, cos, sin)


def _in_proj_bwd(x, g, dx1, dfq, dfk, dfv, drest, w_qkv, w_rest, comm=None):
    t = x.shape[0]
    tm = _row_tile(t)

    def main(ins, outs, scr):
        x_ref, g_ref, dx1_ref, dq_ref, dk_ref, dv_ref, dr_ref, wq_ref, wr_ref = ins
        dx_ref, dg_ref = outs

        @pl.when(pl.program_id(0) == 0)
        def _():
            dg_ref[...] = jnp.zeros_like(dg_ref)
        dh = _dot(dr_ref[...], wr_ref[...])
        for n, ref in enumerate((dq_ref, dk_ref, dv_ref)):
            dh = dh + _dot(ref[...], wq_ref[n * FOX_WIDTH:(n + 1) * FOX_WIDTH, :])
        xv = x_ref[...]
        _, r = _rms(xv, g_ref[...])
        dx, dg = _rms_bwd(xv, g_ref[...], r, dh)
        dx_ref[...] = dx1_ref[...] + dx
        dg_ref[...] += dg

    row = lambda n: pl.BlockSpec((tm, n), lambda i: (i, 0))
    full = lambda a: pl.BlockSpec(a.shape, lambda i: (0,) * a.ndim)
    vec = pl.BlockSpec((1, D_MODEL), lambda i: (0, 0))
    return _hosted_call(
        "in_proj_bwd", main, (t // tm,), [x, g, dx1, dfq, dfk, dfv, drest, w_qkv, w_rest],
        [row(D_MODEL), full(g), row(D_MODEL), row(FOX_WIDTH), row(FOX_WIDTH), row(FOX_WIDTH), row(REST_COLS),
         full(w_qkv), full(w_rest)],
        (jax.ShapeDtypeStruct((t, D_MODEL), F32), jax.ShapeDtypeStruct((1, D_MODEL), F32)), (row(D_MODEL), vec), [], comm)


def _pad_cols(a, n):
    return jnp.pad(a, ((0, 0),) * (a.ndim - 1) + ((0, n - a.shape[-1]),))


def kernel(x, positions, attn_norm_g, w_in, b_forget, q_norm_g, w_uq, kv_norm_g, w_ukv, fox_out_g, mla_out_g, w_o, mlp_norm_g, w_up, w_down, final_norm_g, loss_target, m_attn_norm_g, m_w_in, m_b_forget, m_q_norm_g, m_w_uq, m_kv_norm_g, m_w_ukv, m_fox_out_g, m_mla_out_g, m_w_o, m_mlp_norm_g, m_w_up, m_w_down, m_final_norm_g, v_attn_norm_g, v_w_in, v_b_forget, v_q_norm_g, v_w_uq, v_kv_norm_g, v_w_ukv, v_fox_out_g, v_mla_out_g, v_w_o, v_mlp_norm_g, v_w_up, v_w_down, v_final_norm_g):
    t = x.shape[1]
    tq = _row_tile(t)
    xs = x[0]
    target = loss_target[0]

    early = [jnp.transpose(w_in[0]), _pad_cols(w_uq[0], LANES), w_ukv[0]]
    late = [w_o[0].astype(BF16), w_up[0].astype(BF16), w_down[0].astype(BF16)]
    g_in, wq, wkv = _all_gather([s.astype(BF16) for s in early])
    win = g_in.reshape(IN_COLS, D_MODEL)
    off_ff, off_cq, off_kr = 3 * FOX_WIDTH, 3 * FOX_WIDTH + HEADS, IN_COLS - ROPE
    zeros = lambda n: jnp.zeros((n, D_MODEL), BF16)
    w_qkv = win[:off_ff]
    w_rest = jnp.concatenate([
        win[off_ff:off_cq], zeros(REST_CQ - HEADS), win[off_cq:off_kr],
        zeros(NOPE), win[off_kr:], zeros(LANES - NOPE - ROPE)], axis=0)

    cos, sin = _rope_tables(positions.reshape(t, 1))
    h1, fq, fk, fv, rest = _in_proj(xs, attn_norm_g, w_qkv, w_rest)
    b128 = _pad_cols(b_forget, LANES)
    f2_rows, f2_rep = _forget_cumsum(rest, b128)
    f2_rows = f2_rows.reshape(HEADS, t // tq, 1, tq)
    (fox_o, fox_lse_rows), partly = _attn_fwd(True, fq, fk, fv, f2_rows, comm=_ag_direct(late))
    mq, mk, mkv, cqn, ckvn = _mla_prep(rest, q_norm_g, kv_norm_g, wq, wkv, cos, sin)
    (mla_o, mla_lse_rows), (g_o, g_up, g_down) = _attn_fwd(False, mq, mk, mkv, comm=_ag_forward(partly))
    wo = g_o.reshape(D_MODEL, D_MODEL)
    x1, mixed = _attn_out(xs, fox_o, mla_o, fox_out_g, mla_out_g, wo)
    wup = jnp.transpose(g_up, (1, 0, 2)).reshape(D_MODEL, D_FF)
    wdown = g_down.reshape(D_FF, D_MODEL)
    u, h2, dx2, dx2b, loss8, d_gfin = _mlp_fwd(x1, mlp_norm_g, wup, wdown, final_norm_g.reshape(1, D_MODEL), target)

    du, act, dx1, dx1b, d_gmlp = _mlp_bwd(dx2, u, x1, mlp_norm_g, wup, wdown)
    dw_down = _matmul_tn("dw_down", act, dx2b)
    dw_up = _matmul_tn("dw_up", h2, du, blocks=N_DEV)
    dfox_o, dmla_o, fox_delta_rows, mla_delta_rows, d_gfox, d_gmla = _attn_out_bwd(dx1, fox_o, mla_o, fox_out_g, mla_out_g, wo)
    dw_o = _matmul_tn("dw_o", mixed, dx1b)

    core = lax.axis_index("c").astype(jnp.int32).reshape(1)
    chip = (2 * lax.axis_index("x") + lax.axis_index("y")).astype(jnp.int32).reshape(1)
    names = ("w_in", "w_uq", "w_ukv", "w_o", "w_up", "w_down")
    grads_b = [dw_o.reshape(N_DEV, -1, D_MODEL), dw_up, dw_down.reshape(N_DEV, -1, D_MODEL)]
    (dfq, dfk, dfv, d_fq, d_fk), got_b = _attn_bwd(True, fq, fk, fv, dfox_o, fox_lse_rows, fox_delta_rows,
                                                   f2_rep, comm=_rs_to_sibling(grads_b))
    sums_b = [_rs_sibling_sum("rs_sibling_sum_" + nm, g, l, core) for nm, g, l in zip(names[3:], grads_b, got_b)]
    dz, d_b = _forget_bwd(rest, b128, d_fq, d_fk)
    (dmq, dmkv, dmkr), others_b = _attn_bwd(False, mq, mk, mkv, dmla_o, mla_lse_rows, mla_delta_rows,
                                            comm=_rs_to_chips([s[1] for s in sums_b]))
    drest, dqp, dkvb, d_gq, d_gkv = _mla_prep_bwd(dmq, dmkv, dmkr, dz, rest, q_norm_g, kv_norm_g, wq, wkv, cos, sin)
    dw_uq = _matmul_tn("dw_uq", cqn, dqp, blocks=HEADS)
    dw_ukv = _matmul_tn("dw_ukv", ckvn, dkvb, blocks=HEADS)
    dw_q = _matmul_tn("dw_in_q", dfq, h1)
    dw_k = _matmul_tn("dw_in_k", dfk, h1)
    dw_v = _matmul_tn("dw_in_v", dfv, h1)
    dw_r = _matmul_tn("dw_in_rest", drest, h1)

    dw_in = jnp.concatenate([dw_q, dw_k, dw_v, dw_r[0:HEADS], dw_r[REST_CQ:REST_KR],
                             dw_r[REST_KR + NOPE:REST_KR + NOPE + ROPE]], axis=0)
    grads_a = [dw_in.reshape(N_DEV, IN_SHARD, D_MODEL), dw_uq, dw_ukv]
    got_a = _comm_call("rs_sibling_exchange", _rs_to_sibling(grads_a))
    sums_a = [_rs_sibling_sum("rs_sibling_sum_" + nm, g, l, core) for nm, g, l in zip(names[:3], grads_a, got_a)]
    (grad_x, d_gattn), others_a = _in_proj_bwd(xs, attn_norm_g, dx1, dfq, dfk, dfv, drest, w_qkv, w_rest,
                                               comm=_rs_to_chips([s[1] for s in sums_a]))
    sums, others = sums_a + sums_b, list(others_a) + list(others_b)
    sharded = (w_in, w_uq, w_ukv, w_o, w_up, w_down)
    moments_m = (m_w_in, m_w_uq, m_w_ukv, m_w_o, m_w_up, m_w_down)
    moments_v = (v_w_in, v_w_uq, v_w_ukv, v_w_o, v_w_up, v_w_down)
    g_in_t = _rs_final_sum("rs_final_sum_w_in", sums[0][0], others[0], chip)
    big = [_adamw_given("adamw_w_in", jnp.transpose(g_in_t), w_in, m_w_in, v_w_in)]
    for a in range(1, len(names)):
        big.append(_adamw_sharded("adamw_" + names[a], sharded[a], moments_m[a], moments_v[a], sums[a][0], others[a], chip))
    big_g, big_d, big_m, big_v = [[b[k] for b in big] for k in range(4)]

    as_row = lambda a: a.reshape(1, -1)
    small_w = (attn_norm_g, b_forget, q_norm_g, kv_norm_g, fox_out_g, mla_out_g, mlp_norm_g, final_norm_g)
    small_m = (m_attn_norm_g, m_b_forget, m_q_norm_g, m_kv_norm_g, m_fox_out_g, m_mla_out_g, m_mlp_norm_g, m_final_norm_g)
    small_v = (v_attn_norm_g, v_b_forget, v_q_norm_g, v_kv_norm_g, v_fox_out_g, v_mla_out_g, v_mlp_norm_g, v_final_norm_g)
    total = _small_all_reduce([d_gattn, d_b, d_gq, d_gkv, d_gfox, d_gmla, d_gmlp, d_gfin], loss8)
    small = _adamw_small(total, [as_row(a) for a in small_w], [as_row(a) for a in small_m], [as_row(a) for a in small_v])
    loss = small[0].reshape(())
    s_g, s_d, s_m, s_v = [[small[1 + 4 * r + k].reshape(small_w[r].shape) for r in range(len(small_w))] for k in range(4)]

    def ordered(small_, bigs):
        ga, bf, gq_, gkv_, gfo, gml, gmlp_, gfin_ = small_
        bin_, buq, bukv, bo, bup, bdown = bigs
        return [ga, bin_, bf, gq_, buq, gkv_, bukv, gfo, gml, bo, gmlp_, bup, bdown, gfin_]

    return (loss, grad_x[None], *ordered(s_g, big_g), *ordered(s_d, big_d), *ordered(s_m, big_m), *ordered(s_v, big_v))
```

```python
import math
from typing import Callable, NamedTuple

import numpy as np
import jax
import jax.numpy as jnp
from jax import lax
from jax.experimental import pallas as pl
from jax.experimental.pallas import tpu as pltpu

F32 = jnp.float32
BF16 = jnp.bfloat16
MESH = pl.DeviceIdType.MESH

D_MODEL = 1024
HEADS = 8
HEAD_DIM = 64
FOX_WIDTH = 512
MLA_WIDTH = 512
NOPE = 64
ROPE = 32
QK_DIM = 96
Q_RANK = 384
KV_RANK = 256
D_FF = 4096
IN_COLS = 2216
ROPE_THETA = 10000.0
EPS = 1e-6
FOX_SCALE = 1.0 / math.sqrt(HEAD_DIM)
MLA_SCALE = 1.0 / math.sqrt(QK_DIM)
ADAM_LR = 0.001
ADAM_B1 = 0.9
ADAM_B2 = 0.999
ADAM_EPS = 1e-08
ADAM_WD = 0.01
ADAM_STEP = 10

N_DEV = 8
LANES = 128
REST_COLS = 896
REST_CQ = LANES
REST_CKV = REST_CQ + Q_RANK
REST_KR = REST_CKV + KV_RANK
LOG2E = 1.4426950408889634
LN2 = 0.6931471805599453
FOX_Q_FACTOR = FOX_SCALE * LOG2E
MLA_Q_FACTOR = MLA_SCALE * LOG2E
VMEM_LIMIT = 56 * 1024 * 1024

IN_SHARD = IN_COLS // N_DEV
SMALL_SIZES = (1024, 8, 384, 256, 512, 512, 1024, 1024)
SMALL_ROWS = 16
LOSS_ROW = len(SMALL_SIZES)


def _cparams(*sem):
    return pltpu.CompilerParams(dimension_semantics=sem or None, vmem_limit_bytes=VMEM_LIMIT)


def _row_tile(t):
    return 512 if t >= 2048 else (256 if t >= 512 else 128)


def _dot(a, b):
    return jnp.dot(a, b, preferred_element_type=F32)


def _dot_nt(a, b):
    return lax.dot_general(a, b, (((1,), (1,)), ((), ())), preferred_element_type=F32)


def _dot_tn(a, b):
    return lax.dot_general(a, b, (((0,), (0,)), ((), ())), preferred_element_type=F32)


def _rms(x, g):
    r = lax.rsqrt(jnp.mean(x * x, axis=-1, keepdims=True) + EPS)
    return x * r * g, r


def _rms_bwd(x, g, r, dy):
    xh = x * r
    gdy = dy * g
    dx = r * (gdy - xh * jnp.mean(gdy * xh, axis=-1, keepdims=True))
    return dx, jnp.sum(dy * xh, axis=0, keepdims=True)


def _lane():
    return lax.broadcasted_iota(jnp.int32, (1, LANES), 1)


def _rot(x):
    lane = _lane()
    half = NOPE + ROPE // 2
    first = jnp.logical_and(lane >= NOPE, lane < half)
    second = jnp.logical_and(lane >= half, lane < NOPE + ROPE)
    return jnp.where(first, -pltpu.roll(x, LANES - ROPE // 2, 1), jnp.where(second, pltpu.roll(x, ROPE // 2, 1), 0.0))


def _rope(x, cos, sin):
    return x * cos + _rot(x) * sin


def _rope_bwd(dy, cos, sin):
    return dy * cos - _rot(dy * sin)


def _remote(src, dst, send_sem, recv_sem, to):
    return pltpu.make_async_remote_copy(src_ref=src, dst_ref=dst, send_sem=send_sem, recv_sem=recv_sem,
                                        device_id=to, device_id_type=MESH)


def _hbm_specs(n):
    return [pl.BlockSpec(memory_space=pl.ANY)] * n


def _all_gather(blocks):
    n = len(blocks)

    def body(*refs):
        x_refs, out_refs = refs[:n], refs[n:2 * n]
        send_sems, recv_sems, local_sems = refs[2 * n:]
        x, y, c = lax.axis_index("x"), lax.axis_index("y"), lax.axis_index("c")
        me, sibling = (x, y, c), (x, y, 1 - c)
        chips = [(1 - x, y), (x, 1 - y), (1 - x, 1 - y)]

        def slot(a, px, py, pc):
            return out_refs[a].at[4 * px + 2 * py + pc]

        def copy(a, k, blk, to, src=None):
            return _remote(slot(a, *blk) if src is None else src, slot(a, *blk),
                           send_sems.at[7 * a + k], recv_sems.at[7 * a + k], to)

        mine = [pltpu.make_async_copy(x_refs[a], slot(a, *me), local_sems.at[a]) for a in range(n)]
        first, passed = [], []
        for a in range(n):
            mine[a].start()
            first.append(copy(a, 0, me, sibling, src=x_refs[a]))
            first += [copy(a, 1 + j, me, (*chip, c), src=x_refs[a]) for j, chip in enumerate(chips)]
        for cp in first:
            cp.start()
        for a in range(n):
            for j, chip in enumerate(chips):
                copy(a, 1 + j, (*chip, c), me).wait_recv()
                passed.append(copy(a, 4 + j, (*chip, c), sibling))
                passed[-1].start()
        for a in range(n):
            copy(a, 0, sibling, me).wait_recv()
            for j, chip in enumerate(chips):
                copy(a, 4 + j, (*chip, 1 - c), me).wait_recv()
        for cp in first + passed:
            cp.wait_send()
        for cp in mine:
            cp.wait()

    return pl.pallas_call(
        body, name="all_gather_weights",
        out_shape=[jax.ShapeDtypeStruct((N_DEV,) + b.shape, b.dtype) for b in blocks],
        in_specs=_hbm_specs(n), out_specs=_hbm_specs(n),
        scratch_shapes=[pltpu.SemaphoreType.DMA((7 * n,)), pltpu.SemaphoreType.DMA((7 * n,)), pltpu.SemaphoreType.DMA((n,))],
    )(*blocks)


def _symmetric_comm(inputs, out_shape, aliases, per_array, copies):
    def start(in_refs, out_refs, sems):
        for cp in copies(in_refs, out_refs, *sems):
            cp.start()

    def finish(in_refs, out_refs, sems):
        for cp in copies(in_refs, out_refs, *sems):
            cp.wait()

    n_sems = per_array * len(inputs)
    return _Comm(tuple(inputs), tuple(out_shape), aliases,
                 (pltpu.SemaphoreType.DMA((n_sems,)), pltpu.SemaphoreType.DMA((n_sems,))), start, finish)


def _ag_direct(shards):
    def copies(in_refs, out_refs, send_sems, recv_sems):
        x, y, c = lax.axis_index("x"), lax.axis_index("y"), lax.axis_index("c")
        peers = [(x, y, 1 - c), (1 - x, y, c), (x, 1 - y, c), (1 - x, 1 - y, c)]
        cps = []
        for a in range(len(shards)):
            mine = out_refs[a].at[4 * x + 2 * y + c]
            cps.append(pltpu.make_async_copy(in_refs[a], mine, send_sems.at[5 * a]))
            cps += [_remote(in_refs[a], mine, send_sems.at[5 * a + k], recv_sems.at[5 * a + k], peer)
                    for k, peer in enumerate(peers, start=1)]
        return cps

    return _symmetric_comm(shards, [jax.ShapeDtypeStruct((N_DEV,) + s.shape, s.dtype) for s in shards], {}, 5, copies)


def _ag_to_all(shards):
    def copies(in_refs, out_refs, send_sems, recv_sems):
        x, y, c = lax.axis_index("x"), lax.axis_index("y"), lax.axis_index("c")
        cps = []
        for a in range(len(shards)):
            mine = out_refs[a].at[4 * x + 2 * y + c]
            cps.append(pltpu.make_async_copy(in_refs[a], mine, send_sems.at[N_DEV * a]))
            for k in range(1, N_DEV):
                peer = (x ^ (k >> 2), y ^ ((k >> 1) & 1), c ^ (k & 1))
                cps.append(_remote(in_refs[a], mine, send_sems.at[N_DEV * a + k], recv_sems.at[N_DEV * a + k], peer))
        return cps

    return _symmetric_comm(shards, [jax.ShapeDtypeStruct((N_DEV,) + s.shape, s.dtype) for s in shards], {}, N_DEV, copies)


def _ag_forward(gathered):
    def copies(in_refs, out_refs, send_sems, recv_sems):
        x, y, c = lax.axis_index("x"), lax.axis_index("y"), lax.axis_index("c")
        chips = [(1 - x, y), (x, 1 - y), (1 - x, 1 - y)]
        return [_remote(out_refs[a].at[4 * cx + 2 * cy + c], out_refs[a].at[4 * cx + 2 * cy + c],
                        send_sems.at[3 * a + j], recv_sems.at[3 * a + j], (x, y, 1 - c))
                for a in range(len(gathered)) for j, (cx, cy) in enumerate(chips)]

    shapes = [jax.ShapeDtypeStruct(g.shape, g.dtype) for g in gathered]
    return _symmetric_comm(gathered, shapes, {a: a for a in range(len(gathered))}, 3, copies)


def _rs_to_sibling(grads):
    def copies(in_refs, out_refs, send_sems, recv_sems):
        x, y, c = lax.axis_index("x"), lax.axis_index("y"), lax.axis_index("c")
        return [_remote(in_refs[a].at[2 * q + 1 - c], out_refs[a].at[q], send_sems.at[4 * a + q], recv_sems.at[4 * a + q], (x, y, 1 - c))
                for a in range(len(grads)) for q in range(4)]

    return _symmetric_comm(grads, [jax.ShapeDtypeStruct((4,) + g.shape[1:], g.dtype) for g in grads], {}, 4, copies)


def _rs_to_chips(parts):
    def copies(in_refs, out_refs, send_sems, recv_sems):
        x, y, c = lax.axis_index("x"), lax.axis_index("y"), lax.axis_index("c")
        chips = [(1 - x, y), (x, 1 - y), (1 - x, 1 - y)]
        return [_remote(in_refs[a].at[2 * cx + cy], out_refs[a].at[k], send_sems.at[3 * a + k], recv_sems.at[3 * a + k], (cx, cy, c))
                for a in range(len(parts)) for k, (cx, cy) in enumerate(chips)]

    return _symmetric_comm(parts, [jax.ShapeDtypeStruct((3,) + p.shape[1:], p.dtype) for p in parts], {}, 3, copies)


def _comm_call(name, comm):
    n_in, n_out = len(comm.inputs), len(comm.out_shape)

    def body(*refs):
        ins, outs, sems = refs[:n_in], refs[n_in:n_in + n_out], refs[n_in + n_out:]
        comm.start(ins, outs, sems)
        comm.finish(ins, outs, sems)

    return pl.pallas_call(
        body, name=name, out_shape=list(comm.out_shape), in_specs=_hbm_specs(n_in), out_specs=_hbm_specs(n_out),
        scratch_shapes=list(comm.scratch), input_output_aliases=dict(comm.aliases),
    )(*comm.inputs)


def _small_all_reduce(parts, loss8):
    n = len(parts)

    def body(*refs):
        p_refs, loss_ref, out_ref, pack, land, send_sems, recv_sems = refs[:n], *refs[n:]
        x, y, c = lax.axis_index("x"), lax.axis_index("y"), lax.axis_index("c")
        me = 4 * x + 2 * y + c
        pack[...] = jnp.zeros_like(pack)
        for r, ref in enumerate(p_refs):
            pack[r:r + 1, 0:ref.shape[1]] = ref[...]
        pack[LOSS_ROW:LOSS_ROW + 1, 0:LANES] = loss_ref[0:1, :]
        land[me] = pack[...]
        cps = []
        for k in range(1, N_DEV):
            peer = (x ^ (k >> 2), y ^ ((k >> 1) & 1), c ^ (k & 1))
            cps.append(_remote(pack, land.at[me], send_sems.at[k - 1], recv_sems.at[k - 1], peer))
        for cp in cps:
            cp.start()
        for cp in cps:
            cp.wait()
        acc = land[0]
        for d in range(1, N_DEV):
            acc = acc + land[d]
        out_ref[...] = acc

    vmem = pl.BlockSpec(memory_space=pltpu.VMEM)
    return pl.pallas_call(
        body, name="small_all_reduce",
        out_shape=jax.ShapeDtypeStruct((SMALL_ROWS, D_MODEL), F32),
        in_specs=[vmem] * (n + 1), out_specs=vmem,
        scratch_shapes=[pltpu.VMEM((SMALL_ROWS, D_MODEL), F32), pltpu.VMEM((N_DEV, SMALL_ROWS, D_MODEL), F32),
                        pltpu.SemaphoreType.DMA((N_DEV - 1,)), pltpu.SemaphoreType.DMA((N_DEV - 1,))],
    )(*parts, loss8)


def _rs_sibling_sum(name, grad, got, place):
    _, rows, cols = grad.shape

    def body(place_ref, g_ref, l_ref, own_ref, b_ref):
        s = g_ref[...] + l_ref[...]
        b_ref[...] = s.astype(BF16)

        @pl.when(pl.program_id(0) == place_ref[1])
        def _():
            own_ref[...] = s

    by_chip = pl.BlockSpec((None, rows, cols), lambda q, place_ref: (q, 0, 0))
    return pl.pallas_call(
        body, name=name,
        grid_spec=pltpu.PrefetchScalarGridSpec(
            num_scalar_prefetch=1, grid=(4,),
            in_specs=[pl.BlockSpec((None, rows, cols), lambda q, place_ref: (2 * q + place_ref[0], 0, 0)), by_chip],
            out_specs=[pl.BlockSpec((rows, cols), lambda q, place_ref: (0, 0)), by_chip]),
        out_shape=(jax.ShapeDtypeStruct((rows, cols), F32), jax.ShapeDtypeStruct((4, rows, cols), BF16)),
        compiler_params=_cparams("arbitrary"),
    )(place, grad, got)


def _adamw_math(w, g, m, v):
    m2 = ADAM_B1 * m + (1.0 - ADAM_B1) * g
    v2 = ADAM_B2 * v + (1.0 - ADAM_B2) * (g * g)
    m_hat = m2 / (1.0 - ADAM_B1 ** ADAM_STEP)
    v_hat = v2 / (1.0 - ADAM_B2 ** ADAM_STEP)
    delta = -ADAM_LR * (m_hat / (jnp.sqrt(v_hat) + ADAM_EPS) + ADAM_WD * w)
    return delta, m2, v2


def _update_tile(rows):
    return 256 if rows % 256 == 0 else rows


def _rs_final_sum(name, own, got):
    def body(o_ref, r_ref, g_out):
        g = o_ref[...]
        for k in range(3):
            g = g + r_ref[k].astype(F32)
        g_out[...] = g

    return pl.pallas_call(body, name=name, out_shape=jax.ShapeDtypeStruct(own.shape, F32))(own, got)


def _adamw_sharded(name, w, m, v, own, got):
    _, rows, cols = w.shape
    tr = _update_tile(rows)

    def body(o_ref, r_ref, w_ref, m_ref, v_ref, g_out, d_out, m_out, v_out):
        g = o_ref[:, 0:cols]
        for k in range(3):
            g = g + r_ref[k, :, 0:cols].astype(F32)
        d, m2, v2 = _adamw_math(w_ref[0], g, m_ref[0], v_ref[0])
        g_out[0] = g
        d_out[0] = d
        m_out[0] = m2
        v_out[0] = v2

    mine = pl.BlockSpec((1, tr, cols), lambda i: (0, i, 0))
    shp = jax.ShapeDtypeStruct(w.shape, F32)
    wide = own.shape[1]
    return pl.pallas_call(
        body, name=name, grid=(rows // tr,), out_shape=(shp,) * 4,
        in_specs=[pl.BlockSpec((tr, wide), lambda i: (i, 0)), pl.BlockSpec((3, tr, wide), lambda i: (0, i, 0)),
                  mine, mine, mine],
        out_specs=[mine] * 4,
        compiler_params=_cparams("parallel"),
    )(own, got, w, m, v)


def _adamw_given(name, g, w, m, v):
    _, rows, cols = w.shape
    tr = _update_tile(rows)

    def body(g_ref, w_ref, m_ref, v_ref, g_out, d_out, m_out, v_out):
        g = g_ref[...]
        d, m2, v2 = _adamw_math(w_ref[0], g, m_ref[0], v_ref[0])
        g_out[0] = g
        d_out[0] = d
        m_out[0] = m2
        v_out[0] = v2

    own = pl.BlockSpec((1, tr, cols), lambda i: (0, i, 0))
    shp = jax.ShapeDtypeStruct(w.shape, F32)
    return pl.pallas_call(
        body, name=name, grid=(rows // tr,), out_shape=(shp,) * 4,
        in_specs=[pl.BlockSpec((tr, cols), lambda i: (i, 0)), own, own, own], out_specs=[own] * 4,
        compiler_params=_cparams("parallel"),
    )(g, w, m, v)


def _adamw_small(total, ws, ms, vs):
    n = len(ws)

    def body(*refs):
        t_ref = refs[0]
        w_refs, m_refs, v_refs = refs[1:1 + n], refs[1 + n:1 + 2 * n], refs[1 + 2 * n:1 + 3 * n]
        outs = refs[1 + 3 * n:]
        outs[0][...] = t_ref[LOSS_ROW:LOSS_ROW + 1, 0:1]
        for r in range(n):
            g = t_ref[r:r + 1, 0:w_refs[r].shape[1]]
            d, m2, v2 = _adamw_math(w_refs[r][...], g, m_refs[r][...], v_refs[r][...])
            for k, val in enumerate((g, d, m2, v2)):
                outs[1 + 4 * r + k][...] = val

    vmem = pl.BlockSpec(memory_space=pltpu.VMEM)
    out_shape = [jax.ShapeDtypeStruct((1, 1), F32)]
    for w in ws:
        out_shape += [jax.ShapeDtypeStruct(w.shape, F32)] * 4
    return pl.pallas_call(
        body, name="adamw_small", out_shape=out_shape,
        in_specs=[vmem] * (1 + 3 * n), out_specs=[vmem] * len(out_shape),
    )(total, *ws, *ms, *vs)


def _rope_tables(pos_col):
    t = pos_col.shape[0]
    inv = (np.float32(ROPE_THETA) ** (-np.arange(0, ROPE, 2, dtype=np.float32) / np.float32(ROPE))).astype(np.float32)
    freq = np.zeros((1, LANES), np.float32)
    freq[0, NOPE:NOPE + ROPE // 2] = inv
    freq[0, NOPE + ROPE // 2:NOPE + ROPE] = inv
    tm = _row_tile(t)

    def body(p_ref, f_ref, c_ref, s_ref):
        ang = p_ref[...].astype(F32) * f_ref[...]
        c_ref[...] = jnp.cos(ang)
        s_ref[...] = jnp.sin(ang)

    shp = jax.ShapeDtypeStruct((t, LANES), F32)
    return pl.pallas_call(
        body, name="rope_tables", grid=(t // tm,), out_shape=(shp, shp),
        in_specs=[pl.BlockSpec((tm, 1), lambda i: (i, 0)), pl.BlockSpec((1, LANES), lambda i: (0, 0))],
        out_specs=(pl.BlockSpec((tm, LANES), lambda i: (i, 0)),) * 2,
        compiler_params=_cparams("parallel"),
    )(pos_col, jnp.asarray(freq))


def _in_proj(x, g, w_qkv, w_rest, comm=None):
    t = x.shape[0]
    tm = _row_tile(t)

    def main(ins, outs, scr):
        x_ref, g_ref, wq_ref, wr_ref = ins
        h_ref, fq_ref, fk_ref, fv_ref, r_ref = outs
        h, _ = _rms(x_ref[...], g_ref[...])
        hb = h.astype(BF16)
        h_ref[...] = hb
        for n, (ref, factor) in enumerate(((fq_ref, FOX_Q_FACTOR), (fk_ref, None), (fv_ref, None))):
            part = _dot_nt(hb, wq_ref[n * FOX_WIDTH:(n + 1) * FOX_WIDTH, :])
            ref[...] = (part if factor is None else part * factor).astype(BF16)
        r_ref[...] = _dot_nt(hb, wr_ref[...])

    row = lambda n: pl.BlockSpec((tm, n), lambda i: (i, 0))
    full = lambda a: pl.BlockSpec(a.shape, lambda i: (0,) * a.ndim)
    return _hosted_call(
        "in_proj", main, (t // tm,), [x, g, w_qkv, w_rest], [row(D_MODEL), full(g), full(w_qkv), full(w_rest)],
        (jax.ShapeDtypeStruct((t, D_MODEL), BF16),) + (jax.ShapeDtypeStruct((t, FOX_WIDTH), BF16),) * 3
        + (jax.ShapeDtypeStruct((t, REST_COLS), F32),),
        (row(D_MODEL), row(FOX_WIDTH), row(FOX_WIDTH), row(FOX_WIDTH), row(REST_COLS)), [], comm)


def _log_sigmoid(z):
    return jnp.minimum(z, 0.0) - jnp.log(1.0 + jnp.exp(-jnp.abs(z)))


def _split3(v):
    hi = v.astype(BF16)
    r1 = v - hi.astype(F32)
    mid = r1.astype(BF16)
    lo = (r1 - mid.astype(F32)).astype(BF16)
    return hi, mid, lo


def _scan_tile(t):
    return 512 if t >= 2048 else (256 if t >= 256 else t)


def _forget_cumsum(rest, b128):
    t = rest.shape[0]
    tb = _scan_tile(t)

    def body(r_ref, b_ref, row_ref, rep_ref, f_sc, carry):
        @pl.when(pl.program_id(0) == 0)
        def _():
            carry[...] = jnp.zeros_like(carry)
        lf = _log_sigmoid(r_ref[...] + b_ref[...])
        tri = (lax.broadcasted_iota(jnp.int32, (tb, tb), 0) >= lax.broadcasted_iota(jnp.int32, (tb, tb), 1)).astype(BF16)
        hi, mid, lo = _split3(lf)
        f_sc[...] = (_dot(tri, hi) + _dot(tri, mid)) + _dot(tri, lo) + carry[...]
        carry[...] = f_sc[tb - 1:tb, :]
        f2 = f_sc[...] * LOG2E
        row_ref[...] = jnp.transpose(f2)[0:HEADS, :]
        lane = _lane()
        for h in range(HEADS):
            col = jnp.sum(jnp.where(lane == h, f2, 0.0), axis=1, keepdims=True)
            rep_ref[h] = jnp.broadcast_to(col, (tb, LANES))

    return pl.pallas_call(
        body, name="forget_cumsum", grid=(t // tb,),
        out_shape=(jax.ShapeDtypeStruct((HEADS, t), F32), jax.ShapeDtypeStruct((HEADS, t, LANES), F32)),
        in_specs=[pl.BlockSpec((tb, LANES), lambda i: (i, 0)), pl.BlockSpec((1, LANES), lambda i: (0, 0))],
        out_specs=(pl.BlockSpec((HEADS, tb), lambda i: (0, i)), pl.BlockSpec((HEADS, tb, LANES), lambda i: (0, i, 0))),
        scratch_shapes=[pltpu.VMEM((tb, LANES), F32), pltpu.VMEM((1, LANES), F32)],
        compiler_params=_cparams("arbitrary"),
    )(rest, b128)


def _forget_bwd(rest, b128, d_fq, d_fk):
    t = rest.shape[0]
    tb = _scan_tile(t)
    nb = t // tb

    def body(r_ref, b_ref, dfq_ref, dfk_ref, dz_ref, db_ref, carry):
        @pl.when(pl.program_id(0) == 0)
        def _():
            carry[...] = jnp.zeros_like(carry)
            db_ref[...] = jnp.zeros_like(db_ref)
        tri = (lax.broadcasted_iota(jnp.int32, (tb, tb), 0) <= lax.broadcasted_iota(jnp.int32, (tb, tb), 1)).astype(BF16)
        lane = _lane()
        df = jnp.zeros((tb, LANES), F32)
        for h in range(HEADS):
            df = df + jnp.where(lane == h, dfq_ref[h] + dfk_ref[h], 0.0)
        hi, mid, lo = _split3(df)
        dlf = (_dot(tri, hi) + _dot(tri, mid)) + _dot(tri, lo) + carry[...]
        z = r_ref[...] + b_ref[...]
        dz = dlf / (1.0 + jnp.exp(z))
        dz_ref[...] = dz
        db_ref[...] += jnp.sum(dz, axis=0, keepdims=True)
        carry[...] = carry[...] + jnp.sum(df, axis=0, keepdims=True)

    rev = lambda i: (nb - 1 - i, 0)
    rev3 = pl.BlockSpec((HEADS, tb, LANES), lambda i: (0, nb - 1 - i, 0))
    return pl.pallas_call(
        body, name="forget_bwd", grid=(nb,),
        out_shape=(jax.ShapeDtypeStruct((t, LANES), F32), jax.ShapeDtypeStruct((1, LANES), F32)),
        in_specs=[pl.BlockSpec((tb, LANES), rev), pl.BlockSpec((1, LANES), lambda i: (0, 0)), rev3, rev3],
        out_specs=(pl.BlockSpec((tb, LANES), rev), pl.BlockSpec((1, LANES), lambda i: (0, 0))),
        scratch_shapes=[pltpu.VMEM((1, LANES), F32)],
        compiler_params=_cparams("arbitrary"),
    )(rest, b128, d_fq, d_fk)


def _mla_prep(rest, gq, gkv, wq, wkv, cos, sin):
    t = rest.shape[0]
    tm = _row_tile(t)

    def body(r_ref, gq_ref, gkv_ref, wq_ref, wkv_ref, c_ref, s_ref, q_ref, k_ref, kv_ref, cq_ref, ckv_ref):
        cos_, sin_ = c_ref[...], s_ref[...]
        cq, _ = _rms(r_ref[:, REST_CQ:REST_CKV], gq_ref[...])
        ckv, _ = _rms(r_ref[:, REST_CKV:REST_KR], gkv_ref[...])
        cqb, ckvb = cq.astype(BF16), ckv.astype(BF16)
        cq_ref[...] = cqb
        ckv_ref[...] = ckvb
        k_rope = _rope(r_ref[:, REST_KR:REST_COLS], cos_, sin_)
        lo = _lane() < NOPE
        for h in range(HEADS):
            q_ref[h] = (_rope(_dot(cqb, wq_ref[h]), cos_, sin_) * MLA_Q_FACTOR).astype(BF16)
            kv = _dot(ckvb, wkv_ref[h])
            kv_ref[h] = kv.astype(BF16)
            k_ref[h] = (jnp.where(lo, kv, 0.0) + k_rope).astype(BF16)

    row = lambda n: pl.BlockSpec((tm, n), lambda i: (i, 0))
    full = lambda a: pl.BlockSpec(a.shape, lambda i: (0,) * a.ndim)
    heads = pl.BlockSpec((HEADS, tm, LANES), lambda i: (0, i, 0))
    hshape = jax.ShapeDtypeStruct((HEADS, t, LANES), BF16)
    return pl.pallas_call(
        body, name="mla_prep", grid=(t // tm,),
        out_shape=(hshape, hshape, hshape, jax.ShapeDtypeStruct((t, Q_RANK), BF16), jax.ShapeDtypeStruct((t, KV_RANK), BF16)),
        in_specs=[row(REST_COLS), full(gq), full(gkv), full(wq), full(wkv), row(LANES), row(LANES)],
        out_specs=(heads, heads, heads, row(Q_RANK), row(KV_RANK)),
        compiler_params=_cparams("parallel"),
    )(rest, gq, gkv, wq, wkv, cos, sin)


def _pair_specs(fox, t, tq, blocked_q):
    if fox:
        blk = pl.BlockSpec((tq, LANES), lambda p, i: (i, p))
        whole = pl.BlockSpec((t, LANES), lambda p, i: (0, p))
    else:
        blk = pl.BlockSpec((2, tq, LANES), lambda p, i: (p, i, 0))
        whole = pl.BlockSpec((2, t, LANES), lambda p, i: (p, 0, 0))
    return [blk, whole, whole] if blocked_q else [whole, blk, blk]


def _tile_lanes(x, n):
    return jnp.tile(x, (1, n)) if n > 1 else x


class _Comm(NamedTuple):
    inputs: tuple
    out_shape: tuple
    aliases: dict
    scratch: tuple
    start: Callable
    finish: Callable


def _hosted_call(name, main, grid, args, in_specs, out_shape, out_specs, scratch, comm):
    n_in, n_out, n_scr = len(args), len(out_shape), len(scratch)
    c_in = list(comm.inputs) if comm else []
    c_out = list(comm.out_shape) if comm else []

    def at_step(which):
        hit = pl.program_id(0) == which[0]
        for axis in range(1, len(grid)):
            hit = jnp.logical_and(hit, pl.program_id(axis) == which[axis])
        return hit

    def body(*refs):
        bounds = [0, n_in, len(c_in), n_out, len(c_out), n_scr]
        starts = [sum(bounds[:k + 1]) for k in range(len(bounds))]
        ins, cins, outs, couts, scr = [refs[a:b] for a, b in zip(starts[:-1], starts[1:])]
        sems = refs[starts[-1]:]
        if comm:
            @pl.when(at_step([0] * len(grid)))
            def _():
                comm.start(cins, couts, sems)
        main(ins, outs, scr)
        if comm:
            @pl.when(at_step([n - 1 for n in grid]))
            def _():
                comm.finish(cins, couts, sems)

    res = pl.pallas_call(
        body, name=name, grid=grid,
        out_shape=list(out_shape) + c_out,
        in_specs=list(in_specs) + _hbm_specs(len(c_in)),
        out_specs=list(out_specs) + _hbm_specs(len(c_out)),
        scratch_shapes=list(scratch) + (list(comm.scratch) if comm else []),
        input_output_aliases={n_in + i: n_out + o for i, o in comm.aliases.items()} if comm else {},
        compiler_params=_cparams(*(["arbitrary"] * len(grid))),
    )(*args, *c_in)
    return res[:n_out], res[n_out:]


def _stat_rows(x):
    return jnp.transpose(x)[0:8, :]


FWD_HEADS = 4


def _attn_fwd(fox, q, k, v, f2_rows=None, comm=None):
    t = q.shape[0] if fox else q.shape[1]
    tq = _row_tile(t)
    nq = t // tq
    nh = FWD_HEADS
    wide = (nh // 2) * LANES

    def main(ins, outs, scr):
        q_ref, k_ref, v_ref = ins[:3]
        fr_ref = ins[3] if fox else None
        o_ref, lset_ref = outs
        m_sc, acc_sc = scr
        i = pl.program_id(1)
        lo = _lane() < HEAD_DIM
        hi = jnp.logical_not(lo)
        zero, one = jnp.zeros((), BF16), jnp.ones((), BF16)
        lanes_of = lambda h: slice((h // 2) * LANES, (h // 2 + 1) * LANES)
        if fox:
            qs = [jnp.where(lo if h % 2 == 0 else hi, q_ref[:, lanes_of(h)], zero) for h in range(nh)]
            sum_lanes = [hi if h % 2 == 0 else lo for h in range(nh)]
        else:
            qs = [q_ref[h] for h in range(nh)]
            sum_lanes = [lo] * nh
        m_sc[...] = jnp.full_like(m_sc, -jnp.inf)
        acc_sc[...] = jnp.zeros_like(acc_sc)

        def block(j, r0, nr, c0, nc, seen_from):
            rows = slice(r0, r0 + nr)
            sl = pl.ds(pl.multiple_of(j * tq + c0, math.gcd(tq, c0) if c0 else tq), nc)
            if seen_from is not None:
                seen = (lax.broadcasted_iota(jnp.int32, (nr, nc), 1)
                        <= lax.broadcasted_iota(jnp.int32, (nr, nc), 0) + seen_from)
            for h in range(nh):
                kj, vj = (k_ref[sl, lanes_of(h)], v_ref[sl, lanes_of(h)]) if fox else (k_ref[h, sl, :], v_ref[h, sl, :])
                s = _dot_nt(qs[h][rows], kj)
                if fox:
                    s = s - fr_ref[h, j, :, c0:c0 + nc]
                if seen_from is not None:
                    s = jnp.where(seen, s, -jnp.inf)
                m_prev = m_sc[h, rows]
                m_new = jnp.maximum(m_prev, jnp.max(s, axis=1, keepdims=True))
                p = jnp.exp2((s - _tile_lanes(m_new, nc // LANES)).astype(BF16))
                vj = jnp.where(sum_lanes[h], one, vj)
                acc_sc[h, rows] = jnp.exp2(m_prev - m_new) * acc_sc[h, rows] + _dot(p, vj)
                m_sc[h, rows] = m_new

        def loop_body(j, carry):
            block(j, 0, tq, 0, tq, None)
            return carry

        lax.fori_loop(0, i, loop_body, 0)
        half = tq // 2
        if half % LANES == 0:
            block(i, 0, half, 0, half, 0)
            block(i, half, half, 0, tq, half)
        else:
            block(i, 0, tq, 0, tq, 0)
        res = []
        for h in range(nh):
            acc = acc_sc[h]
            swapped = pltpu.roll(acc, HEAD_DIM, 1)
            res.append(acc / swapped)
            lse2 = m_sc[h] + jnp.log(jnp.where(sum_lanes[h], acc, swapped)) * LOG2E
            lset_ref[h, 0] = _stat_rows(lse2)
        for pr in range(nh // 2):
            even = res[2 * pr] if fox else pltpu.roll(res[2 * pr], HEAD_DIM, 1)
            o_ref[:, pr * LANES:(pr + 1) * LANES] = jnp.where(lo, even, res[2 * pr + 1])

    if fox:
        in_specs = [pl.BlockSpec((tq, wide), lambda g, i: (i, g))] + [pl.BlockSpec((t, wide), lambda g, i: (0, g))] * 2
        in_specs += [pl.BlockSpec((nh, nq, 1, tq), lambda g, i: (g, 0, 0, 0))]
        args = [q, k, v, f2_rows]
    else:
        in_specs = [pl.BlockSpec((nh, tq, LANES), lambda g, i: (g, i, 0))] + [pl.BlockSpec((nh, t, LANES), lambda g, i: (g, 0, 0))] * 2
        args = [q, k, v]
    return _hosted_call(
        "fox_attn_fwd" if fox else "mla_attn_fwd", main, (HEADS // nh, nq), args, in_specs,
        (jax.ShapeDtypeStruct((t, 4 * LANES), F32), jax.ShapeDtypeStruct((HEADS, nq, 8, tq), F32)),
        (pl.BlockSpec((tq, wide), lambda g, i: (i, g)), pl.BlockSpec((nh, 1, 8, tq), lambda g, i: (g, i, 0, 0))),
        [pltpu.VMEM((nh, tq, LANES), F32), pltpu.VMEM((nh, tq, LANES), F32)], comm)


def _head_do(fox, hh, do2, lo):
    if fox:
        return jnp.where(lo if hh == 0 else jnp.logical_not(lo), do2, 0.0)
    return jnp.where(lo, 0.0, pltpu.roll(do2, HEAD_DIM, 1) if hh == 0 else do2)


def _attn_bwd(fox, q, k, v, do, lse_rows, delta_rows, f2_rep=None, comm=None):
    t = q.shape[0] if fox else q.shape[1]
    tq = _row_tile(t)
    nq = t // tq
    scale = FOX_SCALE if fox else MLA_SCALE

    def main(ins, outs, scr):
        if fox:
            q_ref, k_ref, v_ref, f_ref, do_ref, lse_ref, dl_ref = ins
            dq_ref, dk_ref, dv_ref, dfq_ref, dfk_ref = outs
        else:
            q_ref, k_ref, v_ref, do_ref, lse_ref, dl_ref = ins
            dq_ref, dkv_ref, dkr_ref = outs
        dq_sc, dk_sc, dv_sc = scr
        j = pl.program_id(1)
        lane = _lane()
        lo = lane < HEAD_DIM
        hi = jnp.logical_not(lo)
        zero, one = jnp.zeros((), BF16), jnp.ones((), BF16)

        @pl.when(j == 0)
        def _():
            dq_sc[...] = jnp.zeros_like(dq_sc)

        dk_sc[...] = jnp.zeros_like(dk_sc)
        dv_sc[...] = jnp.zeros_like(dv_sc)

        def block(i, r0, nr, c0, nc, masked):
            rows, cols = slice(r0, r0 + nr), slice(c0, c0 + nc)
            sl = pl.ds(pl.multiple_of(i * tq + c0, math.gcd(tq, c0) if c0 else tq), nc)
            do_i = do_ref[sl, :]
            if masked:
                seen = lax.broadcasted_iota(jnp.int32, (nr, nc), 1) >= lax.broadcasted_iota(jnp.int32, (nr, nc), 0)
            for hh in range(2):
                kj = k_ref[rows, :] if fox else k_ref[hh, rows, :]
                vj = v_ref[rows, :] if fox else v_ref[hh, rows, :]
                qi = jnp.where(lo if hh == 0 else hi, q_ref[sl, :], zero) if fox else q_ref[hh, sl, :]
                dob = _head_do(fox, hh, do_i, lo).astype(BF16)
                st = _dot_nt(kj, qi)
                if fox:
                    st = st - _tile_lanes(f_ref[hh, rows, :], nc // LANES)
                if masked:
                    st = jnp.where(seen, st, -jnp.inf)
                pt = jnp.exp2(st - lse_ref[hh, i, 0:1, cols])
                dpt = _dot_nt(vj, dob)
                dst = (pt * (dpt - dl_ref[hh, i, 0:1, cols])).astype(BF16)
                dv_sc[hh, rows] += _dot(pt.astype(BF16), dob)
                if fox:
                    other = hi if hh == 0 else lo
                    qi = jnp.where(other, one, qi)
                    kj = jnp.where(other, one, kj)
                dk_sc[hh, rows] += _dot(dst, qi)
                dq_sc[hh, sl, :] += _dot_tn(dst, kj)

        def loop_body(i, carry):
            block(i, 0, tq, 0, tq, False)
            return carry

        half = tq // 2
        if half % LANES == 0:
            block(j, 0, half, 0, tq, True)
            block(j, half, half, half, half, True)
        else:
            block(j, 0, tq, 0, tq, True)
        lax.fori_loop(j + 1, nq, loop_body, 0)
        if fox:
            dk_ref[...] = (jnp.where(lo, dk_sc[0], dk_sc[1]) * LN2).astype(BF16)
            dv_ref[...] = (dv_sc[0] + dv_sc[1]).astype(BF16)
            for hh in range(2):
                dk = dk_sc[hh]
                dfk_ref[hh] = -jnp.where(hi if hh == 0 else lo, dk, pltpu.roll(dk, HEAD_DIM, 1))
        else:
            rope_lanes = jnp.logical_and(lane >= NOPE, lane < NOPE + ROPE)
            dkr = jnp.zeros((tq, LANES), F32)
            for hh in range(2):
                dk = dk_sc[hh] * LN2
                dkv_ref[hh] = jnp.where(lo, dk, dv_sc[hh])
                dkr = dkr + jnp.where(rope_lanes, dk, 0.0)
            dkr_ref[0] = dkr

        @pl.when(j == nq - 1)
        def _():
            for i in range(nq):
                rows = slice(i * tq, (i + 1) * tq)
                if fox:
                    dq_ref[rows, :] = (jnp.where(lo, dq_sc[0, rows, :], dq_sc[1, rows, :]) * scale).astype(BF16)
                    for hh in range(2):
                        acc = dq_sc[hh, rows, :]
                        dfq_ref[hh, rows, :] = jnp.where(hi if hh == 0 else lo, acc, pltpu.roll(acc, HEAD_DIM, 1))
                else:
                    for hh in range(2):
                        dq_ref[hh, rows, :] = dq_sc[hh, rows, :] * scale

    stat = pl.BlockSpec((2, tq, LANES), lambda p, j: (p, j, 0))
    stat_all = pl.BlockSpec((2, t, LANES), lambda p, j: (p, 0, 0))
    rows4 = pl.BlockSpec((2, nq, 8, tq), lambda p, j: (p, 0, 0, 0))
    pair = pl.BlockSpec((tq, LANES), lambda p, j: (j, p))
    pair_all = pl.BlockSpec((t, LANES), lambda p, j: (0, p))
    in_specs = _pair_specs(fox, t, tq, False)
    args = [q, k, v]
    if fox:
        in_specs += [stat]
        args += [f2_rep]
    in_specs += [pair_all, rows4, rows4]
    args += [do, lse_rows, delta_rows]
    heads_f32 = jax.ShapeDtypeStruct((HEADS, t, LANES), F32)
    if fox:
        wide = jax.ShapeDtypeStruct((t, 4 * LANES), BF16)
        out_shape = (wide, wide, wide, heads_f32, heads_f32)
        out_specs = (pair_all, pair, pair, stat_all, stat)
    else:
        out_shape = (heads_f32, heads_f32, jax.ShapeDtypeStruct((HEADS // 2, t, LANES), F32))
        out_specs = (stat_all, stat, pl.BlockSpec((1, tq, LANES), lambda p, j: (p, j, 0)))
    acc = pltpu.VMEM((2, tq, LANES), F32)
    return _hosted_call("fox_attn_bwd" if fox else "mla_attn_bwd", main, (HEADS // 2, nq), args, in_specs,
                        out_shape, out_specs, [pltpu.VMEM((2, t, LANES), F32), acc, acc], comm)


def _attn_out(x, fox_o, mla_o, gf, gm, w_o):
    t = x.shape[0]
    tm = _row_tile(t)

    def body(x_ref, f_ref, m_ref, gf_ref, gm_ref, w_ref, x1_ref, mix_ref):
        nf, _ = _rms(f_ref[...], gf_ref[...])
        nm, _ = _rms(m_ref[...], gm_ref[...])
        nfb, nmb = nf.astype(BF16), nm.astype(BF16)
        mix_ref[:, :FOX_WIDTH] = nfb
        mix_ref[:, FOX_WIDTH:] = nmb
        x1_ref[...] = x_ref[...] + _dot(nfb, w_ref[:FOX_WIDTH, :]) + _dot(nmb, w_ref[FOX_WIDTH:, :])

    row = lambda n: pl.BlockSpec((tm, n), lambda i: (i, 0))
    full = lambda a: pl.BlockSpec(a.shape, lambda i: (0,) * a.ndim)
    return pl.pallas_call(
        body, name="attn_out", grid=(t // tm,),
        out_shape=(jax.ShapeDtypeStruct((t, D_MODEL), F32), jax.ShapeDtypeStruct((t, D_MODEL), BF16)),
        in_specs=[row(D_MODEL), row(FOX_WIDTH), row(MLA_WIDTH), full(gf), full(gm), full(w_o)],
        out_specs=(row(D_MODEL), row(D_MODEL)),
        compiler_params=_cparams("parallel"),
    )(x, fox_o, mla_o, gf, gm, w_o)


def _mlp_tile(t):
    return 256 if t >= 2048 else 128


def _resident(a):
    return pl.BlockSpec(a.shape, lambda i: (0,) * a.ndim, pipeline_mode=pl.Buffered(1))


FF_CHUNK = 512


def _mlp_fwd(x1, g_mlp, w_up, w_down, g_fin, target):
    t = x1.shape[0]
    tm = _mlp_tile(t)

    def body(x_ref, g_ref, wu_ref, wd_ref, gf_ref, t_ref, u_ref, h_ref, dx_ref, dxb_ref, loss_ref, dg_ref, a_sc):
        @pl.when(pl.program_id(0) == 0)
        def _():
            loss_ref[...] = jnp.zeros_like(loss_ref)
            dg_ref[...] = jnp.zeros_like(dg_ref)

        x = x_ref[...]
        h, _ = _rms(x, g_ref[...])
        hb = h.astype(BF16)
        h_ref[...] = hb
        for f in range(D_FF // FF_CHUNK):
            sl = slice(f * FF_CHUNK, (f + 1) * FF_CHUNK)
            u = _dot(hb, wu_ref[:, sl])
            u_ref[:, sl] = u
            r = jnp.maximum(u, 0.0)
            a_sc[:, sl] = (r * r).astype(BF16)
        x2 = x + _dot(a_sc[...], wd_ref[...])
        y, r2 = _rms(x2, gf_ref[...])
        err = y - t_ref[...]
        loss_ref[...] += 0.5 * jnp.sum(jnp.mean(err * err, axis=-1, keepdims=True))
        dx, dg = _rms_bwd(x2, gf_ref[...], r2, err * (1.0 / D_MODEL))
        dx_ref[...] = dx
        dxb_ref[...] = dx.astype(BF16)
        dg_ref[...] += dg

    row = lambda n: pl.BlockSpec((tm, n), lambda i: (i, 0))
    vec = pl.BlockSpec((1, D_MODEL), lambda i: (0, 0))
    return pl.pallas_call(
        body, name="mlp_fwd", grid=(t // tm,),
        out_shape=(jax.ShapeDtypeStruct((t, D_FF), F32), jax.ShapeDtypeStruct((t, D_MODEL), BF16),
                   jax.ShapeDtypeStruct((t, D_MODEL), F32), jax.ShapeDtypeStruct((t, D_MODEL), BF16),
                   jax.ShapeDtypeStruct((8, LANES), F32), jax.ShapeDtypeStruct((1, D_MODEL), F32)),
        in_specs=[row(D_MODEL), vec, _resident(w_up), _resident(w_down), vec, row(D_MODEL)],
        out_specs=(row(D_FF), row(D_MODEL), row(D_MODEL), row(D_MODEL), pl.BlockSpec((8, LANES), lambda i: (0, 0)), vec),
        scratch_shapes=[pltpu.VMEM((tm, D_FF), BF16)],
        compiler_params=_cparams("arbitrary"),
    )(x1, g_mlp, w_up, w_down, g_fin, target)


def _mlp_bwd(dx2, u, x1, g_mlp, w_up, w_down):
    t = x1.shape[0]
    tm = _mlp_tile(t)

    def body(dx_ref, u_ref, x_ref, g_ref, wu_ref, wd_ref, du_ref, a_ref, dx1_ref, dx1b_ref, dg_ref):
        @pl.when(pl.program_id(0) == 0)
        def _():
            dg_ref[...] = jnp.zeros_like(dg_ref)

        dx2 = dx_ref[...]
        dxb = dx2.astype(BF16)
        for f in range(D_FF // FF_CHUNK):
            sl = slice(f * FF_CHUNK, (f + 1) * FF_CHUNK)
            r = jnp.maximum(u_ref[:, sl], 0.0)
            a_ref[:, sl] = (r * r).astype(BF16)
            da = _dot_nt(dxb, wd_ref[sl, :])
            du_ref[:, sl] = (da * (2.0 * r)).astype(BF16)
        dh = _dot_nt(du_ref[...], wu_ref[...])
        x = x_ref[...]
        _, r1 = _rms(x, g_ref[...])
        dx, dg = _rms_bwd(x, g_ref[...], r1, dh)
        dx1 = dx2 + dx
        dx1_ref[...] = dx1
        dx1b_ref[...] = dx1.astype(BF16)
        dg_ref[...] += dg

    row = lambda n: pl.BlockSpec((tm, n), lambda i: (i, 0))
    vec = pl.BlockSpec((1, D_MODEL), lambda i: (0, 0))
    return pl.pallas_call(
        body, name="mlp_bwd", grid=(t // tm,),
        out_shape=(jax.ShapeDtypeStruct((t, D_FF), BF16), jax.ShapeDtypeStruct((t, D_FF), BF16),
                   jax.ShapeDtypeStruct((t, D_MODEL), F32), jax.ShapeDtypeStruct((t, D_MODEL), BF16),
                   jax.ShapeDtypeStruct((1, D_MODEL), F32)),
        in_specs=[row(D_MODEL), row(D_FF), row(D_MODEL), vec, _resident(w_up), _resident(w_down)],
        out_specs=(row(D_FF), row(D_FF), row(D_MODEL), row(D_MODEL), vec),
        compiler_params=_cparams("arbitrary"),
    )(dx2, u, x1, g_mlp, w_up, w_down)


def _matmul_tn(name, a, b, blocks=None):
    t, m = a.shape
    n = b.shape[1]
    tk = t if a.dtype == BF16 and b.dtype == BF16 else min(t, 2048)
    steps = t // tk
    bm = m if m <= 1024 else 512
    bn = n if n <= 1024 else 512
    width = bn if blocks is None else n // blocks
    per = bn // width

    def body(a_ref, b_ref, o_ref, acc_sc):
        kk = pl.program_id(2)

        @pl.when(kk == 0)
        def _():
            acc_sc[...] = jnp.zeros_like(acc_sc)

        acc_sc[...] += _dot_tn(a_ref[...].astype(BF16), b_ref[...].astype(BF16))

        @pl.when(kk == steps - 1)
        def _():
            if blocks is None:
                o_ref[...] = acc_sc[...]
            else:
                for s in range(per):
                    o_ref[s] = acc_sc[:, s * width:(s + 1) * width]

    if blocks is None:
        o_spec = pl.BlockSpec((bm, bn), lambda i, j, kk: (i, j))
        o_shape = (m, n)
    else:
        o_spec = pl.BlockSpec((per, bm, width), lambda i, j, kk: (j, i, 0))
        o_shape = (blocks, m, width)
    return pl.pallas_call(
        body, name=name, grid=(m // bm, n // bn, steps),
        out_shape=jax.ShapeDtypeStruct(o_shape, F32),
        in_specs=[pl.BlockSpec((tk, bm), lambda i, j, kk: (kk, i)), pl.BlockSpec((tk, bn), lambda i, j, kk: (kk, j))],
        out_specs=o_spec,
        scratch_shapes=[pltpu.VMEM((bm, bn), F32)],
        compiler_params=_cparams("parallel", "parallel", "arbitrary"),
    )(a, b)


def _dw_in(dfq, dfk, dfv, drest, h1):
    t = h1.shape[0]
    tk = min(t, 1024)
    off_ff = 3 * FOX_WIDTH
    off_cq = off_ff + HEADS
    off_kr = IN_COLS - ROPE

    def body(dq_ref, dk_ref, dv_ref, dr_ref, h_ref, o_ref):
        h = h_ref[...]
        r = _dot_tn(dr_ref[...], h)
        parts = [(slice(n * FOX_WIDTH, (n + 1) * FOX_WIDTH), _dot_tn(ref[...], h)) for n, ref in enumerate((dq_ref, dk_ref, dv_ref))]
        parts += [(slice(off_ff, off_cq), r[0:HEADS]), (slice(off_cq, off_kr), r[REST_CQ:REST_KR]),
                  (slice(off_kr, IN_COLS), r[REST_KR + NOPE:REST_KR + NOPE + ROPE])]

        @pl.when(pl.program_id(0) == 0)
        def _():
            for rows, val in parts:
                o_ref[rows, :] = val

        @pl.when(pl.program_id(0) > 0)
        def _():
            for rows, val in parts:
                o_ref[rows, :] += val

    tok = lambda n: pl.BlockSpec((tk, n), lambda kk: (kk, 0))
    return pl.pallas_call(
        body, name="dw_in", grid=(t // tk,),
        out_shape=jax.ShapeDtypeStruct((IN_COLS, D_MODEL), F32),
        in_specs=[tok(FOX_WIDTH), tok(FOX_WIDTH), tok(FOX_WIDTH), tok(REST_COLS), tok(D_MODEL)],
        out_specs=pl.BlockSpec((IN_COLS, D_MODEL), lambda kk: (0, 0)),
        compiler_params=_cparams("arbitrary"),
    )(dfq, dfk, dfv, drest, h1)


def _attn_out_bwd(dx1, fox_o, mla_o, gf, gm, w_o):
    t = dx1.shape[0]
    tm = _row_tile(t)

    def body(dx_ref, f_ref, m_ref, gf_ref, gm_ref, w_ref, df_ref, dm_ref, dlf_ref, dlm_ref, dgf_ref, dgm_ref):
        @pl.when(pl.program_id(0) == 0)
        def _():
            dgf_ref[...] = jnp.zeros_like(dgf_ref)
            dgm_ref[...] = jnp.zeros_like(dgm_ref)
        dxb = dx_ref[...].astype(BF16)
        lane = lax.broadcasted_iota(jnp.int32, (8, LANES), 1)
        picks = [(lane < HEAD_DIM).astype(BF16), (lane >= HEAD_DIM).astype(BF16)]
        for o_ref, g_ref, lo_row, d_ref, dl_ref, dg_ref in ((f_ref, gf_ref, 0, df_ref, dlf_ref, dgf_ref),
                                                             (m_ref, gm_ref, FOX_WIDTH, dm_ref, dlm_ref, dgm_ref)):
            dn = _dot_nt(dxb, w_ref[lo_row:lo_row + FOX_WIDTH, :])
            o = o_ref[...]
            _, r = _rms(o, g_ref[...])
            d, dg = _rms_bwd(o, g_ref[...], r, dn)
            d_ref[...] = d
            dg_ref[...] += dg
            prod = d * o
            for h in range(HEADS):
                parts = _split3(prod[:, (h // 2) * LANES:(h // 2 + 1) * LANES])
                dl_ref[h, 0] = (_dot_nt(picks[h % 2], parts[0]) + _dot_nt(picks[h % 2], parts[1])) + _dot_nt(picks[h % 2], parts[2])

    row = lambda n: pl.BlockSpec((tm, n), lambda i: (i, 0))
    full = lambda a: pl.BlockSpec(a.shape, lambda i: (0,) * a.ndim)
    vec = pl.BlockSpec((1, FOX_WIDTH), lambda i: (0, 0))
    rows = pl.BlockSpec((HEADS, 1, 8, tm), lambda i: (0, i, 0, 0))
    o_shape = jax.ShapeDtypeStruct((t, FOX_WIDTH), F32)
    g_shape = jax.ShapeDtypeStruct((1, FOX_WIDTH), F32)
    r_shape = jax.ShapeDtypeStruct((HEADS, t // tm, 8, tm), F32)
    return pl.pallas_call(
        body, name="attn_out_bwd", grid=(t // tm,),
        out_shape=(o_shape, o_shape, r_shape, r_shape, g_shape, g_shape),
        in_specs=[row(D_MODEL), row(FOX_WIDTH), row(MLA_WIDTH), full(gf), full(gm), full(w_o)],
        out_specs=(row(FOX_WIDTH), row(MLA_WIDTH), rows, rows, vec, vec),
        compiler_params=_cparams("arbitrary"),
    )(dx1, fox_o, mla_o, gf, gm, w_o)


def _mla_prep_bwd(dq, dkv, dkr, dz, rest, gq, gkv, wq, wkv, cos, sin):
    t = rest.shape[0]
    tm = _row_tile(t)

    def body(dq_ref, dkv_ref, dkr_ref, dz_ref, r_ref, gq_ref, gkv_ref, wq_ref, wkv_ref, c_ref, s_ref,
             dr_ref, dqp_ref, dkvb_ref, dgq_ref, dgkv_ref):
        @pl.when(pl.program_id(0) == 0)
        def _():
            dgq_ref[...] = jnp.zeros_like(dgq_ref)
            dgkv_ref[...] = jnp.zeros_like(dgkv_ref)
        cos_, sin_ = c_ref[...], s_ref[...]
        dcq = jnp.zeros((tm, Q_RANK), F32)
        dckv = jnp.zeros((tm, KV_RANK), F32)
        for h in range(HEADS):
            dqp = _rope_bwd(dq_ref[h], cos_, sin_).astype(BF16)
            dqp_ref[:, h * LANES:(h + 1) * LANES] = dqp
            dcq = dcq + _dot_nt(dqp, wq_ref[h])
            dkvb = dkv_ref[h].astype(BF16)
            dkvb_ref[:, h * LANES:(h + 1) * LANES] = dkvb
            dckv = dckv + _dot_nt(dkvb, wkv_ref[h])
        dkrope = dkr_ref[0]
        for pr in range(1, HEADS // 2):
            dkrope = dkrope + dkr_ref[pr]
        cq = r_ref[:, REST_CQ:REST_CKV]
        _, rq = _rms(cq, gq_ref[...])
        d_cq, dgq = _rms_bwd(cq, gq_ref[...], rq, dcq)
        ckv = r_ref[:, REST_CKV:REST_KR]
        _, rkv = _rms(ckv, gkv_ref[...])
        d_ckv, dgkv = _rms_bwd(ckv, gkv_ref[...], rkv, dckv)
        dgq_ref[...] += dgq
        dgkv_ref[...] += dgkv
        dr_ref[:, 0:REST_CQ] = dz_ref[...].astype(BF16)
        dr_ref[:, REST_CQ:REST_CKV] = d_cq.astype(BF16)
        dr_ref[:, REST_CKV:REST_KR] = d_ckv.astype(BF16)
        dr_ref[:, REST_KR:REST_COLS] = _rope_bwd(dkrope, cos_, sin_).astype(BF16)

    row = lambda n: pl.BlockSpec((tm, n), lambda i: (i, 0))
    full = lambda a: pl.BlockSpec(a.shape, lambda i: (0,) * a.ndim)
    heads = pl.BlockSpec((HEADS, tm, LANES), lambda i: (0, i, 0))
    hshape = jax.ShapeDtypeStruct((t, HEADS * LANES), BF16)
    return pl.pallas_call(
        body, name="mla_prep_bwd", grid=(t // tm,),
        out_shape=(jax.ShapeDtypeStruct((t, REST_COLS), BF16), hshape, hshape,
                   jax.ShapeDtypeStruct((1, Q_RANK), F32), jax.ShapeDtypeStruct((1, KV_RANK), F32)),
        in_specs=[heads, heads, pl.BlockSpec((HEADS // 2, tm, LANES), lambda i: (0, i, 0)), row(LANES), row(REST_COLS),
                  full(gq), full(gkv), full(wq), full(wkv), row(LANES), row(LANES)],
        out_specs=(row(REST_COLS), row(HEADS * LANES), row(HEADS * LANES), pl.BlockSpec((1, Q_RANK), lambda i: (0, 0)),
                   pl.BlockSpec((1, KV_RANK), lambda i: (0, 0))),
        compiler_params=_cparams("arbitrary"),
    )(dq, dkv, dkr, dz, rest, gq, gkv, wq, wkv, cos, sin)


def _in_proj_bwd(x, g, dx1, dfq, dfk, dfv, drest, w_qkv, w_rest, comm=None):
    t = x.shape[0]
    tm = _row_tile(t)

    def main(ins, outs, scr):
        x_ref, g_ref, dx1_ref, dq_ref, dk_ref, dv_ref, dr_ref, wq_ref, wr_ref = ins
        dx_ref, dg_ref = outs

        @pl.when(pl.program_id(0) == 0)
        def _():
            dg_ref[...] = jnp.zeros_like(dg_ref)
        dh = _dot(dr_ref[...], wr_ref[...])
        for n, ref in enumerate((dq_ref, dk_ref, dv_ref)):
            dh = dh + _dot(ref[...], wq_ref[n * FOX_WIDTH:(n + 1) * FOX_WIDTH, :])
        xv = x_ref[...]
        _, r = _rms(xv, g_ref[...])
        dx, dg = _rms_bwd(xv, g_ref[...], r, dh)
        dx_ref[...] = dx1_ref[...] + dx
        dg_ref[...] += dg

    row = lambda n: pl.BlockSpec((tm, n), lambda i: (i, 0))
    full = lambda a: pl.BlockSpec(a.shape, lambda i: (0,) * a.ndim)
    vec = pl.BlockSpec((1, D_MODEL), lambda i: (0, 0))
    return _hosted_call(
        "in_proj_bwd", main, (t // tm,), [x, g, dx1, dfq, dfk, dfv, drest, w_qkv, w_rest],
        [row(D_MODEL), full(g), row(D_MODEL), row(FOX_WIDTH), row(FOX_WIDTH), row(FOX_WIDTH), row(REST_COLS),
         full(w_qkv), full(w_rest)],
        (jax.ShapeDtypeStruct((t, D_MODEL), F32), jax.ShapeDtypeStruct((1, D_MODEL), F32)), (row(D_MODEL), vec), [], comm)


def _pad_cols(a, n):
    return jnp.pad(a, ((0, 0),) * (a.ndim - 1) + ((0, n - a.shape[-1]),))


def kernel(x, positions, attn_norm_g, w_in, b_forget, q_norm_g, w_uq, kv_norm_g, w_ukv, fox_out_g, mla_out_g, w_o, mlp_norm_g, w_up, w_down, final_norm_g, loss_target, m_attn_norm_g, m_w_in, m_b_forget, m_q_norm_g, m_w_uq, m_kv_norm_g, m_w_ukv, m_fox_out_g, m_mla_out_g, m_w_o, m_mlp_norm_g, m_w_up, m_w_down, m_final_norm_g, v_attn_norm_g, v_w_in, v_b_forget, v_q_norm_g, v_w_uq, v_kv_norm_g, v_w_ukv, v_fox_out_g, v_mla_out_g, v_w_o, v_mlp_norm_g, v_w_up, v_w_down, v_final_norm_g):
    t = x.shape[1]
    tq = _row_tile(t)
    xs = x[0]
    target = loss_target[0]

    mid = [_pad_cols(w_uq[0], LANES).astype(BF16), w_ukv[0].astype(BF16)]
    late = [w_o[0].astype(BF16), w_up[0].astype(BF16), w_down[0].astype(BF16)]
    g_in, = _all_gather([jnp.transpose(w_in[0]).astype(BF16)])
    win = g_in.reshape(IN_COLS, D_MODEL)
    off_ff, off_cq, off_kr = 3 * FOX_WIDTH, 3 * FOX_WIDTH + HEADS, IN_COLS - ROPE
    zeros = lambda n: jnp.zeros((n, D_MODEL), BF16)
    w_qkv = win[:off_ff]
    w_rest = jnp.concatenate([
        win[off_ff:off_cq], zeros(REST_CQ - HEADS), win[off_cq:off_kr],
        zeros(NOPE), win[off_kr:], zeros(LANES - NOPE - ROPE)], axis=0)

    cos, sin = _rope_tables(positions.reshape(t, 1))
    (h1, fq, fk, fv, rest), (wq, wkv) = _in_proj(xs, attn_norm_g, w_qkv, w_rest, comm=_ag_to_all(mid))
    b128 = _pad_cols(b_forget, LANES)
    f2_rows, f2_rep = _forget_cumsum(rest, b128)
    f2_rows = f2_rows.reshape(HEADS, t // tq, 1, tq)
    (fox_o, fox_lse_rows), partly = _attn_fwd(True, fq, fk, fv, f2_rows, comm=_ag_direct(late))
    mq, mk, mkv, cqn, ckvn = _mla_prep(rest, q_norm_g, kv_norm_g, wq, wkv, cos, sin)
    (mla_o, mla_lse_rows), (g_o, g_up, g_down) = _attn_fwd(False, mq, mk, mkv, comm=_ag_forward(partly))
    wo = g_o.reshape(D_MODEL, D_MODEL)
    x1, mixed = _attn_out(xs, fox_o, mla_o, fox_out_g, mla_out_g, wo)
    wup = jnp.transpose(g_up, (1, 0, 2)).reshape(D_MODEL, D_FF)
    wdown = g_down.reshape(D_FF, D_MODEL)
    u, h2, dx2, dx2b, loss8, d_gfin = _mlp_fwd(x1, mlp_norm_g, wup, wdown, final_norm_g.reshape(1, D_MODEL), target)

    du, act, dx1, dx1b, d_gmlp = _mlp_bwd(dx2, u, x1, mlp_norm_g, wup, wdown)
    dw_down = _matmul_tn("dw_down", act, dx2b)
    dw_up = _matmul_tn("dw_up", h2, du, blocks=N_DEV)
    dfox_o, dmla_o, fox_delta_rows, mla_delta_rows, d_gfox, d_gmla = _attn_out_bwd(dx1, fox_o, mla_o, fox_out_g, mla_out_g, wo)
    dw_o = _matmul_tn("dw_o", mixed, dx1b)

    place = jnp.stack([lax.axis_index("c"), 2 * lax.axis_index("x") + lax.axis_index("y")]).astype(jnp.int32)
    names = ("w_in", "w_uq", "w_ukv", "w_o", "w_up", "w_down")
    grads_b = [dw_o.reshape(N_DEV, -1, D_MODEL), dw_up, dw_down.reshape(N_DEV, -1, D_MODEL)]
    (dfq, dfk, dfv, d_fq, d_fk), got_b = _attn_bwd(True, fq, fk, fv, dfox_o, fox_lse_rows, fox_delta_rows,
                                                   f2_rep, comm=_rs_to_sibling(grads_b))
    sums_b = [_rs_sibling_sum("rs_sibling_sum_" + nm, g, l, place) for nm, g, l in zip(names[3:], grads_b, got_b)]
    dz, d_b = _forget_bwd(rest, b128, d_fq, d_fk)
    (dmq, dmkv, dmkr), others_b = _attn_bwd(False, mq, mk, mkv, dmla_o, mla_lse_rows, mla_delta_rows,
                                            comm=_rs_to_chips([s[1] for s in sums_b]))
    drest, dqp, dkvb, d_gq, d_gkv = _mla_prep_bwd(dmq, dmkv, dmkr, dz, rest, q_norm_g, kv_norm_g, wq, wkv, cos, sin)
    dw_uq = _matmul_tn("dw_uq", cqn, dqp, blocks=HEADS)
    dw_ukv = _matmul_tn("dw_ukv", ckvn, dkvb, blocks=HEADS)
    dw_in = _dw_in(dfq, dfk, dfv, drest, h1)

    grads_a = [dw_in.reshape(N_DEV, IN_SHARD, D_MODEL), dw_uq, dw_ukv]
    got_a = _comm_call("rs_sibling_exchange", _rs_to_sibling(grads_a))
    sums_a = [_rs_sibling_sum("rs_sibling_sum_" + nm, g, l, place) for nm, g, l in zip(names[:3], grads_a, got_a)]
    (grad_x, d_gattn), others_a = _in_proj_bwd(xs, attn_norm_g, dx1, dfq, dfk, dfv, drest, w_qkv, w_rest,
                                               comm=_rs_to_chips([s[1] for s in sums_a]))
    sums, others = sums_a + sums_b, list(others_a) + list(others_b)
    sharded = (w_in, w_uq, w_ukv, w_o, w_up, w_down)
    moments_m = (m_w_in, m_w_uq, m_w_ukv, m_w_o, m_w_up, m_w_down)
    moments_v = (v_w_in, v_w_uq, v_w_ukv, v_w_o, v_w_up, v_w_down)
    g_in_t = _rs_final_sum("rs_final_sum_w_in", sums[0][0], others[0])
    big = [_adamw_given("adamw_w_in", jnp.transpose(g_in_t), w_in, m_w_in, v_w_in)]
    for a in range(1, len(names)):
        big.append(_adamw_sharded("adamw_" + names[a], sharded[a], moments_m[a], moments_v[a], sums[a][0], others[a]))
    big_g, big_d, big_m, big_v = [[b[k] for b in big] for k in range(4)]

    as_row = lambda a: a.reshape(1, -1)
    small_w = (attn_norm_g, b_forget, q_norm_g, kv_norm_g, fox_out_g, mla_out_g, mlp_norm_g, final_norm_g)
    small_m = (m_attn_norm_g, m_b_forget, m_q_norm_g, m_kv_norm_g, m_fox_out_g, m_mla_out_g, m_mlp_norm_g, m_final_norm_g)
    small_v = (v_attn_norm_g, v_b_forget, v_q_norm_g, v_kv_norm_g, v_fox_out_g, v_mla_out_g, v_mlp_norm_g, v_final_norm_g)
    total = _small_all_reduce([d_gattn, d_b, d_gq, d_gkv, d_gfox, d_gmla, d_gmlp, d_gfin], loss8)
    small = _adamw_small(total, [as_row(a) for a in small_w], [as_row(a) for a in small_m], [as_row(a) for a in small_v])
    loss = small[0].reshape(())
    s_g, s_d, s_m, s_v = [[small[1 + 4 * r + k].reshape(small_w[r].shape) for r in range(len(small_w))] for k in range(4)]

    def ordered(small_, bigs):
        ga, bf, gq_, gkv_, gfo, gml, gmlp_, gfin_ = small_
        bin_, buq, bukv, bo, bup, bdown = bigs
        return [ga, bin_, bf, gq_, buq, gkv_, bukv, gfo, gml, bo, gmlp_, bup, bdown, gfin_]

    return (loss, grad_x[None], *ordered(s_g, big_g), *ordered(s_d, big_d), *ordered(s_m, big_m), *ordered(s_v, big_v))
```

```python
import math
from typing import Callable, NamedTuple

import numpy as np
import jax
import jax.numpy as jnp
from jax import lax
from jax.experimental import pallas as pl
from jax.experimental.pallas import tpu as pltpu

F32 = jnp.float32
BF16 = jnp.bfloat16
MESH = pl.DeviceIdType.MESH

D_MODEL = 1024
HEADS = 8
HEAD_DIM = 64
FOX_WIDTH = 512
MLA_WIDTH = 512
NOPE = 64
ROPE = 32
QK_DIM = 96
Q_RANK = 384
KV_RANK = 256
D_FF = 4096
IN_COLS = 2216
ROPE_THETA = 10000.0
EPS = 1e-6
FOX_SCALE = 1.0 / math.sqrt(HEAD_DIM)
MLA_SCALE = 1.0 / math.sqrt(QK_DIM)
ADAM_LR = 0.001
ADAM_B1 = 0.9
ADAM_B2 = 0.999
ADAM_EPS = 1e-08
ADAM_WD = 0.01
ADAM_STEP = 10

N_DEV = 8
LANES = 128
REST_COLS = 896
REST_CQ = LANES
REST_CKV = REST_CQ + Q_RANK
REST_KR = REST_CKV + KV_RANK
LOG2E = 1.4426950408889634
LN2 = 0.6931471805599453
FOX_Q_FACTOR = FOX_SCALE * LOG2E
MLA_Q_FACTOR = MLA_SCALE * LOG2E
VMEM_LIMIT = 56 * 1024 * 1024

IN_SHARD = IN_COLS // N_DEV
SMALL_SIZES = (1024, 8, 384, 256, 512, 512, 1024, 1024)
SMALL_ROWS = 16
LOSS_ROW = len(SMALL_SIZES)


def _cparams(*sem):
    return pltpu.CompilerParams(dimension_semantics=sem or None, vmem_limit_bytes=VMEM_LIMIT)


def _row_tile(t):
    return 512 if t >= 2048 else (256 if t >= 512 else 128)


def _dot(a, b):
    return jnp.dot(a, b, preferred_element_type=F32)


def _dot_nt(a, b):
    return lax.dot_general(a, b, (((1,), (1,)), ((), ())), preferred_element_type=F32)


def _dot_tn(a, b):
    return lax.dot_general(a, b, (((0,), (0,)), ((), ())), preferred_element_type=F32)


def _rms(x, g):
    r = lax.rsqrt(jnp.mean(x * x, axis=-1, keepdims=True) + EPS)
    return x * r * g, r


def _rms_bwd(x, g, r, dy):
    xh = x * r
    gdy = dy * g
    dx = r * (gdy - xh * jnp.mean(gdy * xh, axis=-1, keepdims=True))
    return dx, jnp.sum(dy * xh, axis=0, keepdims=True)


def _lane():
    return lax.broadcasted_iota(jnp.int32, (1, LANES), 1)


def _rot(x):
    lane = _lane()
    half = NOPE + ROPE // 2
    first = jnp.logical_and(lane >= NOPE, lane < half)
    second = jnp.logical_and(lane >= half, lane < NOPE + ROPE)
    return jnp.where(first, -pltpu.roll(x, LANES - ROPE // 2, 1), jnp.where(second, pltpu.roll(x, ROPE // 2, 1), 0.0))


def _rope(x, cos, sin):
    return x * cos + _rot(x) * sin


def _rope_bwd(dy, cos, sin):
    return dy * cos - _rot(dy * sin)


def _remote(src, dst, send_sem, recv_sem, to):
    return pltpu.make_async_remote_copy(src_ref=src, dst_ref=dst, send_sem=send_sem, recv_sem=recv_sem,
                                        device_id=to, device_id_type=MESH)


def _hbm_specs(n):
    return [pl.BlockSpec(memory_space=pl.ANY)] * n


def _all_gather(blocks):
    n = len(blocks)

    def body(*refs):
        x_refs, out_refs = refs[:n], refs[n:2 * n]
        send_sems, recv_sems, local_sems = refs[2 * n:]
        x, y, c = lax.axis_index("x"), lax.axis_index("y"), lax.axis_index("c")
        me, sibling = (x, y, c), (x, y, 1 - c)
        chips = [(1 - x, y), (x, 1 - y), (1 - x, 1 - y)]

        def slot(a, px, py, pc):
            return out_refs[a].at[4 * px + 2 * py + pc]

        def copy(a, k, blk, to, src=None):
            return _remote(slot(a, *blk) if src is None else src, slot(a, *blk),
                           send_sems.at[7 * a + k], recv_sems.at[7 * a + k], to)

        mine = [pltpu.make_async_copy(x_refs[a], slot(a, *me), local_sems.at[a]) for a in range(n)]
        first, passed = [], []
        for a in range(n):
            mine[a].start()
            first.append(copy(a, 0, me, sibling, src=x_refs[a]))
            first += [copy(a, 1 + j, me, (*chip, c), src=x_refs[a]) for j, chip in enumerate(chips)]
        for cp in first:
            cp.start()
        for a in range(n):
            for j, chip in enumerate(chips):
                copy(a, 1 + j, (*chip, c), me).wait_recv()
                passed.append(copy(a, 4 + j, (*chip, c), sibling))
                passed[-1].start()
        for a in range(n):
            copy(a, 0, sibling, me).wait_recv()
            for j, chip in enumerate(chips):
                copy(a, 4 + j, (*chip, 1 - c), me).wait_recv()
        for cp in first + passed:
            cp.wait_send()
        for cp in mine:
            cp.wait()

    return pl.pallas_call(
        body, name="all_gather_weights",
        out_shape=[jax.ShapeDtypeStruct((N_DEV,) + b.shape, b.dtype) for b in blocks],
        in_specs=_hbm_specs(n), out_specs=_hbm_specs(n),
        scratch_shapes=[pltpu.SemaphoreType.DMA((7 * n,)), pltpu.SemaphoreType.DMA((7 * n,)), pltpu.SemaphoreType.DMA((n,))],
    )(*blocks)


def _symmetric_comm(inputs, out_shape, aliases, per_array, copies):
    def start(in_refs, out_refs, sems):
        for cp in copies(in_refs, out_refs, *sems):
            cp.start()

    def finish(in_refs, out_refs, sems):
        for cp in copies(in_refs, out_refs, *sems):
            cp.wait()

    n_sems = per_array * len(inputs)
    return _Comm(tuple(inputs), tuple(out_shape), aliases,
                 (pltpu.SemaphoreType.DMA((n_sems,)), pltpu.SemaphoreType.DMA((n_sems,))), start, finish)


def _ag_direct(shards):
    def copies(in_refs, out_refs, send_sems, recv_sems):
        x, y, c = lax.axis_index("x"), lax.axis_index("y"), lax.axis_index("c")
        peers = [(x, y, 1 - c), (1 - x, y, c), (x, 1 - y, c), (1 - x, 1 - y, c)]
        cps = []
        for a in range(len(shards)):
            mine = out_refs[a].at[4 * x + 2 * y + c]
            cps.append(pltpu.make_async_copy(in_refs[a], mine, send_sems.at[5 * a]))
            cps += [_remote(in_refs[a], mine, send_sems.at[5 * a + k], recv_sems.at[5 * a + k], peer)
                    for k, peer in enumerate(peers, start=1)]
        return cps

    return _symmetric_comm(shards, [jax.ShapeDtypeStruct((N_DEV,) + s.shape, s.dtype) for s in shards], {}, 5, copies)


def _ag_to_all(shards):
    def copies(in_refs, out_refs, send_sems, recv_sems):
        x, y, c = lax.axis_index("x"), lax.axis_index("y"), lax.axis_index("c")
        cps = []
        for a in range(len(shards)):
            mine = out_refs[a].at[4 * x + 2 * y + c]
            cps.append(pltpu.make_async_copy(in_refs[a], mine, send_sems.at[N_DEV * a]))
            for k in range(1, N_DEV):
                peer = (x ^ (k >> 2), y ^ ((k >> 1) & 1), c ^ (k & 1))
                cps.append(_remote(in_refs[a], mine, send_sems.at[N_DEV * a + k], recv_sems.at[N_DEV * a + k], peer))
        return cps

    return _symmetric_comm(shards, [jax.ShapeDtypeStruct((N_DEV,) + s.shape, s.dtype) for s in shards], {}, N_DEV, copies)


def _ag_forward(gathered):
    def copies(in_refs, out_refs, send_sems, recv_sems):
        x, y, c = lax.axis_index("x"), lax.axis_index("y"), lax.axis_index("c")
        chips = [(1 - x, y), (x, 1 - y), (1 - x, 1 - y)]
        return [_remote(out_refs[a].at[4 * cx + 2 * cy + c], out_refs[a].at[4 * cx + 2 * cy + c],
                        send_sems.at[3 * a + j], recv_sems.at[3 * a + j], (x, y, 1 - c))
                for a in range(len(gathered)) for j, (cx, cy) in enumerate(chips)]

    shapes = [jax.ShapeDtypeStruct(g.shape, g.dtype) for g in gathered]
    return _symmetric_comm(gathered, shapes, {a: a for a in range(len(gathered))}, 3, copies)


def _rs_to_sibling(grads):
    def copies(in_refs, out_refs, send_sems, recv_sems):
        x, y, c = lax.axis_index("x"), lax.axis_index("y"), lax.axis_index("c")
        return [_remote(in_refs[a].at[2 * q + 1 - c], out_refs[a].at[q], send_sems.at[4 * a + q], recv_sems.at[4 * a + q], (x, y, 1 - c))
                for a in range(len(grads)) for q in range(4)]

    return _symmetric_comm(grads, [jax.ShapeDtypeStruct((4,) + g.shape[1:], g.dtype) for g in grads], {}, 4, copies)


def _rs_to_chips(parts):
    def copies(in_refs, out_refs, send_sems, recv_sems):
        x, y, c = lax.axis_index("x"), lax.axis_index("y"), lax.axis_index("c")
        chips = [(1 - x, y), (x, 1 - y), (1 - x, 1 - y)]
        return [_remote(in_refs[a].at[2 * cx + cy], out_refs[a].at[k], send_sems.at[3 * a + k], recv_sems.at[3 * a + k], (cx, cy, c))
                for a in range(len(parts)) for k, (cx, cy) in enumerate(chips)]

    return _symmetric_comm(parts, [jax.ShapeDtypeStruct((3,) + p.shape[1:], p.dtype) for p in parts], {}, 3, copies)


def _comm_call(name, comm):
    n_in, n_out = len(comm.inputs), len(comm.out_shape)

    def body(*refs):
        ins, outs, sems = refs[:n_in], refs[n_in:n_in + n_out], refs[n_in + n_out:]
        comm.start(ins, outs, sems)
        comm.finish(ins, outs, sems)

    return pl.pallas_call(
        body, name=name, out_shape=list(comm.out_shape), in_specs=_hbm_specs(n_in), out_specs=_hbm_specs(n_out),
        scratch_shapes=list(comm.scratch), input_output_aliases=dict(comm.aliases),
    )(*comm.inputs)


def _small_all_reduce(parts, loss8):
    n = len(parts)

    def body(*refs):
        p_refs, loss_ref, out_ref, pack, land, send_sems, recv_sems = refs[:n], *refs[n:]
        x, y, c = lax.axis_index("x"), lax.axis_index("y"), lax.axis_index("c")
        me = 4 * x + 2 * y + c
        pack[...] = jnp.zeros_like(pack)
        for r, ref in enumerate(p_refs):
            pack[r:r + 1, 0:ref.shape[1]] = ref[...]
        pack[LOSS_ROW:LOSS_ROW + 1, 0:LANES] = loss_ref[0:1, :]
        land[me] = pack[...]
        cps = []
        for k in range(1, N_DEV):
            peer = (x ^ (k >> 2), y ^ ((k >> 1) & 1), c ^ (k & 1))
            cps.append(_remote(pack, land.at[me], send_sems.at[k - 1], recv_sems.at[k - 1], peer))
        for cp in cps:
            cp.start()
        for cp in cps:
            cp.wait()
        acc = land[0]
        for d in range(1, N_DEV):
            acc = acc + land[d]
        out_ref[...] = acc

    vmem = pl.BlockSpec(memory_space=pltpu.VMEM)
    return pl.pallas_call(
        body, name="small_all_reduce",
        out_shape=jax.ShapeDtypeStruct((SMALL_ROWS, D_MODEL), F32),
        in_specs=[vmem] * (n + 1), out_specs=vmem,
        scratch_shapes=[pltpu.VMEM((SMALL_ROWS, D_MODEL), F32), pltpu.VMEM((N_DEV, SMALL_ROWS, D_MODEL), F32),
                        pltpu.SemaphoreType.DMA((N_DEV - 1,)), pltpu.SemaphoreType.DMA((N_DEV - 1,))],
    )(*parts, loss8)


def _rs_sibling_sum(name, grad, got, place):
    _, rows, cols = grad.shape

    def body(place_ref, g_ref, l_ref, own_ref, b_ref):
        s = g_ref[...] + l_ref[...]
        b_ref[...] = s.astype(BF16)

        @pl.when(pl.program_id(0) == place_ref[1])
        def _():
            own_ref[...] = s

    by_chip = pl.BlockSpec((None, rows, cols), lambda q, place_ref: (q, 0, 0))
    return pl.pallas_call(
        body, name=name,
        grid_spec=pltpu.PrefetchScalarGridSpec(
            num_scalar_prefetch=1, grid=(4,),
            in_specs=[pl.BlockSpec((None, rows, cols), lambda q, place_ref: (2 * q + place_ref[0], 0, 0)), by_chip],
            out_specs=[pl.BlockSpec((rows, cols), lambda q, place_ref: (0, 0)), by_chip]),
        out_shape=(jax.ShapeDtypeStruct((rows, cols), F32), jax.ShapeDtypeStruct((4, rows, cols), BF16)),
        compiler_params=_cparams("arbitrary"),
    )(place, grad, got)


def _adamw_math(w, g, m, v):
    m2 = ADAM_B1 * m + (1.0 - ADAM_B1) * g
    v2 = ADAM_B2 * v + (1.0 - ADAM_B2) * (g * g)
    m_hat = m2 / (1.0 - ADAM_B1 ** ADAM_STEP)
    v_hat = v2 / (1.0 - ADAM_B2 ** ADAM_STEP)
    delta = -ADAM_LR * (m_hat / (jnp.sqrt(v_hat) + ADAM_EPS) + ADAM_WD * w)
    return delta, m2, v2


def _update_tile(rows):
    return 256 if rows % 256 == 0 else rows


def _rs_final_sum(name, own, got):
    def body(o_ref, r_ref, g_out):
        g = o_ref[...]
        for k in range(3):
            g = g + r_ref[k].astype(F32)
        g_out[...] = g

    return pl.pallas_call(body, name=name, out_shape=jax.ShapeDtypeStruct(own.shape, F32))(own, got)


def _adamw_sharded(name, w, m, v, own, got):
    _, rows, cols = w.shape
    tr = _update_tile(rows)

    def body(o_ref, r_ref, w_ref, m_ref, v_ref, g_out, d_out, m_out, v_out):
        g = o_ref[:, 0:cols]
        for k in range(3):
            g = g + r_ref[k, :, 0:cols].astype(F32)
        d, m2, v2 = _adamw_math(w_ref[0], g, m_ref[0], v_ref[0])
        g_out[0] = g
        d_out[0] = d
        m_out[0] = m2
        v_out[0] = v2

    mine = pl.BlockSpec((1, tr, cols), lambda i: (0, i, 0))
    shp = jax.ShapeDtypeStruct(w.shape, F32)
    wide = own.shape[1]
    return pl.pallas_call(
        body, name=name, grid=(rows // tr,), out_shape=(shp,) * 4,
        in_specs=[pl.BlockSpec((tr, wide), lambda i: (i, 0)), pl.BlockSpec((3, tr, wide), lambda i: (0, i, 0)),
                  mine, mine, mine],
        out_specs=[mine] * 4,
        compiler_params=_cparams("parallel"),
    )(own, got, w, m, v)


def _adamw_given(name, g, w, m, v):
    _, rows, cols = w.shape
    tr = _update_tile(rows)

    def body(g_ref, w_ref, m_ref, v_ref, g_out, d_out, m_out, v_out):
        g = g_ref[...]
        d, m2, v2 = _adamw_math(w_ref[0], g, m_ref[0], v_ref[0])
        g_out[0] = g
        d_out[0] = d
        m_out[0] = m2
        v_out[0] = v2

    own = pl.BlockSpec((1, tr, cols), lambda i: (0, i, 0))
    shp = jax.ShapeDtypeStruct(w.shape, F32)
    return pl.pallas_call(
        body, name=name, grid=(rows // tr,), out_shape=(shp,) * 4,
        in_specs=[pl.BlockSpec((tr, cols), lambda i: (i, 0)), own, own, own], out_specs=[own] * 4,
        compiler_params=_cparams("parallel"),
    )(g, w, m, v)


def _adamw_small(total, ws, ms, vs):
    n = len(ws)

    def body(*refs):
        t_ref = refs[0]
        w_refs, m_refs, v_refs = refs[1:1 + n], refs[1 + n:1 + 2 * n], refs[1 + 2 * n:1 + 3 * n]
        outs = refs[1 + 3 * n:]
        outs[0][...] = t_ref[LOSS_ROW:LOSS_ROW + 1, 0:1]
        for r in range(n):
            g = t_ref[r:r + 1, 0:w_refs[r].shape[1]]
            d, m2, v2 = _adamw_math(w_refs[r][...], g, m_refs[r][...], v_refs[r][...])
            for k, val in enumerate((g, d, m2, v2)):
                outs[1 + 4 * r + k][...] = val

    vmem = pl.BlockSpec(memory_space=pltpu.VMEM)
    out_shape = [jax.ShapeDtypeStruct((1, 1), F32)]
    for w in ws:
        out_shape += [jax.ShapeDtypeStruct(w.shape, F32)] * 4
    return pl.pallas_call(
        body, name="adamw_small", out_shape=out_shape,
        in_specs=[vmem] * (1 + 3 * n), out_specs=[vmem] * len(out_shape),
    )(total, *ws, *ms, *vs)


def _rope_tables(pos_col):
    t = pos_col.shape[0]
    inv = (np.float32(ROPE_THETA) ** (-np.arange(0, ROPE, 2, dtype=np.float32) / np.float32(ROPE))).astype(np.float32)
    freq = np.zeros((1, LANES), np.float32)
    freq[0, NOPE:NOPE + ROPE // 2] = inv
    freq[0, NOPE + ROPE // 2:NOPE + ROPE] = inv
    tm = _row_tile(t)

    def body(p_ref, f_ref, c_ref, s_ref):
        ang = p_ref[...].astype(F32) * f_ref[...]
        c_ref[...] = jnp.cos(ang)
        s_ref[...] = jnp.sin(ang)

    shp = jax.ShapeDtypeStruct((t, LANES), F32)
    return pl.pallas_call(
        body, name="rope_tables", grid=(t // tm,), out_shape=(shp, shp),
        in_specs=[pl.BlockSpec((tm, 1), lambda i: (i, 0)), pl.BlockSpec((1, LANES), lambda i: (0, 0))],
        out_specs=(pl.BlockSpec((tm, LANES), lambda i: (i, 0)),) * 2,
        compiler_params=_cparams("parallel"),
    )(pos_col, jnp.asarray(freq))


def _in_proj(x, g, w_qkv, w_rest, comm=None):
    t = x.shape[0]
    tm = _row_tile(t)

    def main(ins, outs, scr):
        x_ref, g_ref, wq_ref, wr_ref = ins
        h_ref, fq_ref, fk_ref, fv_ref, r_ref = outs
        h, _ = _rms(x_ref[...], g_ref[...])
        hb = h.astype(BF16)
        h_ref[...] = hb
        for n, (ref, factor) in enumerate(((fq_ref, FOX_Q_FACTOR), (fk_ref, None), (fv_ref, None))):
            part = _dot_nt(hb, wq_ref[n * FOX_WIDTH:(n + 1) * FOX_WIDTH, :])
            ref[...] = (part if factor is None else part * factor).astype(BF16)
        r_ref[...] = _dot_nt(hb, wr_ref[...])

    row = lambda n: pl.BlockSpec((tm, n), lambda i: (i, 0))
    full = lambda a: pl.BlockSpec(a.shape, lambda i: (0,) * a.ndim)
    return _hosted_call(
        "in_proj", main, (t // tm,), [x, g, w_qkv, w_rest], [row(D_MODEL), full(g), full(w_qkv), full(w_rest)],
        (jax.ShapeDtypeStruct((t, D_MODEL), BF16),) + (jax.ShapeDtypeStruct((t, FOX_WIDTH), BF16),) * 3
        + (jax.ShapeDtypeStruct((t, REST_COLS), F32),),
        (row(D_MODEL), row(FOX_WIDTH), row(FOX_WIDTH), row(FOX_WIDTH), row(REST_COLS)), [], comm)


def _log_sigmoid(z):
    return jnp.minimum(z, 0.0) - jnp.log(1.0 + jnp.exp(-jnp.abs(z)))


def _split3(v):
    hi = v.astype(BF16)
    r1 = v - hi.astype(F32)
    mid = r1.astype(BF16)
    lo = (r1 - mid.astype(F32)).astype(BF16)
    return hi, mid, lo


def _scan_tile(t):
    return 512 if t >= 2048 else (256 if t >= 256 else t)


def _forget_cumsum(rest, b128):
    t = rest.shape[0]
    tb = _scan_tile(t)

    def body(r_ref, b_ref, row_ref, rep_ref, f_sc, carry):
        @pl.when(pl.program_id(0) == 0)
        def _():
            carry[...] = jnp.zeros_like(carry)
        lf = _log_sigmoid(r_ref[...] + b_ref[...])
        tri = (lax.broadcasted_iota(jnp.int32, (tb, tb), 0) >= lax.broadcasted_iota(jnp.int32, (tb, tb), 1)).astype(BF16)
        hi, mid, lo = _split3(lf)
        f_sc[...] = (_dot(tri, hi) + _dot(tri, mid)) + _dot(tri, lo) + carry[...]
        carry[...] = f_sc[tb - 1:tb, :]
        f2 = f_sc[...] * LOG2E
        row_ref[...] = jnp.transpose(f2)[0:HEADS, :]
        lane = _lane()
        for h in range(HEADS):
            col = jnp.sum(jnp.where(lane == h, f2, 0.0), axis=1, keepdims=True)
            rep_ref[h] = jnp.broadcast_to(col, (tb, LANES))

    return pl.pallas_call(
        body, name="forget_cumsum", grid=(t // tb,),
        out_shape=(jax.ShapeDtypeStruct((HEADS, t), F32), jax.ShapeDtypeStruct((HEADS, t, LANES), F32)),
        in_specs=[pl.BlockSpec((tb, LANES), lambda i: (i, 0)), pl.BlockSpec((1, LANES), lambda i: (0, 0))],
        out_specs=(pl.BlockSpec((HEADS, tb), lambda i: (0, i)), pl.BlockSpec((HEADS, tb, LANES), lambda i: (0, i, 0))),
        scratch_shapes=[pltpu.VMEM((tb, LANES), F32), pltpu.VMEM((1, LANES), F32)],
        compiler_params=_cparams("arbitrary"),
    )(rest, b128)


def _forget_bwd(rest, b128, d_fq, d_fk):
    t = rest.shape[0]
    tb = _scan_tile(t)
    nb = t // tb

    def body(r_ref, b_ref, dfq_ref, dfk_ref, dz_ref, db_ref, carry):
        @pl.when(pl.program_id(0) == 0)
        def _():
            carry[...] = jnp.zeros_like(carry)
            db_ref[...] = jnp.zeros_like(db_ref)
        tri = (lax.broadcasted_iota(jnp.int32, (tb, tb), 0) <= lax.broadcasted_iota(jnp.int32, (tb, tb), 1)).astype(BF16)
        lane = _lane()
        df = jnp.zeros((tb, LANES), F32)
        for h in range(HEADS):
            df = df + jnp.where(lane == h, dfq_ref[h] + dfk_ref[h], 0.0)
        hi, mid, lo = _split3(df)
        dlf = (_dot(tri, hi) + _dot(tri, mid)) + _dot(tri, lo) + carry[...]
        z = r_ref[...] + b_ref[...]
        dz = dlf / (1.0 + jnp.exp(z))
        dz_ref[...] = dz
        db_ref[...] += jnp.sum(dz, axis=0, keepdims=True)
        carry[...] = carry[...] + jnp.sum(df, axis=0, keepdims=True)

    rev = lambda i: (nb - 1 - i, 0)
    rev3 = pl.BlockSpec((HEADS, tb, LANES), lambda i: (0, nb - 1 - i, 0))
    return pl.pallas_call(
        body, name="forget_bwd", grid=(nb,),
        out_shape=(jax.ShapeDtypeStruct((t, LANES), F32), jax.ShapeDtypeStruct((1, LANES), F32)),
        in_specs=[pl.BlockSpec((tb, LANES), rev), pl.BlockSpec((1, LANES), lambda i: (0, 0)), rev3, rev3],
        out_specs=(pl.BlockSpec((tb, LANES), rev), pl.BlockSpec((1, LANES), lambda i: (0, 0))),
        scratch_shapes=[pltpu.VMEM((1, LANES), F32)],
        compiler_params=_cparams("arbitrary"),
    )(rest, b128, d_fq, d_fk)


def _mla_prep(rest, gq, gkv, wq, wkv, cos, sin):
    t = rest.shape[0]
    tm = _row_tile(t)

    def body(r_ref, gq_ref, gkv_ref, wq_ref, wkv_ref, c_ref, s_ref, q_ref, k_ref, kv_ref, cq_ref, ckv_ref):
        cos_, sin_ = c_ref[...], s_ref[...]
        cq, _ = _rms(r_ref[:, REST_CQ:REST_CKV], gq_ref[...])
        ckv, _ = _rms(r_ref[:, REST_CKV:REST_KR], gkv_ref[...])
        cqb, ckvb = cq.astype(BF16), ckv.astype(BF16)
        cq_ref[...] = cqb
        ckv_ref[...] = ckvb
        k_rope = _rope(r_ref[:, REST_KR:REST_COLS], cos_, sin_)
        lo = _lane() < NOPE
        for h in range(HEADS):
            q_ref[h] = (_rope(_dot(cqb, wq_ref[h]), cos_, sin_) * MLA_Q_FACTOR).astype(BF16)
            kv = _dot(ckvb, wkv_ref[h])
            kv_ref[h] = kv.astype(BF16)
            k_ref[h] = (jnp.where(lo, kv, 0.0) + k_rope).astype(BF16)

    row = lambda n: pl.BlockSpec((tm, n), lambda i: (i, 0))
    full = lambda a: pl.BlockSpec(a.shape, lambda i: (0,) * a.ndim)
    heads = pl.BlockSpec((HEADS, tm, LANES), lambda i: (0, i, 0))
    hshape = jax.ShapeDtypeStruct((HEADS, t, LANES), BF16)
    return pl.pallas_call(
        body, name="mla_prep", grid=(t // tm,),
        out_shape=(hshape, hshape, hshape, jax.ShapeDtypeStruct((t, Q_RANK), BF16), jax.ShapeDtypeStruct((t, KV_RANK), BF16)),
        in_specs=[row(REST_COLS), full(gq), full(gkv), full(wq), full(wkv), row(LANES), row(LANES)],
        out_specs=(heads, heads, heads, row(Q_RANK), row(KV_RANK)),
        compiler_params=_cparams("parallel"),
    )(rest, gq, gkv, wq, wkv, cos, sin)


def _pair_specs(fox, t, tq, blocked_q):
    if fox:
        blk = pl.BlockSpec((tq, LANES), lambda p, i: (i, p))
        whole = pl.BlockSpec((t, LANES), lambda p, i: (0, p))
    else:
        blk = pl.BlockSpec((2, tq, LANES), lambda p, i: (p, i, 0))
        whole = pl.BlockSpec((2, t, LANES), lambda p, i: (p, 0, 0))
    return [blk, whole, whole] if blocked_q else [whole, blk, blk]


def _tile_lanes(x, n):
    return jnp.tile(x, (1, n)) if n > 1 else x


class _Comm(NamedTuple):
    inputs: tuple
    out_shape: tuple
    aliases: dict
    scratch: tuple
    start: Callable
    finish: Callable


def _hosted_call(name, main, grid, args, in_specs, out_shape, out_specs, scratch, comm):
    n_in, n_out, n_scr = len(args), len(out_shape), len(scratch)
    c_in = list(comm.inputs) if comm else []
    c_out = list(comm.out_shape) if comm else []

    def at_step(which):
        hit = pl.program_id(0) == which[0]
        for axis in range(1, len(grid)):
            hit = jnp.logical_and(hit, pl.program_id(axis) == which[axis])
        return hit

    def body(*refs):
        bounds = [0, n_in, len(c_in), n_out, len(c_out), n_scr]
        starts = [sum(bounds[:k + 1]) for k in range(len(bounds))]
        ins, cins, outs, couts, scr = [refs[a:b] for a, b in zip(starts[:-1], starts[1:])]
        sems = refs[starts[-1]:]
        if comm:
            @pl.when(at_step([0] * len(grid)))
            def _():
                comm.start(cins, couts, sems)
        main(ins, outs, scr)
        if comm:
            @pl.when(at_step([n - 1 for n in grid]))
            def _():
                comm.finish(cins, couts, sems)

    res = pl.pallas_call(
        body, name=name, grid=grid,
        out_shape=list(out_shape) + c_out,
        in_specs=list(in_specs) + _hbm_specs(len(c_in)),
        out_specs=list(out_specs) + _hbm_specs(len(c_out)),
        scratch_shapes=list(scratch) + (list(comm.scratch) if comm else []),
        input_output_aliases={n_in + i: n_out + o for i, o in comm.aliases.items()} if comm else {},
        compiler_params=_cparams(*(["arbitrary"] * len(grid))),
    )(*args, *c_in)
    return res[:n_out], res[n_out:]


def _stat_rows(x):
    return jnp.transpose(x)[0:8, :]


FWD_HEADS = 4


def _attn_fwd(fox, q, k, v, f2_rows=None, comm=None):
    t = q.shape[0] if fox else q.shape[1]
    tq = _row_tile(t)
    nq = t // tq
    nh = FWD_HEADS
    wide = (nh // 2) * LANES

    def main(ins, outs, scr):
        q_ref, k_ref, v_ref = ins[:3]
        fr_ref = ins[3] if fox else None
        o_ref, lset_ref = outs
        m_sc, acc_sc = scr
        i = pl.program_id(1)
        lo = _lane() < HEAD_DIM
        hi = jnp.logical_not(lo)
        zero, one = jnp.zeros((), BF16), jnp.ones((), BF16)
        lanes_of = lambda h: slice((h // 2) * LANES, (h // 2 + 1) * LANES)
        if fox:
            qs = [jnp.where(lo if h % 2 == 0 else hi, q_ref[:, lanes_of(h)], zero) for h in range(nh)]
            sum_lanes = [hi if h % 2 == 0 else lo for h in range(nh)]
        else:
            qs = [q_ref[h] for h in range(nh)]
            sum_lanes = [lo] * nh
        m_sc[...] = jnp.full_like(m_sc, -jnp.inf)
        acc_sc[...] = jnp.zeros_like(acc_sc)

        def block(j, r0, nr, c0, nc, seen_from):
            rows = slice(r0, r0 + nr)
            sl = pl.ds(pl.multiple_of(j * tq + c0, math.gcd(tq, c0) if c0 else tq), nc)
            if seen_from is not None:
                seen = (lax.broadcasted_iota(jnp.int32, (nr, nc), 1)
                        <= lax.broadcasted_iota(jnp.int32, (nr, nc), 0) + seen_from)
            for h in range(nh):
                kj, vj = (k_ref[sl, lanes_of(h)], v_ref[sl, lanes_of(h)]) if fox else (k_ref[h, sl, :], v_ref[h, sl, :])
                s = _dot_nt(qs[h][rows], kj)
                if fox:
                    s = s - fr_ref[h, j, :, c0:c0 + nc]
                if seen_from is not None:
                    s = jnp.where(seen, s, -jnp.inf)
                m_prev = m_sc[h, rows]
                m_new = jnp.maximum(m_prev, jnp.max(s, axis=1, keepdims=True))
                p = jnp.exp2((s - _tile_lanes(m_new, nc // LANES)).astype(BF16))
                vj = jnp.where(sum_lanes[h], one, vj)
                acc_sc[h, rows] = jnp.exp2(m_prev - m_new) * acc_sc[h, rows] + _dot(p, vj)
                m_sc[h, rows] = m_new

        def loop_body(j, carry):
            block(j, 0, tq, 0, tq, None)
            return carry

        lax.fori_loop(0, i, loop_body, 0)
        half = tq // 2
        if half % LANES == 0:
            block(i, 0, half, 0, half, 0)
            block(i, half, half, 0, tq, half)
        else:
            block(i, 0, tq, 0, tq, 0)
        res = []
        for h in range(nh):
            acc = acc_sc[h]
            swapped = pltpu.roll(acc, HEAD_DIM, 1)
            res.append(acc / swapped)
            lse2 = m_sc[h] + jnp.log(jnp.where(sum_lanes[h], acc, swapped)) * LOG2E
            lset_ref[h, 0] = _stat_rows(lse2)
        for pr in range(nh // 2):
            even = res[2 * pr] if fox else pltpu.roll(res[2 * pr], HEAD_DIM, 1)
            o_ref[:, pr * LANES:(pr + 1) * LANES] = jnp.where(lo, even, res[2 * pr + 1])

    if fox:
        in_specs = [pl.BlockSpec((tq, wide), lambda g, i: (i, g))] + [pl.BlockSpec((t, wide), lambda g, i: (0, g))] * 2
        in_specs += [pl.BlockSpec((nh, nq, 1, tq), lambda g, i: (g, 0, 0, 0))]
        args = [q, k, v, f2_rows]
    else:
        in_specs = [pl.BlockSpec((nh, tq, LANES), lambda g, i: (g, i, 0))] + [pl.BlockSpec((nh, t, LANES), lambda g, i: (g, 0, 0))] * 2
        args = [q, k, v]
    return _hosted_call(
        "fox_attn_fwd" if fox else "mla_attn_fwd", main, (HEADS // nh, nq), args, in_specs,
        (jax.ShapeDtypeStruct((t, 4 * LANES), F32), jax.ShapeDtypeStruct((HEADS, nq, 8, tq), F32)),
        (pl.BlockSpec((tq, wide), lambda g, i: (i, g)), pl.BlockSpec((nh, 1, 8, tq), lambda g, i: (g, i, 0, 0))),
        [pltpu.VMEM((nh, tq, LANES), F32), pltpu.VMEM((nh, tq, LANES), F32)], comm)


def _head_do(fox, hh, do2, lo):
    if fox:
        return jnp.where(lo if hh == 0 else jnp.logical_not(lo), do2, 0.0)
    return jnp.where(lo, 0.0, pltpu.roll(do2, HEAD_DIM, 1) if hh == 0 else do2)


def _attn_bwd(fox, q, k, v, do, lse_rows, delta_rows, f2_rep=None, comm=None):
    t = q.shape[0] if fox else q.shape[1]
    tq = _row_tile(t)
    nq = t // tq
    scale = FOX_SCALE if fox else MLA_SCALE

    def main(ins, outs, scr):
        if fox:
            q_ref, k_ref, v_ref, f_ref, do_ref, lse_ref, dl_ref = ins
            dq_ref, dk_ref, dv_ref, dfq_ref, dfk_ref = outs
        else:
            q_ref, k_ref, v_ref, do_ref, lse_ref, dl_ref = ins
            dq_ref, dkv_ref, dkr_ref = outs
        dq_sc, dk_sc, dv_sc, kt_sc = scr
        j = pl.program_id(1)
        lane = _lane()
        lo = lane < HEAD_DIM
        hi = jnp.logical_not(lo)
        zero, one = jnp.zeros((), BF16), jnp.ones((), BF16)

        @pl.when(j == 0)
        def _():
            dq_sc[...] = jnp.zeros_like(dq_sc)

        dk_sc[...] = jnp.zeros_like(dk_sc)
        dv_sc[...] = jnp.zeros_like(dv_sc)
        for hh in range(2):
            kj = k_ref[...] if fox else k_ref[hh]
            if fox:
                kj = jnp.where(hi if hh == 0 else lo, one, kj)
            kt_sc[hh] = jnp.transpose(kj)

        def block(i, r0, nr, c0, nc, masked):
            rows, cols = slice(r0, r0 + nr), slice(c0, c0 + nc)
            sl = pl.ds(pl.multiple_of(i * tq + c0, math.gcd(tq, c0) if c0 else tq), nc)
            do_i = do_ref[sl, :]
            if masked:
                seen = lax.broadcasted_iota(jnp.int32, (nr, nc), 1) >= lax.broadcasted_iota(jnp.int32, (nr, nc), 0)
            for hh in range(2):
                kj = k_ref[rows, :] if fox else k_ref[hh, rows, :]
                vj = v_ref[rows, :] if fox else v_ref[hh, rows, :]
                qi = jnp.where(lo if hh == 0 else hi, q_ref[sl, :], zero) if fox else q_ref[hh, sl, :]
                dob = _head_do(fox, hh, do_i, lo).astype(BF16)
                st = _dot_nt(kj, qi)
                if fox:
                    st = st - _tile_lanes(f_ref[hh, rows, :], nc // LANES)
                if masked:
                    st = jnp.where(seen, st, -jnp.inf)
                pt = jnp.exp2(st - lse_ref[hh, i, 0:1, cols])
                dpt = _dot_nt(vj, dob)
                dst = (pt * (dpt - dl_ref[hh, i, 0:1, cols])).astype(BF16)
                dv_sc[hh, :, rows] += _dot_nt(jnp.transpose(dob), pt.astype(BF16))
                if fox:
                    qi = jnp.where(hi if hh == 0 else lo, one, qi)
                dk_sc[hh, :, rows] += _dot_nt(jnp.transpose(qi), dst)
                dq_sc[hh, i, :, cols] += _dot(kt_sc[hh, :, rows], dst)

        def loop_body(i, carry):
            block(i, 0, tq, 0, tq, False)
            return carry

        half = tq // 2
        if half % LANES == 0:
            block(j, 0, half, 0, tq, True)
            block(j, half, half, half, half, True)
        else:
            block(j, 0, tq, 0, tq, True)
        lax.fori_loop(j + 1, nq, loop_body, 0)
        dks = [jnp.transpose(dk_sc[hh]) for hh in range(2)]
        dvs = [jnp.transpose(dv_sc[hh]) for hh in range(2)]
        if fox:
            dk_ref[...] = (jnp.where(lo, dks[0], dks[1]) * LN2).astype(BF16)
            dv_ref[...] = (dvs[0] + dvs[1]).astype(BF16)
            for hh in range(2):
                dfk_ref[hh] = -jnp.where(hi if hh == 0 else lo, dks[hh], pltpu.roll(dks[hh], HEAD_DIM, 1))
        else:
            rope_lanes = jnp.logical_and(lane >= NOPE, lane < NOPE + ROPE)
            dkr = jnp.zeros((tq, LANES), F32)
            for hh in range(2):
                dk = dks[hh] * LN2
                dkv_ref[hh] = jnp.where(lo, dk, dvs[hh])
                dkr = dkr + jnp.where(rope_lanes, dk, 0.0)
            dkr_ref[0] = dkr

        @pl.when(j == nq - 1)
        def _():
            for i in range(nq):
                rows = slice(i * tq, (i + 1) * tq)
                accs = [jnp.transpose(dq_sc[hh, i]) for hh in range(2)]
                if fox:
                    dq_ref[rows, :] = (jnp.where(lo, accs[0], accs[1]) * scale).astype(BF16)
                    for hh in range(2):
                        dfq_ref[hh, rows, :] = jnp.where(hi if hh == 0 else lo, accs[hh], pltpu.roll(accs[hh], HEAD_DIM, 1))
                else:
                    for hh in range(2):
                        dq_ref[hh, rows, :] = accs[hh] * scale

    stat = pl.BlockSpec((2, tq, LANES), lambda p, j: (p, j, 0))
    stat_all = pl.BlockSpec((2, t, LANES), lambda p, j: (p, 0, 0))
    rows4 = pl.BlockSpec((2, nq, 8, tq), lambda p, j: (p, 0, 0, 0))
    pair = pl.BlockSpec((tq, LANES), lambda p, j: (j, p))
    pair_all = pl.BlockSpec((t, LANES), lambda p, j: (0, p))
    in_specs = _pair_specs(fox, t, tq, False)
    args = [q, k, v]
    if fox:
        in_specs += [stat]
        args += [f2_rep]
    in_specs += [pair_all, rows4, rows4]
    args += [do, lse_rows, delta_rows]
    heads_f32 = jax.ShapeDtypeStruct((HEADS, t, LANES), F32)
    if fox:
        wide = jax.ShapeDtypeStruct((t, 4 * LANES), BF16)
        out_shape = (wide, wide, wide, heads_f32, heads_f32)
        out_specs = (pair_all, pair, pair, stat_all, stat)
    else:
        out_shape = (heads_f32, heads_f32, jax.ShapeDtypeStruct((HEADS // 2, t, LANES), F32))
        out_specs = (stat_all, stat, pl.BlockSpec((1, tq, LANES), lambda p, j: (p, j, 0)))
    acc = pltpu.VMEM((2, LANES, tq), F32)
    return _hosted_call("fox_attn_bwd" if fox else "mla_attn_bwd", main, (HEADS // 2, nq), args, in_specs,
                        out_shape, out_specs,
                        [pltpu.VMEM((2, nq, LANES, tq), F32), acc, acc, pltpu.VMEM((2, LANES, tq), BF16)], comm)


def _attn_out(x, fox_o, mla_o, gf, gm, w_o):
    t = x.shape[0]
    tm = _row_tile(t)

    def body(x_ref, f_ref, m_ref, gf_ref, gm_ref, w_ref, x1_ref, mix_ref):
        nf, _ = _rms(f_ref[...], gf_ref[...])
        nm, _ = _rms(m_ref[...], gm_ref[...])
        nfb, nmb = nf.astype(BF16), nm.astype(BF16)
        mix_ref[:, :FOX_WIDTH] = nfb
        mix_ref[:, FOX_WIDTH:] = nmb
        x1_ref[...] = x_ref[...] + _dot(nfb, w_ref[:FOX_WIDTH, :]) + _dot(nmb, w_ref[FOX_WIDTH:, :])

    row = lambda n: pl.BlockSpec((tm, n), lambda i: (i, 0))
    full = lambda a: pl.BlockSpec(a.shape, lambda i: (0,) * a.ndim)
    return pl.pallas_call(
        body, name="attn_out", grid=(t // tm,),
        out_shape=(jax.ShapeDtypeStruct((t, D_MODEL), F32), jax.ShapeDtypeStruct((t, D_MODEL), BF16)),
        in_specs=[row(D_MODEL), row(FOX_WIDTH), row(MLA_WIDTH), full(gf), full(gm), full(w_o)],
        out_specs=(row(D_MODEL), row(D_MODEL)),
        compiler_params=_cparams("parallel"),
    )(x, fox_o, mla_o, gf, gm, w_o)


def _mlp_tile(t):
    return 256 if t >= 2048 else 128


def _resident(a):
    return pl.BlockSpec(a.shape, lambda i: (0,) * a.ndim, pipeline_mode=pl.Buffered(1))


FF_CHUNK = 512


def _mlp_fwd(x1, g_mlp, w_up, w_down, g_fin, target):
    t = x1.shape[0]
    tm = _mlp_tile(t)

    def body(x_ref, g_ref, wu_ref, wd_ref, gf_ref, t_ref, u_ref, h_ref, dx_ref, dxb_ref, loss_ref, dg_ref, a_sc):
        @pl.when(pl.program_id(0) == 0)
        def _():
            loss_ref[...] = jnp.zeros_like(loss_ref)
            dg_ref[...] = jnp.zeros_like(dg_ref)

        x = x_ref[...]
        h, _ = _rms(x, g_ref[...])
        hb = h.astype(BF16)
        h_ref[...] = hb
        for f in range(D_FF // FF_CHUNK):
            sl = slice(f * FF_CHUNK, (f + 1) * FF_CHUNK)
            u = _dot(hb, wu_ref[:, sl])
            u_ref[:, sl] = u
            r = jnp.maximum(u, 0.0)
            a_sc[:, sl] = (r * r).astype(BF16)
        x2 = x + _dot(a_sc[...], wd_ref[...])
        y, r2 = _rms(x2, gf_ref[...])
        err = y - t_ref[...]
        loss_ref[...] += 0.5 * jnp.sum(jnp.mean(err * err, axis=-1, keepdims=True))
        dx, dg = _rms_bwd(x2, gf_ref[...], r2, err * (1.0 / D_MODEL))
        dx_ref[...] = dx
        dxb_ref[...] = dx.astype(BF16)
        dg_ref[...] += dg

    row = lambda n: pl.BlockSpec((tm, n), lambda i: (i, 0))
    vec = pl.BlockSpec((1, D_MODEL), lambda i: (0, 0))
    return pl.pallas_call(
        body, name="mlp_fwd", grid=(t // tm,),
        out_shape=(jax.ShapeDtypeStruct((t, D_FF), F32), jax.ShapeDtypeStruct((t, D_MODEL), BF16),
                   jax.ShapeDtypeStruct((t, D_MODEL), F32), jax.ShapeDtypeStruct((t, D_MODEL), BF16),
                   jax.ShapeDtypeStruct((8, LANES), F32), jax.ShapeDtypeStruct((1, D_MODEL), F32)),
        in_specs=[row(D_MODEL), vec, _resident(w_up), _resident(w_down), vec, row(D_MODEL)],
        out_specs=(row(D_FF), row(D_MODEL), row(D_MODEL), row(D_MODEL), pl.BlockSpec((8, LANES), lambda i: (0, 0)), vec),
        scratch_shapes=[pltpu.VMEM((tm, D_FF), BF16)],
        compiler_params=_cparams("arbitrary"),
    )(x1, g_mlp, w_up, w_down, g_fin, target)


def _mlp_bwd(dx2, u, x1, g_mlp, w_up, w_down):
    t = x1.shape[0]
    tm = _mlp_tile(t)

    def body(dx_ref, u_ref, x_ref, g_ref, wu_ref, wd_ref, du_ref, a_ref, dx1_ref, dx1b_ref, dg_ref):
        @pl.when(pl.program_id(0) == 0)
        def _():
            dg_ref[...] = jnp.zeros_like(dg_ref)

        dx2 = dx_ref[...]
        dxb = dx2.astype(BF16)
        for f in range(D_FF // FF_CHUNK):
            sl = slice(f * FF_CHUNK, (f + 1) * FF_CHUNK)
            r = jnp.maximum(u_ref[:, sl], 0.0)
            a_ref[:, sl] = (r * r).astype(BF16)
            da = _dot_nt(dxb, wd_ref[sl, :])
            du_ref[:, sl] = (da * (2.0 * r)).astype(BF16)
        dh = _dot_nt(du_ref[...], wu_ref[...])
        x = x_ref[...]
        _, r1 = _rms(x, g_ref[...])
        dx, dg = _rms_bwd(x, g_ref[...], r1, dh)
        dx1 = dx2 + dx
        dx1_ref[...] = dx1
        dx1b_ref[...] = dx1.astype(BF16)
        dg_ref[...] += dg

    row = lambda n: pl.BlockSpec((tm, n), lambda i: (i, 0))
    vec = pl.BlockSpec((1, D_MODEL), lambda i: (0, 0))
    return pl.pallas_call(
        body, name="mlp_bwd", grid=(t // tm,),
        out_shape=(jax.ShapeDtypeStruct((t, D_FF), BF16), jax.ShapeDtypeStruct((t, D_FF), BF16),
                   jax.ShapeDtypeStruct((t, D_MODEL), F32), jax.ShapeDtypeStruct((t, D_MODEL), BF16),
                   jax.ShapeDtypeStruct((1, D_MODEL), F32)),
        in_specs=[row(D_MODEL), row(D_FF), row(D_MODEL), vec, _resident(w_up), _resident(w_down)],
        out_specs=(row(D_FF), row(D_FF), row(D_MODEL), row(D_MODEL), vec),
        compiler_params=_cparams("arbitrary"),
    )(dx2, u, x1, g_mlp, w_up, w_down)


def _matmul_tn(name, a, b, blocks=None):
    t, m = a.shape
    n = b.shape[1]
    tk = t if a.dtype == BF16 and b.dtype == BF16 else min(t, 2048)
    steps = t // tk
    bm = m if m <= 1024 else 512
    bn = n if n <= 1024 else 512
    width = bn if blocks is None else n // blocks
    per = bn // width

    def body(a_ref, b_ref, o_ref, acc_sc):
        kk = pl.program_id(2)

        @pl.when(kk == 0)
        def _():
            acc_sc[...] = jnp.zeros_like(acc_sc)

        acc_sc[...] += _dot_tn(a_ref[...].astype(BF16), b_ref[...].astype(BF16))

        @pl.when(kk == steps - 1)
        def _():
            if blocks is None:
                o_ref[...] = acc_sc[...]
            else:
                for s in range(per):
                    o_ref[s] = acc_sc[:, s * width:(s + 1) * width]

    if blocks is None:
        o_spec = pl.BlockSpec((bm, bn), lambda i, j, kk: (i, j))
        o_shape = (m, n)
    else:
        o_spec = pl.BlockSpec((per, bm, width), lambda i, j, kk: (j, i, 0))
        o_shape = (blocks, m, width)
    return pl.pallas_call(
        body, name=name, grid=(m // bm, n // bn, steps),
        out_shape=jax.ShapeDtypeStruct(o_shape, F32),
        in_specs=[pl.BlockSpec((tk, bm), lambda i, j, kk: (kk, i)), pl.BlockSpec((tk, bn), lambda i, j, kk: (kk, j))],
        out_specs=o_spec,
        scratch_shapes=[pltpu.VMEM((bm, bn), F32)],
        compiler_params=_cparams("parallel", "parallel", "arbitrary"),
    )(a, b)


def _dw_in(dfq, dfk, dfv, drest, h1):
    t = h1.shape[0]
    tk = min(t, 1024)
    off_ff = 3 * FOX_WIDTH
    off_cq = off_ff + HEADS
    off_kr = IN_COLS - ROPE

    def body(dq_ref, dk_ref, dv_ref, dr_ref, h_ref, o_ref):
        h = h_ref[...]
        r = _dot_tn(dr_ref[...], h)
        parts = [(slice(n * FOX_WIDTH, (n + 1) * FOX_WIDTH), _dot_tn(ref[...], h)) for n, ref in enumerate((dq_ref, dk_ref, dv_ref))]
        parts += [(slice(off_ff, off_cq), r[0:HEADS]), (slice(off_cq, off_kr), r[REST_CQ:REST_KR]),
                  (slice(off_kr, IN_COLS), r[REST_KR + NOPE:REST_KR + NOPE + ROPE])]

        @pl.when(pl.program_id(0) == 0)
        def _():
            for rows, val in parts:
                o_ref[rows, :] = val

        @pl.when(pl.program_id(0) > 0)
        def _():
            for rows, val in parts:
                o_ref[rows, :] += val

    tok = lambda n: pl.BlockSpec((tk, n), lambda kk: (kk, 0))
    return pl.pallas_call(
        body, name="dw_in", grid=(t // tk,),
        out_shape=jax.ShapeDtypeStruct((IN_COLS, D_MODEL), F32),
        in_specs=[tok(FOX_WIDTH), tok(FOX_WIDTH), tok(FOX_WIDTH), tok(REST_COLS), tok(D_MODEL)],
        out_specs=pl.BlockSpec((IN_COLS, D_MODEL), lambda kk: (0, 0)),
        compiler_params=_cparams("arbitrary"),
    )(dfq, dfk, dfv, drest, h1)


def _attn_out_bwd(dx1, fox_o, mla_o, gf, gm, w_o):
    t = dx1.shape[0]
    tm = _row_tile(t)

    def body(dx_ref, f_ref, m_ref, gf_ref, gm_ref, w_ref, df_ref, dm_ref, dlf_ref, dlm_ref, dgf_ref, dgm_ref):
        @pl.when(pl.program_id(0) == 0)
        def _():
            dgf_ref[...] = jnp.zeros_like(dgf_ref)
            dgm_ref[...] = jnp.zeros_like(dgm_ref)
        dxb = dx_ref[...].astype(BF16)
        lane = lax.broadcasted_iota(jnp.int32, (8, LANES), 1)
        picks = [(lane < HEAD_DIM).astype(BF16), (lane >= HEAD_DIM).astype(BF16)]
        for o_ref, g_ref, lo_row, d_ref, dl_ref, dg_ref in ((f_ref, gf_ref, 0, df_ref, dlf_ref, dgf_ref),
                                                             (m_ref, gm_ref, FOX_WIDTH, dm_ref, dlm_ref, dgm_ref)):
            dn = _dot_nt(dxb, w_ref[lo_row:lo_row + FOX_WIDTH, :])
            o = o_ref[...]
            _, r = _rms(o, g_ref[...])
            d, dg = _rms_bwd(o, g_ref[...], r, dn)
            d_ref[...] = d
            dg_ref[...] += dg
            prod = d * o
            for h in range(HEADS):
                parts = _split3(prod[:, (h // 2) * LANES:(h // 2 + 1) * LANES])
                dl_ref[h, 0] = (_dot_nt(picks[h % 2], parts[0]) + _dot_nt(picks[h % 2], parts[1])) + _dot_nt(picks[h % 2], parts[2])

    row = lambda n: pl.BlockSpec((tm, n), lambda i: (i, 0))
    full = lambda a: pl.BlockSpec(a.shape, lambda i: (0,) * a.ndim)
    vec = pl.BlockSpec((1, FOX_WIDTH), lambda i: (0, 0))
    rows = pl.BlockSpec((HEADS, 1, 8, tm), lambda i: (0, i, 0, 0))
    o_shape = jax.ShapeDtypeStruct((t, FOX_WIDTH), F32)
    g_shape = jax.ShapeDtypeStruct((1, FOX_WIDTH), F32)
    r_shape = jax.ShapeDtypeStruct((HEADS, t // tm, 8, tm), F32)
    return pl.pallas_call(
        body, name="attn_out_bwd", grid=(t // tm,),
        out_shape=(o_shape, o_shape, r_shape, r_shape, g_shape, g_shape),
        in_specs=[row(D_MODEL), row(FOX_WIDTH), row(MLA_WIDTH), full(gf), full(gm), full(w_o)],
        out_specs=(row(FOX_WIDTH), row(MLA_WIDTH), rows, rows, vec, vec),
        compiler_params=_cparams("arbitrary"),
    )(dx1, fox_o, mla_o, gf, gm, w_o)


def _mla_prep_bwd(dq, dkv, dkr, dz, rest, gq, gkv, wq, wkv, cos, sin):
    t = rest.shape[0]
    tm = _row_tile(t)

    def body(dq_ref, dkv_ref, dkr_ref, dz_ref, r_ref, gq_ref, gkv_ref, wq_ref, wkv_ref, c_ref, s_ref,
             dr_ref, dqp_ref, dkvb_ref, dgq_ref, dgkv_ref):
        @pl.when(pl.program_id(0) == 0)
        def _():
            dgq_ref[...] = jnp.zeros_like(dgq_ref)
            dgkv_ref[...] = jnp.zeros_like(dgkv_ref)
        cos_, sin_ = c_ref[...], s_ref[...]
        dcq = jnp.zeros((tm, Q_RANK), F32)
        dckv = jnp.zeros((tm, KV_RANK), F32)
        for h in range(HEADS):
            dqp = _rope_bwd(dq_ref[h], cos_, sin_).astype(BF16)
            dqp_ref[:, h * LANES:(h + 1) * LANES] = dqp
            dcq = dcq + _dot_nt(dqp, wq_ref[h])
            dkvb = dkv_ref[h].astype(BF16)
            dkvb_ref[:, h * LANES:(h + 1) * LANES] = dkvb
            dckv = dckv + _dot_nt(dkvb, wkv_ref[h])
        dkrope = dkr_ref[0]
        for pr in range(1, HEADS // 2):
            dkrope = dkrope + dkr_ref[pr]
        cq = r_ref[:, REST_CQ:REST_CKV]
        _, rq = _rms(cq, gq_ref[...])
        d_cq, dgq = _rms_bwd(cq, gq_ref[...], rq, dcq)
        ckv = r_ref[:, REST_CKV:REST_KR]
        _, rkv = _rms(ckv, gkv_ref[...])
        d_ckv, dgkv = _rms_bwd(ckv, gkv_ref[...], rkv, dckv)
        dgq_ref[...] += dgq
        dgkv_ref[...] += dgkv
        dr_ref[:, 0:REST_CQ] = dz_ref[...].astype(BF16)
        dr_ref[:, REST_CQ:REST_CKV] = d_cq.astype(BF16)
        dr_ref[:, REST_CKV:REST_KR] = d_ckv.astype(BF16)
        dr_ref[:, REST_KR:REST_COLS] = _rope_bwd(dkrope, cos_, sin_).astype(BF16)

    row = lambda n: pl.BlockSpec((tm, n), lambda i: (i, 0))
    full = lambda a: pl.BlockSpec(a.shape, lambda i: (0,) * a.ndim)
    heads = pl.BlockSpec((HEADS, tm, LANES), lambda i: (0, i, 0))
    hshape = jax.ShapeDtypeStruct((t, HEADS * LANES), BF16)
    return pl.pallas_call(
        body, name="mla_prep_bwd", grid=(t // tm,),
        out_shape=(jax.ShapeDtypeStruct((t, REST_COLS), BF16), hshape, hshape,
                   jax.ShapeDtypeStruct((1, Q_RANK), F32), jax.ShapeDtypeStruct((1, KV_RANK), F32)),
        in_specs=[heads, heads, pl.BlockSpec((HEADS // 2, tm, LANES), lambda i: (0, i, 0)), row(LANES), row(REST_COLS),
                  full(gq), full(gkv), full(wq), full(wkv), row(LANES), row(LANES)],
        out_specs=(row(REST_COLS), row(HEADS * LANES), row(HEADS * LANES), pl.BlockSpec((1, Q_RANK), lambda i: (0, 0)),
                   pl.BlockSpec((1, KV_RANK), lambda i: (0, 0))),
        compiler_params=_cparams("arbitrary"),
    )(dq, dkv, dkr, dz, rest, gq, gkv, wq, wkv, cos, sin)


def _in_proj_bwd(x, g, dx1, dfq, dfk, dfv, drest, w_qkv, w_rest, comm=None):
    t = x.shape[0]
    tm = _row_tile(t)

    def main(ins, outs, scr):
        x_ref, g_ref, dx1_ref, dq_ref, dk_ref, dv_ref, dr_ref, wq_ref, wr_ref = ins
        dx_ref, dg_ref = outs

        @pl.when(pl.program_id(0) == 0)
        def _():
            dg_ref[...] = jnp.zeros_like(dg_ref)
        dh = _dot(dr_ref[...], wr_ref[...])
        for n, ref in enumerate((dq_ref, dk_ref, dv_ref)):
            dh = dh + _dot(ref[...], wq_ref[n * FOX_WIDTH:(n + 1) * FOX_WIDTH, :])
        xv = x_ref[...]
        _, r = _rms(xv, g_ref[...])
        dx, dg = _rms_bwd(xv, g_ref[...], r, dh)
        dx_ref[...] = dx1_ref[...] + dx
        dg_ref[...] += dg

    row = lambda n: pl.BlockSpec((tm, n), lambda i: (i, 0))
    full = lambda a: pl.BlockSpec(a.shape, lambda i: (0,) * a.ndim)
    vec = pl.BlockSpec((1, D_MODEL), lambda i: (0, 0))
    return _hosted_call(
        "in_proj_bwd", main, (t // tm,), [x, g, dx1, dfq, dfk, dfv, drest, w_qkv, w_rest],
        [row(D_MODEL), full(g), row(D_MODEL), row(FOX_WIDTH), row(FOX_WIDTH), row(FOX_WIDTH), row(REST_COLS),
         full(w_qkv), full(w_rest)],
        (jax.ShapeDtypeStruct((t, D_MODEL), F32), jax.ShapeDtypeStruct((1, D_MODEL), F32)), (row(D_MODEL), vec), [], comm)


def _pad_cols(a, n):
    return jnp.pad(a, ((0, 0),) * (a.ndim - 1) + ((0, n - a.shape[-1]),))


def kernel(x, positions, attn_norm_g, w_in, b_forget, q_norm_g, w_uq, kv_norm_g, w_ukv, fox_out_g, mla_out_g, w_o, mlp_norm_g, w_up, w_down, final_norm_g, loss_target, m_attn_norm_g, m_w_in, m_b_forget, m_q_norm_g, m_w_uq, m_kv_norm_g, m_w_ukv, m_fox_out_g, m_mla_out_g, m_w_o, m_mlp_norm_g, m_w_up, m_w_down, m_final_norm_g, v_attn_norm_g, v_w_in, v_b_forget, v_q_norm_g, v_w_uq, v_kv_norm_g, v_w_ukv, v_fox_out_g, v_mla_out_g, v_w_o, v_mlp_norm_g, v_w_up, v_w_down, v_final_norm_g):
    t = x.shape[1]
    tq = _row_tile(t)
    xs = x[0]
    target = loss_target[0]

    mid = [_pad_cols(w_uq[0], LANES).astype(BF16), w_ukv[0].astype(BF16)]
    late = [w_o[0].astype(BF16), w_up[0].astype(BF16), w_down[0].astype(BF16)]
    g_in, = _all_gather([jnp.transpose(w_in[0]).astype(BF16)])
    win = g_in.reshape(IN_COLS, D_MODEL)
    off_ff, off_cq, off_kr = 3 * FOX_WIDTH, 3 * FOX_WIDTH + HEADS, IN_COLS - ROPE
    zeros = lambda n: jnp.zeros((n, D_MODEL), BF16)
    w_qkv = win[:off_ff]
    w_rest = jnp.concatenate([
        win[off_ff:off_cq], zeros(REST_CQ - HEADS), win[off_cq:off_kr],
        zeros(NOPE), win[off_kr:], zeros(LANES - NOPE - ROPE)], axis=0)

    cos, sin = _rope_tables(positions.reshape(t, 1))
    (h1, fq, fk, fv, rest), (wq, wkv) = _in_proj(xs, attn_norm_g, w_qkv, w_rest, comm=_ag_to_all(mid))
    b128 = _pad_cols(b_forget, LANES)
    f2_rows, f2_rep = _forget_cumsum(rest, b128)
    f2_rows = f2_rows.reshape(HEADS, t // tq, 1, tq)
    (fox_o, fox_lse_rows), partly = _attn_fwd(True, fq, fk, fv, f2_rows, comm=_ag_direct(late))
    mq, mk, mkv, cqn, ckvn = _mla_prep(rest, q_norm_g, kv_norm_g, wq, wkv, cos, sin)
    (mla_o, mla_lse_rows), (g_o, g_up, g_down) = _attn_fwd(False, mq, mk, mkv, comm=_ag_forward(partly))
    wo = g_o.reshape(D_MODEL, D_MODEL)
    x1, mixed = _attn_out(xs, fox_o, mla_o, fox_out_g, mla_out_g, wo)
    wup = jnp.transpose(g_up, (1, 0, 2)).reshape(D_MODEL, D_FF)
    wdown = g_down.reshape(D_FF, D_MODEL)
    u, h2, dx2, dx2b, loss8, d_gfin = _mlp_fwd(x1, mlp_norm_g, wup, wdown, final_norm_g.reshape(1, D_MODEL), target)

    du, act, dx1, dx1b, d_gmlp = _mlp_bwd(dx2, u, x1, mlp_norm_g, wup, wdown)
    dw_down = _matmul_tn("dw_down", act, dx2b)
    dw_up = _matmul_tn("dw_up", h2, du, blocks=N_DEV)
    dfox_o, dmla_o, fox_delta_rows, mla_delta_rows, d_gfox, d_gmla = _attn_out_bwd(dx1, fox_o, mla_o, fox_out_g, mla_out_g, wo)
    dw_o = _matmul_tn("dw_o", mixed, dx1b)

    place = jnp.stack([lax.axis_index("c"), 2 * lax.axis_index("x") + lax.axis_index("y")]).astype(jnp.int32)
    names = ("w_in", "w_uq", "w_ukv", "w_o", "w_up", "w_down")
    grads_b = [dw_o.reshape(N_DEV, -1, D_MODEL), dw_up, dw_down.reshape(N_DEV, -1, D_MODEL)]
    (dfq, dfk, dfv, d_fq, d_fk), got_b = _attn_bwd(True, fq, fk, fv, dfox_o, fox_lse_rows, fox_delta_rows,
                                                   f2_rep, comm=_rs_to_sibling(grads_b))
    sums_b = [_rs_sibling_sum("rs_sibling_sum_" + nm, g, l, place) for nm, g, l in zip(names[3:], grads_b, got_b)]
    dz, d_b = _forget_bwd(rest, b128, d_fq, d_fk)
    (dmq, dmkv, dmkr), others_b = _attn_bwd(False, mq, mk, mkv, dmla_o, mla_lse_rows, mla_delta_rows,
                                            comm=_rs_to_chips([s[1] for s in sums_b]))
    drest, dqp, dkvb, d_gq, d_gkv = _mla_prep_bwd(dmq, dmkv, dmkr, dz, rest, q_norm_g, kv_norm_g, wq, wkv, cos, sin)
    dw_uq = _matmul_tn("dw_uq", cqn, dqp, blocks=HEADS)
    dw_ukv = _matmul_tn("dw_ukv", ckvn, dkvb, blocks=HEADS)
    dw_in = _dw_in(dfq, dfk, dfv, drest, h1)

    grads_a = [dw_in.reshape(N_DEV, IN_SHARD, D_MODEL), dw_uq, dw_ukv]
    got_a = _comm_call("rs_sibling_exchange", _rs_to_sibling(grads_a))
    sums_a = [_rs_sibling_sum("rs_sibling_sum_" + nm, g, l, place) for nm, g, l in zip(names[:3], grads_a, got_a)]
    (grad_x, d_gattn), others_a = _in_proj_bwd(xs, attn_norm_g, dx1, dfq, dfk, dfv, drest, w_qkv, w_rest,
                                               comm=_rs_to_chips([s[1] for s in sums_a]))
    sums, others = sums_a + sums_b, list(others_a) + list(others_b)
    sharded = (w_in, w_uq, w_ukv, w_o, w_up, w_down)
    moments_m = (m_w_in, m_w_uq, m_w_ukv, m_w_o, m_w_up, m_w_down)
    moments_v = (v_w_in, v_w_uq, v_w_ukv, v_w_o, v_w_up, v_w_down)
    g_in_t = _rs_final_sum("rs_final_sum_w_in", sums[0][0], others[0])
    big = [_adamw_given("adamw_w_in", jnp.transpose(g_in_t), w_in, m_w_in, v_w_in)]
    for a in range(1, len(names)):
        big.append(_adamw_sharded("adamw_" + names[a], sharded[a], moments_m[a], moments_v[a], sums[a][0], others[a]))
    big_g, big_d, big_m, big_v = [[b[k] for b in big] for k in range(4)]

    as_row = lambda a: a.reshape(1, -1)
    small_w = (attn_norm_g, b_forget, q_norm_g, kv_norm_g, fox_out_g, mla_out_g, mlp_norm_g, final_norm_g)
    small_m = (m_attn_norm_g, m_b_forget, m_q_norm_g, m_kv_norm_g, m_fox_out_g, m_mla_out_g, m_mlp_norm_g, m_final_norm_g)
    small_v = (v_attn_norm_g, v_b_forget, v_q_norm_g, v_kv_norm_g, v_fox_out_g, v_mla_out_g, v_mlp_norm_g, v_final_norm_g)
    total = _small_all_reduce([d_gattn, d_b, d_gq, d_gkv, d_gfox, d_gmla, d_gmlp, d_gfin], loss8)
    small = _adamw_small(total, [as_row(a) for a in small_w], [as_row(a) for a in small_m], [as_row(a) for a in small_v])
    loss = small[0].reshape(())
    s_g, s_d, s_m, s_v = [[small[1 + 4 * r + k].reshape(small_w[r].shape) for r in range(len(small_w))] for k in range(4)]

    def ordered(small_, bigs):
        ga, bf, gq_, gkv_, gfo, gml, gmlp_, gfin_ = small_
        bin_, buq, bukv, bo, bup, bdown = bigs
        return [ga, bin_, bf, gq_, buq, gkv_, bukv, gfo, gml, bo, gmlp_, bup, bdown, gfin_]

    return (loss, grad_x[None], *ordered(s_g, big_g), *ordered(s_d, big_d), *ordered(s_m, big_m), *ordered(s_v, big_v))
```

```python
import math
from typing import Callable, NamedTuple

import numpy as np
import jax
import jax.numpy as jnp
from jax import lax
from jax.experimental import pallas as pl
from jax.experimental.pallas import tpu as pltpu

F32 = jnp.float32
BF16 = jnp.bfloat16
MESH = pl.DeviceIdType.MESH

D_MODEL = 1024
HEADS = 8
HEAD_DIM = 64
FOX_WIDTH = 512
MLA_WIDTH = 512
NOPE = 64
ROPE = 32
QK_DIM = 96
Q_RANK = 384
KV_RANK = 256
D_FF = 4096
IN_COLS = 2216
ROPE_THETA = 10000.0
EPS = 1e-6
FOX_SCALE = 1.0 / math.sqrt(HEAD_DIM)
MLA_SCALE = 1.0 / math.sqrt(QK_DIM)
ADAM_LR = 0.001
ADAM_B1 = 0.9
ADAM_B2 = 0.999
ADAM_EPS = 1e-08
ADAM_WD = 0.01
ADAM_STEP = 10

N_DEV = 8
LANES = 128
REST_COLS = 896
REST_CQ = LANES
REST_CKV = REST_CQ + Q_RANK
REST_KR = REST_CKV + KV_RANK
LOG2E = 1.4426950408889634
LN2 = 0.6931471805599453
FOX_Q_FACTOR = FOX_SCALE * LOG2E
MLA_Q_FACTOR = MLA_SCALE * LOG2E
VMEM_LIMIT = 56 * 1024 * 1024

IN_SHARD = IN_COLS // N_DEV
SMALL_SIZES = (1024, 8, 384, 256, 512, 512, 1024, 1024)
SMALL_ROWS = 16
LOSS_ROW = len(SMALL_SIZES)


def _cparams(*sem):
    return pltpu.CompilerParams(dimension_semantics=sem or None, vmem_limit_bytes=VMEM_LIMIT)


def _row_tile(t):
    return 512 if t >= 2048 else (256 if t >= 512 else 128)


def _dot(a, b):
    return jnp.dot(a, b, preferred_element_type=F32)


def _dot_nt(a, b):
    return lax.dot_general(a, b, (((1,), (1,)), ((), ())), preferred_element_type=F32)


def _dot_tn(a, b):
    return lax.dot_general(a, b, (((0,), (0,)), ((), ())), preferred_element_type=F32)


def _rms(x, g):
    r = lax.rsqrt(jnp.mean(x * x, axis=-1, keepdims=True) + EPS)
    return x * r * g, r


def _rms_bwd(x, g, r, dy):
    xh = x * r
    gdy = dy * g
    dx = r * (gdy - xh * jnp.mean(gdy * xh, axis=-1, keepdims=True))
    return dx, jnp.sum(dy * xh, axis=0, keepdims=True)


def _lane():
    return lax.broadcasted_iota(jnp.int32, (1, LANES), 1)


def _rot(x):
    lane = _lane()
    half = NOPE + ROPE // 2
    first = jnp.logical_and(lane >= NOPE, lane < half)
    second = jnp.logical_and(lane >= half, lane < NOPE + ROPE)
    return jnp.where(first, -pltpu.roll(x, LANES - ROPE // 2, 1), jnp.where(second, pltpu.roll(x, ROPE // 2, 1), 0.0))


def _rope(x, cos, sin):
    return x * cos + _rot(x) * sin


def _rope_bwd(dy, cos, sin):
    return dy * cos - _rot(dy * sin)


def _remote(src, dst, send_sem, recv_sem, to):
    return pltpu.make_async_remote_copy(src_ref=src, dst_ref=dst, send_sem=send_sem, recv_sem=recv_sem,
                                        device_id=to, device_id_type=MESH)


def _hbm_specs(n):
    return [pl.BlockSpec(memory_space=pl.ANY)] * n


def _all_gather(blocks):
    n = len(blocks)

    def body(*refs):
        x_refs, out_refs = refs[:n], refs[n:2 * n]
        send_sems, recv_sems, local_sems = refs[2 * n:]
        x, y, c = lax.axis_index("x"), lax.axis_index("y"), lax.axis_index("c")
        me, sibling = (x, y, c), (x, y, 1 - c)
        chips = [(1 - x, y), (x, 1 - y), (1 - x, 1 - y)]

        def slot(a, px, py, pc):
            return out_refs[a].at[4 * px + 2 * py + pc]

        def copy(a, k, blk, to, src=None):
            return _remote(slot(a, *blk) if src is None else src, slot(a, *blk),
                           send_sems.at[7 * a + k], recv_sems.at[7 * a + k], to)

        mine = [pltpu.make_async_copy(x_refs[a], slot(a, *me), local_sems.at[a]) for a in range(n)]
        first, passed = [], []
        for a in range(n):
            mine[a].start()
            first.append(copy(a, 0, me, sibling, src=x_refs[a]))
            first += [copy(a, 1 + j, me, (*chip, c), src=x_refs[a]) for j, chip in enumerate(chips)]
        for cp in first:
            cp.start()
        for a in range(n):
            for j, chip in enumerate(chips):
                copy(a, 1 + j, (*chip, c), me).wait_recv()
                passed.append(copy(a, 4 + j, (*chip, c), sibling))
                passed[-1].start()
        for a in range(n):
            copy(a, 0, sibling, me).wait_recv()
            for j, chip in enumerate(chips):
                copy(a, 4 + j, (*chip, 1 - c), me).wait_recv()
        for cp in first + passed:
            cp.wait_send()
        for cp in mine:
            cp.wait()

    return pl.pallas_call(
        body, name="all_gather_weights",
        out_shape=[jax.ShapeDtypeStruct((N_DEV,) + b.shape, b.dtype) for b in blocks],
        in_specs=_hbm_specs(n), out_specs=_hbm_specs(n),
        scratch_shapes=[pltpu.SemaphoreType.DMA((7 * n,)), pltpu.SemaphoreType.DMA((7 * n,)), pltpu.SemaphoreType.DMA((n,))],
    )(*blocks)


def _symmetric_comm(inputs, out_shape, aliases, per_array, copies):
    def start(in_refs, out_refs, sems):
        for cp in copies(in_refs, out_refs, *sems):
            cp.start()

    def finish(in_refs, out_refs, sems):
        for cp in copies(in_refs, out_refs, *sems):
            cp.wait()

    n_sems = per_array * len(inputs)
    return _Comm(tuple(inputs), tuple(out_shape), aliases,
                 (pltpu.SemaphoreType.DMA((n_sems,)), pltpu.SemaphoreType.DMA((n_sems,))), start, finish)


def _ag_direct(shards):
    def copies(in_refs, out_refs, send_sems, recv_sems):
        x, y, c = lax.axis_index("x"), lax.axis_index("y"), lax.axis_index("c")
        peers = [(x, y, 1 - c), (1 - x, y, c), (x, 1 - y, c), (1 - x, 1 - y, c)]
        cps = []
        for a in range(len(shards)):
            mine = out_refs[a].at[4 * x + 2 * y + c]
            cps.append(pltpu.make_async_copy(in_refs[a], mine, send_sems.at[5 * a]))
            cps += [_remote(in_refs[a], mine, send_sems.at[5 * a + k], recv_sems.at[5 * a + k], peer)
                    for k, peer in enumerate(peers, start=1)]
        return cps

    return _symmetric_comm(shards, [jax.ShapeDtypeStruct((N_DEV,) + s.shape, s.dtype) for s in shards], {}, 5, copies)


def _ag_to_all(shards):
    def copies(in_refs, out_refs, send_sems, recv_sems):
        x, y, c = lax.axis_index("x"), lax.axis_index("y"), lax.axis_index("c")
        cps = []
        for a in range(len(shards)):
            mine = out_refs[a].at[4 * x + 2 * y + c]
            cps.append(pltpu.make_async_copy(in_refs[a], mine, send_sems.at[N_DEV * a]))
            for k in range(1, N_DEV):
                peer = (x ^ (k >> 2), y ^ ((k >> 1) & 1), c ^ (k & 1))
                cps.append(_remote(in_refs[a], mine, send_sems.at[N_DEV * a + k], recv_sems.at[N_DEV * a + k], peer))
        return cps

    return _symmetric_comm(shards, [jax.ShapeDtypeStruct((N_DEV,) + s.shape, s.dtype) for s in shards], {}, N_DEV, copies)


def _ag_forward(gathered):
    def copies(in_refs, out_refs, send_sems, recv_sems):
        x, y, c = lax.axis_index("x"), lax.axis_index("y"), lax.axis_index("c")
        chips = [(1 - x, y), (x, 1 - y), (1 - x, 1 - y)]
        return [_remote(out_refs[a].at[4 * cx + 2 * cy + c], out_refs[a].at[4 * cx + 2 * cy + c],
                        send_sems.at[3 * a + j], recv_sems.at[3 * a + j], (x, y, 1 - c))
                for a in range(len(gathered)) for j, (cx, cy) in enumerate(chips)]

    shapes = [jax.ShapeDtypeStruct(g.shape, g.dtype) for g in gathered]
    return _symmetric_comm(gathered, shapes, {a: a for a in range(len(gathered))}, 3, copies)


def _rs_to_sibling(grads):
    def copies(in_refs, out_refs, send_sems, recv_sems):
        x, y, c = lax.axis_index("x"), lax.axis_index("y"), lax.axis_index("c")
        return [_remote(in_refs[a].at[2 * q + 1 - c], out_refs[a].at[q], send_sems.at[4 * a + q], recv_sems.at[4 * a + q], (x, y, 1 - c))
                for a in range(len(grads)) for q in range(4)]

    return _symmetric_comm(grads, [jax.ShapeDtypeStruct((4,) + g.shape[1:], g.dtype) for g in grads], {}, 4, copies)


def _rs_to_chips(parts):
    def copies(in_refs, out_refs, send_sems, recv_sems):
        x, y, c = lax.axis_index("x"), lax.axis_index("y"), lax.axis_index("c")
        chips = [(1 - x, y), (x, 1 - y), (1 - x, 1 - y)]
        return [_remote(in_refs[a].at[2 * cx + cy], out_refs[a].at[k], send_sems.at[3 * a + k], recv_sems.at[3 * a + k], (cx, cy, c))
                for a in range(len(parts)) for k, (cx, cy) in enumerate(chips)]

    return _symmetric_comm(parts, [jax.ShapeDtypeStruct((3,) + p.shape[1:], p.dtype) for p in parts], {}, 3, copies)


def _comm_call(name, comm):
    n_in, n_out = len(comm.inputs), len(comm.out_shape)

    def body(*refs):
        ins, outs, sems = refs[:n_in], refs[n_in:n_in + n_out], refs[n_in + n_out:]
        comm.start(ins, outs, sems)
        comm.finish(ins, outs, sems)

    return pl.pallas_call(
        body, name=name, out_shape=list(comm.out_shape), in_specs=_hbm_specs(n_in), out_specs=_hbm_specs(n_out),
        scratch_shapes=list(comm.scratch), input_output_aliases=dict(comm.aliases),
    )(*comm.inputs)


def _small_all_reduce(parts, loss8):
    n = len(parts)

    def body(*refs):
        p_refs, loss_ref, out_ref, pack, land, send_sems, recv_sems = refs[:n], *refs[n:]
        x, y, c = lax.axis_index("x"), lax.axis_index("y"), lax.axis_index("c")
        me = 4 * x + 2 * y + c
        pack[...] = jnp.zeros_like(pack)
        for r, ref in enumerate(p_refs):
            pack[r:r + 1, 0:ref.shape[1]] = ref[...]
        pack[LOSS_ROW:LOSS_ROW + 1, 0:LANES] = loss_ref[0:1, :]
        land[me] = pack[...]
        cps = []
        for k in range(1, N_DEV):
            peer = (x ^ (k >> 2), y ^ ((k >> 1) & 1), c ^ (k & 1))
            cps.append(_remote(pack, land.at[me], send_sems.at[k - 1], recv_sems.at[k - 1], peer))
        for cp in cps:
            cp.start()
        for cp in cps:
            cp.wait()
        acc = land[0]
        for d in range(1, N_DEV):
            acc = acc + land[d]
        out_ref[...] = acc

    vmem = pl.BlockSpec(memory_space=pltpu.VMEM)
    return pl.pallas_call(
        body, name="small_all_reduce",
        out_shape=jax.ShapeDtypeStruct((SMALL_ROWS, D_MODEL), F32),
        in_specs=[vmem] * (n + 1), out_specs=vmem,
        scratch_shapes=[pltpu.VMEM((SMALL_ROWS, D_MODEL), F32), pltpu.VMEM((N_DEV, SMALL_ROWS, D_MODEL), F32),
                        pltpu.SemaphoreType.DMA((N_DEV - 1,)), pltpu.SemaphoreType.DMA((N_DEV - 1,))],
    )(*parts, loss8)


def _rs_sibling_sum(name, grad, got, place):
    _, rows, cols = grad.shape

    def body(place_ref, g_ref, l_ref, own_ref, b_ref):
        s = g_ref[...] + l_ref[...]
        b_ref[...] = s.astype(BF16)

        @pl.when(pl.program_id(0) == place_ref[1])
        def _():
            own_ref[...] = s

    by_chip = pl.BlockSpec((None, rows, cols), lambda q, place_ref: (q, 0, 0))
    return pl.pallas_call(
        body, name=name,
        grid_spec=pltpu.PrefetchScalarGridSpec(
            num_scalar_prefetch=1, grid=(4,),
            in_specs=[pl.BlockSpec((None, rows, cols), lambda q, place_ref: (2 * q + place_ref[0], 0, 0)), by_chip],
            out_specs=[pl.BlockSpec((rows, cols), lambda q, place_ref: (0, 0)), by_chip]),
        out_shape=(jax.ShapeDtypeStruct((rows, cols), F32), jax.ShapeDtypeStruct((4, rows, cols), BF16)),
        compiler_params=_cparams("arbitrary"),
    )(place, grad, got)


def _adamw_math(w, g, m, v):
    m2 = ADAM_B1 * m + (1.0 - ADAM_B1) * g
    v2 = ADAM_B2 * v + (1.0 - ADAM_B2) * (g * g)
    m_hat = m2 / (1.0 - ADAM_B1 ** ADAM_STEP)
    v_hat = v2 / (1.0 - ADAM_B2 ** ADAM_STEP)
    delta = -ADAM_LR * (m_hat / (jnp.sqrt(v_hat) + ADAM_EPS) + ADAM_WD * w)
    return delta, m2, v2


def _update_tile(rows):
    return 256 if rows % 256 == 0 else rows


def _rs_final_sum(name, own, got):
    def body(o_ref, r_ref, g_out):
        g = o_ref[...]
        for k in range(3):
            g = g + r_ref[k].astype(F32)
        g_out[...] = g

    return pl.pallas_call(body, name=name, out_shape=jax.ShapeDtypeStruct(own.shape, F32))(own, got)


def _adamw_sharded(name, w, m, v, own, got):
    _, rows, cols = w.shape
    tr = _update_tile(rows)

    def body(o_ref, r_ref, w_ref, m_ref, v_ref, g_out, d_out, m_out, v_out):
        g = o_ref[:, 0:cols]
        for k in range(3):
            g = g + r_ref[k, :, 0:cols].astype(F32)
        d, m2, v2 = _adamw_math(w_ref[0], g, m_ref[0], v_ref[0])
        g_out[0] = g
        d_out[0] = d
        m_out[0] = m2
        v_out[0] = v2

    mine = pl.BlockSpec((1, tr, cols), lambda i: (0, i, 0))
    shp = jax.ShapeDtypeStruct(w.shape, F32)
    wide = own.shape[1]
    return pl.pallas_call(
        body, name=name, grid=(rows // tr,), out_shape=(shp,) * 4,
        in_specs=[pl.BlockSpec((tr, wide), lambda i: (i, 0)), pl.BlockSpec((3, tr, wide), lambda i: (0, i, 0)),
                  mine, mine, mine],
        out_specs=[mine] * 4,
        compiler_params=_cparams("parallel"),
    )(own, got, w, m, v)


def _adamw_given(name, g, w, m, v):
    _, rows, cols = w.shape
    tr = _update_tile(rows)

    def body(g_ref, w_ref, m_ref, v_ref, g_out, d_out, m_out, v_out):
        g = g_ref[...]
        d, m2, v2 = _adamw_math(w_ref[0], g, m_ref[0], v_ref[0])
        g_out[0] = g
        d_out[0] = d
        m_out[0] = m2
        v_out[0] = v2

    own = pl.BlockSpec((1, tr, cols), lambda i: (0, i, 0))
    shp = jax.ShapeDtypeStruct(w.shape, F32)
    return pl.pallas_call(
        body, name=name, grid=(rows // tr,), out_shape=(shp,) * 4,
        in_specs=[pl.BlockSpec((tr, cols), lambda i: (i, 0)), own, own, own], out_specs=[own] * 4,
        compiler_params=_cparams("parallel"),
    )(g, w, m, v)


def _adamw_small(total, ws, ms, vs):
    n = len(ws)

    def body(*refs):
        t_ref = refs[0]
        w_refs, m_refs, v_refs = refs[1:1 + n], refs[1 + n:1 + 2 * n], refs[1 + 2 * n:1 + 3 * n]
        outs = refs[1 + 3 * n:]
        outs[0][...] = t_ref[LOSS_ROW:LOSS_ROW + 1, 0:1]
        for r in range(n):
            g = t_ref[r:r + 1, 0:w_refs[r].shape[1]]
            d, m2, v2 = _adamw_math(w_refs[r][...], g, m_refs[r][...], v_refs[r][...])
            for k, val in enumerate((g, d, m2, v2)):
                outs[1 + 4 * r + k][...] = val

    vmem = pl.BlockSpec(memory_space=pltpu.VMEM)
    out_shape = [jax.ShapeDtypeStruct((1, 1), F32)]
    for w in ws:
        out_shape += [jax.ShapeDtypeStruct(w.shape, F32)] * 4
    return pl.pallas_call(
        body, name="adamw_small", out_shape=out_shape,
        in_specs=[vmem] * (1 + 3 * n), out_specs=[vmem] * len(out_shape),
    )(total, *ws, *ms, *vs)


def _rope_tables(pos_col):
    t = pos_col.shape[0]
    inv = (np.float32(ROPE_THETA) ** (-np.arange(0, ROPE, 2, dtype=np.float32) / np.float32(ROPE))).astype(np.float32)
    freq = np.zeros((1, LANES), np.float32)
    freq[0, NOPE:NOPE + ROPE // 2] = inv
    freq[0, NOPE + ROPE // 2:NOPE + ROPE] = inv
    tm = _row_tile(t)

    def body(p_ref, f_ref, c_ref, s_ref):
        ang = p_ref[...].astype(F32) * f_ref[...]
        c_ref[...] = jnp.cos(ang)
        s_ref[...] = jnp.sin(ang)

    shp = jax.ShapeDtypeStruct((t, LANES), F32)
    return pl.pallas_call(
        body, name="rope_tables", grid=(t // tm,), out_shape=(shp, shp),
        in_specs=[pl.BlockSpec((tm, 1), lambda i: (i, 0)), pl.BlockSpec((1, LANES), lambda i: (0, 0))],
        out_specs=(pl.BlockSpec((tm, LANES), lambda i: (i, 0)),) * 2,
        compiler_params=_cparams("parallel"),
    )(pos_col, jnp.asarray(freq))


def _in_proj(x, g, w_qkv, w_rest, comm=None):
    t = x.shape[0]
    tm = _row_tile(t)

    def main(ins, outs, scr):
        x_ref, g_ref, wq_ref, wr_ref = ins
        h_ref, fq_ref, fk_ref, fv_ref, r_ref = outs
        h, _ = _rms(x_ref[...], g_ref[...])
        hb = h.astype(BF16)
        h_ref[...] = hb
        for n, (ref, factor) in enumerate(((fq_ref, FOX_Q_FACTOR), (fk_ref, None), (fv_ref, None))):
            part = _dot_nt(hb, wq_ref[n * FOX_WIDTH:(n + 1) * FOX_WIDTH, :])
            ref[...] = (part if factor is None else part * factor).astype(BF16)
        r_ref[...] = _dot_nt(hb, wr_ref[...])

    row = lambda n: pl.BlockSpec((tm, n), lambda i: (i, 0))
    full = lambda a: pl.BlockSpec(a.shape, lambda i: (0,) * a.ndim)
    return _hosted_call(
        "in_proj", main, (t // tm,), [x, g, w_qkv, w_rest], [row(D_MODEL), full(g), full(w_qkv), full(w_rest)],
        (jax.ShapeDtypeStruct((t, D_MODEL), BF16),) + (jax.ShapeDtypeStruct((t, FOX_WIDTH), BF16),) * 3
        + (jax.ShapeDtypeStruct((t, REST_COLS), F32),),
        (row(D_MODEL), row(FOX_WIDTH), row(FOX_WIDTH), row(FOX_WIDTH), row(REST_COLS)), [], comm)


def _log_sigmoid(z):
    return jnp.minimum(z, 0.0) - jnp.log(1.0 + jnp.exp(-jnp.abs(z)))


def _split3(v):
    hi = v.astype(BF16)
    r1 = v - hi.astype(F32)
    mid = r1.astype(BF16)
    lo = (r1 - mid.astype(F32)).astype(BF16)
    return hi, mid, lo


def _scan_tile(t):
    return 512 if t >= 2048 else (256 if t >= 256 else t)


def _forget_cumsum(rest, b128):
    t = rest.shape[0]
    tb = _scan_tile(t)

    def body(r_ref, b_ref, row_ref, rep_ref, f_sc, carry):
        @pl.when(pl.program_id(0) == 0)
        def _():
            carry[...] = jnp.zeros_like(carry)
        lf = _log_sigmoid(r_ref[...] + b_ref[...])
        tri = (lax.broadcasted_iota(jnp.int32, (tb, tb), 0) >= lax.broadcasted_iota(jnp.int32, (tb, tb), 1)).astype(BF16)
        hi, mid, lo = _split3(lf)
        f_sc[...] = (_dot(tri, hi) + _dot(tri, mid)) + _dot(tri, lo) + carry[...]
        carry[...] = f_sc[tb - 1:tb, :]
        f2 = f_sc[...] * LOG2E
        row_ref[...] = jnp.transpose(f2)[0:HEADS, :]
        lane = _lane()
        for h in range(HEADS):
            col = jnp.sum(jnp.where(lane == h, f2, 0.0), axis=1, keepdims=True)
            rep_ref[h] = jnp.broadcast_to(col, (tb, LANES))

    return pl.pallas_call(
        body, name="forget_cumsum", grid=(t // tb,),
        out_shape=(jax.ShapeDtypeStruct((HEADS, t), F32), jax.ShapeDtypeStruct((HEADS, t, LANES), F32)),
        in_specs=[pl.BlockSpec((tb, LANES), lambda i: (i, 0)), pl.BlockSpec((1, LANES), lambda i: (0, 0))],
        out_specs=(pl.BlockSpec((HEADS, tb), lambda i: (0, i)), pl.BlockSpec((HEADS, tb, LANES), lambda i: (0, i, 0))),
        scratch_shapes=[pltpu.VMEM((tb, LANES), F32), pltpu.VMEM((1, LANES), F32)],
        compiler_params=_cparams("arbitrary"),
    )(rest, b128)


def _forget_bwd(rest, b128, d_fq, d_fk):
    t = rest.shape[0]
    tb = _scan_tile(t)
    nb = t // tb

    def body(r_ref, b_ref, dfq_ref, dfk_ref, dz_ref, db_ref, carry):
        @pl.when(pl.program_id(0) == 0)
        def _():
            carry[...] = jnp.zeros_like(carry)
            db_ref[...] = jnp.zeros_like(db_ref)
        tri = (lax.broadcasted_iota(jnp.int32, (tb, tb), 0) <= lax.broadcasted_iota(jnp.int32, (tb, tb), 1)).astype(BF16)
        lane = _lane()
        df = jnp.zeros((tb, LANES), F32)
        for h in range(HEADS):
            df = df + jnp.where(lane == h, dfq_ref[h] + dfk_ref[h], 0.0)
        hi, mid, lo = _split3(df)
        dlf = (_dot(tri, hi) + _dot(tri, mid)) + _dot(tri, lo) + carry[...]
        z = r_ref[...] + b_ref[...]
        dz = dlf / (1.0 + jnp.exp(z))
        dz_ref[...] = dz
        db_ref[...] += jnp.sum(dz, axis=0, keepdims=True)
        carry[...] = carry[...] + jnp.sum(df, axis=0, keepdims=True)

    rev = lambda i: (nb - 1 - i, 0)
    rev3 = pl.BlockSpec((HEADS, tb, LANES), lambda i: (0, nb - 1 - i, 0))
    return pl.pallas_call(
        body, name="forget_bwd", grid=(nb,),
        out_shape=(jax.ShapeDtypeStruct((t, LANES), F32), jax.ShapeDtypeStruct((1, LANES), F32)),
        in_specs=[pl.BlockSpec((tb, LANES), rev), pl.BlockSpec((1, LANES), lambda i: (0, 0)), rev3, rev3],
        out_specs=(pl.BlockSpec((tb, LANES), rev), pl.BlockSpec((1, LANES), lambda i: (0, 0))),
        scratch_shapes=[pltpu.VMEM((1, LANES), F32)],
        compiler_params=_cparams("arbitrary"),
    )(rest, b128, d_fq, d_fk)


def _mla_prep(rest, gq, gkv, wq, wkv, cos, sin):
    t = rest.shape[0]
    tm = _row_tile(t)

    def body(r_ref, gq_ref, gkv_ref, wq_ref, wkv_ref, c_ref, s_ref, q_ref, k_ref, kv_ref, cq_ref, ckv_ref):
        cos_, sin_ = c_ref[...], s_ref[...]
        cq, _ = _rms(r_ref[:, REST_CQ:REST_CKV], gq_ref[...])
        ckv, _ = _rms(r_ref[:, REST_CKV:REST_KR], gkv_ref[...])
        cqb, ckvb = cq.astype(BF16), ckv.astype(BF16)
        cq_ref[...] = cqb
        ckv_ref[...] = ckvb
        k_rope = _rope(r_ref[:, REST_KR:REST_COLS], cos_, sin_)
        lo = _lane() < NOPE
        q_all = _dot(cqb, wq_ref[...])
        kv_all = _dot(ckvb, wkv_ref[...])
        for h in range(HEADS):
            head = slice(h * LANES, (h + 1) * LANES)
            q_ref[h] = (_rope(q_all[:, head], cos_, sin_) * MLA_Q_FACTOR).astype(BF16)
            kv = kv_all[:, head]
            kv_ref[h] = kv.astype(BF16)
            k_ref[h] = (jnp.where(lo, kv, 0.0) + k_rope).astype(BF16)

    row = lambda n: pl.BlockSpec((tm, n), lambda i: (i, 0))
    full = lambda a: pl.BlockSpec(a.shape, lambda i: (0,) * a.ndim)
    heads = pl.BlockSpec((HEADS, tm, LANES), lambda i: (0, i, 0))
    hshape = jax.ShapeDtypeStruct((HEADS, t, LANES), BF16)
    return pl.pallas_call(
        body, name="mla_prep", grid=(t // tm,),
        out_shape=(hshape, hshape, hshape, jax.ShapeDtypeStruct((t, Q_RANK), BF16), jax.ShapeDtypeStruct((t, KV_RANK), BF16)),
        in_specs=[row(REST_COLS), full(gq), full(gkv), full(wq), full(wkv), row(LANES), row(LANES)],
        out_specs=(heads, heads, heads, row(Q_RANK), row(KV_RANK)),
        compiler_params=_cparams("parallel"),
    )(rest, gq, gkv, wq, wkv, cos, sin)


def _tile_lanes(x, n):
    return jnp.tile(x, (1, n)) if n > 1 else x


class _Comm(NamedTuple):
    inputs: tuple
    out_shape: tuple
    aliases: dict
    scratch: tuple
    start: Callable
    finish: Callable


def _hosted_call(name, main, grid, args, in_specs, out_shape, out_specs, scratch, comm):
    n_in, n_out, n_scr = len(args), len(out_shape), len(scratch)
    c_in = list(comm.inputs) if comm else []
    c_out = list(comm.out_shape) if comm else []

    def at_step(which):
        hit = pl.program_id(0) == which[0]
        for axis in range(1, len(grid)):
            hit = jnp.logical_and(hit, pl.program_id(axis) == which[axis])
        return hit

    def body(*refs):
        bounds = [0, n_in, len(c_in), n_out, len(c_out), n_scr]
        starts = [sum(bounds[:k + 1]) for k in range(len(bounds))]
        ins, cins, outs, couts, scr = [refs[a:b] for a, b in zip(starts[:-1], starts[1:])]
        sems = refs[starts[-1]:]
        if comm:
            @pl.when(at_step([0] * len(grid)))
            def _():
                comm.start(cins, couts, sems)
        main(ins, outs, scr)
        if comm:
            @pl.when(at_step([n - 1 for n in grid]))
            def _():
                comm.finish(cins, couts, sems)

    res = pl.pallas_call(
        body, name=name, grid=grid,
        out_shape=list(out_shape) + c_out,
        in_specs=list(in_specs) + _hbm_specs(len(c_in)),
        out_specs=list(out_specs) + _hbm_specs(len(c_out)),
        scratch_shapes=list(scratch) + (list(comm.scratch) if comm else []),
        input_output_aliases={n_in + i: n_out + o for i, o in comm.aliases.items()} if comm else {},
        compiler_params=_cparams(*(["arbitrary"] * len(grid))),
    )(*args, *c_in)
    return res[:n_out], res[n_out:]


def _stat_rows(x):
    return jnp.transpose(x)[0:8, :]


FWD_HEADS = 4


def _attn_fwd(fox, q, k, v, f2_rows=None, comm=None):
    t = q.shape[0] if fox else q.shape[1]
    tq = _row_tile(t)
    nq = t // tq
    nh = FWD_HEADS
    wide = (nh // 2) * LANES

    def main(ins, outs, scr):
        q_ref, k_ref, v_ref = ins[:3]
        fr_ref = ins[3] if fox else None
        o_ref, lset_ref = outs
        m_sc, acc_sc = scr
        i = pl.program_id(1)
        lo = _lane() < HEAD_DIM
        hi = jnp.logical_not(lo)
        zero, one = jnp.zeros((), BF16), jnp.ones((), BF16)
        lanes_of = lambda h: slice((h // 2) * LANES, (h // 2 + 1) * LANES)
        if fox:
            qs = [jnp.where(lo if h % 2 == 0 else hi, q_ref[:, lanes_of(h)], zero) for h in range(nh)]
            sum_lanes = [hi if h % 2 == 0 else lo for h in range(nh)]
        else:
            qs = [q_ref[h] for h in range(nh)]
            sum_lanes = [lo] * nh
        m_sc[...] = jnp.full_like(m_sc, -jnp.inf)
        acc_sc[...] = jnp.zeros_like(acc_sc)

        def block(j, r0, nr, c0, nc, seen_from):
            rows = slice(r0, r0 + nr)
            sl = pl.ds(pl.multiple_of(j * tq + c0, math.gcd(tq, c0) if c0 else tq), nc)
            if seen_from is not None:
                seen = (lax.broadcasted_iota(jnp.int32, (nr, nc), 1)
                        <= lax.broadcasted_iota(jnp.int32, (nr, nc), 0) + seen_from)
            for h in range(nh):
                kj, vj = (k_ref[sl, lanes_of(h)], v_ref[sl, lanes_of(h)]) if fox else (k_ref[h, sl, :], v_ref[h, sl, :])
                s = _dot_nt(qs[h][rows], kj)
                if fox:
                    s = s - fr_ref[h, j, :, c0:c0 + nc]
                if seen_from is not None:
                    s = jnp.where(seen, s, -jnp.inf)
                m_prev = m_sc[h, rows]
                m_new = jnp.maximum(m_prev, jnp.max(s, axis=1, keepdims=True))
                p = jnp.exp2((s - _tile_lanes(m_new, nc // LANES)).astype(BF16))
                vj = jnp.where(sum_lanes[h], one, vj)
                acc_sc[h, rows] = jnp.exp2(m_prev - m_new) * acc_sc[h, rows] + _dot(p, vj)
                m_sc[h, rows] = m_new

        def loop_body(j, carry):
            block(j, 0, tq, 0, tq, None)
            return carry

        lax.fori_loop(0, i, loop_body, 0)
        half = tq // 2
        if half % LANES == 0:
            block(i, 0, half, 0, half, 0)
            block(i, half, half, 0, tq, half)
        else:
            block(i, 0, tq, 0, tq, 0)
        res = []
        for h in range(nh):
            acc = acc_sc[h]
            swapped = pltpu.roll(acc, HEAD_DIM, 1)
            res.append(acc / swapped)
            lse2 = m_sc[h] + jnp.log(jnp.where(sum_lanes[h], acc, swapped)) * LOG2E
            lset_ref[h, 0] = _stat_rows(lse2)
        for pr in range(nh // 2):
            even = res[2 * pr] if fox else pltpu.roll(res[2 * pr], HEAD_DIM, 1)
            o_ref[:, pr * LANES:(pr + 1) * LANES] = jnp.where(lo, even, res[2 * pr + 1])

    if fox:
        in_specs = [pl.BlockSpec((tq, wide), lambda g, i: (i, g))] + [pl.BlockSpec((t, wide), lambda g, i: (0, g))] * 2
        in_specs += [pl.BlockSpec((nh, nq, 1, tq), lambda g, i: (g, 0, 0, 0))]
        args = [q, k, v, f2_rows]
    else:
        in_specs = [pl.BlockSpec((nh, tq, LANES), lambda g, i: (g, i, 0))] + [pl.BlockSpec((nh, t, LANES), lambda g, i: (g, 0, 0))] * 2
        args = [q, k, v]
    return _hosted_call(
        "fox_attn_fwd" if fox else "mla_attn_fwd", main, (HEADS // nh, nq), args, in_specs,
        (jax.ShapeDtypeStruct((t, 4 * LANES), F32), jax.ShapeDtypeStruct((HEADS, nq, 8, tq), F32)),
        (pl.BlockSpec((tq, wide), lambda g, i: (i, g)), pl.BlockSpec((nh, 1, 8, tq), lambda g, i: (g, i, 0, 0))),
        [pltpu.VMEM((nh, tq, LANES), F32), pltpu.VMEM((nh, tq, LANES), F32)], comm)


def _head_do(fox, hh, do2, lo):
    if fox:
        return jnp.where(lo if hh == 0 else jnp.logical_not(lo), do2, 0.0)
    return jnp.where(lo, 0.0, pltpu.roll(do2, HEAD_DIM, 1) if hh == 0 else do2)


def _attn_bwd(fox, q, k, v, do, lse_rows, delta_rows, f2_rep=None, comm=None):
    t = q.shape[0] if fox else q.shape[1]
    tq = _row_tile(t)
    nq = t // tq
    scale = FOX_SCALE if fox else MLA_SCALE

    def main(ins, outs, scr):
        if fox:
            q_ref, k_ref, v_ref, f_ref, do_ref, lse_ref, dl_ref = ins
            dq_ref, dk_ref, dv_ref, dfq_ref, dfk_ref = outs
        else:
            q_ref, k_ref, v_ref, do_ref, lse_ref, dl_ref = ins
            dq_ref, dkv_ref, dkr_ref = outs
        dq_sc, dk_sc, dv_sc = scr
        j = pl.program_id(1)
        lane = _lane()
        lo = lane < HEAD_DIM
        hi = jnp.logical_not(lo)
        zero, one = jnp.zeros((), BF16), jnp.ones((), BF16)

        @pl.when(j == 0)
        def _():
            dq_sc[...] = jnp.zeros_like(dq_sc)

        dk_sc[...] = jnp.zeros_like(dk_sc)
        dv_sc[...] = jnp.zeros_like(dv_sc)

        def block(i, r0, nr, c0, nc, masked):
            rows, cols = slice(r0, r0 + nr), slice(c0, c0 + nc)
            sl = pl.ds(pl.multiple_of(i * tq + c0, math.gcd(tq, c0) if c0 else tq), nc)
            do_i = do_ref[sl, :]
            if masked:
                seen = lax.broadcasted_iota(jnp.int32, (nr, nc), 1) >= lax.broadcasted_iota(jnp.int32, (nr, nc), 0)
            for hh in range(2):
                kj = k_ref[rows, :] if fox else k_ref[hh, rows, :]
                vj = v_ref[rows, :] if fox else v_ref[hh, rows, :]
                qi = jnp.where(lo if hh == 0 else hi, q_ref[sl, :], zero) if fox else q_ref[hh, sl, :]
                dob = _head_do(fox, hh, do_i, lo).astype(BF16)
                st = _dot_nt(kj, qi)
                if fox:
                    st = st - _tile_lanes(f_ref[hh, rows, :], nc // LANES)
                if masked:
                    st = jnp.where(seen, st, -jnp.inf)
                pt = jnp.exp2(st - lse_ref[hh, i, 0:1, cols])
                dpt = _dot_nt(vj, dob)
                dst = (pt * (dpt - dl_ref[hh, i, 0:1, cols])).astype(BF16)
                dv_sc[hh, rows] += _dot(pt.astype(BF16), dob)
                if fox:
                    other = hi if hh == 0 else lo
                    qi = jnp.where(other, one, qi)
                    kj = jnp.where(other, one, kj)
                dk_sc[hh, rows] += _dot(dst, qi)
                dq_sc[hh, sl, :] += _dot_tn(dst, kj)

        def loop_body(i, carry):
            block(i, 0, tq, 0, tq, False)
            return carry

        half = tq // 2
        if half % LANES == 0:
            block(j, 0, half, 0, tq, True)
            block(j, half, half, half, half, True)
        else:
            block(j, 0, tq, 0, tq, True)
        lax.fori_loop(j + 1, nq, loop_body, 0)
        if fox:
            dk_ref[...] = (jnp.where(lo, dk_sc[0], dk_sc[1]) * LN2).astype(BF16)
            dv_ref[...] = (dv_sc[0] + dv_sc[1]).astype(BF16)
            for hh in range(2):
                dk = dk_sc[hh]
                dfk_ref[hh] = -jnp.where(hi if hh == 0 else lo, dk, pltpu.roll(dk, HEAD_DIM, 1))
        else:
            rope_lanes = jnp.logical_and(lane >= NOPE, lane < NOPE + ROPE)
            dkr = jnp.zeros((tq, LANES), F32)
            for hh in range(2):
                dk = dk_sc[hh] * LN2
                dkv_ref[hh] = jnp.where(lo, dk, dv_sc[hh])
                dkr = dkr + jnp.where(rope_lanes, dk, 0.0)
            dkr_ref[0] = dkr

        @pl.when(j == nq - 1)
        def _():
            for i in range(nq):
                rows = slice(i * tq, (i + 1) * tq)
                if fox:
                    dq_ref[rows, :] = (jnp.where(lo, dq_sc[0, rows, :], dq_sc[1, rows, :]) * scale).astype(BF16)
                    for hh in range(2):
                        acc = dq_sc[hh, rows, :]
                        dfq_ref[hh, rows, :] = jnp.where(hi if hh == 0 else lo, acc, pltpu.roll(acc, HEAD_DIM, 1))
                else:
                    for hh in range(2):
                        dq_ref[hh, rows, :] = dq_sc[hh, rows, :] * scale

    stat = pl.BlockSpec((2, tq, LANES), lambda p, j: (p, j, 0))
    stat_all = pl.BlockSpec((2, t, LANES), lambda p, j: (p, 0, 0))
    rows4 = pl.BlockSpec((2, nq, 8, tq), lambda p, j: (p, 0, 0, 0))
    pair = pl.BlockSpec((tq, LANES), lambda p, j: (j, p))
    pair_all = pl.BlockSpec((t, LANES), lambda p, j: (0, p))
    if fox:
        in_specs = [pair_all, pair, pair, stat]
        args = [q, k, v, f2_rep]
    else:
        in_specs = [stat_all, stat, stat]
        args = [q, k, v]
    in_specs += [pair_all, rows4, rows4]
    args += [do, lse_rows, delta_rows]
    heads_f32 = jax.ShapeDtypeStruct((HEADS, t, LANES), F32)
    if fox:
        wide = jax.ShapeDtypeStruct((t, 4 * LANES), BF16)
        out_shape = (wide, wide, wide, heads_f32, heads_f32)
        out_specs = (pair_all, pair, pair, stat_all, stat)
    else:
        out_shape = (heads_f32, heads_f32, jax.ShapeDtypeStruct((HEADS // 2, t, LANES), F32))
        out_specs = (stat_all, stat, pl.BlockSpec((1, tq, LANES), lambda p, j: (p, j, 0)))
    acc = pltpu.VMEM((2, tq, LANES), F32)
    return _hosted_call("fox_attn_bwd" if fox else "mla_attn_bwd", main, (HEADS // 2, nq), args, in_specs,
                        out_shape, out_specs, [pltpu.VMEM((2, t, LANES), F32), acc, acc], comm)


def _attn_out(x, fox_o, mla_o, gf, gm, w_o):
    t = x.shape[0]
    tm = _row_tile(t)

    def body(x_ref, f_ref, m_ref, gf_ref, gm_ref, w_ref, x1_ref, mix_ref):
        nf, _ = _rms(f_ref[...], gf_ref[...])
        nm, _ = _rms(m_ref[...], gm_ref[...])
        nfb, nmb = nf.astype(BF16), nm.astype(BF16)
        mix_ref[:, :FOX_WIDTH] = nfb
        mix_ref[:, FOX_WIDTH:] = nmb
        x1_ref[...] = x_ref[...] + _dot(nfb, w_ref[:FOX_WIDTH, :]) + _dot(nmb, w_ref[FOX_WIDTH:, :])

    row = lambda n: pl.BlockSpec((tm, n), lambda i: (i, 0))
    full = lambda a: pl.BlockSpec(a.shape, lambda i: (0,) * a.ndim)
    return pl.pallas_call(
        body, name="attn_out", grid=(t // tm,),
        out_shape=(jax.ShapeDtypeStruct((t, D_MODEL), F32), jax.ShapeDtypeStruct((t, D_MODEL), BF16)),
        in_specs=[row(D_MODEL), row(FOX_WIDTH), row(MLA_WIDTH), full(gf), full(gm), full(w_o)],
        out_specs=(row(D_MODEL), row(D_MODEL)),
        compiler_params=_cparams("parallel"),
    )(x, fox_o, mla_o, gf, gm, w_o)


def _mlp_tile(t):
    return 256 if t >= 2048 else 128


def _resident(a):
    return pl.BlockSpec(a.shape, lambda i: (0,) * a.ndim, pipeline_mode=pl.Buffered(1))


FF_CHUNK = 512


def _mlp_fwd(x1, g_mlp, w_up, w_down, g_fin, target):
    t = x1.shape[0]
    tm = _mlp_tile(t)

    def body(x_ref, g_ref, wu_ref, wd_ref, gf_ref, t_ref, u_ref, h_ref, dx_ref, dxb_ref, loss_ref, dg_ref, a_sc):
        @pl.when(pl.program_id(0) == 0)
        def _():
            loss_ref[...] = jnp.zeros_like(loss_ref)
            dg_ref[...] = jnp.zeros_like(dg_ref)

        x = x_ref[...]
        h, _ = _rms(x, g_ref[...])
        hb = h.astype(BF16)
        h_ref[...] = hb
        for f in range(D_FF // FF_CHUNK):
            sl = slice(f * FF_CHUNK, (f + 1) * FF_CHUNK)
            u = _dot(hb, wu_ref[:, sl])
            u_ref[:, sl] = u
            r = jnp.maximum(u, 0.0)
            a_sc[:, sl] = (r * r).astype(BF16)
        x2 = x + _dot(a_sc[...], wd_ref[...])
        y, r2 = _rms(x2, gf_ref[...])
        err = y - t_ref[...]
        loss_ref[...] += 0.5 * jnp.sum(jnp.mean(err * err, axis=-1, keepdims=True))
        dx, dg = _rms_bwd(x2, gf_ref[...], r2, err * (1.0 / D_MODEL))
        dx_ref[...] = dx
        dxb_ref[...] = dx.astype(BF16)
        dg_ref[...] += dg

    row = lambda n: pl.BlockSpec((tm, n), lambda i: (i, 0))
    vec = pl.BlockSpec((1, D_MODEL), lambda i: (0, 0))
    return pl.pallas_call(
        body, name="mlp_fwd", grid=(t // tm,),
        out_shape=(jax.ShapeDtypeStruct((t, D_FF), F32), jax.ShapeDtypeStruct((t, D_MODEL), BF16),
                   jax.ShapeDtypeStruct((t, D_MODEL), F32), jax.ShapeDtypeStruct((t, D_MODEL), BF16),
                   jax.ShapeDtypeStruct((8, LANES), F32), jax.ShapeDtypeStruct((1, D_MODEL), F32)),
        in_specs=[row(D_MODEL), vec, _resident(w_up), _resident(w_down), vec, row(D_MODEL)],
        out_specs=(row(D_FF), row(D_MODEL), row(D_MODEL), row(D_MODEL), pl.BlockSpec((8, LANES), lambda i: (0, 0)), vec),
        scratch_shapes=[pltpu.VMEM((tm, D_FF), BF16)],
        compiler_params=_cparams("arbitrary"),
    )(x1, g_mlp, w_up, w_down, g_fin, target)


def _mlp_bwd(dx2, u, x1, g_mlp, w_up, w_down):
    t = x1.shape[0]
    tm = _mlp_tile(t)

    def body(dx_ref, u_ref, x_ref, g_ref, wu_ref, wd_ref, du_ref, a_ref, dx1_ref, dx1b_ref, dg_ref):
        @pl.when(pl.program_id(0) == 0)
        def _():
            dg_ref[...] = jnp.zeros_like(dg_ref)

        dx2 = dx_ref[...]
        dxb = dx2.astype(BF16)
        for f in range(D_FF // FF_CHUNK):
            sl = slice(f * FF_CHUNK, (f + 1) * FF_CHUNK)
            r = jnp.maximum(u_ref[:, sl], 0.0)
            a_ref[:, sl] = (r * r).astype(BF16)
            da = _dot_nt(dxb, wd_ref[sl, :])
            du_ref[:, sl] = (da * (2.0 * r)).astype(BF16)
        dh = _dot_nt(du_ref[...], wu_ref[...])
        x = x_ref[...]
        _, r1 = _rms(x, g_ref[...])
        dx, dg = _rms_bwd(x, g_ref[...], r1, dh)
        dx1 = dx2 + dx
        dx1_ref[...] = dx1
        dx1b_ref[...] = dx1.astype(BF16)
        dg_ref[...] += dg

    row = lambda n: pl.BlockSpec((tm, n), lambda i: (i, 0))
    vec = pl.BlockSpec((1, D_MODEL), lambda i: (0, 0))
    return pl.pallas_call(
        body, name="mlp_bwd", grid=(t // tm,),
        out_shape=(jax.ShapeDtypeStruct((t, D_FF), BF16), jax.ShapeDtypeStruct((t, D_FF), BF16),
                   jax.ShapeDtypeStruct((t, D_MODEL), F32), jax.ShapeDtypeStruct((t, D_MODEL), BF16),
                   jax.ShapeDtypeStruct((1, D_MODEL), F32)),
        in_specs=[row(D_MODEL), row(D_FF), row(D_MODEL), vec, _resident(w_up), _resident(w_down)],
        out_specs=(row(D_FF), row(D_FF), row(D_MODEL), row(D_MODEL), vec),
        compiler_params=_cparams("arbitrary"),
    )(dx2, u, x1, g_mlp, w_up, w_down)


def _matmul_tn(name, a, b, blocks=None):
    t, m = a.shape
    n = b.shape[1]
    tk = t if a.dtype == BF16 and b.dtype == BF16 else min(t, 2048)
    steps = t // tk
    bm = m if m <= 1024 else 512
    bn = n if n <= 1024 else 512
    width = bn if blocks is None else n // blocks
    per = bn // width

    def body(a_ref, b_ref, o_ref, acc_sc):
        kk = pl.program_id(2)

        @pl.when(kk == 0)
        def _():
            acc_sc[...] = jnp.zeros_like(acc_sc)

        acc_sc[...] += _dot_tn(a_ref[...].astype(BF16), b_ref[...].astype(BF16))

        @pl.when(kk == steps - 1)
        def _():
            if blocks is None:
                o_ref[...] = acc_sc[...]
            else:
                for s in range(per):
                    o_ref[s] = acc_sc[:, s * width:(s + 1) * width]

    if blocks is None:
        o_spec = pl.BlockSpec((bm, bn), lambda i, j, kk: (i, j))
        o_shape = (m, n)
    else:
        o_spec = pl.BlockSpec((per, bm, width), lambda i, j, kk: (j, i, 0))
        o_shape = (blocks, m, width)
    return pl.pallas_call(
        body, name=name, grid=(m // bm, n // bn, steps),
        out_shape=jax.ShapeDtypeStruct(o_shape, F32),
        in_specs=[pl.BlockSpec((tk, bm), lambda i, j, kk: (kk, i)), pl.BlockSpec((tk, bn), lambda i, j, kk: (kk, j))],
        out_specs=o_spec,
        scratch_shapes=[pltpu.VMEM((bm, bn), F32)],
        compiler_params=_cparams("parallel", "parallel", "arbitrary"),
    )(a, b)


def _dw_in(dfq, dfk, dfv, drest, h1):
    t = h1.shape[0]
    tk = min(t, 1024)
    off_ff = 3 * FOX_WIDTH
    off_cq = off_ff + HEADS
    off_kr = IN_COLS - ROPE

    def body(dq_ref, dk_ref, dv_ref, dr_ref, h_ref, o_ref):
        h = h_ref[...]
        r = _dot_tn(dr_ref[...], h)
        parts = [(slice(n * FOX_WIDTH, (n + 1) * FOX_WIDTH), _dot_tn(ref[...], h)) for n, ref in enumerate((dq_ref, dk_ref, dv_ref))]
        parts += [(slice(off_ff, off_cq), r[0:HEADS]), (slice(off_cq, off_kr), r[REST_CQ:REST_KR]),
                  (slice(off_kr, IN_COLS), r[REST_KR + NOPE:REST_KR + NOPE + ROPE])]

        @pl.when(pl.program_id(0) == 0)
        def _():
            for rows, val in parts:
                o_ref[rows, :] = val

        @pl.when(pl.program_id(0) > 0)
        def _():
            for rows, val in parts:
                o_ref[rows, :] += val

    tok = lambda n: pl.BlockSpec((tk, n), lambda kk: (kk, 0))
    return pl.pallas_call(
        body, name="dw_in", grid=(t // tk,),
        out_shape=jax.ShapeDtypeStruct((IN_COLS, D_MODEL), F32),
        in_specs=[tok(FOX_WIDTH), tok(FOX_WIDTH), tok(FOX_WIDTH), tok(REST_COLS), tok(D_MODEL)],
        out_specs=pl.BlockSpec((IN_COLS, D_MODEL), lambda kk: (0, 0)),
        compiler_params=_cparams("arbitrary"),
    )(dfq, dfk, dfv, drest, h1)


def _attn_out_bwd(dx1, fox_o, mla_o, gf, gm, w_o):
    t = dx1.shape[0]
    tm = _row_tile(t)

    def body(dx_ref, f_ref, m_ref, gf_ref, gm_ref, w_ref, df_ref, dm_ref, dlf_ref, dlm_ref, dgf_ref, dgm_ref):
        @pl.when(pl.program_id(0) == 0)
        def _():
            dgf_ref[...] = jnp.zeros_like(dgf_ref)
            dgm_ref[...] = jnp.zeros_like(dgm_ref)
        dxb = dx_ref[...].astype(BF16)
        lane = lax.broadcasted_iota(jnp.int32, (8, LANES), 1)
        picks = [(lane < HEAD_DIM).astype(BF16), (lane >= HEAD_DIM).astype(BF16)]
        for o_ref, g_ref, lo_row, d_ref, dl_ref, dg_ref in ((f_ref, gf_ref, 0, df_ref, dlf_ref, dgf_ref),
                                                             (m_ref, gm_ref, FOX_WIDTH, dm_ref, dlm_ref, dgm_ref)):
            dn = _dot_nt(dxb, w_ref[lo_row:lo_row + FOX_WIDTH, :])
            o = o_ref[...]
            _, r = _rms(o, g_ref[...])
            d, dg = _rms_bwd(o, g_ref[...], r, dn)
            d_ref[...] = d
            dg_ref[...] += dg
            prod = d * o
            for h in range(HEADS):
                parts = _split3(prod[:, (h // 2) * LANES:(h // 2 + 1) * LANES])
                dl_ref[h, 0] = (_dot_nt(picks[h % 2], parts[0]) + _dot_nt(picks[h % 2], parts[1])) + _dot_nt(picks[h % 2], parts[2])

    row = lambda n: pl.BlockSpec((tm, n), lambda i: (i, 0))
    full = lambda a: pl.BlockSpec(a.shape, lambda i: (0,) * a.ndim)
    vec = pl.BlockSpec((1, FOX_WIDTH), lambda i: (0, 0))
    rows = pl.BlockSpec((HEADS, 1, 8, tm), lambda i: (0, i, 0, 0))
    o_shape = jax.ShapeDtypeStruct((t, FOX_WIDTH), F32)
    g_shape = jax.ShapeDtypeStruct((1, FOX_WIDTH), F32)
    r_shape = jax.ShapeDtypeStruct((HEADS, t // tm, 8, tm), F32)
    return pl.pallas_call(
        body, name="attn_out_bwd", grid=(t // tm,),
        out_shape=(o_shape, o_shape, r_shape, r_shape, g_shape, g_shape),
        in_specs=[row(D_MODEL), row(FOX_WIDTH), row(MLA_WIDTH), full(gf), full(gm), full(w_o)],
        out_specs=(row(FOX_WIDTH), row(MLA_WIDTH), rows, rows, vec, vec),
        compiler_params=_cparams("arbitrary"),
    )(dx1, fox_o, mla_o, gf, gm, w_o)


def _mla_prep_bwd(dq, dkv, dkr, dz, rest, gq, gkv, wq, wkv, cos, sin):
    t = rest.shape[0]
    tm = _row_tile(t)

    def body(dq_ref, dkv_ref, dkr_ref, dz_ref, r_ref, gq_ref, gkv_ref, wq_ref, wkv_ref, c_ref, s_ref,
             dr_ref, dqp_ref, dkvb_ref, dgq_ref, dgkv_ref):
        @pl.when(pl.program_id(0) == 0)
        def _():
            dgq_ref[...] = jnp.zeros_like(dgq_ref)
            dgkv_ref[...] = jnp.zeros_like(dgkv_ref)
        cos_, sin_ = c_ref[...], s_ref[...]
        for h in range(HEADS):
            dqp_ref[:, h * LANES:(h + 1) * LANES] = _rope_bwd(dq_ref[h], cos_, sin_).astype(BF16)
            dkvb_ref[:, h * LANES:(h + 1) * LANES] = dkv_ref[h].astype(BF16)
        dcq = _dot_nt(dqp_ref[...], wq_ref[...])
        dckv = _dot_nt(dkvb_ref[...], wkv_ref[...])
        dkrope = dkr_ref[0]
        for pr in range(1, HEADS // 2):
            dkrope = dkrope + dkr_ref[pr]
        cq = r_ref[:, REST_CQ:REST_CKV]
        _, rq = _rms(cq, gq_ref[...])
        d_cq, dgq = _rms_bwd(cq, gq_ref[...], rq, dcq)
        ckv = r_ref[:, REST_CKV:REST_KR]
        _, rkv = _rms(ckv, gkv_ref[...])
        d_ckv, dgkv = _rms_bwd(ckv, gkv_ref[...], rkv, dckv)
        dgq_ref[...] += dgq
        dgkv_ref[...] += dgkv
        dr_ref[:, 0:REST_CQ] = dz_ref[...].astype(BF16)
        dr_ref[:, REST_CQ:REST_CKV] = d_cq.astype(BF16)
        dr_ref[:, REST_CKV:REST_KR] = d_ckv.astype(BF16)
        dr_ref[:, REST_KR:REST_COLS] = _rope_bwd(dkrope, cos_, sin_).astype(BF16)

    row = lambda n: pl.BlockSpec((tm, n), lambda i: (i, 0))
    full = lambda a: pl.BlockSpec(a.shape, lambda i: (0,) * a.ndim)
    heads = pl.BlockSpec((HEADS, tm, LANES), lambda i: (0, i, 0))
    hshape = jax.ShapeDtypeStruct((t, HEADS * LANES), BF16)
    return pl.pallas_call(
        body, name="mla_prep_bwd", grid=(t // tm,),
        out_shape=(jax.ShapeDtypeStruct((t, REST_COLS), BF16), hshape, hshape,
                   jax.ShapeDtypeStruct((1, Q_RANK), F32), jax.ShapeDtypeStruct((1, KV_RANK), F32)),
        in_specs=[heads, heads, pl.BlockSpec((HEADS // 2, tm, LANES), lambda i: (0, i, 0)), row(LANES), row(REST_COLS),
                  full(gq), full(gkv), full(wq), full(wkv), row(LANES), row(LANES)],
        out_specs=(row(REST_COLS), row(HEADS * LANES), row(HEADS * LANES), pl.BlockSpec((1, Q_RANK), lambda i: (0, 0)),
                   pl.BlockSpec((1, KV_RANK), lambda i: (0, 0))),
        compiler_params=_cparams("arbitrary"),
    )(dq, dkv, dkr, dz, rest, gq, gkv, wq, wkv, cos, sin)


def _in_proj_bwd(x, g, dx1, dfq, dfk, dfv, drest, w_qkv, w_rest, comm=None):
    t = x.shape[0]
    tm = _row_tile(t)

    def main(ins, outs, scr):
        x_ref, g_ref, dx1_ref, dq_ref, dk_ref, dv_ref, dr_ref, wq_ref, wr_ref = ins
        dx_ref, dg_ref = outs

        @pl.when(pl.program_id(0) == 0)
        def _():
            dg_ref[...] = jnp.zeros_like(dg_ref)
        dh = _dot(dr_ref[...], wr_ref[...])
        for n, ref in enumerate((dq_ref, dk_ref, dv_ref)):
            dh = dh + _dot(ref[...], wq_ref[n * FOX_WIDTH:(n + 1) * FOX_WIDTH, :])
        xv = x_ref[...]
        _, r = _rms(xv, g_ref[...])
        dx, dg = _rms_bwd(xv, g_ref[...], r, dh)
        dx_ref[...] = dx1_ref[...] + dx
        dg_ref[...] += dg

    row = lambda n: pl.BlockSpec((tm, n), lambda i: (i, 0))
    full = lambda a: pl.BlockSpec(a.shape, lambda i: (0,) * a.ndim)
    vec = pl.BlockSpec((1, D_MODEL), lambda i: (0, 0))
    return _hosted_call(
        "in_proj_bwd", main, (t // tm,), [x, g, dx1, dfq, dfk, dfv, drest, w_qkv, w_rest],
        [row(D_MODEL), full(g), row(D_MODEL), row(FOX_WIDTH), row(FOX_WIDTH), row(FOX_WIDTH), row(REST_COLS),
         full(w_qkv), full(w_rest)],
        (jax.ShapeDtypeStruct((t, D_MODEL), F32), jax.ShapeDtypeStruct((1, D_MODEL), F32)), (row(D_MODEL), vec), [], comm)


def _pad_cols(a, n):
    return jnp.pad(a, ((0, 0),) * (a.ndim - 1) + ((0, n - a.shape[-1]),))


def kernel(x, positions, attn_norm_g, w_in, b_forget, q_norm_g, w_uq, kv_norm_g, w_ukv, fox_out_g, mla_out_g, w_o, mlp_norm_g, w_up, w_down, final_norm_g, loss_target, m_attn_norm_g, m_w_in, m_b_forget, m_q_norm_g, m_w_uq, m_kv_norm_g, m_w_ukv, m_fox_out_g, m_mla_out_g, m_w_o, m_mlp_norm_g, m_w_up, m_w_down, m_final_norm_g, v_attn_norm_g, v_w_in, v_b_forget, v_q_norm_g, v_w_uq, v_kv_norm_g, v_w_ukv, v_fox_out_g, v_mla_out_g, v_w_o, v_mlp_norm_g, v_w_up, v_w_down, v_final_norm_g):
    t = x.shape[1]
    tq = _row_tile(t)
    xs = x[0]
    target = loss_target[0]

    mid = [_pad_cols(w_uq[0], LANES).astype(BF16), w_ukv[0].astype(BF16)]
    late = [w_o[0].astype(BF16), w_up[0].astype(BF16), w_down[0].astype(BF16)]
    g_in, = _all_gather([jnp.transpose(w_in[0]).astype(BF16)])
    win = g_in.reshape(IN_COLS, D_MODEL)
    off_ff, off_cq, off_kr = 3 * FOX_WIDTH, 3 * FOX_WIDTH + HEADS, IN_COLS - ROPE
    zeros = lambda n: jnp.zeros((n, D_MODEL), BF16)
    w_qkv = win[:off_ff]
    w_rest = jnp.concatenate([
        win[off_ff:off_cq], zeros(REST_CQ - HEADS), win[off_cq:off_kr],
        zeros(NOPE), win[off_kr:], zeros(LANES - NOPE - ROPE)], axis=0)

    cos, sin = _rope_tables(positions.reshape(t, 1))
    (h1, fq, fk, fv, rest), (g_uq, g_ukv) = _in_proj(xs, attn_norm_g, w_qkv, w_rest, comm=_ag_to_all(mid))
    wq = jnp.transpose(g_uq, (1, 0, 2)).reshape(Q_RANK, HEADS * LANES)
    wkv = jnp.transpose(g_ukv, (1, 0, 2)).reshape(KV_RANK, HEADS * LANES)
    b128 = _pad_cols(b_forget, LANES)
    f2_rows, f2_rep = _forget_cumsum(rest, b128)
    f2_rows = f2_rows.reshape(HEADS, t // tq, 1, tq)
    (fox_o, fox_lse_rows), partly = _attn_fwd(True, fq, fk, fv, f2_rows, comm=_ag_direct(late))
    mq, mk, mkv, cqn, ckvn = _mla_prep(rest, q_norm_g, kv_norm_g, wq, wkv, cos, sin)
    (mla_o, mla_lse_rows), (g_o, g_up, g_down) = _attn_fwd(False, mq, mk, mkv, comm=_ag_forward(partly))
    wo = g_o.reshape(D_MODEL, D_MODEL)
    x1, mixed = _attn_out(xs, fox_o, mla_o, fox_out_g, mla_out_g, wo)
    wup = jnp.transpose(g_up, (1, 0, 2)).reshape(D_MODEL, D_FF)
    wdown = g_down.reshape(D_FF, D_MODEL)
    u, h2, dx2, dx2b, loss8, d_gfin = _mlp_fwd(x1, mlp_norm_g, wup, wdown, final_norm_g.reshape(1, D_MODEL), target)

    du, act, dx1, dx1b, d_gmlp = _mlp_bwd(dx2, u, x1, mlp_norm_g, wup, wdown)
    dw_down = _matmul_tn("dw_down", act, dx2b)
    dw_up = _matmul_tn("dw_up", h2, du, blocks=N_DEV)
    dfox_o, dmla_o, fox_delta_rows, mla_delta_rows, d_gfox, d_gmla = _attn_out_bwd(dx1, fox_o, mla_o, fox_out_g, mla_out_g, wo)
    dw_o = _matmul_tn("dw_o", mixed, dx1b)

    place = jnp.stack([lax.axis_index("c"), 2 * lax.axis_index("x") + lax.axis_index("y")]).astype(jnp.int32)
    names = ("w_in", "w_uq", "w_ukv", "w_o", "w_up", "w_down")
    grads_b = [dw_o.reshape(N_DEV, -1, D_MODEL), dw_up, dw_down.reshape(N_DEV, -1, D_MODEL)]
    (dfq, dfk, dfv, d_fq, d_fk), got_b = _attn_bwd(True, fq, fk, fv, dfox_o, fox_lse_rows, fox_delta_rows,
                                                   f2_rep, comm=_rs_to_sibling(grads_b))
    sums_b = [_rs_sibling_sum("rs_sibling_sum_" + nm, g, l, place) for nm, g, l in zip(names[3:], grads_b, got_b)]
    dz, d_b = _forget_bwd(rest, b128, d_fq, d_fk)
    (dmq, dmkv, dmkr), others_b = _attn_bwd(False, mq, mk, mkv, dmla_o, mla_lse_rows, mla_delta_rows,
                                            comm=_rs_to_chips([s[1] for s in sums_b]))
    drest, dqp, dkvb, d_gq, d_gkv = _mla_prep_bwd(dmq, dmkv, dmkr, dz, rest, q_norm_g, kv_norm_g, wq, wkv, cos, sin)
    dw_uq = _matmul_tn("dw_uq", cqn, dqp, blocks=HEADS)
    dw_ukv = _matmul_tn("dw_ukv", ckvn, dkvb, blocks=HEADS)
    dw_in = _dw_in(dfq, dfk, dfv, drest, h1)

    grads_a = [dw_in.reshape(N_DEV, IN_SHARD, D_MODEL), dw_uq, dw_ukv]
    got_a = _comm_call("rs_sibling_exchange", _rs_to_sibling(grads_a))
    sums_a = [_rs_sibling_sum("rs_sibling_sum_" + nm, g, l, place) for nm, g, l in zip(names[:3], grads_a, got_a)]
    (grad_x, d_gattn), others_a = _in_proj_bwd(xs, attn_norm_g, dx1, dfq, dfk, dfv, drest, w_qkv, w_rest,
                                               comm=_rs_to_chips([s[1] for s in sums_a]))
    sums, others = sums_a + sums_b, list(others_a) + list(others_b)
    sharded = (w_in, w_uq, w_ukv, w_o, w_up, w_down)
    moments_m = (m_w_in, m_w_uq, m_w_ukv, m_w_o, m_w_up, m_w_down)
    moments_v = (v_w_in, v_w_uq, v_w_ukv, v_w_o, v_w_up, v_w_down)
    g_in_t = _rs_final_sum("rs_final_sum_w_in", sums[0][0], others[0])
    big = [_adamw_given("adamw_w_in", jnp.transpose(g_in_t), w_in, m_w_in, v_w_in)]
    for a in range(1, len(names)):
        big.append(_adamw_sharded("adamw_" + names[a], sharded[a], moments_m[a], moments_v[a], sums[a][0], others[a]))
    big_g, big_d, big_m, big_v = [[b[k] for b in big] for k in range(4)]

    as_row = lambda a: a.reshape(1, -1)
    small_w = (attn_norm_g, b_forget, q_norm_g, kv_norm_g, fox_out_g, mla_out_g, mlp_norm_g, final_norm_g)
    small_m = (m_attn_norm_g, m_b_forget, m_q_norm_g, m_kv_norm_g, m_fox_out_g, m_mla_out_g, m_mlp_norm_g, m_final_norm_g)
    small_v = (v_attn_norm_g, v_b_forget, v_q_norm_g, v_kv_norm_g, v_fox_out_g, v_mla_out_g, v_mlp_norm_g, v_final_norm_g)
    total = _small_all_reduce([d_gattn, d_b, d_gq, d_gkv, d_gfox, d_gmla, d_gmlp, d_gfin], loss8)
    small = _adamw_small(total, [as_row(a) for a in small_w], [as_row(a) for a in small_m], [as_row(a) for a in small_v])
    loss = small[0].reshape(())
    s_g, s_d, s_m, s_v = [[small[1 + 4 * r + k].reshape(small_w[r].shape) for r in range(len(small_w))] for k in range(4)]

    def ordered(small_, bigs):
        ga, bf, gq_, gkv_, gfo, gml, gmlp_, gfin_ = small_
        bin_, buq, bukv, bo, bup, bdown = bigs
        return [ga, bin_, bf, gq_, buq, gkv_, bukv, gfo, gml, bo, gmlp_, bup, bdown, gfin_]

    return (loss, grad_x[None], *ordered(s_g, big_g), *ordered(s_d, big_d), *ordered(s_m, big_m), *ordered(s_v, big_v))
```

```python
import math
from typing import Callable, NamedTuple

import numpy as np
import jax
import jax.numpy as jnp
from jax import lax
from jax.experimental import pallas as pl
from jax.experimental.pallas import tpu as pltpu

F32 = jnp.float32
BF16 = jnp.bfloat16
MESH = pl.DeviceIdType.MESH

D_MODEL = 1024
HEADS = 8
HEAD_DIM = 64
FOX_WIDTH = 512
MLA_WIDTH = 512
NOPE = 64
ROPE = 32
QK_DIM = 96
Q_RANK = 384
KV_RANK = 256
D_FF = 4096
IN_COLS = 2216
ROPE_THETA = 10000.0
EPS = 1e-6
FOX_SCALE = 1.0 / math.sqrt(HEAD_DIM)
MLA_SCALE = 1.0 / math.sqrt(QK_DIM)
ADAM_LR = 0.001
ADAM_B1 = 0.9
ADAM_B2 = 0.999
ADAM_EPS = 1e-08
ADAM_WD = 0.01
ADAM_STEP = 10

N_DEV = 8
LANES = 128
REST_COLS = 896
REST_CQ = LANES
REST_CKV = REST_CQ + Q_RANK
REST_KR = REST_CKV + KV_RANK
LOG2E = 1.4426950408889634
LN2 = 0.6931471805599453
FOX_Q_FACTOR = FOX_SCALE * LOG2E
MLA_Q_FACTOR = MLA_SCALE * LOG2E
VMEM_LIMIT = 56 * 1024 * 1024

IN_SHARD = IN_COLS // N_DEV
SMALL_SIZES = (1024, 8, 384, 256, 512, 512, 1024, 1024)
SMALL_ROWS = 16
LOSS_ROW = len(SMALL_SIZES)


def _cparams(*sem):
    return pltpu.CompilerParams(dimension_semantics=sem or None, vmem_limit_bytes=VMEM_LIMIT)


def _row_tile(t):
    return 512 if t >= 2048 else (256 if t >= 512 else 128)


def _dot(a, b):
    return jnp.dot(a, b, preferred_element_type=F32)


def _dot_nt(a, b):
    return lax.dot_general(a, b, (((1,), (1,)), ((), ())), preferred_element_type=F32)


def _dot_tn(a, b):
    return lax.dot_general(a, b, (((0,), (0,)), ((), ())), preferred_element_type=F32)


def _rms(x, g):
    r = lax.rsqrt(jnp.mean(x * x, axis=-1, keepdims=True) + EPS)
    return x * r * g, r


def _rms_bwd(x, g, r, dy):
    xh = x * r
    gdy = dy * g
    dx = r * (gdy - xh * jnp.mean(gdy * xh, axis=-1, keepdims=True))
    return dx, jnp.sum(dy * xh, axis=0, keepdims=True)


def _lane():
    return lax.broadcasted_iota(jnp.int32, (1, LANES), 1)


def _rot(x):
    lane = _lane()
    half = NOPE + ROPE // 2
    first = jnp.logical_and(lane >= NOPE, lane < half)
    second = jnp.logical_and(lane >= half, lane < NOPE + ROPE)
    return jnp.where(first, -pltpu.roll(x, LANES - ROPE // 2, 1), jnp.where(second, pltpu.roll(x, ROPE // 2, 1), 0.0))


def _rope(x, cos, sin):
    return x * cos + _rot(x) * sin


def _rope_bwd(dy, cos, sin):
    return dy * cos - _rot(dy * sin)


def _remote(src, dst, send_sem, recv_sem, to):
    return pltpu.make_async_remote_copy(src_ref=src, dst_ref=dst, send_sem=send_sem, recv_sem=recv_sem,
                                        device_id=to, device_id_type=MESH)


def _hbm_specs(n):
    return [pl.BlockSpec(memory_space=pl.ANY)] * n


def _all_gather(blocks):
    n = len(blocks)

    def body(*refs):
        x_refs, out_refs = refs[:n], refs[n:2 * n]
        send_sems, recv_sems, local_sems = refs[2 * n:]
        x, y, c = lax.axis_index("x"), lax.axis_index("y"), lax.axis_index("c")
        me, sibling = (x, y, c), (x, y, 1 - c)
        chips = [(1 - x, y), (x, 1 - y), (1 - x, 1 - y)]

        def slot(a, px, py, pc):
            return out_refs[a].at[4 * px + 2 * py + pc]

        def copy(a, k, blk, to, src=None):
            return _remote(slot(a, *blk) if src is None else src, slot(a, *blk),
                           send_sems.at[7 * a + k], recv_sems.at[7 * a + k], to)

        mine = [pltpu.make_async_copy(x_refs[a], slot(a, *me), local_sems.at[a]) for a in range(n)]
        first, passed = [], []
        for a in range(n):
            mine[a].start()
            first.append(copy(a, 0, me, sibling, src=x_refs[a]))
            first += [copy(a, 1 + j, me, (*chip, c), src=x_refs[a]) for j, chip in enumerate(chips)]
        for cp in first:
            cp.start()
        for a in range(n):
            for j, chip in enumerate(chips):
                copy(a, 1 + j, (*chip, c), me).wait_recv()
                passed.append(copy(a, 4 + j, (*chip, c), sibling))
                passed[-1].start()
        for a in range(n):
            copy(a, 0, sibling, me).wait_recv()
            for j, chip in enumerate(chips):
                copy(a, 4 + j, (*chip, 1 - c), me).wait_recv()
        for cp in first + passed:
            cp.wait_send()
        for cp in mine:
            cp.wait()

    return pl.pallas_call(
        body, name="all_gather_weights",
        out_shape=[jax.ShapeDtypeStruct((N_DEV,) + b.shape, b.dtype) for b in blocks],
        in_specs=_hbm_specs(n), out_specs=_hbm_specs(n),
        scratch_shapes=[pltpu.SemaphoreType.DMA((7 * n,)), pltpu.SemaphoreType.DMA((7 * n,)), pltpu.SemaphoreType.DMA((n,))],
    )(*blocks)


def _symmetric_comm(inputs, out_shape, aliases, per_array, copies):
    def start(in_refs, out_refs, sems):
        for cp in copies(in_refs, out_refs, *sems):
            cp.start()

    def finish(in_refs, out_refs, sems):
        for cp in copies(in_refs, out_refs, *sems):
            cp.wait()

    n_sems = per_array * len(inputs)
    return _Comm(tuple(inputs), tuple(out_shape), aliases,
                 (pltpu.SemaphoreType.DMA((n_sems,)), pltpu.SemaphoreType.DMA((n_sems,))), start, finish)


def _ag_direct(shards):
    def copies(in_refs, out_refs, send_sems, recv_sems):
        x, y, c = lax.axis_index("x"), lax.axis_index("y"), lax.axis_index("c")
        peers = [(x, y, 1 - c), (1 - x, y, c), (x, 1 - y, c), (1 - x, 1 - y, c)]
        cps = []
        for a in range(len(shards)):
            mine = out_refs[a].at[4 * x + 2 * y + c]
            cps.append(pltpu.make_async_copy(in_refs[a], mine, send_sems.at[5 * a]))
            cps += [_remote(in_refs[a], mine, send_sems.at[5 * a + k], recv_sems.at[5 * a + k], peer)
                    for k, peer in enumerate(peers, start=1)]
        return cps

    return _symmetric_comm(shards, [jax.ShapeDtypeStruct((N_DEV,) + s.shape, s.dtype) for s in shards], {}, 5, copies)


def _ag_to_all(shards):
    def copies(in_refs, out_refs, send_sems, recv_sems):
        x, y, c = lax.axis_index("x"), lax.axis_index("y"), lax.axis_index("c")
        cps = []
        for a in range(len(shards)):
            mine = out_refs[a].at[4 * x + 2 * y + c]
            cps.append(pltpu.make_async_copy(in_refs[a], mine, send_sems.at[N_DEV * a]))
            for k in range(1, N_DEV):
                peer = (x ^ (k >> 2), y ^ ((k >> 1) & 1), c ^ (k & 1))
                cps.append(_remote(in_refs[a], mine, send_sems.at[N_DEV * a + k], recv_sems.at[N_DEV * a + k], peer))
        return cps

    return _symmetric_comm(shards, [jax.ShapeDtypeStruct((N_DEV,) + s.shape, s.dtype) for s in shards], {}, N_DEV, copies)


def _ag_forward(gathered):
    def copies(in_refs, out_refs, send_sems, recv_sems):
        x, y, c = lax.axis_index("x"), lax.axis_index("y"), lax.axis_index("c")
        chips = [(1 - x, y), (x, 1 - y), (1 - x, 1 - y)]
        return [_remote(out_refs[a].at[4 * cx + 2 * cy + c], out_refs[a].at[4 * cx + 2 * cy + c],
                        send_sems.at[3 * a + j], recv_sems.at[3 * a + j], (x, y, 1 - c))
                for a in range(len(gathered)) for j, (cx, cy) in enumerate(chips)]

    shapes = [jax.ShapeDtypeStruct(g.shape, g.dtype) for g in gathered]
    return _symmetric_comm(gathered, shapes, {a: a for a in range(len(gathered))}, 3, copies)


def _rs_to_sibling(grads):
    def copies(in_refs, out_refs, send_sems, recv_sems):
        x, y, c = lax.axis_index("x"), lax.axis_index("y"), lax.axis_index("c")
        return [_remote(in_refs[a].at[2 * q + 1 - c], out_refs[a].at[q], send_sems.at[4 * a + q], recv_sems.at[4 * a + q], (x, y, 1 - c))
                for a in range(len(grads)) for q in range(4)]

    return _symmetric_comm(grads, [jax.ShapeDtypeStruct((4,) + g.shape[1:], g.dtype) for g in grads], {}, 4, copies)


def _rs_to_chips(parts):
    def copies(in_refs, out_refs, send_sems, recv_sems):
        x, y, c = lax.axis_index("x"), lax.axis_index("y"), lax.axis_index("c")
        chips = [(1 - x, y), (x, 1 - y), (1 - x, 1 - y)]
        return [_remote(in_refs[a].at[2 * cx + cy], out_refs[a].at[k], send_sems.at[3 * a + k], recv_sems.at[3 * a + k], (cx, cy, c))
                for a in range(len(parts)) for k, (cx, cy) in enumerate(chips)]

    return _symmetric_comm(parts, [jax.ShapeDtypeStruct((3,) + p.shape[1:], p.dtype) for p in parts], {}, 3, copies)


def _comm_call(name, comm):
    n_in, n_out = len(comm.inputs), len(comm.out_shape)

    def body(*refs):
        ins, outs, sems = refs[:n_in], refs[n_in:n_in + n_out], refs[n_in + n_out:]
        comm.start(ins, outs, sems)
        comm.finish(ins, outs, sems)

    return pl.pallas_call(
        body, name=name, out_shape=list(comm.out_shape), in_specs=_hbm_specs(n_in), out_specs=_hbm_specs(n_out),
        scratch_shapes=list(comm.scratch), input_output_aliases=dict(comm.aliases),
    )(*comm.inputs)


def _small_all_reduce(parts, loss8):
    n = len(parts)

    def body(*refs):
        p_refs, loss_ref, out_ref, pack, land, send_sems, recv_sems = refs[:n], *refs[n:]
        x, y, c = lax.axis_index("x"), lax.axis_index("y"), lax.axis_index("c")
        me = 4 * x + 2 * y + c
        pack[...] = jnp.zeros_like(pack)
        for r, ref in enumerate(p_refs):
            pack[r:r + 1, 0:ref.shape[1]] = ref[...]
        pack[LOSS_ROW:LOSS_ROW + 1, 0:LANES] = loss_ref[0:1, :]
        land[me] = pack[...]
        cps = []
        for k in range(1, N_DEV):
            peer = (x ^ (k >> 2), y ^ ((k >> 1) & 1), c ^ (k & 1))
            cps.append(_remote(pack, land.at[me], send_sems.at[k - 1], recv_sems.at[k - 1], peer))
        for cp in cps:
            cp.start()
        for cp in cps:
            cp.wait()
        acc = land[0]
        for d in range(1, N_DEV):
            acc = acc + land[d]
        out_ref[...] = acc

    vmem = pl.BlockSpec(memory_space=pltpu.VMEM)
    return pl.pallas_call(
        body, name="small_all_reduce",
        out_shape=jax.ShapeDtypeStruct((SMALL_ROWS, D_MODEL), F32),
        in_specs=[vmem] * (n + 1), out_specs=vmem,
        scratch_shapes=[pltpu.VMEM((SMALL_ROWS, D_MODEL), F32), pltpu.VMEM((N_DEV, SMALL_ROWS, D_MODEL), F32),
                        pltpu.SemaphoreType.DMA((N_DEV - 1,)), pltpu.SemaphoreType.DMA((N_DEV - 1,))],
    )(*parts, loss8)


def _rs_sibling_sum(name, grad, got, place):
    _, rows, cols = grad.shape

    def body(place_ref, g_ref, l_ref, own_ref, b_ref):
        s = g_ref[...] + l_ref[...]
        b_ref[...] = s.astype(BF16)

        @pl.when(pl.program_id(0) == place_ref[1])
        def _():
            own_ref[...] = s

    by_chip = pl.BlockSpec((None, rows, cols), lambda q, place_ref: (q, 0, 0))
    return pl.pallas_call(
        body, name=name,
        grid_spec=pltpu.PrefetchScalarGridSpec(
            num_scalar_prefetch=1, grid=(4,),
            in_specs=[pl.BlockSpec((None, rows, cols), lambda q, place_ref: (2 * q + place_ref[0], 0, 0)), by_chip],
            out_specs=[pl.BlockSpec((rows, cols), lambda q, place_ref: (0, 0)), by_chip]),
        out_shape=(jax.ShapeDtypeStruct((rows, cols), F32), jax.ShapeDtypeStruct((4, rows, cols), BF16)),
        compiler_params=_cparams("arbitrary"),
    )(place, grad, got)


def _adamw_math(w, g, m, v):
    m2 = ADAM_B1 * m + (1.0 - ADAM_B1) * g
    v2 = ADAM_B2 * v + (1.0 - ADAM_B2) * (g * g)
    m_hat = m2 / (1.0 - ADAM_B1 ** ADAM_STEP)
    v_hat = v2 / (1.0 - ADAM_B2 ** ADAM_STEP)
    delta = -ADAM_LR * (m_hat / (jnp.sqrt(v_hat) + ADAM_EPS) + ADAM_WD * w)
    return delta, m2, v2


def _update_tile(rows):
    return 256 if rows % 256 == 0 else rows


def _rs_final_sum(name, own, got):
    def body(o_ref, r_ref, g_out):
        g = o_ref[...]
        for k in range(3):
            g = g + r_ref[k].astype(F32)
        g_out[...] = g

    return pl.pallas_call(body, name=name, out_shape=jax.ShapeDtypeStruct(own.shape, F32))(own, got)


def _adamw_sharded(name, w, m, v, own, got):
    _, rows, cols = w.shape
    tr = _update_tile(rows)

    def body(o_ref, r_ref, w_ref, m_ref, v_ref, g_out, d_out, m_out, v_out):
        g = o_ref[:, 0:cols]
        for k in range(3):
            g = g + r_ref[k, :, 0:cols].astype(F32)
        d, m2, v2 = _adamw_math(w_ref[0], g, m_ref[0], v_ref[0])
        g_out[0] = g
        d_out[0] = d
        m_out[0] = m2
        v_out[0] = v2

    mine = pl.BlockSpec((1, tr, cols), lambda i: (0, i, 0))
    shp = jax.ShapeDtypeStruct(w.shape, F32)
    wide = own.shape[1]
    return pl.pallas_call(
        body, name=name, grid=(rows // tr,), out_shape=(shp,) * 4,
        in_specs=[pl.BlockSpec((tr, wide), lambda i: (i, 0)), pl.BlockSpec((3, tr, wide), lambda i: (0, i, 0)),
                  mine, mine, mine],
        out_specs=[mine] * 4,
        compiler_params=_cparams("parallel"),
    )(own, got, w, m, v)


def _adamw_given(name, g, w, m, v):
    _, rows, cols = w.shape
    tr = _update_tile(rows)

    def body(g_ref, w_ref, m_ref, v_ref, g_out, d_out, m_out, v_out):
        g = g_ref[...]
        d, m2, v2 = _adamw_math(w_ref[0], g, m_ref[0], v_ref[0])
        g_out[0] = g
        d_out[0] = d
        m_out[0] = m2
        v_out[0] = v2

    own = pl.BlockSpec((1, tr, cols), lambda i: (0, i, 0))
    shp = jax.ShapeDtypeStruct(w.shape, F32)
    return pl.pallas_call(
        body, name=name, grid=(rows // tr,), out_shape=(shp,) * 4,
        in_specs=[pl.BlockSpec((tr, cols), lambda i: (i, 0)), own, own, own], out_specs=[own] * 4,
        compiler_params=_cparams("parallel"),
    )(g, w, m, v)


def _adamw_small(total, ws, ms, vs):
    n = len(ws)

    def body(*refs):
        t_ref = refs[0]
        w_refs, m_refs, v_refs = refs[1:1 + n], refs[1 + n:1 + 2 * n], refs[1 + 2 * n:1 + 3 * n]
        outs = refs[1 + 3 * n:]
        outs[0][...] = t_ref[LOSS_ROW:LOSS_ROW + 1, 0:1]
        for r in range(n):
            g = t_ref[r:r + 1, 0:w_refs[r].shape[1]]
            d, m2, v2 = _adamw_math(w_refs[r][...], g, m_refs[r][...], v_refs[r][...])
            for k, val in enumerate((g, d, m2, v2)):
                outs[1 + 4 * r + k][...] = val

    vmem = pl.BlockSpec(memory_space=pltpu.VMEM)
    out_shape = [jax.ShapeDtypeStruct((1, 1), F32)]
    for w in ws:
        out_shape += [jax.ShapeDtypeStruct(w.shape, F32)] * 4
    return pl.pallas_call(
        body, name="adamw_small", out_shape=out_shape,
        in_specs=[vmem] * (1 + 3 * n), out_specs=[vmem] * len(out_shape),
    )(total, *ws, *ms, *vs)


def _rope_tables(pos_col):
    t = pos_col.shape[0]
    inv = (np.float32(ROPE_THETA) ** (-np.arange(0, ROPE, 2, dtype=np.float32) / np.float32(ROPE))).astype(np.float32)
    freq = np.zeros((1, LANES), np.float32)
    freq[0, NOPE:NOPE + ROPE // 2] = inv
    freq[0, NOPE + ROPE // 2:NOPE + ROPE] = inv
    tm = _row_tile(t)

    def body(p_ref, f_ref, c_ref, s_ref):
        ang = p_ref[...].astype(F32) * f_ref[...]
        c_ref[...] = jnp.cos(ang)
        s_ref[...] = jnp.sin(ang)

    shp = jax.ShapeDtypeStruct((t, LANES), F32)
    return pl.pallas_call(
        body, name="rope_tables", grid=(t // tm,), out_shape=(shp, shp),
        in_specs=[pl.BlockSpec((tm, 1), lambda i: (i, 0)), pl.BlockSpec((1, LANES), lambda i: (0, 0))],
        out_specs=(pl.BlockSpec((tm, LANES), lambda i: (i, 0)),) * 2,
        compiler_params=_cparams("parallel"),
    )(pos_col, jnp.asarray(freq))


def _in_proj(x, g, w_qkv, w_rest, comm=None):
    t = x.shape[0]
    tm = _row_tile(t)

    def main(ins, outs, scr):
        x_ref, g_ref, wq_ref, wr_ref = ins
        h_ref, fq_ref, fk_ref, fv_ref, r_ref = outs
        h, _ = _rms(x_ref[...], g_ref[...])
        hb = h.astype(BF16)
        h_ref[...] = hb
        for n, (ref, factor) in enumerate(((fq_ref, FOX_Q_FACTOR), (fk_ref, None), (fv_ref, None))):
            part = _dot_nt(hb, wq_ref[n * FOX_WIDTH:(n + 1) * FOX_WIDTH, :])
            ref[...] = (part if factor is None else part * factor).astype(BF16)
        r_ref[...] = _dot_nt(hb, wr_ref[...])

    row = lambda n: pl.BlockSpec((tm, n), lambda i: (i, 0))
    full = lambda a: pl.BlockSpec(a.shape, lambda i: (0,) * a.ndim)
    return _hosted_call(
        "in_proj", main, (t // tm,), [x, g, w_qkv, w_rest], [row(D_MODEL), full(g), full(w_qkv), full(w_rest)],
        (jax.ShapeDtypeStruct((t, D_MODEL), BF16),) + (jax.ShapeDtypeStruct((t, FOX_WIDTH), BF16),) * 3
        + (jax.ShapeDtypeStruct((t, REST_COLS), F32),),
        (row(D_MODEL), row(FOX_WIDTH), row(FOX_WIDTH), row(FOX_WIDTH), row(REST_COLS)), [], comm)


def _log_sigmoid(z):
    return jnp.minimum(z, 0.0) - jnp.log(1.0 + jnp.exp(-jnp.abs(z)))


def _split3(v):
    hi = v.astype(BF16)
    r1 = v - hi.astype(F32)
    mid = r1.astype(BF16)
    lo = (r1 - mid.astype(F32)).astype(BF16)
    return hi, mid, lo


def _scan_tile(t):
    return 512 if t >= 2048 else (256 if t >= 256 else t)


def _forget_cumsum(rest, b128):
    t = rest.shape[0]
    tb = _scan_tile(t)

    def body(r_ref, b_ref, row_ref, rep_ref, f_sc, carry):
        @pl.when(pl.program_id(0) == 0)
        def _():
            carry[...] = jnp.zeros_like(carry)
        lf = _log_sigmoid(r_ref[...] + b_ref[...])
        tri = (lax.broadcasted_iota(jnp.int32, (tb, tb), 0) >= lax.broadcasted_iota(jnp.int32, (tb, tb), 1)).astype(BF16)
        hi, mid, lo = _split3(lf)
        f_sc[...] = (_dot(tri, hi) + _dot(tri, mid)) + _dot(tri, lo) + carry[...]
        carry[...] = f_sc[tb - 1:tb, :]
        f2 = f_sc[...] * LOG2E
        row_ref[...] = jnp.transpose(f2)[0:HEADS, :]
        lane = _lane()
        for h in range(HEADS):
            col = jnp.sum(jnp.where(lane == h, f2, 0.0), axis=1, keepdims=True)
            rep_ref[h] = jnp.broadcast_to(col, (tb, LANES))

    return pl.pallas_call(
        body, name="forget_cumsum", grid=(t // tb,),
        out_shape=(jax.ShapeDtypeStruct((HEADS, t), F32), jax.ShapeDtypeStruct((HEADS, t, LANES), F32)),
        in_specs=[pl.BlockSpec((tb, LANES), lambda i: (i, 0)), pl.BlockSpec((1, LANES), lambda i: (0, 0))],
        out_specs=(pl.BlockSpec((HEADS, tb), lambda i: (0, i)), pl.BlockSpec((HEADS, tb, LANES), lambda i: (0, i, 0))),
        scratch_shapes=[pltpu.VMEM((tb, LANES), F32), pltpu.VMEM((1, LANES), F32)],
        compiler_params=_cparams("arbitrary"),
    )(rest, b128)


def _forget_bwd(rest, b128, d_fq, d_fk):
    t = rest.shape[0]
    tb = _scan_tile(t)
    nb = t // tb

    def body(r_ref, b_ref, dfq_ref, dfk_ref, dz_ref, db_ref, carry):
        @pl.when(pl.program_id(0) == 0)
        def _():
            carry[...] = jnp.zeros_like(carry)
            db_ref[...] = jnp.zeros_like(db_ref)
        tri = (lax.broadcasted_iota(jnp.int32, (tb, tb), 0) <= lax.broadcasted_iota(jnp.int32, (tb, tb), 1)).astype(BF16)
        lane = _lane()
        df = jnp.zeros((tb, LANES), F32)
        for h in range(HEADS):
            df = df + jnp.where(lane == h, dfq_ref[h] + dfk_ref[h], 0.0)
        hi, mid, lo = _split3(df)
        dlf = (_dot(tri, hi) + _dot(tri, mid)) + _dot(tri, lo) + carry[...]
        z = r_ref[...] + b_ref[...]
        dz = dlf / (1.0 + jnp.exp(z))
        dz_ref[...] = dz
        db_ref[...] += jnp.sum(dz, axis=0, keepdims=True)
        carry[...] = carry[...] + jnp.sum(df, axis=0, keepdims=True)

    rev = lambda i: (nb - 1 - i, 0)
    rev3 = pl.BlockSpec((HEADS, tb, LANES), lambda i: (0, nb - 1 - i, 0))
    return pl.pallas_call(
        body, name="forget_bwd", grid=(nb,),
        out_shape=(jax.ShapeDtypeStruct((t, LANES), F32), jax.ShapeDtypeStruct((1, LANES), F32)),
        in_specs=[pl.BlockSpec((tb, LANES), rev), pl.BlockSpec((1, LANES), lambda i: (0, 0)), rev3, rev3],
        out_specs=(pl.BlockSpec((tb, LANES), rev), pl.BlockSpec((1, LANES), lambda i: (0, 0))),
        scratch_shapes=[pltpu.VMEM((1, LANES), F32)],
        compiler_params=_cparams("arbitrary"),
    )(rest, b128, d_fq, d_fk)


def _mla_prep(rest, gq, gkv, wq, wkv, cos, sin):
    t = rest.shape[0]
    tm = _row_tile(t)

    def body(r_ref, gq_ref, gkv_ref, wq_ref, wkv_ref, c_ref, s_ref, q_ref, k_ref, kv_ref, cq_ref, ckv_ref):
        cos_, sin_ = c_ref[...], s_ref[...]
        cq, _ = _rms(r_ref[:, REST_CQ:REST_CKV], gq_ref[...])
        ckv, _ = _rms(r_ref[:, REST_CKV:REST_KR], gkv_ref[...])
        cqb, ckvb = cq.astype(BF16), ckv.astype(BF16)
        cq_ref[...] = cqb
        ckv_ref[...] = ckvb
        k_rope = _rope(r_ref[:, REST_KR:REST_COLS], cos_, sin_)
        lo = _lane() < NOPE
        for h in range(HEADS):
            q_ref[h] = (_rope(_dot(cqb, wq_ref[h]), cos_, sin_) * MLA_Q_FACTOR).astype(BF16)
            kv = _dot(ckvb, wkv_ref[h])
            kv_ref[h] = kv.astype(BF16)
            k_ref[h] = (jnp.where(lo, kv, 0.0) + k_rope).astype(BF16)

    row = lambda n: pl.BlockSpec((tm, n), lambda i: (i, 0))
    full = lambda a: pl.BlockSpec(a.shape, lambda i: (0,) * a.ndim)
    heads = pl.BlockSpec((HEADS, tm, LANES), lambda i: (0, i, 0))
    hshape = jax.ShapeDtypeStruct((HEADS, t, LANES), BF16)
    return pl.pallas_call(
        body, name="mla_prep", grid=(t // tm,),
        out_shape=(hshape, hshape, hshape, jax.ShapeDtypeStruct((t, Q_RANK), BF16), jax.ShapeDtypeStruct((t, KV_RANK), BF16)),
        in_specs=[row(REST_COLS), full(gq), full(gkv), full(wq), full(wkv), row(LANES), row(LANES)],
        out_specs=(heads, heads, heads, row(Q_RANK), row(KV_RANK)),
        compiler_params=_cparams("parallel"),
    )(rest, gq, gkv, wq, wkv, cos, sin)


def _tile_lanes(x, n):
    return jnp.tile(x, (1, n)) if n > 1 else x


class _Comm(NamedTuple):
    inputs: tuple
    out_shape: tuple
    aliases: dict
    scratch: tuple
    start: Callable
    finish: Callable


def _hosted_call(name, main, grid, args, in_specs, out_shape, out_specs, scratch, comm):
    n_in, n_out, n_scr = len(args), len(out_shape), len(scratch)
    c_in = list(comm.inputs) if comm else []
    c_out = list(comm.out_shape) if comm else []

    def at_step(which):
        hit = pl.program_id(0) == which[0]
        for axis in range(1, len(grid)):
            hit = jnp.logical_and(hit, pl.program_id(axis) == which[axis])
        return hit

    def body(*refs):
        bounds = [0, n_in, len(c_in), n_out, len(c_out), n_scr]
        starts = [sum(bounds[:k + 1]) for k in range(len(bounds))]
        ins, cins, outs, couts, scr = [refs[a:b] for a, b in zip(starts[:-1], starts[1:])]
        sems = refs[starts[-1]:]
        if comm:
            @pl.when(at_step([0] * len(grid)))
            def _():
                comm.start(cins, couts, sems)
        main(ins, outs, scr)
        if comm:
            @pl.when(at_step([n - 1 for n in grid]))
            def _():
                comm.finish(cins, couts, sems)

    res = pl.pallas_call(
        body, name=name, grid=grid,
        out_shape=list(out_shape) + c_out,
        in_specs=list(in_specs) + _hbm_specs(len(c_in)),
        out_specs=list(out_specs) + _hbm_specs(len(c_out)),
        scratch_shapes=list(scratch) + (list(comm.scratch) if comm else []),
        input_output_aliases={n_in + i: n_out + o for i, o in comm.aliases.items()} if comm else {},
        compiler_params=_cparams(*(["arbitrary"] * len(grid))),
    )(*args, *c_in)
    return res[:n_out], res[n_out:]


def _stat_rows(x):
    return jnp.transpose(x)[0:8, :]


FWD_HEADS = 8


def _attn_fwd(fox, q, k, v, f2_rows=None, comm=None):
    t = q.shape[0] if fox else q.shape[1]
    tq = _row_tile(t)
    nq = t // tq
    nh = FWD_HEADS
    wide = (nh // 2) * LANES

    def main(ins, outs, scr):
        q_ref, k_ref, v_ref = ins[:3]
        fr_ref = ins[3] if fox else None
        o_ref, lset_ref = outs
        m_sc, acc_sc = scr
        i = pl.program_id(1)
        lo = _lane() < HEAD_DIM
        hi = jnp.logical_not(lo)
        zero, one = jnp.zeros((), BF16), jnp.ones((), BF16)
        lanes_of = lambda h: slice((h // 2) * LANES, (h // 2 + 1) * LANES)
        if fox:
            qs = [jnp.where(lo if h % 2 == 0 else hi, q_ref[:, lanes_of(h)], zero) for h in range(nh)]
            sum_lanes = [hi if h % 2 == 0 else lo for h in range(nh)]
        else:
            qs = [q_ref[h] for h in range(nh)]
            sum_lanes = [lo] * nh
        m_sc[...] = jnp.full_like(m_sc, -jnp.inf)
        acc_sc[...] = jnp.zeros_like(acc_sc)

        def block(j, r0, nr, c0, nc, seen_from):
            rows = slice(r0, r0 + nr)
            sl = pl.ds(pl.multiple_of(j * tq + c0, math.gcd(tq, c0) if c0 else tq), nc)
            if seen_from is not None:
                seen = (lax.broadcasted_iota(jnp.int32, (nr, nc), 1)
                        <= lax.broadcasted_iota(jnp.int32, (nr, nc), 0) + seen_from)
            for h in range(nh):
                kj, vj = (k_ref[sl, lanes_of(h)], v_ref[sl, lanes_of(h)]) if fox else (k_ref[h, sl, :], v_ref[h, sl, :])
                s = _dot_nt(qs[h][rows], kj)
                if fox:
                    s = s - fr_ref[h, j, :, c0:c0 + nc]
                if seen_from is not None:
                    s = jnp.where(seen, s, -jnp.inf)
                m_prev = m_sc[h, rows]
                m_new = jnp.maximum(m_prev, jnp.max(s, axis=1, keepdims=True))
                p = jnp.exp2((s - _tile_lanes(m_new, nc // LANES)).astype(BF16))
                vj = jnp.where(sum_lanes[h], one, vj)
                acc_sc[h, rows] = jnp.exp2(m_prev - m_new) * acc_sc[h, rows] + _dot(p, vj)
                m_sc[h, rows] = m_new

        def loop_body(j, carry):
            block(j, 0, tq, 0, tq, None)
            return carry

        lax.fori_loop(0, i, loop_body, 0)
        half = tq // 2
        if half % LANES == 0:
            block(i, 0, half, 0, half, 0)
            block(i, half, half, 0, tq, half)
        else:
            block(i, 0, tq, 0, tq, 0)
        res = []
        for h in range(nh):
            acc = acc_sc[h]
            swapped = pltpu.roll(acc, HEAD_DIM, 1)
            res.append(acc / swapped)
            lse2 = m_sc[h] + jnp.log(jnp.where(sum_lanes[h], acc, swapped)) * LOG2E
            lset_ref[h, 0] = _stat_rows(lse2)
        for pr in range(nh // 2):
            even = res[2 * pr] if fox else pltpu.roll(res[2 * pr], HEAD_DIM, 1)
            o_ref[:, pr * LANES:(pr + 1) * LANES] = jnp.where(lo, even, res[2 * pr + 1])

    if fox:
        in_specs = [pl.BlockSpec((tq, wide), lambda g, i: (i, g))] + [pl.BlockSpec((t, wide), lambda g, i: (0, g))] * 2
        in_specs += [pl.BlockSpec((nh, nq, 1, tq), lambda g, i: (g, 0, 0, 0))]
        args = [q, k, v, f2_rows]
    else:
        in_specs = [pl.BlockSpec((nh, tq, LANES), lambda g, i: (g, i, 0))] + [pl.BlockSpec((nh, t, LANES), lambda g, i: (g, 0, 0))] * 2
        args = [q, k, v]
    return _hosted_call(
        "fox_attn_fwd" if fox else "mla_attn_fwd", main, (HEADS // nh, nq), args, in_specs,
        (jax.ShapeDtypeStruct((t, 4 * LANES), F32), jax.ShapeDtypeStruct((HEADS, nq, 8, tq), F32)),
        (pl.BlockSpec((tq, wide), lambda g, i: (i, g)), pl.BlockSpec((nh, 1, 8, tq), lambda g, i: (g, i, 0, 0))),
        [pltpu.VMEM((nh, tq, LANES), F32), pltpu.VMEM((nh, tq, LANES), F32)], comm)


def _head_do(fox, hh, do2, lo):
    if fox:
        return jnp.where(lo if hh == 0 else jnp.logical_not(lo), do2, 0.0)
    return jnp.where(lo, 0.0, pltpu.roll(do2, HEAD_DIM, 1) if hh == 0 else do2)


def _attn_bwd(fox, q, k, v, do, lse_rows, delta_rows, f2_rep=None, comm=None):
    t = q.shape[0] if fox else q.shape[1]
    tq = _row_tile(t)
    nq = t // tq
    scale = FOX_SCALE if fox else MLA_SCALE

    def main(ins, outs, scr):
        if fox:
            q_ref, k_ref, v_ref, f_ref, do_ref, lse_ref, dl_ref = ins
            dq_ref, dk_ref, dv_ref, dfq_ref, dfk_ref = outs
        else:
            q_ref, k_ref, v_ref, do_ref, lse_ref, dl_ref = ins
            dq_ref, dkv_ref, dkr_ref = outs
        dq_sc, dk_sc, dv_sc = scr
        j = pl.program_id(1)
        lane = _lane()
        lo = lane < HEAD_DIM
        hi = jnp.logical_not(lo)
        zero, one = jnp.zeros((), BF16), jnp.ones((), BF16)

        @pl.when(j == 0)
        def _():
            dq_sc[...] = jnp.zeros_like(dq_sc)

        dk_sc[...] = jnp.zeros_like(dk_sc)
        dv_sc[...] = jnp.zeros_like(dv_sc)

        def block(i, r0, nr, c0, nc, masked):
            rows, cols = slice(r0, r0 + nr), slice(c0, c0 + nc)
            sl = pl.ds(pl.multiple_of(i * tq + c0, math.gcd(tq, c0) if c0 else tq), nc)
            do_i = do_ref[sl, :]
            if masked:
                seen = lax.broadcasted_iota(jnp.int32, (nr, nc), 1) >= lax.broadcasted_iota(jnp.int32, (nr, nc), 0)
            for hh in range(2):
                kj = k_ref[rows, :] if fox else k_ref[hh, rows, :]
                vj = v_ref[rows, :] if fox else v_ref[hh, rows, :]
                qi = jnp.where(lo if hh == 0 else hi, q_ref[sl, :], zero) if fox else q_ref[hh, sl, :]
                dob = _head_do(fox, hh, do_i, lo).astype(BF16)
                st = _dot_nt(kj, qi)
                if fox:
                    st = st - _tile_lanes(f_ref[hh, rows, :], nc // LANES)
                if masked:
                    st = jnp.where(seen, st, -jnp.inf)
                pt = jnp.exp2(st - lse_ref[hh, i, 0:1, cols])
                dpt = _dot_nt(vj, dob)
                dst = (pt * (dpt - dl_ref[hh, i, 0:1, cols])).astype(BF16)
                dv_sc[hh, rows] += _dot(pt.astype(BF16), dob)
                if fox:
                    other = hi if hh == 0 else lo
                    qi = jnp.where(other, one, qi)
                    kj = jnp.where(other, one, kj)
                dk_sc[hh, rows] += _dot(dst, qi)
                dq_sc[hh, sl, :] += _dot_tn(dst, kj)

        def loop_body(i, carry):
            block(i, 0, tq, 0, tq, False)
            return carry

        half = tq // 2
        if half % LANES == 0:
            block(j, 0, half, 0, tq, True)
            block(j, half, half, half, half, True)
        else:
            block(j, 0, tq, 0, tq, True)
        lax.fori_loop(j + 1, nq, loop_body, 0)
        if fox:
            dk_ref[...] = (jnp.where(lo, dk_sc[0], dk_sc[1]) * LN2).astype(BF16)
            dv_ref[...] = (dv_sc[0] + dv_sc[1]).astype(BF16)
            for hh in range(2):
                dk = dk_sc[hh]
                dfk_ref[hh] = -jnp.where(hi if hh == 0 else lo, dk, pltpu.roll(dk, HEAD_DIM, 1))
        else:
            rope_lanes = jnp.logical_and(lane >= NOPE, lane < NOPE + ROPE)
            dkr = jnp.zeros((tq, LANES), F32)
            for hh in range(2):
                dk = dk_sc[hh] * LN2
                dkv_ref[hh] = jnp.where(lo, dk, dv_sc[hh])
                dkr = dkr + jnp.where(rope_lanes, dk, 0.0)
            dkr_ref[0] = dkr

        @pl.when(j == nq - 1)
        def _():
            for i in range(nq):
                rows = slice(i * tq, (i + 1) * tq)
                if fox:
                    dq_ref[rows, :] = (jnp.where(lo, dq_sc[0, rows, :], dq_sc[1, rows, :]) * scale).astype(BF16)
                    for hh in range(2):
                        acc = dq_sc[hh, rows, :]
                        dfq_ref[hh, rows, :] = jnp.where(hi if hh == 0 else lo, acc, pltpu.roll(acc, HEAD_DIM, 1))
                else:
                    for hh in range(2):
                        dq_ref[hh, rows, :] = dq_sc[hh, rows, :] * scale

    stat = pl.BlockSpec((2, tq, LANES), lambda p, j: (p, j, 0))
    stat_all = pl.BlockSpec((2, t, LANES), lambda p, j: (p, 0, 0))
    rows4 = pl.BlockSpec((2, nq, 8, tq), lambda p, j: (p, 0, 0, 0))
    pair = pl.BlockSpec((tq, LANES), lambda p, j: (j, p))
    pair_all = pl.BlockSpec((t, LANES), lambda p, j: (0, p))
    if fox:
        in_specs = [pair_all, pair, pair, stat]
        args = [q, k, v, f2_rep]
    else:
        in_specs = [stat_all, stat, stat]
        args = [q, k, v]
    in_specs += [pair_all, rows4, rows4]
    args += [do, lse_rows, delta_rows]
    heads_f32 = jax.ShapeDtypeStruct((HEADS, t, LANES), F32)
    if fox:
        wide = jax.ShapeDtypeStruct((t, 4 * LANES), BF16)
        out_shape = (wide, wide, wide, heads_f32, heads_f32)
        out_specs = (pair_all, pair, pair, stat_all, stat)
    else:
        out_shape = (heads_f32, heads_f32, jax.ShapeDtypeStruct((HEADS // 2, t, LANES), F32))
        out_specs = (stat_all, stat, pl.BlockSpec((1, tq, LANES), lambda p, j: (p, j, 0)))
    acc = pltpu.VMEM((2, tq, LANES), F32)
    return _hosted_call("fox_attn_bwd" if fox else "mla_attn_bwd", main, (HEADS // 2, nq), args, in_specs,
                        out_shape, out_specs, [pltpu.VMEM((2, t, LANES), F32), acc, acc], comm)


def _attn_out(x, fox_o, mla_o, gf, gm, w_o):
    t = x.shape[0]
    tm = _row_tile(t)

    def body(x_ref, f_ref, m_ref, gf_ref, gm_ref, w_ref, x1_ref, mix_ref):
        nf, _ = _rms(f_ref[...], gf_ref[...])
        nm, _ = _rms(m_ref[...], gm_ref[...])
        nfb, nmb = nf.astype(BF16), nm.astype(BF16)
        mix_ref[:, :FOX_WIDTH] = nfb
        mix_ref[:, FOX_WIDTH:] = nmb
        x1_ref[...] = x_ref[...] + _dot(nfb, w_ref[:FOX_WIDTH, :]) + _dot(nmb, w_ref[FOX_WIDTH:, :])

    row = lambda n: pl.BlockSpec((tm, n), lambda i: (i, 0))
    full = lambda a: pl.BlockSpec(a.shape, lambda i: (0,) * a.ndim)
    return pl.pallas_call(
        body, name="attn_out", grid=(t // tm,),
        out_shape=(jax.ShapeDtypeStruct((t, D_MODEL), F32), jax.ShapeDtypeStruct((t, D_MODEL), BF16)),
        in_specs=[row(D_MODEL), row(FOX_WIDTH), row(MLA_WIDTH), full(gf), full(gm), full(w_o)],
        out_specs=(row(D_MODEL), row(D_MODEL)),
        compiler_params=_cparams("parallel"),
    )(x, fox_o, mla_o, gf, gm, w_o)


def _mlp_tile(t):
    return 256 if t >= 2048 else 128


def _resident(a):
    return pl.BlockSpec(a.shape, lambda i: (0,) * a.ndim, pipeline_mode=pl.Buffered(1))


FF_CHUNK = 512


def _mlp_fwd(x1, g_mlp, w_up, w_down, g_fin, target):
    t = x1.shape[0]
    tm = _mlp_tile(t)

    def body(x_ref, g_ref, wu_ref, wd_ref, gf_ref, t_ref, u_ref, h_ref, dx_ref, dxb_ref, loss_ref, dg_ref, a_sc):
        @pl.when(pl.program_id(0) == 0)
        def _():
            loss_ref[...] = jnp.zeros_like(loss_ref)
            dg_ref[...] = jnp.zeros_like(dg_ref)

        x = x_ref[...]
        h, _ = _rms(x, g_ref[...])
        hb = h.astype(BF16)
        h_ref[...] = hb
        for f in range(D_FF // FF_CHUNK):
            sl = slice(f * FF_CHUNK, (f + 1) * FF_CHUNK)
            u = _dot(hb, wu_ref[:, sl])
            u_ref[:, sl] = u
            r = jnp.maximum(u, 0.0)
            a_sc[:, sl] = (r * r).astype(BF16)
        x2 = x + _dot(a_sc[...], wd_ref[...])
        y, r2 = _rms(x2, gf_ref[...])
        err = y - t_ref[...]
        loss_ref[...] += 0.5 * jnp.sum(jnp.mean(err * err, axis=-1, keepdims=True))
        dx, dg = _rms_bwd(x2, gf_ref[...], r2, err * (1.0 / D_MODEL))
        dx_ref[...] = dx
        dxb_ref[...] = dx.astype(BF16)
        dg_ref[...] += dg

    row = lambda n: pl.BlockSpec((tm, n), lambda i: (i, 0))
    vec = pl.BlockSpec((1, D_MODEL), lambda i: (0, 0))
    return pl.pallas_call(
        body, name="mlp_fwd", grid=(t // tm,),
        out_shape=(jax.ShapeDtypeStruct((t, D_FF), F32), jax.ShapeDtypeStruct((t, D_MODEL), BF16),
                   jax.ShapeDtypeStruct((t, D_MODEL), F32), jax.ShapeDtypeStruct((t, D_MODEL), BF16),
                   jax.ShapeDtypeStruct((8, LANES), F32), jax.ShapeDtypeStruct((1, D_MODEL), F32)),
        in_specs=[row(D_MODEL), vec, _resident(w_up), _resident(w_down), vec, row(D_MODEL)],
        out_specs=(row(D_FF), row(D_MODEL), row(D_MODEL), row(D_MODEL), pl.BlockSpec((8, LANES), lambda i: (0, 0)), vec),
        scratch_shapes=[pltpu.VMEM((tm, D_FF), BF16)],
        compiler_params=_cparams("arbitrary"),
    )(x1, g_mlp, w_up, w_down, g_fin, target)


def _mlp_bwd(dx2, u, x1, g_mlp, w_up, w_down):
    t = x1.shape[0]
    tm = _mlp_tile(t)

    def body(dx_ref, u_ref, x_ref, g_ref, wu_ref, wd_ref, du_ref, a_ref, dx1_ref, dx1b_ref, dg_ref):
        @pl.when(pl.program_id(0) == 0)
        def _():
            dg_ref[...] = jnp.zeros_like(dg_ref)

        dx2 = dx_ref[...]
        dxb = dx2.astype(BF16)
        for f in range(D_FF // FF_CHUNK):
            sl = slice(f * FF_CHUNK, (f + 1) * FF_CHUNK)
            r = jnp.maximum(u_ref[:, sl], 0.0)
            a_ref[:, sl] = (r * r).astype(BF16)
            da = _dot_nt(dxb, wd_ref[sl, :])
            du_ref[:, sl] = (da * (2.0 * r)).astype(BF16)
        dh = _dot_nt(du_ref[...], wu_ref[...])
        x = x_ref[...]
        _, r1 = _rms(x, g_ref[...])
        dx, dg = _rms_bwd(x, g_ref[...], r1, dh)
        dx1 = dx2 + dx
        dx1_ref[...] = dx1
        dx1b_ref[...] = dx1.astype(BF16)
        dg_ref[...] += dg

    row = lambda n: pl.BlockSpec((tm, n), lambda i: (i, 0))
    vec = pl.BlockSpec((1, D_MODEL), lambda i: (0, 0))
    return pl.pallas_call(
        body, name="mlp_bwd", grid=(t // tm,),
        out_shape=(jax.ShapeDtypeStruct((t, D_FF), BF16), jax.ShapeDtypeStruct((t, D_FF), BF16),
                   jax.ShapeDtypeStruct((t, D_MODEL), F32), jax.ShapeDtypeStruct((t, D_MODEL), BF16),
                   jax.ShapeDtypeStruct((1, D_MODEL), F32)),
        in_specs=[row(D_MODEL), row(D_FF), row(D_MODEL), vec, _resident(w_up), _resident(w_down)],
        out_specs=(row(D_FF), row(D_FF), row(D_MODEL), row(D_MODEL), vec),
        compiler_params=_cparams("arbitrary"),
    )(dx2, u, x1, g_mlp, w_up, w_down)


def _matmul_tn(name, a, b, blocks=None):
    t, m = a.shape
    n = b.shape[1]
    tk = t if a.dtype == BF16 and b.dtype == BF16 else min(t, 2048)
    steps = t // tk
    bm = m if m <= 1024 else 512
    bn = n if n <= 1024 else 512
    width = bn if blocks is None else n // blocks
    per = bn // width

    def body(a_ref, b_ref, o_ref, acc_sc):
        kk = pl.program_id(2)

        @pl.when(kk == 0)
        def _():
            acc_sc[...] = jnp.zeros_like(acc_sc)

        acc_sc[...] += _dot_tn(a_ref[...].astype(BF16), b_ref[...].astype(BF16))

        @pl.when(kk == steps - 1)
        def _():
            if blocks is None:
                o_ref[...] = acc_sc[...]
            else:
                for s in range(per):
                    o_ref[s] = acc_sc[:, s * width:(s + 1) * width]

    if blocks is None:
        o_spec = pl.BlockSpec((bm, bn), lambda i, j, kk: (i, j))
        o_shape = (m, n)
    else:
        o_spec = pl.BlockSpec((per, bm, width), lambda i, j, kk: (j, i, 0))
        o_shape = (blocks, m, width)
    return pl.pallas_call(
        body, name=name, grid=(m // bm, n // bn, steps),
        out_shape=jax.ShapeDtypeStruct(o_shape, F32),
        in_specs=[pl.BlockSpec((tk, bm), lambda i, j, kk: (kk, i)), pl.BlockSpec((tk, bn), lambda i, j, kk: (kk, j))],
        out_specs=o_spec,
        scratch_shapes=[pltpu.VMEM((bm, bn), F32)],
        compiler_params=_cparams("parallel", "parallel", "arbitrary"),
    )(a, b)


def _dw_in(dfq, dfk, dfv, drest, h1):
    t = h1.shape[0]
    tk = min(t, 1024)
    off_ff = 3 * FOX_WIDTH
    off_cq = off_ff + HEADS
    off_kr = IN_COLS - ROPE

    def body(dq_ref, dk_ref, dv_ref, dr_ref, h_ref, o_ref):
        h = h_ref[...]
        r = _dot_tn(dr_ref[...], h)
        parts = [(slice(n * FOX_WIDTH, (n + 1) * FOX_WIDTH), _dot_tn(ref[...], h)) for n, ref in enumerate((dq_ref, dk_ref, dv_ref))]
        parts += [(slice(off_ff, off_cq), r[0:HEADS]), (slice(off_cq, off_kr), r[REST_CQ:REST_KR]),
                  (slice(off_kr, IN_COLS), r[REST_KR + NOPE:REST_KR + NOPE + ROPE])]

        @pl.when(pl.program_id(0) == 0)
        def _():
            for rows, val in parts:
                o_ref[rows, :] = val

        @pl.when(pl.program_id(0) > 0)
        def _():
            for rows, val in parts:
                o_ref[rows, :] += val

    tok = lambda n: pl.BlockSpec((tk, n), lambda kk: (kk, 0))
    return pl.pallas_call(
        body, name="dw_in", grid=(t // tk,),
        out_shape=jax.ShapeDtypeStruct((IN_COLS, D_MODEL), F32),
        in_specs=[tok(FOX_WIDTH), tok(FOX_WIDTH), tok(FOX_WIDTH), tok(REST_COLS), tok(D_MODEL)],
        out_specs=pl.BlockSpec((IN_COLS, D_MODEL), lambda kk: (0, 0)),
        compiler_params=_cparams("arbitrary"),
    )(dfq, dfk, dfv, drest, h1)


def _attn_out_bwd(dx1, fox_o, mla_o, gf, gm, w_o):
    t = dx1.shape[0]
    tm = _row_tile(t)

    def body(dx_ref, f_ref, m_ref, gf_ref, gm_ref, w_ref, df_ref, dm_ref, dlf_ref, dlm_ref, dgf_ref, dgm_ref):
        @pl.when(pl.program_id(0) == 0)
        def _():
            dgf_ref[...] = jnp.zeros_like(dgf_ref)
            dgm_ref[...] = jnp.zeros_like(dgm_ref)
        dxb = dx_ref[...].astype(BF16)
        lane = lax.broadcasted_iota(jnp.int32, (8, LANES), 1)
        picks = [(lane < HEAD_DIM).astype(BF16), (lane >= HEAD_DIM).astype(BF16)]
        for o_ref, g_ref, lo_row, d_ref, dl_ref, dg_ref in ((f_ref, gf_ref, 0, df_ref, dlf_ref, dgf_ref),
                                                             (m_ref, gm_ref, FOX_WIDTH, dm_ref, dlm_ref, dgm_ref)):
            dn = _dot_nt(dxb, w_ref[lo_row:lo_row + FOX_WIDTH, :])
            o = o_ref[...]
            _, r = _rms(o, g_ref[...])
            d, dg = _rms_bwd(o, g_ref[...], r, dn)
            d_ref[...] = d
            dg_ref[...] += dg
            prod = d * o
            for h in range(HEADS):
                parts = _split3(prod[:, (h // 2) * LANES:(h // 2 + 1) * LANES])
                dl_ref[h, 0] = (_dot_nt(picks[h % 2], parts[0]) + _dot_nt(picks[h % 2], parts[1])) + _dot_nt(picks[h % 2], parts[2])

    row = lambda n: pl.BlockSpec((tm, n), lambda i: (i, 0))
    full = lambda a: pl.BlockSpec(a.shape, lambda i: (0,) * a.ndim)
    vec = pl.BlockSpec((1, FOX_WIDTH), lambda i: (0, 0))
    rows = pl.BlockSpec((HEADS, 1, 8, tm), lambda i: (0, i, 0, 0))
    o_shape = jax.ShapeDtypeStruct((t, FOX_WIDTH), F32)
    g_shape = jax.ShapeDtypeStruct((1, FOX_WIDTH), F32)
    r_shape = jax.ShapeDtypeStruct((HEADS, t // tm, 8, tm), F32)
    return pl.pallas_call(
        body, name="attn_out_bwd", grid=(t // tm,),
        out_shape=(o_shape, o_shape, r_shape, r_shape, g_shape, g_shape),
        in_specs=[row(D_MODEL), row(FOX_WIDTH), row(MLA_WIDTH), full(gf), full(gm), full(w_o)],
        out_specs=(row(FOX_WIDTH), row(MLA_WIDTH), rows, rows, vec, vec),
        compiler_params=_cparams("arbitrary"),
    )(dx1, fox_o, mla_o, gf, gm, w_o)


def _mla_prep_bwd(dq, dkv, dkr, dz, rest, gq, gkv, wq, wkv, cos, sin):
    t = rest.shape[0]
    tm = _row_tile(t)

    def body(dq_ref, dkv_ref, dkr_ref, dz_ref, r_ref, gq_ref, gkv_ref, wq_ref, wkv_ref, c_ref, s_ref,
             dr_ref, dqp_ref, dkvb_ref, dgq_ref, dgkv_ref):
        @pl.when(pl.program_id(0) == 0)
        def _():
            dgq_ref[...] = jnp.zeros_like(dgq_ref)
            dgkv_ref[...] = jnp.zeros_like(dgkv_ref)
        cos_, sin_ = c_ref[...], s_ref[...]
        dcq = jnp.zeros((tm, Q_RANK), F32)
        dckv = jnp.zeros((tm, KV_RANK), F32)
        for h in range(HEADS):
            dqp = _rope_bwd(dq_ref[h], cos_, sin_).astype(BF16)
            dqp_ref[:, h * LANES:(h + 1) * LANES] = dqp
            dcq = dcq + _dot_nt(dqp, wq_ref[h])
            dkvb = dkv_ref[h].astype(BF16)
            dkvb_ref[:, h * LANES:(h + 1) * LANES] = dkvb
            dckv = dckv + _dot_nt(dkvb, wkv_ref[h])
        dkrope = dkr_ref[0]
        for pr in range(1, HEADS // 2):
            dkrope = dkrope + dkr_ref[pr]
        cq = r_ref[:, REST_CQ:REST_CKV]
        _, rq = _rms(cq, gq_ref[...])
        d_cq, dgq = _rms_bwd(cq, gq_ref[...], rq, dcq)
        ckv = r_ref[:, REST_CKV:REST_KR]
        _, rkv = _rms(ckv, gkv_ref[...])
        d_ckv, dgkv = _rms_bwd(ckv, gkv_ref[...], rkv, dckv)
        dgq_ref[...] += dgq
        dgkv_ref[...] += dgkv
        dr_ref[:, 0:REST_CQ] = dz_ref[...].astype(BF16)
        dr_ref[:, REST_CQ:REST_CKV] = d_cq.astype(BF16)
        dr_ref[:, REST_CKV:REST_KR] = d_ckv.astype(BF16)
        dr_ref[:, REST_KR:REST_COLS] = _rope_bwd(dkrope, cos_, sin_).astype(BF16)

    row = lambda n: pl.BlockSpec((tm, n), lambda i: (i, 0))
    full = lambda a: pl.BlockSpec(a.shape, lambda i: (0,) * a.ndim)
    heads = pl.BlockSpec((HEADS, tm, LANES), lambda i: (0, i, 0))
    hshape = jax.ShapeDtypeStruct((t, HEADS * LANES), BF16)
    return pl.pallas_call(
        body, name="mla_prep_bwd", grid=(t // tm,),
        out_shape=(jax.ShapeDtypeStruct((t, REST_COLS), BF16), hshape, hshape,
                   jax.ShapeDtypeStruct((1, Q_RANK), F32), jax.ShapeDtypeStruct((1, KV_RANK), F32)),
        in_specs=[heads, heads, pl.BlockSpec((HEADS // 2, tm, LANES), lambda i: (0, i, 0)), row(LANES), row(REST_COLS),
                  full(gq), full(gkv), full(wq), full(wkv), row(LANES), row(LANES)],
        out_specs=(row(REST_COLS), row(HEADS * LANES), row(HEADS * LANES), pl.BlockSpec((1, Q_RANK), lambda i: (0, 0)),
                   pl.BlockSpec((1, KV_RANK), lambda i: (0, 0))),
        compiler_params=_cparams("arbitrary"),
    )(dq, dkv, dkr, dz, rest, gq, gkv, wq, wkv, cos, sin)


def _in_proj_bwd(x, g, dx1, dfq, dfk, dfv, drest, w_qkv, w_rest, comm=None):
    t = x.shape[0]
    tm = _row_tile(t)

    def main(ins, outs, scr):
        x_ref, g_ref, dx1_ref, dq_ref, dk_ref, dv_ref, dr_ref, wq_ref, wr_ref = ins
        dx_ref, dg_ref = outs

        @pl.when(pl.program_id(0) == 0)
        def _():
            dg_ref[...] = jnp.zeros_like(dg_ref)
        dh = _dot(dr_ref[...], wr_ref[...])
        for n, ref in enumerate((dq_ref, dk_ref, dv_ref)):
            dh = dh + _dot(ref[...], wq_ref[n * FOX_WIDTH:(n + 1) * FOX_WIDTH, :])
        xv = x_ref[...]
        _, r = _rms(xv, g_ref[...])
        dx, dg = _rms_bwd(xv, g_ref[...], r, dh)
        dx_ref[...] = dx1_ref[...] + dx
        dg_ref[...] += dg

    row = lambda n: pl.BlockSpec((tm, n), lambda i: (i, 0))
    full = lambda a: pl.BlockSpec(a.shape, lambda i: (0,) * a.ndim)
    vec = pl.BlockSpec((1, D_MODEL), lambda i: (0, 0))
    return _hosted_call(
        "in_proj_bwd", main, (t // tm,), [x, g, dx1, dfq, dfk, dfv, drest, w_qkv, w_rest],
        [row(D_MODEL), full(g), row(D_MODEL), row(FOX_WIDTH), row(FOX_WIDTH), row(FOX_WIDTH), row(REST_COLS),
         full(w_qkv), full(w_rest)],
        (jax.ShapeDtypeStruct((t, D_MODEL), F32), jax.ShapeDtypeStruct((1, D_MODEL), F32)), (row(D_MODEL), vec), [], comm)


def _pad_cols(a, n):
    return jnp.pad(a, ((0, 0),) * (a.ndim - 1) + ((0, n - a.shape[-1]),))


def kernel(x, positions, attn_norm_g, w_in, b_forget, q_norm_g, w_uq, kv_norm_g, w_ukv, fox_out_g, mla_out_g, w_o, mlp_norm_g, w_up, w_down, final_norm_g, loss_target, m_attn_norm_g, m_w_in, m_b_forget, m_q_norm_g, m_w_uq, m_kv_norm_g, m_w_ukv, m_fox_out_g, m_mla_out_g, m_w_o, m_mlp_norm_g, m_w_up, m_w_down, m_final_norm_g, v_attn_norm_g, v_w_in, v_b_forget, v_q_norm_g, v_w_uq, v_kv_norm_g, v_w_ukv, v_fox_out_g, v_mla_out_g, v_w_o, v_mlp_norm_g, v_w_up, v_w_down, v_final_norm_g):
    t = x.shape[1]
    tq = _row_tile(t)
    xs = x[0]
    target = loss_target[0]

    mid = [_pad_cols(w_uq[0], LANES).astype(BF16), w_ukv[0].astype(BF16)]
    late = [w_o[0].astype(BF16), w_up[0].astype(BF16), w_down[0].astype(BF16)]
    g_in, = _all_gather([jnp.transpose(w_in[0]).astype(BF16)])
    win = g_in.reshape(IN_COLS, D_MODEL)
    off_ff, off_cq, off_kr = 3 * FOX_WIDTH, 3 * FOX_WIDTH + HEADS, IN_COLS - ROPE
    zeros = lambda n: jnp.zeros((n, D_MODEL), BF16)
    w_qkv = win[:off_ff]
    w_rest = jnp.concatenate([
        win[off_ff:off_cq], zeros(REST_CQ - HEADS), win[off_cq:off_kr],
        zeros(NOPE), win[off_kr:], zeros(LANES - NOPE - ROPE)], axis=0)

    cos, sin = _rope_tables(positions.reshape(t, 1))
    (h1, fq, fk, fv, rest), (wq, wkv) = _in_proj(xs, attn_norm_g, w_qkv, w_rest, comm=_ag_to_all(mid))
    b128 = _pad_cols(b_forget, LANES)
    f2_rows, f2_rep = _forget_cumsum(rest, b128)
    f2_rows = f2_rows.reshape(HEADS, t // tq, 1, tq)
    (fox_o, fox_lse_rows), partly = _attn_fwd(True, fq, fk, fv, f2_rows, comm=_ag_direct(late))
    mq, mk, mkv, cqn, ckvn = _mla_prep(rest, q_norm_g, kv_norm_g, wq, wkv, cos, sin)
    (mla_o, mla_lse_rows), (g_o, g_up, g_down) = _attn_fwd(False, mq, mk, mkv, comm=_ag_forward(partly))
    wo = g_o.reshape(D_MODEL, D_MODEL)
    x1, mixed = _attn_out(xs, fox_o, mla_o, fox_out_g, mla_out_g, wo)
    wup = jnp.transpose(g_up, (1, 0, 2)).reshape(D_MODEL, D_FF)
    wdown = g_down.reshape(D_FF, D_MODEL)
    u, h2, dx2, dx2b, loss8, d_gfin = _mlp_fwd(x1, mlp_norm_g, wup, wdown, final_norm_g.reshape(1, D_MODEL), target)

    du, act, dx1, dx1b, d_gmlp = _mlp_bwd(dx2, u, x1, mlp_norm_g, wup, wdown)
    dw_down = _matmul_tn("dw_down", act, dx2b)
    dw_up = _matmul_tn("dw_up", h2, du, blocks=N_DEV)
    dfox_o, dmla_o, fox_delta_rows, mla_delta_rows, d_gfox, d_gmla = _attn_out_bwd(dx1, fox_o, mla_o, fox_out_g, mla_out_g, wo)
    dw_o = _matmul_tn("dw_o", mixed, dx1b)

    place = jnp.stack([lax.axis_index("c"), 2 * lax.axis_index("x") + lax.axis_index("y")]).astype(jnp.int32)
    names = ("w_in", "w_uq", "w_ukv", "w_o", "w_up", "w_down")
    grads_b = [dw_o.reshape(N_DEV, -1, D_MODEL), dw_up, dw_down.reshape(N_DEV, -1, D_MODEL)]
    (dfq, dfk, dfv, d_fq, d_fk), got_b = _attn_bwd(True, fq, fk, fv, dfox_o, fox_lse_rows, fox_delta_rows,
                                                   f2_rep, comm=_rs_to_sibling(grads_b))
    sums_b = [_rs_sibling_sum("rs_sibling_sum_" + nm, g, l, place) for nm, g, l in zip(names[3:], grads_b, got_b)]
    dz, d_b = _forget_bwd(rest, b128, d_fq, d_fk)
    (dmq, dmkv, dmkr), others_b = _attn_bwd(False, mq, mk, mkv, dmla_o, mla_lse_rows, mla_delta_rows,
                                            comm=_rs_to_chips([s[1] for s in sums_b]))
    drest, dqp, dkvb, d_gq, d_gkv = _mla_prep_bwd(dmq, dmkv, dmkr, dz, rest, q_norm_g, kv_norm_g, wq, wkv, cos, sin)
    dw_uq = _matmul_tn("dw_uq", cqn, dqp, blocks=HEADS)
    dw_ukv = _matmul_tn("dw_ukv", ckvn, dkvb, blocks=HEADS)
    dw_in = _dw_in(dfq, dfk, dfv, drest, h1)

    grads_a = [dw_in.reshape(N_DEV, IN_SHARD, D_MODEL), dw_uq, dw_ukv]
    got_a = _comm_call("rs_sibling_exchange", _rs_to_sibling(grads_a))
    sums_a = [_rs_sibling_sum("rs_sibling_sum_" + nm, g, l, place) for nm, g, l in zip(names[:3], grads_a, got_a)]
    (grad_x, d_gattn), others_a = _in_proj_bwd(xs, attn_norm_g, dx1, dfq, dfk, dfv, drest, w_qkv, w_rest,
                                               comm=_rs_to_chips([s[1] for s in sums_a]))
    sums, others = sums_a + sums_b, list(others_a) + list(others_b)
    sharded = (w_in, w_uq, w_ukv, w_o, w_up, w_down)
    moments_m = (m_w_in, m_w_uq, m_w_ukv, m_w_o, m_w_up, m_w_down)
    moments_v = (v_w_in, v_w_uq, v_w_ukv, v_w_o, v_w_up, v_w_down)
    g_in_t = _rs_final_sum("rs_final_sum_w_in", sums[0][0], others[0])
    big = [_adamw_given("adamw_w_in", jnp.transpose(g_in_t), w_in, m_w_in, v_w_in)]
    for a in range(1, len(names)):
        big.append(_adamw_sharded("adamw_" + names[a], sharded[a], moments_m[a], moments_v[a], sums[a][0], others[a]))
    big_g, big_d, big_m, big_v = [[b[k] for b in big] for k in range(4)]

    as_row = lambda a: a.reshape(1, -1)
    small_w = (attn_norm_g, b_forget, q_norm_g, kv_norm_g, fox_out_g, mla_out_g, mlp_norm_g, final_norm_g)
    small_m = (m_attn_norm_g, m_b_forget, m_q_norm_g, m_kv_norm_g, m_fox_out_g, m_mla_out_g, m_mlp_norm_g, m_final_norm_g)
    small_v = (v_attn_norm_g, v_b_forget, v_q_norm_g, v_kv_norm_g, v_fox_out_g, v_mla_out_g, v_mlp_norm_g, v_final_norm_g)
    total = _small_all_reduce([d_gattn, d_b, d_gq, d_gkv, d_gfox, d_gmla, d_gmlp, d_gfin], loss8)
    small = _adamw_small(total, [as_row(a) for a in small_w], [as_row(a) for a in small_m], [as_row(a) for a in small_v])
    loss = small[0].reshape(())
    s_g, s_d, s_m, s_v = [[small[1 + 4 * r + k].reshape(small_w[r].shape) for r in range(len(small_w))] for k in range(4)]

    def ordered(small_, bigs):
        ga, bf, gq_, gkv_, gfo, gml, gmlp_, gfin_ = small_
        bin_, buq, bukv, bo, bup, bdown = bigs
        return [ga, bin_, bf, gq_, buq, gkv_, bukv, gfo, gml, bo, gmlp_, bup, bdown, gfin_]

    return (loss, grad_x[None], *ordered(s_g, big_g), *ordered(s_d, big_d), *ordered(s_m, big_m), *ordered(s_v, big_v))
```

```python
import math
from typing import Callable, NamedTuple

import numpy as np
import jax
import jax.numpy as jnp
from jax import lax
from jax.experimental import pallas as pl
from jax.experimental.pallas import tpu as pltpu

F32 = jnp.float32
BF16 = jnp.bfloat16
MESH = pl.DeviceIdType.MESH

D_MODEL = 1024
HEADS = 8
HEAD_DIM = 64
FOX_WIDTH = 512
MLA_WIDTH = 512
NOPE = 64
ROPE = 32
QK_DIM = 96
Q_RANK = 384
KV_RANK = 256
D_FF = 4096
IN_COLS = 2216
ROPE_THETA = 10000.0
EPS = 1e-6
FOX_SCALE = 1.0 / math.sqrt(HEAD_DIM)
MLA_SCALE = 1.0 / math.sqrt(QK_DIM)
ADAM_LR = 0.001
ADAM_B1 = 0.9
ADAM_B2 = 0.999
ADAM_EPS = 1e-08
ADAM_WD = 0.01
ADAM_STEP = 10

N_DEV = 8
LANES = 128
REST_COLS = 896
REST_CQ = LANES
REST_CKV = REST_CQ + Q_RANK
REST_KR = REST_CKV + KV_RANK
LOG2E = 1.4426950408889634
LN2 = 0.6931471805599453
FOX_Q_FACTOR = FOX_SCALE * LOG2E
MLA_Q_FACTOR = MLA_SCALE * LOG2E
VMEM_LIMIT = 56 * 1024 * 1024

IN_SHARD = IN_COLS // N_DEV
SMALL_SIZES = (1024, 8, 384, 256, 512, 512, 1024, 1024)
SMALL_ROWS = 16
LOSS_ROW = len(SMALL_SIZES)


def _cparams(*sem):
    return pltpu.CompilerParams(dimension_semantics=sem or None, vmem_limit_bytes=VMEM_LIMIT)


def _row_tile(t):
    return 512 if t >= 2048 else (256 if t >= 512 else 128)


def _dot(a, b):
    return jnp.dot(a, b, preferred_element_type=F32)


def _dot_nt(a, b):
    return lax.dot_general(a, b, (((1,), (1,)), ((), ())), preferred_element_type=F32)


def _dot_tn(a, b):
    return lax.dot_general(a, b, (((0,), (0,)), ((), ())), preferred_element_type=F32)


def _rms(x, g):
    r = lax.rsqrt(jnp.mean(x * x, axis=-1, keepdims=True) + EPS)
    return x * r * g, r


def _rms_bwd(x, g, r, dy):
    xh = x * r
    gdy = dy * g
    dx = r * (gdy - xh * jnp.mean(gdy * xh, axis=-1, keepdims=True))
    return dx, jnp.sum(dy * xh, axis=0, keepdims=True)


def _lane():
    return lax.broadcasted_iota(jnp.int32, (1, LANES), 1)


def _rot(x):
    lane = _lane()
    half = NOPE + ROPE // 2
    first = jnp.logical_and(lane >= NOPE, lane < half)
    second = jnp.logical_and(lane >= half, lane < NOPE + ROPE)
    return jnp.where(first, -pltpu.roll(x, LANES - ROPE // 2, 1), jnp.where(second, pltpu.roll(x, ROPE // 2, 1), 0.0))


def _rope(x, cos, sin):
    return x * cos + _rot(x) * sin


def _rope_bwd(dy, cos, sin):
    return dy * cos - _rot(dy * sin)


def _remote(src, dst, send_sem, recv_sem, to):
    return pltpu.make_async_remote_copy(src_ref=src, dst_ref=dst, send_sem=send_sem, recv_sem=recv_sem,
                                        device_id=to, device_id_type=MESH)


def _hbm_specs(n):
    return [pl.BlockSpec(memory_space=pl.ANY)] * n


def _all_gather(blocks):
    n = len(blocks)

    def body(*refs):
        x_refs, out_refs = refs[:n], refs[n:2 * n]
        send_sems, recv_sems, local_sems = refs[2 * n:]
        x, y, c = lax.axis_index("x"), lax.axis_index("y"), lax.axis_index("c")
        me, sibling = (x, y, c), (x, y, 1 - c)
        chips = [(1 - x, y), (x, 1 - y), (1 - x, 1 - y)]

        def slot(a, px, py, pc):
            return out_refs[a].at[4 * px + 2 * py + pc]

        def copy(a, k, blk, to, src=None):
            return _remote(slot(a, *blk) if src is None else src, slot(a, *blk),
                           send_sems.at[7 * a + k], recv_sems.at[7 * a + k], to)

        mine = [pltpu.make_async_copy(x_refs[a], slot(a, *me), local_sems.at[a]) for a in range(n)]
        first, passed = [], []
        for a in range(n):
            mine[a].start()
            first.append(copy(a, 0, me, sibling, src=x_refs[a]))
            first += [copy(a, 1 + j, me, (*chip, c), src=x_refs[a]) for j, chip in enumerate(chips)]
        for cp in first:
            cp.start()
        for a in range(n):
            for j, chip in enumerate(chips):
                copy(a, 1 + j, (*chip, c), me).wait_recv()
                passed.append(copy(a, 4 + j, (*chip, c), sibling))
                passed[-1].start()
        for a in range(n):
            copy(a, 0, sibling, me).wait_recv()
            for j, chip in enumerate(chips):
                copy(a, 4 + j, (*chip, 1 - c), me).wait_recv()
        for cp in first + passed:
            cp.wait_send()
        for cp in mine:
            cp.wait()

    return pl.pallas_call(
        body, name="all_gather_weights",
        out_shape=[jax.ShapeDtypeStruct((N_DEV,) + b.shape, b.dtype) for b in blocks],
        in_specs=_hbm_specs(n), out_specs=_hbm_specs(n),
        scratch_shapes=[pltpu.SemaphoreType.DMA((7 * n,)), pltpu.SemaphoreType.DMA((7 * n,)), pltpu.SemaphoreType.DMA((n,))],
    )(*blocks)


def _symmetric_comm(inputs, out_shape, aliases, per_array, copies):
    def start(in_refs, out_refs, sems):
        for cp in copies(in_refs, out_refs, *sems):
            cp.start()

    def finish(in_refs, out_refs, sems):
        for cp in copies(in_refs, out_refs, *sems):
            cp.wait()

    n_sems = per_array * len(inputs)
    return _Comm(tuple(inputs), tuple(out_shape), aliases,
                 (pltpu.SemaphoreType.DMA((n_sems,)), pltpu.SemaphoreType.DMA((n_sems,))), start, finish)


def _ag_direct(shards):
    def copies(in_refs, out_refs, send_sems, recv_sems):
        x, y, c = lax.axis_index("x"), lax.axis_index("y"), lax.axis_index("c")
        peers = [(x, y, 1 - c), (1 - x, y, c), (x, 1 - y, c), (1 - x, 1 - y, c)]
        cps = []
        for a in range(len(shards)):
            mine = out_refs[a].at[4 * x + 2 * y + c]
            cps.append(pltpu.make_async_copy(in_refs[a], mine, send_sems.at[5 * a]))
            cps += [_remote(in_refs[a], mine, send_sems.at[5 * a + k], recv_sems.at[5 * a + k], peer)
                    for k, peer in enumerate(peers, start=1)]
        return cps

    return _symmetric_comm(shards, [jax.ShapeDtypeStruct((N_DEV,) + s.shape, s.dtype) for s in shards], {}, 5, copies)


def _ag_to_all(shards):
    def copies(in_refs, out_refs, send_sems, recv_sems):
        x, y, c = lax.axis_index("x"), lax.axis_index("y"), lax.axis_index("c")
        cps = []
        for a in range(len(shards)):
            mine = out_refs[a].at[4 * x + 2 * y + c]
            cps.append(pltpu.make_async_copy(in_refs[a], mine, send_sems.at[N_DEV * a]))
            for k in range(1, N_DEV):
                peer = (x ^ (k >> 2), y ^ ((k >> 1) & 1), c ^ (k & 1))
                cps.append(_remote(in_refs[a], mine, send_sems.at[N_DEV * a + k], recv_sems.at[N_DEV * a + k], peer))
        return cps

    return _symmetric_comm(shards, [jax.ShapeDtypeStruct((N_DEV,) + s.shape, s.dtype) for s in shards], {}, N_DEV, copies)


def _ag_forward(gathered):
    def copies(in_refs, out_refs, send_sems, recv_sems):
        x, y, c = lax.axis_index("x"), lax.axis_index("y"), lax.axis_index("c")
        chips = [(1 - x, y), (x, 1 - y), (1 - x, 1 - y)]
        return [_remote(out_refs[a].at[4 * cx + 2 * cy + c], out_refs[a].at[4 * cx + 2 * cy + c],
                        send_sems.at[3 * a + j], recv_sems.at[3 * a + j], (x, y, 1 - c))
                for a in range(len(gathered)) for j, (cx, cy) in enumerate(chips)]

    shapes = [jax.ShapeDtypeStruct(g.shape, g.dtype) for g in gathered]
    return _symmetric_comm(gathered, shapes, {a: a for a in range(len(gathered))}, 3, copies)


def _rs_to_sibling(grads):
    def copies(in_refs, out_refs, send_sems, recv_sems):
        x, y, c = lax.axis_index("x"), lax.axis_index("y"), lax.axis_index("c")
        return [_remote(in_refs[a].at[2 * q + 1 - c], out_refs[a].at[q], send_sems.at[4 * a + q], recv_sems.at[4 * a + q], (x, y, 1 - c))
                for a in range(len(grads)) for q in range(4)]

    return _symmetric_comm(grads, [jax.ShapeDtypeStruct((4,) + g.shape[1:], g.dtype) for g in grads], {}, 4, copies)


def _rs_to_chips(parts):
    def copies(in_refs, out_refs, send_sems, recv_sems):
        x, y, c = lax.axis_index("x"), lax.axis_index("y"), lax.axis_index("c")
        chips = [(1 - x, y), (x, 1 - y), (1 - x, 1 - y)]
        return [_remote(in_refs[a].at[2 * cx + cy], out_refs[a].at[k], send_sems.at[3 * a + k], recv_sems.at[3 * a + k], (cx, cy, c))
                for a in range(len(parts)) for k, (cx, cy) in enumerate(chips)]

    return _symmetric_comm(parts, [jax.ShapeDtypeStruct((3,) + p.shape[1:], p.dtype) for p in parts], {}, 3, copies)


def _comm_call(name, comm):
    n_in, n_out = len(comm.inputs), len(comm.out_shape)

    def body(*refs):
        ins, outs, sems = refs[:n_in], refs[n_in:n_in + n_out], refs[n_in + n_out:]
        comm.start(ins, outs, sems)
        comm.finish(ins, outs, sems)

    return pl.pallas_call(
        body, name=name, out_shape=list(comm.out_shape), in_specs=_hbm_specs(n_in), out_specs=_hbm_specs(n_out),
        scratch_shapes=list(comm.scratch), input_output_aliases=dict(comm.aliases),
    )(*comm.inputs)


def _small_all_reduce(parts, loss8):
    n = len(parts)

    def body(*refs):
        p_refs, loss_ref, out_ref, pack, land, send_sems, recv_sems = refs[:n], *refs[n:]
        x, y, c = lax.axis_index("x"), lax.axis_index("y"), lax.axis_index("c")
        me = 4 * x + 2 * y + c
        pack[...] = jnp.zeros_like(pack)
        for r, ref in enumerate(p_refs):
            pack[r:r + 1, 0:ref.shape[1]] = ref[...]
        pack[LOSS_ROW:LOSS_ROW + 1, 0:LANES] = loss_ref[0:1, :]
        land[me] = pack[...]
        cps = []
        for k in range(1, N_DEV):
            peer = (x ^ (k >> 2), y ^ ((k >> 1) & 1), c ^ (k & 1))
            cps.append(_remote(pack, land.at[me], send_sems.at[k - 1], recv_sems.at[k - 1], peer))
        for cp in cps:
            cp.start()
        for cp in cps:
            cp.wait()
        acc = land[0]
        for d in range(1, N_DEV):
            acc = acc + land[d]
        out_ref[...] = acc

    vmem = pl.BlockSpec(memory_space=pltpu.VMEM)
    return pl.pallas_call(
        body, name="small_all_reduce",
        out_shape=jax.ShapeDtypeStruct((SMALL_ROWS, D_MODEL), F32),
        in_specs=[vmem] * (n + 1), out_specs=vmem,
        scratch_shapes=[pltpu.VMEM((SMALL_ROWS, D_MODEL), F32), pltpu.VMEM((N_DEV, SMALL_ROWS, D_MODEL), F32),
                        pltpu.SemaphoreType.DMA((N_DEV - 1,)), pltpu.SemaphoreType.DMA((N_DEV - 1,))],
    )(*parts, loss8)


def _rs_sibling_sum(name, grad, got, place):
    _, rows, cols = grad.shape

    def body(place_ref, g_ref, l_ref, own_ref, b_ref):
        s = g_ref[...] + l_ref[...]
        b_ref[...] = s.astype(BF16)

        @pl.when(pl.program_id(0) == place_ref[1])
        def _():
            own_ref[...] = s

    by_chip = pl.BlockSpec((None, rows, cols), lambda q, place_ref: (q, 0, 0))
    return pl.pallas_call(
        body, name=name,
        grid_spec=pltpu.PrefetchScalarGridSpec(
            num_scalar_prefetch=1, grid=(4,),
            in_specs=[pl.BlockSpec((None, rows, cols), lambda q, place_ref: (2 * q + place_ref[0], 0, 0)), by_chip],
            out_specs=[pl.BlockSpec((rows, cols), lambda q, place_ref: (0, 0)), by_chip]),
        out_shape=(jax.ShapeDtypeStruct((rows, cols), F32), jax.ShapeDtypeStruct((4, rows, cols), BF16)),
        compiler_params=_cparams("arbitrary"),
    )(place, grad, got)


def _adamw_math(w, g, m, v):
    m2 = ADAM_B1 * m + (1.0 - ADAM_B1) * g
    v2 = ADAM_B2 * v + (1.0 - ADAM_B2) * (g * g)
    m_hat = m2 / (1.0 - ADAM_B1 ** ADAM_STEP)
    v_hat = v2 / (1.0 - ADAM_B2 ** ADAM_STEP)
    delta = -ADAM_LR * (m_hat / (jnp.sqrt(v_hat) + ADAM_EPS) + ADAM_WD * w)
    return delta, m2, v2


def _update_tile(rows):
    return 256 if rows % 256 == 0 else rows


def _rs_final_sum(name, own, got):
    def body(o_ref, r_ref, g_out):
        g = o_ref[...]
        for k in range(3):
            g = g + r_ref[k].astype(F32)
        g_out[...] = g

    return pl.pallas_call(body, name=name, out_shape=jax.ShapeDtypeStruct(own.shape, F32))(own, got)


def _adamw_sharded(name, w, m, v, own, got):
    _, rows, cols = w.shape
    tr = _update_tile(rows)

    def body(o_ref, r_ref, w_ref, m_ref, v_ref, g_out, d_out, m_out, v_out):
        g = o_ref[:, 0:cols]
        for k in range(3):
            g = g + r_ref[k, :, 0:cols].astype(F32)
        d, m2, v2 = _adamw_math(w_ref[0], g, m_ref[0], v_ref[0])
        g_out[0] = g
        d_out[0] = d
        m_out[0] = m2
        v_out[0] = v2

    mine = pl.BlockSpec((1, tr, cols), lambda i: (0, i, 0))
    shp = jax.ShapeDtypeStruct(w.shape, F32)
    wide = own.shape[1]
    return pl.pallas_call(
        body, name=name, grid=(rows // tr,), out_shape=(shp,) * 4,
        in_specs=[pl.BlockSpec((tr, wide), lambda i: (i, 0)), pl.BlockSpec((3, tr, wide), lambda i: (0, i, 0)),
                  mine, mine, mine],
        out_specs=[mine] * 4,
        compiler_params=_cparams("parallel"),
    )(own, got, w, m, v)


def _adamw_given(name, g, w, m, v):
    _, rows, cols = w.shape
    tr = _update_tile(rows)

    def body(g_ref, w_ref, m_ref, v_ref, g_out, d_out, m_out, v_out):
        g = g_ref[...]
        d, m2, v2 = _adamw_math(w_ref[0], g, m_ref[0], v_ref[0])
        g_out[0] = g
        d_out[0] = d
        m_out[0] = m2
        v_out[0] = v2

    own = pl.BlockSpec((1, tr, cols), lambda i: (0, i, 0))
    shp = jax.ShapeDtypeStruct(w.shape, F32)
    return pl.pallas_call(
        body, name=name, grid=(rows // tr,), out_shape=(shp,) * 4,
        in_specs=[pl.BlockSpec((tr, cols), lambda i: (i, 0)), own, own, own], out_specs=[own] * 4,
        compiler_params=_cparams("parallel"),
    )(g, w, m, v)


def _adamw_small(total, ws, ms, vs):
    n = len(ws)

    def body(*refs):
        t_ref = refs[0]
        w_refs, m_refs, v_refs = refs[1:1 + n], refs[1 + n:1 + 2 * n], refs[1 + 2 * n:1 + 3 * n]
        outs = refs[1 + 3 * n:]
        outs[0][...] = t_ref[LOSS_ROW:LOSS_ROW + 1, 0:1]
        for r in range(n):
            g = t_ref[r:r + 1, 0:w_refs[r].shape[1]]
            d, m2, v2 = _adamw_math(w_refs[r][...], g, m_refs[r][...], v_refs[r][...])
            for k, val in enumerate((g, d, m2, v2)):
                outs[1 + 4 * r + k][...] = val

    vmem = pl.BlockSpec(memory_space=pltpu.VMEM)
    out_shape = [jax.ShapeDtypeStruct((1, 1), F32)]
    for w in ws:
        out_shape += [jax.ShapeDtypeStruct(w.shape, F32)] * 4
    return pl.pallas_call(
        body, name="adamw_small", out_shape=out_shape,
        in_specs=[vmem] * (1 + 3 * n), out_specs=[vmem] * len(out_shape),
    )(total, *ws, *ms, *vs)


def _rope_tables(pos_col):
    t = pos_col.shape[0]
    inv = (np.float32(ROPE_THETA) ** (-np.arange(0, ROPE, 2, dtype=np.float32) / np.float32(ROPE))).astype(np.float32)
    freq = np.zeros((1, LANES), np.float32)
    freq[0, NOPE:NOPE + ROPE // 2] = inv
    freq[0, NOPE + ROPE // 2:NOPE + ROPE] = inv
    tm = _row_tile(t)

    def body(p_ref, f_ref, c_ref, s_ref):
        ang = p_ref[...].astype(F32) * f_ref[...]
        c_ref[...] = jnp.cos(ang)
        s_ref[...] = jnp.sin(ang)

    shp = jax.ShapeDtypeStruct((t, LANES), F32)
    return pl.pallas_call(
        body, name="rope_tables", grid=(t // tm,), out_shape=(shp, shp),
        in_specs=[pl.BlockSpec((tm, 1), lambda i: (i, 0)), pl.BlockSpec((1, LANES), lambda i: (0, 0))],
        out_specs=(pl.BlockSpec((tm, LANES), lambda i: (i, 0)),) * 2,
        compiler_params=_cparams("parallel"),
    )(pos_col, jnp.asarray(freq))


def _in_proj(x, g, w_qkv, w_rest, comm=None):
    t = x.shape[0]
    tm = _row_tile(t)

    def main(ins, outs, scr):
        x_ref, g_ref, wq_ref, wr_ref = ins
        h_ref, fq_ref, fk_ref, fv_ref, r_ref = outs
        h, _ = _rms(x_ref[...], g_ref[...])
        hb = h.astype(BF16)
        h_ref[...] = hb
        for n, (ref, factor) in enumerate(((fq_ref, FOX_Q_FACTOR), (fk_ref, None), (fv_ref, None))):
            part = _dot_nt(hb, wq_ref[n * FOX_WIDTH:(n + 1) * FOX_WIDTH, :])
            ref[...] = (part if factor is None else part * factor).astype(BF16)
        r_ref[...] = _dot_nt(hb, wr_ref[...])

    row = lambda n: pl.BlockSpec((tm, n), lambda i: (i, 0))
    full = lambda a: pl.BlockSpec(a.shape, lambda i: (0,) * a.ndim)
    return _hosted_call(
        "in_proj", main, (t // tm,), [x, g, w_qkv, w_rest], [row(D_MODEL), full(g), full(w_qkv), full(w_rest)],
        (jax.ShapeDtypeStruct((t, D_MODEL), BF16),) + (jax.ShapeDtypeStruct((t, FOX_WIDTH), BF16),) * 3
        + (jax.ShapeDtypeStruct((t, REST_COLS), F32),),
        (row(D_MODEL), row(FOX_WIDTH), row(FOX_WIDTH), row(FOX_WIDTH), row(REST_COLS)), [], comm)


def _log_sigmoid(z):
    return jnp.minimum(z, 0.0) - jnp.log(1.0 + jnp.exp(-jnp.abs(z)))


def _split3(v):
    hi = v.astype(BF16)
    r1 = v - hi.astype(F32)
    mid = r1.astype(BF16)
    lo = (r1 - mid.astype(F32)).astype(BF16)
    return hi, mid, lo


def _scan_tile(t):
    return 512 if t >= 2048 else (256 if t >= 256 else t)


def _forget_cumsum(rest, b128):
    t = rest.shape[0]
    tb = _scan_tile(t)

    def body(r_ref, b_ref, row_ref, rep_ref, f_sc, carry):
        @pl.when(pl.program_id(0) == 0)
        def _():
            carry[...] = jnp.zeros_like(carry)
        lf = _log_sigmoid(r_ref[...] + b_ref[...])
        tri = (lax.broadcasted_iota(jnp.int32, (tb, tb), 0) >= lax.broadcasted_iota(jnp.int32, (tb, tb), 1)).astype(BF16)
        hi, mid, lo = _split3(lf)
        f_sc[...] = (_dot(tri, hi) + _dot(tri, mid)) + _dot(tri, lo) + carry[...]
        carry[...] = f_sc[tb - 1:tb, :]
        f2 = f_sc[...] * LOG2E
        row_ref[...] = jnp.transpose(f2)[0:HEADS, :]
        lane = _lane()
        for h in range(HEADS):
            col = jnp.sum(jnp.where(lane == h, f2, 0.0), axis=1, keepdims=True)
            rep_ref[h] = jnp.broadcast_to(col, (tb, LANES))

    return pl.pallas_call(
        body, name="forget_cumsum", grid=(t // tb,),
        out_shape=(jax.ShapeDtypeStruct((HEADS, t), F32), jax.ShapeDtypeStruct((HEADS, t, LANES), F32)),
        in_specs=[pl.BlockSpec((tb, LANES), lambda i: (i, 0)), pl.BlockSpec((1, LANES), lambda i: (0, 0))],
        out_specs=(pl.BlockSpec((HEADS, tb), lambda i: (0, i)), pl.BlockSpec((HEADS, tb, LANES), lambda i: (0, i, 0))),
        scratch_shapes=[pltpu.VMEM((tb, LANES), F32), pltpu.VMEM((1, LANES), F32)],
        compiler_params=_cparams("arbitrary"),
    )(rest, b128)


def _forget_bwd(rest, b128, d_fq, d_fk):
    t = rest.shape[0]
    tb = _scan_tile(t)
    nb = t // tb

    def body(r_ref, b_ref, dfq_ref, dfk_ref, dz_ref, db_ref, carry):
        @pl.when(pl.program_id(0) == 0)
        def _():
            carry[...] = jnp.zeros_like(carry)
            db_ref[...] = jnp.zeros_like(db_ref)
        tri = (lax.broadcasted_iota(jnp.int32, (tb, tb), 0) <= lax.broadcasted_iota(jnp.int32, (tb, tb), 1)).astype(BF16)
        lane = _lane()
        df = jnp.zeros((tb, LANES), F32)
        for h in range(HEADS):
            df = df + jnp.where(lane == h, dfq_ref[h] + dfk_ref[h], 0.0)
        hi, mid, lo = _split3(df)
        dlf = (_dot(tri, hi) + _dot(tri, mid)) + _dot(tri, lo) + carry[...]
        z = r_ref[...] + b_ref[...]
        dz = dlf / (1.0 + jnp.exp(z))
        dz_ref[...] = dz
        db_ref[...] += jnp.sum(dz, axis=0, keepdims=True)
        carry[...] = carry[...] + jnp.sum(df, axis=0, keepdims=True)

    rev = lambda i: (nb - 1 - i, 0)
    rev3 = pl.BlockSpec((HEADS, tb, LANES), lambda i: (0, nb - 1 - i, 0))
    return pl.pallas_call(
        body, name="forget_bwd", grid=(nb,),
        out_shape=(jax.ShapeDtypeStruct((t, LANES), F32), jax.ShapeDtypeStruct((1, LANES), F32)),
        in_specs=[pl.BlockSpec((tb, LANES), rev), pl.BlockSpec((1, LANES), lambda i: (0, 0)), rev3, rev3],
        out_specs=(pl.BlockSpec((tb, LANES), rev), pl.BlockSpec((1, LANES), lambda i: (0, 0))),
        scratch_shapes=[pltpu.VMEM((1, LANES), F32)],
        compiler_params=_cparams("arbitrary"),
    )(rest, b128, d_fq, d_fk)


def _mla_prep(rest, gq, gkv, wq, wkv, cos, sin):
    t = rest.shape[0]
    tm = _row_tile(t)

    def body(r_ref, gq_ref, gkv_ref, wq_ref, wkv_ref, c_ref, s_ref, q_ref, k_ref, kv_ref, cq_ref, ckv_ref):
        cos_, sin_ = c_ref[...], s_ref[...]
        cq, _ = _rms(r_ref[:, REST_CQ:REST_CKV], gq_ref[...])
        ckv, _ = _rms(r_ref[:, REST_CKV:REST_KR], gkv_ref[...])
        cqb, ckvb = cq.astype(BF16), ckv.astype(BF16)
        cq_ref[...] = cqb
        ckv_ref[...] = ckvb
        k_rope = _rope(r_ref[:, REST_KR:REST_COLS], cos_, sin_)
        lo = _lane() < NOPE
        for h in range(HEADS):
            q_ref[h] = (_rope(_dot(cqb, wq_ref[h]), cos_, sin_) * MLA_Q_FACTOR).astype(BF16)
            kv = _dot(ckvb, wkv_ref[h])
            kv_ref[h] = kv.astype(BF16)
            k_ref[h] = (jnp.where(lo, kv, 0.0) + k_rope).astype(BF16)

    row = lambda n: pl.BlockSpec((tm, n), lambda i: (i, 0))
    full = lambda a: pl.BlockSpec(a.shape, lambda i: (0,) * a.ndim)
    heads = pl.BlockSpec((HEADS, tm, LANES), lambda i: (0, i, 0))
    hshape = jax.ShapeDtypeStruct((HEADS, t, LANES), BF16)
    return pl.pallas_call(
        body, name="mla_prep", grid=(t // tm,),
        out_shape=(hshape, hshape, hshape, jax.ShapeDtypeStruct((t, Q_RANK), BF16), jax.ShapeDtypeStruct((t, KV_RANK), BF16)),
        in_specs=[row(REST_COLS), full(gq), full(gkv), full(wq), full(wkv), row(LANES), row(LANES)],
        out_specs=(heads, heads, heads, row(Q_RANK), row(KV_RANK)),
        compiler_params=_cparams("parallel"),
    )(rest, gq, gkv, wq, wkv, cos, sin)


def _tile_lanes(x, n):
    return jnp.tile(x, (1, n)) if n > 1 else x


class _Comm(NamedTuple):
    inputs: tuple
    out_shape: tuple
    aliases: dict
    scratch: tuple
    start: Callable
    finish: Callable


def _hosted_call(name, main, grid, args, in_specs, out_shape, out_specs, scratch, comm):
    n_in, n_out, n_scr = len(args), len(out_shape), len(scratch)
    c_in = list(comm.inputs) if comm else []
    c_out = list(comm.out_shape) if comm else []

    def at_step(which):
        hit = pl.program_id(0) == which[0]
        for axis in range(1, len(grid)):
            hit = jnp.logical_and(hit, pl.program_id(axis) == which[axis])
        return hit

    def body(*refs):
        bounds = [0, n_in, len(c_in), n_out, len(c_out), n_scr]
        starts = [sum(bounds[:k + 1]) for k in range(len(bounds))]
        ins, cins, outs, couts, scr = [refs[a:b] for a, b in zip(starts[:-1], starts[1:])]
        sems = refs[starts[-1]:]
        if comm:
            @pl.when(at_step([0] * len(grid)))
            def _():
                comm.start(cins, couts, sems)
        main(ins, outs, scr)
        if comm:
            @pl.when(at_step([n - 1 for n in grid]))
            def _():
                comm.finish(cins, couts, sems)

    res = pl.pallas_call(
        body, name=name, grid=grid,
        out_shape=list(out_shape) + c_out,
        in_specs=list(in_specs) + _hbm_specs(len(c_in)),
        out_specs=list(out_specs) + _hbm_specs(len(c_out)),
        scratch_shapes=list(scratch) + (list(comm.scratch) if comm else []),
        input_output_aliases={n_in + i: n_out + o for i, o in comm.aliases.items()} if comm else {},
        compiler_params=_cparams(*(["arbitrary"] * len(grid))),
    )(*args, *c_in)
    return res[:n_out], res[n_out:]


def _stat_rows(x):
    return jnp.transpose(x)[0:8, :]


FWD_HEADS = 4


def _attn_fwd(fox, q, k, v, f2_rows=None, comm=None):
    t = q.shape[0] if fox else q.shape[1]
    tq = _row_tile(t)
    nq = t // tq
    nh = FWD_HEADS
    wide = (nh // 2) * LANES

    def main(ins, outs, scr):
        q_ref, k_ref, v_ref = ins[:3]
        fr_ref = ins[3] if fox else None
        o_ref, lset_ref = outs
        m_sc, acc_sc = scr
        i = pl.program_id(1)
        lo = _lane() < HEAD_DIM
        hi = jnp.logical_not(lo)
        zero, one = jnp.zeros((), BF16), jnp.ones((), BF16)
        lanes_of = lambda h: slice((h // 2) * LANES, (h // 2 + 1) * LANES)
        if fox:
            qs = [jnp.where(lo if h % 2 == 0 else hi, q_ref[:, lanes_of(h)], zero) for h in range(nh)]
            sum_lanes = [hi if h % 2 == 0 else lo for h in range(nh)]
        else:
            qs = [q_ref[h] for h in range(nh)]
            sum_lanes = [lo] * nh
        m_sc[...] = jnp.full_like(m_sc, -jnp.inf)
        acc_sc[...] = jnp.zeros_like(acc_sc)

        def block(j, r0, nr, c0, nc, seen_from):
            rows = slice(r0, r0 + nr)
            sl = pl.ds(pl.multiple_of(j * tq + c0, math.gcd(tq, c0) if c0 else tq), nc)
            if seen_from is not None:
                seen = (lax.broadcasted_iota(jnp.int32, (nr, nc), 1)
                        <= lax.broadcasted_iota(jnp.int32, (nr, nc), 0) + seen_from)
            for h in range(nh):
                kj, vj = (k_ref[sl, lanes_of(h)], v_ref[sl, lanes_of(h)]) if fox else (k_ref[h, sl, :], v_ref[h, sl, :])
                s = _dot_nt(qs[h][rows], kj)
                if fox:
                    s = s - fr_ref[h, j, :, c0:c0 + nc]
                if seen_from is not None:
                    s = jnp.where(seen, s, -jnp.inf)
                m_prev = m_sc[h, rows]
                m_new = jnp.maximum(m_prev, jnp.max(s, axis=1, keepdims=True))
                p = jnp.exp2((s - _tile_lanes(m_new, nc // LANES)).astype(BF16))
                vj = jnp.where(sum_lanes[h], one, vj)
                acc_sc[h, rows] = jnp.exp2(m_prev - m_new) * acc_sc[h, rows] + _dot(p, vj)
                m_sc[h, rows] = m_new

        def loop_body(j, carry):
            block(j, 0, tq, 0, tq, None)
            return carry

        lax.fori_loop(0, i, loop_body, 0)
        half = tq // 2
        if half % LANES == 0:
            block(i, 0, half, 0, half, 0)
            block(i, half, half, 0, tq, half)
        else:
            block(i, 0, tq, 0, tq, 0)
        res = []
        for h in range(nh):
            acc = acc_sc[h]
            swapped = pltpu.roll(acc, HEAD_DIM, 1)
            res.append(acc / swapped)
            lse2 = m_sc[h] + jnp.log(jnp.where(sum_lanes[h], acc, swapped)) * LOG2E
            lset_ref[h, 0] = _stat_rows(lse2)
        for pr in range(nh // 2):
            even = res[2 * pr] if fox else pltpu.roll(res[2 * pr], HEAD_DIM, 1)
            o_ref[:, pr * LANES:(pr + 1) * LANES] = jnp.where(lo, even, res[2 * pr + 1])

    if fox:
        in_specs = [pl.BlockSpec((tq, wide), lambda g, i: (i, g))] + [pl.BlockSpec((t, wide), lambda g, i: (0, g))] * 2
        in_specs += [pl.BlockSpec((nh, nq, 1, tq), lambda g, i: (g, 0, 0, 0))]
        args = [q, k, v, f2_rows]
    else:
        in_specs = [pl.BlockSpec((nh, tq, LANES), lambda g, i: (g, i, 0))] + [pl.BlockSpec((nh, t, LANES), lambda g, i: (g, 0, 0))] * 2
        args = [q, k, v]
    return _hosted_call(
        "fox_attn_fwd" if fox else "mla_attn_fwd", main, (HEADS // nh, nq), args, in_specs,
        (jax.ShapeDtypeStruct((t, 4 * LANES), F32), jax.ShapeDtypeStruct((HEADS, nq, 8, tq), F32)),
        (pl.BlockSpec((tq, wide), lambda g, i: (i, g)), pl.BlockSpec((nh, 1, 8, tq), lambda g, i: (g, i, 0, 0))),
        [pltpu.VMEM((nh, tq, LANES), F32), pltpu.VMEM((nh, tq, LANES), F32)], comm)


def _head_do(fox, hh, do2, lo):
    if fox:
        return jnp.where(lo if hh == 0 else jnp.logical_not(lo), do2, 0.0)
    return jnp.where(lo, 0.0, pltpu.roll(do2, HEAD_DIM, 1) if hh == 0 else do2)


def _attn_bwd(fox, q, k, v, do, lse_rows, delta_rows, f2_rep=None, comm=None):
    t = q.shape[0] if fox else q.shape[1]
    tq = _row_tile(t)
    nq = t // tq
    scale = FOX_SCALE if fox else MLA_SCALE

    def main(ins, outs, scr):
        if fox:
            q_ref, k_ref, v_ref, f_ref, do_ref, lse_ref, dl_ref = ins
            dq_ref, dk_ref, dv_ref, dfq_ref, dfk_ref = outs
        else:
            q_ref, k_ref, v_ref, do_ref, lse_ref, dl_ref = ins
            dq_ref, dkv_ref, dkr_ref = outs
        dq_sc, dk_sc, dv_sc = scr
        j = pl.program_id(1)
        lane = _lane()
        lo = lane < HEAD_DIM
        hi = jnp.logical_not(lo)
        zero, one = jnp.zeros((), BF16), jnp.ones((), BF16)

        @pl.when(j == 0)
        def _():
            dq_sc[...] = jnp.zeros_like(dq_sc)

        dk_sc[...] = jnp.zeros_like(dk_sc)
        dv_sc[...] = jnp.zeros_like(dv_sc)

        def block(i, r0, nr, c0, nc, masked):
            rows, cols = slice(r0, r0 + nr), slice(c0, c0 + nc)
            sl = pl.ds(pl.multiple_of(i * tq + c0, math.gcd(tq, c0) if c0 else tq), nc)
            do_i = do_ref[sl, :]
            if masked:
                seen = lax.broadcasted_iota(jnp.int32, (nr, nc), 1) >= lax.broadcasted_iota(jnp.int32, (nr, nc), 0)
            for hh in range(2):
                kj = k_ref[rows, :] if fox else k_ref[hh, rows, :]
                vj = v_ref[rows, :] if fox else v_ref[hh, rows, :]
                qi = jnp.where(lo if hh == 0 else hi, q_ref[sl, :], zero) if fox else q_ref[hh, sl, :]
                dob = _head_do(fox, hh, do_i, lo).astype(BF16)
                st = _dot_nt(kj, qi)
                if fox:
                    st = st - _tile_lanes(f_ref[hh, rows, :], nc // LANES)
                if masked:
                    st = jnp.where(seen, st, -jnp.inf)
                pt = jnp.exp2(st - lse_ref[hh, i, 0:1, cols])
                dpt = _dot_nt(vj, dob)
                dst = (pt * (dpt - dl_ref[hh, i, 0:1, cols])).astype(BF16)
                dv_sc[hh, rows] += _dot(pt.astype(BF16), dob)
                if fox:
                    other = hi if hh == 0 else lo
                    qi = jnp.where(other, one, qi)
                    kj = jnp.where(other, one, kj)
                dk_sc[hh, rows] += _dot(dst, qi)
                dq_sc[hh, sl, :] += _dot_tn(dst, kj)

        def loop_body(i, carry):
            block(i, 0, tq, 0, tq, False)
            return carry

        half = tq // 2
        if half % LANES == 0:
            block(j, 0, half, 0, tq, True)
            block(j, half, half, half, half, True)
        else:
            block(j, 0, tq, 0, tq, True)
        lax.fori_loop(j + 1, nq, loop_body, 0)
        if fox:
            dk_ref[...] = (jnp.where(lo, dk_sc[0], dk_sc[1]) * LN2).astype(BF16)
            dv_ref[...] = (dv_sc[0] + dv_sc[1]).astype(BF16)
            for hh in range(2):
                dk = dk_sc[hh]
                dfk_ref[hh] = -jnp.where(hi if hh == 0 else lo, dk, pltpu.roll(dk, HEAD_DIM, 1))
        else:
            rope_lanes = jnp.logical_and(lane >= NOPE, lane < NOPE + ROPE)
            dkr = jnp.zeros((tq, LANES), F32)
            for hh in range(2):
                dk = dk_sc[hh] * LN2
                dkv_ref[hh] = jnp.where(lo, dk, dv_sc[hh])
                dkr = dkr + jnp.where(rope_lanes, dk, 0.0)
            dkr_ref[0] = dkr

        @pl.when(j == nq - 1)
        def _():
            for i in range(nq):
                rows = slice(i * tq, (i + 1) * tq)
                if fox:
                    dq_ref[rows, :] = (jnp.where(lo, dq_sc[0, rows, :], dq_sc[1, rows, :]) * scale).astype(BF16)
                    for hh in range(2):
                        acc = dq_sc[hh, rows, :]
                        dfq_ref[hh, rows, :] = jnp.where(hi if hh == 0 else lo, acc, pltpu.roll(acc, HEAD_DIM, 1))
                else:
                    for hh in range(2):
                        dq_ref[hh, rows, :] = dq_sc[hh, rows, :] * scale

    stat = pl.BlockSpec((2, tq, LANES), lambda p, j: (p, j, 0))
    stat_all = pl.BlockSpec((2, t, LANES), lambda p, j: (p, 0, 0))
    rows4 = pl.BlockSpec((2, nq, 8, tq), lambda p, j: (p, 0, 0, 0))
    pair = pl.BlockSpec((tq, LANES), lambda p, j: (j, p))
    pair_all = pl.BlockSpec((t, LANES), lambda p, j: (0, p))
    if fox:
        in_specs = [pair_all, pair, pair, stat]
        args = [q, k, v, f2_rep]
    else:
        in_specs = [stat_all, stat, stat]
        args = [q, k, v]
    in_specs += [pair_all, rows4, rows4]
    args += [do, lse_rows, delta_rows]
    heads_f32 = jax.ShapeDtypeStruct((HEADS, t, LANES), F32)
    if fox:
        wide = jax.ShapeDtypeStruct((t, 4 * LANES), BF16)
        out_shape = (wide, wide, wide, heads_f32, heads_f32)
        out_specs = (pair_all, pair, pair, stat_all, stat)
    else:
        out_shape = (heads_f32, heads_f32, jax.ShapeDtypeStruct((HEADS // 2, t, LANES), F32))
        out_specs = (stat_all, stat, pl.BlockSpec((1, tq, LANES), lambda p, j: (p, j, 0)))
    acc = pltpu.VMEM((2, tq, LANES), F32)
    return _hosted_call("fox_attn_bwd" if fox else "mla_attn_bwd", main, (HEADS // 2, nq), args, in_specs,
                        out_shape, out_specs, [pltpu.VMEM((2, t, LANES), F32), acc, acc], comm)


def _mlp_tile(t):
    return 256 if t >= 2048 else 128


def _resident(a):
    return pl.BlockSpec(a.shape, lambda i: (0,) * a.ndim, pipeline_mode=pl.Buffered(1))


FF_CHUNK = 512


def _out_mlp_fwd(x0, fox_o, mla_o, g_fox, g_mla, w_o, g_mlp, w_up, w_down, g_fin, target):
    t = x0.shape[0]
    tm = _mlp_tile(t)

    def body(x0_ref, f_ref, m_ref, gfo_ref, gml_ref, wo_ref, g_ref, wu_ref, wd_ref, gf_ref, t_ref,
             x1_ref, mix_ref, u_ref, h_ref, dx_ref, dxb_ref, loss_ref, dg_ref, a_sc):
        @pl.when(pl.program_id(0) == 0)
        def _():
            loss_ref[...] = jnp.zeros_like(loss_ref)
            dg_ref[...] = jnp.zeros_like(dg_ref)

        nf, _ = _rms(f_ref[...], gfo_ref[...])
        nm, _ = _rms(m_ref[...], gml_ref[...])
        nfb, nmb = nf.astype(BF16), nm.astype(BF16)
        mix_ref[:, :FOX_WIDTH] = nfb
        mix_ref[:, FOX_WIDTH:] = nmb
        x = x0_ref[...] + _dot(nfb, wo_ref[:FOX_WIDTH, :]) + _dot(nmb, wo_ref[FOX_WIDTH:, :])
        x1_ref[...] = x
        h, _ = _rms(x, g_ref[...])
        hb = h.astype(BF16)
        h_ref[...] = hb
        for f in range(D_FF // FF_CHUNK):
            sl = slice(f * FF_CHUNK, (f + 1) * FF_CHUNK)
            u = _dot(hb, wu_ref[:, sl])
            u_ref[:, sl] = u
            r = jnp.maximum(u, 0.0)
            a_sc[:, sl] = (r * r).astype(BF16)
        x2 = x + _dot(a_sc[...], wd_ref[...])
        y, r2 = _rms(x2, gf_ref[...])
        err = y - t_ref[...]
        loss_ref[...] += 0.5 * jnp.sum(jnp.mean(err * err, axis=-1, keepdims=True))
        dx, dg = _rms_bwd(x2, gf_ref[...], r2, err * (1.0 / D_MODEL))
        dx_ref[...] = dx
        dxb_ref[...] = dx.astype(BF16)
        dg_ref[...] += dg

    row = lambda n: pl.BlockSpec((tm, n), lambda i: (i, 0))
    vec = pl.BlockSpec((1, D_MODEL), lambda i: (0, 0))
    half = pl.BlockSpec((1, FOX_WIDTH), lambda i: (0, 0))
    wide_f32 = jax.ShapeDtypeStruct((t, D_MODEL), F32)
    wide_bf16 = jax.ShapeDtypeStruct((t, D_MODEL), BF16)
    return pl.pallas_call(
        body, name="out_mlp_fwd", grid=(t // tm,),
        out_shape=(wide_f32, wide_bf16, jax.ShapeDtypeStruct((t, D_FF), F32), wide_bf16, wide_f32, wide_bf16,
                   jax.ShapeDtypeStruct((8, LANES), F32), jax.ShapeDtypeStruct((1, D_MODEL), F32)),
        in_specs=[row(D_MODEL), row(FOX_WIDTH), row(MLA_WIDTH), half, half, _resident(w_o), vec, _resident(w_up),
                  _resident(w_down), vec, row(D_MODEL)],
        out_specs=(row(D_MODEL), row(D_MODEL), row(D_FF), row(D_MODEL), row(D_MODEL), row(D_MODEL),
                   pl.BlockSpec((8, LANES), lambda i: (0, 0)), vec),
        scratch_shapes=[pltpu.VMEM((tm, D_FF), BF16)],
        compiler_params=_cparams("arbitrary"),
    )(x0, fox_o, mla_o, g_fox, g_mla, w_o, g_mlp, w_up, w_down, g_fin, target)


def _mlp_bwd(dx2, u, x1, g_mlp, w_up, w_down):
    t = x1.shape[0]
    tm = _mlp_tile(t)

    def body(dx_ref, u_ref, x_ref, g_ref, wu_ref, wd_ref, du_ref, a_ref, dx1_ref, dx1b_ref, dg_ref):
        @pl.when(pl.program_id(0) == 0)
        def _():
            dg_ref[...] = jnp.zeros_like(dg_ref)

        dx2 = dx_ref[...]
        dxb = dx2.astype(BF16)
        for f in range(D_FF // FF_CHUNK):
            sl = slice(f * FF_CHUNK, (f + 1) * FF_CHUNK)
            r = jnp.maximum(u_ref[:, sl], 0.0)
            a_ref[:, sl] = (r * r).astype(BF16)
            da = _dot_nt(dxb, wd_ref[sl, :])
            du_ref[:, sl] = (da * (2.0 * r)).astype(BF16)
        dh = _dot_nt(du_ref[...], wu_ref[...])
        x = x_ref[...]
        _, r1 = _rms(x, g_ref[...])
        dx, dg = _rms_bwd(x, g_ref[...], r1, dh)
        dx1 = dx2 + dx
        dx1_ref[...] = dx1
        dx1b_ref[...] = dx1.astype(BF16)
        dg_ref[...] += dg

    row = lambda n: pl.BlockSpec((tm, n), lambda i: (i, 0))
    vec = pl.BlockSpec((1, D_MODEL), lambda i: (0, 0))
    return pl.pallas_call(
        body, name="mlp_bwd", grid=(t // tm,),
        out_shape=(jax.ShapeDtypeStruct((t, D_FF), BF16), jax.ShapeDtypeStruct((t, D_FF), BF16),
                   jax.ShapeDtypeStruct((t, D_MODEL), F32), jax.ShapeDtypeStruct((t, D_MODEL), BF16),
                   jax.ShapeDtypeStruct((1, D_MODEL), F32)),
        in_specs=[row(D_MODEL), row(D_FF), row(D_MODEL), vec, _resident(w_up), _resident(w_down)],
        out_specs=(row(D_FF), row(D_FF), row(D_MODEL), row(D_MODEL), vec),
        compiler_params=_cparams("arbitrary"),
    )(dx2, u, x1, g_mlp, w_up, w_down)


def _matmul_tn(name, a, b, blocks=None):
    t, m = a.shape
    n = b.shape[1]
    tk = t if a.dtype == BF16 and b.dtype == BF16 else min(t, 2048)
    steps = t // tk
    bm = m if m <= 1024 else 512
    bn = n if n <= 1024 else 512
    width = bn if blocks is None else n // blocks
    per = bn // width

    def body(a_ref, b_ref, o_ref, acc_sc):
        kk = pl.program_id(2)

        @pl.when(kk == 0)
        def _():
            acc_sc[...] = jnp.zeros_like(acc_sc)

        acc_sc[...] += _dot_tn(a_ref[...].astype(BF16), b_ref[...].astype(BF16))

        @pl.when(kk == steps - 1)
        def _():
            if blocks is None:
                o_ref[...] = acc_sc[...]
            else:
                for s in range(per):
                    o_ref[s] = acc_sc[:, s * width:(s + 1) * width]

    if blocks is None:
        o_spec = pl.BlockSpec((bm, bn), lambda i, j, kk: (i, j))
        o_shape = (m, n)
    else:
        o_spec = pl.BlockSpec((per, bm, width), lambda i, j, kk: (j, i, 0))
        o_shape = (blocks, m, width)
    return pl.pallas_call(
        body, name=name, grid=(m // bm, n // bn, steps),
        out_shape=jax.ShapeDtypeStruct(o_shape, F32),
        in_specs=[pl.BlockSpec((tk, bm), lambda i, j, kk: (kk, i)), pl.BlockSpec((tk, bn), lambda i, j, kk: (kk, j))],
        out_specs=o_spec,
        scratch_shapes=[pltpu.VMEM((bm, bn), F32)],
        compiler_params=_cparams("parallel", "parallel", "arbitrary"),
    )(a, b)


def _dw_in(dfq, dfk, dfv, drest, h1):
    t = h1.shape[0]
    tk = min(t, 1024)
    off_ff = 3 * FOX_WIDTH
    off_cq = off_ff + HEADS
    off_kr = IN_COLS - ROPE

    def body(dq_ref, dk_ref, dv_ref, dr_ref, h_ref, o_ref):
        h = h_ref[...]
        r = _dot_tn(dr_ref[...], h)
        parts = [(slice(n * FOX_WIDTH, (n + 1) * FOX_WIDTH), _dot_tn(ref[...], h)) for n, ref in enumerate((dq_ref, dk_ref, dv_ref))]
        parts += [(slice(off_ff, off_cq), r[0:HEADS]), (slice(off_cq, off_kr), r[REST_CQ:REST_KR]),
                  (slice(off_kr, IN_COLS), r[REST_KR + NOPE:REST_KR + NOPE + ROPE])]

        @pl.when(pl.program_id(0) == 0)
        def _():
            for rows, val in parts:
                o_ref[rows, :] = val

        @pl.when(pl.program_id(0) > 0)
        def _():
            for rows, val in parts:
                o_ref[rows, :] += val

    tok = lambda n: pl.BlockSpec((tk, n), lambda kk: (kk, 0))
    return pl.pallas_call(
        body, name="dw_in", grid=(t // tk,),
        out_shape=jax.ShapeDtypeStruct((IN_COLS, D_MODEL), F32),
        in_specs=[tok(FOX_WIDTH), tok(FOX_WIDTH), tok(FOX_WIDTH), tok(REST_COLS), tok(D_MODEL)],
        out_specs=pl.BlockSpec((IN_COLS, D_MODEL), lambda kk: (0, 0)),
        compiler_params=_cparams("arbitrary"),
    )(dfq, dfk, dfv, drest, h1)


def _attn_out_bwd(dx1, fox_o, mla_o, gf, gm, w_o):
    t = dx1.shape[0]
    tm = _row_tile(t)

    def body(dx_ref, f_ref, m_ref, gf_ref, gm_ref, w_ref, df_ref, dm_ref, dlf_ref, dlm_ref, dgf_ref, dgm_ref):
        @pl.when(pl.program_id(0) == 0)
        def _():
            dgf_ref[...] = jnp.zeros_like(dgf_ref)
            dgm_ref[...] = jnp.zeros_like(dgm_ref)
        dxb = dx_ref[...].astype(BF16)
        lane = lax.broadcasted_iota(jnp.int32, (8, LANES), 1)
        picks = [(lane < HEAD_DIM).astype(BF16), (lane >= HEAD_DIM).astype(BF16)]
        for o_ref, g_ref, lo_row, d_ref, dl_ref, dg_ref in ((f_ref, gf_ref, 0, df_ref, dlf_ref, dgf_ref),
                                                             (m_ref, gm_ref, FOX_WIDTH, dm_ref, dlm_ref, dgm_ref)):
            dn = _dot_nt(dxb, w_ref[lo_row:lo_row + FOX_WIDTH, :])
            o = o_ref[...]
            _, r = _rms(o, g_ref[...])
            d, dg = _rms_bwd(o, g_ref[...], r, dn)
            d_ref[...] = d
            dg_ref[...] += dg
            prod = d * o
            for h in range(HEADS):
                parts = _split3(prod[:, (h // 2) * LANES:(h // 2 + 1) * LANES])
                dl_ref[h, 0] = (_dot_nt(picks[h % 2], parts[0]) + _dot_nt(picks[h % 2], parts[1])) + _dot_nt(picks[h % 2], parts[2])

    row = lambda n: pl.BlockSpec((tm, n), lambda i: (i, 0))
    full = lambda a: pl.BlockSpec(a.shape, lambda i: (0,) * a.ndim)
    vec = pl.BlockSpec((1, FOX_WIDTH), lambda i: (0, 0))
    rows = pl.BlockSpec((HEADS, 1, 8, tm), lambda i: (0, i, 0, 0))
    o_shape = jax.ShapeDtypeStruct((t, FOX_WIDTH), F32)
    g_shape = jax.ShapeDtypeStruct((1, FOX_WIDTH), F32)
    r_shape = jax.ShapeDtypeStruct((HEADS, t // tm, 8, tm), F32)
    return pl.pallas_call(
        body, name="attn_out_bwd", grid=(t // tm,),
        out_shape=(o_shape, o_shape, r_shape, r_shape, g_shape, g_shape),
        in_specs=[row(D_MODEL), row(FOX_WIDTH), row(MLA_WIDTH), full(gf), full(gm), full(w_o)],
        out_specs=(row(FOX_WIDTH), row(MLA_WIDTH), rows, rows, vec, vec),
        compiler_params=_cparams("arbitrary"),
    )(dx1, fox_o, mla_o, gf, gm, w_o)


def _mla_prep_bwd(dq, dkv, dkr, dz, rest, gq, gkv, wq, wkv, cos, sin):
    t = rest.shape[0]
    tm = _row_tile(t)

    def body(dq_ref, dkv_ref, dkr_ref, dz_ref, r_ref, gq_ref, gkv_ref, wq_ref, wkv_ref, c_ref, s_ref,
             dr_ref, dqp_ref, dkvb_ref, dgq_ref, dgkv_ref):
        @pl.when(pl.program_id(0) == 0)
        def _():
            dgq_ref[...] = jnp.zeros_like(dgq_ref)
            dgkv_ref[...] = jnp.zeros_like(dgkv_ref)
        cos_, sin_ = c_ref[...], s_ref[...]
        dcq = jnp.zeros((tm, Q_RANK), F32)
        dckv = jnp.zeros((tm, KV_RANK), F32)
        for h in range(HEADS):
            dqp = _rope_bwd(dq_ref[h], cos_, sin_).astype(BF16)
            dqp_ref[:, h * LANES:(h + 1) * LANES] = dqp
            dcq = dcq + _dot_nt(dqp, wq_ref[h])
            dkvb = dkv_ref[h].astype(BF16)
            dkvb_ref[:, h * LANES:(h + 1) * LANES] = dkvb
            dckv = dckv + _dot_nt(dkvb, wkv_ref[h])
        dkrope = dkr_ref[0]
        for pr in range(1, HEADS // 2):
            dkrope = dkrope + dkr_ref[pr]
        cq = r_ref[:, REST_CQ:REST_CKV]
        _, rq = _rms(cq, gq_ref[...])
        d_cq, dgq = _rms_bwd(cq, gq_ref[...], rq, dcq)
        ckv = r_ref[:, REST_CKV:REST_KR]
        _, rkv = _rms(ckv, gkv_ref[...])
        d_ckv, dgkv = _rms_bwd(ckv, gkv_ref[...], rkv, dckv)
        dgq_ref[...] += dgq
        dgkv_ref[...] += dgkv
        dr_ref[:, 0:REST_CQ] = dz_ref[...].astype(BF16)
        dr_ref[:, REST_CQ:REST_CKV] = d_cq.astype(BF16)
        dr_ref[:, REST_CKV:REST_KR] = d_ckv.astype(BF16)
        dr_ref[:, REST_KR:REST_COLS] = _rope_bwd(dkrope, cos_, sin_).astype(BF16)

    row = lambda n: pl.BlockSpec((tm, n), lambda i: (i, 0))
    full = lambda a: pl.BlockSpec(a.shape, lambda i: (0,) * a.ndim)
    heads = pl.BlockSpec((HEADS, tm, LANES), lambda i: (0, i, 0))
    hshape = jax.ShapeDtypeStruct((t, HEADS * LANES), BF16)
    return pl.pallas_call(
        body, name="mla_prep_bwd", grid=(t // tm,),
        out_shape=(jax.ShapeDtypeStruct((t, REST_COLS), BF16), hshape, hshape,
                   jax.ShapeDtypeStruct((1, Q_RANK), F32), jax.ShapeDtypeStruct((1, KV_RANK), F32)),
        in_specs=[heads, heads, pl.BlockSpec((HEADS // 2, tm, LANES), lambda i: (0, i, 0)), row(LANES), row(REST_COLS),
                  full(gq), full(gkv), full(wq), full(wkv), row(LANES), row(LANES)],
        out_specs=(row(REST_COLS), row(HEADS * LANES), row(HEADS * LANES), pl.BlockSpec((1, Q_RANK), lambda i: (0, 0)),
                   pl.BlockSpec((1, KV_RANK), lambda i: (0, 0))),
        compiler_params=_cparams("arbitrary"),
    )(dq, dkv, dkr, dz, rest, gq, gkv, wq, wkv, cos, sin)


def _in_proj_bwd(x, g, dx1, dfq, dfk, dfv, drest, w_qkv, w_rest, comm=None):
    t = x.shape[0]
    tm = _row_tile(t)

    def main(ins, outs, scr):
        x_ref, g_ref, dx1_ref, dq_ref, dk_ref, dv_ref, dr_ref, wq_ref, wr_ref = ins
        dx_ref, dg_ref = outs

        @pl.when(pl.program_id(0) == 0)
        def _():
            dg_ref[...] = jnp.zeros_like(dg_ref)
        dh = _dot(dr_ref[...], wr_ref[...])
        for n, ref in enumerate((dq_ref, dk_ref, dv_ref)):
            dh = dh + _dot(ref[...], wq_ref[n * FOX_WIDTH:(n + 1) * FOX_WIDTH, :])
        xv = x_ref[...]
        _, r = _rms(xv, g_ref[...])
        dx, dg = _rms_bwd(xv, g_ref[...], r, dh)
        dx_ref[...] = dx1_ref[...] + dx
        dg_ref[...] += dg

    row = lambda n: pl.BlockSpec((tm, n), lambda i: (i, 0))
    full = lambda a: pl.BlockSpec(a.shape, lambda i: (0,) * a.ndim)
    vec = pl.BlockSpec((1, D_MODEL), lambda i: (0, 0))
    return _hosted_call(
        "in_proj_bwd", main, (t // tm,), [x, g, dx1, dfq, dfk, dfv, drest, w_qkv, w_rest],
        [row(D_MODEL), full(g), row(D_MODEL), row(FOX_WIDTH), row(FOX_WIDTH), row(FOX_WIDTH), row(REST_COLS),
         full(w_qkv), full(w_rest)],
        (jax.ShapeDtypeStruct((t, D_MODEL), F32), jax.ShapeDtypeStruct((1, D_MODEL), F32)), (row(D_MODEL), vec), [], comm)


def _pad_cols(a, n):
    return jnp.pad(a, ((0, 0),) * (a.ndim - 1) + ((0, n - a.shape[-1]),))


def kernel(x, positions, attn_norm_g, w_in, b_forget, q_norm_g, w_uq, kv_norm_g, w_ukv, fox_out_g, mla_out_g, w_o, mlp_norm_g, w_up, w_down, final_norm_g, loss_target, m_attn_norm_g, m_w_in, m_b_forget, m_q_norm_g, m_w_uq, m_kv_norm_g, m_w_ukv, m_fox_out_g, m_mla_out_g, m_w_o, m_mlp_norm_g, m_w_up, m_w_down, m_final_norm_g, v_attn_norm_g, v_w_in, v_b_forget, v_q_norm_g, v_w_uq, v_kv_norm_g, v_w_ukv, v_fox_out_g, v_mla_out_g, v_w_o, v_mlp_norm_g, v_w_up, v_w_down, v_final_norm_g):
    t = x.shape[1]
    tq = _row_tile(t)
    xs = x[0]
    target = loss_target[0]

    mid = [_pad_cols(w_uq[0], LANES).astype(BF16), w_ukv[0].astype(BF16)]
    late = [w_o[0].astype(BF16), w_up[0].astype(BF16), w_down[0].astype(BF16)]
    g_in, = _all_gather([jnp.transpose(w_in[0]).astype(BF16)])
    win = g_in.reshape(IN_COLS, D_MODEL)
    off_ff, off_cq, off_kr = 3 * FOX_WIDTH, 3 * FOX_WIDTH + HEADS, IN_COLS - ROPE
    zeros = lambda n: jnp.zeros((n, D_MODEL), BF16)
    w_qkv = win[:off_ff]
    w_rest = jnp.concatenate([
        win[off_ff:off_cq], zeros(REST_CQ - HEADS), win[off_cq:off_kr],
        zeros(NOPE), win[off_kr:], zeros(LANES - NOPE - ROPE)], axis=0)

    cos, sin = _rope_tables(positions.reshape(t, 1))
    (h1, fq, fk, fv, rest), (wq, wkv) = _in_proj(xs, attn_norm_g, w_qkv, w_rest, comm=_ag_to_all(mid))
    b128 = _pad_cols(b_forget, LANES)
    f2_rows, f2_rep = _forget_cumsum(rest, b128)
    f2_rows = f2_rows.reshape(HEADS, t // tq, 1, tq)
    (fox_o, fox_lse_rows), partly = _attn_fwd(True, fq, fk, fv, f2_rows, comm=_ag_direct(late))
    mq, mk, mkv, cqn, ckvn = _mla_prep(rest, q_norm_g, kv_norm_g, wq, wkv, cos, sin)
    (mla_o, mla_lse_rows), (g_o, g_up, g_down) = _attn_fwd(False, mq, mk, mkv, comm=_ag_forward(partly))
    wo = g_o.reshape(D_MODEL, D_MODEL)
    wup = jnp.transpose(g_up, (1, 0, 2)).reshape(D_MODEL, D_FF)
    wdown = g_down.reshape(D_FF, D_MODEL)
    x1, mixed, u, h2, dx2, dx2b, loss8, d_gfin = _out_mlp_fwd(
        xs, fox_o, mla_o, fox_out_g, mla_out_g, wo, mlp_norm_g, wup, wdown, final_norm_g.reshape(1, D_MODEL), target)

    du, act, dx1, dx1b, d_gmlp = _mlp_bwd(dx2, u, x1, mlp_norm_g, wup, wdown)
    dw_down = _matmul_tn("dw_down", act, dx2b)
    dw_up = _matmul_tn("dw_up", h2, du, blocks=N_DEV)
    dfox_o, dmla_o, fox_delta_rows, mla_delta_rows, d_gfox, d_gmla = _attn_out_bwd(dx1, fox_o, mla_o, fox_out_g, mla_out_g, wo)
    dw_o = _matmul_tn("dw_o", mixed, dx1b)

    place = jnp.stack([lax.axis_index("c"), 2 * lax.axis_index("x") + lax.axis_index("y")]).astype(jnp.int32)
    names = ("w_in", "w_uq", "w_ukv", "w_o", "w_up", "w_down")
    grads_b = [dw_o.reshape(N_DEV, -1, D_MODEL), dw_up, dw_down.reshape(N_DEV, -1, D_MODEL)]
    (dfq, dfk, dfv, d_fq, d_fk), got_b = _attn_bwd(True, fq, fk, fv, dfox_o, fox_lse_rows, fox_delta_rows,
                                                   f2_rep, comm=_rs_to_sibling(grads_b))
    sums_b = [_rs_sibling_sum("rs_sibling_sum_" + nm, g, l, place) for nm, g, l in zip(names[3:], grads_b, got_b)]
    dz, d_b = _forget_bwd(rest, b128, d_fq, d_fk)
    (dmq, dmkv, dmkr), others_b = _attn_bwd(False, mq, mk, mkv, dmla_o, mla_lse_rows, mla_delta_rows,
                                            comm=_rs_to_chips([s[1] for s in sums_b]))
    drest, dqp, dkvb, d_gq, d_gkv = _mla_prep_bwd(dmq, dmkv, dmkr, dz, rest, q_norm_g, kv_norm_g, wq, wkv, cos, sin)
    dw_uq = _matmul_tn("dw_uq", cqn, dqp, blocks=HEADS)
    dw_ukv = _matmul_tn("dw_ukv", ckvn, dkvb, blocks=HEADS)
    dw_in = _dw_in(dfq, dfk, dfv, drest, h1)

    grads_a = [dw_in.reshape(N_DEV, IN_SHARD, D_MODEL), dw_uq, dw_ukv]
    got_a = _comm_call("rs_sibling_exchange", _rs_to_sibling(grads_a))
    sums_a = [_rs_sibling_sum("rs_sibling_sum_" + nm, g, l, place) for nm, g, l in zip(names[:3], grads_a, got_a)]
    (grad_x, d_gattn), others_a = _in_proj_bwd(xs, attn_norm_g, dx1, dfq, dfk, dfv, drest, w_qkv, w_rest,
                                               comm=_rs_to_chips([s[1] for s in sums_a]))
    sums, others = sums_a + sums_b, list(others_a) + list(others_b)
    sharded = (w_in, w_uq, w_ukv, w_o, w_up, w_down)
    moments_m = (m_w_in, m_w_uq, m_w_ukv, m_w_o, m_w_up, m_w_down)
    moments_v = (v_w_in, v_w_uq, v_w_ukv, v_w_o, v_w_up, v_w_down)
    g_in_t = _rs_final_sum("rs_final_sum_w_in", sums[0][0], others[0])
    big = [_adamw_given("adamw_w_in", jnp.transpose(g_in_t), w_in, m_w_in, v_w_in)]
    for a in range(1, len(names)):
        big.append(_adamw_sharded("adamw_" + names[a], sharded[a], moments_m[a], moments_v[a], sums[a][0], others[a]))
    big_g, big_d, big_m, big_v = [[b[k] for b in big] for k in range(4)]

    as_row = lambda a: a.reshape(1, -1)
    small_w = (attn_norm_g, b_forget, q_norm_g, kv_norm_g, fox_out_g, mla_out_g, mlp_norm_g, final_norm_g)
    small_m = (m_attn_norm_g, m_b_forget, m_q_norm_g, m_kv_norm_g, m_fox_out_g, m_mla_out_g, m_mlp_norm_g, m_final_norm_g)
    small_v = (v_attn_norm_g, v_b_forget, v_q_norm_g, v_kv_norm_g, v_fox_out_g, v_mla_out_g, v_mlp_norm_g, v_final_norm_g)
    total = _small_all_reduce([d_gattn, d_b, d_gq, d_gkv, d_gfox, d_gmla, d_gmlp, d_gfin], loss8)
    small = _adamw_small(total, [as_row(a) for a in small_w], [as_row(a) for a in small_m], [as_row(a) for a in small_v])
    loss = small[0].reshape(())
    s_g, s_d, s_m, s_v = [[small[1 + 4 * r + k].reshape(small_w[r].shape) for r in range(len(small_w))] for k in range(4)]

    def ordered(small_, bigs):
        ga, bf, gq_, gkv_, gfo, gml, gmlp_, gfin_ = small_
        bin_, buq, bukv, bo, bup, bdown = bigs
        return [ga, bin_, bf, gq_, buq, gkv_, bukv, gfo, gml, bo, gmlp_, bup, bdown, gfin_]

    return (loss, grad_x[None], *ordered(s_g, big_g), *ordered(s_d, big_d), *ordered(s_m, big_m), *ordered(s_v, big_v))
```

```python
import math
from typing import Callable, NamedTuple

import numpy as np
import jax
import jax.numpy as jnp
from jax import lax
from jax.experimental import pallas as pl
from jax.experimental.pallas import tpu as pltpu

F32 = jnp.float32
BF16 = jnp.bfloat16
MESH = pl.DeviceIdType.MESH

D_MODEL = 1024
HEADS = 8
HEAD_DIM = 64
FOX_WIDTH = 512
MLA_WIDTH = 512
NOPE = 64
ROPE = 32
QK_DIM = 96
Q_RANK = 384
KV_RANK = 256
D_FF = 4096
IN_COLS = 2216
ROPE_THETA = 10000.0
EPS = 1e-6
FOX_SCALE = 1.0 / math.sqrt(HEAD_DIM)
MLA_SCALE = 1.0 / math.sqrt(QK_DIM)
ADAM_LR = 0.001
ADAM_B1 = 0.9
ADAM_B2 = 0.999
ADAM_EPS = 1e-08
ADAM_WD = 0.01
ADAM_STEP = 10

N_DEV = 8
LANES = 128
REST_COLS = 896
REST_CQ = LANES
REST_CKV = REST_CQ + Q_RANK
REST_KR = REST_CKV + KV_RANK
LOG2E = 1.4426950408889634
LN2 = 0.6931471805599453
FOX_Q_FACTOR = FOX_SCALE * LOG2E
MLA_Q_FACTOR = MLA_SCALE * LOG2E
VMEM_LIMIT = 56 * 1024 * 1024

IN_SHARD = IN_COLS // N_DEV
SMALL_SIZES = (1024, 8, 384, 256, 512, 512, 1024, 1024)
SMALL_ROWS = 16
LOSS_ROW = len(SMALL_SIZES)


def _cparams(*sem):
    return pltpu.CompilerParams(dimension_semantics=sem or None, vmem_limit_bytes=VMEM_LIMIT)


def _row_tile(t):
    return 512 if t >= 2048 else (256 if t >= 512 else 128)


def _dot(a, b):
    return jnp.dot(a, b, preferred_element_type=F32)


def _dot_nt(a, b):
    return lax.dot_general(a, b, (((1,), (1,)), ((), ())), preferred_element_type=F32)


def _dot_tn(a, b):
    return lax.dot_general(a, b, (((0,), (0,)), ((), ())), preferred_element_type=F32)


def _rms(x, g):
    r = lax.rsqrt(jnp.mean(x * x, axis=-1, keepdims=True) + EPS)
    return x * r * g, r


def _rms_bwd(x, g, r, dy):
    xh = x * r
    gdy = dy * g
    dx = r * (gdy - xh * jnp.mean(gdy * xh, axis=-1, keepdims=True))
    return dx, jnp.sum(dy * xh, axis=0, keepdims=True)


def _lane():
    return lax.broadcasted_iota(jnp.int32, (1, LANES), 1)


def _rot(x):
    lane = _lane()
    half = NOPE + ROPE // 2
    first = jnp.logical_and(lane >= NOPE, lane < half)
    second = jnp.logical_and(lane >= half, lane < NOPE + ROPE)
    return jnp.where(first, -pltpu.roll(x, LANES - ROPE // 2, 1), jnp.where(second, pltpu.roll(x, ROPE // 2, 1), 0.0))


def _rope(x, cos, sin):
    return x * cos + _rot(x) * sin


def _rope_bwd(dy, cos, sin):
    return dy * cos - _rot(dy * sin)


def _remote(src, dst, send_sem, recv_sem, to):
    return pltpu.make_async_remote_copy(src_ref=src, dst_ref=dst, send_sem=send_sem, recv_sem=recv_sem,
                                        device_id=to, device_id_type=MESH)


def _hbm_specs(n):
    return [pl.BlockSpec(memory_space=pl.ANY)] * n


def _all_gather(blocks):
    n = len(blocks)

    def body(*refs):
        x_refs, out_refs = refs[:n], refs[n:2 * n]
        send_sems, recv_sems, local_sems = refs[2 * n:]
        x, y, c = lax.axis_index("x"), lax.axis_index("y"), lax.axis_index("c")
        me, sibling = (x, y, c), (x, y, 1 - c)
        chips = [(1 - x, y), (x, 1 - y), (1 - x, 1 - y)]

        def slot(a, px, py, pc):
            return out_refs[a].at[4 * px + 2 * py + pc]

        def copy(a, k, blk, to, src=None):
            return _remote(slot(a, *blk) if src is None else src, slot(a, *blk),
                           send_sems.at[7 * a + k], recv_sems.at[7 * a + k], to)

        mine = [pltpu.make_async_copy(x_refs[a], slot(a, *me), local_sems.at[a]) for a in range(n)]
        first, passed = [], []
        for a in range(n):
            mine[a].start()
            first.append(copy(a, 0, me, sibling, src=x_refs[a]))
            first += [copy(a, 1 + j, me, (*chip, c), src=x_refs[a]) for j, chip in enumerate(chips)]
        for cp in first:
            cp.start()
        for a in range(n):
            for j, chip in enumerate(chips):
                copy(a, 1 + j, (*chip, c), me).wait_recv()
                passed.append(copy(a, 4 + j, (*chip, c), sibling))
                passed[-1].start()
        for a in range(n):
            copy(a, 0, sibling, me).wait_recv()
            for j, chip in enumerate(chips):
                copy(a, 4 + j, (*chip, 1 - c), me).wait_recv()
        for cp in first + passed:
            cp.wait_send()
        for cp in mine:
            cp.wait()

    return pl.pallas_call(
        body, name="all_gather_weights",
        out_shape=[jax.ShapeDtypeStruct((N_DEV,) + b.shape, b.dtype) for b in blocks],
        in_specs=_hbm_specs(n), out_specs=_hbm_specs(n),
        scratch_shapes=[pltpu.SemaphoreType.DMA((7 * n,)), pltpu.SemaphoreType.DMA((7 * n,)), pltpu.SemaphoreType.DMA((n,))],
    )(*blocks)


def _symmetric_comm(inputs, out_shape, aliases, per_array, copies):
    def start(in_refs, out_refs, sems):
        for cp in copies(in_refs, out_refs, *sems):
            cp.start()

    def finish(in_refs, out_refs, sems):
        for cp in copies(in_refs, out_refs, *sems):
            cp.wait()

    n_sems = per_array * len(inputs)
    return _Comm(tuple(inputs), tuple(out_shape), aliases,
                 (pltpu.SemaphoreType.DMA((n_sems,)), pltpu.SemaphoreType.DMA((n_sems,))), start, finish)


def _ag_direct(shards):
    def copies(in_refs, out_refs, send_sems, recv_sems):
        x, y, c = lax.axis_index("x"), lax.axis_index("y"), lax.axis_index("c")
        peers = [(x, y, 1 - c), (1 - x, y, c), (x, 1 - y, c), (1 - x, 1 - y, c)]
        cps = []
        for a in range(len(shards)):
            mine = out_refs[a].at[4 * x + 2 * y + c]
            cps.append(pltpu.make_async_copy(in_refs[a], mine, send_sems.at[5 * a]))
            cps += [_remote(in_refs[a], mine, send_sems.at[5 * a + k], recv_sems.at[5 * a + k], peer)
                    for k, peer in enumerate(peers, start=1)]
        return cps

    return _symmetric_comm(shards, [jax.ShapeDtypeStruct((N_DEV,) + s.shape, s.dtype) for s in shards], {}, 5, copies)


def _ag_to_all(shards):
    def copies(in_refs, out_refs, send_sems, recv_sems):
        x, y, c = lax.axis_index("x"), lax.axis_index("y"), lax.axis_index("c")
        cps = []
        for a in range(len(shards)):
            mine = out_refs[a].at[4 * x + 2 * y + c]
            cps.append(pltpu.make_async_copy(in_refs[a], mine, send_sems.at[N_DEV * a]))
            for k in range(1, N_DEV):
                peer = (x ^ (k >> 2), y ^ ((k >> 1) & 1), c ^ (k & 1))
                cps.append(_remote(in_refs[a], mine, send_sems.at[N_DEV * a + k], recv_sems.at[N_DEV * a + k], peer))
        return cps

    return _symmetric_comm(shards, [jax.ShapeDtypeStruct((N_DEV,) + s.shape, s.dtype) for s in shards], {}, N_DEV, copies)


def _ag_forward(gathered):
    def copies(in_refs, out_refs, send_sems, recv_sems):
        x, y, c = lax.axis_index("x"), lax.axis_index("y"), lax.axis_index("c")
        chips = [(1 - x, y), (x, 1 - y), (1 - x, 1 - y)]
        return [_remote(out_refs[a].at[4 * cx + 2 * cy + c], out_refs[a].at[4 * cx + 2 * cy + c],
                        send_sems.at[3 * a + j], recv_sems.at[3 * a + j], (x, y, 1 - c))
                for a in range(len(gathered)) for j, (cx, cy) in enumerate(chips)]

    shapes = [jax.ShapeDtypeStruct(g.shape, g.dtype) for g in gathered]
    return _symmetric_comm(gathered, shapes, {a: a for a in range(len(gathered))}, 3, copies)


def _rs_to_sibling(grads):
    def copies(in_refs, out_refs, send_sems, recv_sems):
        x, y, c = lax.axis_index("x"), lax.axis_index("y"), lax.axis_index("c")
        return [_remote(in_refs[a].at[2 * q + 1 - c], out_refs[a].at[q], send_sems.at[4 * a + q], recv_sems.at[4 * a + q], (x, y, 1 - c))
                for a in range(len(grads)) for q in range(4)]

    return _symmetric_comm(grads, [jax.ShapeDtypeStruct((4,) + g.shape[1:], g.dtype) for g in grads], {}, 4, copies)


def _rs_to_chips(parts):
    def copies(in_refs, out_refs, send_sems, recv_sems):
        x, y, c = lax.axis_index("x"), lax.axis_index("y"), lax.axis_index("c")
        chips = [(1 - x, y), (x, 1 - y), (1 - x, 1 - y)]
        return [_remote(in_refs[a].at[2 * cx + cy], out_refs[a].at[k], send_sems.at[3 * a + k], recv_sems.at[3 * a + k], (cx, cy, c))
                for a in range(len(parts)) for k, (cx, cy) in enumerate(chips)]

    return _symmetric_comm(parts, [jax.ShapeDtypeStruct((3,) + p.shape[1:], p.dtype) for p in parts], {}, 3, copies)


def _comm_call(name, comm):
    n_in, n_out = len(comm.inputs), len(comm.out_shape)

    def body(*refs):
        ins, outs, sems = refs[:n_in], refs[n_in:n_in + n_out], refs[n_in + n_out:]
        comm.start(ins, outs, sems)
        comm.finish(ins, outs, sems)

    return pl.pallas_call(
        body, name=name, out_shape=list(comm.out_shape), in_specs=_hbm_specs(n_in), out_specs=_hbm_specs(n_out),
        scratch_shapes=list(comm.scratch), input_output_aliases=dict(comm.aliases),
    )(*comm.inputs)


def _small_all_reduce(parts, loss8):
    n = len(parts)

    def body(*refs):
        p_refs, loss_ref, out_ref, pack, land, send_sems, recv_sems = refs[:n], *refs[n:]
        x, y, c = lax.axis_index("x"), lax.axis_index("y"), lax.axis_index("c")
        me = 4 * x + 2 * y + c
        pack[...] = jnp.zeros_like(pack)
        for r, ref in enumerate(p_refs):
            pack[r:r + 1, 0:ref.shape[1]] = ref[...]
        pack[LOSS_ROW:LOSS_ROW + 1, 0:LANES] = loss_ref[0:1, :]
        land[me] = pack[...]
        cps = []
        for k in range(1, N_DEV):
            peer = (x ^ (k >> 2), y ^ ((k >> 1) & 1), c ^ (k & 1))
            cps.append(_remote(pack, land.at[me], send_sems.at[k - 1], recv_sems.at[k - 1], peer))
        for cp in cps:
            cp.start()
        for cp in cps:
            cp.wait()
        acc = land[0]
        for d in range(1, N_DEV):
            acc = acc + land[d]
        out_ref[...] = acc

    vmem = pl.BlockSpec(memory_space=pltpu.VMEM)
    return pl.pallas_call(
        body, name="small_all_reduce",
        out_shape=jax.ShapeDtypeStruct((SMALL_ROWS, D_MODEL), F32),
        in_specs=[vmem] * (n + 1), out_specs=vmem,
        scratch_shapes=[pltpu.VMEM((SMALL_ROWS, D_MODEL), F32), pltpu.VMEM((N_DEV, SMALL_ROWS, D_MODEL), F32),
                        pltpu.SemaphoreType.DMA((N_DEV - 1,)), pltpu.SemaphoreType.DMA((N_DEV - 1,))],
    )(*parts, loss8)


def _rs_sibling_sum(name, grad, got, place):
    _, rows, cols = grad.shape

    def body(place_ref, g_ref, l_ref, own_ref, b_ref):
        s = g_ref[...] + l_ref[...]
        b_ref[...] = s.astype(BF16)

        @pl.when(pl.program_id(0) == place_ref[1])
        def _():
            own_ref[...] = s

    by_chip = pl.BlockSpec((None, rows, cols), lambda q, place_ref: (q, 0, 0))
    return pl.pallas_call(
        body, name=name,
        grid_spec=pltpu.PrefetchScalarGridSpec(
            num_scalar_prefetch=1, grid=(4,),
            in_specs=[pl.BlockSpec((None, rows, cols), lambda q, place_ref: (2 * q + place_ref[0], 0, 0)), by_chip],
            out_specs=[pl.BlockSpec((rows, cols), lambda q, place_ref: (0, 0)), by_chip]),
        out_shape=(jax.ShapeDtypeStruct((rows, cols), F32), jax.ShapeDtypeStruct((4, rows, cols), BF16)),
        compiler_params=_cparams("arbitrary"),
    )(place, grad, got)


def _adamw_math(w, g, m, v):
    m2 = ADAM_B1 * m + (1.0 - ADAM_B1) * g
    v2 = ADAM_B2 * v + (1.0 - ADAM_B2) * (g * g)
    m_hat = m2 / (1.0 - ADAM_B1 ** ADAM_STEP)
    v_hat = v2 / (1.0 - ADAM_B2 ** ADAM_STEP)
    delta = -ADAM_LR * (m_hat / (jnp.sqrt(v_hat) + ADAM_EPS) + ADAM_WD * w)
    return delta, m2, v2


def _update_tile(rows):
    return 256 if rows % 256 == 0 else rows


def _rs_final_sum(name, own, got):
    def body(o_ref, r_ref, g_out):
        g = o_ref[...]
        for k in range(3):
            g = g + r_ref[k].astype(F32)
        g_out[...] = g

    return pl.pallas_call(body, name=name, out_shape=jax.ShapeDtypeStruct(own.shape, F32))(own, got)


def _adamw_sharded(name, w, m, v, own, got):
    _, rows, cols = w.shape
    tr = _update_tile(rows)

    def body(o_ref, r_ref, w_ref, m_ref, v_ref, g_out, d_out, m_out, v_out):
        g = o_ref[:, 0:cols]
        for k in range(3):
            g = g + r_ref[k, :, 0:cols].astype(F32)
        d, m2, v2 = _adamw_math(w_ref[0], g, m_ref[0], v_ref[0])
        g_out[0] = g
        d_out[0] = d
        m_out[0] = m2
        v_out[0] = v2

    mine = pl.BlockSpec((1, tr, cols), lambda i: (0, i, 0))
    shp = jax.ShapeDtypeStruct(w.shape, F32)
    wide = own.shape[1]
    return pl.pallas_call(
        body, name=name, grid=(rows // tr,), out_shape=(shp,) * 4,
        in_specs=[pl.BlockSpec((tr, wide), lambda i: (i, 0)), pl.BlockSpec((3, tr, wide), lambda i: (0, i, 0)),
                  mine, mine, mine],
        out_specs=[mine] * 4,
        compiler_params=_cparams("parallel"),
    )(own, got, w, m, v)


def _adamw_given(name, g, w, m, v):
    _, rows, cols = w.shape
    tr = _update_tile(rows)

    def body(g_ref, w_ref, m_ref, v_ref, g_out, d_out, m_out, v_out):
        g = g_ref[...]
        d, m2, v2 = _adamw_math(w_ref[0], g, m_ref[0], v_ref[0])
        g_out[0] = g
        d_out[0] = d
        m_out[0] = m2
        v_out[0] = v2

    own = pl.BlockSpec((1, tr, cols), lambda i: (0, i, 0))
    shp = jax.ShapeDtypeStruct(w.shape, F32)
    return pl.pallas_call(
        body, name=name, grid=(rows // tr,), out_shape=(shp,) * 4,
        in_specs=[pl.BlockSpec((tr, cols), lambda i: (i, 0)), own, own, own], out_specs=[own] * 4,
        compiler_params=_cparams("parallel"),
    )(g, w, m, v)


def _adamw_small(total, ws, ms, vs):
    n = len(ws)

    def body(*refs):
        t_ref = refs[0]
        w_refs, m_refs, v_refs = refs[1:1 + n], refs[1 + n:1 + 2 * n], refs[1 + 2 * n:1 + 3 * n]
        outs = refs[1 + 3 * n:]
        outs[0][...] = t_ref[LOSS_ROW:LOSS_ROW + 1, 0:1]
        for r in range(n):
            g = t_ref[r:r + 1, 0:w_refs[r].shape[1]]
            d, m2, v2 = _adamw_math(w_refs[r][...], g, m_refs[r][...], v_refs[r][...])
            for k, val in enumerate((g, d, m2, v2)):
                outs[1 + 4 * r + k][...] = val

    vmem = pl.BlockSpec(memory_space=pltpu.VMEM)
    out_shape = [jax.ShapeDtypeStruct((1, 1), F32)]
    for w in ws:
        out_shape += [jax.ShapeDtypeStruct(w.shape, F32)] * 4
    return pl.pallas_call(
        body, name="adamw_small", out_shape=out_shape,
        in_specs=[vmem] * (1 + 3 * n), out_specs=[vmem] * len(out_shape),
    )(total, *ws, *ms, *vs)


def _rope_tables(pos_col):
    t = pos_col.shape[0]
    inv = (np.float32(ROPE_THETA) ** (-np.arange(0, ROPE, 2, dtype=np.float32) / np.float32(ROPE))).astype(np.float32)
    freq = np.zeros((1, LANES), np.float32)
    freq[0, NOPE:NOPE + ROPE // 2] = inv
    freq[0, NOPE + ROPE // 2:NOPE + ROPE] = inv
    tm = _row_tile(t)

    def body(p_ref, f_ref, c_ref, s_ref):
        ang = p_ref[...].astype(F32) * f_ref[...]
        c_ref[...] = jnp.cos(ang)
        s_ref[...] = jnp.sin(ang)

    shp = jax.ShapeDtypeStruct((t, LANES), F32)
    return pl.pallas_call(
        body, name="rope_tables", grid=(t // tm,), out_shape=(shp, shp),
        in_specs=[pl.BlockSpec((tm, 1), lambda i: (i, 0)), pl.BlockSpec((1, LANES), lambda i: (0, 0))],
        out_specs=(pl.BlockSpec((tm, LANES), lambda i: (i, 0)),) * 2,
        compiler_params=_cparams("parallel"),
    )(pos_col, jnp.asarray(freq))


def _in_proj(x, g, w_qkv, w_rest, comm=None):
    t = x.shape[0]
    tm = _row_tile(t)

    def main(ins, outs, scr):
        x_ref, g_ref, wq_ref, wr_ref = ins
        h_ref, fq_ref, fk_ref, fv_ref, r_ref = outs
        h, _ = _rms(x_ref[...], g_ref[...])
        hb = h.astype(BF16)
        h_ref[...] = hb
        for n, (ref, factor) in enumerate(((fq_ref, FOX_Q_FACTOR), (fk_ref, None), (fv_ref, None))):
            part = _dot_nt(hb, wq_ref[n * FOX_WIDTH:(n + 1) * FOX_WIDTH, :])
            ref[...] = (part if factor is None else part * factor).astype(BF16)
        r_ref[...] = _dot_nt(hb, wr_ref[...])

    row = lambda n: pl.BlockSpec((tm, n), lambda i: (i, 0))
    full = lambda a: pl.BlockSpec(a.shape, lambda i: (0,) * a.ndim)
    return _hosted_call(
        "in_proj", main, (t // tm,), [x, g, w_qkv, w_rest], [row(D_MODEL), full(g), full(w_qkv), full(w_rest)],
        (jax.ShapeDtypeStruct((t, D_MODEL), BF16),) + (jax.ShapeDtypeStruct((t, FOX_WIDTH), BF16),) * 3
        + (jax.ShapeDtypeStruct((t, REST_COLS), F32),),
        (row(D_MODEL), row(FOX_WIDTH), row(FOX_WIDTH), row(FOX_WIDTH), row(REST_COLS)), [], comm)


def _log_sigmoid(z):
    return jnp.minimum(z, 0.0) - jnp.log(1.0 + jnp.exp(-jnp.abs(z)))


def _split3(v):
    hi = v.astype(BF16)
    r1 = v - hi.astype(F32)
    mid = r1.astype(BF16)
    lo = (r1 - mid.astype(F32)).astype(BF16)
    return hi, mid, lo


def _scan_tile(t):
    return 512 if t >= 2048 else (256 if t >= 256 else t)


def _forget_cumsum(rest, b128):
    t = rest.shape[0]
    tb = _scan_tile(t)

    def body(r_ref, b_ref, row_ref, rep_ref, f_sc, carry):
        @pl.when(pl.program_id(0) == 0)
        def _():
            carry[...] = jnp.zeros_like(carry)
        lf = _log_sigmoid(r_ref[...] + b_ref[...])
        tri = (lax.broadcasted_iota(jnp.int32, (tb, tb), 0) >= lax.broadcasted_iota(jnp.int32, (tb, tb), 1)).astype(BF16)
        hi, mid, lo = _split3(lf)
        f_sc[...] = (_dot(tri, hi) + _dot(tri, mid)) + _dot(tri, lo) + carry[...]
        carry[...] = f_sc[tb - 1:tb, :]
        f2 = f_sc[...] * LOG2E
        row_ref[...] = jnp.transpose(f2)[0:HEADS, :]
        lane = _lane()
        for h in range(HEADS):
            col = jnp.sum(jnp.where(lane == h, f2, 0.0), axis=1, keepdims=True)
            rep_ref[h] = jnp.broadcast_to(col, (tb, LANES))

    return pl.pallas_call(
        body, name="forget_cumsum", grid=(t // tb,),
        out_shape=(jax.ShapeDtypeStruct((HEADS, t), F32), jax.ShapeDtypeStruct((HEADS, t, LANES), F32)),
        in_specs=[pl.BlockSpec((tb, LANES), lambda i: (i, 0)), pl.BlockSpec((1, LANES), lambda i: (0, 0))],
        out_specs=(pl.BlockSpec((HEADS, tb), lambda i: (0, i)), pl.BlockSpec((HEADS, tb, LANES), lambda i: (0, i, 0))),
        scratch_shapes=[pltpu.VMEM((tb, LANES), F32), pltpu.VMEM((1, LANES), F32)],
        compiler_params=_cparams("arbitrary"),
    )(rest, b128)


def _forget_bwd(rest, b128, d_fq, d_fk):
    t = rest.shape[0]
    tb = _scan_tile(t)
    nb = t // tb

    def body(r_ref, b_ref, dfq_ref, dfk_ref, dz_ref, db_ref, carry):
        @pl.when(pl.program_id(0) == 0)
        def _():
            carry[...] = jnp.zeros_like(carry)
            db_ref[...] = jnp.zeros_like(db_ref)
        tri = (lax.broadcasted_iota(jnp.int32, (tb, tb), 0) <= lax.broadcasted_iota(jnp.int32, (tb, tb), 1)).astype(BF16)
        lane = _lane()
        df = jnp.zeros((tb, LANES), F32)
        for h in range(HEADS):
            df = df + jnp.where(lane == h, dfq_ref[h] + dfk_ref[h], 0.0)
        hi, mid, lo = _split3(df)
        dlf = (_dot(tri, hi) + _dot(tri, mid)) + _dot(tri, lo) + carry[...]
        z = r_ref[...] + b_ref[...]
        dz = dlf / (1.0 + jnp.exp(z))
        dz_ref[...] = dz
        db_ref[...] += jnp.sum(dz, axis=0, keepdims=True)
        carry[...] = carry[...] + jnp.sum(df, axis=0, keepdims=True)

    rev = lambda i: (nb - 1 - i, 0)
    rev3 = pl.BlockSpec((HEADS, tb, LANES), lambda i: (0, nb - 1 - i, 0))
    return pl.pallas_call(
        body, name="forget_bwd", grid=(nb,),
        out_shape=(jax.ShapeDtypeStruct((t, LANES), F32), jax.ShapeDtypeStruct((1, LANES), F32)),
        in_specs=[pl.BlockSpec((tb, LANES), rev), pl.BlockSpec((1, LANES), lambda i: (0, 0)), rev3, rev3],
        out_specs=(pl.BlockSpec((tb, LANES), rev), pl.BlockSpec((1, LANES), lambda i: (0, 0))),
        scratch_shapes=[pltpu.VMEM((1, LANES), F32)],
        compiler_params=_cparams("arbitrary"),
    )(rest, b128, d_fq, d_fk)


def _mla_prep(rest, gq, gkv, wq, wkv, cos, sin):
    t = rest.shape[0]
    tm = _row_tile(t)

    def body(r_ref, gq_ref, gkv_ref, wq_ref, wkv_ref, c_ref, s_ref, q_ref, k_ref, kv_ref, cq_ref, ckv_ref):
        cos_, sin_ = c_ref[...], s_ref[...]
        cq, _ = _rms(r_ref[:, REST_CQ:REST_CKV], gq_ref[...])
        ckv, _ = _rms(r_ref[:, REST_CKV:REST_KR], gkv_ref[...])
        cqb, ckvb = cq.astype(BF16), ckv.astype(BF16)
        cq_ref[...] = cqb
        ckv_ref[...] = ckvb
        k_rope = _rope(r_ref[:, REST_KR:REST_COLS], cos_, sin_)
        lo = _lane() < NOPE
        for h in range(HEADS):
            q_ref[h] = (_rope(_dot(cqb, wq_ref[h]), cos_, sin_) * MLA_Q_FACTOR).astype(BF16)
            kv = _dot(ckvb, wkv_ref[h])
            kv_ref[h] = kv.astype(BF16)
            k_ref[h] = (jnp.where(lo, kv, 0.0) + k_rope).astype(BF16)

    row = lambda n: pl.BlockSpec((tm, n), lambda i: (i, 0))
    full = lambda a: pl.BlockSpec(a.shape, lambda i: (0,) * a.ndim)
    heads = pl.BlockSpec((HEADS, tm, LANES), lambda i: (0, i, 0))
    hshape = jax.ShapeDtypeStruct((HEADS, t, LANES), BF16)
    return pl.pallas_call(
        body, name="mla_prep", grid=(t // tm,),
        out_shape=(hshape, hshape, hshape, jax.ShapeDtypeStruct((t, Q_RANK), BF16), jax.ShapeDtypeStruct((t, KV_RANK), BF16)),
        in_specs=[row(REST_COLS), full(gq), full(gkv), full(wq), full(wkv), row(LANES), row(LANES)],
        out_specs=(heads, heads, heads, row(Q_RANK), row(KV_RANK)),
        compiler_params=_cparams("parallel"),
    )(rest, gq, gkv, wq, wkv, cos, sin)


def _tile_lanes(x, n):
    return jnp.tile(x, (1, n)) if n > 1 else x


class _Comm(NamedTuple):
    inputs: tuple
    out_shape: tuple
    aliases: dict
    scratch: tuple
    start: Callable
    finish: Callable


def _hosted_call(name, main, grid, args, in_specs, out_shape, out_specs, scratch, comm):
    n_in, n_out, n_scr = len(args), len(out_shape), len(scratch)
    c_in = list(comm.inputs) if comm else []
    c_out = list(comm.out_shape) if comm else []

    def at_step(which):
        hit = pl.program_id(0) == which[0]
        for axis in range(1, len(grid)):
            hit = jnp.logical_and(hit, pl.program_id(axis) == which[axis])
        return hit

    def body(*refs):
        bounds = [0, n_in, len(c_in), n_out, len(c_out), n_scr]
        starts = [sum(bounds[:k + 1]) for k in range(len(bounds))]
        ins, cins, outs, couts, scr = [refs[a:b] for a, b in zip(starts[:-1], starts[1:])]
        sems = refs[starts[-1]:]
        if comm:
            @pl.when(at_step([0] * len(grid)))
            def _():
                comm.start(cins, couts, sems)
        main(ins, outs, scr)
        if comm:
            @pl.when(at_step([n - 1 for n in grid]))
            def _():
                comm.finish(cins, couts, sems)

    res = pl.pallas_call(
        body, name=name, grid=grid,
        out_shape=list(out_shape) + c_out,
        in_specs=list(in_specs) + _hbm_specs(len(c_in)),
        out_specs=list(out_specs) + _hbm_specs(len(c_out)),
        scratch_shapes=list(scratch) + (list(comm.scratch) if comm else []),
        input_output_aliases={n_in + i: n_out + o for i, o in comm.aliases.items()} if comm else {},
        compiler_params=_cparams(*(["arbitrary"] * len(grid))),
    )(*args, *c_in)
    return res[:n_out], res[n_out:]


def _stat_rows(x):
    return jnp.transpose(x)[0:8, :]


FWD_HEADS = 4


def _attn_fwd(fox, q, k, v, f2_rows=None, comm=None):
    t = q.shape[0] if fox else q.shape[1]
    tq = _row_tile(t)
    nq = t // tq
    nh = FWD_HEADS
    wide = (nh // 2) * LANES

    def main(ins, outs, scr):
        q_ref, k_ref, v_ref = ins[:3]
        fr_ref = ins[3] if fox else None
        o_ref, lset_ref = outs
        m_sc, acc_sc = scr
        i = pl.program_id(1)
        lo = _lane() < HEAD_DIM
        hi = jnp.logical_not(lo)
        zero, one = jnp.zeros((), BF16), jnp.ones((), BF16)
        lanes_of = lambda h: slice((h // 2) * LANES, (h // 2 + 1) * LANES)
        if fox:
            qs = [jnp.where(lo if h % 2 == 0 else hi, q_ref[:, lanes_of(h)], zero) for h in range(nh)]
            sum_lanes = [hi if h % 2 == 0 else lo for h in range(nh)]
        else:
            qs = [q_ref[h] for h in range(nh)]
            sum_lanes = [lo] * nh
        m_sc[...] = jnp.full_like(m_sc, -jnp.inf)
        acc_sc[...] = jnp.zeros_like(acc_sc)

        def block(j, r0, nr, c0, nc, seen_from):
            rows = slice(r0, r0 + nr)
            sl = pl.ds(pl.multiple_of(j * tq + c0, math.gcd(tq, c0) if c0 else tq), nc)
            if seen_from is not None:
                seen = (lax.broadcasted_iota(jnp.int32, (nr, nc), 1)
                        <= lax.broadcasted_iota(jnp.int32, (nr, nc), 0) + seen_from)
            for h in range(nh):
                kj, vj = (k_ref[sl, lanes_of(h)], v_ref[sl, lanes_of(h)]) if fox else (k_ref[h, sl, :], v_ref[h, sl, :])
                s = _dot_nt(qs[h][rows], kj)
                if fox:
                    s = s - fr_ref[h, j, :, c0:c0 + nc]
                if seen_from is not None:
                    s = jnp.where(seen, s, -jnp.inf)
                m_prev = m_sc[h, rows]
                m_new = jnp.maximum(m_prev, jnp.max(s, axis=1, keepdims=True))
                p = jnp.exp2((s - _tile_lanes(m_new, nc // LANES)).astype(BF16))
                vj = jnp.where(sum_lanes[h], one, vj)
                acc_sc[h, rows] = jnp.exp2(m_prev - m_new) * acc_sc[h, rows] + _dot(p, vj)
                m_sc[h, rows] = m_new

        def loop_body(j, carry):
            block(j, 0, tq, 0, tq, None)
            return carry

        lax.fori_loop(0, i, loop_body, 0)
        half = tq // 2
        if half % LANES == 0:
            block(i, 0, half, 0, half, 0)
            block(i, half, half, 0, tq, half)
        else:
            block(i, 0, tq, 0, tq, 0)
        res = []
        for h in range(nh):
            acc = acc_sc[h]
            swapped = pltpu.roll(acc, HEAD_DIM, 1)
            res.append(acc / swapped)
            lse2 = m_sc[h] + jnp.log(jnp.where(sum_lanes[h], acc, swapped)) * LOG2E
            lset_ref[h, 0] = _stat_rows(lse2)
        for pr in range(nh // 2):
            even = res[2 * pr] if fox else pltpu.roll(res[2 * pr], HEAD_DIM, 1)
            o_ref[:, pr * LANES:(pr + 1) * LANES] = jnp.where(lo, even, res[2 * pr + 1])

    if fox:
        in_specs = [pl.BlockSpec((tq, wide), lambda g, i: (i, g))] + [pl.BlockSpec((t, wide), lambda g, i: (0, g))] * 2
        in_specs += [pl.BlockSpec((nh, nq, 1, tq), lambda g, i: (g, 0, 0, 0))]
        args = [q, k, v, f2_rows]
    else:
        in_specs = [pl.BlockSpec((nh, tq, LANES), lambda g, i: (g, i, 0))] + [pl.BlockSpec((nh, t, LANES), lambda g, i: (g, 0, 0))] * 2
        args = [q, k, v]
    return _hosted_call(
        "fox_attn_fwd" if fox else "mla_attn_fwd", main, (HEADS // nh, nq), args, in_specs,
        (jax.ShapeDtypeStruct((t, 4 * LANES), F32), jax.ShapeDtypeStruct((HEADS, nq, 8, tq), F32)),
        (pl.BlockSpec((tq, wide), lambda g, i: (i, g)), pl.BlockSpec((nh, 1, 8, tq), lambda g, i: (g, i, 0, 0))),
        [pltpu.VMEM((nh, tq, LANES), F32), pltpu.VMEM((nh, tq, LANES), F32)], comm)


def _head_do(fox, hh, do2, lo):
    if fox:
        return jnp.where(lo if hh == 0 else jnp.logical_not(lo), do2, 0.0)
    return jnp.where(lo, 0.0, pltpu.roll(do2, HEAD_DIM, 1) if hh == 0 else do2)


def _attn_bwd(fox, q, k, v, do, lse_rows, delta_rows, f2_rep=None, comm=None):
    t = q.shape[0] if fox else q.shape[1]
    tq = _row_tile(t)
    nq = t // tq
    scale = FOX_SCALE if fox else MLA_SCALE

    def main(ins, outs, scr):
        if fox:
            q_ref, k_ref, v_ref, f_ref, do_ref, lse_ref, dl_ref = ins
            dq_ref, dk_ref, dv_ref, dfq_ref, dfk_ref = outs
        else:
            q_ref, k_ref, v_ref, do_ref, lse_ref, dl_ref = ins
            dq_ref, dkv_ref, dkr_ref = outs
        dq_sc, dk_sc, dv_sc = scr
        j = pl.program_id(1)
        lane = _lane()
        lo = lane < HEAD_DIM
        hi = jnp.logical_not(lo)
        zero, one = jnp.zeros((), BF16), jnp.ones((), BF16)

        @pl.when(j == 0)
        def _():
            dq_sc[...] = jnp.zeros_like(dq_sc)

        dk_sc[...] = jnp.zeros_like(dk_sc)
        dv_sc[...] = jnp.zeros_like(dv_sc)

        def block(i, r0, nr, c0, nc, masked):
            rows, cols = slice(r0, r0 + nr), slice(c0, c0 + nc)
            sl = pl.ds(pl.multiple_of(i * tq + c0, math.gcd(tq, c0) if c0 else tq), nc)
            do_i = do_ref[sl, :]
            if masked:
                seen = lax.broadcasted_iota(jnp.int32, (nr, nc), 1) >= lax.broadcasted_iota(jnp.int32, (nr, nc), 0)
            for hh in range(2):
                kj = k_ref[rows, :] if fox else k_ref[hh, rows, :]
                vj = v_ref[rows, :] if fox else v_ref[hh, rows, :]
                qi = jnp.where(lo if hh == 0 else hi, q_ref[sl, :], zero) if fox else q_ref[hh, sl, :]
                dob = _head_do(fox, hh, do_i, lo).astype(BF16)
                st = _dot_nt(kj, qi)
                if fox:
                    st = st - _tile_lanes(f_ref[hh, rows, :], nc // LANES)
                if masked:
                    st = jnp.where(seen, st, -jnp.inf)
                pt = jnp.exp2(st - lse_ref[hh, i, 0:1, cols])
                dpt = _dot_nt(vj, dob)
                dst = (pt * (dpt - dl_ref[hh, i, 0:1, cols])).astype(BF16)
                dv_sc[hh, rows] += _dot(pt.astype(BF16), dob)
                if fox:
                    other = hi if hh == 0 else lo
                    qi = jnp.where(other, one, qi)
                    kj = jnp.where(other, one, kj)
                dk_sc[hh, rows] += _dot(dst, qi)
                dq_sc[hh, sl, :] += _dot_tn(dst, kj)

        def loop_body(i, carry):
            block(i, 0, tq, 0, tq, False)
            return carry

        half = tq // 2
        if half % LANES == 0:
            block(j, 0, half, 0, tq, True)
            block(j, half, half, half, half, True)
        else:
            block(j, 0, tq, 0, tq, True)
        lax.fori_loop(j + 1, nq, loop_body, 0)
        if fox:
            dk_ref[...] = (jnp.where(lo, dk_sc[0], dk_sc[1]) * LN2).astype(BF16)
            dv_ref[...] = (dv_sc[0] + dv_sc[1]).astype(BF16)
            for hh in range(2):
                dk = dk_sc[hh]
                dfk_ref[hh] = -jnp.where(hi if hh == 0 else lo, dk, pltpu.roll(dk, HEAD_DIM, 1))
        else:
            rope_lanes = jnp.logical_and(lane >= NOPE, lane < NOPE + ROPE)
            dkr = jnp.zeros((tq, LANES), F32)
            for hh in range(2):
                dk = dk_sc[hh] * LN2
                dkv_ref[hh] = jnp.where(lo, dk, dv_sc[hh])
                dkr = dkr + jnp.where(rope_lanes, dk, 0.0)
            dkr_ref[0] = dkr

        @pl.when(j == nq - 1)
        def _():
            for i in range(nq):
                rows = slice(i * tq, (i + 1) * tq)
                if fox:
                    dq_ref[rows, :] = (jnp.where(lo, dq_sc[0, rows, :], dq_sc[1, rows, :]) * scale).astype(BF16)
                    for hh in range(2):
                        acc = dq_sc[hh, rows, :]
                        dfq_ref[hh, rows, :] = jnp.where(hi if hh == 0 else lo, acc, pltpu.roll(acc, HEAD_DIM, 1))
                else:
                    for hh in range(2):
                        dq_ref[hh, rows, :] = dq_sc[hh, rows, :] * scale

    stat = pl.BlockSpec((2, tq, LANES), lambda p, j: (p, j, 0))
    stat_all = pl.BlockSpec((2, t, LANES), lambda p, j: (p, 0, 0))
    rows4 = pl.BlockSpec((2, nq, 8, tq), lambda p, j: (p, 0, 0, 0))
    pair = pl.BlockSpec((tq, LANES), lambda p, j: (j, p))
    pair_all = pl.BlockSpec((t, LANES), lambda p, j: (0, p))
    if fox:
        in_specs = [pair_all, pair, pair, stat]
        args = [q, k, v, f2_rep]
    else:
        in_specs = [stat_all, stat, stat]
        args = [q, k, v]
    in_specs += [pair_all, rows4, rows4]
    args += [do, lse_rows, delta_rows]
    heads_f32 = jax.ShapeDtypeStruct((HEADS, t, LANES), F32)
    if fox:
        wide = jax.ShapeDtypeStruct((t, 4 * LANES), BF16)
        out_shape = (wide, wide, wide, heads_f32, heads_f32)
        out_specs = (pair_all, pair, pair, stat_all, stat)
    else:
        out_shape = (heads_f32, heads_f32, jax.ShapeDtypeStruct((HEADS // 2, t, LANES), F32))
        out_specs = (stat_all, stat, pl.BlockSpec((1, tq, LANES), lambda p, j: (p, j, 0)))
    acc = pltpu.VMEM((2, tq, LANES), F32)
    return _hosted_call("fox_attn_bwd" if fox else "mla_attn_bwd", main, (HEADS // 2, nq), args, in_specs,
                        out_shape, out_specs, [pltpu.VMEM((2, t, LANES), F32), acc, acc], comm)


def _mlp_tile(t):
    return 256 if t >= 2048 else 128


def _resident(a):
    return pl.BlockSpec(a.shape, lambda i: (0,) * a.ndim, pipeline_mode=pl.Buffered(1))


FF_CHUNK = 512


def _out_mlp_fwd(x0, fox_o, mla_o, g_fox, g_mla, w_o, g_mlp, w_up, w_down, g_fin, target):
    t = x0.shape[0]
    tm = _mlp_tile(t)
    assert w_up.shape == (D_FF // FF_CHUNK, D_MODEL, FF_CHUNK)

    def body(x0_ref, f_ref, m_ref, gfo_ref, gml_ref, wo_ref, g_ref, wu_ref, wd_ref, gf_ref, t_ref,
             x1_ref, mix_ref, u_ref, h_ref, dx_ref, dxb_ref, loss_ref, dg_ref, a_sc):
        @pl.when(pl.program_id(0) == 0)
        def _():
            loss_ref[...] = jnp.zeros_like(loss_ref)
            dg_ref[...] = jnp.zeros_like(dg_ref)

        nf, _ = _rms(f_ref[...], gfo_ref[...])
        nm, _ = _rms(m_ref[...], gml_ref[...])
        nfb, nmb = nf.astype(BF16), nm.astype(BF16)
        mix_ref[:, :FOX_WIDTH] = nfb
        mix_ref[:, FOX_WIDTH:] = nmb
        x = x0_ref[...] + _dot(nfb, wo_ref[:FOX_WIDTH, :]) + _dot(nmb, wo_ref[FOX_WIDTH:, :])
        x1_ref[...] = x
        h, _ = _rms(x, g_ref[...])
        hb = h.astype(BF16)
        h_ref[...] = hb
        for f in range(D_FF // FF_CHUNK):
            sl = slice(f * FF_CHUNK, (f + 1) * FF_CHUNK)
            u = _dot(hb, wu_ref[f])
            u_ref[:, sl] = u
            r = jnp.maximum(u, 0.0)
            a_sc[:, sl] = (r * r).astype(BF16)
        x2 = x + _dot(a_sc[...], wd_ref[...])
        y, r2 = _rms(x2, gf_ref[...])
        err = y - t_ref[...]
        loss_ref[...] += 0.5 * jnp.sum(jnp.mean(err * err, axis=-1, keepdims=True))
        dx, dg = _rms_bwd(x2, gf_ref[...], r2, err * (1.0 / D_MODEL))
        dx_ref[...] = dx
        dxb_ref[...] = dx.astype(BF16)
        dg_ref[...] += dg

    row = lambda n: pl.BlockSpec((tm, n), lambda i: (i, 0))
    vec = pl.BlockSpec((1, D_MODEL), lambda i: (0, 0))
    half = pl.BlockSpec((1, FOX_WIDTH), lambda i: (0, 0))
    wide_f32 = jax.ShapeDtypeStruct((t, D_MODEL), F32)
    wide_bf16 = jax.ShapeDtypeStruct((t, D_MODEL), BF16)
    return pl.pallas_call(
        body, name="out_mlp_fwd", grid=(t // tm,),
        out_shape=(wide_f32, wide_bf16, jax.ShapeDtypeStruct((t, D_FF), F32), wide_bf16, wide_f32, wide_bf16,
                   jax.ShapeDtypeStruct((8, LANES), F32), jax.ShapeDtypeStruct((1, D_MODEL), F32)),
        in_specs=[row(D_MODEL), row(FOX_WIDTH), row(MLA_WIDTH), half, half, _resident(w_o), vec, _resident(w_up),
                  _resident(w_down), vec, row(D_MODEL)],
        out_specs=(row(D_MODEL), row(D_MODEL), row(D_FF), row(D_MODEL), row(D_MODEL), row(D_MODEL),
                   pl.BlockSpec((8, LANES), lambda i: (0, 0)), vec),
        scratch_shapes=[pltpu.VMEM((tm, D_FF), BF16)],
        compiler_params=_cparams("arbitrary"),
    )(x0, fox_o, mla_o, g_fox, g_mla, w_o, g_mlp, w_up, w_down, g_fin, target)


def _mlp_bwd(dx2, u, x1, g_mlp, w_up, w_down):
    t = x1.shape[0]
    tm = _mlp_tile(t)

    def body(dx_ref, u_ref, x_ref, g_ref, wu_ref, wd_ref, du_ref, a_ref, dx1_ref, dx1b_ref, dg_ref):
        @pl.when(pl.program_id(0) == 0)
        def _():
            dg_ref[...] = jnp.zeros_like(dg_ref)

        dx2 = dx_ref[...]
        dxb = dx2.astype(BF16)
        for f in range(D_FF // FF_CHUNK):
            sl = slice(f * FF_CHUNK, (f + 1) * FF_CHUNK)
            r = jnp.maximum(u_ref[:, sl], 0.0)
            a_ref[:, sl] = (r * r).astype(BF16)
            da = _dot_nt(dxb, wd_ref[sl, :])
            du_ref[:, sl] = (da * (2.0 * r)).astype(BF16)
        dh = _dot_nt(du_ref[...], wu_ref[...])
        x = x_ref[...]
        _, r1 = _rms(x, g_ref[...])
        dx, dg = _rms_bwd(x, g_ref[...], r1, dh)
        dx1 = dx2 + dx
        dx1_ref[...] = dx1
        dx1b_ref[...] = dx1.astype(BF16)
        dg_ref[...] += dg

    row = lambda n: pl.BlockSpec((tm, n), lambda i: (i, 0))
    vec = pl.BlockSpec((1, D_MODEL), lambda i: (0, 0))
    return pl.pallas_call(
        body, name="mlp_bwd", grid=(t // tm,),
        out_shape=(jax.ShapeDtypeStruct((t, D_FF), BF16), jax.ShapeDtypeStruct((t, D_FF), BF16),
                   jax.ShapeDtypeStruct((t, D_MODEL), F32), jax.ShapeDtypeStruct((t, D_MODEL), BF16),
                   jax.ShapeDtypeStruct((1, D_MODEL), F32)),
        in_specs=[row(D_MODEL), row(D_FF), row(D_MODEL), vec, _resident(w_up), _resident(w_down)],
        out_specs=(row(D_FF), row(D_FF), row(D_MODEL), row(D_MODEL), vec),
        compiler_params=_cparams("arbitrary"),
    )(dx2, u, x1, g_mlp, w_up, w_down)


def _matmul_tn(name, a, b, blocks=None):
    t, m = a.shape
    n = b.shape[1]
    tk = t if a.dtype == BF16 and b.dtype == BF16 else min(t, 2048)
    steps = t // tk
    bm = m if m <= 1024 else 512
    bn = n if n <= 1024 else 512
    width = bn if blocks is None else n // blocks
    per = bn // width

    def body(a_ref, b_ref, o_ref, acc_sc):
        kk = pl.program_id(2)

        @pl.when(kk == 0)
        def _():
            acc_sc[...] = jnp.zeros_like(acc_sc)

        acc_sc[...] += _dot_tn(a_ref[...].astype(BF16), b_ref[...].astype(BF16))

        @pl.when(kk == steps - 1)
        def _():
            if blocks is None:
                o_ref[...] = acc_sc[...]
            else:
                for s in range(per):
                    o_ref[s] = acc_sc[:, s * width:(s + 1) * width]

    if blocks is None:
        o_spec = pl.BlockSpec((bm, bn), lambda i, j, kk: (i, j))
        o_shape = (m, n)
    else:
        o_spec = pl.BlockSpec((per, bm, width), lambda i, j, kk: (j, i, 0))
        o_shape = (blocks, m, width)
    return pl.pallas_call(
        body, name=name, grid=(m // bm, n // bn, steps),
        out_shape=jax.ShapeDtypeStruct(o_shape, F32),
        in_specs=[pl.BlockSpec((tk, bm), lambda i, j, kk: (kk, i)), pl.BlockSpec((tk, bn), lambda i, j, kk: (kk, j))],
        out_specs=o_spec,
        scratch_shapes=[pltpu.VMEM((bm, bn), F32)],
        compiler_params=_cparams("parallel", "parallel", "arbitrary"),
    )(a, b)


def _dw_in(dfq, dfk, dfv, drest, h1):
    t = h1.shape[0]
    tk = min(t, 1024)
    off_ff = 3 * FOX_WIDTH
    off_cq = off_ff + HEADS
    off_kr = IN_COLS - ROPE

    def body(dq_ref, dk_ref, dv_ref, dr_ref, h_ref, o_ref):
        h = h_ref[...]
        r = _dot_tn(dr_ref[...], h)
        parts = [(slice(n * FOX_WIDTH, (n + 1) * FOX_WIDTH), _dot_tn(ref[...], h)) for n, ref in enumerate((dq_ref, dk_ref, dv_ref))]
        parts += [(slice(off_ff, off_cq), r[0:HEADS]), (slice(off_cq, off_kr), r[REST_CQ:REST_KR]),
                  (slice(off_kr, IN_COLS), r[REST_KR + NOPE:REST_KR + NOPE + ROPE])]

        @pl.when(pl.program_id(0) == 0)
        def _():
            for rows, val in parts:
                o_ref[rows, :] = val

        @pl.when(pl.program_id(0) > 0)
        def _():
            for rows, val in parts:
                o_ref[rows, :] += val

    tok = lambda n: pl.BlockSpec((tk, n), lambda kk: (kk, 0))
    return pl.pallas_call(
        body, name="dw_in", grid=(t // tk,),
        out_shape=jax.ShapeDtypeStruct((IN_COLS, D_MODEL), F32),
        in_specs=[tok(FOX_WIDTH), tok(FOX_WIDTH), tok(FOX_WIDTH), tok(REST_COLS), tok(D_MODEL)],
        out_specs=pl.BlockSpec((IN_COLS, D_MODEL), lambda kk: (0, 0)),
        compiler_params=_cparams("arbitrary"),
    )(dfq, dfk, dfv, drest, h1)


def _attn_out_bwd(dx1, fox_o, mla_o, gf, gm, w_o):
    t = dx1.shape[0]
    tm = _row_tile(t)

    def body(dx_ref, f_ref, m_ref, gf_ref, gm_ref, w_ref, df_ref, dm_ref, dlf_ref, dlm_ref, dgf_ref, dgm_ref):
        @pl.when(pl.program_id(0) == 0)
        def _():
            dgf_ref[...] = jnp.zeros_like(dgf_ref)
            dgm_ref[...] = jnp.zeros_like(dgm_ref)
        dxb = dx_ref[...].astype(BF16)
        lane = lax.broadcasted_iota(jnp.int32, (8, LANES), 1)
        picks = [(lane < HEAD_DIM).astype(BF16), (lane >= HEAD_DIM).astype(BF16)]
        for o_ref, g_ref, lo_row, d_ref, dl_ref, dg_ref in ((f_ref, gf_ref, 0, df_ref, dlf_ref, dgf_ref),
                                                             (m_ref, gm_ref, FOX_WIDTH, dm_ref, dlm_ref, dgm_ref)):
            dn = _dot_nt(dxb, w_ref[lo_row:lo_row + FOX_WIDTH, :])
            o = o_ref[...]
            _, r = _rms(o, g_ref[...])
            d, dg = _rms_bwd(o, g_ref[...], r, dn)
            d_ref[...] = d
            dg_ref[...] += dg
            prod = d * o
            for h in range(HEADS):
                parts = _split3(prod[:, (h // 2) * LANES:(h // 2 + 1) * LANES])
                dl_ref[h, 0] = (_dot_nt(picks[h % 2], parts[0]) + _dot_nt(picks[h % 2], parts[1])) + _dot_nt(picks[h % 2], parts[2])

    row = lambda n: pl.BlockSpec((tm, n), lambda i: (i, 0))
    full = lambda a: pl.BlockSpec(a.shape, lambda i: (0,) * a.ndim)
    vec = pl.BlockSpec((1, FOX_WIDTH), lambda i: (0, 0))
    rows = pl.BlockSpec((HEADS, 1, 8, tm), lambda i: (0, i, 0, 0))
    o_shape = jax.ShapeDtypeStruct((t, FOX_WIDTH), F32)
    g_shape = jax.ShapeDtypeStruct((1, FOX_WIDTH), F32)
    r_shape = jax.ShapeDtypeStruct((HEADS, t // tm, 8, tm), F32)
    return pl.pallas_call(
        body, name="attn_out_bwd", grid=(t // tm,),
        out_shape=(o_shape, o_shape, r_shape, r_shape, g_shape, g_shape),
        in_specs=[row(D_MODEL), row(FOX_WIDTH), row(MLA_WIDTH), full(gf), full(gm), full(w_o)],
        out_specs=(row(FOX_WIDTH), row(MLA_WIDTH), rows, rows, vec, vec),
        compiler_params=_cparams("arbitrary"),
    )(dx1, fox_o, mla_o, gf, gm, w_o)


def _mla_prep_bwd(dq, dkv, dkr, dz, rest, gq, gkv, wq, wkv, cos, sin):
    t = rest.shape[0]
    tm = _row_tile(t)

    def body(dq_ref, dkv_ref, dkr_ref, dz_ref, r_ref, gq_ref, gkv_ref, wq_ref, wkv_ref, c_ref, s_ref,
             dr_ref, dqp_ref, dkvb_ref, dgq_ref, dgkv_ref):
        @pl.when(pl.program_id(0) == 0)
        def _():
            dgq_ref[...] = jnp.zeros_like(dgq_ref)
            dgkv_ref[...] = jnp.zeros_like(dgkv_ref)
        cos_, sin_ = c_ref[...], s_ref[...]
        dcq = jnp.zeros((tm, Q_RANK), F32)
        dckv = jnp.zeros((tm, KV_RANK), F32)
        for h in range(HEADS):
            dqp = _rope_bwd(dq_ref[h], cos_, sin_).astype(BF16)
            dqp_ref[:, h * LANES:(h + 1) * LANES] = dqp
            dcq = dcq + _dot_nt(dqp, wq_ref[h])
            dkvb = dkv_ref[h].astype(BF16)
            dkvb_ref[:, h * LANES:(h + 1) * LANES] = dkvb
            dckv = dckv + _dot_nt(dkvb, wkv_ref[h])
        dkrope = dkr_ref[0]
        for pr in range(1, HEADS // 2):
            dkrope = dkrope + dkr_ref[pr]
        cq = r_ref[:, REST_CQ:REST_CKV]
        _, rq = _rms(cq, gq_ref[...])
        d_cq, dgq = _rms_bwd(cq, gq_ref[...], rq, dcq)
        ckv = r_ref[:, REST_CKV:REST_KR]
        _, rkv = _rms(ckv, gkv_ref[...])
        d_ckv, dgkv = _rms_bwd(ckv, gkv_ref[...], rkv, dckv)
        dgq_ref[...] += dgq
        dgkv_ref[...] += dgkv
        dr_ref[:, 0:REST_CQ] = dz_ref[...].astype(BF16)
        dr_ref[:, REST_CQ:REST_CKV] = d_cq.astype(BF16)
        dr_ref[:, REST_CKV:REST_KR] = d_ckv.astype(BF16)
        dr_ref[:, REST_KR:REST_COLS] = _rope_bwd(dkrope, cos_, sin_).astype(BF16)

    row = lambda n: pl.BlockSpec((tm, n), lambda i: (i, 0))
    full = lambda a: pl.BlockSpec(a.shape, lambda i: (0,) * a.ndim)
    heads = pl.BlockSpec((HEADS, tm, LANES), lambda i: (0, i, 0))
    hshape = jax.ShapeDtypeStruct((t, HEADS * LANES), BF16)
    return pl.pallas_call(
        body, name="mla_prep_bwd", grid=(t // tm,),
        out_shape=(jax.ShapeDtypeStruct((t, REST_COLS), BF16), hshape, hshape,
                   jax.ShapeDtypeStruct((1, Q_RANK), F32), jax.ShapeDtypeStruct((1, KV_RANK), F32)),
        in_specs=[heads, heads, pl.BlockSpec((HEADS // 2, tm, LANES), lambda i: (0, i, 0)), row(LANES), row(REST_COLS),
                  full(gq), full(gkv), full(wq), full(wkv), row(LANES), row(LANES)],
        out_specs=(row(REST_COLS), row(HEADS * LANES), row(HEADS * LANES), pl.BlockSpec((1, Q_RANK), lambda i: (0, 0)),
                   pl.BlockSpec((1, KV_RANK), lambda i: (0, 0))),
        compiler_params=_cparams("arbitrary"),
    )(dq, dkv, dkr, dz, rest, gq, gkv, wq, wkv, cos, sin)


def _in_proj_bwd(x, g, dx1, dfq, dfk, dfv, drest, w_qkv, w_rest, comm=None):
    t = x.shape[0]
    tm = _row_tile(t)

    def main(ins, outs, scr):
        x_ref, g_ref, dx1_ref, dq_ref, dk_ref, dv_ref, dr_ref, wq_ref, wr_ref = ins
        dx_ref, dg_ref = outs

        @pl.when(pl.program_id(0) == 0)
        def _():
            dg_ref[...] = jnp.zeros_like(dg_ref)
        dh = _dot(dr_ref[...], wr_ref[...])
        for n, ref in enumerate((dq_ref, dk_ref, dv_ref)):
            dh = dh + _dot(ref[...], wq_ref[n * FOX_WIDTH:(n + 1) * FOX_WIDTH, :])
        xv = x_ref[...]
        _, r = _rms(xv, g_ref[...])
        dx, dg = _rms_bwd(xv, g_ref[...], r, dh)
        dx_ref[...] = dx1_ref[...] + dx
        dg_ref[...] += dg

    row = lambda n: pl.BlockSpec((tm, n), lambda i: (i, 0))
    full = lambda a: pl.BlockSpec(a.shape, lambda i: (0,) * a.ndim)
    vec = pl.BlockSpec((1, D_MODEL), lambda i: (0, 0))
    return _hosted_call(
        "in_proj_bwd", main, (t // tm,), [x, g, dx1, dfq, dfk, dfv, drest, w_qkv, w_rest],
        [row(D_MODEL), full(g), row(D_MODEL), row(FOX_WIDTH), row(FOX_WIDTH), row(FOX_WIDTH), row(REST_COLS),
         full(w_qkv), full(w_rest)],
        (jax.ShapeDtypeStruct((t, D_MODEL), F32), jax.ShapeDtypeStruct((1, D_MODEL), F32)), (row(D_MODEL), vec), [], comm)


def _pad_cols(a, n):
    return jnp.pad(a, ((0, 0),) * (a.ndim - 1) + ((0, n - a.shape[-1]),))


def kernel(x, positions, attn_norm_g, w_in, b_forget, q_norm_g, w_uq, kv_norm_g, w_ukv, fox_out_g, mla_out_g, w_o, mlp_norm_g, w_up, w_down, final_norm_g, loss_target, m_attn_norm_g, m_w_in, m_b_forget, m_q_norm_g, m_w_uq, m_kv_norm_g, m_w_ukv, m_fox_out_g, m_mla_out_g, m_w_o, m_mlp_norm_g, m_w_up, m_w_down, m_final_norm_g, v_attn_norm_g, v_w_in, v_b_forget, v_q_norm_g, v_w_uq, v_kv_norm_g, v_w_ukv, v_fox_out_g, v_mla_out_g, v_w_o, v_mlp_norm_g, v_w_up, v_w_down, v_final_norm_g):
    t = x.shape[1]
    tq = _row_tile(t)
    xs = x[0]
    target = loss_target[0]

    mid = [_pad_cols(w_uq[0], LANES).astype(BF16), w_ukv[0].astype(BF16)]
    late = [w_o[0].astype(BF16), w_up[0].astype(BF16), w_down[0].astype(BF16)]
    g_in, = _all_gather([jnp.transpose(w_in[0]).astype(BF16)])
    win = g_in.reshape(IN_COLS, D_MODEL)
    off_ff, off_cq, off_kr = 3 * FOX_WIDTH, 3 * FOX_WIDTH + HEADS, IN_COLS - ROPE
    zeros = lambda n: jnp.zeros((n, D_MODEL), BF16)
    w_qkv = win[:off_ff]
    w_rest = jnp.concatenate([
        win[off_ff:off_cq], zeros(REST_CQ - HEADS), win[off_cq:off_kr],
        zeros(NOPE), win[off_kr:], zeros(LANES - NOPE - ROPE)], axis=0)

    cos, sin = _rope_tables(positions.reshape(t, 1))
    (h1, fq, fk, fv, rest), (wq, wkv) = _in_proj(xs, attn_norm_g, w_qkv, w_rest, comm=_ag_to_all(mid))
    b128 = _pad_cols(b_forget, LANES)
    f2_rows, f2_rep = _forget_cumsum(rest, b128)
    f2_rows = f2_rows.reshape(HEADS, t // tq, 1, tq)
    (fox_o, fox_lse_rows), partly = _attn_fwd(True, fq, fk, fv, f2_rows, comm=_ag_direct(late))
    mq, mk, mkv, cqn, ckvn = _mla_prep(rest, q_norm_g, kv_norm_g, wq, wkv, cos, sin)
    (mla_o, mla_lse_rows), (g_o, g_up, g_down) = _attn_fwd(False, mq, mk, mkv, comm=_ag_forward(partly))
    wo = g_o.reshape(D_MODEL, D_MODEL)
    wdown = g_down.reshape(D_FF, D_MODEL)
    x1, mixed, u, h2, dx2, dx2b, loss8, d_gfin = _out_mlp_fwd(
        xs, fox_o, mla_o, fox_out_g, mla_out_g, wo, mlp_norm_g, g_up, wdown, final_norm_g.reshape(1, D_MODEL), target)
    wup = jnp.transpose(g_up, (1, 0, 2)).reshape(D_MODEL, D_FF)

    du, act, dx1, dx1b, d_gmlp = _mlp_bwd(dx2, u, x1, mlp_norm_g, wup, wdown)
    dw_down = _matmul_tn("dw_down", act, dx2b)
    dw_up = _matmul_tn("dw_up", h2, du, blocks=N_DEV)
    dfox_o, dmla_o, fox_delta_rows, mla_delta_rows, d_gfox, d_gmla = _attn_out_bwd(dx1, fox_o, mla_o, fox_out_g, mla_out_g, wo)
    dw_o = _matmul_tn("dw_o", mixed, dx1b)

    place = jnp.stack([lax.axis_index("c"), 2 * lax.axis_index("x") + lax.axis_index("y")]).astype(jnp.int32)
    names = ("w_in", "w_uq", "w_ukv", "w_o", "w_up", "w_down")
    grads_b = [dw_o.reshape(N_DEV, -1, D_MODEL), dw_up, dw_down.reshape(N_DEV, -1, D_MODEL)]
    (dfq, dfk, dfv, d_fq, d_fk), got_b = _attn_bwd(True, fq, fk, fv, dfox_o, fox_lse_rows, fox_delta_rows,
                                                   f2_rep, comm=_rs_to_sibling(grads_b))
    sums_b = [_rs_sibling_sum("rs_sibling_sum_" + nm, g, l, place) for nm, g, l in zip(names[3:], grads_b, got_b)]
    dz, d_b = _forget_bwd(rest, b128, d_fq, d_fk)
    (dmq, dmkv, dmkr), others_b = _attn_bwd(False, mq, mk, mkv, dmla_o, mla_lse_rows, mla_delta_rows,
                                            comm=_rs_to_chips([s[1] for s in sums_b]))
    drest, dqp, dkvb, d_gq, d_gkv = _mla_prep_bwd(dmq, dmkv, dmkr, dz, rest, q_norm_g, kv_norm_g, wq, wkv, cos, sin)
    dw_uq = _matmul_tn("dw_uq", cqn, dqp, blocks=HEADS)
    dw_ukv = _matmul_tn("dw_ukv", ckvn, dkvb, blocks=HEADS)
    dw_in = _dw_in(dfq, dfk, dfv, drest, h1)

    grads_a = [dw_in.reshape(N_DEV, IN_SHARD, D_MODEL), dw_uq, dw_ukv]
    got_a = _comm_call("rs_sibling_exchange", _rs_to_sibling(grads_a))
    sums_a = [_rs_sibling_sum("rs_sibling_sum_" + nm, g, l, place) for nm, g, l in zip(names[:3], grads_a, got_a)]
    (grad_x, d_gattn), others_a = _in_proj_bwd(xs, attn_norm_g, dx1, dfq, dfk, dfv, drest, w_qkv, w_rest,
                                               comm=_rs_to_chips([s[1] for s in sums_a]))
    sums, others = sums_a + sums_b, list(others_a) + list(others_b)
    sharded = (w_in, w_uq, w_ukv, w_o, w_up, w_down)
    moments_m = (m_w_in, m_w_uq, m_w_ukv, m_w_o, m_w_up, m_w_down)
    moments_v = (v_w_in, v_w_uq, v_w_ukv, v_w_o, v_w_up, v_w_down)
    g_in_t = _rs_final_sum("rs_final_sum_w_in", sums[0][0], others[0])
    big = [_adamw_given("adamw_w_in", jnp.transpose(g_in_t), w_in, m_w_in, v_w_in)]
    for a in range(1, len(names)):
        big.append(_adamw_sharded("adamw_" + names[a], sharded[a], moments_m[a], moments_v[a], sums[a][0], others[a]))
    big_g, big_d, big_m, big_v = [[b[k] for b in big] for k in range(4)]

    as_row = lambda a: a.reshape(1, -1)
    small_w = (attn_norm_g, b_forget, q_norm_g, kv_norm_g, fox_out_g, mla_out_g, mlp_norm_g, final_norm_g)
    small_m = (m_attn_norm_g, m_b_forget, m_q_norm_g, m_kv_norm_g, m_fox_out_g, m_mla_out_g, m_mlp_norm_g, m_final_norm_g)
    small_v = (v_attn_norm_g, v_b_forget, v_q_norm_g, v_kv_norm_g, v_fox_out_g, v_mla_out_g, v_mlp_norm_g, v_final_norm_g)
    total = _small_all_reduce([d_gattn, d_b, d_gq, d_gkv, d_gfox, d_gmla, d_gmlp, d_gfin], loss8)
    small = _adamw_small(total, [as_row(a) for a in small_w], [as_row(a) for a in small_m], [as_row(a) for a in small_v])
    loss = small[0].reshape(())
    s_g, s_d, s_m, s_v = [[small[1 + 4 * r + k].reshape(small_w[r].shape) for r in range(len(small_w))] for k in range(4)]

    def ordered(small_, bigs):
        ga, bf, gq_, gkv_, gfo, gml, gmlp_, gfin_ = small_
        bin_, buq, bukv, bo, bup, bdown = bigs
        return [ga, bin_, bf, gq_, buq, gkv_, bukv, gfo, gml, bo, gmlp_, bup, bdown, gfin_]

    return (loss, grad_x[None], *ordered(s_g, big_g), *ordered(s_d, big_d), *ordered(s_m, big_m), *ordered(s_v, big_v))
```

```python
import math
from typing import Callable, NamedTuple

import numpy as np
import jax
import jax.numpy as jnp
from jax import lax
from jax.experimental import pallas as pl
from jax.experimental.pallas import tpu as pltpu

F32 = jnp.float32
BF16 = jnp.bfloat16
MESH = pl.DeviceIdType.MESH

D_MODEL = 1024
HEADS = 8
HEAD_DIM = 64
FOX_WIDTH = 512
MLA_WIDTH = 512
NOPE = 64
ROPE = 32
QK_DIM = 96
Q_RANK = 384
KV_RANK = 256
D_FF = 4096
IN_COLS = 2216
ROPE_THETA = 10000.0
EPS = 1e-6
FOX_SCALE = 1.0 / math.sqrt(HEAD_DIM)
MLA_SCALE = 1.0 / math.sqrt(QK_DIM)
ADAM_LR = 0.001
ADAM_B1 = 0.9
ADAM_B2 = 0.999
ADAM_EPS = 1e-08
ADAM_WD = 0.01
ADAM_STEP = 10

N_DEV = 8
LANES = 128
REST_COLS = 896
REST_CQ = LANES
REST_CKV = REST_CQ + Q_RANK
REST_KR = REST_CKV + KV_RANK
LOG2E = 1.4426950408889634
LN2 = 0.6931471805599453
FOX_Q_FACTOR = FOX_SCALE * LOG2E
MLA_Q_FACTOR = MLA_SCALE * LOG2E
VMEM_LIMIT = 56 * 1024 * 1024

IN_SHARD = IN_COLS // N_DEV
SMALL_SIZES = (1024, 8, 384, 256, 512, 512, 1024, 1024)
SMALL_ROWS = 16
LOSS_ROW = len(SMALL_SIZES)


def _cparams(*sem):
    return pltpu.CompilerParams(dimension_semantics=sem or None, vmem_limit_bytes=VMEM_LIMIT)


def _row_tile(t):
    return 512 if t >= 2048 else (256 if t >= 512 else 128)


def _dot(a, b):
    return jnp.dot(a, b, preferred_element_type=F32)


def _dot_nt(a, b):
    return lax.dot_general(a, b, (((1,), (1,)), ((), ())), preferred_element_type=F32)


def _dot_tn(a, b):
    return lax.dot_general(a, b, (((0,), (0,)), ((), ())), preferred_element_type=F32)


def _rms(x, g):
    r = lax.rsqrt(jnp.mean(x * x, axis=-1, keepdims=True) + EPS)
    return x * r * g, r


def _rms_bwd(x, g, r, dy):
    xh = x * r
    gdy = dy * g
    dx = r * (gdy - xh * jnp.mean(gdy * xh, axis=-1, keepdims=True))
    return dx, jnp.sum(dy * xh, axis=0, keepdims=True)


def _lane():
    return lax.broadcasted_iota(jnp.int32, (1, LANES), 1)


def _rot(x):
    lane = _lane()
    half = NOPE + ROPE // 2
    first = jnp.logical_and(lane >= NOPE, lane < half)
    second = jnp.logical_and(lane >= half, lane < NOPE + ROPE)
    return jnp.where(first, -pltpu.roll(x, LANES - ROPE // 2, 1), jnp.where(second, pltpu.roll(x, ROPE // 2, 1), 0.0))


def _rope(x, cos, sin):
    return x * cos + _rot(x) * sin


def _rope_bwd(dy, cos, sin):
    return dy * cos - _rot(dy * sin)


def _remote(src, dst, send_sem, recv_sem, to):
    return pltpu.make_async_remote_copy(src_ref=src, dst_ref=dst, send_sem=send_sem, recv_sem=recv_sem,
                                        device_id=to, device_id_type=MESH)


def _hbm_specs(n):
    return [pl.BlockSpec(memory_space=pl.ANY)] * n


def _all_gather(blocks):
    n = len(blocks)

    def body(*refs):
        x_refs, out_refs = refs[:n], refs[n:2 * n]
        send_sems, recv_sems, local_sems = refs[2 * n:]
        x, y, c = lax.axis_index("x"), lax.axis_index("y"), lax.axis_index("c")
        me, sibling = (x, y, c), (x, y, 1 - c)
        chips = [(1 - x, y), (x, 1 - y), (1 - x, 1 - y)]

        def slot(a, px, py, pc):
            return out_refs[a].at[4 * px + 2 * py + pc]

        def copy(a, k, blk, to, src=None):
            return _remote(slot(a, *blk) if src is None else src, slot(a, *blk),
                           send_sems.at[7 * a + k], recv_sems.at[7 * a + k], to)

        mine = [pltpu.make_async_copy(x_refs[a], slot(a, *me), local_sems.at[a]) for a in range(n)]
        first, passed = [], []
        for a in range(n):
            mine[a].start()
            first.append(copy(a, 0, me, sibling, src=x_refs[a]))
            first += [copy(a, 1 + j, me, (*chip, c), src=x_refs[a]) for j, chip in enumerate(chips)]
        for cp in first:
            cp.start()
        for a in range(n):
            for j, chip in enumerate(chips):
                copy(a, 1 + j, (*chip, c), me).wait_recv()
                passed.append(copy(a, 4 + j, (*chip, c), sibling))
                passed[-1].start()
        for a in range(n):
            copy(a, 0, sibling, me).wait_recv()
            for j, chip in enumerate(chips):
                copy(a, 4 + j, (*chip, 1 - c), me).wait_recv()
        for cp in first + passed:
            cp.wait_send()
        for cp in mine:
            cp.wait()

    return pl.pallas_call(
        body, name="all_gather_weights",
        out_shape=[jax.ShapeDtypeStruct((N_DEV,) + b.shape, b.dtype) for b in blocks],
        in_specs=_hbm_specs(n), out_specs=_hbm_specs(n),
        scratch_shapes=[pltpu.SemaphoreType.DMA((7 * n,)), pltpu.SemaphoreType.DMA((7 * n,)), pltpu.SemaphoreType.DMA((n,))],
    )(*blocks)


def _symmetric_comm(inputs, out_shape, aliases, per_array, copies):
    def start(in_refs, out_refs, sems):
        for cp in copies(in_refs, out_refs, *sems):
            cp.start()

    def finish(in_refs, out_refs, sems):
        for cp in copies(in_refs, out_refs, *sems):
            cp.wait()

    n_sems = per_array * len(inputs)
    return _Comm(tuple(inputs), tuple(out_shape), aliases,
                 (pltpu.SemaphoreType.DMA((n_sems,)), pltpu.SemaphoreType.DMA((n_sems,))), start, finish)


def _ag_direct(shards):
    def copies(in_refs, out_refs, send_sems, recv_sems):
        x, y, c = lax.axis_index("x"), lax.axis_index("y"), lax.axis_index("c")
        peers = [(x, y, 1 - c), (1 - x, y, c), (x, 1 - y, c), (1 - x, 1 - y, c)]
        cps = []
        for a in range(len(shards)):
            mine = out_refs[a].at[4 * x + 2 * y + c]
            cps.append(pltpu.make_async_copy(in_refs[a], mine, send_sems.at[5 * a]))
            cps += [_remote(in_refs[a], mine, send_sems.at[5 * a + k], recv_sems.at[5 * a + k], peer)
                    for k, peer in enumerate(peers, start=1)]
        return cps

    return _symmetric_comm(shards, [jax.ShapeDtypeStruct((N_DEV,) + s.shape, s.dtype) for s in shards], {}, 5, copies)


def _ag_to_all(shards):
    def copies(in_refs, out_refs, send_sems, recv_sems):
        x, y, c = lax.axis_index("x"), lax.axis_index("y"), lax.axis_index("c")
        cps = []
        for a in range(len(shards)):
            mine = out_refs[a].at[4 * x + 2 * y + c]
            cps.append(pltpu.make_async_copy(in_refs[a], mine, send_sems.at[N_DEV * a]))
            for k in range(1, N_DEV):
                peer = (x ^ (k >> 2), y ^ ((k >> 1) & 1), c ^ (k & 1))
                cps.append(_remote(in_refs[a], mine, send_sems.at[N_DEV * a + k], recv_sems.at[N_DEV * a + k], peer))
        return cps

    return _symmetric_comm(shards, [jax.ShapeDtypeStruct((N_DEV,) + s.shape, s.dtype) for s in shards], {}, N_DEV, copies)


def _ag_forward(gathered):
    def copies(in_refs, out_refs, send_sems, recv_sems):
        x, y, c = lax.axis_index("x"), lax.axis_index("y"), lax.axis_index("c")
        chips = [(1 - x, y), (x, 1 - y), (1 - x, 1 - y)]
        return [_remote(out_refs[a].at[4 * cx + 2 * cy + c], out_refs[a].at[4 * cx + 2 * cy + c],
                        send_sems.at[3 * a + j], recv_sems.at[3 * a + j], (x, y, 1 - c))
                for a in range(len(gathered)) for j, (cx, cy) in enumerate(chips)]

    shapes = [jax.ShapeDtypeStruct(g.shape, g.dtype) for g in gathered]
    return _symmetric_comm(gathered, shapes, {a: a for a in range(len(gathered))}, 3, copies)


def _rs_to_sibling(grads):
    def copies(in_refs, out_refs, send_sems, recv_sems):
        x, y, c = lax.axis_index("x"), lax.axis_index("y"), lax.axis_index("c")
        return [_remote(in_refs[a].at[2 * q + 1 - c], out_refs[a].at[q], send_sems.at[4 * a + q], recv_sems.at[4 * a + q], (x, y, 1 - c))
                for a in range(len(grads)) for q in range(4)]

    return _symmetric_comm(grads, [jax.ShapeDtypeStruct((4,) + g.shape[1:], g.dtype) for g in grads], {}, 4, copies)


def _rs_to_chips(parts):
    def copies(in_refs, out_refs, send_sems, recv_sems):
        x, y, c = lax.axis_index("x"), lax.axis_index("y"), lax.axis_index("c")
        chips = [(1 - x, y), (x, 1 - y), (1 - x, 1 - y)]
        return [_remote(in_refs[a].at[2 * cx + cy], out_refs[a].at[k], send_sems.at[3 * a + k], recv_sems.at[3 * a + k], (cx, cy, c))
                for a in range(len(parts)) for k, (cx, cy) in enumerate(chips)]

    return _symmetric_comm(parts, [jax.ShapeDtypeStruct((3,) + p.shape[1:], p.dtype) for p in parts], {}, 3, copies)


def _comm_call(name, comm):
    n_in, n_out = len(comm.inputs), len(comm.out_shape)

    def body(*refs):
        ins, outs, sems = refs[:n_in], refs[n_in:n_in + n_out], refs[n_in + n_out:]
        comm.start(ins, outs, sems)
        comm.finish(ins, outs, sems)

    return pl.pallas_call(
        body, name=name, out_shape=list(comm.out_shape), in_specs=_hbm_specs(n_in), out_specs=_hbm_specs(n_out),
        scratch_shapes=list(comm.scratch), input_output_aliases=dict(comm.aliases),
    )(*comm.inputs)


def _small_all_reduce(parts, loss8):
    n = len(parts)

    def body(*refs):
        p_refs, loss_ref, out_ref, pack, land, send_sems, recv_sems = refs[:n], *refs[n:]
        x, y, c = lax.axis_index("x"), lax.axis_index("y"), lax.axis_index("c")
        me = 4 * x + 2 * y + c
        pack[...] = jnp.zeros_like(pack)
        for r, ref in enumerate(p_refs):
            pack[r:r + 1, 0:ref.shape[1]] = ref[...]
        pack[LOSS_ROW:LOSS_ROW + 1, 0:LANES] = loss_ref[0:1, :]
        land[me] = pack[...]
        cps = []
        for k in range(1, N_DEV):
            peer = (x ^ (k >> 2), y ^ ((k >> 1) & 1), c ^ (k & 1))
            cps.append(_remote(pack, land.at[me], send_sems.at[k - 1], recv_sems.at[k - 1], peer))
        for cp in cps:
            cp.start()
        for cp in cps:
            cp.wait()
        acc = land[0]
        for d in range(1, N_DEV):
            acc = acc + land[d]
        out_ref[...] = acc

    vmem = pl.BlockSpec(memory_space=pltpu.VMEM)
    return pl.pallas_call(
        body, name="small_all_reduce",
        out_shape=jax.ShapeDtypeStruct((SMALL_ROWS, D_MODEL), F32),
        in_specs=[vmem] * (n + 1), out_specs=vmem,
        scratch_shapes=[pltpu.VMEM((SMALL_ROWS, D_MODEL), F32), pltpu.VMEM((N_DEV, SMALL_ROWS, D_MODEL), F32),
                        pltpu.SemaphoreType.DMA((N_DEV - 1,)), pltpu.SemaphoreType.DMA((N_DEV - 1,))],
    )(*parts, loss8)


def _rs_sibling_sum(name, grad, got, place):
    _, rows, cols = grad.shape

    def body(place_ref, g_ref, l_ref, own_ref, b_ref):
        s = g_ref[...] + l_ref[...]
        b_ref[...] = s.astype(BF16)

        @pl.when(pl.program_id(0) == place_ref[1])
        def _():
            own_ref[...] = s

    by_chip = pl.BlockSpec((None, rows, cols), lambda q, place_ref: (q, 0, 0))
    return pl.pallas_call(
        body, name=name,
        grid_spec=pltpu.PrefetchScalarGridSpec(
            num_scalar_prefetch=1, grid=(4,),
            in_specs=[pl.BlockSpec((None, rows, cols), lambda q, place_ref: (2 * q + place_ref[0], 0, 0)), by_chip],
            out_specs=[pl.BlockSpec((rows, cols), lambda q, place_ref: (0, 0)), by_chip]),
        out_shape=(jax.ShapeDtypeStruct((rows, cols), F32), jax.ShapeDtypeStruct((4, rows, cols), BF16)),
        compiler_params=_cparams("arbitrary"),
    )(place, grad, got)


def _adamw_math(w, g, m, v):
    m2 = ADAM_B1 * m + (1.0 - ADAM_B1) * g
    v2 = ADAM_B2 * v + (1.0 - ADAM_B2) * (g * g)
    m_hat = m2 / (1.0 - ADAM_B1 ** ADAM_STEP)
    v_hat = v2 / (1.0 - ADAM_B2 ** ADAM_STEP)
    delta = -ADAM_LR * (m_hat / (jnp.sqrt(v_hat) + ADAM_EPS) + ADAM_WD * w)
    return delta, m2, v2


def _update_tile(rows):
    return 256 if rows % 256 == 0 else rows


def _rs_final_sum(name, own, got):
    def body(o_ref, r_ref, g_out):
        g = o_ref[...]
        for k in range(3):
            g = g + r_ref[k].astype(F32)
        g_out[...] = g

    return pl.pallas_call(body, name=name, out_shape=jax.ShapeDtypeStruct(own.shape, F32))(own, got)


def _adamw_sharded(name, w, m, v, own, got):
    _, rows, cols = w.shape
    tr = _update_tile(rows)

    def body(o_ref, r_ref, w_ref, m_ref, v_ref, g_out, d_out, m_out, v_out):
        g = o_ref[:, 0:cols]
        for k in range(3):
            g = g + r_ref[k, :, 0:cols].astype(F32)
        d, m2, v2 = _adamw_math(w_ref[0], g, m_ref[0], v_ref[0])
        g_out[0] = g
        d_out[0] = d
        m_out[0] = m2
        v_out[0] = v2

    mine = pl.BlockSpec((1, tr, cols), lambda i: (0, i, 0))
    shp = jax.ShapeDtypeStruct(w.shape, F32)
    wide = own.shape[1]
    return pl.pallas_call(
        body, name=name, grid=(rows // tr,), out_shape=(shp,) * 4,
        in_specs=[pl.BlockSpec((tr, wide), lambda i: (i, 0)), pl.BlockSpec((3, tr, wide), lambda i: (0, i, 0)),
                  mine, mine, mine],
        out_specs=[mine] * 4,
        compiler_params=_cparams("parallel"),
    )(own, got, w, m, v)


def _adamw_given(name, g, w, m, v):
    _, rows, cols = w.shape
    tr = _update_tile(rows)

    def body(g_ref, w_ref, m_ref, v_ref, g_out, d_out, m_out, v_out):
        g = g_ref[...]
        d, m2, v2 = _adamw_math(w_ref[0], g, m_ref[0], v_ref[0])
        g_out[0] = g
        d_out[0] = d
        m_out[0] = m2
        v_out[0] = v2

    own = pl.BlockSpec((1, tr, cols), lambda i: (0, i, 0))
    shp = jax.ShapeDtypeStruct(w.shape, F32)
    return pl.pallas_call(
        body, name=name, grid=(rows // tr,), out_shape=(shp,) * 4,
        in_specs=[pl.BlockSpec((tr, cols), lambda i: (i, 0)), own, own, own], out_specs=[own] * 4,
        compiler_params=_cparams("parallel"),
    )(g, w, m, v)


def _adamw_small(total, ws, ms, vs):
    n = len(ws)

    def body(*refs):
        t_ref = refs[0]
        w_refs, m_refs, v_refs = refs[1:1 + n], refs[1 + n:1 + 2 * n], refs[1 + 2 * n:1 + 3 * n]
        outs = refs[1 + 3 * n:]
        outs[0][...] = t_ref[LOSS_ROW:LOSS_ROW + 1, 0:1]
        for r in range(n):
            g = t_ref[r:r + 1, 0:w_refs[r].shape[1]]
            d, m2, v2 = _adamw_math(w_refs[r][...], g, m_refs[r][...], v_refs[r][...])
            for k, val in enumerate((g, d, m2, v2)):
                outs[1 + 4 * r + k][...] = val

    vmem = pl.BlockSpec(memory_space=pltpu.VMEM)
    out_shape = [jax.ShapeDtypeStruct((1, 1), F32)]
    for w in ws:
        out_shape += [jax.ShapeDtypeStruct(w.shape, F32)] * 4
    return pl.pallas_call(
        body, name="adamw_small", out_shape=out_shape,
        in_specs=[vmem] * (1 + 3 * n), out_specs=[vmem] * len(out_shape),
    )(total, *ws, *ms, *vs)


def _rope_tables(pos_col):
    t = pos_col.shape[0]
    inv = (np.float32(ROPE_THETA) ** (-np.arange(0, ROPE, 2, dtype=np.float32) / np.float32(ROPE))).astype(np.float32)
    freq = np.zeros((1, LANES), np.float32)
    freq[0, NOPE:NOPE + ROPE // 2] = inv
    freq[0, NOPE + ROPE // 2:NOPE + ROPE] = inv
    tm = _row_tile(t)

    def body(p_ref, f_ref, c_ref, s_ref):
        ang = p_ref[...].astype(F32) * f_ref[...]
        c_ref[...] = jnp.cos(ang)
        s_ref[...] = jnp.sin(ang)

    shp = jax.ShapeDtypeStruct((t, LANES), F32)
    return pl.pallas_call(
        body, name="rope_tables", grid=(t // tm,), out_shape=(shp, shp),
        in_specs=[pl.BlockSpec((tm, 1), lambda i: (i, 0)), pl.BlockSpec((1, LANES), lambda i: (0, 0))],
        out_specs=(pl.BlockSpec((tm, LANES), lambda i: (i, 0)),) * 2,
        compiler_params=_cparams("parallel"),
    )(pos_col, jnp.asarray(freq))


def _in_proj(x, g, w_qkv, w_rest, comm=None):
    t = x.shape[0]
    tm = _row_tile(t)

    def main(ins, outs, scr):
        x_ref, g_ref, wq_ref, wr_ref = ins
        h_ref, fq_ref, fk_ref, fv_ref, r_ref = outs
        h, _ = _rms(x_ref[...], g_ref[...])
        hb = h.astype(BF16)
        h_ref[...] = hb
        for n, (ref, factor) in enumerate(((fq_ref, FOX_Q_FACTOR), (fk_ref, None), (fv_ref, None))):
            part = _dot_nt(hb, wq_ref[n * FOX_WIDTH:(n + 1) * FOX_WIDTH, :])
            ref[...] = (part if factor is None else part * factor).astype(BF16)
        r_ref[...] = _dot_nt(hb, wr_ref[...])

    row = lambda n: pl.BlockSpec((tm, n), lambda i: (i, 0))
    full = lambda a: pl.BlockSpec(a.shape, lambda i: (0,) * a.ndim)
    return _hosted_call(
        "in_proj", main, (t // tm,), [x, g, w_qkv, w_rest], [row(D_MODEL), full(g), full(w_qkv), full(w_rest)],
        (jax.ShapeDtypeStruct((t, D_MODEL), BF16),) + (jax.ShapeDtypeStruct((t, FOX_WIDTH), BF16),) * 3
        + (jax.ShapeDtypeStruct((t, REST_COLS), F32),),
        (row(D_MODEL), row(FOX_WIDTH), row(FOX_WIDTH), row(FOX_WIDTH), row(REST_COLS)), [], comm)


def _log_sigmoid(z):
    return jnp.minimum(z, 0.0) - jnp.log(1.0 + jnp.exp(-jnp.abs(z)))


def _split3(v):
    hi = v.astype(BF16)
    r1 = v - hi.astype(F32)
    mid = r1.astype(BF16)
    lo = (r1 - mid.astype(F32)).astype(BF16)
    return hi, mid, lo


def _scan_tile(t):
    return 512 if t >= 2048 else (256 if t >= 256 else t)


def _forget_cumsum(rest, b128):
    t = rest.shape[0]
    tb = _scan_tile(t)

    def body(r_ref, b_ref, row_ref, rep_ref, f_sc, carry):
        @pl.when(pl.program_id(0) == 0)
        def _():
            carry[...] = jnp.zeros_like(carry)
        lf = _log_sigmoid(r_ref[...] + b_ref[...])
        tri = (lax.broadcasted_iota(jnp.int32, (tb, tb), 0) >= lax.broadcasted_iota(jnp.int32, (tb, tb), 1)).astype(BF16)
        hi, mid, lo = _split3(lf)
        f_sc[...] = (_dot(tri, hi) + _dot(tri, mid)) + _dot(tri, lo) + carry[...]
        carry[...] = f_sc[tb - 1:tb, :]
        f2 = f_sc[...] * LOG2E
        row_ref[...] = jnp.transpose(f2)[0:HEADS, :]
        lane = _lane()
        for h in range(HEADS):
            col = jnp.sum(jnp.where(lane == h, f2, 0.0), axis=1, keepdims=True)
            rep_ref[h] = jnp.broadcast_to(col, (tb, LANES))

    return pl.pallas_call(
        body, name="forget_cumsum", grid=(t // tb,),
        out_shape=(jax.ShapeDtypeStruct((HEADS, t), F32), jax.ShapeDtypeStruct((HEADS, t, LANES), F32)),
        in_specs=[pl.BlockSpec((tb, LANES), lambda i: (i, 0)), pl.BlockSpec((1, LANES), lambda i: (0, 0))],
        out_specs=(pl.BlockSpec((HEADS, tb), lambda i: (0, i)), pl.BlockSpec((HEADS, tb, LANES), lambda i: (0, i, 0))),
        scratch_shapes=[pltpu.VMEM((tb, LANES), F32), pltpu.VMEM((1, LANES), F32)],
        compiler_params=_cparams("arbitrary"),
    )(rest, b128)


def _forget_bwd(rest, b128, d_fq, d_fk):
    t = rest.shape[0]
    tb = _scan_tile(t)
    nb = t // tb

    def body(r_ref, b_ref, dfq_ref, dfk_ref, dz_ref, db_ref, carry):
        @pl.when(pl.program_id(0) == 0)
        def _():
            carry[...] = jnp.zeros_like(carry)
            db_ref[...] = jnp.zeros_like(db_ref)
        tri = (lax.broadcasted_iota(jnp.int32, (tb, tb), 0) <= lax.broadcasted_iota(jnp.int32, (tb, tb), 1)).astype(BF16)
        lane = _lane()
        df = jnp.zeros((tb, LANES), F32)
        for h in range(HEADS):
            df = df + jnp.where(lane == h, dfq_ref[h] + dfk_ref[h], 0.0)
        hi, mid, lo = _split3(df)
        dlf = (_dot(tri, hi) + _dot(tri, mid)) + _dot(tri, lo) + carry[...]
        z = r_ref[...] + b_ref[...]
        dz = dlf / (1.0 + jnp.exp(z))
        dz_ref[...] = dz
        db_ref[...] += jnp.sum(dz, axis=0, keepdims=True)
        carry[...] = carry[...] + jnp.sum(df, axis=0, keepdims=True)

    rev = lambda i: (nb - 1 - i, 0)
    rev3 = pl.BlockSpec((HEADS, tb, LANES), lambda i: (0, nb - 1 - i, 0))
    return pl.pallas_call(
        body, name="forget_bwd", grid=(nb,),
        out_shape=(jax.ShapeDtypeStruct((t, LANES), F32), jax.ShapeDtypeStruct((1, LANES), F32)),
        in_specs=[pl.BlockSpec((tb, LANES), rev), pl.BlockSpec((1, LANES), lambda i: (0, 0)), rev3, rev3],
        out_specs=(pl.BlockSpec((tb, LANES), rev), pl.BlockSpec((1, LANES), lambda i: (0, 0))),
        scratch_shapes=[pltpu.VMEM((1, LANES), F32)],
        compiler_params=_cparams("arbitrary"),
    )(rest, b128, d_fq, d_fk)


def _mla_prep(rest, gq, gkv, wq, wkv, cos, sin):
    t = rest.shape[0]
    tm = _row_tile(t)

    def body(r_ref, gq_ref, gkv_ref, wq_ref, wkv_ref, c_ref, s_ref, q_ref, k_ref, kv_ref, cq_ref, ckv_ref):
        cos_, sin_ = c_ref[...], s_ref[...]
        cq, _ = _rms(r_ref[:, REST_CQ:REST_CKV], gq_ref[...])
        ckv, _ = _rms(r_ref[:, REST_CKV:REST_KR], gkv_ref[...])
        cqb, ckvb = cq.astype(BF16), ckv.astype(BF16)
        cq_ref[...] = cqb
        ckv_ref[...] = ckvb
        k_rope = _rope(r_ref[:, REST_KR:REST_COLS], cos_, sin_)
        lo = _lane() < NOPE
        for h in range(HEADS):
            q_ref[h] = (_rope(_dot(cqb, wq_ref[h]), cos_, sin_) * MLA_Q_FACTOR).astype(BF16)
            kv = _dot(ckvb, wkv_ref[h])
            kv_ref[h] = kv.astype(BF16)
            k_ref[h] = (jnp.where(lo, kv, 0.0) + k_rope).astype(BF16)

    row = lambda n: pl.BlockSpec((tm, n), lambda i: (i, 0))
    full = lambda a: pl.BlockSpec(a.shape, lambda i: (0,) * a.ndim)
    heads = pl.BlockSpec((HEADS, tm, LANES), lambda i: (0, i, 0))
    hshape = jax.ShapeDtypeStruct((HEADS, t, LANES), BF16)
    return pl.pallas_call(
        body, name="mla_prep", grid=(t // tm,),
        out_shape=(hshape, hshape, hshape, jax.ShapeDtypeStruct((t, Q_RANK), BF16), jax.ShapeDtypeStruct((t, KV_RANK), BF16)),
        in_specs=[row(REST_COLS), full(gq), full(gkv), full(wq), full(wkv), row(LANES), row(LANES)],
        out_specs=(heads, heads, heads, row(Q_RANK), row(KV_RANK)),
        compiler_params=_cparams("parallel"),
    )(rest, gq, gkv, wq, wkv, cos, sin)


def _tile_lanes(x, n):
    return jnp.tile(x, (1, n)) if n > 1 else x


class _Comm(NamedTuple):
    inputs: tuple
    out_shape: tuple
    aliases: dict
    scratch: tuple
    start: Callable
    finish: Callable


def _hosted_call(name, main, grid, args, in_specs, out_shape, out_specs, scratch, comm):
    n_in, n_out, n_scr = len(args), len(out_shape), len(scratch)
    c_in = list(comm.inputs) if comm else []
    c_out = list(comm.out_shape) if comm else []

    def at_step(which):
        hit = pl.program_id(0) == which[0]
        for axis in range(1, len(grid)):
            hit = jnp.logical_and(hit, pl.program_id(axis) == which[axis])
        return hit

    def body(*refs):
        bounds = [0, n_in, len(c_in), n_out, len(c_out), n_scr]
        starts = [sum(bounds[:k + 1]) for k in range(len(bounds))]
        ins, cins, outs, couts, scr = [refs[a:b] for a, b in zip(starts[:-1], starts[1:])]
        sems = refs[starts[-1]:]
        if comm:
            @pl.when(at_step([0] * len(grid)))
            def _():
                comm.start(cins, couts, sems)
        main(ins, outs, scr)
        if comm:
            @pl.when(at_step([n - 1 for n in grid]))
            def _():
                comm.finish(cins, couts, sems)

    res = pl.pallas_call(
        body, name=name, grid=grid,
        out_shape=list(out_shape) + c_out,
        in_specs=list(in_specs) + _hbm_specs(len(c_in)),
        out_specs=list(out_specs) + _hbm_specs(len(c_out)),
        scratch_shapes=list(scratch) + (list(comm.scratch) if comm else []),
        input_output_aliases={n_in + i: n_out + o for i, o in comm.aliases.items()} if comm else {},
        compiler_params=_cparams(*(["arbitrary"] * len(grid))),
    )(*args, *c_in)
    return res[:n_out], res[n_out:]


def _stat_rows(x):
    return jnp.transpose(x)[0:8, :]


FWD_HEADS = 4


def _attn_fwd(fox, q, k, v, f2_rows=None, comm=None):
    t = q.shape[0] if fox else q.shape[1]
    tq = _row_tile(t)
    nq = t // tq
    nh = FWD_HEADS
    wide = (nh // 2) * LANES

    def main(ins, outs, scr):
        q_ref, k_ref, v_ref = ins[:3]
        fr_ref = ins[3] if fox else None
        o_ref, lset_ref = outs
        m_sc, acc_sc = scr
        i = pl.program_id(1)
        lo = _lane() < HEAD_DIM
        hi = jnp.logical_not(lo)
        zero, one = jnp.zeros((), BF16), jnp.ones((), BF16)
        lanes_of = lambda h: slice((h // 2) * LANES, (h // 2 + 1) * LANES)
        if fox:
            qs = [jnp.where(lo if h % 2 == 0 else hi, q_ref[:, lanes_of(h)], zero) for h in range(nh)]
            sum_lanes = [hi if h % 2 == 0 else lo for h in range(nh)]
        else:
            qs = [q_ref[h] for h in range(nh)]
            sum_lanes = [lo] * nh
        m_sc[...] = jnp.full_like(m_sc, -jnp.inf)
        acc_sc[...] = jnp.zeros_like(acc_sc)

        def block(j, r0, nr, c0, nc, seen_from):
            rows = slice(r0, r0 + nr)
            sl = pl.ds(pl.multiple_of(j * tq + c0, math.gcd(tq, c0) if c0 else tq), nc)
            if seen_from is not None:
                seen = (lax.broadcasted_iota(jnp.int32, (nr, nc), 1)
                        <= lax.broadcasted_iota(jnp.int32, (nr, nc), 0) + seen_from)
            for h in range(nh):
                kj, vj = (k_ref[sl, lanes_of(h)], v_ref[sl, lanes_of(h)]) if fox else (k_ref[h, sl, :], v_ref[h, sl, :])
                s = _dot_nt(qs[h][rows], kj)
                if fox:
                    s = s - fr_ref[h, j, :, c0:c0 + nc]
                if seen_from is not None:
                    s = jnp.where(seen, s, -jnp.inf)
                m_prev = m_sc[h, rows]
                m_new = jnp.maximum(m_prev, jnp.max(s, axis=1, keepdims=True))
                p = jnp.exp2((s - _tile_lanes(m_new, nc // LANES)).astype(BF16))
                vj = jnp.where(sum_lanes[h], one, vj)
                acc_sc[h, rows] = jnp.exp2(m_prev - m_new) * acc_sc[h, rows] + _dot(p, vj)
                m_sc[h, rows] = m_new

        def loop_body(j, carry):
            block(j, 0, tq, 0, tq, None)
            return carry

        lax.fori_loop(0, i, loop_body, 0)
        half = tq // 2
        if half % LANES == 0:
            block(i, 0, half, 0, half, 0)
            block(i, half, half, 0, tq, half)
        else:
            block(i, 0, tq, 0, tq, 0)
        res = []
        for h in range(nh):
            acc = acc_sc[h]
            swapped = pltpu.roll(acc, HEAD_DIM, 1)
            res.append(acc / swapped)
            lse2 = m_sc[h] + jnp.log(jnp.where(sum_lanes[h], acc, swapped)) * LOG2E
            lset_ref[h, 0] = _stat_rows(lse2)
        for pr in range(nh // 2):
            even = res[2 * pr] if fox else pltpu.roll(res[2 * pr], HEAD_DIM, 1)
            o_ref[:, pr * LANES:(pr + 1) * LANES] = jnp.where(lo, even, res[2 * pr + 1])

    if fox:
        in_specs = [pl.BlockSpec((tq, wide), lambda g, i: (i, g))] + [pl.BlockSpec((t, wide), lambda g, i: (0, g))] * 2
        in_specs += [pl.BlockSpec((nh, nq, 1, tq), lambda g, i: (g, 0, 0, 0))]
        args = [q, k, v, f2_rows]
    else:
        in_specs = [pl.BlockSpec((nh, tq, LANES), lambda g, i: (g, i, 0))] + [pl.BlockSpec((nh, t, LANES), lambda g, i: (g, 0, 0))] * 2
        args = [q, k, v]
    return _hosted_call(
        "fox_attn_fwd" if fox else "mla_attn_fwd", main, (HEADS // nh, nq), args, in_specs,
        (jax.ShapeDtypeStruct((t, 4 * LANES), F32), jax.ShapeDtypeStruct((HEADS, nq, 8, tq), F32)),
        (pl.BlockSpec((tq, wide), lambda g, i: (i, g)), pl.BlockSpec((nh, 1, 8, tq), lambda g, i: (g, i, 0, 0))),
        [pltpu.VMEM((nh, tq, LANES), F32), pltpu.VMEM((nh, tq, LANES), F32)], comm)


def _head_do(fox, hh, do2, lo):
    if fox:
        return jnp.where(lo if hh == 0 else jnp.logical_not(lo), do2, 0.0)
    return jnp.where(lo, 0.0, pltpu.roll(do2, HEAD_DIM, 1) if hh == 0 else do2)


def _attn_bwd(fox, q, k, v, do, lse_rows, delta_rows, f2_rep=None, comm=None):
    t = q.shape[0] if fox else q.shape[1]
    tq = _row_tile(t)
    nq = t // tq
    scale = FOX_SCALE if fox else MLA_SCALE

    def main(ins, outs, scr):
        if fox:
            q_ref, k_ref, v_ref, f_ref, do_ref, lse_ref, dl_ref = ins
            dq_ref, dk_ref, dv_ref, dfq_ref, dfk_ref = outs
        else:
            q_ref, k_ref, v_ref, do_ref, lse_ref, dl_ref = ins
            dq_ref, dkv_ref, dkr_ref = outs
        dq_sc, dk_sc, dv_sc = scr
        j = pl.program_id(1)
        lane = _lane()
        lo = lane < HEAD_DIM
        hi = jnp.logical_not(lo)
        zero, one = jnp.zeros((), BF16), jnp.ones((), BF16)

        @pl.when(j == 0)
        def _():
            dq_sc[...] = jnp.zeros_like(dq_sc)

        dk_sc[...] = jnp.zeros_like(dk_sc)
        dv_sc[...] = jnp.zeros_like(dv_sc)

        def block(i, r0, nr, c0, nc, masked):
            rows, cols = slice(r0, r0 + nr), slice(c0, c0 + nc)
            sl = pl.ds(pl.multiple_of(i * tq + c0, math.gcd(tq, c0) if c0 else tq), nc)
            do_i = do_ref[sl, :]
            if masked:
                seen = lax.broadcasted_iota(jnp.int32, (nr, nc), 1) >= lax.broadcasted_iota(jnp.int32, (nr, nc), 0)
            for hh in range(2):
                kj = k_ref[rows, :] if fox else k_ref[hh, rows, :]
                vj = v_ref[rows, :] if fox else v_ref[hh, rows, :]
                qi = jnp.where(lo if hh == 0 else hi, q_ref[sl, :], zero) if fox else q_ref[hh, sl, :]
                dob = _head_do(fox, hh, do_i, lo).astype(BF16)
                st = _dot_nt(kj, qi)
                if fox:
                    st = st - _tile_lanes(f_ref[hh, rows, :], nc // LANES)
                if masked:
                    st = jnp.where(seen, st, -jnp.inf)
                pt = jnp.exp2(st - lse_ref[hh, i, 0:1, cols])
                dpt = _dot_nt(vj, dob)
                dst = (pt * (dpt - dl_ref[hh, i, 0:1, cols])).astype(BF16)
                dv_sc[hh, rows] += _dot(pt.astype(BF16), dob)
                if fox:
                    other = hi if hh == 0 else lo
                    qi = jnp.where(other, one, qi)
                    kj = jnp.where(other, one, kj)
                dk_sc[hh, rows] += _dot(dst, qi)
                dq_sc[hh, sl, :] += _dot_tn(dst, kj)

        def loop_body(i, carry):
            block(i, 0, tq, 0, tq, False)
            return carry

        half = tq // 2
        if half % LANES == 0:
            block(j, 0, half, 0, tq, True)
            block(j, half, half, half, half, True)
        else:
            block(j, 0, tq, 0, tq, True)
        lax.fori_loop(j + 1, nq, loop_body, 0)
        if fox:
            dk_ref[...] = (jnp.where(lo, dk_sc[0], dk_sc[1]) * LN2).astype(BF16)
            dv_ref[...] = (dv_sc[0] + dv_sc[1]).astype(BF16)
            for hh in range(2):
                dk = dk_sc[hh]
                dfk_ref[hh] = -jnp.where(hi if hh == 0 else lo, dk, pltpu.roll(dk, HEAD_DIM, 1))
        else:
            rope_lanes = jnp.logical_and(lane >= NOPE, lane < NOPE + ROPE)
            dkr = jnp.zeros((tq, LANES), F32)
            for hh in range(2):
                dk = dk_sc[hh] * LN2
                dkv_ref[hh] = jnp.where(lo, dk, dv_sc[hh])
                dkr = dkr + jnp.where(rope_lanes, dk, 0.0)
            dkr_ref[0] = dkr

        @pl.when(j == nq - 1)
        def _():
            for i in range(nq):
                rows = slice(i * tq, (i + 1) * tq)
                if fox:
                    dq_ref[rows, :] = (jnp.where(lo, dq_sc[0, rows, :], dq_sc[1, rows, :]) * scale).astype(BF16)
                    for hh in range(2):
                        acc = dq_sc[hh, rows, :]
                        dfq_ref[hh, rows, :] = jnp.where(hi if hh == 0 else lo, acc, pltpu.roll(acc, HEAD_DIM, 1))
                else:
                    for hh in range(2):
                        dq_ref[hh, rows, :] = dq_sc[hh, rows, :] * scale

    stat = pl.BlockSpec((2, tq, LANES), lambda p, j: (p, j, 0))
    stat_all = pl.BlockSpec((2, t, LANES), lambda p, j: (p, 0, 0))
    rows4 = pl.BlockSpec((2, nq, 8, tq), lambda p, j: (p, 0, 0, 0))
    pair = pl.BlockSpec((tq, LANES), lambda p, j: (j, p))
    pair_all = pl.BlockSpec((t, LANES), lambda p, j: (0, p))
    if fox:
        in_specs = [pair_all, pair, pair, stat]
        args = [q, k, v, f2_rep]
    else:
        in_specs = [stat_all, stat, stat]
        args = [q, k, v]
    in_specs += [pair_all, rows4, rows4]
    args += [do, lse_rows, delta_rows]
    heads_f32 = jax.ShapeDtypeStruct((HEADS, t, LANES), F32)
    if fox:
        wide = jax.ShapeDtypeStruct((t, 4 * LANES), BF16)
        out_shape = (wide, wide, wide, heads_f32, heads_f32)
        out_specs = (pair_all, pair, pair, stat_all, stat)
    else:
        out_shape = (heads_f32, heads_f32, jax.ShapeDtypeStruct((HEADS // 2, t, LANES), F32))
        out_specs = (stat_all, stat, pl.BlockSpec((1, tq, LANES), lambda p, j: (p, j, 0)))
    acc = pltpu.VMEM((2, tq, LANES), F32)
    return _hosted_call("fox_attn_bwd" if fox else "mla_attn_bwd", main, (HEADS // 2, nq), args, in_specs,
                        out_shape, out_specs, [pltpu.VMEM((2, t, LANES), F32), acc, acc], comm)


def _attn_out(x, fox_o, mla_o, gf, gm, w_o):
    t = x.shape[0]
    tm = _row_tile(t)

    def body(x_ref, f_ref, m_ref, gf_ref, gm_ref, w_ref, x1_ref, mix_ref):
        nf, _ = _rms(f_ref[...], gf_ref[...])
        nm, _ = _rms(m_ref[...], gm_ref[...])
        nfb, nmb = nf.astype(BF16), nm.astype(BF16)
        mix_ref[:, :FOX_WIDTH] = nfb
        mix_ref[:, FOX_WIDTH:] = nmb
        x1_ref[...] = x_ref[...] + _dot(nfb, w_ref[:FOX_WIDTH, :]) + _dot(nmb, w_ref[FOX_WIDTH:, :])

    row = lambda n: pl.BlockSpec((tm, n), lambda i: (i, 0))
    full = lambda a: pl.BlockSpec(a.shape, lambda i: (0,) * a.ndim)
    return pl.pallas_call(
        body, name="attn_out", grid=(t // tm,),
        out_shape=(jax.ShapeDtypeStruct((t, D_MODEL), F32), jax.ShapeDtypeStruct((t, D_MODEL), BF16)),
        in_specs=[row(D_MODEL), row(FOX_WIDTH), row(MLA_WIDTH), full(gf), full(gm), full(w_o)],
        out_specs=(row(D_MODEL), row(D_MODEL)),
        compiler_params=_cparams("parallel"),
    )(x, fox_o, mla_o, gf, gm, w_o)


def _mlp_tile(t):
    return 256 if t >= 2048 else 128


def _resident(a):
    return pl.BlockSpec(a.shape, lambda i: (0,) * a.ndim, pipeline_mode=pl.Buffered(1))


FF_CHUNK = 512


def _mlp_fwd(x1, g_mlp, w_up, w_down, g_fin, target):
    t = x1.shape[0]
    tm = _mlp_tile(t)

    def body(x_ref, g_ref, wu_ref, wd_ref, gf_ref, t_ref, u_ref, h_ref, dx_ref, dxb_ref, loss_ref, dg_ref, a_sc):
        @pl.when(pl.program_id(0) == 0)
        def _():
            loss_ref[...] = jnp.zeros_like(loss_ref)
            dg_ref[...] = jnp.zeros_like(dg_ref)

        x = x_ref[...]
        h, _ = _rms(x, g_ref[...])
        hb = h.astype(BF16)
        h_ref[...] = hb
        for f in range(D_FF // FF_CHUNK):
            sl = slice(f * FF_CHUNK, (f + 1) * FF_CHUNK)
            u = _dot(hb, wu_ref[:, sl])
            u_ref[:, sl] = u
            r = jnp.maximum(u, 0.0)
            a_sc[:, sl] = (r * r).astype(BF16)
        x2 = x + _dot(a_sc[...], wd_ref[...])
        y, r2 = _rms(x2, gf_ref[...])
        err = y - t_ref[...]
        loss_ref[...] += 0.5 * jnp.sum(jnp.mean(err * err, axis=-1, keepdims=True))
        dx, dg = _rms_bwd(x2, gf_ref[...], r2, err * (1.0 / D_MODEL))
        dx_ref[...] = dx
        dxb_ref[...] = dx.astype(BF16)
        dg_ref[...] += dg

    row = lambda n: pl.BlockSpec((tm, n), lambda i: (i, 0))
    vec = pl.BlockSpec((1, D_MODEL), lambda i: (0, 0))
    return pl.pallas_call(
        body, name="mlp_fwd", grid=(t // tm,),
        out_shape=(jax.ShapeDtypeStruct((t, D_FF), F32), jax.ShapeDtypeStruct((t, D_MODEL), BF16),
                   jax.ShapeDtypeStruct((t, D_MODEL), F32), jax.ShapeDtypeStruct((t, D_MODEL), BF16),
                   jax.ShapeDtypeStruct((8, LANES), F32), jax.ShapeDtypeStruct((1, D_MODEL), F32)),
        in_specs=[row(D_MODEL), vec, _resident(w_up), _resident(w_down), vec, row(D_MODEL)],
        out_specs=(row(D_FF), row(D_MODEL), row(D_MODEL), row(D_MODEL), pl.BlockSpec((8, LANES), lambda i: (0, 0)), vec),
        scratch_shapes=[pltpu.VMEM((tm, D_FF), BF16)],
        compiler_params=_cparams("arbitrary"),
    )(x1, g_mlp, w_up, w_down, g_fin, target)


def _mlp_bwd(dx2, u, x1, g_mlp, w_up, w_down):
    t = x1.shape[0]
    tm = _mlp_tile(t)

    def body(dx_ref, u_ref, x_ref, g_ref, wu_ref, wd_ref, du_ref, a_ref, dx1_ref, dx1b_ref, dg_ref):
        @pl.when(pl.program_id(0) == 0)
        def _():
            dg_ref[...] = jnp.zeros_like(dg_ref)

        dx2 = dx_ref[...]
        dxb = dx2.astype(BF16)
        for f in range(D_FF // FF_CHUNK):
            sl = slice(f * FF_CHUNK, (f + 1) * FF_CHUNK)
            r = jnp.maximum(u_ref[:, sl], 0.0)
            a_ref[:, sl] = (r * r).astype(BF16)
            da = _dot_nt(dxb, wd_ref[sl, :])
            du_ref[:, sl] = (da * (2.0 * r)).astype(BF16)
        dh = _dot_nt(du_ref[...], wu_ref[...])
        x = x_ref[...]
        _, r1 = _rms(x, g_ref[...])
        dx, dg = _rms_bwd(x, g_ref[...], r1, dh)
        dx1 = dx2 + dx
        dx1_ref[...] = dx1
        dx1b_ref[...] = dx1.astype(BF16)
        dg_ref[...] += dg

    row = lambda n: pl.BlockSpec((tm, n), lambda i: (i, 0))
    vec = pl.BlockSpec((1, D_MODEL), lambda i: (0, 0))
    return pl.pallas_call(
        body, name="mlp_bwd", grid=(t // tm,),
        out_shape=(jax.ShapeDtypeStruct((t, D_FF), BF16), jax.ShapeDtypeStruct((t, D_FF), BF16),
                   jax.ShapeDtypeStruct((t, D_MODEL), F32), jax.ShapeDtypeStruct((t, D_MODEL), BF16),
                   jax.ShapeDtypeStruct((1, D_MODEL), F32)),
        in_specs=[row(D_MODEL), row(D_FF), row(D_MODEL), vec, _resident(w_up), _resident(w_down)],
        out_specs=(row(D_FF), row(D_FF), row(D_MODEL), row(D_MODEL), vec),
        compiler_params=_cparams("arbitrary"),
    )(dx2, u, x1, g_mlp, w_up, w_down)


def _matmul_tn(name, a, b, blocks=None):
    t, m = a.shape
    n = b.shape[1]
    tk = t if a.dtype == BF16 and b.dtype == BF16 else min(t, 2048)
    steps = t // tk
    bm = m if m <= 1024 else 512
    bn = n if n <= 1024 else 512
    width = bn if blocks is None else n // blocks
    per = bn // width

    def body(a_ref, b_ref, o_ref, acc_sc):
        kk = pl.program_id(2)

        @pl.when(kk == 0)
        def _():
            acc_sc[...] = jnp.zeros_like(acc_sc)

        acc_sc[...] += _dot_tn(a_ref[...].astype(BF16), b_ref[...].astype(BF16))

        @pl.when(kk == steps - 1)
        def _():
            if blocks is None:
                o_ref[...] = acc_sc[...]
            else:
                for s in range(per):
                    o_ref[s] = acc_sc[:, s * width:(s + 1) * width]

    if blocks is None:
        o_spec = pl.BlockSpec((bm, bn), lambda i, j, kk: (i, j))
        o_shape = (m, n)
    else:
        o_spec = pl.BlockSpec((per, bm, width), lambda i, j, kk: (j, i, 0))
        o_shape = (blocks, m, width)
    return pl.pallas_call(
        body, name=name, grid=(m // bm, n // bn, steps),
        out_shape=jax.ShapeDtypeStruct(o_shape, F32),
        in_specs=[pl.BlockSpec((tk, bm), lambda i, j, kk: (kk, i)), pl.BlockSpec((tk, bn), lambda i, j, kk: (kk, j))],
        out_specs=o_spec,
        scratch_shapes=[pltpu.VMEM((bm, bn), F32)],
        compiler_params=_cparams("parallel", "parallel", "arbitrary"),
    )(a, b)


def _dw_in(dfq, dfk, dfv, drest, h1):
    t = h1.shape[0]
    tk = min(t, 1024)
    off_ff = 3 * FOX_WIDTH
    off_cq = off_ff + HEADS
    off_kr = IN_COLS - ROPE

    def body(dq_ref, dk_ref, dv_ref, dr_ref, h_ref, o_ref):
        h = h_ref[...]
        r = _dot_tn(dr_ref[...], h)
        parts = [(slice(n * FOX_WIDTH, (n + 1) * FOX_WIDTH), _dot_tn(ref[...], h)) for n, ref in enumerate((dq_ref, dk_ref, dv_ref))]
        parts += [(slice(off_ff, off_cq), r[0:HEADS]), (slice(off_cq, off_kr), r[REST_CQ:REST_KR]),
                  (slice(off_kr, IN_COLS), r[REST_KR + NOPE:REST_KR + NOPE + ROPE])]

        @pl.when(pl.program_id(0) == 0)
        def _():
            for rows, val in parts:
                o_ref[rows, :] = val

        @pl.when(pl.program_id(0) > 0)
        def _():
            for rows, val in parts:
                o_ref[rows, :] += val

    tok = lambda n: pl.BlockSpec((tk, n), lambda kk: (kk, 0))
    return pl.pallas_call(
        body, name="dw_in", grid=(t // tk,),
        out_shape=jax.ShapeDtypeStruct((IN_COLS, D_MODEL), F32),
        in_specs=[tok(FOX_WIDTH), tok(FOX_WIDTH), tok(FOX_WIDTH), tok(REST_COLS), tok(D_MODEL)],
        out_specs=pl.BlockSpec((IN_COLS, D_MODEL), lambda kk: (0, 0)),
        compiler_params=_cparams("arbitrary"),
    )(dfq, dfk, dfv, drest, h1)


def _attn_out_bwd(dx1, fox_o, mla_o, gf, gm, w_o):
    t = dx1.shape[0]
    tm = _row_tile(t)

    def body(dx_ref, f_ref, m_ref, gf_ref, gm_ref, w_ref, df_ref, dm_ref, dlf_ref, dlm_ref, dgf_ref, dgm_ref):
        @pl.when(pl.program_id(0) == 0)
        def _():
            dgf_ref[...] = jnp.zeros_like(dgf_ref)
            dgm_ref[...] = jnp.zeros_like(dgm_ref)
        dxb = dx_ref[...].astype(BF16)
        lane = lax.broadcasted_iota(jnp.int32, (8, LANES), 1)
        upper = lax.broadcasted_iota(jnp.int32, (8, LANES), 0) < 4
        pick = jnp.where(upper, (lane < HEAD_DIM).astype(F32), (lane >= HEAD_DIM).astype(F32)).astype(BF16)
        for o_ref, g_ref, lo_row, d_ref, dl_ref, dg_ref in ((f_ref, gf_ref, 0, df_ref, dlf_ref, dgf_ref),
                                                             (m_ref, gm_ref, FOX_WIDTH, dm_ref, dlm_ref, dgm_ref)):
            dn = _dot_nt(dxb, w_ref[lo_row:lo_row + FOX_WIDTH, :])
            o = o_ref[...]
            _, r = _rms(o, g_ref[...])
            d, dg = _rms_bwd(o, g_ref[...], r, dn)
            d_ref[...] = d
            dg_ref[...] += dg
            prod = d * o
            for pr in range(HEADS // 2):
                parts = _split3(prod[:, pr * LANES:(pr + 1) * LANES])
                both = (_dot_nt(pick, parts[0]) + _dot_nt(pick, parts[1])) + _dot_nt(pick, parts[2])
                dl_ref[2 * pr, 0] = both
                dl_ref[2 * pr + 1, 0] = pltpu.roll(both, 4, 0)

    row = lambda n: pl.BlockSpec((tm, n), lambda i: (i, 0))
    full = lambda a: pl.BlockSpec(a.shape, lambda i: (0,) * a.ndim)
    vec = pl.BlockSpec((1, FOX_WIDTH), lambda i: (0, 0))
    rows = pl.BlockSpec((HEADS, 1, 8, tm), lambda i: (0, i, 0, 0))
    o_shape = jax.ShapeDtypeStruct((t, FOX_WIDTH), F32)
    g_shape = jax.ShapeDtypeStruct((1, FOX_WIDTH), F32)
    r_shape = jax.ShapeDtypeStruct((HEADS, t // tm, 8, tm), F32)
    return pl.pallas_call(
        body, name="attn_out_bwd", grid=(t // tm,),
        out_shape=(o_shape, o_shape, r_shape, r_shape, g_shape, g_shape),
        in_specs=[row(D_MODEL), row(FOX_WIDTH), row(MLA_WIDTH), full(gf), full(gm), full(w_o)],
        out_specs=(row(FOX_WIDTH), row(MLA_WIDTH), rows, rows, vec, vec),
        compiler_params=_cparams("arbitrary"),
    )(dx1, fox_o, mla_o, gf, gm, w_o)


def _mla_prep_bwd(dq, dkv, dkr, dz, rest, gq, gkv, wq, wkv, cos, sin):
    t = rest.shape[0]
    tm = _row_tile(t)

    def body(dq_ref, dkv_ref, dkr_ref, dz_ref, r_ref, gq_ref, gkv_ref, wq_ref, wkv_ref, c_ref, s_ref,
             dr_ref, dqp_ref, dkvb_ref, dgq_ref, dgkv_ref):
        @pl.when(pl.program_id(0) == 0)
        def _():
            dgq_ref[...] = jnp.zeros_like(dgq_ref)
            dgkv_ref[...] = jnp.zeros_like(dgkv_ref)
        cos_, sin_ = c_ref[...], s_ref[...]
        dcq = jnp.zeros((tm, Q_RANK), F32)
        dckv = jnp.zeros((tm, KV_RANK), F32)
        for h in range(HEADS):
            dqp = _rope_bwd(dq_ref[h], cos_, sin_).astype(BF16)
            dqp_ref[:, h * LANES:(h + 1) * LANES] = dqp
            dcq = dcq + _dot_nt(dqp, wq_ref[h])
            dkvb = dkv_ref[h].astype(BF16)
            dkvb_ref[:, h * LANES:(h + 1) * LANES] = dkvb
            dckv = dckv + _dot_nt(dkvb, wkv_ref[h])
        dkrope = dkr_ref[0]
        for pr in range(1, HEADS // 2):
            dkrope = dkrope + dkr_ref[pr]
        cq = r_ref[:, REST_CQ:REST_CKV]
        _, rq = _rms(cq, gq_ref[...])
        d_cq, dgq = _rms_bwd(cq, gq_ref[...], rq, dcq)
        ckv = r_ref[:, REST_CKV:REST_KR]
        _, rkv = _rms(ckv, gkv_ref[...])
        d_ckv, dgkv = _rms_bwd(ckv, gkv_ref[...], rkv, dckv)
        dgq_ref[...] += dgq
        dgkv_ref[...] += dgkv
        dr_ref[:, 0:REST_CQ] = dz_ref[...].astype(BF16)
        dr_ref[:, REST_CQ:REST_CKV] = d_cq.astype(BF16)
        dr_ref[:, REST_CKV:REST_KR] = d_ckv.astype(BF16)
        dr_ref[:, REST_KR:REST_COLS] = _rope_bwd(dkrope, cos_, sin_).astype(BF16)

    row = lambda n: pl.BlockSpec((tm, n), lambda i: (i, 0))
    full = lambda a: pl.BlockSpec(a.shape, lambda i: (0,) * a.ndim)
    heads = pl.BlockSpec((HEADS, tm, LANES), lambda i: (0, i, 0))
    hshape = jax.ShapeDtypeStruct((t, HEADS * LANES), BF16)
    return pl.pallas_call(
        body, name="mla_prep_bwd", grid=(t // tm,),
        out_shape=(jax.ShapeDtypeStruct((t, REST_COLS), BF16), hshape, hshape,
                   jax.ShapeDtypeStruct((1, Q_RANK), F32), jax.ShapeDtypeStruct((1, KV_RANK), F32)),
        in_specs=[heads, heads, pl.BlockSpec((HEADS // 2, tm, LANES), lambda i: (0, i, 0)), row(LANES), row(REST_COLS),
                  full(gq), full(gkv), full(wq), full(wkv), row(LANES), row(LANES)],
        out_specs=(row(REST_COLS), row(HEADS * LANES), row(HEADS * LANES), pl.BlockSpec((1, Q_RANK), lambda i: (0, 0)),
                   pl.BlockSpec((1, KV_RANK), lambda i: (0, 0))),
        compiler_params=_cparams("arbitrary"),
    )(dq, dkv, dkr, dz, rest, gq, gkv, wq, wkv, cos, sin)


def _in_proj_bwd(x, g, dx1, dfq, dfk, dfv, drest, w_qkv, w_rest, comm=None):
    t = x.shape[0]
    tm = _row_tile(t)

    def main(ins, outs, scr):
        x_ref, g_ref, dx1_ref, dq_ref, dk_ref, dv_ref, dr_ref, wq_ref, wr_ref = ins
        dx_ref, dg_ref = outs

        @pl.when(pl.program_id(0) == 0)
        def _():
            dg_ref[...] = jnp.zeros_like(dg_ref)
        dh = _dot(dr_ref[...], wr_ref[...])
        for n, ref in enumerate((dq_ref, dk_ref, dv_ref)):
            dh = dh + _dot(ref[...], wq_ref[n * FOX_WIDTH:(n + 1) * FOX_WIDTH, :])
        xv = x_ref[...]
        _, r = _rms(xv, g_ref[...])
        dx, dg = _rms_bwd(xv, g_ref[...], r, dh)
        dx_ref[...] = dx1_ref[...] + dx
        dg_ref[...] += dg

    row = lambda n: pl.BlockSpec((tm, n), lambda i: (i, 0))
    full = lambda a: pl.BlockSpec(a.shape, lambda i: (0,) * a.ndim)
    vec = pl.BlockSpec((1, D_MODEL), lambda i: (0, 0))
    return _hosted_call(
        "in_proj_bwd", main, (t // tm,), [x, g, dx1, dfq, dfk, dfv, drest, w_qkv, w_rest],
        [row(D_MODEL), full(g), row(D_MODEL), row(FOX_WIDTH), row(FOX_WIDTH), row(FOX_WIDTH), row(REST_COLS),
         full(w_qkv), full(w_rest)],
        (jax.ShapeDtypeStruct((t, D_MODEL), F32), jax.ShapeDtypeStruct((1, D_MODEL), F32)), (row(D_MODEL), vec), [], comm)


def _pad_cols(a, n):
    return jnp.pad(a, ((0, 0),) * (a.ndim - 1) + ((0, n - a.shape[-1]),))


def kernel(x, positions, attn_norm_g, w_in, b_forget, q_norm_g, w_uq, kv_norm_g, w_ukv, fox_out_g, mla_out_g, w_o, mlp_norm_g, w_up, w_down, final_norm_g, loss_target, m_attn_norm_g, m_w_in, m_b_forget, m_q_norm_g, m_w_uq, m_kv_norm_g, m_w_ukv, m_fox_out_g, m_mla_out_g, m_w_o, m_mlp_norm_g, m_w_up, m_w_down, m_final_norm_g, v_attn_norm_g, v_w_in, v_b_forget, v_q_norm_g, v_w_uq, v_kv_norm_g, v_w_ukv, v_fox_out_g, v_mla_out_g, v_w_o, v_mlp_norm_g, v_w_up, v_w_down, v_final_norm_g):
    t = x.shape[1]
    tq = _row_tile(t)
    xs = x[0]
    target = loss_target[0]

    mid = [_pad_cols(w_uq[0], LANES).astype(BF16), w_ukv[0].astype(BF16)]
    late = [w_o[0].astype(BF16), w_up[0].astype(BF16), w_down[0].astype(BF16)]
    g_in, = _all_gather([jnp.transpose(w_in[0]).astype(BF16)])
    win = g_in.reshape(IN_COLS, D_MODEL)
    off_ff, off_cq, off_kr = 3 * FOX_WIDTH, 3 * FOX_WIDTH + HEADS, IN_COLS - ROPE
    zeros = lambda n: jnp.zeros((n, D_MODEL), BF16)
    w_qkv = win[:off_ff]
    w_rest = jnp.concatenate([
        win[off_ff:off_cq], zeros(REST_CQ - HEADS), win[off_cq:off_kr],
        zeros(NOPE), win[off_kr:], zeros(LANES - NOPE - ROPE)], axis=0)

    cos, sin = _rope_tables(positions.reshape(t, 1))
    (h1, fq, fk, fv, rest), (wq, wkv) = _in_proj(xs, attn_norm_g, w_qkv, w_rest, comm=_ag_to_all(mid))
    b128 = _pad_cols(b_forget, LANES)
    f2_rows, f2_rep = _forget_cumsum(rest, b128)
    f2_rows = f2_rows.reshape(HEADS, t // tq, 1, tq)
    (fox_o, fox_lse_rows), partly = _attn_fwd(True, fq, fk, fv, f2_rows, comm=_ag_direct(late))
    mq, mk, mkv, cqn, ckvn = _mla_prep(rest, q_norm_g, kv_norm_g, wq, wkv, cos, sin)
    (mla_o, mla_lse_rows), (g_o, g_up, g_down) = _attn_fwd(False, mq, mk, mkv, comm=_ag_forward(partly))
    wo = g_o.reshape(D_MODEL, D_MODEL)
    x1, mixed = _attn_out(xs, fox_o, mla_o, fox_out_g, mla_out_g, wo)
    wup = jnp.transpose(g_up, (1, 0, 2)).reshape(D_MODEL, D_FF)
    wdown = g_down.reshape(D_FF, D_MODEL)
    u, h2, dx2, dx2b, loss8, d_gfin = _mlp_fwd(x1, mlp_norm_g, wup, wdown, final_norm_g.reshape(1, D_MODEL), target)

    du, act, dx1, dx1b, d_gmlp = _mlp_bwd(dx2, u, x1, mlp_norm_g, wup, wdown)
    dw_down = _matmul_tn("dw_down", act, dx2b)
    dw_up = _matmul_tn("dw_up", h2, du, blocks=N_DEV)
    dfox_o, dmla_o, fox_delta_rows, mla_delta_rows, d_gfox, d_gmla = _attn_out_bwd(dx1, fox_o, mla_o, fox_out_g, mla_out_g, wo)
    dw_o = _matmul_tn("dw_o", mixed, dx1b)

    place = jnp.stack([lax.axis_index("c"), 2 * lax.axis_index("x") + lax.axis_index("y")]).astype(jnp.int32)
    names = ("w_in", "w_uq", "w_ukv", "w_o", "w_up", "w_down")
    grads_b = [dw_o.reshape(N_DEV, -1, D_MODEL), dw_up, dw_down.reshape(N_DEV, -1, D_MODEL)]
    (dfq, dfk, dfv, d_fq, d_fk), got_b = _attn_bwd(True, fq, fk, fv, dfox_o, fox_lse_rows, fox_delta_rows,
                                                   f2_rep, comm=_rs_to_sibling(grads_b))
    sums_b = [_rs_sibling_sum("rs_sibling_sum_" + nm, g, l, place) for nm, g, l in zip(names[3:], grads_b, got_b)]
    dz, d_b = _forget_bwd(rest, b128, d_fq, d_fk)
    (dmq, dmkv, dmkr), others_b = _attn_bwd(False, mq, mk, mkv, dmla_o, mla_lse_rows, mla_delta_rows,
                                            comm=_rs_to_chips([s[1] for s in sums_b]))
    drest, dqp, dkvb, d_gq, d_gkv = _mla_prep_bwd(dmq, dmkv, dmkr, dz, rest, q_norm_g, kv_norm_g, wq, wkv, cos, sin)
    dw_uq = _matmul_tn("dw_uq", cqn, dqp, blocks=HEADS)
    dw_ukv = _matmul_tn("dw_ukv", ckvn, dkvb, blocks=HEADS)
    dw_in = _dw_in(dfq, dfk, dfv, drest, h1)

    grads_a = [dw_in.reshape(N_DEV, IN_SHARD, D_MODEL), dw_uq, dw_ukv]
    got_a = _comm_call("rs_sibling_exchange", _rs_to_sibling(grads_a))
    sums_a = [_rs_sibling_sum("rs_sibling_sum_" + nm, g, l, place) for nm, g, l in zip(names[:3], grads_a, got_a)]
    (grad_x, d_gattn), others_a = _in_proj_bwd(xs, attn_norm_g, dx1, dfq, dfk, dfv, drest, w_qkv, w_rest,
                                               comm=_rs_to_chips([s[1] for s in sums_a]))
    sums, others = sums_a + sums_b, list(others_a) + list(others_b)
    sharded = (w_in, w_uq, w_ukv, w_o, w_up, w_down)
    moments_m = (m_w_in, m_w_uq, m_w_ukv, m_w_o, m_w_up, m_w_down)
    moments_v = (v_w_in, v_w_uq, v_w_ukv, v_w_o, v_w_up, v_w_down)
    g_in_t = _rs_final_sum("rs_final_sum_w_in", sums[0][0], others[0])
    big = [_adamw_given("adamw_w_in", jnp.transpose(g_in_t), w_in, m_w_in, v_w_in)]
    for a in range(1, len(names)):
        big.append(_adamw_sharded("adamw_" + names[a], sharded[a], moments_m[a], moments_v[a], sums[a][0], others[a]))
    big_g, big_d, big_m, big_v = [[b[k] for b in big] for k in range(4)]

    as_row = lambda a: a.reshape(1, -1)
    small_w = (attn_norm_g, b_forget, q_norm_g, kv_norm_g, fox_out_g, mla_out_g, mlp_norm_g, final_norm_g)
    small_m = (m_attn_norm_g, m_b_forget, m_q_norm_g, m_kv_norm_g, m_fox_out_g, m_mla_out_g, m_mlp_norm_g, m_final_norm_g)
    small_v = (v_attn_norm_g, v_b_forget, v_q_norm_g, v_kv_norm_g, v_fox_out_g, v_mla_out_g, v_mlp_norm_g, v_final_norm_g)
    total = _small_all_reduce([d_gattn, d_b, d_gq, d_gkv, d_gfox, d_gmla, d_gmlp, d_gfin], loss8)
    small = _adamw_small(total, [as_row(a) for a in small_w], [as_row(a) for a in small_m], [as_row(a) for a in small_v])
    loss = small[0].reshape(())
    s_g, s_d, s_m, s_v = [[small[1 + 4 * r + k].reshape(small_w[r].shape) for r in range(len(small_w))] for k in range(4)]

    def ordered(small_, bigs):
        ga, bf, gq_, gkv_, gfo, gml, gmlp_, gfin_ = small_
        bin_, buq, bukv, bo, bup, bdown = bigs
        return [ga, bin_, bf, gq_, buq, gkv_, bukv, gfo, gml, bo, gmlp_, bup, bdown, gfin_]

    return (loss, grad_x[None], *ordered(s_g, big_g), *ordered(s_d, big_d), *ordered(s_m, big_m), *ordered(s_v, big_v))
```

```python
import math
from typing import Callable, NamedTuple

import numpy as np
import jax
import jax.numpy as jnp
from jax import lax
from jax.experimental import pallas as pl
from jax.experimental.pallas import tpu as pltpu

F32 = jnp.float32
BF16 = jnp.bfloat16
MESH = pl.DeviceIdType.MESH

D_MODEL = 1024
HEADS = 8
HEAD_DIM = 64
FOX_WIDTH = 512
MLA_WIDTH = 512
NOPE = 64
ROPE = 32
QK_DIM = 96
Q_RANK = 384
KV_RANK = 256
D_FF = 4096
IN_COLS = 2216
ROPE_THETA = 10000.0
EPS = 1e-6
FOX_SCALE = 1.0 / math.sqrt(HEAD_DIM)
MLA_SCALE = 1.0 / math.sqrt(QK_DIM)
ADAM_LR = 0.001
ADAM_B1 = 0.9
ADAM_B2 = 0.999
ADAM_EPS = 1e-08
ADAM_WD = 0.01
ADAM_STEP = 10

N_DEV = 8
LANES = 128
REST_COLS = 896
REST_CQ = LANES
REST_CKV = REST_CQ + Q_RANK
REST_KR = REST_CKV + KV_RANK
LOG2E = 1.4426950408889634
LN2 = 0.6931471805599453
FOX_Q_FACTOR = FOX_SCALE * LOG2E
MLA_Q_FACTOR = MLA_SCALE * LOG2E
VMEM_LIMIT = 56 * 1024 * 1024

IN_SHARD = IN_COLS // N_DEV
SMALL_SIZES = (1024, 8, 384, 256, 512, 512, 1024, 1024)
SMALL_ROWS = 16
LOSS_ROW = len(SMALL_SIZES)


def _cparams(*sem):
    return pltpu.CompilerParams(dimension_semantics=sem or None, vmem_limit_bytes=VMEM_LIMIT)


def _row_tile(t):
    return 512 if t >= 2048 else (256 if t >= 512 else 128)


def _dot(a, b):
    return jnp.dot(a, b, preferred_element_type=F32)


def _dot_nt(a, b):
    return lax.dot_general(a, b, (((1,), (1,)), ((), ())), preferred_element_type=F32)


def _dot_tn(a, b):
    return lax.dot_general(a, b, (((0,), (0,)), ((), ())), preferred_element_type=F32)


def _rms(x, g):
    r = lax.rsqrt(jnp.mean(x * x, axis=-1, keepdims=True) + EPS)
    return x * r * g, r


def _rms_bwd(x, g, r, dy):
    xh = x * r
    gdy = dy * g
    dx = r * (gdy - xh * jnp.mean(gdy * xh, axis=-1, keepdims=True))
    return dx, jnp.sum(dy * xh, axis=0, keepdims=True)


def _lane():
    return lax.broadcasted_iota(jnp.int32, (1, LANES), 1)


def _rot(x):
    lane = _lane()
    half = NOPE + ROPE // 2
    first = jnp.logical_and(lane >= NOPE, lane < half)
    second = jnp.logical_and(lane >= half, lane < NOPE + ROPE)
    return jnp.where(first, -pltpu.roll(x, LANES - ROPE // 2, 1), jnp.where(second, pltpu.roll(x, ROPE // 2, 1), 0.0))


def _rope(x, cos, sin):
    return x * cos + _rot(x) * sin


def _rope_bwd(dy, cos, sin):
    return dy * cos - _rot(dy * sin)


def _remote(src, dst, send_sem, recv_sem, to):
    return pltpu.make_async_remote_copy(src_ref=src, dst_ref=dst, send_sem=send_sem, recv_sem=recv_sem,
                                        device_id=to, device_id_type=MESH)


def _hbm_specs(n):
    return [pl.BlockSpec(memory_space=pl.ANY)] * n


def _all_gather(blocks):
    n = len(blocks)

    def body(*refs):
        x_refs, out_refs = refs[:n], refs[n:2 * n]
        send_sems, recv_sems, local_sems = refs[2 * n:]
        x, y, c = lax.axis_index("x"), lax.axis_index("y"), lax.axis_index("c")
        me, sibling = (x, y, c), (x, y, 1 - c)
        chips = [(1 - x, y), (x, 1 - y), (1 - x, 1 - y)]

        def slot(a, px, py, pc):
            return out_refs[a].at[4 * px + 2 * py + pc]

        def copy(a, k, blk, to, src=None):
            return _remote(slot(a, *blk) if src is None else src, slot(a, *blk),
                           send_sems.at[7 * a + k], recv_sems.at[7 * a + k], to)

        mine = [pltpu.make_async_copy(x_refs[a], slot(a, *me), local_sems.at[a]) for a in range(n)]
        first, passed = [], []
        for a in range(n):
            mine[a].start()
            first.append(copy(a, 0, me, sibling, src=x_refs[a]))
            first += [copy(a, 1 + j, me, (*chip, c), src=x_refs[a]) for j, chip in enumerate(chips)]
        for cp in first:
            cp.start()
        for a in range(n):
            for j, chip in enumerate(chips):
                copy(a, 1 + j, (*chip, c), me).wait_recv()
                passed.append(copy(a, 4 + j, (*chip, c), sibling))
                passed[-1].start()
        for a in range(n):
            copy(a, 0, sibling, me).wait_recv()
            for j, chip in enumerate(chips):
                copy(a, 4 + j, (*chip, 1 - c), me).wait_recv()
        for cp in first + passed:
            cp.wait_send()
        for cp in mine:
            cp.wait()

    return pl.pallas_call(
        body, name="all_gather_weights",
        out_shape=[jax.ShapeDtypeStruct((N_DEV,) + b.shape, b.dtype) for b in blocks],
        in_specs=_hbm_specs(n), out_specs=_hbm_specs(n),
        scratch_shapes=[pltpu.SemaphoreType.DMA((7 * n,)), pltpu.SemaphoreType.DMA((7 * n,)), pltpu.SemaphoreType.DMA((n,))],
    )(*blocks)


def _symmetric_comm(inputs, out_shape, aliases, per_array, copies):
    def start(in_refs, out_refs, sems):
        for cp in copies(in_refs, out_refs, *sems):
            cp.start()

    def finish(in_refs, out_refs, sems):
        for cp in copies(in_refs, out_refs, *sems):
            cp.wait()

    n_sems = per_array * len(inputs)
    return _Comm(tuple(inputs), tuple(out_shape), aliases,
                 (pltpu.SemaphoreType.DMA((n_sems,)), pltpu.SemaphoreType.DMA((n_sems,))), start, finish)


def _ag_direct(shards):
    def copies(in_refs, out_refs, send_sems, recv_sems):
        x, y, c = lax.axis_index("x"), lax.axis_index("y"), lax.axis_index("c")
        peers = [(x, y, 1 - c), (1 - x, y, c), (x, 1 - y, c), (1 - x, 1 - y, c)]
        cps = []
        for a in range(len(shards)):
            mine = out_refs[a].at[4 * x + 2 * y + c]
            cps.append(pltpu.make_async_copy(in_refs[a], mine, send_sems.at[5 * a]))
            cps += [_remote(in_refs[a], mine, send_sems.at[5 * a + k], recv_sems.at[5 * a + k], peer)
                    for k, peer in enumerate(peers, start=1)]
        return cps

    return _symmetric_comm(shards, [jax.ShapeDtypeStruct((N_DEV,) + s.shape, s.dtype) for s in shards], {}, 5, copies)


def _ag_to_all(shards):
    def copies(in_refs, out_refs, send_sems, recv_sems):
        x, y, c = lax.axis_index("x"), lax.axis_index("y"), lax.axis_index("c")
        cps = []
        for a in range(len(shards)):
            mine = out_refs[a].at[4 * x + 2 * y + c]
            cps.append(pltpu.make_async_copy(in_refs[a], mine, send_sems.at[N_DEV * a]))
            for k in range(1, N_DEV):
                peer = (x ^ (k >> 2), y ^ ((k >> 1) & 1), c ^ (k & 1))
                cps.append(_remote(in_refs[a], mine, send_sems.at[N_DEV * a + k], recv_sems.at[N_DEV * a + k], peer))
        return cps

    return _symmetric_comm(shards, [jax.ShapeDtypeStruct((N_DEV,) + s.shape, s.dtype) for s in shards], {}, N_DEV, copies)


def _ag_forward(gathered):
    def copies(in_refs, out_refs, send_sems, recv_sems):
        x, y, c = lax.axis_index("x"), lax.axis_index("y"), lax.axis_index("c")
        chips = [(1 - x, y), (x, 1 - y), (1 - x, 1 - y)]
        return [_remote(out_refs[a].at[4 * cx + 2 * cy + c], out_refs[a].at[4 * cx + 2 * cy + c],
                        send_sems.at[3 * a + j], recv_sems.at[3 * a + j], (x, y, 1 - c))
                for a in range(len(gathered)) for j, (cx, cy) in enumerate(chips)]

    shapes = [jax.ShapeDtypeStruct(g.shape, g.dtype) for g in gathered]
    return _symmetric_comm(gathered, shapes, {a: a for a in range(len(gathered))}, 3, copies)


def _rs_to_sibling(grads):
    def copies(in_refs, out_refs, send_sems, recv_sems):
        x, y, c = lax.axis_index("x"), lax.axis_index("y"), lax.axis_index("c")
        return [_remote(in_refs[a].at[2 * q + 1 - c], out_refs[a].at[q], send_sems.at[4 * a + q], recv_sems.at[4 * a + q], (x, y, 1 - c))
                for a in range(len(grads)) for q in range(4)]

    return _symmetric_comm(grads, [jax.ShapeDtypeStruct((4,) + g.shape[1:], g.dtype) for g in grads], {}, 4, copies)


def _rs_to_chips(parts):
    def copies(in_refs, out_refs, send_sems, recv_sems):
        x, y, c = lax.axis_index("x"), lax.axis_index("y"), lax.axis_index("c")
        chips = [(1 - x, y), (x, 1 - y), (1 - x, 1 - y)]
        return [_remote(in_refs[a].at[2 * cx + cy], out_refs[a].at[k], send_sems.at[3 * a + k], recv_sems.at[3 * a + k], (cx, cy, c))
                for a in range(len(parts)) for k, (cx, cy) in enumerate(chips)]

    return _symmetric_comm(parts, [jax.ShapeDtypeStruct((3,) + p.shape[1:], p.dtype) for p in parts], {}, 3, copies)


def _comm_call(name, comm):
    n_in, n_out = len(comm.inputs), len(comm.out_shape)

    def body(*refs):
        ins, outs, sems = refs[:n_in], refs[n_in:n_in + n_out], refs[n_in + n_out:]
        comm.start(ins, outs, sems)
        comm.finish(ins, outs, sems)

    return pl.pallas_call(
        body, name=name, out_shape=list(comm.out_shape), in_specs=_hbm_specs(n_in), out_specs=_hbm_specs(n_out),
        scratch_shapes=list(comm.scratch), input_output_aliases=dict(comm.aliases),
    )(*comm.inputs)


def _small_all_reduce(parts, loss8):
    n = len(parts)

    def body(*refs):
        p_refs, loss_ref, out_ref, pack, land, send_sems, recv_sems = refs[:n], *refs[n:]
        x, y, c = lax.axis_index("x"), lax.axis_index("y"), lax.axis_index("c")
        me = 4 * x + 2 * y + c
        pack[...] = jnp.zeros_like(pack)
        for r, ref in enumerate(p_refs):
            pack[r:r + 1, 0:ref.shape[1]] = ref[...]
        pack[LOSS_ROW:LOSS_ROW + 1, 0:LANES] = loss_ref[0:1, :]
        land[me] = pack[...]
        cps = []
        for k in range(1, N_DEV):
            peer = (x ^ (k >> 2), y ^ ((k >> 1) & 1), c ^ (k & 1))
            cps.append(_remote(pack, land.at[me], send_sems.at[k - 1], recv_sems.at[k - 1], peer))
        for cp in cps:
            cp.start()
        for cp in cps:
            cp.wait()
        acc = land[0]
        for d in range(1, N_DEV):
            acc = acc + land[d]
        out_ref[...] = acc

    vmem = pl.BlockSpec(memory_space=pltpu.VMEM)
    return pl.pallas_call(
        body, name="small_all_reduce",
        out_shape=jax.ShapeDtypeStruct((SMALL_ROWS, D_MODEL), F32),
        in_specs=[vmem] * (n + 1), out_specs=vmem,
        scratch_shapes=[pltpu.VMEM((SMALL_ROWS, D_MODEL), F32), pltpu.VMEM((N_DEV, SMALL_ROWS, D_MODEL), F32),
                        pltpu.SemaphoreType.DMA((N_DEV - 1,)), pltpu.SemaphoreType.DMA((N_DEV - 1,))],
    )(*parts, loss8)


def _rs_sibling_sum(name, grad, got, place):
    _, rows, cols = grad.shape

    def body(place_ref, g_ref, l_ref, own_ref, b_ref):
        s = g_ref[...] + l_ref[...]
        b_ref[...] = s.astype(BF16)

        @pl.when(pl.program_id(0) == place_ref[1])
        def _():
            own_ref[...] = s

    by_chip = pl.BlockSpec((None, rows, cols), lambda q, place_ref: (q, 0, 0))
    return pl.pallas_call(
        body, name=name,
        grid_spec=pltpu.PrefetchScalarGridSpec(
            num_scalar_prefetch=1, grid=(4,),
            in_specs=[pl.BlockSpec((None, rows, cols), lambda q, place_ref: (2 * q + place_ref[0], 0, 0)), by_chip],
            out_specs=[pl.BlockSpec((rows, cols), lambda q, place_ref: (0, 0)), by_chip]),
        out_shape=(jax.ShapeDtypeStruct((rows, cols), F32), jax.ShapeDtypeStruct((4, rows, cols), BF16)),
        compiler_params=_cparams("arbitrary"),
    )(place, grad, got)


def _adamw_math(w, g, m, v):
    m2 = ADAM_B1 * m + (1.0 - ADAM_B1) * g
    v2 = ADAM_B2 * v + (1.0 - ADAM_B2) * (g * g)
    m_hat = m2 / (1.0 - ADAM_B1 ** ADAM_STEP)
    v_hat = v2 / (1.0 - ADAM_B2 ** ADAM_STEP)
    delta = -ADAM_LR * (m_hat / (jnp.sqrt(v_hat) + ADAM_EPS) + ADAM_WD * w)
    return delta, m2, v2


def _update_tile(rows):
    return 256 if rows % 256 == 0 else rows


def _rs_final_sum(name, own, got):
    def body(o_ref, r_ref, g_out):
        g = o_ref[...]
        for k in range(3):
            g = g + r_ref[k].astype(F32)
        g_out[...] = g

    return pl.pallas_call(body, name=name, out_shape=jax.ShapeDtypeStruct(own.shape, F32))(own, got)


def _adamw_sharded(name, w, m, v, own, got):
    _, rows, cols = w.shape
    tr = _update_tile(rows)

    def body(o_ref, r_ref, w_ref, m_ref, v_ref, g_out, d_out, m_out, v_out):
        g = o_ref[:, 0:cols]
        for k in range(3):
            g = g + r_ref[k, :, 0:cols].astype(F32)
        d, m2, v2 = _adamw_math(w_ref[0], g, m_ref[0], v_ref[0])
        g_out[0] = g
        d_out[0] = d
        m_out[0] = m2
        v_out[0] = v2

    mine = pl.BlockSpec((1, tr, cols), lambda i: (0, i, 0))
    shp = jax.ShapeDtypeStruct(w.shape, F32)
    wide = own.shape[1]
    return pl.pallas_call(
        body, name=name, grid=(rows // tr,), out_shape=(shp,) * 4,
        in_specs=[pl.BlockSpec((tr, wide), lambda i: (i, 0)), pl.BlockSpec((3, tr, wide), lambda i: (0, i, 0)),
                  mine, mine, mine],
        out_specs=[mine] * 4,
        compiler_params=_cparams("parallel"),
    )(own, got, w, m, v)


def _adamw_given(name, g, w, m, v):
    _, rows, cols = w.shape
    tr = _update_tile(rows)

    def body(g_ref, w_ref, m_ref, v_ref, g_out, d_out, m_out, v_out):
        g = g_ref[...]
        d, m2, v2 = _adamw_math(w_ref[0], g, m_ref[0], v_ref[0])
        g_out[0] = g
        d_out[0] = d
        m_out[0] = m2
        v_out[0] = v2

    own = pl.BlockSpec((1, tr, cols), lambda i: (0, i, 0))
    shp = jax.ShapeDtypeStruct(w.shape, F32)
    return pl.pallas_call(
        body, name=name, grid=(rows // tr,), out_shape=(shp,) * 4,
        in_specs=[pl.BlockSpec((tr, cols), lambda i: (i, 0)), own, own, own], out_specs=[own] * 4,
        compiler_params=_cparams("parallel"),
    )(g, w, m, v)


def _adamw_small(total, ws, ms, vs):
    n = len(ws)

    def body(*refs):
        t_ref = refs[0]
        w_refs, m_refs, v_refs = refs[1:1 + n], refs[1 + n:1 + 2 * n], refs[1 + 2 * n:1 + 3 * n]
        outs = refs[1 + 3 * n:]
        outs[0][...] = t_ref[LOSS_ROW:LOSS_ROW + 1, 0:1]
        for r in range(n):
            g = t_ref[r:r + 1, 0:w_refs[r].shape[1]]
            d, m2, v2 = _adamw_math(w_refs[r][...], g, m_refs[r][...], v_refs[r][...])
            for k, val in enumerate((g, d, m2, v2)):
                outs[1 + 4 * r + k][...] = val

    vmem = pl.BlockSpec(memory_space=pltpu.VMEM)
    out_shape = [jax.ShapeDtypeStruct((1, 1), F32)]
    for w in ws:
        out_shape += [jax.ShapeDtypeStruct(w.shape, F32)] * 4
    return pl.pallas_call(
        body, name="adamw_small", out_shape=out_shape,
        in_specs=[vmem] * (1 + 3 * n), out_specs=[vmem] * len(out_shape),
    )(total, *ws, *ms, *vs)


def _rope_tables(pos_col):
    t = pos_col.shape[0]
    inv = (np.float32(ROPE_THETA) ** (-np.arange(0, ROPE, 2, dtype=np.float32) / np.float32(ROPE))).astype(np.float32)
    freq = np.zeros((1, LANES), np.float32)
    freq[0, NOPE:NOPE + ROPE // 2] = inv
    freq[0, NOPE + ROPE // 2:NOPE + ROPE] = inv
    tm = _row_tile(t)

    def body(p_ref, f_ref, c_ref, s_ref):
        ang = p_ref[...].astype(F32) * f_ref[...]
        c_ref[...] = jnp.cos(ang)
        s_ref[...] = jnp.sin(ang)

    shp = jax.ShapeDtypeStruct((t, LANES), F32)
    return pl.pallas_call(
        body, name="rope_tables", grid=(t // tm,), out_shape=(shp, shp),
        in_specs=[pl.BlockSpec((tm, 1), lambda i: (i, 0)), pl.BlockSpec((1, LANES), lambda i: (0, 0))],
        out_specs=(pl.BlockSpec((tm, LANES), lambda i: (i, 0)),) * 2,
        compiler_params=_cparams("parallel"),
    )(pos_col, jnp.asarray(freq))


def _in_proj(x, g, w_qkv, w_rest, comm=None):
    t = x.shape[0]
    tm = _row_tile(t)

    def main(ins, outs, scr):
        x_ref, g_ref, wq_ref, wr_ref = ins
        h_ref, fq_ref, fk_ref, fv_ref, r_ref = outs
        h, _ = _rms(x_ref[...], g_ref[...])
        hb = h.astype(BF16)
        h_ref[...] = hb
        for n, (ref, factor) in enumerate(((fq_ref, FOX_Q_FACTOR), (fk_ref, None), (fv_ref, None))):
            part = _dot_nt(hb, wq_ref[n * FOX_WIDTH:(n + 1) * FOX_WIDTH, :])
            ref[...] = (part if factor is None else part * factor).astype(BF16)
        r_ref[...] = _dot_nt(hb, wr_ref[...])

    row = lambda n: pl.BlockSpec((tm, n), lambda i: (i, 0))
    full = lambda a: pl.BlockSpec(a.shape, lambda i: (0,) * a.ndim)
    return _hosted_call(
        "in_proj", main, (t // tm,), [x, g, w_qkv, w_rest], [row(D_MODEL), full(g), full(w_qkv), full(w_rest)],
        (jax.ShapeDtypeStruct((t, D_MODEL), BF16),) + (jax.ShapeDtypeStruct((t, FOX_WIDTH), BF16),) * 3
        + (jax.ShapeDtypeStruct((t, REST_COLS), F32),),
        (row(D_MODEL), row(FOX_WIDTH), row(FOX_WIDTH), row(FOX_WIDTH), row(REST_COLS)), [], comm)


def _log_sigmoid(z):
    return jnp.minimum(z, 0.0) - jnp.log(1.0 + jnp.exp(-jnp.abs(z)))


def _split3(v):
    hi = v.astype(BF16)
    r1 = v - hi.astype(F32)
    mid = r1.astype(BF16)
    lo = (r1 - mid.astype(F32)).astype(BF16)
    return hi, mid, lo


def _scan_tile(t):
    return 512 if t >= 2048 else (256 if t >= 256 else t)


def _forget_cumsum(rest, b128):
    t = rest.shape[0]
    tb = _scan_tile(t)

    def body(r_ref, b_ref, row_ref, rep_ref, f_sc, carry):
        @pl.when(pl.program_id(0) == 0)
        def _():
            carry[...] = jnp.zeros_like(carry)
        lf = _log_sigmoid(r_ref[...] + b_ref[...])
        tri = (lax.broadcasted_iota(jnp.int32, (tb, tb), 0) >= lax.broadcasted_iota(jnp.int32, (tb, tb), 1)).astype(BF16)
        hi, mid, lo = _split3(lf)
        f_sc[...] = (_dot(tri, hi) + _dot(tri, mid)) + _dot(tri, lo) + carry[...]
        carry[...] = f_sc[tb - 1:tb, :]
        f2 = f_sc[...] * LOG2E
        row_ref[...] = jnp.transpose(f2)[0:HEADS, :]
        lane = _lane()
        for h in range(HEADS):
            col = jnp.sum(jnp.where(lane == h, f2, 0.0), axis=1, keepdims=True)
            rep_ref[h] = jnp.broadcast_to(col, (tb, LANES))

    return pl.pallas_call(
        body, name="forget_cumsum", grid=(t // tb,),
        out_shape=(jax.ShapeDtypeStruct((HEADS, t), F32), jax.ShapeDtypeStruct((HEADS, t, LANES), F32)),
        in_specs=[pl.BlockSpec((tb, LANES), lambda i: (i, 0)), pl.BlockSpec((1, LANES), lambda i: (0, 0))],
        out_specs=(pl.BlockSpec((HEADS, tb), lambda i: (0, i)), pl.BlockSpec((HEADS, tb, LANES), lambda i: (0, i, 0))),
        scratch_shapes=[pltpu.VMEM((tb, LANES), F32), pltpu.VMEM((1, LANES), F32)],
        compiler_params=_cparams("arbitrary"),
    )(rest, b128)


def _forget_bwd(rest, b128, d_fq, d_fk):
    t = rest.shape[0]
    tb = _scan_tile(t)
    nb = t // tb

    def body(r_ref, b_ref, dfq_ref, dfk_ref, dz_ref, db_ref, carry):
        @pl.when(pl.program_id(0) == 0)
        def _():
            carry[...] = jnp.zeros_like(carry)
            db_ref[...] = jnp.zeros_like(db_ref)
        tri = (lax.broadcasted_iota(jnp.int32, (tb, tb), 0) <= lax.broadcasted_iota(jnp.int32, (tb, tb), 1)).astype(BF16)
        lane = _lane()
        df = jnp.zeros((tb, LANES), F32)
        for h in range(HEADS):
            df = df + jnp.where(lane == h, dfq_ref[h] + dfk_ref[h], 0.0)
        hi, mid, lo = _split3(df)
        dlf = (_dot(tri, hi) + _dot(tri, mid)) + _dot(tri, lo) + carry[...]
        z = r_ref[...] + b_ref[...]
        dz = dlf / (1.0 + jnp.exp(z))
        dz_ref[...] = dz
        db_ref[...] += jnp.sum(dz, axis=0, keepdims=True)
        carry[...] = carry[...] + jnp.sum(df, axis=0, keepdims=True)

    rev = lambda i: (nb - 1 - i, 0)
    rev3 = pl.BlockSpec((HEADS, tb, LANES), lambda i: (0, nb - 1 - i, 0))
    return pl.pallas_call(
        body, name="forget_bwd", grid=(nb,),
        out_shape=(jax.ShapeDtypeStruct((t, LANES), F32), jax.ShapeDtypeStruct((1, LANES), F32)),
        in_specs=[pl.BlockSpec((tb, LANES), rev), pl.BlockSpec((1, LANES), lambda i: (0, 0)), rev3, rev3],
        out_specs=(pl.BlockSpec((tb, LANES), rev), pl.BlockSpec((1, LANES), lambda i: (0, 0))),
        scratch_shapes=[pltpu.VMEM((1, LANES), F32)],
        compiler_params=_cparams("arbitrary"),
    )(rest, b128, d_fq, d_fk)


def _mla_prep(rest, gq, gkv, wq, wkv, cos, sin):
    t = rest.shape[0]
    tm = _row_tile(t)

    def body(r_ref, gq_ref, gkv_ref, wq_ref, wkv_ref, c_ref, s_ref, q_ref, k_ref, kv_ref, cq_ref, ckv_ref):
        cos_, sin_ = c_ref[...], s_ref[...]
        cq, _ = _rms(r_ref[:, REST_CQ:REST_CKV], gq_ref[...])
        ckv, _ = _rms(r_ref[:, REST_CKV:REST_KR], gkv_ref[...])
        cqb, ckvb = cq.astype(BF16), ckv.astype(BF16)
        cq_ref[...] = cqb
        ckv_ref[...] = ckvb
        k_rope = _rope(r_ref[:, REST_KR:REST_COLS], cos_, sin_)
        lo = _lane() < NOPE
        for h in range(HEADS):
            q_ref[h] = (_rope(_dot(cqb, wq_ref[h]), cos_, sin_) * MLA_Q_FACTOR).astype(BF16)
            kv = _dot(ckvb, wkv_ref[h])
            kv_ref[h] = kv.astype(BF16)
            k_ref[h] = (jnp.where(lo, kv, 0.0) + k_rope).astype(BF16)

    row = lambda n: pl.BlockSpec((tm, n), lambda i: (i, 0))
    full = lambda a: pl.BlockSpec(a.shape, lambda i: (0,) * a.ndim)
    heads = pl.BlockSpec((HEADS, tm, LANES), lambda i: (0, i, 0))
    hshape = jax.ShapeDtypeStruct((HEADS, t, LANES), BF16)
    return pl.pallas_call(
        body, name="mla_prep", grid=(t // tm,),
        out_shape=(hshape, hshape, hshape, jax.ShapeDtypeStruct((t, Q_RANK), BF16), jax.ShapeDtypeStruct((t, KV_RANK), BF16)),
        in_specs=[row(REST_COLS), full(gq), full(gkv), full(wq), full(wkv), row(LANES), row(LANES)],
        out_specs=(heads, heads, heads, row(Q_RANK), row(KV_RANK)),
        compiler_params=_cparams("parallel"),
    )(rest, gq, gkv, wq, wkv, cos, sin)


def _tile_lanes(x, n):
    return jnp.tile(x, (1, n)) if n > 1 else x


class _Comm(NamedTuple):
    inputs: tuple
    out_shape: tuple
    aliases: dict
    scratch: tuple
    start: Callable
    finish: Callable


def _hosted_call(name, main, grid, args, in_specs, out_shape, out_specs, scratch, comm):
    n_in, n_out, n_scr = len(args), len(out_shape), len(scratch)
    c_in = list(comm.inputs) if comm else []
    c_out = list(comm.out_shape) if comm else []

    def at_step(which):
        hit = pl.program_id(0) == which[0]
        for axis in range(1, len(grid)):
            hit = jnp.logical_and(hit, pl.program_id(axis) == which[axis])
        return hit

    def body(*refs):
        bounds = [0, n_in, len(c_in), n_out, len(c_out), n_scr]
        starts = [sum(bounds[:k + 1]) for k in range(len(bounds))]
        ins, cins, outs, couts, scr = [refs[a:b] for a, b in zip(starts[:-1], starts[1:])]
        sems = refs[starts[-1]:]
        if comm:
            @pl.when(at_step([0] * len(grid)))
            def _():
                comm.start(cins, couts, sems)
        main(ins, outs, scr)
        if comm:
            @pl.when(at_step([n - 1 for n in grid]))
            def _():
                comm.finish(cins, couts, sems)

    res = pl.pallas_call(
        body, name=name, grid=grid,
        out_shape=list(out_shape) + c_out,
        in_specs=list(in_specs) + _hbm_specs(len(c_in)),
        out_specs=list(out_specs) + _hbm_specs(len(c_out)),
        scratch_shapes=list(scratch) + (list(comm.scratch) if comm else []),
        input_output_aliases={n_in + i: n_out + o for i, o in comm.aliases.items()} if comm else {},
        compiler_params=_cparams(*(["arbitrary"] * len(grid))),
    )(*args, *c_in)
    return res[:n_out], res[n_out:]


def _stat_rows(x):
    return jnp.transpose(x)[0:8, :]


FWD_HEADS = 4


def _attn_fwd(fox, q, k, v, f2_rows=None, comm=None):
    t = q.shape[0] if fox else q.shape[1]
    tq = _row_tile(t)
    nq = t // tq
    nh = FWD_HEADS
    wide = (nh // 2) * LANES

    def main(ins, outs, scr):
        q_ref, k_ref, v_ref = ins[:3]
        fr_ref = ins[3] if fox else None
        o_ref, lset_ref = outs
        m_sc, acc_sc = scr
        i = pl.program_id(1)
        lo = _lane() < HEAD_DIM
        hi = jnp.logical_not(lo)
        zero, one = jnp.zeros((), BF16), jnp.ones((), BF16)
        lanes_of = lambda h: slice((h // 2) * LANES, (h // 2 + 1) * LANES)
        if fox:
            qs = [jnp.where(lo if h % 2 == 0 else hi, q_ref[:, lanes_of(h)], zero) for h in range(nh)]
            sum_lanes = [hi if h % 2 == 0 else lo for h in range(nh)]
        else:
            qs = [q_ref[h] for h in range(nh)]
            sum_lanes = [lo] * nh
        m_sc[...] = jnp.full_like(m_sc, -jnp.inf)
        acc_sc[...] = jnp.zeros_like(acc_sc)

        def block(j, r0, nr, c0, nc, seen_from):
            rows = slice(r0, r0 + nr)
            sl = pl.ds(pl.multiple_of(j * tq + c0, math.gcd(tq, c0) if c0 else tq), nc)
            if seen_from is not None:
                seen = (lax.broadcasted_iota(jnp.int32, (nr, nc), 1)
                        <= lax.broadcasted_iota(jnp.int32, (nr, nc), 0) + seen_from)
            for h in range(nh):
                kj, vj = (k_ref[sl, lanes_of(h)], v_ref[sl, lanes_of(h)]) if fox else (k_ref[h, sl, :], v_ref[h, sl, :])
                s = _dot_nt(qs[h][rows], kj)
                if fox:
                    s = s - fr_ref[h, j, :, c0:c0 + nc]
                if seen_from is not None:
                    s = jnp.where(seen, s, -jnp.inf)
                m_prev = m_sc[h, rows]
                m_new = jnp.maximum(m_prev, jnp.max(s, axis=1, keepdims=True))
                p = jnp.exp2((s - _tile_lanes(m_new, nc // LANES)).astype(BF16))
                vj = jnp.where(sum_lanes[h], one, vj)
                acc_sc[h, rows] = jnp.exp2(m_prev - m_new) * acc_sc[h, rows] + _dot(p, vj)
                m_sc[h, rows] = m_new

        def loop_body(j, carry):
            block(j, 0, tq, 0, tq, None)
            return carry

        lax.fori_loop(0, i, loop_body, 0)
        half = tq // 2
        if half % LANES == 0:
            block(i, 0, half, 0, half, 0)
            block(i, half, half, 0, tq, half)
        else:
            block(i, 0, tq, 0, tq, 0)
        res = []
        for h in range(nh):
            acc = acc_sc[h]
            swapped = pltpu.roll(acc, HEAD_DIM, 1)
            res.append(acc / swapped)
            lse2 = m_sc[h] + jnp.log(jnp.where(sum_lanes[h], acc, swapped)) * LOG2E
            lset_ref[h, 0] = _stat_rows(lse2)
        for pr in range(nh // 2):
            even = res[2 * pr] if fox else pltpu.roll(res[2 * pr], HEAD_DIM, 1)
            o_ref[:, pr * LANES:(pr + 1) * LANES] = jnp.where(lo, even, res[2 * pr + 1])

    if fox:
        in_specs = [pl.BlockSpec((tq, wide), lambda g, i: (i, g))] + [pl.BlockSpec((t, wide), lambda g, i: (0, g))] * 2
        in_specs += [pl.BlockSpec((nh, nq, 1, tq), lambda g, i: (g, 0, 0, 0))]
        args = [q, k, v, f2_rows]
    else:
        in_specs = [pl.BlockSpec((nh, tq, LANES), lambda g, i: (g, i, 0))] + [pl.BlockSpec((nh, t, LANES), lambda g, i: (g, 0, 0))] * 2
        args = [q, k, v]
    return _hosted_call(
        "fox_attn_fwd" if fox else "mla_attn_fwd", main, (HEADS // nh, nq), args, in_specs,
        (jax.ShapeDtypeStruct((t, 4 * LANES), F32), jax.ShapeDtypeStruct((HEADS, nq, 8, tq), F32)),
        (pl.BlockSpec((tq, wide), lambda g, i: (i, g)), pl.BlockSpec((nh, 1, 8, tq), lambda g, i: (g, i, 0, 0))),
        [pltpu.VMEM((nh, tq, LANES), F32), pltpu.VMEM((nh, tq, LANES), F32)], comm)


def _head_do(fox, hh, do2, lo):
    if fox:
        return jnp.where(lo if hh == 0 else jnp.logical_not(lo), do2, 0.0)
    return jnp.where(lo, 0.0, pltpu.roll(do2, HEAD_DIM, 1) if hh == 0 else do2)


BWD_HEADS = 4


def _attn_bwd(fox, q, k, v, do, lse_rows, delta_rows, f2_rep=None, comm=None):
    t = q.shape[0] if fox else q.shape[1]
    tq = _row_tile(t)
    nq = t // tq
    scale = FOX_SCALE if fox else MLA_SCALE
    nh = BWD_HEADS

    def main(ins, outs, scr):
        if fox:
            q_ref, k_ref, v_ref, f_ref, do_ref, lse_ref, dl_ref = ins
            dq_ref, dk_ref, dv_ref, dfq_ref, dfk_ref = outs
        else:
            q_ref, k_ref, v_ref, do_ref, lse_ref, dl_ref = ins
            dq_ref, dkv_ref, dkr_ref = outs
        dq_sc, dk_sc, dv_sc = scr
        j = pl.program_id(1)
        lane = _lane()
        lo = lane < HEAD_DIM
        hi = jnp.logical_not(lo)
        zero, one = jnp.zeros((), BF16), jnp.ones((), BF16)
        lanes_of = lambda h: slice((h // 2) * LANES, (h // 2 + 1) * LANES)

        @pl.when(j == 0)
        def _():
            dq_sc[...] = jnp.zeros_like(dq_sc)

        dk_sc[...] = jnp.zeros_like(dk_sc)
        dv_sc[...] = jnp.zeros_like(dv_sc)

        def block(i, r0, nr, c0, nc, masked):
            rows, cols = slice(r0, r0 + nr), slice(c0, c0 + nc)
            sl = pl.ds(pl.multiple_of(i * tq + c0, math.gcd(tq, c0) if c0 else tq), nc)
            if masked:
                seen = lax.broadcasted_iota(jnp.int32, (nr, nc), 1) >= lax.broadcasted_iota(jnp.int32, (nr, nc), 0)
            for h in range(nh):
                hh = h % 2
                kj = k_ref[rows, lanes_of(h)] if fox else k_ref[h, rows, :]
                vj = v_ref[rows, lanes_of(h)] if fox else v_ref[h, rows, :]
                qi = jnp.where(lo if hh == 0 else hi, q_ref[sl, lanes_of(h)], zero) if fox else q_ref[h, sl, :]
                dob = _head_do(fox, hh, do_ref[sl, lanes_of(h)], lo).astype(BF16)
                st = _dot_nt(kj, qi)
                if fox:
                    st = st - _tile_lanes(f_ref[h, rows, :], nc // LANES)
                if masked:
                    st = jnp.where(seen, st, -jnp.inf)
                pt = jnp.exp2(st - lse_ref[h, i, 0:1, cols])
                dpt = _dot_nt(vj, dob)
                dst = (pt * (dpt - dl_ref[h, i, 0:1, cols])).astype(BF16)
                dv_sc[h, rows] += _dot(pt.astype(BF16), dob)
                if fox:
                    other = hi if hh == 0 else lo
                    qi = jnp.where(other, one, qi)
                    kj = jnp.where(other, one, kj)
                dk_sc[h, rows] += _dot(dst, qi)
                dq_sc[h, sl, :] += _dot_tn(dst, kj)

        def loop_body(i, carry):
            block(i, 0, tq, 0, tq, False)
            return carry

        half = tq // 2
        if half % LANES == 0:
            block(j, 0, half, 0, tq, True)
            block(j, half, half, half, half, True)
        else:
            block(j, 0, tq, 0, tq, True)
        lax.fori_loop(j + 1, nq, loop_body, 0)
        rope_lanes = jnp.logical_and(lane >= NOPE, lane < NOPE + ROPE)
        for pr in range(nh // 2):
            a, b = 2 * pr, 2 * pr + 1
            if fox:
                dk_ref[:, lanes_of(a)] = (jnp.where(lo, dk_sc[a], dk_sc[b]) * LN2).astype(BF16)
                dv_ref[:, lanes_of(a)] = (dv_sc[a] + dv_sc[b]).astype(BF16)
                for h in (a, b):
                    dk = dk_sc[h]
                    dfk_ref[h] = -jnp.where(hi if h == a else lo, dk, pltpu.roll(dk, HEAD_DIM, 1))
            else:
                dkr = jnp.zeros((tq, LANES), F32)
                for h in (a, b):
                    dk = dk_sc[h] * LN2
                    dkv_ref[h] = jnp.where(lo, dk, dv_sc[h])
                    dkr = dkr + jnp.where(rope_lanes, dk, 0.0)
                dkr_ref[pr] = dkr

        @pl.when(j == nq - 1)
        def _():
            for i in range(nq):
                rows = slice(i * tq, (i + 1) * tq)
                for pr in range(nh // 2):
                    a, b = 2 * pr, 2 * pr + 1
                    if fox:
                        dq_ref[rows, lanes_of(a)] = (jnp.where(lo, dq_sc[a, rows, :], dq_sc[b, rows, :]) * scale).astype(BF16)
                        for h in (a, b):
                            acc = dq_sc[h, rows, :]
                            dfq_ref[h, rows, :] = jnp.where(hi if h == a else lo, acc, pltpu.roll(acc, HEAD_DIM, 1))
                    else:
                        for h in (a, b):
                            dq_ref[h, rows, :] = dq_sc[h, rows, :] * scale

    wide_cols = (nh // 2) * LANES
    stat = pl.BlockSpec((nh, tq, LANES), lambda p, j: (p, j, 0))
    stat_all = pl.BlockSpec((nh, t, LANES), lambda p, j: (p, 0, 0))
    rows4 = pl.BlockSpec((nh, nq, 8, tq), lambda p, j: (p, 0, 0, 0))
    pair = pl.BlockSpec((tq, wide_cols), lambda p, j: (j, p))
    pair_all = pl.BlockSpec((t, wide_cols), lambda p, j: (0, p))
    if fox:
        in_specs = [pair_all, pair, pair, stat]
        args = [q, k, v, f2_rep]
    else:
        in_specs = [stat_all, stat, stat]
        args = [q, k, v]
    in_specs += [pair_all, rows4, rows4]
    args += [do, lse_rows, delta_rows]
    heads_f32 = jax.ShapeDtypeStruct((HEADS, t, LANES), F32)
    if fox:
        wide = jax.ShapeDtypeStruct((t, 4 * LANES), BF16)
        out_shape = (wide, wide, wide, heads_f32, heads_f32)
        out_specs = (pair_all, pair, pair, stat_all, stat)
    else:
        out_shape = (heads_f32, heads_f32, jax.ShapeDtypeStruct((HEADS // 2, t, LANES), F32))
        out_specs = (stat_all, stat, pl.BlockSpec((nh // 2, tq, LANES), lambda p, j: (p, j, 0)))
    acc = pltpu.VMEM((nh, tq, LANES), F32)
    return _hosted_call("fox_attn_bwd" if fox else "mla_attn_bwd", main, (HEADS // nh, nq), args, in_specs,
                        out_shape, out_specs, [pltpu.VMEM((nh, t, LANES), F32), acc, acc], comm)


def _attn_out(x, fox_o, mla_o, gf, gm, w_o):
    t = x.shape[0]
    tm = _row_tile(t)

    def body(x_ref, f_ref, m_ref, gf_ref, gm_ref, w_ref, x1_ref, mix_ref):
        nf, _ = _rms(f_ref[...], gf_ref[...])
        nm, _ = _rms(m_ref[...], gm_ref[...])
        nfb, nmb = nf.astype(BF16), nm.astype(BF16)
        mix_ref[:, :FOX_WIDTH] = nfb
        mix_ref[:, FOX_WIDTH:] = nmb
        x1_ref[...] = x_ref[...] + _dot(nfb, w_ref[:FOX_WIDTH, :]) + _dot(nmb, w_ref[FOX_WIDTH:, :])

    row = lambda n: pl.BlockSpec((tm, n), lambda i: (i, 0))
    full = lambda a: pl.BlockSpec(a.shape, lambda i: (0,) * a.ndim)
    return pl.pallas_call(
        body, name="attn_out", grid=(t // tm,),
        out_shape=(jax.ShapeDtypeStruct((t, D_MODEL), F32), jax.ShapeDtypeStruct((t, D_MODEL), BF16)),
        in_specs=[row(D_MODEL), row(FOX_WIDTH), row(MLA_WIDTH), full(gf), full(gm), full(w_o)],
        out_specs=(row(D_MODEL), row(D_MODEL)),
        compiler_params=_cparams("parallel"),
    )(x, fox_o, mla_o, gf, gm, w_o)


def _mlp_tile(t):
    return 256 if t >= 2048 else 128


def _resident(a):
    return pl.BlockSpec(a.shape, lambda i: (0,) * a.ndim, pipeline_mode=pl.Buffered(1))


FF_CHUNK = 512


def _mlp_fwd(x1, g_mlp, w_up, w_down, g_fin, target):
    t = x1.shape[0]
    tm = _mlp_tile(t)

    def body(x_ref, g_ref, wu_ref, wd_ref, gf_ref, t_ref, u_ref, h_ref, dx_ref, dxb_ref, loss_ref, dg_ref, a_sc):
        @pl.when(pl.program_id(0) == 0)
        def _():
            loss_ref[...] = jnp.zeros_like(loss_ref)
            dg_ref[...] = jnp.zeros_like(dg_ref)

        x = x_ref[...]
        h, _ = _rms(x, g_ref[...])
        hb = h.astype(BF16)
        h_ref[...] = hb
        for f in range(D_FF // FF_CHUNK):
            sl = slice(f * FF_CHUNK, (f + 1) * FF_CHUNK)
            u = _dot(hb, wu_ref[:, sl])
            u_ref[:, sl] = u
            r = jnp.maximum(u, 0.0)
            a_sc[:, sl] = (r * r).astype(BF16)
        x2 = x + _dot(a_sc[...], wd_ref[...])
        y, r2 = _rms(x2, gf_ref[...])
        err = y - t_ref[...]
        loss_ref[...] += 0.5 * jnp.sum(jnp.mean(err * err, axis=-1, keepdims=True))
        dx, dg = _rms_bwd(x2, gf_ref[...], r2, err * (1.0 / D_MODEL))
        dx_ref[...] = dx
        dxb_ref[...] = dx.astype(BF16)
        dg_ref[...] += dg

    row = lambda n: pl.BlockSpec((tm, n), lambda i: (i, 0))
    vec = pl.BlockSpec((1, D_MODEL), lambda i: (0, 0))
    return pl.pallas_call(
        body, name="mlp_fwd", grid=(t // tm,),
        out_shape=(jax.ShapeDtypeStruct((t, D_FF), F32), jax.ShapeDtypeStruct((t, D_MODEL), BF16),
                   jax.ShapeDtypeStruct((t, D_MODEL), F32), jax.ShapeDtypeStruct((t, D_MODEL), BF16),
                   jax.ShapeDtypeStruct((8, LANES), F32), jax.ShapeDtypeStruct((1, D_MODEL), F32)),
        in_specs=[row(D_MODEL), vec, _resident(w_up), _resident(w_down), vec, row(D_MODEL)],
        out_specs=(row(D_FF), row(D_MODEL), row(D_MODEL), row(D_MODEL), pl.BlockSpec((8, LANES), lambda i: (0, 0)), vec),
        scratch_shapes=[pltpu.VMEM((tm, D_FF), BF16)],
        compiler_params=_cparams("arbitrary"),
    )(x1, g_mlp, w_up, w_down, g_fin, target)


def _mlp_bwd(dx2, u, x1, g_mlp, w_up, w_down):
    t = x1.shape[0]
    tm = _mlp_tile(t)

    def body(dx_ref, u_ref, x_ref, g_ref, wu_ref, wd_ref, du_ref, a_ref, dx1_ref, dx1b_ref, dg_ref):
        @pl.when(pl.program_id(0) == 0)
        def _():
            dg_ref[...] = jnp.zeros_like(dg_ref)

        dx2 = dx_ref[...]
        dxb = dx2.astype(BF16)
        for f in range(D_FF // FF_CHUNK):
            sl = slice(f * FF_CHUNK, (f + 1) * FF_CHUNK)
            r = jnp.maximum(u_ref[:, sl], 0.0)
            a_ref[:, sl] = (r * r).astype(BF16)
            da = _dot_nt(dxb, wd_ref[sl, :])
            du_ref[:, sl] = (da * (2.0 * r)).astype(BF16)
        dh = _dot_nt(du_ref[...], wu_ref[...])
        x = x_ref[...]
        _, r1 = _rms(x, g_ref[...])
        dx, dg = _rms_bwd(x, g_ref[...], r1, dh)
        dx1 = dx2 + dx
        dx1_ref[...] = dx1
        dx1b_ref[...] = dx1.astype(BF16)
        dg_ref[...] += dg

    row = lambda n: pl.BlockSpec((tm, n), lambda i: (i, 0))
    vec = pl.BlockSpec((1, D_MODEL), lambda i: (0, 0))
    return pl.pallas_call(
        body, name="mlp_bwd", grid=(t // tm,),
        out_shape=(jax.ShapeDtypeStruct((t, D_FF), BF16), jax.ShapeDtypeStruct((t, D_FF), BF16),
                   jax.ShapeDtypeStruct((t, D_MODEL), F32), jax.ShapeDtypeStruct((t, D_MODEL), BF16),
                   jax.ShapeDtypeStruct((1, D_MODEL), F32)),
        in_specs=[row(D_MODEL), row(D_FF), row(D_MODEL), vec, _resident(w_up), _resident(w_down)],
        out_specs=(row(D_FF), row(D_FF), row(D_MODEL), row(D_MODEL), vec),
        compiler_params=_cparams("arbitrary"),
    )(dx2, u, x1, g_mlp, w_up, w_down)


def _matmul_tn(name, a, b, blocks=None):
    t, m = a.shape
    n = b.shape[1]
    tk = t if a.dtype == BF16 and b.dtype == BF16 else min(t, 2048)
    steps = t // tk
    bm = m if m <= 1024 else 512
    bn = n if n <= 1024 else 512
    width = bn if blocks is None else n // blocks
    per = bn // width

    def body(a_ref, b_ref, o_ref, acc_sc):
        kk = pl.program_id(2)

        @pl.when(kk == 0)
        def _():
            acc_sc[...] = jnp.zeros_like(acc_sc)

        acc_sc[...] += _dot_tn(a_ref[...].astype(BF16), b_ref[...].astype(BF16))

        @pl.when(kk == steps - 1)
        def _():
            if blocks is None:
                o_ref[...] = acc_sc[...]
            else:
                for s in range(per):
                    o_ref[s] = acc_sc[:, s * width:(s + 1) * width]

    if blocks is None:
        o_spec = pl.BlockSpec((bm, bn), lambda i, j, kk: (i, j))
        o_shape = (m, n)
    else:
        o_spec = pl.BlockSpec((per, bm, width), lambda i, j, kk: (j, i, 0))
        o_shape = (blocks, m, width)
    return pl.pallas_call(
        body, name=name, grid=(m // bm, n // bn, steps),
        out_shape=jax.ShapeDtypeStruct(o_shape, F32),
        in_specs=[pl.BlockSpec((tk, bm), lambda i, j, kk: (kk, i)), pl.BlockSpec((tk, bn), lambda i, j, kk: (kk, j))],
        out_specs=o_spec,
        scratch_shapes=[pltpu.VMEM((bm, bn), F32)],
        compiler_params=_cparams("parallel", "parallel", "arbitrary"),
    )(a, b)


def _dw_in(dfq, dfk, dfv, drest, h1):
    t = h1.shape[0]
    tk = min(t, 1024)
    off_ff = 3 * FOX_WIDTH
    off_cq = off_ff + HEADS
    off_kr = IN_COLS - ROPE

    def body(dq_ref, dk_ref, dv_ref, dr_ref, h_ref, o_ref):
        h = h_ref[...]
        r = _dot_tn(dr_ref[...], h)
        parts = [(slice(n * FOX_WIDTH, (n + 1) * FOX_WIDTH), _dot_tn(ref[...], h)) for n, ref in enumerate((dq_ref, dk_ref, dv_ref))]
        parts += [(slice(off_ff, off_cq), r[0:HEADS]), (slice(off_cq, off_kr), r[REST_CQ:REST_KR]),
                  (slice(off_kr, IN_COLS), r[REST_KR + NOPE:REST_KR + NOPE + ROPE])]

        @pl.when(pl.program_id(0) == 0)
        def _():
            for rows, val in parts:
                o_ref[rows, :] = val

        @pl.when(pl.program_id(0) > 0)
        def _():
            for rows, val in parts:
                o_ref[rows, :] += val

    tok = lambda n: pl.BlockSpec((tk, n), lambda kk: (kk, 0))
    return pl.pallas_call(
        body, name="dw_in", grid=(t // tk,),
        out_shape=jax.ShapeDtypeStruct((IN_COLS, D_MODEL), F32),
        in_specs=[tok(FOX_WIDTH), tok(FOX_WIDTH), tok(FOX_WIDTH), tok(REST_COLS), tok(D_MODEL)],
        out_specs=pl.BlockSpec((IN_COLS, D_MODEL), lambda kk: (0, 0)),
        compiler_params=_cparams("arbitrary"),
    )(dfq, dfk, dfv, drest, h1)


def _attn_out_bwd(dx1, fox_o, mla_o, gf, gm, w_o):
    t = dx1.shape[0]
    tm = _row_tile(t)

    def body(dx_ref, f_ref, m_ref, gf_ref, gm_ref, w_ref, df_ref, dm_ref, dlf_ref, dlm_ref, dgf_ref, dgm_ref):
        @pl.when(pl.program_id(0) == 0)
        def _():
            dgf_ref[...] = jnp.zeros_like(dgf_ref)
            dgm_ref[...] = jnp.zeros_like(dgm_ref)
        dxb = dx_ref[...].astype(BF16)
        lane = lax.broadcasted_iota(jnp.int32, (8, LANES), 1)
        upper = lax.broadcasted_iota(jnp.int32, (8, LANES), 0) < 4
        pick = jnp.where(upper, (lane < HEAD_DIM).astype(F32), (lane >= HEAD_DIM).astype(F32)).astype(BF16)
        for o_ref, g_ref, lo_row, d_ref, dl_ref, dg_ref in ((f_ref, gf_ref, 0, df_ref, dlf_ref, dgf_ref),
                                                             (m_ref, gm_ref, FOX_WIDTH, dm_ref, dlm_ref, dgm_ref)):
            dn = _dot_nt(dxb, w_ref[lo_row:lo_row + FOX_WIDTH, :])
            o = o_ref[...]
            _, r = _rms(o, g_ref[...])
            d, dg = _rms_bwd(o, g_ref[...], r, dn)
            d_ref[...] = d
            dg_ref[...] += dg
            prod = d * o
            for pr in range(HEADS // 2):
                parts = _split3(prod[:, pr * LANES:(pr + 1) * LANES])
                both = (_dot_nt(pick, parts[0]) + _dot_nt(pick, parts[1])) + _dot_nt(pick, parts[2])
                dl_ref[2 * pr, 0] = both
                dl_ref[2 * pr + 1, 0] = pltpu.roll(both, 4, 0)

    row = lambda n: pl.BlockSpec((tm, n), lambda i: (i, 0))
    full = lambda a: pl.BlockSpec(a.shape, lambda i: (0,) * a.ndim)
    vec = pl.BlockSpec((1, FOX_WIDTH), lambda i: (0, 0))
    rows = pl.BlockSpec((HEADS, 1, 8, tm), lambda i: (0, i, 0, 0))
    o_shape = jax.ShapeDtypeStruct((t, FOX_WIDTH), F32)
    g_shape = jax.ShapeDtypeStruct((1, FOX_WIDTH), F32)
    r_shape = jax.ShapeDtypeStruct((HEADS, t // tm, 8, tm), F32)
    return pl.pallas_call(
        body, name="attn_out_bwd", grid=(t // tm,),
        out_shape=(o_shape, o_shape, r_shape, r_shape, g_shape, g_shape),
        in_specs=[row(D_MODEL), row(FOX_WIDTH), row(MLA_WIDTH), full(gf), full(gm), full(w_o)],
        out_specs=(row(FOX_WIDTH), row(MLA_WIDTH), rows, rows, vec, vec),
        compiler_params=_cparams("arbitrary"),
    )(dx1, fox_o, mla_o, gf, gm, w_o)


def _mla_prep_bwd(dq, dkv, dkr, dz, rest, gq, gkv, wq, wkv, cos, sin):
    t = rest.shape[0]
    tm = _row_tile(t)

    def body(dq_ref, dkv_ref, dkr_ref, dz_ref, r_ref, gq_ref, gkv_ref, wq_ref, wkv_ref, c_ref, s_ref,
             dr_ref, dqp_ref, dkvb_ref, dgq_ref, dgkv_ref):
        @pl.when(pl.program_id(0) == 0)
        def _():
            dgq_ref[...] = jnp.zeros_like(dgq_ref)
            dgkv_ref[...] = jnp.zeros_like(dgkv_ref)
        cos_, sin_ = c_ref[...], s_ref[...]
        dcq = jnp.zeros((tm, Q_RANK), F32)
        dckv = jnp.zeros((tm, KV_RANK), F32)
        for h in range(HEADS):
            dqp = _rope_bwd(dq_ref[h], cos_, sin_).astype(BF16)
            dqp_ref[:, h * LANES:(h + 1) * LANES] = dqp
            dcq = dcq + _dot_nt(dqp, wq_ref[h])
            dkvb = dkv_ref[h].astype(BF16)
            dkvb_ref[:, h * LANES:(h + 1) * LANES] = dkvb
            dckv = dckv + _dot_nt(dkvb, wkv_ref[h])
        dkrope = dkr_ref[0]
        for pr in range(1, HEADS // 2):
            dkrope = dkrope + dkr_ref[pr]
        cq = r_ref[:, REST_CQ:REST_CKV]
        _, rq = _rms(cq, gq_ref[...])
        d_cq, dgq = _rms_bwd(cq, gq_ref[...], rq, dcq)
        ckv = r_ref[:, REST_CKV:REST_KR]
        _, rkv = _rms(ckv, gkv_ref[...])
        d_ckv, dgkv = _rms_bwd(ckv, gkv_ref[...], rkv, dckv)
        dgq_ref[...] += dgq
        dgkv_ref[...] += dgkv
        dr_ref[:, 0:REST_CQ] = dz_ref[...].astype(BF16)
        dr_ref[:, REST_CQ:REST_CKV] = d_cq.astype(BF16)
        dr_ref[:, REST_CKV:REST_KR] = d_ckv.astype(BF16)
        dr_ref[:, REST_KR:REST_COLS] = _rope_bwd(dkrope, cos_, sin_).astype(BF16)

    row = lambda n: pl.BlockSpec((tm, n), lambda i: (i, 0))
    full = lambda a: pl.BlockSpec(a.shape, lambda i: (0,) * a.ndim)
    heads = pl.BlockSpec((HEADS, tm, LANES), lambda i: (0, i, 0))
    hshape = jax.ShapeDtypeStruct((t, HEADS * LANES), BF16)
    return pl.pallas_call(
        body, name="mla_prep_bwd", grid=(t // tm,),
        out_shape=(jax.ShapeDtypeStruct((t, REST_COLS), BF16), hshape, hshape,
                   jax.ShapeDtypeStruct((1, Q_RANK), F32), jax.ShapeDtypeStruct((1, KV_RANK), F32)),
        in_specs=[heads, heads, pl.BlockSpec((HEADS // 2, tm, LANES), lambda i: (0, i, 0)), row(LANES), row(REST_COLS),
                  full(gq), full(gkv), full(wq), full(wkv), row(LANES), row(LANES)],
        out_specs=(row(REST_COLS), row(HEADS * LANES), row(HEADS * LANES), pl.BlockSpec((1, Q_RANK), lambda i: (0, 0)),
                   pl.BlockSpec((1, KV_RANK), lambda i: (0, 0))),
        compiler_params=_cparams("arbitrary"),
    )(dq, dkv, dkr, dz, rest, gq, gkv, wq, wkv, cos, sin)


def _in_proj_bwd(x, g, dx1, dfq, dfk, dfv, drest, w_qkv, w_rest, comm=None):
    t = x.shape[0]
    tm = _row_tile(t)

    def main(ins, outs, scr):
        x_ref, g_ref, dx1_ref, dq_ref, dk_ref, dv_ref, dr_ref, wq_ref, wr_ref = ins
        dx_ref, dg_ref = outs

        @pl.when(pl.program_id(0) == 0)
        def _():
            dg_ref[...] = jnp.zeros_like(dg_ref)
        dh = _dot(dr_ref[...], wr_ref[...])
        for n, ref in enumerate((dq_ref, dk_ref, dv_ref)):
            dh = dh + _dot(ref[...], wq_ref[n * FOX_WIDTH:(n + 1) * FOX_WIDTH, :])
        xv = x_ref[...]
        _, r = _rms(xv, g_ref[...])
        dx, dg = _rms_bwd(xv, g_ref[...], r, dh)
        dx_ref[...] = dx1_ref[...] + dx
        dg_ref[...] += dg

    row = lambda n: pl.BlockSpec((tm, n), lambda i: (i, 0))
    full = lambda a: pl.BlockSpec(a.shape, lambda i: (0,) * a.ndim)
    vec = pl.BlockSpec((1, D_MODEL), lambda i: (0, 0))
    return _hosted_call(
        "in_proj_bwd", main, (t // tm,), [x, g, dx1, dfq, dfk, dfv, drest, w_qkv, w_rest],
        [row(D_MODEL), full(g), row(D_MODEL), row(FOX_WIDTH), row(FOX_WIDTH), row(FOX_WIDTH), row(REST_COLS),
         full(w_qkv), full(w_rest)],
        (jax.ShapeDtypeStruct((t, D_MODEL), F32), jax.ShapeDtypeStruct((1, D_MODEL), F32)), (row(D_MODEL), vec), [], comm)


def _pad_cols(a, n):
    return jnp.pad(a, ((0, 0),) * (a.ndim - 1) + ((0, n - a.shape[-1]),))


def kernel(x, positions, attn_norm_g, w_in, b_forget, q_norm_g, w_uq, kv_norm_g, w_ukv, fox_out_g, mla_out_g, w_o, mlp_norm_g, w_up, w_down, final_norm_g, loss_target, m_attn_norm_g, m_w_in, m_b_forget, m_q_norm_g, m_w_uq, m_kv_norm_g, m_w_ukv, m_fox_out_g, m_mla_out_g, m_w_o, m_mlp_norm_g, m_w_up, m_w_down, m_final_norm_g, v_attn_norm_g, v_w_in, v_b_forget, v_q_norm_g, v_w_uq, v_kv_norm_g, v_w_ukv, v_fox_out_g, v_mla_out_g, v_w_o, v_mlp_norm_g, v_w_up, v_w_down, v_final_norm_g):
    t = x.shape[1]
    tq = _row_tile(t)
    xs = x[0]
    target = loss_target[0]

    mid = [_pad_cols(w_uq[0], LANES).astype(BF16), w_ukv[0].astype(BF16)]
    late = [w_o[0].astype(BF16), w_up[0].astype(BF16), w_down[0].astype(BF16)]
    g_in, = _all_gather([jnp.transpose(w_in[0]).astype(BF16)])
    win = g_in.reshape(IN_COLS, D_MODEL)
    off_ff, off_cq, off_kr = 3 * FOX_WIDTH, 3 * FOX_WIDTH + HEADS, IN_COLS - ROPE
    zeros = lambda n: jnp.zeros((n, D_MODEL), BF16)
    w_qkv = win[:off_ff]
    w_rest = jnp.concatenate([
        win[off_ff:off_cq], zeros(REST_CQ - HEADS), win[off_cq:off_kr],
        zeros(NOPE), win[off_kr:], zeros(LANES - NOPE - ROPE)], axis=0)

    cos, sin = _rope_tables(positions.reshape(t, 1))
    (h1, fq, fk, fv, rest), (wq, wkv) = _in_proj(xs, attn_norm_g, w_qkv, w_rest, comm=_ag_to_all(mid))
    b128 = _pad_cols(b_forget, LANES)
    f2_rows, f2_rep = _forget_cumsum(rest, b128)
    f2_rows = f2_rows.reshape(HEADS, t // tq, 1, tq)
    (fox_o, fox_lse_rows), partly = _attn_fwd(True, fq, fk, fv, f2_rows, comm=_ag_direct(late))
    mq, mk, mkv, cqn, ckvn = _mla_prep(rest, q_norm_g, kv_norm_g, wq, wkv, cos, sin)
    (mla_o, mla_lse_rows), (g_o, g_up, g_down) = _attn_fwd(False, mq, mk, mkv, comm=_ag_forward(partly))
    wo = g_o.reshape(D_MODEL, D_MODEL)
    x1, mixed = _attn_out(xs, fox_o, mla_o, fox_out_g, mla_out_g, wo)
    wup = jnp.transpose(g_up, (1, 0, 2)).reshape(D_MODEL, D_FF)
    wdown = g_down.reshape(D_FF, D_MODEL)
    u, h2, dx2, dx2b, loss8, d_gfin = _mlp_fwd(x1, mlp_norm_g, wup, wdown, final_norm_g.reshape(1, D_MODEL), target)

    du, act, dx1, dx1b, d_gmlp = _mlp_bwd(dx2, u, x1, mlp_norm_g, wup, wdown)
    dw_down = _matmul_tn("dw_down", act, dx2b)
    dw_up = _matmul_tn("dw_up", h2, du, blocks=N_DEV)
    dfox_o, dmla_o, fox_delta_rows, mla_delta_rows, d_gfox, d_gmla = _attn_out_bwd(dx1, fox_o, mla_o, fox_out_g, mla_out_g, wo)
    dw_o = _matmul_tn("dw_o", mixed, dx1b)

    place = jnp.stack([lax.axis_index("c"), 2 * lax.axis_index("x") + lax.axis_index("y")]).astype(jnp.int32)
    names = ("w_in", "w_uq", "w_ukv", "w_o", "w_up", "w_down")
    grads_b = [dw_o.reshape(N_DEV, -1, D_MODEL), dw_up, dw_down.reshape(N_DEV, -1, D_MODEL)]
    (dfq, dfk, dfv, d_fq, d_fk), got_b = _attn_bwd(True, fq, fk, fv, dfox_o, fox_lse_rows, fox_delta_rows,
                                                   f2_rep, comm=_rs_to_sibling(grads_b))
    sums_b = [_rs_sibling_sum("rs_sibling_sum_" + nm, g, l, place) for nm, g, l in zip(names[3:], grads_b, got_b)]
    dz, d_b = _forget_bwd(rest, b128, d_fq, d_fk)
    (dmq, dmkv, dmkr), others_b = _attn_bwd(False, mq, mk, mkv, dmla_o, mla_lse_rows, mla_delta_rows,
                                            comm=_rs_to_chips([s[1] for s in sums_b]))
    drest, dqp, dkvb, d_gq, d_gkv = _mla_prep_bwd(dmq, dmkv, dmkr, dz, rest, q_norm_g, kv_norm_g, wq, wkv, cos, sin)
    dw_uq = _matmul_tn("dw_uq", cqn, dqp, blocks=HEADS)
    dw_ukv = _matmul_tn("dw_ukv", ckvn, dkvb, blocks=HEADS)
    dw_in = _dw_in(dfq, dfk, dfv, drest, h1)

    grads_a = [dw_in.reshape(N_DEV, IN_SHARD, D_MODEL), dw_uq, dw_ukv]
    got_a = _comm_call("rs_sibling_exchange", _rs_to_sibling(grads_a))
    sums_a = [_rs_sibling_sum("rs_sibling_sum_" + nm, g, l, place) for nm, g, l in zip(names[:3], grads_a, got_a)]
    (grad_x, d_gattn), others_a = _in_proj_bwd(xs, attn_norm_g, dx1, dfq, dfk, dfv, drest, w_qkv, w_rest,
                                               comm=_rs_to_chips([s[1] for s in sums_a]))
    sums, others = sums_a + sums_b, list(others_a) + list(others_b)
    sharded = (w_in, w_uq, w_ukv, w_o, w_up, w_down)
    moments_m = (m_w_in, m_w_uq, m_w_ukv, m_w_o, m_w_up, m_w_down)
    moments_v = (v_w_in, v_w_uq, v_w_ukv, v_w_o, v_w_up, v_w_down)
    g_in_t = _rs_final_sum("rs_final_sum_w_in", sums[0][0], others[0])
    big = [_adamw_given("adamw_w_in", jnp.transpose(g_in_t), w_in, m_w_in, v_w_in)]
    for a in range(1, len(names)):
        big.append(_adamw_sharded("adamw_" + names[a], sharded[a], moments_m[a], moments_v[a], sums[a][0], others[a]))
    big_g, big_d, big_m, big_v = [[b[k] for b in big] for k in range(4)]

    as_row = lambda a: a.reshape(1, -1)
    small_w = (attn_norm_g, b_forget, q_norm_g, kv_norm_g, fox_out_g, mla_out_g, mlp_norm_g, final_norm_g)
    small_m = (m_attn_norm_g, m_b_forget, m_q_norm_g, m_kv_norm_g, m_fox_out_g, m_mla_out_g, m_mlp_norm_g, m_final_norm_g)
    small_v = (v_attn_norm_g, v_b_forget, v_q_norm_g, v_kv_norm_g, v_fox_out_g, v_mla_out_g, v_mlp_norm_g, v_final_norm_g)
    total = _small_all_reduce([d_gattn, d_b, d_gq, d_gkv, d_gfox, d_gmla, d_gmlp, d_gfin], loss8)
    small = _adamw_small(total, [as_row(a) for a in small_w], [as_row(a) for a in small_m], [as_row(a) for a in small_v])
    loss = small[0].reshape(())
    s_g, s_d, s_m, s_v = [[small[1 + 4 * r + k].reshape(small_w[r].shape) for r in range(len(small_w))] for k in range(4)]

    def ordered(small_, bigs):
        ga, bf, gq_, gkv_, gfo, gml, gmlp_, gfin_ = small_
        bin_, buq, bukv, bo, bup, bdown = bigs
        return [ga, bin_, bf, gq_, buq, gkv_, bukv, gfo, gml, bo, gmlp_, bup, bdown, gfin_]

    return (loss, grad_x[None], *ordered(s_g, big_g), *ordered(s_d, big_d), *ordered(s_m, big_m), *ordered(s_v, big_v))
```

```python
import math
from typing import Callable, NamedTuple

import numpy as np
import jax
import jax.numpy as jnp
from jax import lax
from jax.experimental import pallas as pl
from jax.experimental.pallas import tpu as pltpu

F32 = jnp.float32
BF16 = jnp.bfloat16
MESH = pl.DeviceIdType.MESH

D_MODEL = 1024
HEADS = 8
HEAD_DIM = 64
FOX_WIDTH = 512
MLA_WIDTH = 512
NOPE = 64
ROPE = 32
QK_DIM = 96
Q_RANK = 384
KV_RANK = 256
D_FF = 4096
IN_COLS = 2216
ROPE_THETA = 10000.0
EPS = 1e-6
FOX_SCALE = 1.0 / math.sqrt(HEAD_DIM)
MLA_SCALE = 1.0 / math.sqrt(QK_DIM)
ADAM_LR = 0.001
ADAM_B1 = 0.9
ADAM_B2 = 0.999
ADAM_EPS = 1e-08
ADAM_WD = 0.01
ADAM_STEP = 10

N_DEV = 8
LANES = 128
REST_COLS = 896
REST_CQ = LANES
REST_CKV = REST_CQ + Q_RANK
REST_KR = REST_CKV + KV_RANK
LOG2E = 1.4426950408889634
LN2 = 0.6931471805599453
FOX_Q_FACTOR = FOX_SCALE * LOG2E
MLA_Q_FACTOR = MLA_SCALE * LOG2E
VMEM_LIMIT = 56 * 1024 * 1024

IN_SHARD = IN_COLS // N_DEV
SMALL_SIZES = (1024, 8, 384, 256, 512, 512, 1024, 1024)
SMALL_ROWS = 16
LOSS_ROW = len(SMALL_SIZES)


def _cparams(*sem):
    return pltpu.CompilerParams(dimension_semantics=sem or None, vmem_limit_bytes=VMEM_LIMIT)


def _row_tile(t):
    return 512 if t >= 2048 else (256 if t >= 512 else 128)


def _dot(a, b):
    return jnp.dot(a, b, preferred_element_type=F32)


def _dot_nt(a, b):
    return lax.dot_general(a, b, (((1,), (1,)), ((), ())), preferred_element_type=F32)


def _dot_tn(a, b):
    return lax.dot_general(a, b, (((0,), (0,)), ((), ())), preferred_element_type=F32)


def _rms(x, g):
    r = lax.rsqrt(jnp.mean(x * x, axis=-1, keepdims=True) + EPS)
    return x * r * g, r


def _rms_bwd(x, g, r, dy):
    xh = x * r
    gdy = dy * g
    dx = r * (gdy - xh * jnp.mean(gdy * xh, axis=-1, keepdims=True))
    return dx, jnp.sum(dy * xh, axis=0, keepdims=True)


def _lane():
    return lax.broadcasted_iota(jnp.int32, (1, LANES), 1)


def _rot(x):
    lane = _lane()
    half = NOPE + ROPE // 2
    first = jnp.logical_and(lane >= NOPE, lane < half)
    second = jnp.logical_and(lane >= half, lane < NOPE + ROPE)
    return jnp.where(first, -pltpu.roll(x, LANES - ROPE // 2, 1), jnp.where(second, pltpu.roll(x, ROPE // 2, 1), 0.0))


def _rope(x, cos, sin):
    return x * cos + _rot(x) * sin


def _rope_bwd(dy, cos, sin):
    return dy * cos - _rot(dy * sin)


def _remote(src, dst, send_sem, recv_sem, to):
    return pltpu.make_async_remote_copy(src_ref=src, dst_ref=dst, send_sem=send_sem, recv_sem=recv_sem,
                                        device_id=to, device_id_type=MESH)


def _hbm_specs(n):
    return [pl.BlockSpec(memory_space=pl.ANY)] * n


def _all_gather(blocks):
    n = len(blocks)

    def body(*refs):
        x_refs, out_refs = refs[:n], refs[n:2 * n]
        send_sems, recv_sems, local_sems = refs[2 * n:]
        x, y, c = lax.axis_index("x"), lax.axis_index("y"), lax.axis_index("c")
        me, sibling = (x, y, c), (x, y, 1 - c)
        chips = [(1 - x, y), (x, 1 - y), (1 - x, 1 - y)]

        def slot(a, px, py, pc):
            return out_refs[a].at[4 * px + 2 * py + pc]

        def copy(a, k, blk, to, src=None):
            return _remote(slot(a, *blk) if src is None else src, slot(a, *blk),
                           send_sems.at[7 * a + k], recv_sems.at[7 * a + k], to)

        mine = [pltpu.make_async_copy(x_refs[a], slot(a, *me), local_sems.at[a]) for a in range(n)]
        first, passed = [], []
        for a in range(n):
            mine[a].start()
            first.append(copy(a, 0, me, sibling, src=x_refs[a]))
            first += [copy(a, 1 + j, me, (*chip, c), src=x_refs[a]) for j, chip in enumerate(chips)]
        for cp in first:
            cp.start()
        for a in range(n):
            for j, chip in enumerate(chips):
                copy(a, 1 + j, (*chip, c), me).wait_recv()
                passed.append(copy(a, 4 + j, (*chip, c), sibling))
                passed[-1].start()
        for a in range(n):
            copy(a, 0, sibling, me).wait_recv()
            for j, chip in enumerate(chips):
                copy(a, 4 + j, (*chip, 1 - c), me).wait_recv()
        for cp in first + passed:
            cp.wait_send()
        for cp in mine:
            cp.wait()

    return pl.pallas_call(
        body, name="all_gather_weights",
        out_shape=[jax.ShapeDtypeStruct((N_DEV,) + b.shape, b.dtype) for b in blocks],
        in_specs=_hbm_specs(n), out_specs=_hbm_specs(n),
        scratch_shapes=[pltpu.SemaphoreType.DMA((7 * n,)), pltpu.SemaphoreType.DMA((7 * n,)), pltpu.SemaphoreType.DMA((n,))],
    )(*blocks)


def _symmetric_comm(inputs, out_shape, aliases, per_array, copies):
    def start(in_refs, out_refs, sems):
        for cp in copies(in_refs, out_refs, *sems):
            cp.start()

    def finish(in_refs, out_refs, sems):
        for cp in copies(in_refs, out_refs, *sems):
            cp.wait()

    n_sems = per_array * len(inputs)
    return _Comm(tuple(inputs), tuple(out_shape), aliases,
                 (pltpu.SemaphoreType.DMA((n_sems,)), pltpu.SemaphoreType.DMA((n_sems,))), start, finish)


def _ag_direct(shards):
    def copies(in_refs, out_refs, send_sems, recv_sems):
        x, y, c = lax.axis_index("x"), lax.axis_index("y"), lax.axis_index("c")
        peers = [(x, y, 1 - c), (1 - x, y, c), (x, 1 - y, c), (1 - x, 1 - y, c)]
        cps = []
        for a in range(len(shards)):
            mine = out_refs[a].at[4 * x + 2 * y + c]
            cps.append(pltpu.make_async_copy(in_refs[a], mine, send_sems.at[5 * a]))
            cps += [_remote(in_refs[a], mine, send_sems.at[5 * a + k], recv_sems.at[5 * a + k], peer)
                    for k, peer in enumerate(peers, start=1)]
        return cps

    return _symmetric_comm(shards, [jax.ShapeDtypeStruct((N_DEV,) + s.shape, s.dtype) for s in shards], {}, 5, copies)


def _ag_to_all(shards):
    def copies(in_refs, out_refs, send_sems, recv_sems):
        x, y, c = lax.axis_index("x"), lax.axis_index("y"), lax.axis_index("c")
        cps = []
        for a in range(len(shards)):
            mine = out_refs[a].at[4 * x + 2 * y + c]
            cps.append(pltpu.make_async_copy(in_refs[a], mine, send_sems.at[N_DEV * a]))
            for k in range(1, N_DEV):
                peer = (x ^ (k >> 2), y ^ ((k >> 1) & 1), c ^ (k & 1))
                cps.append(_remote(in_refs[a], mine, send_sems.at[N_DEV * a + k], recv_sems.at[N_DEV * a + k], peer))
        return cps

    return _symmetric_comm(shards, [jax.ShapeDtypeStruct((N_DEV,) + s.shape, s.dtype) for s in shards], {}, N_DEV, copies)


def _ag_forward(gathered):
    def copies(in_refs, out_refs, send_sems, recv_sems):
        x, y, c = lax.axis_index("x"), lax.axis_index("y"), lax.axis_index("c")
        chips = [(1 - x, y), (x, 1 - y), (1 - x, 1 - y)]
        return [_remote(out_refs[a].at[4 * cx + 2 * cy + c], out_refs[a].at[4 * cx + 2 * cy + c],
                        send_sems.at[3 * a + j], recv_sems.at[3 * a + j], (x, y, 1 - c))
                for a in range(len(gathered)) for j, (cx, cy) in enumerate(chips)]

    shapes = [jax.ShapeDtypeStruct(g.shape, g.dtype) for g in gathered]
    return _symmetric_comm(gathered, shapes, {a: a for a in range(len(gathered))}, 3, copies)


def _rs_to_sibling(grads):
    def copies(in_refs, out_refs, send_sems, recv_sems):
        x, y, c = lax.axis_index("x"), lax.axis_index("y"), lax.axis_index("c")
        return [_remote(in_refs[a].at[2 * q + 1 - c], out_refs[a].at[q], send_sems.at[4 * a + q], recv_sems.at[4 * a + q], (x, y, 1 - c))
                for a in range(len(grads)) for q in range(4)]

    return _symmetric_comm(grads, [jax.ShapeDtypeStruct((4,) + g.shape[1:], g.dtype) for g in grads], {}, 4, copies)


def _rs_to_chips(parts):
    def copies(in_refs, out_refs, send_sems, recv_sems):
        x, y, c = lax.axis_index("x"), lax.axis_index("y"), lax.axis_index("c")
        chips = [(1 - x, y), (x, 1 - y), (1 - x, 1 - y)]
        return [_remote(in_refs[a].at[2 * cx + cy], out_refs[a].at[k], send_sems.at[3 * a + k], recv_sems.at[3 * a + k], (cx, cy, c))
                for a in range(len(parts)) for k, (cx, cy) in enumerate(chips)]

    return _symmetric_comm(parts, [jax.ShapeDtypeStruct((3,) + p.shape[1:], p.dtype) for p in parts], {}, 3, copies)


def _comm_call(name, comm):
    n_in, n_out = len(comm.inputs), len(comm.out_shape)

    def body(*refs):
        ins, outs, sems = refs[:n_in], refs[n_in:n_in + n_out], refs[n_in + n_out:]
        comm.start(ins, outs, sems)
        comm.finish(ins, outs, sems)

    return pl.pallas_call(
        body, name=name, out_shape=list(comm.out_shape), in_specs=_hbm_specs(n_in), out_specs=_hbm_specs(n_out),
        scratch_shapes=list(comm.scratch), input_output_aliases=dict(comm.aliases),
    )(*comm.inputs)


def _small_all_reduce(parts, loss8):
    n = len(parts)

    def body(*refs):
        p_refs, loss_ref, out_ref, pack, land, send_sems, recv_sems = refs[:n], *refs[n:]
        x, y, c = lax.axis_index("x"), lax.axis_index("y"), lax.axis_index("c")
        me = 4 * x + 2 * y + c
        pack[...] = jnp.zeros_like(pack)
        for r, ref in enumerate(p_refs):
            pack[r:r + 1, 0:ref.shape[1]] = ref[...]
        pack[LOSS_ROW:LOSS_ROW + 1, 0:LANES] = loss_ref[0:1, :]
        land[me] = pack[...]
        cps = []
        for k in range(1, N_DEV):
            peer = (x ^ (k >> 2), y ^ ((k >> 1) & 1), c ^ (k & 1))
            cps.append(_remote(pack, land.at[me], send_sems.at[k - 1], recv_sems.at[k - 1], peer))
        for cp in cps:
            cp.start()
        for cp in cps:
            cp.wait()
        acc = land[0]
        for d in range(1, N_DEV):
            acc = acc + land[d]
        out_ref[...] = acc

    vmem = pl.BlockSpec(memory_space=pltpu.VMEM)
    return pl.pallas_call(
        body, name="small_all_reduce",
        out_shape=jax.ShapeDtypeStruct((SMALL_ROWS, D_MODEL), F32),
        in_specs=[vmem] * (n + 1), out_specs=vmem,
        scratch_shapes=[pltpu.VMEM((SMALL_ROWS, D_MODEL), F32), pltpu.VMEM((N_DEV, SMALL_ROWS, D_MODEL), F32),
                        pltpu.SemaphoreType.DMA((N_DEV - 1,)), pltpu.SemaphoreType.DMA((N_DEV - 1,))],
    )(*parts, loss8)


def _rs_sibling_sum(name, grad, got, place):
    _, rows, cols = grad.shape

    def body(place_ref, g_ref, l_ref, own_ref, b_ref):
        s = g_ref[...] + l_ref[...]
        b_ref[...] = s.astype(BF16)

        @pl.when(pl.program_id(0) == place_ref[1])
        def _():
            own_ref[...] = s

    by_chip = pl.BlockSpec((None, rows, cols), lambda q, place_ref: (q, 0, 0))
    return pl.pallas_call(
        body, name=name,
        grid_spec=pltpu.PrefetchScalarGridSpec(
            num_scalar_prefetch=1, grid=(4,),
            in_specs=[pl.BlockSpec((None, rows, cols), lambda q, place_ref: (2 * q + place_ref[0], 0, 0)), by_chip],
            out_specs=[pl.BlockSpec((rows, cols), lambda q, place_ref: (0, 0)), by_chip]),
        out_shape=(jax.ShapeDtypeStruct((rows, cols), F32), jax.ShapeDtypeStruct((4, rows, cols), BF16)),
        compiler_params=_cparams("arbitrary"),
    )(place, grad, got)


def _adamw_math(w, g, m, v):
    m2 = ADAM_B1 * m + (1.0 - ADAM_B1) * g
    v2 = ADAM_B2 * v + (1.0 - ADAM_B2) * (g * g)
    m_hat = m2 / (1.0 - ADAM_B1 ** ADAM_STEP)
    v_hat = v2 / (1.0 - ADAM_B2 ** ADAM_STEP)
    delta = -ADAM_LR * (m_hat / (jnp.sqrt(v_hat) + ADAM_EPS) + ADAM_WD * w)
    return delta, m2, v2


def _update_tile(rows):
    return 256 if rows % 256 == 0 else rows


def _rs_final_sum(name, own, got):
    def body(o_ref, r_ref, g_out):
        g = o_ref[...]
        for k in range(3):
            g = g + r_ref[k].astype(F32)
        g_out[...] = g

    return pl.pallas_call(body, name=name, out_shape=jax.ShapeDtypeStruct(own.shape, F32))(own, got)


def _adamw_sharded(name, w, m, v, own, got):
    _, rows, cols = w.shape
    tr = _update_tile(rows)

    def body(o_ref, r_ref, w_ref, m_ref, v_ref, g_out, d_out, m_out, v_out):
        g = o_ref[:, 0:cols]
        for k in range(3):
            g = g + r_ref[k, :, 0:cols].astype(F32)
        d, m2, v2 = _adamw_math(w_ref[0], g, m_ref[0], v_ref[0])
        g_out[0] = g
        d_out[0] = d
        m_out[0] = m2
        v_out[0] = v2

    mine = pl.BlockSpec((1, tr, cols), lambda i: (0, i, 0))
    shp = jax.ShapeDtypeStruct(w.shape, F32)
    wide = own.shape[1]
    return pl.pallas_call(
        body, name=name, grid=(rows // tr,), out_shape=(shp,) * 4,
        in_specs=[pl.BlockSpec((tr, wide), lambda i: (i, 0)), pl.BlockSpec((3, tr, wide), lambda i: (0, i, 0)),
                  mine, mine, mine],
        out_specs=[mine] * 4,
        compiler_params=_cparams("parallel"),
    )(own, got, w, m, v)


def _adamw_given(name, g, w, m, v):
    _, rows, cols = w.shape
    tr = _update_tile(rows)

    def body(g_ref, w_ref, m_ref, v_ref, g_out, d_out, m_out, v_out):
        g = g_ref[...]
        d, m2, v2 = _adamw_math(w_ref[0], g, m_ref[0], v_ref[0])
        g_out[0] = g
        d_out[0] = d
        m_out[0] = m2
        v_out[0] = v2

    own = pl.BlockSpec((1, tr, cols), lambda i: (0, i, 0))
    shp = jax.ShapeDtypeStruct(w.shape, F32)
    return pl.pallas_call(
        body, name=name, grid=(rows // tr,), out_shape=(shp,) * 4,
        in_specs=[pl.BlockSpec((tr, cols), lambda i: (i, 0)), own, own, own], out_specs=[own] * 4,
        compiler_params=_cparams("parallel"),
    )(g, w, m, v)


def _adamw_small(total, ws, ms, vs):
    n = len(ws)

    def body(*refs):
        t_ref = refs[0]
        w_refs, m_refs, v_refs = refs[1:1 + n], refs[1 + n:1 + 2 * n], refs[1 + 2 * n:1 + 3 * n]
        outs = refs[1 + 3 * n:]
        outs[0][...] = t_ref[LOSS_ROW:LOSS_ROW + 1, 0:1]
        for r in range(n):
            g = t_ref[r:r + 1, 0:w_refs[r].shape[1]]
            d, m2, v2 = _adamw_math(w_refs[r][...], g, m_refs[r][...], v_refs[r][...])
            for k, val in enumerate((g, d, m2, v2)):
                outs[1 + 4 * r + k][...] = val

    vmem = pl.BlockSpec(memory_space=pltpu.VMEM)
    out_shape = [jax.ShapeDtypeStruct((1, 1), F32)]
    for w in ws:
        out_shape += [jax.ShapeDtypeStruct(w.shape, F32)] * 4
    return pl.pallas_call(
        body, name="adamw_small", out_shape=out_shape,
        in_specs=[vmem] * (1 + 3 * n), out_specs=[vmem] * len(out_shape),
    )(total, *ws, *ms, *vs)


def _rope_tables(pos_col):
    t = pos_col.shape[0]
    inv = (np.float32(ROPE_THETA) ** (-np.arange(0, ROPE, 2, dtype=np.float32) / np.float32(ROPE))).astype(np.float32)
    freq = np.zeros((1, LANES), np.float32)
    freq[0, NOPE:NOPE + ROPE // 2] = inv
    freq[0, NOPE + ROPE // 2:NOPE + ROPE] = inv
    tm = _row_tile(t)

    def body(p_ref, f_ref, c_ref, s_ref):
        ang = p_ref[...].astype(F32) * f_ref[...]
        c_ref[...] = jnp.cos(ang)
        s_ref[...] = jnp.sin(ang)

    shp = jax.ShapeDtypeStruct((t, LANES), F32)
    return pl.pallas_call(
        body, name="rope_tables", grid=(t // tm,), out_shape=(shp, shp),
        in_specs=[pl.BlockSpec((tm, 1), lambda i: (i, 0)), pl.BlockSpec((1, LANES), lambda i: (0, 0))],
        out_specs=(pl.BlockSpec((tm, LANES), lambda i: (i, 0)),) * 2,
        compiler_params=_cparams("parallel"),
    )(pos_col, jnp.asarray(freq))


def _in_proj(x, g, w_qkv, w_rest, comm=None):
    t = x.shape[0]
    tm = _row_tile(t)

    def main(ins, outs, scr):
        x_ref, g_ref, wq_ref, wr_ref = ins
        h_ref, fq_ref, fk_ref, fv_ref, r_ref = outs
        h, _ = _rms(x_ref[...], g_ref[...])
        hb = h.astype(BF16)
        h_ref[...] = hb
        for n, (ref, factor) in enumerate(((fq_ref, FOX_Q_FACTOR), (fk_ref, None), (fv_ref, None))):
            part = _dot_nt(hb, wq_ref[n * FOX_WIDTH:(n + 1) * FOX_WIDTH, :])
            ref[...] = (part if factor is None else part * factor).astype(BF16)
        r_ref[...] = _dot_nt(hb, wr_ref[...])

    row = lambda n: pl.BlockSpec((tm, n), lambda i: (i, 0))
    full = lambda a: pl.BlockSpec(a.shape, lambda i: (0,) * a.ndim)
    return _hosted_call(
        "in_proj", main, (t // tm,), [x, g, w_qkv, w_rest], [row(D_MODEL), full(g), full(w_qkv), full(w_rest)],
        (jax.ShapeDtypeStruct((t, D_MODEL), BF16),) + (jax.ShapeDtypeStruct((t, FOX_WIDTH), BF16),) * 3
        + (jax.ShapeDtypeStruct((t, REST_COLS), F32),),
        (row(D_MODEL), row(FOX_WIDTH), row(FOX_WIDTH), row(FOX_WIDTH), row(REST_COLS)), [], comm)


def _log_sigmoid(z):
    return jnp.minimum(z, 0.0) - jnp.log(1.0 + jnp.exp(-jnp.abs(z)))


def _split3(v):
    hi = v.astype(BF16)
    r1 = v - hi.astype(F32)
    mid = r1.astype(BF16)
    lo = (r1 - mid.astype(F32)).astype(BF16)
    return hi, mid, lo


def _scan_tile(t):
    return 512 if t >= 2048 else (256 if t >= 256 else t)


def _forget_cumsum(rest, b128):
    t = rest.shape[0]
    tb = _scan_tile(t)

    def body(r_ref, b_ref, row_ref, rep_ref, f_sc, carry):
        @pl.when(pl.program_id(0) == 0)
        def _():
            carry[...] = jnp.zeros_like(carry)
        lf = _log_sigmoid(r_ref[...] + b_ref[...])
        tri = (lax.broadcasted_iota(jnp.int32, (tb, tb), 0) >= lax.broadcasted_iota(jnp.int32, (tb, tb), 1)).astype(BF16)
        hi, mid, lo = _split3(lf)
        f_sc[...] = (_dot(tri, hi) + _dot(tri, mid)) + _dot(tri, lo) + carry[...]
        carry[...] = f_sc[tb - 1:tb, :]
        f2 = f_sc[...] * LOG2E
        row_ref[...] = jnp.transpose(f2)[0:HEADS, :]
        lane = _lane()
        for h in range(HEADS):
            col = jnp.sum(jnp.where(lane == h, f2, 0.0), axis=1, keepdims=True)
            rep_ref[h] = jnp.broadcast_to(col, (tb, LANES))

    return pl.pallas_call(
        body, name="forget_cumsum", grid=(t // tb,),
        out_shape=(jax.ShapeDtypeStruct((HEADS, t), F32), jax.ShapeDtypeStruct((HEADS, t, LANES), F32)),
        in_specs=[pl.BlockSpec((tb, LANES), lambda i: (i, 0)), pl.BlockSpec((1, LANES), lambda i: (0, 0))],
        out_specs=(pl.BlockSpec((HEADS, tb), lambda i: (0, i)), pl.BlockSpec((HEADS, tb, LANES), lambda i: (0, i, 0))),
        scratch_shapes=[pltpu.VMEM((tb, LANES), F32), pltpu.VMEM((1, LANES), F32)],
        compiler_params=_cparams("arbitrary"),
    )(rest, b128)


def _forget_bwd(rest, b128, d_fq, d_fk):
    t = rest.shape[0]
    tb = _scan_tile(t)
    nb = t // tb

    def body(r_ref, b_ref, dfq_ref, dfk_ref, dz_ref, db_ref, carry):
        @pl.when(pl.program_id(0) == 0)
        def _():
            carry[...] = jnp.zeros_like(carry)
            db_ref[...] = jnp.zeros_like(db_ref)
        tri = (lax.broadcasted_iota(jnp.int32, (tb, tb), 0) <= lax.broadcasted_iota(jnp.int32, (tb, tb), 1)).astype(BF16)
        lane = _lane()
        df = jnp.zeros((tb, LANES), F32)
        for h in range(HEADS):
            df = df + jnp.where(lane == h, dfq_ref[h] + dfk_ref[h], 0.0)
        hi, mid, lo = _split3(df)
        dlf = (_dot(tri, hi) + _dot(tri, mid)) + _dot(tri, lo) + carry[...]
        z = r_ref[...] + b_ref[...]
        dz = dlf / (1.0 + jnp.exp(z))
        dz_ref[...] = dz
        db_ref[...] += jnp.sum(dz, axis=0, keepdims=True)
        carry[...] = carry[...] + jnp.sum(df, axis=0, keepdims=True)

    rev = lambda i: (nb - 1 - i, 0)
    rev3 = pl.BlockSpec((HEADS, tb, LANES), lambda i: (0, nb - 1 - i, 0))
    return pl.pallas_call(
        body, name="forget_bwd", grid=(nb,),
        out_shape=(jax.ShapeDtypeStruct((t, LANES), F32), jax.ShapeDtypeStruct((1, LANES), F32)),
        in_specs=[pl.BlockSpec((tb, LANES), rev), pl.BlockSpec((1, LANES), lambda i: (0, 0)), rev3, rev3],
        out_specs=(pl.BlockSpec((tb, LANES), rev), pl.BlockSpec((1, LANES), lambda i: (0, 0))),
        scratch_shapes=[pltpu.VMEM((1, LANES), F32)],
        compiler_params=_cparams("arbitrary"),
    )(rest, b128, d_fq, d_fk)


def _mla_prep(rest, gq, gkv, wq, wkv, cos, sin):
    t = rest.shape[0]
    tm = _row_tile(t)

    def body(r_ref, gq_ref, gkv_ref, wq_ref, wkv_ref, c_ref, s_ref, q_ref, k_ref, kv_ref, cq_ref, ckv_ref):
        cos_, sin_ = c_ref[...], s_ref[...]
        cq, _ = _rms(r_ref[:, REST_CQ:REST_CKV], gq_ref[...])
        ckv, _ = _rms(r_ref[:, REST_CKV:REST_KR], gkv_ref[...])
        cqb, ckvb = cq.astype(BF16), ckv.astype(BF16)
        cq_ref[...] = cqb
        ckv_ref[...] = ckvb
        k_rope = _rope(r_ref[:, REST_KR:REST_COLS], cos_, sin_)
        lo = _lane() < NOPE
        for h in range(HEADS):
            q_ref[h] = (_rope(_dot(cqb, wq_ref[h]), cos_, sin_) * MLA_Q_FACTOR).astype(BF16)
            kv = _dot(ckvb, wkv_ref[h])
            kv_ref[h] = kv.astype(BF16)
            k_ref[h] = (jnp.where(lo, kv, 0.0) + k_rope).astype(BF16)

    row = lambda n: pl.BlockSpec((tm, n), lambda i: (i, 0))
    full = lambda a: pl.BlockSpec(a.shape, lambda i: (0,) * a.ndim)
    heads = pl.BlockSpec((HEADS, tm, LANES), lambda i: (0, i, 0))
    hshape = jax.ShapeDtypeStruct((HEADS, t, LANES), BF16)
    return pl.pallas_call(
        body, name="mla_prep", grid=(t // tm,),
        out_shape=(hshape, hshape, hshape, jax.ShapeDtypeStruct((t, Q_RANK), BF16), jax.ShapeDtypeStruct((t, KV_RANK), BF16)),
        in_specs=[row(REST_COLS), full(gq), full(gkv), full(wq), full(wkv), row(LANES), row(LANES)],
        out_specs=(heads, heads, heads, row(Q_RANK), row(KV_RANK)),
        compiler_params=_cparams("parallel"),
    )(rest, gq, gkv, wq, wkv, cos, sin)


def _tile_lanes(x, n):
    return jnp.tile(x, (1, n)) if n > 1 else x


class _Comm(NamedTuple):
    inputs: tuple
    out_shape: tuple
    aliases: dict
    scratch: tuple
    start: Callable
    finish: Callable


def _hosted_call(name, main, grid, args, in_specs, out_shape, out_specs, scratch, comm):
    n_in, n_out, n_scr = len(args), len(out_shape), len(scratch)
    c_in = list(comm.inputs) if comm else []
    c_out = list(comm.out_shape) if comm else []

    def at_step(which):
        hit = pl.program_id(0) == which[0]
        for axis in range(1, len(grid)):
            hit = jnp.logical_and(hit, pl.program_id(axis) == which[axis])
        return hit

    def body(*refs):
        bounds = [0, n_in, len(c_in), n_out, len(c_out), n_scr]
        starts = [sum(bounds[:k + 1]) for k in range(len(bounds))]
        ins, cins, outs, couts, scr = [refs[a:b] for a, b in zip(starts[:-1], starts[1:])]
        sems = refs[starts[-1]:]
        if comm:
            @pl.when(at_step([0] * len(grid)))
            def _():
                comm.start(cins, couts, sems)
        main(ins, outs, scr)
        if comm:
            @pl.when(at_step([n - 1 for n in grid]))
            def _():
                comm.finish(cins, couts, sems)

    res = pl.pallas_call(
        body, name=name, grid=grid,
        out_shape=list(out_shape) + c_out,
        in_specs=list(in_specs) + _hbm_specs(len(c_in)),
        out_specs=list(out_specs) + _hbm_specs(len(c_out)),
        scratch_shapes=list(scratch) + (list(comm.scratch) if comm else []),
        input_output_aliases={n_in + i: n_out + o for i, o in comm.aliases.items()} if comm else {},
        compiler_params=_cparams(*(["arbitrary"] * len(grid))),
    )(*args, *c_in)
    return res[:n_out], res[n_out:]


def _stat_rows(x):
    return jnp.transpose(x)[0:8, :]


FWD_HEADS = 4


def _attn_fwd(fox, q, k, v, f2_rows=None, comm=None):
    t = q.shape[0] if fox else q.shape[1]
    tq = _row_tile(t)
    nq = t // tq
    nh = FWD_HEADS
    wide = (nh // 2) * LANES

    def main(ins, outs, scr):
        q_ref, k_ref, v_ref = ins[:3]
        fr_ref = ins[3] if fox else None
        o_ref, lset_ref = outs
        m_sc, acc_sc = scr
        i = pl.program_id(1)
        lo = _lane() < HEAD_DIM
        hi = jnp.logical_not(lo)
        zero, one = jnp.zeros((), BF16), jnp.ones((), BF16)
        lanes_of = lambda h: slice((h // 2) * LANES, (h // 2 + 1) * LANES)
        if fox:
            qs = [jnp.where(lo if h % 2 == 0 else hi, q_ref[:, lanes_of(h)], zero) for h in range(nh)]
            sum_lanes = [hi if h % 2 == 0 else lo for h in range(nh)]
        else:
            qs = [q_ref[h] for h in range(nh)]
            sum_lanes = [lo] * nh
        m_sc[...] = jnp.full_like(m_sc, -jnp.inf)
        acc_sc[...] = jnp.zeros_like(acc_sc)

        def block(j, r0, nr, c0, nc, seen_from):
            rows = slice(r0, r0 + nr)
            sl = pl.ds(pl.multiple_of(j * tq + c0, math.gcd(tq, c0) if c0 else tq), nc)
            if seen_from is not None:
                seen = (lax.broadcasted_iota(jnp.int32, (nr, nc), 1)
                        <= lax.broadcasted_iota(jnp.int32, (nr, nc), 0) + seen_from)
            for h in range(nh):
                kj, vj = (k_ref[sl, lanes_of(h)], v_ref[sl, lanes_of(h)]) if fox else (k_ref[h, sl, :], v_ref[h, sl, :])
                s = _dot_nt(qs[h][rows], kj)
                if fox and nc == 2 * tq:
                    s = s - jnp.concatenate([fr_ref[h, j], fr_ref[h, j + 1]], axis=1)
                elif fox:
                    s = s - fr_ref[h, j, :, c0:c0 + nc]
                if seen_from is not None:
                    s = jnp.where(seen, s, -jnp.inf)
                m_prev = m_sc[h, rows]
                m_new = jnp.maximum(m_prev, jnp.max(s, axis=1, keepdims=True))
                p = jnp.exp2((s - _tile_lanes(m_new, nc // LANES)).astype(BF16))
                vj = jnp.where(sum_lanes[h], one, vj)
                acc_sc[h, rows] = jnp.exp2(m_prev - m_new) * acc_sc[h, rows] + _dot(p, vj)
                m_sc[h, rows] = m_new

        def loop_body(jj, carry):
            block(2 * jj, 0, tq, 0, 2 * tq, None)
            return carry

        lax.fori_loop(0, i // 2, loop_body, 0)

        @pl.when(i % 2 == 1)
        def _():
            block(i - 1, 0, tq, 0, tq, None)

        half = tq // 2
        if half % LANES == 0:
            block(i, 0, half, 0, half, 0)
            block(i, half, half, 0, tq, half)
        else:
            block(i, 0, tq, 0, tq, 0)
        res = []
        for h in range(nh):
            acc = acc_sc[h]
            swapped = pltpu.roll(acc, HEAD_DIM, 1)
            res.append(acc / swapped)
            lse2 = m_sc[h] + jnp.log(jnp.where(sum_lanes[h], acc, swapped)) * LOG2E
            lset_ref[h, 0] = _stat_rows(lse2)
        for pr in range(nh // 2):
            even = res[2 * pr] if fox else pltpu.roll(res[2 * pr], HEAD_DIM, 1)
            o_ref[:, pr * LANES:(pr + 1) * LANES] = jnp.where(lo, even, res[2 * pr + 1])

    if fox:
        in_specs = [pl.BlockSpec((tq, wide), lambda g, i: (i, g))] + [pl.BlockSpec((t, wide), lambda g, i: (0, g))] * 2
        in_specs += [pl.BlockSpec((nh, nq, 1, tq), lambda g, i: (g, 0, 0, 0))]
        args = [q, k, v, f2_rows]
    else:
        in_specs = [pl.BlockSpec((nh, tq, LANES), lambda g, i: (g, i, 0))] + [pl.BlockSpec((nh, t, LANES), lambda g, i: (g, 0, 0))] * 2
        args = [q, k, v]
    return _hosted_call(
        "fox_attn_fwd" if fox else "mla_attn_fwd", main, (HEADS // nh, nq), args, in_specs,
        (jax.ShapeDtypeStruct((t, 4 * LANES), F32), jax.ShapeDtypeStruct((HEADS, nq, 8, tq), F32)),
        (pl.BlockSpec((tq, wide), lambda g, i: (i, g)), pl.BlockSpec((nh, 1, 8, tq), lambda g, i: (g, i, 0, 0))),
        [pltpu.VMEM((nh, tq, LANES), F32), pltpu.VMEM((nh, tq, LANES), F32)], comm)


def _head_do(fox, hh, do2, lo):
    if fox:
        return jnp.where(lo if hh == 0 else jnp.logical_not(lo), do2, 0.0)
    return jnp.where(lo, 0.0, pltpu.roll(do2, HEAD_DIM, 1) if hh == 0 else do2)


BWD_HEADS = 4


def _attn_bwd(fox, q, k, v, do, lse_rows, delta_rows, f2_rep=None, comm=None):
    t = q.shape[0] if fox else q.shape[1]
    tq = _row_tile(t)
    nq = t // tq
    scale = FOX_SCALE if fox else MLA_SCALE
    nh = BWD_HEADS

    def main(ins, outs, scr):
        if fox:
            q_ref, k_ref, v_ref, f_ref, do_ref, lse_ref, dl_ref = ins
            dq_ref, dk_ref, dv_ref, dfq_ref, dfk_ref = outs
        else:
            q_ref, k_ref, v_ref, do_ref, lse_ref, dl_ref = ins
            dq_ref, dkv_ref, dkr_ref = outs
        dq_sc, dk_sc, dv_sc = scr
        j = pl.program_id(1)
        lane = _lane()
        lo = lane < HEAD_DIM
        hi = jnp.logical_not(lo)
        zero, one = jnp.zeros((), BF16), jnp.ones((), BF16)
        lanes_of = lambda h: slice((h // 2) * LANES, (h // 2 + 1) * LANES)

        @pl.when(j == 0)
        def _():
            dq_sc[...] = jnp.zeros_like(dq_sc)

        dk_sc[...] = jnp.zeros_like(dk_sc)
        dv_sc[...] = jnp.zeros_like(dv_sc)

        def block(i, r0, nr, c0, nc, masked):
            rows, cols = slice(r0, r0 + nr), slice(c0, c0 + nc)
            sl = pl.ds(pl.multiple_of(i * tq + c0, math.gcd(tq, c0) if c0 else tq), nc)
            if masked:
                seen = lax.broadcasted_iota(jnp.int32, (nr, nc), 1) >= lax.broadcasted_iota(jnp.int32, (nr, nc), 0)
            for h in range(nh):
                hh = h % 2
                kj = k_ref[rows, lanes_of(h)] if fox else k_ref[h, rows, :]
                vj = v_ref[rows, lanes_of(h)] if fox else v_ref[h, rows, :]
                qi = jnp.where(lo if hh == 0 else hi, q_ref[sl, lanes_of(h)], zero) if fox else q_ref[h, sl, :]
                dob = _head_do(fox, hh, do_ref[sl, lanes_of(h)], lo).astype(BF16)
                st = _dot_nt(kj, qi)
                if fox:
                    st = st - _tile_lanes(f_ref[h, rows, :], nc // LANES)
                if masked:
                    st = jnp.where(seen, st, -jnp.inf)
                pt = jnp.exp2(st - lse_ref[h, i, 0:1, cols])
                dpt = _dot_nt(vj, dob)
                dst = (pt * (dpt - dl_ref[h, i, 0:1, cols])).astype(BF16)
                dv_sc[h, rows] += _dot(pt.astype(BF16), dob)
                if fox:
                    other = hi if hh == 0 else lo
                    qi = jnp.where(other, one, qi)
                    kj = jnp.where(other, one, kj)
                dk_sc[h, rows] += _dot(dst, qi)
                dq_sc[h, sl, :] += _dot_tn(dst, kj)

        def loop_body(i, carry):
            block(i, 0, tq, 0, tq, False)
            return carry

        half = tq // 2
        if half % LANES == 0:
            block(j, 0, half, 0, tq, True)
            block(j, half, half, half, half, True)
        else:
            block(j, 0, tq, 0, tq, True)
        lax.fori_loop(j + 1, nq, loop_body, 0)
        rope_lanes = jnp.logical_and(lane >= NOPE, lane < NOPE + ROPE)
        for pr in range(nh // 2):
            a, b = 2 * pr, 2 * pr + 1
            if fox:
                dk_ref[:, lanes_of(a)] = (jnp.where(lo, dk_sc[a], dk_sc[b]) * LN2).astype(BF16)
                dv_ref[:, lanes_of(a)] = (dv_sc[a] + dv_sc[b]).astype(BF16)
                for h in (a, b):
                    dk = dk_sc[h]
                    dfk_ref[h] = -jnp.where(hi if h == a else lo, dk, pltpu.roll(dk, HEAD_DIM, 1))
            else:
                dkr = jnp.zeros((tq, LANES), F32)
                for h in (a, b):
                    dk = dk_sc[h] * LN2
                    dkv_ref[h] = jnp.where(lo, dk, dv_sc[h])
                    dkr = dkr + jnp.where(rope_lanes, dk, 0.0)
                dkr_ref[pr] = dkr

        @pl.when(j == nq - 1)
        def _():
            for i in range(nq):
                rows = slice(i * tq, (i + 1) * tq)
                for pr in range(nh // 2):
                    a, b = 2 * pr, 2 * pr + 1
                    if fox:
                        dq_ref[rows, lanes_of(a)] = (jnp.where(lo, dq_sc[a, rows, :], dq_sc[b, rows, :]) * scale).astype(BF16)
                        for h in (a, b):
                            acc = dq_sc[h, rows, :]
                            dfq_ref[h, rows, :] = jnp.where(hi if h == a else lo, acc, pltpu.roll(acc, HEAD_DIM, 1))
                    else:
                        for h in (a, b):
                            dq_ref[h, rows, :] = dq_sc[h, rows, :] * scale

    wide_cols = (nh // 2) * LANES
    stat = pl.BlockSpec((nh, tq, LANES), lambda p, j: (p, j, 0))
    stat_all = pl.BlockSpec((nh, t, LANES), lambda p, j: (p, 0, 0))
    rows4 = pl.BlockSpec((nh, nq, 8, tq), lambda p, j: (p, 0, 0, 0))
    pair = pl.BlockSpec((tq, wide_cols), lambda p, j: (j, p))
    pair_all = pl.BlockSpec((t, wide_cols), lambda p, j: (0, p))
    if fox:
        in_specs = [pair_all, pair, pair, stat]
        args = [q, k, v, f2_rep]
    else:
        in_specs = [stat_all, stat, stat]
        args = [q, k, v]
    in_specs += [pair_all, rows4, rows4]
    args += [do, lse_rows, delta_rows]
    heads_f32 = jax.ShapeDtypeStruct((HEADS, t, LANES), F32)
    if fox:
        wide = jax.ShapeDtypeStruct((t, 4 * LANES), BF16)
        out_shape = (wide, wide, wide, heads_f32, heads_f32)
        out_specs = (pair_all, pair, pair, stat_all, stat)
    else:
        out_shape = (heads_f32, heads_f32, jax.ShapeDtypeStruct((HEADS // 2, t, LANES), F32))
        out_specs = (stat_all, stat, pl.BlockSpec((nh // 2, tq, LANES), lambda p, j: (p, j, 0)))
    acc = pltpu.VMEM((nh, tq, LANES), F32)
    return _hosted_call("fox_attn_bwd" if fox else "mla_attn_bwd", main, (HEADS // nh, nq), args, in_specs,
                        out_shape, out_specs, [pltpu.VMEM((nh, t, LANES), F32), acc, acc], comm)


def _attn_out(x, fox_o, mla_o, gf, gm, w_o):
    t = x.shape[0]
    tm = _row_tile(t)

    def body(x_ref, f_ref, m_ref, gf_ref, gm_ref, w_ref, x1_ref, mix_ref):
        nf, _ = _rms(f_ref[...], gf_ref[...])
        nm, _ = _rms(m_ref[...], gm_ref[...])
        nfb, nmb = nf.astype(BF16), nm.astype(BF16)
        mix_ref[:, :FOX_WIDTH] = nfb
        mix_ref[:, FOX_WIDTH:] = nmb
        x1_ref[...] = x_ref[...] + _dot(nfb, w_ref[:FOX_WIDTH, :]) + _dot(nmb, w_ref[FOX_WIDTH:, :])

    row = lambda n: pl.BlockSpec((tm, n), lambda i: (i, 0))
    full = lambda a: pl.BlockSpec(a.shape, lambda i: (0,) * a.ndim)
    return pl.pallas_call(
        body, name="attn_out", grid=(t // tm,),
        out_shape=(jax.ShapeDtypeStruct((t, D_MODEL), F32), jax.ShapeDtypeStruct((t, D_MODEL), BF16)),
        in_specs=[row(D_MODEL), row(FOX_WIDTH), row(MLA_WIDTH), full(gf), full(gm), full(w_o)],
        out_specs=(row(D_MODEL), row(D_MODEL)),
        compiler_params=_cparams("parallel"),
    )(x, fox_o, mla_o, gf, gm, w_o)


def _mlp_tile(t):
    return 256 if t >= 2048 else 128


def _resident(a):
    return pl.BlockSpec(a.shape, lambda i: (0,) * a.ndim, pipeline_mode=pl.Buffered(1))


FF_CHUNK = 512


def _mlp_fwd(x1, g_mlp, w_up, w_down, g_fin, target):
    t = x1.shape[0]
    tm = _mlp_tile(t)

    def body(x_ref, g_ref, wu_ref, wd_ref, gf_ref, t_ref, u_ref, h_ref, dx_ref, dxb_ref, loss_ref, dg_ref, a_sc):
        @pl.when(pl.program_id(0) == 0)
        def _():
            loss_ref[...] = jnp.zeros_like(loss_ref)
            dg_ref[...] = jnp.zeros_like(dg_ref)

        x = x_ref[...]
        h, _ = _rms(x, g_ref[...])
        hb = h.astype(BF16)
        h_ref[...] = hb
        for f in range(D_FF // FF_CHUNK):
            sl = slice(f * FF_CHUNK, (f + 1) * FF_CHUNK)
            u = _dot(hb, wu_ref[:, sl])
            u_ref[:, sl] = u
            r = jnp.maximum(u, 0.0)
            a_sc[:, sl] = (r * r).astype(BF16)
        x2 = x + _dot(a_sc[...], wd_ref[...])
        y, r2 = _rms(x2, gf_ref[...])
        err = y - t_ref[...]
        loss_ref[...] += 0.5 * jnp.sum(jnp.mean(err * err, axis=-1, keepdims=True))
        dx, dg = _rms_bwd(x2, gf_ref[...], r2, err * (1.0 / D_MODEL))
        dx_ref[...] = dx
        dxb_ref[...] = dx.astype(BF16)
        dg_ref[...] += dg

    row = lambda n: pl.BlockSpec((tm, n), lambda i: (i, 0))
    vec = pl.BlockSpec((1, D_MODEL), lambda i: (0, 0))
    return pl.pallas_call(
        body, name="mlp_fwd", grid=(t // tm,),
        out_shape=(jax.ShapeDtypeStruct((t, D_FF), F32), jax.ShapeDtypeStruct((t, D_MODEL), BF16),
                   jax.ShapeDtypeStruct((t, D_MODEL), F32), jax.ShapeDtypeStruct((t, D_MODEL), BF16),
                   jax.ShapeDtypeStruct((8, LANES), F32), jax.ShapeDtypeStruct((1, D_MODEL), F32)),
        in_specs=[row(D_MODEL), vec, _resident(w_up), _resident(w_down), vec, row(D_MODEL)],
        out_specs=(row(D_FF), row(D_MODEL), row(D_MODEL), row(D_MODEL), pl.BlockSpec((8, LANES), lambda i: (0, 0)), vec),
        scratch_shapes=[pltpu.VMEM((tm, D_FF), BF16)],
        compiler_params=_cparams("arbitrary"),
    )(x1, g_mlp, w_up, w_down, g_fin, target)


def _mlp_bwd(dx2, u, x1, g_mlp, w_up, w_down):
    t = x1.shape[0]
    tm = _mlp_tile(t)

    def body(dx_ref, u_ref, x_ref, g_ref, wu_ref, wd_ref, du_ref, a_ref, dx1_ref, dx1b_ref, dg_ref):
        @pl.when(pl.program_id(0) == 0)
        def _():
            dg_ref[...] = jnp.zeros_like(dg_ref)

        dx2 = dx_ref[...]
        dxb = dx2.astype(BF16)
        for f in range(D_FF // FF_CHUNK):
            sl = slice(f * FF_CHUNK, (f + 1) * FF_CHUNK)
            r = jnp.maximum(u_ref[:, sl], 0.0)
            a_ref[:, sl] = (r * r).astype(BF16)
            da = _dot_nt(dxb, wd_ref[sl, :])
            du_ref[:, sl] = (da * (2.0 * r)).astype(BF16)
        dh = _dot_nt(du_ref[...], wu_ref[...])
        x = x_ref[...]
        _, r1 = _rms(x, g_ref[...])
        dx, dg = _rms_bwd(x, g_ref[...], r1, dh)
        dx1 = dx2 + dx
        dx1_ref[...] = dx1
        dx1b_ref[...] = dx1.astype(BF16)
        dg_ref[...] += dg

    row = lambda n: pl.BlockSpec((tm, n), lambda i: (i, 0))
    vec = pl.BlockSpec((1, D_MODEL), lambda i: (0, 0))
    return pl.pallas_call(
        body, name="mlp_bwd", grid=(t // tm,),
        out_shape=(jax.ShapeDtypeStruct((t, D_FF), BF16), jax.ShapeDtypeStruct((t, D_FF), BF16),
                   jax.ShapeDtypeStruct((t, D_MODEL), F32), jax.ShapeDtypeStruct((t, D_MODEL), BF16),
                   jax.ShapeDtypeStruct((1, D_MODEL), F32)),
        in_specs=[row(D_MODEL), row(D_FF), row(D_MODEL), vec, _resident(w_up), _resident(w_down)],
        out_specs=(row(D_FF), row(D_FF), row(D_MODEL), row(D_MODEL), vec),
        compiler_params=_cparams("arbitrary"),
    )(dx2, u, x1, g_mlp, w_up, w_down)


def _matmul_tn(name, a, b, blocks=None):
    t, m = a.shape
    n = b.shape[1]
    tk = t if a.dtype == BF16 and b.dtype == BF16 else min(t, 2048)
    steps = t // tk
    bm = m if m <= 1024 else 512
    bn = n if n <= 1024 else 512
    width = bn if blocks is None else n // blocks
    per = bn // width

    def body(a_ref, b_ref, o_ref, acc_sc):
        kk = pl.program_id(2)

        @pl.when(kk == 0)
        def _():
            acc_sc[...] = jnp.zeros_like(acc_sc)

        acc_sc[...] += _dot_tn(a_ref[...].astype(BF16), b_ref[...].astype(BF16))

        @pl.when(kk == steps - 1)
        def _():
            if blocks is None:
                o_ref[...] = acc_sc[...]
            else:
                for s in range(per):
                    o_ref[s] = acc_sc[:, s * width:(s + 1) * width]

    if blocks is None:
        o_spec = pl.BlockSpec((bm, bn), lambda i, j, kk: (i, j))
        o_shape = (m, n)
    else:
        o_spec = pl.BlockSpec((per, bm, width), lambda i, j, kk: (j, i, 0))
        o_shape = (blocks, m, width)
    return pl.pallas_call(
        body, name=name, grid=(m // bm, n // bn, steps),
        out_shape=jax.ShapeDtypeStruct(o_shape, F32),
        in_specs=[pl.BlockSpec((tk, bm), lambda i, j, kk: (kk, i)), pl.BlockSpec((tk, bn), lambda i, j, kk: (kk, j))],
        out_specs=o_spec,
        scratch_shapes=[pltpu.VMEM((bm, bn), F32)],
        compiler_params=_cparams("parallel", "parallel", "arbitrary"),
    )(a, b)


def _dw_in(dfq, dfk, dfv, drest, h1):
    t = h1.shape[0]
    tk = min(t, 1024)
    off_ff = 3 * FOX_WIDTH
    off_cq = off_ff + HEADS
    off_kr = IN_COLS - ROPE

    def body(dq_ref, dk_ref, dv_ref, dr_ref, h_ref, o_ref):
        h = h_ref[...]
        r = _dot_tn(dr_ref[...], h)
        parts = [(slice(n * FOX_WIDTH, (n + 1) * FOX_WIDTH), _dot_tn(ref[...], h)) for n, ref in enumerate((dq_ref, dk_ref, dv_ref))]
        parts += [(slice(off_ff, off_cq), r[0:HEADS]), (slice(off_cq, off_kr), r[REST_CQ:REST_KR]),
                  (slice(off_kr, IN_COLS), r[REST_KR + NOPE:REST_KR + NOPE + ROPE])]

        @pl.when(pl.program_id(0) == 0)
        def _():
            for rows, val in parts:
                o_ref[rows, :] = val

        @pl.when(pl.program_id(0) > 0)
        def _():
            for rows, val in parts:
                o_ref[rows, :] += val

    tok = lambda n: pl.BlockSpec((tk, n), lambda kk: (kk, 0))
    return pl.pallas_call(
        body, name="dw_in", grid=(t // tk,),
        out_shape=jax.ShapeDtypeStruct((IN_COLS, D_MODEL), F32),
        in_specs=[tok(FOX_WIDTH), tok(FOX_WIDTH), tok(FOX_WIDTH), tok(REST_COLS), tok(D_MODEL)],
        out_specs=pl.BlockSpec((IN_COLS, D_MODEL), lambda kk: (0, 0)),
        compiler_params=_cparams("arbitrary"),
    )(dfq, dfk, dfv, drest, h1)


def _attn_out_bwd(dx1, fox_o, mla_o, gf, gm, w_o):
    t = dx1.shape[0]
    tm = _row_tile(t)

    def body(dx_ref, f_ref, m_ref, gf_ref, gm_ref, w_ref, df_ref, dm_ref, dlf_ref, dlm_ref, dgf_ref, dgm_ref):
        @pl.when(pl.program_id(0) == 0)
        def _():
            dgf_ref[...] = jnp.zeros_like(dgf_ref)
            dgm_ref[...] = jnp.zeros_like(dgm_ref)
        dxb = dx_ref[...].astype(BF16)
        lane = lax.broadcasted_iota(jnp.int32, (8, LANES), 1)
        upper = lax.broadcasted_iota(jnp.int32, (8, LANES), 0) < 4
        pick = jnp.where(upper, (lane < HEAD_DIM).astype(F32), (lane >= HEAD_DIM).astype(F32)).astype(BF16)
        for o_ref, g_ref, lo_row, d_ref, dl_ref, dg_ref in ((f_ref, gf_ref, 0, df_ref, dlf_ref, dgf_ref),
                                                             (m_ref, gm_ref, FOX_WIDTH, dm_ref, dlm_ref, dgm_ref)):
            dn = _dot_nt(dxb, w_ref[lo_row:lo_row + FOX_WIDTH, :])
            o = o_ref[...]
            _, r = _rms(o, g_ref[...])
            d, dg = _rms_bwd(o, g_ref[...], r, dn)
            d_ref[...] = d
            dg_ref[...] += dg
            prod = d * o
            for pr in range(HEADS // 2):
                parts = _split3(prod[:, pr * LANES:(pr + 1) * LANES])
                both = (_dot_nt(pick, parts[0]) + _dot_nt(pick, parts[1])) + _dot_nt(pick, parts[2])
                dl_ref[2 * pr, 0] = both
                dl_ref[2 * pr + 1, 0] = pltpu.roll(both, 4, 0)

    row = lambda n: pl.BlockSpec((tm, n), lambda i: (i, 0))
    full = lambda a: pl.BlockSpec(a.shape, lambda i: (0,) * a.ndim)
    vec = pl.BlockSpec((1, FOX_WIDTH), lambda i: (0, 0))
    rows = pl.BlockSpec((HEADS, 1, 8, tm), lambda i: (0, i, 0, 0))
    o_shape = jax.ShapeDtypeStruct((t, FOX_WIDTH), F32)
    g_shape = jax.ShapeDtypeStruct((1, FOX_WIDTH), F32)
    r_shape = jax.ShapeDtypeStruct((HEADS, t // tm, 8, tm), F32)
    return pl.pallas_call(
        body, name="attn_out_bwd", grid=(t // tm,),
        out_shape=(o_shape, o_shape, r_shape, r_shape, g_shape, g_shape),
        in_specs=[row(D_MODEL), row(FOX_WIDTH), row(MLA_WIDTH), full(gf), full(gm), full(w_o)],
        out_specs=(row(FOX_WIDTH), row(MLA_WIDTH), rows, rows, vec, vec),
        compiler_params=_cparams("arbitrary"),
    )(dx1, fox_o, mla_o, gf, gm, w_o)


def _mla_prep_bwd(dq, dkv, dkr, dz, rest, gq, gkv, wq, wkv, cos, sin):
    t = rest.shape[0]
    tm = _row_tile(t)

    def body(dq_ref, dkv_ref, dkr_ref, dz_ref, r_ref, gq_ref, gkv_ref, wq_ref, wkv_ref, c_ref, s_ref,
             dr_ref, dqp_ref, dkvb_ref, dgq_ref, dgkv_ref):
        @pl.when(pl.program_id(0) == 0)
        def _():
            dgq_ref[...] = jnp.zeros_like(dgq_ref)
            dgkv_ref[...] = jnp.zeros_like(dgkv_ref)
        cos_, sin_ = c_ref[...], s_ref[...]
        dcq = jnp.zeros((tm, Q_RANK), F32)
        dckv = jnp.zeros((tm, KV_RANK), F32)
        for h in range(HEADS):
            dqp = _rope_bwd(dq_ref[h], cos_, sin_).astype(BF16)
            dqp_ref[:, h * LANES:(h + 1) * LANES] = dqp
            dcq = dcq + _dot_nt(dqp, wq_ref[h])
            dkvb = dkv_ref[h].astype(BF16)
            dkvb_ref[:, h * LANES:(h + 1) * LANES] = dkvb
            dckv = dckv + _dot_nt(dkvb, wkv_ref[h])
        dkrope = dkr_ref[0]
        for pr in range(1, HEADS // 2):
            dkrope = dkrope + dkr_ref[pr]
        cq = r_ref[:, REST_CQ:REST_CKV]
        _, rq = _rms(cq, gq_ref[...])
        d_cq, dgq = _rms_bwd(cq, gq_ref[...], rq, dcq)
        ckv = r_ref[:, REST_CKV:REST_KR]
        _, rkv = _rms(ckv, gkv_ref[...])
        d_ckv, dgkv = _rms_bwd(ckv, gkv_ref[...], rkv, dckv)
        dgq_ref[...] += dgq
        dgkv_ref[...] += dgkv
        dr_ref[:, 0:REST_CQ] = dz_ref[...].astype(BF16)
        dr_ref[:, REST_CQ:REST_CKV] = d_cq.astype(BF16)
        dr_ref[:, REST_CKV:REST_KR] = d_ckv.astype(BF16)
        dr_ref[:, REST_KR:REST_COLS] = _rope_bwd(dkrope, cos_, sin_).astype(BF16)

    row = lambda n: pl.BlockSpec((tm, n), lambda i: (i, 0))
    full = lambda a: pl.BlockSpec(a.shape, lambda i: (0,) * a.ndim)
    heads = pl.BlockSpec((HEADS, tm, LANES), lambda i: (0, i, 0))
    hshape = jax.ShapeDtypeStruct((t, HEADS * LANES), BF16)
    return pl.pallas_call(
        body, name="mla_prep_bwd", grid=(t // tm,),
        out_shape=(jax.ShapeDtypeStruct((t, REST_COLS), BF16), hshape, hshape,
                   jax.ShapeDtypeStruct((1, Q_RANK), F32), jax.ShapeDtypeStruct((1, KV_RANK), F32)),
        in_specs=[heads, heads, pl.BlockSpec((HEADS // 2, tm, LANES), lambda i: (0, i, 0)), row(LANES), row(REST_COLS),
                  full(gq), full(gkv), full(wq), full(wkv), row(LANES), row(LANES)],
        out_specs=(row(REST_COLS), row(HEADS * LANES), row(HEADS * LANES), pl.BlockSpec((1, Q_RANK), lambda i: (0, 0)),
                   pl.BlockSpec((1, KV_RANK), lambda i: (0, 0))),
        compiler_params=_cparams("arbitrary"),
    )(dq, dkv, dkr, dz, rest, gq, gkv, wq, wkv, cos, sin)


def _in_proj_bwd(x, g, dx1, dfq, dfk, dfv, drest, w_qkv, w_rest, comm=None):
    t = x.shape[0]
    tm = _row_tile(t)

    def main(ins, outs, scr):
        x_ref, g_ref, dx1_ref, dq_ref, dk_ref, dv_ref, dr_ref, wq_ref, wr_ref = ins
        dx_ref, dg_ref = outs

        @pl.when(pl.program_id(0) == 0)
        def _():
            dg_ref[...] = jnp.zeros_like(dg_ref)
        dh = _dot(dr_ref[...], wr_ref[...])
        for n, ref in enumerate((dq_ref, dk_ref, dv_ref)):
            dh = dh + _dot(ref[...], wq_ref[n * FOX_WIDTH:(n + 1) * FOX_WIDTH, :])
        xv = x_ref[...]
        _, r = _rms(xv, g_ref[...])
        dx, dg = _rms_bwd(xv, g_ref[...], r, dh)
        dx_ref[...] = dx1_ref[...] + dx
        dg_ref[...] += dg

    row = lambda n: pl.BlockSpec((tm, n), lambda i: (i, 0))
    full = lambda a: pl.BlockSpec(a.shape, lambda i: (0,) * a.ndim)
    vec = pl.BlockSpec((1, D_MODEL), lambda i: (0, 0))
    return _hosted_call(
        "in_proj_bwd", main, (t // tm,), [x, g, dx1, dfq, dfk, dfv, drest, w_qkv, w_rest],
        [row(D_MODEL), full(g), row(D_MODEL), row(FOX_WIDTH), row(FOX_WIDTH), row(FOX_WIDTH), row(REST_COLS),
         full(w_qkv), full(w_rest)],
        (jax.ShapeDtypeStruct((t, D_MODEL), F32), jax.ShapeDtypeStruct((1, D_MODEL), F32)), (row(D_MODEL), vec), [], comm)


def _pad_cols(a, n):
    return jnp.pad(a, ((0, 0),) * (a.ndim - 1) + ((0, n - a.shape[-1]),))


def kernel(x, positions, attn_norm_g, w_in, b_forget, q_norm_g, w_uq, kv_norm_g, w_ukv, fox_out_g, mla_out_g, w_o, mlp_norm_g, w_up, w_down, final_norm_g, loss_target, m_attn_norm_g, m_w_in, m_b_forget, m_q_norm_g, m_w_uq, m_kv_norm_g, m_w_ukv, m_fox_out_g, m_mla_out_g, m_w_o, m_mlp_norm_g, m_w_up, m_w_down, m_final_norm_g, v_attn_norm_g, v_w_in, v_b_forget, v_q_norm_g, v_w_uq, v_kv_norm_g, v_w_ukv, v_fox_out_g, v_mla_out_g, v_w_o, v_mlp_norm_g, v_w_up, v_w_down, v_final_norm_g):
    t = x.shape[1]
    tq = _row_tile(t)
    xs = x[0]
    target = loss_target[0]

    mid = [_pad_cols(w_uq[0], LANES).astype(BF16), w_ukv[0].astype(BF16)]
    late = [w_o[0].astype(BF16), w_up[0].astype(BF16), w_down[0].astype(BF16)]
    g_in, = _all_gather([jnp.transpose(w_in[0]).astype(BF16)])
    win = g_in.reshape(IN_COLS, D_MODEL)
    off_ff, off_cq, off_kr = 3 * FOX_WIDTH, 3 * FOX_WIDTH + HEADS, IN_COLS - ROPE
    zeros = lambda n: jnp.zeros((n, D_MODEL), BF16)
    w_qkv = win[:off_ff]
    w_rest = jnp.concatenate([
        win[off_ff:off_cq], zeros(REST_CQ - HEADS), win[off_cq:off_kr],
        zeros(NOPE), win[off_kr:], zeros(LANES - NOPE - ROPE)], axis=0)

    cos, sin = _rope_tables(positions.reshape(t, 1))
    (h1, fq, fk, fv, rest), (wq, wkv) = _in_proj(xs, attn_norm_g, w_qkv, w_rest, comm=_ag_to_all(mid))
    b128 = _pad_cols(b_forget, LANES)
    f2_rows, f2_rep = _forget_cumsum(rest, b128)
    f2_rows = f2_rows.reshape(HEADS, t // tq, 1, tq)
    (fox_o, fox_lse_rows), partly = _attn_fwd(True, fq, fk, fv, f2_rows, comm=_ag_direct(late))
    mq, mk, mkv, cqn, ckvn = _mla_prep(rest, q_norm_g, kv_norm_g, wq, wkv, cos, sin)
    (mla_o, mla_lse_rows), (g_o, g_up, g_down) = _attn_fwd(False, mq, mk, mkv, comm=_ag_forward(partly))
    wo = g_o.reshape(D_MODEL, D_MODEL)
    x1, mixed = _attn_out(xs, fox_o, mla_o, fox_out_g, mla_out_g, wo)
    wup = jnp.transpose(g_up, (1, 0, 2)).reshape(D_MODEL, D_FF)
    wdown = g_down.reshape(D_FF, D_MODEL)
    u, h2, dx2, dx2b, loss8, d_gfin = _mlp_fwd(x1, mlp_norm_g, wup, wdown, final_norm_g.reshape(1, D_MODEL), target)

    du, act, dx1, dx1b, d_gmlp = _mlp_bwd(dx2, u, x1, mlp_norm_g, wup, wdown)
    dw_down = _matmul_tn("dw_down", act, dx2b)
    dw_up = _matmul_tn("dw_up", h2, du, blocks=N_DEV)
    dfox_o, dmla_o, fox_delta_rows, mla_delta_rows, d_gfox, d_gmla = _attn_out_bwd(dx1, fox_o, mla_o, fox_out_g, mla_out_g, wo)
    dw_o = _matmul_tn("dw_o", mixed, dx1b)

    place = jnp.stack([lax.axis_index("c"), 2 * lax.axis_index("x") + lax.axis_index("y")]).astype(jnp.int32)
    names = ("w_in", "w_uq", "w_ukv", "w_o", "w_up", "w_down")
    grads_b = [dw_o.reshape(N_DEV, -1, D_MODEL), dw_up, dw_down.reshape(N_DEV, -1, D_MODEL)]
    (dfq, dfk, dfv, d_fq, d_fk), got_b = _attn_bwd(True, fq, fk, fv, dfox_o, fox_lse_rows, fox_delta_rows,
                                                   f2_rep, comm=_rs_to_sibling(grads_b))
    sums_b = [_rs_sibling_sum("rs_sibling_sum_" + nm, g, l, place) for nm, g, l in zip(names[3:], grads_b, got_b)]
    dz, d_b = _forget_bwd(rest, b128, d_fq, d_fk)
    (dmq, dmkv, dmkr), others_b = _attn_bwd(False, mq, mk, mkv, dmla_o, mla_lse_rows, mla_delta_rows,
                                            comm=_rs_to_chips([s[1] for s in sums_b]))
    drest, dqp, dkvb, d_gq, d_gkv = _mla_prep_bwd(dmq, dmkv, dmkr, dz, rest, q_norm_g, kv_norm_g, wq, wkv, cos, sin)
    dw_uq = _matmul_tn("dw_uq", cqn, dqp, blocks=HEADS)
    dw_ukv = _matmul_tn("dw_ukv", ckvn, dkvb, blocks=HEADS)
    dw_in = _dw_in(dfq, dfk, dfv, drest, h1)

    grads_a = [dw_in.reshape(N_DEV, IN_SHARD, D_MODEL), dw_uq, dw_ukv]
    got_a = _comm_call("rs_sibling_exchange", _rs_to_sibling(grads_a))
    sums_a = [_rs_sibling_sum("rs_sibling_sum_" + nm, g, l, place) for nm, g, l in zip(names[:3], grads_a, got_a)]
    (grad_x, d_gattn), others_a = _in_proj_bwd(xs, attn_norm_g, dx1, dfq, dfk, dfv, drest, w_qkv, w_rest,
                                               comm=_rs_to_chips([s[1] for s in sums_a]))
    sums, others = sums_a + sums_b, list(others_a) + list(others_b)
    sharded = (w_in, w_uq, w_ukv, w_o, w_up, w_down)
    moments_m = (m_w_in, m_w_uq, m_w_ukv, m_w_o, m_w_up, m_w_down)
    moments_v = (v_w_in, v_w_uq, v_w_ukv, v_w_o, v_w_up, v_w_down)
    g_in_t = _rs_final_sum("rs_final_sum_w_in", sums[0][0], others[0])
    big = [_adamw_given("adamw_w_in", jnp.transpose(g_in_t), w_in, m_w_in, v_w_in)]
    for a in range(1, len(names)):
        big.append(_adamw_sharded("adamw_" + names[a], sharded[a], moments_m[a], moments_v[a], sums[a][0], others[a]))
    big_g, big_d, big_m, big_v = [[b[k] for b in big] for k in range(4)]

    as_row = lambda a: a.reshape(1, -1)
    small_w = (attn_norm_g, b_forget, q_norm_g, kv_norm_g, fox_out_g, mla_out_g, mlp_norm_g, final_norm_g)
    small_m = (m_attn_norm_g, m_b_forget, m_q_norm_g, m_kv_norm_g, m_fox_out_g, m_mla_out_g, m_mlp_norm_g, m_final_norm_g)
    small_v = (v_attn_norm_g, v_b_forget, v_q_norm_g, v_kv_norm_g, v_fox_out_g, v_mla_out_g, v_mlp_norm_g, v_final_norm_g)
    total = _small_all_reduce([d_gattn, d_b, d_gq, d_gkv, d_gfox, d_gmla, d_gmlp, d_gfin], loss8)
    small = _adamw_small(total, [as_row(a) for a in small_w], [as_row(a) for a in small_m], [as_row(a) for a in small_v])
    loss = small[0].reshape(())
    s_g, s_d, s_m, s_v = [[small[1 + 4 * r + k].reshape(small_w[r].shape) for r in range(len(small_w))] for k in range(4)]

    def ordered(small_, bigs):
        ga, bf, gq_, gkv_, gfo, gml, gmlp_, gfin_ = small_
        bin_, buq, bukv, bo, bup, bdown = bigs
        return [ga, bin_, bf, gq_, buq, gkv_, bukv, gfo, gml, bo, gmlp_, bup, bdown, gfin_]

    return (loss, grad_x[None], *ordered(s_g, big_g), *ordered(s_d, big_d), *ordered(s_m, big_m), *ordered(s_v, big_v))
```

```python
import math
from typing import Callable, NamedTuple

import numpy as np
import jax
import jax.numpy as jnp
from jax import lax
from jax.experimental import pallas as pl
from jax.experimental.pallas import tpu as pltpu

F32 = jnp.float32
BF16 = jnp.bfloat16
MESH = pl.DeviceIdType.MESH

D_MODEL = 1024
HEADS = 8
HEAD_DIM = 64
FOX_WIDTH = 512
MLA_WIDTH = 512
NOPE = 64
ROPE = 32
QK_DIM = 96
Q_RANK = 384
KV_RANK = 256
D_FF = 4096
IN_COLS = 2216
ROPE_THETA = 10000.0
EPS = 1e-6
FOX_SCALE = 1.0 / math.sqrt(HEAD_DIM)
MLA_SCALE = 1.0 / math.sqrt(QK_DIM)
ADAM_LR = 0.001
ADAM_B1 = 0.9
ADAM_B2 = 0.999
ADAM_EPS = 1e-08
ADAM_WD = 0.01
ADAM_STEP = 10

N_DEV = 8
LANES = 128
REST_COLS = 896
REST_CQ = LANES
REST_CKV = REST_CQ + Q_RANK
REST_KR = REST_CKV + KV_RANK
LOG2E = 1.4426950408889634
LN2 = 0.6931471805599453
FOX_Q_FACTOR = FOX_SCALE * LOG2E
MLA_Q_FACTOR = MLA_SCALE * LOG2E
VMEM_LIMIT = 56 * 1024 * 1024

IN_SHARD = IN_COLS // N_DEV
SMALL_SIZES = (1024, 8, 384, 256, 512, 512, 1024, 1024)
SMALL_ROWS = 16
LOSS_ROW = len(SMALL_SIZES)


def _cparams(*sem):
    return pltpu.CompilerParams(dimension_semantics=sem or None, vmem_limit_bytes=VMEM_LIMIT)


def _row_tile(t):
    return 512 if t >= 2048 else (256 if t >= 512 else 128)


def _dot(a, b):
    return jnp.dot(a, b, preferred_element_type=F32)


def _dot_nt(a, b):
    return lax.dot_general(a, b, (((1,), (1,)), ((), ())), preferred_element_type=F32)


def _dot_tn(a, b):
    return lax.dot_general(a, b, (((0,), (0,)), ((), ())), preferred_element_type=F32)


def _rms(x, g):
    r = lax.rsqrt(jnp.mean(x * x, axis=-1, keepdims=True) + EPS)
    return x * r * g, r


def _rms_bwd(x, g, r, dy):
    xh = x * r
    gdy = dy * g
    dx = r * (gdy - xh * jnp.mean(gdy * xh, axis=-1, keepdims=True))
    return dx, jnp.sum(dy * xh, axis=0, keepdims=True)


def _lane():
    return lax.broadcasted_iota(jnp.int32, (1, LANES), 1)


def _rot(x):
    lane = _lane()
    half = NOPE + ROPE // 2
    first = jnp.logical_and(lane >= NOPE, lane < half)
    second = jnp.logical_and(lane >= half, lane < NOPE + ROPE)
    return jnp.where(first, -pltpu.roll(x, LANES - ROPE // 2, 1), jnp.where(second, pltpu.roll(x, ROPE // 2, 1), 0.0))


def _rope(x, cos, sin):
    return x * cos + _rot(x) * sin


def _rope_bwd(dy, cos, sin):
    return dy * cos - _rot(dy * sin)


def _remote(src, dst, send_sem, recv_sem, to):
    return pltpu.make_async_remote_copy(src_ref=src, dst_ref=dst, send_sem=send_sem, recv_sem=recv_sem,
                                        device_id=to, device_id_type=MESH)


def _hbm_specs(n):
    return [pl.BlockSpec(memory_space=pl.ANY)] * n


def _all_gather(blocks):
    n = len(blocks)

    def body(*refs):
        x_refs, out_refs = refs[:n], refs[n:2 * n]
        send_sems, recv_sems, local_sems = refs[2 * n:]
        x, y, c = lax.axis_index("x"), lax.axis_index("y"), lax.axis_index("c")
        me, sibling = (x, y, c), (x, y, 1 - c)
        chips = [(1 - x, y), (x, 1 - y), (1 - x, 1 - y)]

        def slot(a, px, py, pc):
            return out_refs[a].at[4 * px + 2 * py + pc]

        def copy(a, k, blk, to, src=None):
            return _remote(slot(a, *blk) if src is None else src, slot(a, *blk),
                           send_sems.at[7 * a + k], recv_sems.at[7 * a + k], to)

        mine = [pltpu.make_async_copy(x_refs[a], slot(a, *me), local_sems.at[a]) for a in range(n)]
        first, passed = [], []
        for a in range(n):
            mine[a].start()
            first.append(copy(a, 0, me, sibling, src=x_refs[a]))
            first += [copy(a, 1 + j, me, (*chip, c), src=x_refs[a]) for j, chip in enumerate(chips)]
        for cp in first:
            cp.start()
        for a in range(n):
            for j, chip in enumerate(chips):
                copy(a, 1 + j, (*chip, c), me).wait_recv()
                passed.append(copy(a, 4 + j, (*chip, c), sibling))
                passed[-1].start()
        for a in range(n):
            copy(a, 0, sibling, me).wait_recv()
            for j, chip in enumerate(chips):
                copy(a, 4 + j, (*chip, 1 - c), me).wait_recv()
        for cp in first + passed:
            cp.wait_send()
        for cp in mine:
            cp.wait()

    return pl.pallas_call(
        body, name="all_gather_weights",
        out_shape=[jax.ShapeDtypeStruct((N_DEV,) + b.shape, b.dtype) for b in blocks],
        in_specs=_hbm_specs(n), out_specs=_hbm_specs(n),
        scratch_shapes=[pltpu.SemaphoreType.DMA((7 * n,)), pltpu.SemaphoreType.DMA((7 * n,)), pltpu.SemaphoreType.DMA((n,))],
    )(*blocks)


def _symmetric_comm(inputs, out_shape, aliases, per_array, copies):
    def start(in_refs, out_refs, sems):
        for cp in copies(in_refs, out_refs, *sems):
            cp.start()

    def finish(in_refs, out_refs, sems):
        for cp in copies(in_refs, out_refs, *sems):
            cp.wait()

    n_sems = per_array * len(inputs)
    return _Comm(tuple(inputs), tuple(out_shape), aliases,
                 (pltpu.SemaphoreType.DMA((n_sems,)), pltpu.SemaphoreType.DMA((n_sems,))), start, finish)


def _ag_direct(shards):
    def copies(in_refs, out_refs, send_sems, recv_sems):
        x, y, c = lax.axis_index("x"), lax.axis_index("y"), lax.axis_index("c")
        peers = [(x, y, 1 - c), (1 - x, y, c), (x, 1 - y, c), (1 - x, 1 - y, c)]
        cps = []
        for a in range(len(shards)):
            mine = out_refs[a].at[4 * x + 2 * y + c]
            cps.append(pltpu.make_async_copy(in_refs[a], mine, send_sems.at[5 * a]))
            cps += [_remote(in_refs[a], mine, send_sems.at[5 * a + k], recv_sems.at[5 * a + k], peer)
                    for k, peer in enumerate(peers, start=1)]
        return cps

    return _symmetric_comm(shards, [jax.ShapeDtypeStruct((N_DEV,) + s.shape, s.dtype) for s in shards], {}, 5, copies)


def _ag_to_all(shards):
    def copies(in_refs, out_refs, send_sems, recv_sems):
        x, y, c = lax.axis_index("x"), lax.axis_index("y"), lax.axis_index("c")
        cps = []
        for a in range(len(shards)):
            mine = out_refs[a].at[4 * x + 2 * y + c]
            cps.append(pltpu.make_async_copy(in_refs[a], mine, send_sems.at[N_DEV * a]))
            for k in range(1, N_DEV):
                peer = (x ^ (k >> 2), y ^ ((k >> 1) & 1), c ^ (k & 1))
                cps.append(_remote(in_refs[a], mine, send_sems.at[N_DEV * a + k], recv_sems.at[N_DEV * a + k], peer))
        return cps

    return _symmetric_comm(shards, [jax.ShapeDtypeStruct((N_DEV,) + s.shape, s.dtype) for s in shards], {}, N_DEV, copies)


def _ag_forward(gathered):
    def copies(in_refs, out_refs, send_sems, recv_sems):
        x, y, c = lax.axis_index("x"), lax.axis_index("y"), lax.axis_index("c")
        chips = [(1 - x, y), (x, 1 - y), (1 - x, 1 - y)]
        return [_remote(out_refs[a].at[4 * cx + 2 * cy + c], out_refs[a].at[4 * cx + 2 * cy + c],
                        send_sems.at[3 * a + j], recv_sems.at[3 * a + j], (x, y, 1 - c))
                for a in range(len(gathered)) for j, (cx, cy) in enumerate(chips)]

    shapes = [jax.ShapeDtypeStruct(g.shape, g.dtype) for g in gathered]
    return _symmetric_comm(gathered, shapes, {a: a for a in range(len(gathered))}, 3, copies)


def _rs_to_sibling(grads):
    def copies(in_refs, out_refs, send_sems, recv_sems):
        x, y, c = lax.axis_index("x"), lax.axis_index("y"), lax.axis_index("c")
        return [_remote(in_refs[a].at[2 * q + 1 - c], out_refs[a].at[q], send_sems.at[4 * a + q], recv_sems.at[4 * a + q], (x, y, 1 - c))
                for a in range(len(grads)) for q in range(4)]

    return _symmetric_comm(grads, [jax.ShapeDtypeStruct((4,) + g.shape[1:], g.dtype) for g in grads], {}, 4, copies)


def _rs_to_chips(parts):
    def copies(in_refs, out_refs, send_sems, recv_sems):
        x, y, c = lax.axis_index("x"), lax.axis_index("y"), lax.axis_index("c")
        chips = [(1 - x, y), (x, 1 - y), (1 - x, 1 - y)]
        return [_remote(in_refs[a].at[2 * cx + cy], out_refs[a].at[k], send_sems.at[3 * a + k], recv_sems.at[3 * a + k], (cx, cy, c))
                for a in range(len(parts)) for k, (cx, cy) in enumerate(chips)]

    return _symmetric_comm(parts, [jax.ShapeDtypeStruct((3,) + p.shape[1:], p.dtype) for p in parts], {}, 3, copies)


def _comm_call(name, comm):
    n_in, n_out = len(comm.inputs), len(comm.out_shape)

    def body(*refs):
        ins, outs, sems = refs[:n_in], refs[n_in:n_in + n_out], refs[n_in + n_out:]
        comm.start(ins, outs, sems)
        comm.finish(ins, outs, sems)

    return pl.pallas_call(
        body, name=name, out_shape=list(comm.out_shape), in_specs=_hbm_specs(n_in), out_specs=_hbm_specs(n_out),
        scratch_shapes=list(comm.scratch), input_output_aliases=dict(comm.aliases),
    )(*comm.inputs)


def _small_all_reduce(parts, loss8):
    n = len(parts)

    def body(*refs):
        p_refs, loss_ref, out_ref, pack, land, send_sems, recv_sems = refs[:n], *refs[n:]
        x, y, c = lax.axis_index("x"), lax.axis_index("y"), lax.axis_index("c")
        me = 4 * x + 2 * y + c
        pack[...] = jnp.zeros_like(pack)
        for r, ref in enumerate(p_refs):
            pack[r:r + 1, 0:ref.shape[1]] = ref[...]
        pack[LOSS_ROW:LOSS_ROW + 1, 0:LANES] = loss_ref[0:1, :]
        land[me] = pack[...]
        cps = []
        for k in range(1, N_DEV):
            peer = (x ^ (k >> 2), y ^ ((k >> 1) & 1), c ^ (k & 1))
            cps.append(_remote(pack, land.at[me], send_sems.at[k - 1], recv_sems.at[k - 1], peer))
        for cp in cps:
            cp.start()
        for cp in cps:
            cp.wait()
        acc = land[0]
        for d in range(1, N_DEV):
            acc = acc + land[d]
        out_ref[...] = acc

    vmem = pl.BlockSpec(memory_space=pltpu.VMEM)
    return pl.pallas_call(
        body, name="small_all_reduce",
        out_shape=jax.ShapeDtypeStruct((SMALL_ROWS, D_MODEL), F32),
        in_specs=[vmem] * (n + 1), out_specs=vmem,
        scratch_shapes=[pltpu.VMEM((SMALL_ROWS, D_MODEL), F32), pltpu.VMEM((N_DEV, SMALL_ROWS, D_MODEL), F32),
                        pltpu.SemaphoreType.DMA((N_DEV - 1,)), pltpu.SemaphoreType.DMA((N_DEV - 1,))],
    )(*parts, loss8)


def _rs_sibling_sum(name, grad, got, place):
    _, rows, cols = grad.shape

    def body(place_ref, g_ref, l_ref, own_ref, b_ref):
        s = g_ref[...] + l_ref[...]
        b_ref[...] = s.astype(BF16)

        @pl.when(pl.program_id(0) == place_ref[1])
        def _():
            own_ref[...] = s

    by_chip = pl.BlockSpec((None, rows, cols), lambda q, place_ref: (q, 0, 0))
    return pl.pallas_call(
        body, name=name,
        grid_spec=pltpu.PrefetchScalarGridSpec(
            num_scalar_prefetch=1, grid=(4,),
            in_specs=[pl.BlockSpec((None, rows, cols), lambda q, place_ref: (2 * q + place_ref[0], 0, 0)), by_chip],
            out_specs=[pl.BlockSpec((rows, cols), lambda q, place_ref: (0, 0)), by_chip]),
        out_shape=(jax.ShapeDtypeStruct((rows, cols), F32), jax.ShapeDtypeStruct((4, rows, cols), BF16)),
        compiler_params=_cparams("arbitrary"),
    )(place, grad, got)


def _adamw_math(w, g, m, v):
    m2 = ADAM_B1 * m + (1.0 - ADAM_B1) * g
    v2 = ADAM_B2 * v + (1.0 - ADAM_B2) * (g * g)
    m_hat = m2 / (1.0 - ADAM_B1 ** ADAM_STEP)
    v_hat = v2 / (1.0 - ADAM_B2 ** ADAM_STEP)
    delta = -ADAM_LR * (m_hat / (jnp.sqrt(v_hat) + ADAM_EPS) + ADAM_WD * w)
    return delta, m2, v2


def _update_tile(rows):
    return 256 if rows % 256 == 0 else rows


def _rs_final_sum(name, own, got):
    def body(o_ref, r_ref, g_out):
        g = o_ref[...]
        for k in range(3):
            g = g + r_ref[k].astype(F32)
        g_out[...] = g

    return pl.pallas_call(body, name=name, out_shape=jax.ShapeDtypeStruct(own.shape, F32))(own, got)


def _adamw_sharded(name, w, m, v, own, got):
    _, rows, cols = w.shape
    tr = _update_tile(rows)

    def body(o_ref, r_ref, w_ref, m_ref, v_ref, g_out, d_out, m_out, v_out):
        g = o_ref[:, 0:cols]
        for k in range(3):
            g = g + r_ref[k, :, 0:cols].astype(F32)
        d, m2, v2 = _adamw_math(w_ref[0], g, m_ref[0], v_ref[0])
        g_out[0] = g
        d_out[0] = d
        m_out[0] = m2
        v_out[0] = v2

    mine = pl.BlockSpec((1, tr, cols), lambda i: (0, i, 0))
    shp = jax.ShapeDtypeStruct(w.shape, F32)
    wide = own.shape[1]
    return pl.pallas_call(
        body, name=name, grid=(rows // tr,), out_shape=(shp,) * 4,
        in_specs=[pl.BlockSpec((tr, wide), lambda i: (i, 0)), pl.BlockSpec((3, tr, wide), lambda i: (0, i, 0)),
                  mine, mine, mine],
        out_specs=[mine] * 4,
        compiler_params=_cparams("parallel"),
    )(own, got, w, m, v)


def _adamw_given(name, g, w, m, v):
    _, rows, cols = w.shape
    tr = _update_tile(rows)

    def body(g_ref, w_ref, m_ref, v_ref, g_out, d_out, m_out, v_out):
        g = g_ref[...]
        d, m2, v2 = _adamw_math(w_ref[0], g, m_ref[0], v_ref[0])
        g_out[0] = g
        d_out[0] = d
        m_out[0] = m2
        v_out[0] = v2

    own = pl.BlockSpec((1, tr, cols), lambda i: (0, i, 0))
    shp = jax.ShapeDtypeStruct(w.shape, F32)
    return pl.pallas_call(
        body, name=name, grid=(rows // tr,), out_shape=(shp,) * 4,
        in_specs=[pl.BlockSpec((tr, cols), lambda i: (i, 0)), own, own, own], out_specs=[own] * 4,
        compiler_params=_cparams("parallel"),
    )(g, w, m, v)


def _adamw_small(total, ws, ms, vs):
    n = len(ws)

    def body(*refs):
        t_ref = refs[0]
        w_refs, m_refs, v_refs = refs[1:1 + n], refs[1 + n:1 + 2 * n], refs[1 + 2 * n:1 + 3 * n]
        outs = refs[1 + 3 * n:]
        outs[0][...] = t_ref[LOSS_ROW:LOSS_ROW + 1, 0:1]
        for r in range(n):
            g = t_ref[r:r + 1, 0:w_refs[r].shape[1]]
            d, m2, v2 = _adamw_math(w_refs[r][...], g, m_refs[r][...], v_refs[r][...])
            for k, val in enumerate((g, d, m2, v2)):
                outs[1 + 4 * r + k][...] = val

    vmem = pl.BlockSpec(memory_space=pltpu.VMEM)
    out_shape = [jax.ShapeDtypeStruct((1, 1), F32)]
    for w in ws:
        out_shape += [jax.ShapeDtypeStruct(w.shape, F32)] * 4
    return pl.pallas_call(
        body, name="adamw_small", out_shape=out_shape,
        in_specs=[vmem] * (1 + 3 * n), out_specs=[vmem] * len(out_shape),
    )(total, *ws, *ms, *vs)


def _rope_tables(pos_col):
    t = pos_col.shape[0]
    inv = (np.float32(ROPE_THETA) ** (-np.arange(0, ROPE, 2, dtype=np.float32) / np.float32(ROPE))).astype(np.float32)
    freq = np.zeros((1, LANES), np.float32)
    freq[0, NOPE:NOPE + ROPE // 2] = inv
    freq[0, NOPE + ROPE // 2:NOPE + ROPE] = inv
    tm = _row_tile(t)

    def body(p_ref, f_ref, c_ref, s_ref):
        ang = p_ref[...].astype(F32) * f_ref[...]
        c_ref[...] = jnp.cos(ang)
        s_ref[...] = jnp.sin(ang)

    shp = jax.ShapeDtypeStruct((t, LANES), F32)
    return pl.pallas_call(
        body, name="rope_tables", grid=(t // tm,), out_shape=(shp, shp),
        in_specs=[pl.BlockSpec((tm, 1), lambda i: (i, 0)), pl.BlockSpec((1, LANES), lambda i: (0, 0))],
        out_specs=(pl.BlockSpec((tm, LANES), lambda i: (i, 0)),) * 2,
        compiler_params=_cparams("parallel"),
    )(pos_col, jnp.asarray(freq))


def _in_proj(x, g, w_qkv, w_rest, comm=None):
    t = x.shape[0]
    tm = _row_tile(t)

    def main(ins, outs, scr):
        x_ref, g_ref, wq_ref, wr_ref = ins
        h_ref, fq_ref, fk_ref, fv_ref, r_ref = outs
        h, _ = _rms(x_ref[...], g_ref[...])
        hb = h.astype(BF16)
        h_ref[...] = hb
        for n, (ref, factor) in enumerate(((fq_ref, FOX_Q_FACTOR), (fk_ref, None), (fv_ref, None))):
            part = _dot_nt(hb, wq_ref[n * FOX_WIDTH:(n + 1) * FOX_WIDTH, :])
            ref[...] = (part if factor is None else part * factor).astype(BF16)
        r_ref[...] = _dot_nt(hb, wr_ref[...])

    row = lambda n: pl.BlockSpec((tm, n), lambda i: (i, 0))
    full = lambda a: pl.BlockSpec(a.shape, lambda i: (0,) * a.ndim)
    return _hosted_call(
        "in_proj", main, (t // tm,), [x, g, w_qkv, w_rest], [row(D_MODEL), full(g), full(w_qkv), full(w_rest)],
        (jax.ShapeDtypeStruct((t, D_MODEL), BF16),) + (jax.ShapeDtypeStruct((t, FOX_WIDTH), BF16),) * 3
        + (jax.ShapeDtypeStruct((t, REST_COLS), F32),),
        (row(D_MODEL), row(FOX_WIDTH), row(FOX_WIDTH), row(FOX_WIDTH), row(REST_COLS)), [], comm)


def _log_sigmoid(z):
    return jnp.minimum(z, 0.0) - jnp.log(1.0 + jnp.exp(-jnp.abs(z)))


def _split3(v):
    hi = v.astype(BF16)
    r1 = v - hi.astype(F32)
    mid = r1.astype(BF16)
    lo = (r1 - mid.astype(F32)).astype(BF16)
    return hi, mid, lo


def _scan_tile(t):
    return 512 if t >= 2048 else (256 if t >= 256 else t)


def _forget_cumsum(rest, b128):
    t = rest.shape[0]
    tb = _scan_tile(t)

    def body(r_ref, b_ref, row_ref, rep_ref, f_sc, carry):
        @pl.when(pl.program_id(0) == 0)
        def _():
            carry[...] = jnp.zeros_like(carry)
        lf = _log_sigmoid(r_ref[...] + b_ref[...])
        tri = (lax.broadcasted_iota(jnp.int32, (tb, tb), 0) >= lax.broadcasted_iota(jnp.int32, (tb, tb), 1)).astype(BF16)
        hi, mid, lo = _split3(lf)
        f_sc[...] = (_dot(tri, hi) + _dot(tri, mid)) + _dot(tri, lo) + carry[...]
        carry[...] = f_sc[tb - 1:tb, :]
        f2 = f_sc[...] * LOG2E
        row_ref[...] = jnp.transpose(f2)[0:HEADS, :]
        lane = _lane()
        for h in range(HEADS):
            col = jnp.sum(jnp.where(lane == h, f2, 0.0), axis=1, keepdims=True)
            rep_ref[h] = jnp.broadcast_to(col, (tb, LANES))

    return pl.pallas_call(
        body, name="forget_cumsum", grid=(t // tb,),
        out_shape=(jax.ShapeDtypeStruct((HEADS, t), F32), jax.ShapeDtypeStruct((HEADS, t, LANES), F32)),
        in_specs=[pl.BlockSpec((tb, LANES), lambda i: (i, 0)), pl.BlockSpec((1, LANES), lambda i: (0, 0))],
        out_specs=(pl.BlockSpec((HEADS, tb), lambda i: (0, i)), pl.BlockSpec((HEADS, tb, LANES), lambda i: (0, i, 0))),
        scratch_shapes=[pltpu.VMEM((tb, LANES), F32), pltpu.VMEM((1, LANES), F32)],
        compiler_params=_cparams("arbitrary"),
    )(rest, b128)


def _forget_bwd(rest, b128, d_fq, d_fk):
    t = rest.shape[0]
    tb = _scan_tile(t)
    nb = t // tb

    def body(r_ref, b_ref, dfq_ref, dfk_ref, dz_ref, db_ref, carry):
        @pl.when(pl.program_id(0) == 0)
        def _():
            carry[...] = jnp.zeros_like(carry)
            db_ref[...] = jnp.zeros_like(db_ref)
        tri = (lax.broadcasted_iota(jnp.int32, (tb, tb), 0) <= lax.broadcasted_iota(jnp.int32, (tb, tb), 1)).astype(BF16)
        lane = _lane()
        df = jnp.zeros((tb, LANES), F32)
        for h in range(HEADS):
            df = df + jnp.where(lane == h, dfq_ref[h] + dfk_ref[h], 0.0)
        hi, mid, lo = _split3(df)
        dlf = (_dot(tri, hi) + _dot(tri, mid)) + _dot(tri, lo) + carry[...]
        z = r_ref[...] + b_ref[...]
        dz = dlf / (1.0 + jnp.exp(z))
        dz_ref[...] = dz
        db_ref[...] += jnp.sum(dz, axis=0, keepdims=True)
        carry[...] = carry[...] + jnp.sum(df, axis=0, keepdims=True)

    rev = lambda i: (nb - 1 - i, 0)
    rev3 = pl.BlockSpec((HEADS, tb, LANES), lambda i: (0, nb - 1 - i, 0))
    return pl.pallas_call(
        body, name="forget_bwd", grid=(nb,),
        out_shape=(jax.ShapeDtypeStruct((t, LANES), F32), jax.ShapeDtypeStruct((1, LANES), F32)),
        in_specs=[pl.BlockSpec((tb, LANES), rev), pl.BlockSpec((1, LANES), lambda i: (0, 0)), rev3, rev3],
        out_specs=(pl.BlockSpec((tb, LANES), rev), pl.BlockSpec((1, LANES), lambda i: (0, 0))),
        scratch_shapes=[pltpu.VMEM((1, LANES), F32)],
        compiler_params=_cparams("arbitrary"),
    )(rest, b128, d_fq, d_fk)


def _mla_prep(rest, gq, gkv, wq, wkv, cos, sin):
    t = rest.shape[0]
    tm = _row_tile(t)

    def body(r_ref, gq_ref, gkv_ref, wq_ref, wkv_ref, c_ref, s_ref, q_ref, k_ref, kv_ref, cq_ref, ckv_ref):
        cos_, sin_ = c_ref[...], s_ref[...]
        cq, _ = _rms(r_ref[:, REST_CQ:REST_CKV], gq_ref[...])
        ckv, _ = _rms(r_ref[:, REST_CKV:REST_KR], gkv_ref[...])
        cqb, ckvb = cq.astype(BF16), ckv.astype(BF16)
        cq_ref[...] = cqb
        ckv_ref[...] = ckvb
        k_rope = _rope(r_ref[:, REST_KR:REST_COLS], cos_, sin_)
        lo = _lane() < NOPE
        for h in range(HEADS):
            q_ref[h] = (_rope(_dot(cqb, wq_ref[h]), cos_, sin_) * MLA_Q_FACTOR).astype(BF16)
            kv = _dot(ckvb, wkv_ref[h])
            kv_ref[h] = kv.astype(BF16)
            k_ref[h] = (jnp.where(lo, kv, 0.0) + k_rope).astype(BF16)

    row = lambda n: pl.BlockSpec((tm, n), lambda i: (i, 0))
    full = lambda a: pl.BlockSpec(a.shape, lambda i: (0,) * a.ndim)
    heads = pl.BlockSpec((HEADS, tm, LANES), lambda i: (0, i, 0))
    hshape = jax.ShapeDtypeStruct((HEADS, t, LANES), BF16)
    return pl.pallas_call(
        body, name="mla_prep", grid=(t // tm,),
        out_shape=(hshape, hshape, hshape, jax.ShapeDtypeStruct((t, Q_RANK), BF16), jax.ShapeDtypeStruct((t, KV_RANK), BF16)),
        in_specs=[row(REST_COLS), full(gq), full(gkv), full(wq), full(wkv), row(LANES), row(LANES)],
        out_specs=(heads, heads, heads, row(Q_RANK), row(KV_RANK)),
        compiler_params=_cparams("parallel"),
    )(rest, gq, gkv, wq, wkv, cos, sin)


def _tile_lanes(x, n):
    return jnp.tile(x, (1, n)) if n > 1 else x


class _Comm(NamedTuple):
    inputs: tuple
    out_shape: tuple
    aliases: dict
    scratch: tuple
    start: Callable
    finish: Callable


def _hosted_call(name, main, grid, args, in_specs, out_shape, out_specs, scratch, comm):
    n_in, n_out, n_scr = len(args), len(out_shape), len(scratch)
    c_in = list(comm.inputs) if comm else []
    c_out = list(comm.out_shape) if comm else []

    def at_step(which):
        hit = pl.program_id(0) == which[0]
        for axis in range(1, len(grid)):
            hit = jnp.logical_and(hit, pl.program_id(axis) == which[axis])
        return hit

    def body(*refs):
        bounds = [0, n_in, len(c_in), n_out, len(c_out), n_scr]
        starts = [sum(bounds[:k + 1]) for k in range(len(bounds))]
        ins, cins, outs, couts, scr = [refs[a:b] for a, b in zip(starts[:-1], starts[1:])]
        sems = refs[starts[-1]:]
        if comm:
            @pl.when(at_step([0] * len(grid)))
            def _():
                comm.start(cins, couts, sems)
        main(ins, outs, scr)
        if comm:
            @pl.when(at_step([n - 1 for n in grid]))
            def _():
                comm.finish(cins, couts, sems)

    res = pl.pallas_call(
        body, name=name, grid=grid,
        out_shape=list(out_shape) + c_out,
        in_specs=list(in_specs) + _hbm_specs(len(c_in)),
        out_specs=list(out_specs) + _hbm_specs(len(c_out)),
        scratch_shapes=list(scratch) + (list(comm.scratch) if comm else []),
        input_output_aliases={n_in + i: n_out + o for i, o in comm.aliases.items()} if comm else {},
        compiler_params=_cparams(*(["arbitrary"] * len(grid))),
    )(*args, *c_in)
    return res[:n_out], res[n_out:]


def _stat_rows(x):
    return jnp.transpose(x)[0:8, :]


FWD_HEADS = 4


def _attn_fwd(fox, q, k, v, f2_rows=None, comm=None):
    t = q.shape[0] if fox else q.shape[1]
    tq = _row_tile(t)
    nq = t // tq
    nh = FWD_HEADS
    wide = (nh // 2) * LANES

    def main(ins, outs, scr):
        q_ref, k_ref, v_ref = ins[:3]
        fr_ref = ins[3] if fox else None
        o_ref, lset_ref = outs
        m_sc, acc_sc = scr
        i = pl.program_id(1)
        lo = _lane() < HEAD_DIM
        hi = jnp.logical_not(lo)
        zero, one = jnp.zeros((), BF16), jnp.ones((), BF16)
        lanes_of = lambda h: slice((h // 2) * LANES, (h // 2 + 1) * LANES)
        if fox:
            qs = [jnp.where(lo if h % 2 == 0 else hi, q_ref[:, lanes_of(h)], zero) for h in range(nh)]
            sum_lanes = [hi if h % 2 == 0 else lo for h in range(nh)]
        else:
            qs = [q_ref[h] for h in range(nh)]
            sum_lanes = [lo] * nh
        m_sc[...] = jnp.full_like(m_sc, -jnp.inf)
        acc_sc[...] = jnp.zeros_like(acc_sc)

        def block(j, r0, nr, c0, nc, seen_from):
            rows = slice(r0, r0 + nr)
            sl = pl.ds(pl.multiple_of(j * tq + c0, math.gcd(tq, c0) if c0 else tq), nc)
            if seen_from is not None:
                seen = (lax.broadcasted_iota(jnp.int32, (nr, nc), 1)
                        <= lax.broadcasted_iota(jnp.int32, (nr, nc), 0) + seen_from)
            for h in range(nh):
                kj, vj = (k_ref[sl, lanes_of(h)], v_ref[sl, lanes_of(h)]) if fox else (k_ref[h, sl, :], v_ref[h, sl, :])
                s = _dot_nt(qs[h][rows], kj)
                if fox and nc > tq:
                    s = s - jnp.concatenate([fr_ref[h, j + b] for b in range(nc // tq)], axis=1)
                elif fox:
                    s = s - fr_ref[h, j, :, c0:c0 + nc]
                if seen_from is not None:
                    s = jnp.where(seen, s, -jnp.inf)
                m_prev = m_sc[h, rows]
                m_new = jnp.maximum(m_prev, jnp.max(s, axis=1, keepdims=True))
                p = jnp.exp2((s - _tile_lanes(m_new, nc // LANES)).astype(BF16))
                vj = jnp.where(sum_lanes[h], one, vj)
                acc_sc[h, rows] = jnp.exp2(m_prev - m_new) * acc_sc[h, rows] + _dot(p, vj)
                m_sc[h, rows] = m_new

        def loop_body(jj, carry):
            block(4 * jj, 0, tq, 0, 4 * tq, None)
            return carry

        lax.fori_loop(0, i // 4, loop_body, 0)
        rest = i % 4

        @pl.when(rest >= 2)
        def _():
            block(i - rest, 0, tq, 0, 2 * tq, None)

        @pl.when(rest % 2 == 1)
        def _():
            block(i - 1, 0, tq, 0, tq, None)

        half = tq // 2
        if half % LANES == 0:
            block(i, 0, half, 0, half, 0)
            block(i, half, half, 0, tq, half)
        else:
            block(i, 0, tq, 0, tq, 0)
        res = []
        for h in range(nh):
            acc = acc_sc[h]
            swapped = pltpu.roll(acc, HEAD_DIM, 1)
            res.append(acc / swapped)
            lse2 = m_sc[h] + jnp.log(jnp.where(sum_lanes[h], acc, swapped)) * LOG2E
            lset_ref[h, 0] = _stat_rows(lse2)
        for pr in range(nh // 2):
            even = res[2 * pr] if fox else pltpu.roll(res[2 * pr], HEAD_DIM, 1)
            o_ref[:, pr * LANES:(pr + 1) * LANES] = jnp.where(lo, even, res[2 * pr + 1])

    if fox:
        in_specs = [pl.BlockSpec((tq, wide), lambda g, i: (i, g))] + [pl.BlockSpec((t, wide), lambda g, i: (0, g))] * 2
        in_specs += [pl.BlockSpec((nh, nq, 1, tq), lambda g, i: (g, 0, 0, 0))]
        args = [q, k, v, f2_rows]
    else:
        in_specs = [pl.BlockSpec((nh, tq, LANES), lambda g, i: (g, i, 0))] + [pl.BlockSpec((nh, t, LANES), lambda g, i: (g, 0, 0))] * 2
        args = [q, k, v]
    return _hosted_call(
        "fox_attn_fwd" if fox else "mla_attn_fwd", main, (HEADS // nh, nq), args, in_specs,
        (jax.ShapeDtypeStruct((t, 4 * LANES), F32), jax.ShapeDtypeStruct((HEADS, nq, 8, tq), F32)),
        (pl.BlockSpec((tq, wide), lambda g, i: (i, g)), pl.BlockSpec((nh, 1, 8, tq), lambda g, i: (g, i, 0, 0))),
        [pltpu.VMEM((nh, tq, LANES), F32), pltpu.VMEM((nh, tq, LANES), F32)], comm)


def _head_do(fox, hh, do2, lo):
    if fox:
        return jnp.where(lo if hh == 0 else jnp.logical_not(lo), do2, 0.0)
    return jnp.where(lo, 0.0, pltpu.roll(do2, HEAD_DIM, 1) if hh == 0 else do2)


BWD_HEADS = 4


def _attn_bwd(fox, q, k, v, do, lse_rows, delta_rows, f2_rep=None, comm=None):
    t = q.shape[0] if fox else q.shape[1]
    tq = _row_tile(t)
    nq = t // tq
    scale = FOX_SCALE if fox else MLA_SCALE
    nh = BWD_HEADS

    def main(ins, outs, scr):
        if fox:
            q_ref, k_ref, v_ref, f_ref, do_ref, lse_ref, dl_ref = ins
            dq_ref, dk_ref, dv_ref, dfq_ref, dfk_ref = outs
        else:
            q_ref, k_ref, v_ref, do_ref, lse_ref, dl_ref = ins
            dq_ref, dkv_ref, dkr_ref = outs
        dq_sc, dk_sc, dv_sc = scr
        j = pl.program_id(1)
        lane = _lane()
        lo = lane < HEAD_DIM
        hi = jnp.logical_not(lo)
        zero, one = jnp.zeros((), BF16), jnp.ones((), BF16)
        lanes_of = lambda h: slice((h // 2) * LANES, (h // 2 + 1) * LANES)

        @pl.when(j == 0)
        def _():
            dq_sc[...] = jnp.zeros_like(dq_sc)

        dk_sc[...] = jnp.zeros_like(dk_sc)
        dv_sc[...] = jnp.zeros_like(dv_sc)

        def block(i, r0, nr, c0, nc, masked):
            rows, cols = slice(r0, r0 + nr), slice(c0, c0 + nc)
            sl = pl.ds(pl.multiple_of(i * tq + c0, math.gcd(tq, c0) if c0 else tq), nc)
            if masked:
                seen = lax.broadcasted_iota(jnp.int32, (nr, nc), 1) >= lax.broadcasted_iota(jnp.int32, (nr, nc), 0)
            for h in range(nh):
                hh = h % 2
                kj = k_ref[rows, lanes_of(h)] if fox else k_ref[h, rows, :]
                vj = v_ref[rows, lanes_of(h)] if fox else v_ref[h, rows, :]
                qi = jnp.where(lo if hh == 0 else hi, q_ref[sl, lanes_of(h)], zero) if fox else q_ref[h, sl, :]
                dob = _head_do(fox, hh, do_ref[sl, lanes_of(h)], lo).astype(BF16)
                st = _dot_nt(kj, qi)
                if fox:
                    st = st - _tile_lanes(f_ref[h, rows, :], nc // LANES)
                if masked:
                    st = jnp.where(seen, st, -jnp.inf)
                pt = jnp.exp2(st - lse_ref[h, i, 0:1, cols])
                dpt = _dot_nt(vj, dob)
                dst = (pt * (dpt - dl_ref[h, i, 0:1, cols])).astype(BF16)
                dv_sc[h, rows] += _dot(pt.astype(BF16), dob)
                if fox:
                    other = hi if hh == 0 else lo
                    qi = jnp.where(other, one, qi)
                    kj = jnp.where(other, one, kj)
                dk_sc[h, rows] += _dot(dst, qi)
                dq_sc[h, sl, :] += _dot_tn(dst, kj)

        def loop_body(i, carry):
            block(i, 0, tq, 0, tq, False)
            return carry

        half = tq // 2
        if half % LANES == 0:
            block(j, 0, half, 0, tq, True)
            block(j, half, half, half, half, True)
        else:
            block(j, 0, tq, 0, tq, True)
        lax.fori_loop(j + 1, nq, loop_body, 0)
        rope_lanes = jnp.logical_and(lane >= NOPE, lane < NOPE + ROPE)
        for pr in range(nh // 2):
            a, b = 2 * pr, 2 * pr + 1
            if fox:
                dk_ref[:, lanes_of(a)] = (jnp.where(lo, dk_sc[a], dk_sc[b]) * LN2).astype(BF16)
                dv_ref[:, lanes_of(a)] = (dv_sc[a] + dv_sc[b]).astype(BF16)
                for h in (a, b):
                    dk = dk_sc[h]
                    dfk_ref[h] = -jnp.where(hi if h == a else lo, dk, pltpu.roll(dk, HEAD_DIM, 1))
            else:
                dkr = jnp.zeros((tq, LANES), F32)
                for h in (a, b):
                    dk = dk_sc[h] * LN2
                    dkv_ref[h] = jnp.where(lo, dk, dv_sc[h])
                    dkr = dkr + jnp.where(rope_lanes, dk, 0.0)
                dkr_ref[pr] = dkr

        @pl.when(j == nq - 1)
        def _():
            for i in range(nq):
                rows = slice(i * tq, (i + 1) * tq)
                for pr in range(nh // 2):
                    a, b = 2 * pr, 2 * pr + 1
                    if fox:
                        dq_ref[rows, lanes_of(a)] = (jnp.where(lo, dq_sc[a, rows, :], dq_sc[b, rows, :]) * scale).astype(BF16)
                        for h in (a, b):
                            acc = dq_sc[h, rows, :]
                            dfq_ref[h, rows, :] = jnp.where(hi if h == a else lo, acc, pltpu.roll(acc, HEAD_DIM, 1))
                    else:
                        for h in (a, b):
                            dq_ref[h, rows, :] = dq_sc[h, rows, :] * scale

    wide_cols = (nh // 2) * LANES
    stat = pl.BlockSpec((nh, tq, LANES), lambda p, j: (p, j, 0))
    stat_all = pl.BlockSpec((nh, t, LANES), lambda p, j: (p, 0, 0))
    rows4 = pl.BlockSpec((nh, nq, 8, tq), lambda p, j: (p, 0, 0, 0))
    pair = pl.BlockSpec((tq, wide_cols), lambda p, j: (j, p))
    pair_all = pl.BlockSpec((t, wide_cols), lambda p, j: (0, p))
    if fox:
        in_specs = [pair_all, pair, pair, stat]
        args = [q, k, v, f2_rep]
    else:
        in_specs = [stat_all, stat, stat]
        args = [q, k, v]
    in_specs += [pair_all, rows4, rows4]
    args += [do, lse_rows, delta_rows]
    heads_f32 = jax.ShapeDtypeStruct((HEADS, t, LANES), F32)
    if fox:
        wide = jax.ShapeDtypeStruct((t, 4 * LANES), BF16)
        out_shape = (wide, wide, wide, heads_f32, heads_f32)
        out_specs = (pair_all, pair, pair, stat_all, stat)
    else:
        out_shape = (heads_f32, heads_f32, jax.ShapeDtypeStruct((HEADS // 2, t, LANES), F32))
        out_specs = (stat_all, stat, pl.BlockSpec((nh // 2, tq, LANES), lambda p, j: (p, j, 0)))
    acc = pltpu.VMEM((nh, tq, LANES), F32)
    return _hosted_call("fox_attn_bwd" if fox else "mla_attn_bwd", main, (HEADS // nh, nq), args, in_specs,
                        out_shape, out_specs, [pltpu.VMEM((nh, t, LANES), F32), acc, acc], comm)


def _attn_out(x, fox_o, mla_o, gf, gm, w_o):
    t = x.shape[0]
    tm = _row_tile(t)

    def body(x_ref, f_ref, m_ref, gf_ref, gm_ref, w_ref, x1_ref, mix_ref):
        nf, _ = _rms(f_ref[...], gf_ref[...])
        nm, _ = _rms(m_ref[...], gm_ref[...])
        nfb, nmb = nf.astype(BF16), nm.astype(BF16)
        mix_ref[:, :FOX_WIDTH] = nfb
        mix_ref[:, FOX_WIDTH:] = nmb
        x1_ref[...] = x_ref[...] + _dot(nfb, w_ref[:FOX_WIDTH, :]) + _dot(nmb, w_ref[FOX_WIDTH:, :])

    row = lambda n: pl.BlockSpec((tm, n), lambda i: (i, 0))
    full = lambda a: pl.BlockSpec(a.shape, lambda i: (0,) * a.ndim)
    return pl.pallas_call(
        body, name="attn_out", grid=(t // tm,),
        out_shape=(jax.ShapeDtypeStruct((t, D_MODEL), F32), jax.ShapeDtypeStruct((t, D_MODEL), BF16)),
        in_specs=[row(D_MODEL), row(FOX_WIDTH), row(MLA_WIDTH), full(gf), full(gm), full(w_o)],
        out_specs=(row(D_MODEL), row(D_MODEL)),
        compiler_params=_cparams("parallel"),
    )(x, fox_o, mla_o, gf, gm, w_o)


def _mlp_tile(t):
    return 256 if t >= 2048 else 128


def _resident(a):
    return pl.BlockSpec(a.shape, lambda i: (0,) * a.ndim, pipeline_mode=pl.Buffered(1))


FF_CHUNK = 512


def _mlp_fwd(x1, g_mlp, w_up, w_down, g_fin, target):
    t = x1.shape[0]
    tm = _mlp_tile(t)

    def body(x_ref, g_ref, wu_ref, wd_ref, gf_ref, t_ref, u_ref, h_ref, dx_ref, dxb_ref, loss_ref, dg_ref, a_sc):
        @pl.when(pl.program_id(0) == 0)
        def _():
            loss_ref[...] = jnp.zeros_like(loss_ref)
            dg_ref[...] = jnp.zeros_like(dg_ref)

        x = x_ref[...]
        h, _ = _rms(x, g_ref[...])
        hb = h.astype(BF16)
        h_ref[...] = hb
        for f in range(D_FF // FF_CHUNK):
            sl = slice(f * FF_CHUNK, (f + 1) * FF_CHUNK)
            u = _dot(hb, wu_ref[:, sl])
            u_ref[:, sl] = u
            r = jnp.maximum(u, 0.0)
            a_sc[:, sl] = (r * r).astype(BF16)
        x2 = x + _dot(a_sc[...], wd_ref[...])
        y, r2 = _rms(x2, gf_ref[...])
        err = y - t_ref[...]
        loss_ref[...] += 0.5 * jnp.sum(jnp.mean(err * err, axis=-1, keepdims=True))
        dx, dg = _rms_bwd(x2, gf_ref[...], r2, err * (1.0 / D_MODEL))
        dx_ref[...] = dx
        dxb_ref[...] = dx.astype(BF16)
        dg_ref[...] += dg

    row = lambda n: pl.BlockSpec((tm, n), lambda i: (i, 0))
    vec = pl.BlockSpec((1, D_MODEL), lambda i: (0, 0))
    return pl.pallas_call(
        body, name="mlp_fwd", grid=(t // tm,),
        out_shape=(jax.ShapeDtypeStruct((t, D_FF), F32), jax.ShapeDtypeStruct((t, D_MODEL), BF16),
                   jax.ShapeDtypeStruct((t, D_MODEL), F32), jax.ShapeDtypeStruct((t, D_MODEL), BF16),
                   jax.ShapeDtypeStruct((8, LANES), F32), jax.ShapeDtypeStruct((1, D_MODEL), F32)),
        in_specs=[row(D_MODEL), vec, _resident(w_up), _resident(w_down), vec, row(D_MODEL)],
        out_specs=(row(D_FF), row(D_MODEL), row(D_MODEL), row(D_MODEL), pl.BlockSpec((8, LANES), lambda i: (0, 0)), vec),
        scratch_shapes=[pltpu.VMEM((tm, D_FF), BF16)],
        compiler_params=_cparams("arbitrary"),
    )(x1, g_mlp, w_up, w_down, g_fin, target)


def _mlp_bwd(dx2, u, x1, g_mlp, w_up, w_down):
    t = x1.shape[0]
    tm = _mlp_tile(t)

    def body(dx_ref, u_ref, x_ref, g_ref, wu_ref, wd_ref, du_ref, a_ref, dx1_ref, dx1b_ref, dg_ref):
        @pl.when(pl.program_id(0) == 0)
        def _():
            dg_ref[...] = jnp.zeros_like(dg_ref)

        dx2 = dx_ref[...]
        dxb = dx2.astype(BF16)
        for f in range(D_FF // FF_CHUNK):
            sl = slice(f * FF_CHUNK, (f + 1) * FF_CHUNK)
            r = jnp.maximum(u_ref[:, sl], 0.0)
            a_ref[:, sl] = (r * r).astype(BF16)
            da = _dot_nt(dxb, wd_ref[sl, :])
            du_ref[:, sl] = (da * (2.0 * r)).astype(BF16)
        dh = _dot_nt(du_ref[...], wu_ref[...])
        x = x_ref[...]
        _, r1 = _rms(x, g_ref[...])
        dx, dg = _rms_bwd(x, g_ref[...], r1, dh)
        dx1 = dx2 + dx
        dx1_ref[...] = dx1
        dx1b_ref[...] = dx1.astype(BF16)
        dg_ref[...] += dg

    row = lambda n: pl.BlockSpec((tm, n), lambda i: (i, 0))
    vec = pl.BlockSpec((1, D_MODEL), lambda i: (0, 0))
    return pl.pallas_call(
        body, name="mlp_bwd", grid=(t // tm,),
        out_shape=(jax.ShapeDtypeStruct((t, D_FF), BF16), jax.ShapeDtypeStruct((t, D_FF), BF16),
                   jax.ShapeDtypeStruct((t, D_MODEL), F32), jax.ShapeDtypeStruct((t, D_MODEL), BF16),
                   jax.ShapeDtypeStruct((1, D_MODEL), F32)),
        in_specs=[row(D_MODEL), row(D_FF), row(D_MODEL), vec, _resident(w_up), _resident(w_down)],
        out_specs=(row(D_FF), row(D_FF), row(D_MODEL), row(D_MODEL), vec),
        compiler_params=_cparams("arbitrary"),
    )(dx2, u, x1, g_mlp, w_up, w_down)


def _matmul_tn(name, a, b, blocks=None):
    t, m = a.shape
    n = b.shape[1]
    tk = t if a.dtype == BF16 and b.dtype == BF16 else min(t, 2048)
    steps = t // tk
    bm = m if m <= 1024 else 512
    bn = n if n <= 1024 else 512
    width = bn if blocks is None else n // blocks
    per = bn // width

    def body(a_ref, b_ref, o_ref, acc_sc):
        kk = pl.program_id(2)

        @pl.when(kk == 0)
        def _():
            acc_sc[...] = jnp.zeros_like(acc_sc)

        acc_sc[...] += _dot_tn(a_ref[...].astype(BF16), b_ref[...].astype(BF16))

        @pl.when(kk == steps - 1)
        def _():
            if blocks is None:
                o_ref[...] = acc_sc[...]
            else:
                for s in range(per):
                    o_ref[s] = acc_sc[:, s * width:(s + 1) * width]

    if blocks is None:
        o_spec = pl.BlockSpec((bm, bn), lambda i, j, kk: (i, j))
        o_shape = (m, n)
    else:
        o_spec = pl.BlockSpec((per, bm, width), lambda i, j, kk: (j, i, 0))
        o_shape = (blocks, m, width)
    return pl.pallas_call(
        body, name=name, grid=(m // bm, n // bn, steps),
        out_shape=jax.ShapeDtypeStruct(o_shape, F32),
        in_specs=[pl.BlockSpec((tk, bm), lambda i, j, kk: (kk, i)), pl.BlockSpec((tk, bn), lambda i, j, kk: (kk, j))],
        out_specs=o_spec,
        scratch_shapes=[pltpu.VMEM((bm, bn), F32)],
        compiler_params=_cparams("parallel", "parallel", "arbitrary"),
    )(a, b)


def _dw_in(dfq, dfk, dfv, drest, h1):
    t = h1.shape[0]
    tk = min(t, 2048)
    off_ff = 3 * FOX_WIDTH
    off_cq = off_ff + HEADS
    off_kr = IN_COLS - ROPE

    def body(dq_ref, dk_ref, dv_ref, dr_ref, h_ref, o_ref):
        h = h_ref[...]
        r = _dot_tn(dr_ref[...], h)
        parts = [(slice(n * FOX_WIDTH, (n + 1) * FOX_WIDTH), _dot_tn(ref[...], h)) for n, ref in enumerate((dq_ref, dk_ref, dv_ref))]
        parts += [(slice(off_ff, off_cq), r[0:HEADS]), (slice(off_cq, off_kr), r[REST_CQ:REST_KR]),
                  (slice(off_kr, IN_COLS), r[REST_KR + NOPE:REST_KR + NOPE + ROPE])]

        @pl.when(pl.program_id(0) == 0)
        def _():
            for rows, val in parts:
                o_ref[rows, :] = val

        @pl.when(pl.program_id(0) > 0)
        def _():
            for rows, val in parts:
                o_ref[rows, :] += val

    tok = lambda n: pl.BlockSpec((tk, n), lambda kk: (kk, 0))
    return pl.pallas_call(
        body, name="dw_in", grid=(t // tk,),
        out_shape=jax.ShapeDtypeStruct((IN_COLS, D_MODEL), F32),
        in_specs=[tok(FOX_WIDTH), tok(FOX_WIDTH), tok(FOX_WIDTH), tok(REST_COLS), tok(D_MODEL)],
        out_specs=pl.BlockSpec((IN_COLS, D_MODEL), lambda kk: (0, 0)),
        compiler_params=_cparams("arbitrary"),
    )(dfq, dfk, dfv, drest, h1)


def _attn_out_bwd(dx1, fox_o, mla_o, gf, gm, w_o):
    t = dx1.shape[0]
    tm = _row_tile(t)

    def body(dx_ref, f_ref, m_ref, gf_ref, gm_ref, w_ref, df_ref, dm_ref, dlf_ref, dlm_ref, dgf_ref, dgm_ref):
        @pl.when(pl.program_id(0) == 0)
        def _():
            dgf_ref[...] = jnp.zeros_like(dgf_ref)
            dgm_ref[...] = jnp.zeros_like(dgm_ref)
        dxb = dx_ref[...].astype(BF16)
        lane = lax.broadcasted_iota(jnp.int32, (8, LANES), 1)
        upper = lax.broadcasted_iota(jnp.int32, (8, LANES), 0) < 4
        pick = jnp.where(upper, (lane < HEAD_DIM).astype(F32), (lane >= HEAD_DIM).astype(F32)).astype(BF16)
        for o_ref, g_ref, lo_row, d_ref, dl_ref, dg_ref in ((f_ref, gf_ref, 0, df_ref, dlf_ref, dgf_ref),
                                                             (m_ref, gm_ref, FOX_WIDTH, dm_ref, dlm_ref, dgm_ref)):
            dn = _dot_nt(dxb, w_ref[lo_row:lo_row + FOX_WIDTH, :])
            o = o_ref[...]
            _, r = _rms(o, g_ref[...])
            d, dg = _rms_bwd(o, g_ref[...], r, dn)
            d_ref[...] = d
            dg_ref[...] += dg
            prod = d * o
            for pr in range(HEADS // 2):
                parts = _split3(prod[:, pr * LANES:(pr + 1) * LANES])
                both = (_dot_nt(pick, parts[0]) + _dot_nt(pick, parts[1])) + _dot_nt(pick, parts[2])
                dl_ref[2 * pr, 0] = both
                dl_ref[2 * pr + 1, 0] = pltpu.roll(both, 4, 0)

    row = lambda n: pl.BlockSpec((tm, n), lambda i: (i, 0))
    full = lambda a: pl.BlockSpec(a.shape, lambda i: (0,) * a.ndim)
    vec = pl.BlockSpec((1, FOX_WIDTH), lambda i: (0, 0))
    rows = pl.BlockSpec((HEADS, 1, 8, tm), lambda i: (0, i, 0, 0))
    o_shape = jax.ShapeDtypeStruct((t, FOX_WIDTH), F32)
    g_shape = jax.ShapeDtypeStruct((1, FOX_WIDTH), F32)
    r_shape = jax.ShapeDtypeStruct((HEADS, t // tm, 8, tm), F32)
    return pl.pallas_call(
        body, name="attn_out_bwd", grid=(t // tm,),
        out_shape=(o_shape, o_shape, r_shape, r_shape, g_shape, g_shape),
        in_specs=[row(D_MODEL), row(FOX_WIDTH), row(MLA_WIDTH), full(gf), full(gm), full(w_o)],
        out_specs=(row(FOX_WIDTH), row(MLA_WIDTH), rows, rows, vec, vec),
        compiler_params=_cparams("arbitrary"),
    )(dx1, fox_o, mla_o, gf, gm, w_o)


def _mla_prep_bwd(dq, dkv, dkr, dz, rest, gq, gkv, wq, wkv, cos, sin):
    t = rest.shape[0]
    tm = _row_tile(t)

    def body(dq_ref, dkv_ref, dkr_ref, dz_ref, r_ref, gq_ref, gkv_ref, wq_ref, wkv_ref, c_ref, s_ref,
             dr_ref, dqp_ref, dkvb_ref, dgq_ref, dgkv_ref):
        @pl.when(pl.program_id(0) == 0)
        def _():
            dgq_ref[...] = jnp.zeros_like(dgq_ref)
            dgkv_ref[...] = jnp.zeros_like(dgkv_ref)
        cos_, sin_ = c_ref[...], s_ref[...]
        dcq = jnp.zeros((tm, Q_RANK), F32)
        dckv = jnp.zeros((tm, KV_RANK), F32)
        for h in range(HEADS):
            dqp = _rope_bwd(dq_ref[h], cos_, sin_).astype(BF16)
            dqp_ref[:, h * LANES:(h + 1) * LANES] = dqp
            dcq = dcq + _dot_nt(dqp, wq_ref[h])
            dkvb = dkv_ref[h].astype(BF16)
            dkvb_ref[:, h * LANES:(h + 1) * LANES] = dkvb
            dckv = dckv + _dot_nt(dkvb, wkv_ref[h])
        dkrope = dkr_ref[0]
        for pr in range(1, HEADS // 2):
            dkrope = dkrope + dkr_ref[pr]
        cq = r_ref[:, REST_CQ:REST_CKV]
        _, rq = _rms(cq, gq_ref[...])
        d_cq, dgq = _rms_bwd(cq, gq_ref[...], rq, dcq)
        ckv = r_ref[:, REST_CKV:REST_KR]
        _, rkv = _rms(ckv, gkv_ref[...])
        d_ckv, dgkv = _rms_bwd(ckv, gkv_ref[...], rkv, dckv)
        dgq_ref[...] += dgq
        dgkv_ref[...] += dgkv
        dr_ref[:, 0:REST_CQ] = dz_ref[...].astype(BF16)
        dr_ref[:, REST_CQ:REST_CKV] = d_cq.astype(BF16)
        dr_ref[:, REST_CKV:REST_KR] = d_ckv.astype(BF16)
        dr_ref[:, REST_KR:REST_COLS] = _rope_bwd(dkrope, cos_, sin_).astype(BF16)

    row = lambda n: pl.BlockSpec((tm, n), lambda i: (i, 0))
    full = lambda a: pl.BlockSpec(a.shape, lambda i: (0,) * a.ndim)
    heads = pl.BlockSpec((HEADS, tm, LANES), lambda i: (0, i, 0))
    hshape = jax.ShapeDtypeStruct((t, HEADS * LANES), BF16)
    return pl.pallas_call(
        body, name="mla_prep_bwd", grid=(t // tm,),
        out_shape=(jax.ShapeDtypeStruct((t, REST_COLS), BF16), hshape, hshape,
                   jax.ShapeDtypeStruct((1, Q_RANK), F32), jax.ShapeDtypeStruct((1, KV_RANK), F32)),
        in_specs=[heads, heads, pl.BlockSpec((HEADS // 2, tm, LANES), lambda i: (0, i, 0)), row(LANES), row(REST_COLS),
                  full(gq), full(gkv), full(wq), full(wkv), row(LANES), row(LANES)],
        out_specs=(row(REST_COLS), row(HEADS * LANES), row(HEADS * LANES), pl.BlockSpec((1, Q_RANK), lambda i: (0, 0)),
                   pl.BlockSpec((1, KV_RANK), lambda i: (0, 0))),
        compiler_params=_cparams("arbitrary"),
    )(dq, dkv, dkr, dz, rest, gq, gkv, wq, wkv, cos, sin)


def _in_proj_bwd(x, g, dx1, dfq, dfk, dfv, drest, w_qkv, w_rest, comm=None):
    t = x.shape[0]
    tm = _row_tile(t)

    def main(ins, outs, scr):
        x_ref, g_ref, dx1_ref, dq_ref, dk_ref, dv_ref, dr_ref, wq_ref, wr_ref = ins
        dx_ref, dg_ref = outs

        @pl.when(pl.program_id(0) == 0)
        def _():
            dg_ref[...] = jnp.zeros_like(dg_ref)
        dh = _dot(dr_ref[...], wr_ref[...])
        for n, ref in enumerate((dq_ref, dk_ref, dv_ref)):
            dh = dh + _dot(ref[...], wq_ref[n * FOX_WIDTH:(n + 1) * FOX_WIDTH, :])
        xv = x_ref[...]
        _, r = _rms(xv, g_ref[...])
        dx, dg = _rms_bwd(xv, g_ref[...], r, dh)
        dx_ref[...] = dx1_ref[...] + dx
        dg_ref[...] += dg

    row = lambda n: pl.BlockSpec((tm, n), lambda i: (i, 0))
    full = lambda a: pl.BlockSpec(a.shape, lambda i: (0,) * a.ndim)
    vec = pl.BlockSpec((1, D_MODEL), lambda i: (0, 0))
    return _hosted_call(
        "in_proj_bwd", main, (t // tm,), [x, g, dx1, dfq, dfk, dfv, drest, w_qkv, w_rest],
        [row(D_MODEL), full(g), row(D_MODEL), row(FOX_WIDTH), row(FOX_WIDTH), row(FOX_WIDTH), row(REST_COLS),
         full(w_qkv), full(w_rest)],
        (jax.ShapeDtypeStruct((t, D_MODEL), F32), jax.ShapeDtypeStruct((1, D_MODEL), F32)), (row(D_MODEL), vec), [], comm)


def _pad_cols(a, n):
    return jnp.pad(a, ((0, 0),) * (a.ndim - 1) + ((0, n - a.shape[-1]),))


def kernel(x, positions, attn_norm_g, w_in, b_forget, q_norm_g, w_uq, kv_norm_g, w_ukv, fox_out_g, mla_out_g, w_o, mlp_norm_g, w_up, w_down, final_norm_g, loss_target, m_attn_norm_g, m_w_in, m_b_forget, m_q_norm_g, m_w_uq, m_kv_norm_g, m_w_ukv, m_fox_out_g, m_mla_out_g, m_w_o, m_mlp_norm_g, m_w_up, m_w_down, m_final_norm_g, v_attn_norm_g, v_w_in, v_b_forget, v_q_norm_g, v_w_uq, v_kv_norm_g, v_w_ukv, v_fox_out_g, v_mla_out_g, v_w_o, v_mlp_norm_g, v_w_up, v_w_down, v_final_norm_g):
    t = x.shape[1]
    tq = _row_tile(t)
    xs = x[0]
    target = loss_target[0]

    mid = [_pad_cols(w_uq[0], LANES).astype(BF16), w_ukv[0].astype(BF16)]
    late = [w_o[0].astype(BF16), w_up[0].astype(BF16), w_down[0].astype(BF16)]
    g_in, = _all_gather([jnp.transpose(w_in[0]).astype(BF16)])
    win = g_in.reshape(IN_COLS, D_MODEL)
    off_ff, off_cq, off_kr = 3 * FOX_WIDTH, 3 * FOX_WIDTH + HEADS, IN_COLS - ROPE
    zeros = lambda n: jnp.zeros((n, D_MODEL), BF16)
    w_qkv = win[:off_ff]
    w_rest = jnp.concatenate([
        win[off_ff:off_cq], zeros(REST_CQ - HEADS), win[off_cq:off_kr],
        zeros(NOPE), win[off_kr:], zeros(LANES - NOPE - ROPE)], axis=0)

    cos, sin = _rope_tables(positions.reshape(t, 1))
    (h1, fq, fk, fv, rest), (wq, wkv) = _in_proj(xs, attn_norm_g, w_qkv, w_rest, comm=_ag_to_all(mid))
    b128 = _pad_cols(b_forget, LANES)
    f2_rows, f2_rep = _forget_cumsum(rest, b128)
    f2_rows = f2_rows.reshape(HEADS, t // tq, 1, tq)
    (fox_o, fox_lse_rows), partly = _attn_fwd(True, fq, fk, fv, f2_rows, comm=_ag_direct(late))
    mq, mk, mkv, cqn, ckvn = _mla_prep(rest, q_norm_g, kv_norm_g, wq, wkv, cos, sin)
    (mla_o, mla_lse_rows), (g_o, g_up, g_down) = _attn_fwd(False, mq, mk, mkv, comm=_ag_forward(partly))
    wo = g_o.reshape(D_MODEL, D_MODEL)
    x1, mixed = _attn_out(xs, fox_o, mla_o, fox_out_g, mla_out_g, wo)
    wup = jnp.transpose(g_up, (1, 0, 2)).reshape(D_MODEL, D_FF)
    wdown = g_down.reshape(D_FF, D_MODEL)
    u, h2, dx2, dx2b, loss8, d_gfin = _mlp_fwd(x1, mlp_norm_g, wup, wdown, final_norm_g.reshape(1, D_MODEL), target)

    du, act, dx1, dx1b, d_gmlp = _mlp_bwd(dx2, u, x1, mlp_norm_g, wup, wdown)
    dw_down = _matmul_tn("dw_down", act, dx2b)
    dw_up = _matmul_tn("dw_up", h2, du, blocks=N_DEV)
    dfox_o, dmla_o, fox_delta_rows, mla_delta_rows, d_gfox, d_gmla = _attn_out_bwd(dx1, fox_o, mla_o, fox_out_g, mla_out_g, wo)
    dw_o = _matmul_tn("dw_o", mixed, dx1b)

    place = jnp.stack([lax.axis_index("c"), 2 * lax.axis_index("x") + lax.axis_index("y")]).astype(jnp.int32)
    names = ("w_in", "w_uq", "w_ukv", "w_o", "w_up", "w_down")
    grads_b = [dw_o.reshape(N_DEV, -1, D_MODEL), dw_up, dw_down.reshape(N_DEV, -1, D_MODEL)]
    (dfq, dfk, dfv, d_fq, d_fk), got_b = _attn_bwd(True, fq, fk, fv, dfox_o, fox_lse_rows, fox_delta_rows,
                                                   f2_rep, comm=_rs_to_sibling(grads_b))
    sums_b = [_rs_sibling_sum("rs_sibling_sum_" + nm, g, l, place) for nm, g, l in zip(names[3:], grads_b, got_b)]
    dz, d_b = _forget_bwd(rest, b128, d_fq, d_fk)
    (dmq, dmkv, dmkr), others_b = _attn_bwd(False, mq, mk, mkv, dmla_o, mla_lse_rows, mla_delta_rows,
                                            comm=_rs_to_chips([s[1] for s in sums_b]))
    drest, dqp, dkvb, d_gq, d_gkv = _mla_prep_bwd(dmq, dmkv, dmkr, dz, rest, q_norm_g, kv_norm_g, wq, wkv, cos, sin)
    dw_uq = _matmul_tn("dw_uq", cqn, dqp, blocks=HEADS)
    dw_ukv = _matmul_tn("dw_ukv", ckvn, dkvb, blocks=HEADS)
    dw_in = _dw_in(dfq, dfk, dfv, drest, h1)

    grads_a = [dw_in.reshape(N_DEV, IN_SHARD, D_MODEL), dw_uq, dw_ukv]
    got_a = _comm_call("rs_sibling_exchange", _rs_to_sibling(grads_a))
    sums_a = [_rs_sibling_sum("rs_sibling_sum_" + nm, g, l, place) for nm, g, l in zip(names[:3], grads_a, got_a)]
    (grad_x, d_gattn), others_a = _in_proj_bwd(xs, attn_norm_g, dx1, dfq, dfk, dfv, drest, w_qkv, w_rest,
                                               comm=_rs_to_chips([s[1] for s in sums_a]))
    sums, others = sums_a + sums_b, list(others_a) + list(others_b)
    sharded = (w_in, w_uq, w_ukv, w_o, w_up, w_down)
    moments_m = (m_w_in, m_w_uq, m_w_ukv, m_w_o, m_w_up, m_w_down)
    moments_v = (v_w_in, v_w_uq, v_w_ukv, v_w_o, v_w_up, v_w_down)
    g_in_t = _rs_final_sum("rs_final_sum_w_in", sums[0][0], others[0])
    big = [_adamw_given("adamw_w_in", jnp.transpose(g_in_t), w_in, m_w_in, v_w_in)]
    for a in range(1, len(names)):
        big.append(_adamw_sharded("adamw_" + names[a], sharded[a], moments_m[a], moments_v[a], sums[a][0], others[a]))
    big_g, big_d, big_m, big_v = [[b[k] for b in big] for k in range(4)]

    as_row = lambda a: a.reshape(1, -1)
    small_w = (attn_norm_g, b_forget, q_norm_g, kv_norm_g, fox_out_g, mla_out_g, mlp_norm_g, final_norm_g)
    small_m = (m_attn_norm_g, m_b_forget, m_q_norm_g, m_kv_norm_g, m_fox_out_g, m_mla_out_g, m_mlp_norm_g, m_final_norm_g)
    small_v = (v_attn_norm_g, v_b_forget, v_q_norm_g, v_kv_norm_g, v_fox_out_g, v_mla_out_g, v_mlp_norm_g, v_final_norm_g)
    total = _small_all_reduce([d_gattn, d_b, d_gq, d_gkv, d_gfox, d_gmla, d_gmlp, d_gfin], loss8)
    small = _adamw_small(total, [as_row(a) for a in small_w], [as_row(a) for a in small_m], [as_row(a) for a in small_v])
    loss = small[0].reshape(())
    s_g, s_d, s_m, s_v = [[small[1 + 4 * r + k].reshape(small_w[r].shape) for r in range(len(small_w))] for k in range(4)]

    def ordered(small_, bigs):
        ga, bf, gq_, gkv_, gfo, gml, gmlp_, gfin_ = small_
        bin_, buq, bukv, bo, bup, bdown = bigs
        return [ga, bin_, bf, gq_, buq, gkv_, bukv, gfo, gml, bo, gmlp_, bup, bdown, gfin_]

    return (loss, grad_x[None], *ordered(s_g, big_g), *ordered(s_d, big_d), *ordered(s_m, big_m), *ordered(s_v, big_v))
```

```python
import math
from typing import Callable, NamedTuple

import numpy as np
import jax
import jax.numpy as jnp
from jax import lax
from jax.experimental import pallas as pl
from jax.experimental.pallas import tpu as pltpu

F32 = jnp.float32
BF16 = jnp.bfloat16
MESH = pl.DeviceIdType.MESH

D_MODEL = 1024
HEADS = 8
HEAD_DIM = 64
FOX_WIDTH = 512
MLA_WIDTH = 512
NOPE = 64
ROPE = 32
QK_DIM = 96
Q_RANK = 384
KV_RANK = 256
D_FF = 4096
IN_COLS = 2216
ROPE_THETA = 10000.0
EPS = 1e-6
FOX_SCALE = 1.0 / math.sqrt(HEAD_DIM)
MLA_SCALE = 1.0 / math.sqrt(QK_DIM)
ADAM_LR = 0.001
ADAM_B1 = 0.9
ADAM_B2 = 0.999
ADAM_EPS = 1e-08
ADAM_WD = 0.01
ADAM_STEP = 10

N_DEV = 8
LANES = 128
REST_COLS = 896
REST_CQ = LANES
REST_CKV = REST_CQ + Q_RANK
REST_KR = REST_CKV + KV_RANK
LOG2E = 1.4426950408889634
LN2 = 0.6931471805599453
FOX_Q_FACTOR = FOX_SCALE * LOG2E
MLA_Q_FACTOR = MLA_SCALE * LOG2E
VMEM_LIMIT = 56 * 1024 * 1024

IN_SHARD = IN_COLS // N_DEV
SMALL_SIZES = (1024, 8, 384, 256, 512, 512, 1024, 1024)
SMALL_ROWS = 16
LOSS_ROW = len(SMALL_SIZES)


def _cparams(*sem):
    return pltpu.CompilerParams(dimension_semantics=sem or None, vmem_limit_bytes=VMEM_LIMIT)


def _row_tile(t):
    return 512 if t >= 2048 else (256 if t >= 512 else 128)


def _dot(a, b):
    return jnp.dot(a, b, preferred_element_type=F32)


def _dot_nt(a, b):
    return lax.dot_general(a, b, (((1,), (1,)), ((), ())), preferred_element_type=F32)


def _dot_tn(a, b):
    return lax.dot_general(a, b, (((0,), (0,)), ((), ())), preferred_element_type=F32)


def _rms(x, g):
    r = lax.rsqrt(jnp.mean(x * x, axis=-1, keepdims=True) + EPS)
    return x * r * g, r


def _rms_bwd(x, g, r, dy):
    xh = x * r
    gdy = dy * g
    dx = r * (gdy - xh * jnp.mean(gdy * xh, axis=-1, keepdims=True))
    return dx, jnp.sum(dy * xh, axis=0, keepdims=True)


def _lane():
    return lax.broadcasted_iota(jnp.int32, (1, LANES), 1)


def _rot(x):
    lane = _lane()
    half = NOPE + ROPE // 2
    first = jnp.logical_and(lane >= NOPE, lane < half)
    second = jnp.logical_and(lane >= half, lane < NOPE + ROPE)
    return jnp.where(first, -pltpu.roll(x, LANES - ROPE // 2, 1), jnp.where(second, pltpu.roll(x, ROPE // 2, 1), 0.0))


def _rope(x, cos, sin):
    return x * cos + _rot(x) * sin


def _rope_bwd(dy, cos, sin):
    return dy * cos - _rot(dy * sin)


def _remote(src, dst, send_sem, recv_sem, to):
    return pltpu.make_async_remote_copy(src_ref=src, dst_ref=dst, send_sem=send_sem, recv_sem=recv_sem,
                                        device_id=to, device_id_type=MESH)


def _hbm_specs(n):
    return [pl.BlockSpec(memory_space=pl.ANY)] * n


def _all_gather(blocks):
    n = len(blocks)

    def body(*refs):
        x_refs, out_refs = refs[:n], refs[n:2 * n]
        send_sems, recv_sems, local_sems = refs[2 * n:]
        x, y, c = lax.axis_index("x"), lax.axis_index("y"), lax.axis_index("c")
        me, sibling = (x, y, c), (x, y, 1 - c)
        chips = [(1 - x, y), (x, 1 - y), (1 - x, 1 - y)]

        def slot(a, px, py, pc):
            return out_refs[a].at[4 * px + 2 * py + pc]

        def copy(a, k, blk, to, src=None):
            return _remote(slot(a, *blk) if src is None else src, slot(a, *blk),
                           send_sems.at[7 * a + k], recv_sems.at[7 * a + k], to)

        mine = [pltpu.make_async_copy(x_refs[a], slot(a, *me), local_sems.at[a]) for a in range(n)]
        first, passed = [], []
        for a in range(n):
            mine[a].start()
            first.append(copy(a, 0, me, sibling, src=x_refs[a]))
            first += [copy(a, 1 + j, me, (*chip, c), src=x_refs[a]) for j, chip in enumerate(chips)]
        for cp in first:
            cp.start()
        for a in range(n):
            for j, chip in enumerate(chips):
                copy(a, 1 + j, (*chip, c), me).wait_recv()
                passed.append(copy(a, 4 + j, (*chip, c), sibling))
                passed[-1].start()
        for a in range(n):
            copy(a, 0, sibling, me).wait_recv()
            for j, chip in enumerate(chips):
                copy(a, 4 + j, (*chip, 1 - c), me).wait_recv()
        for cp in first + passed:
            cp.wait_send()
        for cp in mine:
            cp.wait()

    return pl.pallas_call(
        body, name="all_gather_weights",
        out_shape=[jax.ShapeDtypeStruct((N_DEV,) + b.shape, b.dtype) for b in blocks],
        in_specs=_hbm_specs(n), out_specs=_hbm_specs(n),
        scratch_shapes=[pltpu.SemaphoreType.DMA((7 * n,)), pltpu.SemaphoreType.DMA((7 * n,)), pltpu.SemaphoreType.DMA((n,))],
    )(*blocks)


def _symmetric_comm(inputs, out_shape, aliases, per_array, copies):
    def start(in_refs, out_refs, sems):
        for cp in copies(in_refs, out_refs, *sems):
            cp.start()

    def finish(in_refs, out_refs, sems):
        for cp in copies(in_refs, out_refs, *sems):
            cp.wait()

    n_sems = per_array * len(inputs)
    return _Comm(tuple(inputs), tuple(out_shape), aliases,
                 (pltpu.SemaphoreType.DMA((n_sems,)), pltpu.SemaphoreType.DMA((n_sems,))), start, finish)


def _ag_direct(shards):
    def copies(in_refs, out_refs, send_sems, recv_sems):
        x, y, c = lax.axis_index("x"), lax.axis_index("y"), lax.axis_index("c")
        peers = [(x, y, 1 - c), (1 - x, y, c), (x, 1 - y, c), (1 - x, 1 - y, c)]
        cps = []
        for a in range(len(shards)):
            mine = out_refs[a].at[4 * x + 2 * y + c]
            cps.append(pltpu.make_async_copy(in_refs[a], mine, send_sems.at[5 * a]))
            cps += [_remote(in_refs[a], mine, send_sems.at[5 * a + k], recv_sems.at[5 * a + k], peer)
                    for k, peer in enumerate(peers, start=1)]
        return cps

    return _symmetric_comm(shards, [jax.ShapeDtypeStruct((N_DEV,) + s.shape, s.dtype) for s in shards], {}, 5, copies)


def _ag_to_all(shards):
    def copies(in_refs, out_refs, send_sems, recv_sems):
        x, y, c = lax.axis_index("x"), lax.axis_index("y"), lax.axis_index("c")
        cps = []
        for a in range(len(shards)):
            mine = out_refs[a].at[4 * x + 2 * y + c]
            cps.append(pltpu.make_async_copy(in_refs[a], mine, send_sems.at[N_DEV * a]))
            for k in range(1, N_DEV):
                peer = (x ^ (k >> 2), y ^ ((k >> 1) & 1), c ^ (k & 1))
                cps.append(_remote(in_refs[a], mine, send_sems.at[N_DEV * a + k], recv_sems.at[N_DEV * a + k], peer))
        return cps

    return _symmetric_comm(shards, [jax.ShapeDtypeStruct((N_DEV,) + s.shape, s.dtype) for s in shards], {}, N_DEV, copies)


def _ag_forward(gathered):
    def copies(in_refs, out_refs, send_sems, recv_sems):
        x, y, c = lax.axis_index("x"), lax.axis_index("y"), lax.axis_index("c")
        chips = [(1 - x, y), (x, 1 - y), (1 - x, 1 - y)]
        return [_remote(out_refs[a].at[4 * cx + 2 * cy + c], out_refs[a].at[4 * cx + 2 * cy + c],
                        send_sems.at[3 * a + j], recv_sems.at[3 * a + j], (x, y, 1 - c))
                for a in range(len(gathered)) for j, (cx, cy) in enumerate(chips)]

    shapes = [jax.ShapeDtypeStruct(g.shape, g.dtype) for g in gathered]
    return _symmetric_comm(gathered, shapes, {a: a for a in range(len(gathered))}, 3, copies)


def _rs_to_sibling(grads):
    def copies(in_refs, out_refs, send_sems, recv_sems):
        x, y, c = lax.axis_index("x"), lax.axis_index("y"), lax.axis_index("c")
        return [_remote(in_refs[a].at[2 * q + 1 - c], out_refs[a].at[q], send_sems.at[4 * a + q], recv_sems.at[4 * a + q], (x, y, 1 - c))
                for a in range(len(grads)) for q in range(4)]

    return _symmetric_comm(grads, [jax.ShapeDtypeStruct((4,) + g.shape[1:], g.dtype) for g in grads], {}, 4, copies)


def _rs_to_chips(parts):
    def copies(in_refs, out_refs, send_sems, recv_sems):
        x, y, c = lax.axis_index("x"), lax.axis_index("y"), lax.axis_index("c")
        chips = [(1 - x, y), (x, 1 - y), (1 - x, 1 - y)]
        return [_remote(in_refs[a].at[2 * cx + cy], out_refs[a].at[k], send_sems.at[3 * a + k], recv_sems.at[3 * a + k], (cx, cy, c))
                for a in range(len(parts)) for k, (cx, cy) in enumerate(chips)]

    return _symmetric_comm(parts, [jax.ShapeDtypeStruct((3,) + p.shape[1:], p.dtype) for p in parts], {}, 3, copies)


def _comm_call(name, comm):
    n_in, n_out = len(comm.inputs), len(comm.out_shape)

    def body(*refs):
        ins, outs, sems = refs[:n_in], refs[n_in:n_in + n_out], refs[n_in + n_out:]
        comm.start(ins, outs, sems)
        comm.finish(ins, outs, sems)

    return pl.pallas_call(
        body, name=name, out_shape=list(comm.out_shape), in_specs=_hbm_specs(n_in), out_specs=_hbm_specs(n_out),
        scratch_shapes=list(comm.scratch), input_output_aliases=dict(comm.aliases),
    )(*comm.inputs)


def _small_all_reduce(parts, loss8):
    n = len(parts)

    def body(*refs):
        p_refs, loss_ref, out_ref, pack, land, send_sems, recv_sems = refs[:n], *refs[n:]
        x, y, c = lax.axis_index("x"), lax.axis_index("y"), lax.axis_index("c")
        me = 4 * x + 2 * y + c
        pack[...] = jnp.zeros_like(pack)
        for r, ref in enumerate(p_refs):
            pack[r:r + 1, 0:ref.shape[1]] = ref[...]
        pack[LOSS_ROW:LOSS_ROW + 1, 0:LANES] = loss_ref[0:1, :]
        land[me] = pack[...]
        cps = []
        for k in range(1, N_DEV):
            peer = (x ^ (k >> 2), y ^ ((k >> 1) & 1), c ^ (k & 1))
            cps.append(_remote(pack, land.at[me], send_sems.at[k - 1], recv_sems.at[k - 1], peer))
        for cp in cps:
            cp.start()
        for cp in cps:
            cp.wait()
        acc = land[0]
        for d in range(1, N_DEV):
            acc = acc + land[d]
        out_ref[...] = acc

    vmem = pl.BlockSpec(memory_space=pltpu.VMEM)
    return pl.pallas_call(
        body, name="small_all_reduce",
        out_shape=jax.ShapeDtypeStruct((SMALL_ROWS, D_MODEL), F32),
        in_specs=[vmem] * (n + 1), out_specs=vmem,
        scratch_shapes=[pltpu.VMEM((SMALL_ROWS, D_MODEL), F32), pltpu.VMEM((N_DEV, SMALL_ROWS, D_MODEL), F32),
                        pltpu.SemaphoreType.DMA((N_DEV - 1,)), pltpu.SemaphoreType.DMA((N_DEV - 1,))],
    )(*parts, loss8)


def _rs_sibling_sum(name, grad, got, place):
    _, rows, cols = grad.shape

    def body(place_ref, g_ref, l_ref, own_ref, b_ref):
        s = g_ref[...] + l_ref[...]
        b_ref[...] = s.astype(BF16)

        @pl.when(pl.program_id(0) == place_ref[1])
        def _():
            own_ref[...] = s

    by_chip = pl.BlockSpec((None, rows, cols), lambda q, place_ref: (q, 0, 0))
    return pl.pallas_call(
        body, name=name,
        grid_spec=pltpu.PrefetchScalarGridSpec(
            num_scalar_prefetch=1, grid=(4,),
            in_specs=[pl.BlockSpec((None, rows, cols), lambda q, place_ref: (2 * q + place_ref[0], 0, 0)), by_chip],
            out_specs=[pl.BlockSpec((rows, cols), lambda q, place_ref: (0, 0)), by_chip]),
        out_shape=(jax.ShapeDtypeStruct((rows, cols), F32), jax.ShapeDtypeStruct((4, rows, cols), BF16)),
        compiler_params=_cparams("arbitrary"),
    )(place, grad, got)


def _adamw_math(w, g, m, v):
    m2 = ADAM_B1 * m + (1.0 - ADAM_B1) * g
    v2 = ADAM_B2 * v + (1.0 - ADAM_B2) * (g * g)
    m_hat = m2 / (1.0 - ADAM_B1 ** ADAM_STEP)
    v_hat = v2 / (1.0 - ADAM_B2 ** ADAM_STEP)
    delta = -ADAM_LR * (m_hat / (jnp.sqrt(v_hat) + ADAM_EPS) + ADAM_WD * w)
    return delta, m2, v2


def _update_tile(rows):
    return 256 if rows % 256 == 0 else rows


def _rs_final_sum(name, own, got):
    def body(o_ref, r_ref, g_out):
        g = o_ref[...]
        for k in range(3):
            g = g + r_ref[k].astype(F32)
        g_out[...] = g

    return pl.pallas_call(body, name=name, out_shape=jax.ShapeDtypeStruct(own.shape, F32))(own, got)


def _adamw_sharded(name, w, m, v, own, got):
    _, rows, cols = w.shape
    tr = _update_tile(rows)

    def body(o_ref, r_ref, w_ref, m_ref, v_ref, g_out, d_out, m_out, v_out):
        g = o_ref[:, 0:cols]
        for k in range(3):
            g = g + r_ref[k, :, 0:cols].astype(F32)
        d, m2, v2 = _adamw_math(w_ref[0], g, m_ref[0], v_ref[0])
        g_out[0] = g
        d_out[0] = d
        m_out[0] = m2
        v_out[0] = v2

    mine = pl.BlockSpec((1, tr, cols), lambda i: (0, i, 0))
    shp = jax.ShapeDtypeStruct(w.shape, F32)
    wide = own.shape[1]
    return pl.pallas_call(
        body, name=name, grid=(rows // tr,), out_shape=(shp,) * 4,
        in_specs=[pl.BlockSpec((tr, wide), lambda i: (i, 0)), pl.BlockSpec((3, tr, wide), lambda i: (0, i, 0)),
                  mine, mine, mine],
        out_specs=[mine] * 4,
        compiler_params=_cparams("parallel"),
    )(own, got, w, m, v)


def _adamw_given(name, g, w, m, v):
    _, rows, cols = w.shape
    tr = _update_tile(rows)

    def body(g_ref, w_ref, m_ref, v_ref, g_out, d_out, m_out, v_out):
        g = g_ref[...]
        d, m2, v2 = _adamw_math(w_ref[0], g, m_ref[0], v_ref[0])
        g_out[0] = g
        d_out[0] = d
        m_out[0] = m2
        v_out[0] = v2

    own = pl.BlockSpec((1, tr, cols), lambda i: (0, i, 0))
    shp = jax.ShapeDtypeStruct(w.shape, F32)
    return pl.pallas_call(
        body, name=name, grid=(rows // tr,), out_shape=(shp,) * 4,
        in_specs=[pl.BlockSpec((tr, cols), lambda i: (i, 0)), own, own, own], out_specs=[own] * 4,
        compiler_params=_cparams("parallel"),
    )(g, w, m, v)


def _adamw_small(total, ws, ms, vs):
    n = len(ws)

    def body(*refs):
        t_ref = refs[0]
        w_refs, m_refs, v_refs = refs[1:1 + n], refs[1 + n:1 + 2 * n], refs[1 + 2 * n:1 + 3 * n]
        outs = refs[1 + 3 * n:]
        outs[0][...] = t_ref[LOSS_ROW:LOSS_ROW + 1, 0:1]
        for r in range(n):
            g = t_ref[r:r + 1, 0:w_refs[r].shape[1]]
            d, m2, v2 = _adamw_math(w_refs[r][...], g, m_refs[r][...], v_refs[r][...])
            for k, val in enumerate((g, d, m2, v2)):
                outs[1 + 4 * r + k][...] = val

    vmem = pl.BlockSpec(memory_space=pltpu.VMEM)
    out_shape = [jax.ShapeDtypeStruct((1, 1), F32)]
    for w in ws:
        out_shape += [jax.ShapeDtypeStruct(w.shape, F32)] * 4
    return pl.pallas_call(
        body, name="adamw_small", out_shape=out_shape,
        in_specs=[vmem] * (1 + 3 * n), out_specs=[vmem] * len(out_shape),
    )(total, *ws, *ms, *vs)


def _rope_tables(pos_col):
    t = pos_col.shape[0]
    inv = (np.float32(ROPE_THETA) ** (-np.arange(0, ROPE, 2, dtype=np.float32) / np.float32(ROPE))).astype(np.float32)
    freq = np.zeros((1, LANES), np.float32)
    freq[0, NOPE:NOPE + ROPE // 2] = inv
    freq[0, NOPE + ROPE // 2:NOPE + ROPE] = inv
    tm = _row_tile(t)

    def body(p_ref, f_ref, c_ref, s_ref):
        ang = p_ref[...].astype(F32) * f_ref[...]
        c_ref[...] = jnp.cos(ang)
        s_ref[...] = jnp.sin(ang)

    shp = jax.ShapeDtypeStruct((t, LANES), F32)
    return pl.pallas_call(
        body, name="rope_tables", grid=(t // tm,), out_shape=(shp, shp),
        in_specs=[pl.BlockSpec((tm, 1), lambda i: (i, 0)), pl.BlockSpec((1, LANES), lambda i: (0, 0))],
        out_specs=(pl.BlockSpec((tm, LANES), lambda i: (i, 0)),) * 2,
        compiler_params=_cparams("parallel"),
    )(pos_col, jnp.asarray(freq))


def _in_proj(x, g, w_qkv, w_rest, comm=None):
    t = x.shape[0]
    tm = _row_tile(t)

    def main(ins, outs, scr):
        x_ref, g_ref, wq_ref, wr_ref = ins
        h_ref, fq_ref, fk_ref, fv_ref, r_ref = outs
        h, _ = _rms(x_ref[...], g_ref[...])
        hb = h.astype(BF16)
        h_ref[...] = hb
        for n, (ref, factor) in enumerate(((fq_ref, FOX_Q_FACTOR), (fk_ref, None), (fv_ref, None))):
            part = _dot_nt(hb, wq_ref[n * FOX_WIDTH:(n + 1) * FOX_WIDTH, :])
            ref[...] = (part if factor is None else part * factor).astype(BF16)
        r_ref[...] = _dot_nt(hb, wr_ref[...])

    row = lambda n: pl.BlockSpec((tm, n), lambda i: (i, 0))
    full = lambda a: pl.BlockSpec(a.shape, lambda i: (0,) * a.ndim)
    return _hosted_call(
        "in_proj", main, (t // tm,), [x, g, w_qkv, w_rest], [row(D_MODEL), full(g), full(w_qkv), full(w_rest)],
        (jax.ShapeDtypeStruct((t, D_MODEL), BF16),) + (jax.ShapeDtypeStruct((t, FOX_WIDTH), BF16),) * 3
        + (jax.ShapeDtypeStruct((t, REST_COLS), F32),),
        (row(D_MODEL), row(FOX_WIDTH), row(FOX_WIDTH), row(FOX_WIDTH), row(REST_COLS)), [], comm)


def _log_sigmoid(z):
    return jnp.minimum(z, 0.0) - jnp.log(1.0 + jnp.exp(-jnp.abs(z)))


def _split3(v):
    hi = v.astype(BF16)
    r1 = v - hi.astype(F32)
    mid = r1.astype(BF16)
    lo = (r1 - mid.astype(F32)).astype(BF16)
    return hi, mid, lo


def _scan_tile(t):
    return 512 if t >= 2048 else (256 if t >= 256 else t)


def _forget_cumsum(rest, b128):
    t = rest.shape[0]
    tb = _scan_tile(t)

    def body(r_ref, b_ref, row_ref, rep_ref, f_sc, carry):
        @pl.when(pl.program_id(0) == 0)
        def _():
            carry[...] = jnp.zeros_like(carry)
        lf = _log_sigmoid(r_ref[...] + b_ref[...])
        tri = (lax.broadcasted_iota(jnp.int32, (tb, tb), 0) >= lax.broadcasted_iota(jnp.int32, (tb, tb), 1)).astype(BF16)
        hi, mid, lo = _split3(lf)
        f_sc[...] = (_dot(tri, hi) + _dot(tri, mid)) + _dot(tri, lo) + carry[...]
        carry[...] = f_sc[tb - 1:tb, :]
        f2 = f_sc[...] * LOG2E
        row_ref[...] = jnp.transpose(f2)[0:HEADS, :]
        lane = _lane()
        for h in range(HEADS):
            col = jnp.sum(jnp.where(lane == h, f2, 0.0), axis=1, keepdims=True)
            rep_ref[h] = jnp.broadcast_to(col, (tb, LANES))

    return pl.pallas_call(
        body, name="forget_cumsum", grid=(t // tb,),
        out_shape=(jax.ShapeDtypeStruct((HEADS, t), F32), jax.ShapeDtypeStruct((HEADS, t, LANES), F32)),
        in_specs=[pl.BlockSpec((tb, LANES), lambda i: (i, 0)), pl.BlockSpec((1, LANES), lambda i: (0, 0))],
        out_specs=(pl.BlockSpec((HEADS, tb), lambda i: (0, i)), pl.BlockSpec((HEADS, tb, LANES), lambda i: (0, i, 0))),
        scratch_shapes=[pltpu.VMEM((tb, LANES), F32), pltpu.VMEM((1, LANES), F32)],
        compiler_params=_cparams("arbitrary"),
    )(rest, b128)


def _forget_bwd(rest, b128, d_fq, d_fk):
    t = rest.shape[0]
    tb = _scan_tile(t)
    nb = t // tb

    def body(r_ref, b_ref, dfq_ref, dfk_ref, dz_ref, db_ref, carry):
        @pl.when(pl.program_id(0) == 0)
        def _():
            carry[...] = jnp.zeros_like(carry)
            db_ref[...] = jnp.zeros_like(db_ref)
        tri = (lax.broadcasted_iota(jnp.int32, (tb, tb), 0) <= lax.broadcasted_iota(jnp.int32, (tb, tb), 1)).astype(BF16)
        lane = _lane()
        df = jnp.zeros((tb, LANES), F32)
        for h in range(HEADS):
            df = df + jnp.where(lane == h, dfq_ref[h] + dfk_ref[h], 0.0)
        hi, mid, lo = _split3(df)
        dlf = (_dot(tri, hi) + _dot(tri, mid)) + _dot(tri, lo) + carry[...]
        z = r_ref[...] + b_ref[...]
        dz = dlf / (1.0 + jnp.exp(z))
        dz_ref[...] = dz
        db_ref[...] += jnp.sum(dz, axis=0, keepdims=True)
        carry[...] = carry[...] + jnp.sum(df, axis=0, keepdims=True)

    rev = lambda i: (nb - 1 - i, 0)
    rev3 = pl.BlockSpec((HEADS, tb, LANES), lambda i: (0, nb - 1 - i, 0))
    return pl.pallas_call(
        body, name="forget_bwd", grid=(nb,),
        out_shape=(jax.ShapeDtypeStruct((t, LANES), F32), jax.ShapeDtypeStruct((1, LANES), F32)),
        in_specs=[pl.BlockSpec((tb, LANES), rev), pl.BlockSpec((1, LANES), lambda i: (0, 0)), rev3, rev3],
        out_specs=(pl.BlockSpec((tb, LANES), rev), pl.BlockSpec((1, LANES), lambda i: (0, 0))),
        scratch_shapes=[pltpu.VMEM((1, LANES), F32)],
        compiler_params=_cparams("arbitrary"),
    )(rest, b128, d_fq, d_fk)


def _mla_prep(rest, gq, gkv, wq, wkv, cos, sin):
    t = rest.shape[0]
    tm = _row_tile(t)

    def body(r_ref, gq_ref, gkv_ref, wq_ref, wkv_ref, c_ref, s_ref, q_ref, k_ref, kv_ref, cq_ref, ckv_ref):
        cos_, sin_ = c_ref[...], s_ref[...]
        cq, _ = _rms(r_ref[:, REST_CQ:REST_CKV], gq_ref[...])
        ckv, _ = _rms(r_ref[:, REST_CKV:REST_KR], gkv_ref[...])
        cqb, ckvb = cq.astype(BF16), ckv.astype(BF16)
        cq_ref[...] = cqb
        ckv_ref[...] = ckvb
        k_rope = _rope(r_ref[:, REST_KR:REST_COLS], cos_, sin_)
        lo = _lane() < NOPE
        for h in range(HEADS):
            q_ref[h] = (_rope(_dot(cqb, wq_ref[h]), cos_, sin_) * MLA_Q_FACTOR).astype(BF16)
            kv = _dot(ckvb, wkv_ref[h])
            kv_ref[h] = kv.astype(BF16)
            k_ref[h] = (jnp.where(lo, kv, 0.0) + k_rope).astype(BF16)

    row = lambda n: pl.BlockSpec((tm, n), lambda i: (i, 0))
    full = lambda a: pl.BlockSpec(a.shape, lambda i: (0,) * a.ndim)
    heads = pl.BlockSpec((HEADS, tm, LANES), lambda i: (0, i, 0))
    hshape = jax.ShapeDtypeStruct((HEADS, t, LANES), BF16)
    return pl.pallas_call(
        body, name="mla_prep", grid=(t // tm,),
        out_shape=(hshape, hshape, hshape, jax.ShapeDtypeStruct((t, Q_RANK), BF16), jax.ShapeDtypeStruct((t, KV_RANK), BF16)),
        in_specs=[row(REST_COLS), full(gq), full(gkv), full(wq), full(wkv), row(LANES), row(LANES)],
        out_specs=(heads, heads, heads, row(Q_RANK), row(KV_RANK)),
        compiler_params=_cparams("parallel"),
    )(rest, gq, gkv, wq, wkv, cos, sin)


def _tile_lanes(x, n):
    return jnp.tile(x, (1, n)) if n > 1 else x


class _Comm(NamedTuple):
    inputs: tuple
    out_shape: tuple
    aliases: dict
    scratch: tuple
    start: Callable
    finish: Callable


def _hosted_call(name, main, grid, args, in_specs, out_shape, out_specs, scratch, comm):
    n_in, n_out, n_scr = len(args), len(out_shape), len(scratch)
    c_in = list(comm.inputs) if comm else []
    c_out = list(comm.out_shape) if comm else []

    def at_step(which):
        hit = pl.program_id(0) == which[0]
        for axis in range(1, len(grid)):
            hit = jnp.logical_and(hit, pl.program_id(axis) == which[axis])
        return hit

    def body(*refs):
        bounds = [0, n_in, len(c_in), n_out, len(c_out), n_scr]
        starts = [sum(bounds[:k + 1]) for k in range(len(bounds))]
        ins, cins, outs, couts, scr = [refs[a:b] for a, b in zip(starts[:-1], starts[1:])]
        sems = refs[starts[-1]:]
        if comm:
            @pl.when(at_step([0] * len(grid)))
            def _():
                comm.start(cins, couts, sems)
        main(ins, outs, scr)
        if comm:
            @pl.when(at_step([n - 1 for n in grid]))
            def _():
                comm.finish(cins, couts, sems)

    res = pl.pallas_call(
        body, name=name, grid=grid,
        out_shape=list(out_shape) + c_out,
        in_specs=list(in_specs) + _hbm_specs(len(c_in)),
        out_specs=list(out_specs) + _hbm_specs(len(c_out)),
        scratch_shapes=list(scratch) + (list(comm.scratch) if comm else []),
        input_output_aliases={n_in + i: n_out + o for i, o in comm.aliases.items()} if comm else {},
        compiler_params=_cparams(*(["arbitrary"] * len(grid))),
    )(*args, *c_in)
    return res[:n_out], res[n_out:]


def _stat_rows(x):
    return jnp.transpose(x)[0:8, :]


FWD_HEADS = 4


def _attn_fwd(fox, q, k, v, f2_rows=None, comm=None):
    t = q.shape[0] if fox else q.shape[1]
    tq = _row_tile(t)
    nq = t // tq
    nh = FWD_HEADS
    wide = (nh // 2) * LANES

    def main(ins, outs, scr):
        q_ref, k_ref, v_ref = ins[:3]
        fr_ref = ins[3] if fox else None
        o_ref, lset_ref = outs
        m_sc, acc_sc = scr
        i = pl.program_id(1)
        lo = _lane() < HEAD_DIM
        hi = jnp.logical_not(lo)
        zero, one = jnp.zeros((), BF16), jnp.ones((), BF16)
        lanes_of = lambda h: slice((h // 2) * LANES, (h // 2 + 1) * LANES)
        if fox:
            qs = [jnp.where(lo if h % 2 == 0 else hi, q_ref[:, lanes_of(h)], zero) for h in range(nh)]
            sum_lanes = [hi if h % 2 == 0 else lo for h in range(nh)]
        else:
            qs = [q_ref[h] for h in range(nh)]
            sum_lanes = [lo] * nh
        m_sc[...] = jnp.full_like(m_sc, -jnp.inf)
        acc_sc[...] = jnp.zeros_like(acc_sc)

        def block(j, r0, nr, c0, nc, seen_from):
            rows = slice(r0, r0 + nr)
            sl = pl.ds(pl.multiple_of(j * tq + c0, math.gcd(tq, c0) if c0 else tq), nc)
            if seen_from is not None:
                seen = (lax.broadcasted_iota(jnp.int32, (nr, nc), 1)
                        <= lax.broadcasted_iota(jnp.int32, (nr, nc), 0) + seen_from)
            for h in range(nh):
                kj, vj = (k_ref[sl, lanes_of(h)], v_ref[sl, lanes_of(h)]) if fox else (k_ref[h, sl, :], v_ref[h, sl, :])
                s = _dot_nt(qs[h][rows], kj)
                if fox and nc > tq:
                    s = s - jnp.concatenate([fr_ref[h, j + b] for b in range(nc // tq)], axis=1)
                elif fox:
                    s = s - fr_ref[h, j, :, c0:c0 + nc]
                if seen_from is not None:
                    s = jnp.where(seen, s, -jnp.inf)
                m_prev = m_sc[h, rows]
                m_new = jnp.maximum(m_prev, jnp.max(s, axis=1, keepdims=True))
                p = jnp.exp2((s - _tile_lanes(m_new, nc // LANES)).astype(BF16))
                vj = jnp.where(sum_lanes[h], one, vj)
                acc_sc[h, rows] = jnp.exp2(m_prev - m_new) * acc_sc[h, rows] + _dot(p, vj)
                m_sc[h, rows] = m_new

        def loop_body(jj, carry):
            block(4 * jj, 0, tq, 0, 4 * tq, None)
            return carry

        lax.fori_loop(0, i // 4, loop_body, 0)
        rest = i % 4

        @pl.when(rest >= 2)
        def _():
            block(i - rest, 0, tq, 0, 2 * tq, None)

        @pl.when(rest % 2 == 1)
        def _():
            block(i - 1, 0, tq, 0, tq, None)

        half = tq // 2
        if half % LANES == 0:
            block(i, 0, half, 0, half, 0)
            block(i, half, half, 0, tq, half)
        else:
            block(i, 0, tq, 0, tq, 0)
        res = []
        for h in range(nh):
            acc = acc_sc[h]
            swapped = pltpu.roll(acc, HEAD_DIM, 1)
            res.append(acc / swapped)
            lse2 = m_sc[h] + jnp.log(jnp.where(sum_lanes[h], acc, swapped)) * LOG2E
            lset_ref[h, 0] = _stat_rows(lse2)
        for pr in range(nh // 2):
            even = res[2 * pr] if fox else pltpu.roll(res[2 * pr], HEAD_DIM, 1)
            o_ref[:, pr * LANES:(pr + 1) * LANES] = jnp.where(lo, even, res[2 * pr + 1])

    if fox:
        in_specs = [pl.BlockSpec((tq, wide), lambda g, i: (i, g))] + [pl.BlockSpec((t, wide), lambda g, i: (0, g))] * 2
        in_specs += [pl.BlockSpec((nh, nq, 1, tq), lambda g, i: (g, 0, 0, 0))]
        args = [q, k, v, f2_rows]
    else:
        in_specs = [pl.BlockSpec((nh, tq, LANES), lambda g, i: (g, i, 0))] + [pl.BlockSpec((nh, t, LANES), lambda g, i: (g, 0, 0))] * 2
        args = [q, k, v]
    return _hosted_call(
        "fox_attn_fwd" if fox else "mla_attn_fwd", main, (HEADS // nh, nq), args, in_specs,
        (jax.ShapeDtypeStruct((t, 4 * LANES), F32), jax.ShapeDtypeStruct((HEADS, nq, 8, tq), F32)),
        (pl.BlockSpec((tq, wide), lambda g, i: (i, g)), pl.BlockSpec((nh, 1, 8, tq), lambda g, i: (g, i, 0, 0))),
        [pltpu.VMEM((nh, tq, LANES), F32), pltpu.VMEM((nh, tq, LANES), F32)], comm)


def _head_do(fox, hh, do2, lo):
    if fox:
        return jnp.where(lo if hh == 0 else jnp.logical_not(lo), do2, 0.0)
    return jnp.where(lo, 0.0, pltpu.roll(do2, HEAD_DIM, 1) if hh == 0 else do2)


BWD_HEADS = 4


def _attn_bwd(fox, q, k, v, do, lse_rows, delta_rows, f2_rep=None, comm=None):
    t = q.shape[0] if fox else q.shape[1]
    tq = _row_tile(t)
    nq = t // tq
    scale = FOX_SCALE if fox else MLA_SCALE
    nh = BWD_HEADS

    def main(ins, outs, scr):
        if fox:
            q_ref, k_ref, v_ref, f_ref, do_ref, lse_ref, dl_ref = ins
            dq_ref, dk_ref, dv_ref, dfq_ref, dfk_ref = outs
        else:
            q_ref, k_ref, v_ref, do_ref, lse_ref, dl_ref = ins
            dq_ref, dkv_ref, dkr_ref = outs
        dq_sc, dk_sc, dv_sc = scr
        j = pl.program_id(1)
        lane = _lane()
        lo = lane < HEAD_DIM
        hi = jnp.logical_not(lo)
        zero, one = jnp.zeros((), BF16), jnp.ones((), BF16)
        lanes_of = lambda h: slice((h // 2) * LANES, (h // 2 + 1) * LANES)

        @pl.when(j == 0)
        def _():
            dq_sc[...] = jnp.zeros_like(dq_sc)

        dk_sc[...] = jnp.zeros_like(dk_sc)
        dv_sc[...] = jnp.zeros_like(dv_sc)

        def block(i, r0, nr, c0, nc, masked):
            rows, cols = slice(r0, r0 + nr), slice(c0, c0 + nc)
            sl = pl.ds(pl.multiple_of(i * tq + c0, math.gcd(tq, c0) if c0 else tq), nc)
            if masked:
                seen = lax.broadcasted_iota(jnp.int32, (nr, nc), 1) >= lax.broadcasted_iota(jnp.int32, (nr, nc), 0)
            if nc > tq:
                stat = lambda ref, h: jnp.concatenate([ref[h, i + b, 0:1, :] for b in range(nc // tq)], axis=1)
            else:
                stat = lambda ref, h: ref[h, i, 0:1, cols]
            for h in range(nh):
                hh = h % 2
                kj = k_ref[rows, lanes_of(h)] if fox else k_ref[h, rows, :]
                vj = v_ref[rows, lanes_of(h)] if fox else v_ref[h, rows, :]
                qi = jnp.where(lo if hh == 0 else hi, q_ref[sl, lanes_of(h)], zero) if fox else q_ref[h, sl, :]
                dob = _head_do(fox, hh, do_ref[sl, lanes_of(h)], lo).astype(BF16)
                st = _dot_nt(kj, qi)
                if fox:
                    st = st - _tile_lanes(f_ref[h, rows, :], nc // LANES)
                if masked:
                    st = jnp.where(seen, st, -jnp.inf)
                pt = jnp.exp2(st - stat(lse_ref, h))
                dpt = _dot_nt(vj, dob)
                dst = (pt * (dpt - stat(dl_ref, h))).astype(BF16)
                dv_sc[h, rows] += _dot(pt.astype(BF16), dob)
                if fox:
                    other = hi if hh == 0 else lo
                    qi = jnp.where(other, one, qi)
                    kj = jnp.where(other, one, kj)
                dk_sc[h, rows] += _dot(dst, qi)
                dq_sc[h, sl, :] += _dot_tn(dst, kj)

        def loop_body(ii, carry):
            block(j + 1 + 2 * ii, 0, tq, 0, 2 * tq, False)
            return carry

        half = tq // 2
        if half % LANES == 0:
            block(j, 0, half, 0, tq, True)
            block(j, half, half, half, half, True)
        else:
            block(j, 0, tq, 0, tq, True)
        later = nq - 1 - j
        lax.fori_loop(0, later // 2, loop_body, 0)

        @pl.when(later % 2 == 1)
        def _():
            block(j + later, 0, tq, 0, tq, False)

        rope_lanes = jnp.logical_and(lane >= NOPE, lane < NOPE + ROPE)
        for pr in range(nh // 2):
            a, b = 2 * pr, 2 * pr + 1
            if fox:
                dk_ref[:, lanes_of(a)] = (jnp.where(lo, dk_sc[a], dk_sc[b]) * LN2).astype(BF16)
                dv_ref[:, lanes_of(a)] = (dv_sc[a] + dv_sc[b]).astype(BF16)
                for h in (a, b):
                    dk = dk_sc[h]
                    dfk_ref[h] = -jnp.where(hi if h == a else lo, dk, pltpu.roll(dk, HEAD_DIM, 1))
            else:
                dkr = jnp.zeros((tq, LANES), F32)
                for h in (a, b):
                    dk = dk_sc[h] * LN2
                    dkv_ref[h] = jnp.where(lo, dk, dv_sc[h])
                    dkr = dkr + jnp.where(rope_lanes, dk, 0.0)
                dkr_ref[pr] = dkr

        @pl.when(j == nq - 1)
        def _():
            for i in range(nq):
                rows = slice(i * tq, (i + 1) * tq)
                for pr in range(nh // 2):
                    a, b = 2 * pr, 2 * pr + 1
                    if fox:
                        dq_ref[rows, lanes_of(a)] = (jnp.where(lo, dq_sc[a, rows, :], dq_sc[b, rows, :]) * scale).astype(BF16)
                        for h in (a, b):
                            acc = dq_sc[h, rows, :]
                            dfq_ref[h, rows, :] = jnp.where(hi if h == a else lo, acc, pltpu.roll(acc, HEAD_DIM, 1))
                    else:
                        for h in (a, b):
                            dq_ref[h, rows, :] = dq_sc[h, rows, :] * scale

    wide_cols = (nh // 2) * LANES
    stat = pl.BlockSpec((nh, tq, LANES), lambda p, j: (p, j, 0))
    stat_all = pl.BlockSpec((nh, t, LANES), lambda p, j: (p, 0, 0))
    rows4 = pl.BlockSpec((nh, nq, 8, tq), lambda p, j: (p, 0, 0, 0))
    pair = pl.BlockSpec((tq, wide_cols), lambda p, j: (j, p))
    pair_all = pl.BlockSpec((t, wide_cols), lambda p, j: (0, p))
    if fox:
        in_specs = [pair_all, pair, pair, stat]
        args = [q, k, v, f2_rep]
    else:
        in_specs = [stat_all, stat, stat]
        args = [q, k, v]
    in_specs += [pair_all, rows4, rows4]
    args += [do, lse_rows, delta_rows]
    heads_f32 = jax.ShapeDtypeStruct((HEADS, t, LANES), F32)
    if fox:
        wide = jax.ShapeDtypeStruct((t, 4 * LANES), BF16)
        out_shape = (wide, wide, wide, heads_f32, heads_f32)
        out_specs = (pair_all, pair, pair, stat_all, stat)
    else:
        out_shape = (heads_f32, heads_f32, jax.ShapeDtypeStruct((HEADS // 2, t, LANES), F32))
        out_specs = (stat_all, stat, pl.BlockSpec((nh // 2, tq, LANES), lambda p, j: (p, j, 0)))
    acc = pltpu.VMEM((nh, tq, LANES), F32)
    return _hosted_call("fox_attn_bwd" if fox else "mla_attn_bwd", main, (HEADS // nh, nq), args, in_specs,
                        out_shape, out_specs, [pltpu.VMEM((nh, t, LANES), F32), acc, acc], comm)


def _attn_out(x, fox_o, mla_o, gf, gm, w_o):
    t = x.shape[0]
    tm = _row_tile(t)

    def body(x_ref, f_ref, m_ref, gf_ref, gm_ref, w_ref, x1_ref, mix_ref):
        nf, _ = _rms(f_ref[...], gf_ref[...])
        nm, _ = _rms(m_ref[...], gm_ref[...])
        nfb, nmb = nf.astype(BF16), nm.astype(BF16)
        mix_ref[:, :FOX_WIDTH] = nfb
        mix_ref[:, FOX_WIDTH:] = nmb
        x1_ref[...] = x_ref[...] + _dot(nfb, w_ref[:FOX_WIDTH, :]) + _dot(nmb, w_ref[FOX_WIDTH:, :])

    row = lambda n: pl.BlockSpec((tm, n), lambda i: (i, 0))
    full = lambda a: pl.BlockSpec(a.shape, lambda i: (0,) * a.ndim)
    return pl.pallas_call(
        body, name="attn_out", grid=(t // tm,),
        out_shape=(jax.ShapeDtypeStruct((t, D_MODEL), F32), jax.ShapeDtypeStruct((t, D_MODEL), BF16)),
        in_specs=[row(D_MODEL), row(FOX_WIDTH), row(MLA_WIDTH), full(gf), full(gm), full(w_o)],
        out_specs=(row(D_MODEL), row(D_MODEL)),
        compiler_params=_cparams("parallel"),
    )(x, fox_o, mla_o, gf, gm, w_o)


def _mlp_tile(t):
    return 256 if t >= 2048 else 128


def _resident(a):
    return pl.BlockSpec(a.shape, lambda i: (0,) * a.ndim, pipeline_mode=pl.Buffered(1))


FF_CHUNK = 512


def _mlp_fwd(x1, g_mlp, w_up, w_down, g_fin, target):
    t = x1.shape[0]
    tm = _mlp_tile(t)

    def body(x_ref, g_ref, wu_ref, wd_ref, gf_ref, t_ref, u_ref, h_ref, dx_ref, dxb_ref, loss_ref, dg_ref, a_sc):
        @pl.when(pl.program_id(0) == 0)
        def _():
            loss_ref[...] = jnp.zeros_like(loss_ref)
            dg_ref[...] = jnp.zeros_like(dg_ref)

        x = x_ref[...]
        h, _ = _rms(x, g_ref[...])
        hb = h.astype(BF16)
        h_ref[...] = hb
        for f in range(D_FF // FF_CHUNK):
            sl = slice(f * FF_CHUNK, (f + 1) * FF_CHUNK)
            u = _dot(hb, wu_ref[:, sl])
            u_ref[:, sl] = u
            r = jnp.maximum(u, 0.0)
            a_sc[:, sl] = (r * r).astype(BF16)
        x2 = x + _dot(a_sc[...], wd_ref[...])
        y, r2 = _rms(x2, gf_ref[...])
        err = y - t_ref[...]
        loss_ref[...] += 0.5 * jnp.sum(jnp.mean(err * err, axis=-1, keepdims=True))
        dx, dg = _rms_bwd(x2, gf_ref[...], r2, err * (1.0 / D_MODEL))
        dx_ref[...] = dx
        dxb_ref[...] = dx.astype(BF16)
        dg_ref[...] += dg

    row = lambda n: pl.BlockSpec((tm, n), lambda i: (i, 0))
    vec = pl.BlockSpec((1, D_MODEL), lambda i: (0, 0))
    return pl.pallas_call(
        body, name="mlp_fwd", grid=(t // tm,),
        out_shape=(jax.ShapeDtypeStruct((t, D_FF), F32), jax.ShapeDtypeStruct((t, D_MODEL), BF16),
                   jax.ShapeDtypeStruct((t, D_MODEL), F32), jax.ShapeDtypeStruct((t, D_MODEL), BF16),
                   jax.ShapeDtypeStruct((8, LANES), F32), jax.ShapeDtypeStruct((1, D_MODEL), F32)),
        in_specs=[row(D_MODEL), vec, _resident(w_up), _resident(w_down), vec, row(D_MODEL)],
        out_specs=(row(D_FF), row(D_MODEL), row(D_MODEL), row(D_MODEL), pl.BlockSpec((8, LANES), lambda i: (0, 0)), vec),
        scratch_shapes=[pltpu.VMEM((tm, D_FF), BF16)],
        compiler_params=_cparams("arbitrary"),
    )(x1, g_mlp, w_up, w_down, g_fin, target)


def _mlp_bwd(dx2, u, x1, g_mlp, w_up, w_down):
    t = x1.shape[0]
    tm = _mlp_tile(t)

    def body(dx_ref, u_ref, x_ref, g_ref, wu_ref, wd_ref, du_ref, a_ref, dx1_ref, dx1b_ref, dg_ref):
        @pl.when(pl.program_id(0) == 0)
        def _():
            dg_ref[...] = jnp.zeros_like(dg_ref)

        dx2 = dx_ref[...]
        dxb = dx2.astype(BF16)
        for f in range(D_FF // FF_CHUNK):
            sl = slice(f * FF_CHUNK, (f + 1) * FF_CHUNK)
            r = jnp.maximum(u_ref[:, sl], 0.0)
            a_ref[:, sl] = (r * r).astype(BF16)
            da = _dot_nt(dxb, wd_ref[sl, :])
            du_ref[:, sl] = (da * (2.0 * r)).astype(BF16)
        dh = _dot_nt(du_ref[...], wu_ref[...])
        x = x_ref[...]
        _, r1 = _rms(x, g_ref[...])
        dx, dg = _rms_bwd(x, g_ref[...], r1, dh)
        dx1 = dx2 + dx
        dx1_ref[...] = dx1
        dx1b_ref[...] = dx1.astype(BF16)
        dg_ref[...] += dg

    row = lambda n: pl.BlockSpec((tm, n), lambda i: (i, 0))
    vec = pl.BlockSpec((1, D_MODEL), lambda i: (0, 0))
    return pl.pallas_call(
        body, name="mlp_bwd", grid=(t // tm,),
        out_shape=(jax.ShapeDtypeStruct((t, D_FF), BF16), jax.ShapeDtypeStruct((t, D_FF), BF16),
                   jax.ShapeDtypeStruct((t, D_MODEL), F32), jax.ShapeDtypeStruct((t, D_MODEL), BF16),
                   jax.ShapeDtypeStruct((1, D_MODEL), F32)),
        in_specs=[row(D_MODEL), row(D_FF), row(D_MODEL), vec, _resident(w_up), _resident(w_down)],
        out_specs=(row(D_FF), row(D_FF), row(D_MODEL), row(D_MODEL), vec),
        compiler_params=_cparams("arbitrary"),
    )(dx2, u, x1, g_mlp, w_up, w_down)


def _matmul_tn(name, a, b, blocks=None):
    t, m = a.shape
    n = b.shape[1]
    tk = t if a.dtype == BF16 and b.dtype == BF16 else min(t, 2048)
    steps = t // tk
    bm = m if m <= 1024 else 512
    bn = n if n <= 1024 else 512
    width = bn if blocks is None else n // blocks
    per = bn // width

    def body(a_ref, b_ref, o_ref, acc_sc):
        kk = pl.program_id(2)

        @pl.when(kk == 0)
        def _():
            acc_sc[...] = jnp.zeros_like(acc_sc)

        acc_sc[...] += _dot_tn(a_ref[...].astype(BF16), b_ref[...].astype(BF16))

        @pl.when(kk == steps - 1)
        def _():
            if blocks is None:
                o_ref[...] = acc_sc[...]
            else:
                for s in range(per):
                    o_ref[s] = acc_sc[:, s * width:(s + 1) * width]

    if blocks is None:
        o_spec = pl.BlockSpec((bm, bn), lambda i, j, kk: (i, j))
        o_shape = (m, n)
    else:
        o_spec = pl.BlockSpec((per, bm, width), lambda i, j, kk: (j, i, 0))
        o_shape = (blocks, m, width)
    return pl.pallas_call(
        body, name=name, grid=(m // bm, n // bn, steps),
        out_shape=jax.ShapeDtypeStruct(o_shape, F32),
        in_specs=[pl.BlockSpec((tk, bm), lambda i, j, kk: (kk, i)), pl.BlockSpec((tk, bn), lambda i, j, kk: (kk, j))],
        out_specs=o_spec,
        scratch_shapes=[pltpu.VMEM((bm, bn), F32)],
        compiler_params=_cparams("parallel", "parallel", "arbitrary"),
    )(a, b)


def _dw_in(dfq, dfk, dfv, drest, h1):
    t = h1.shape[0]
    tk = min(t, 2048)
    off_ff = 3 * FOX_WIDTH
    off_cq = off_ff + HEADS
    off_kr = IN_COLS - ROPE

    def body(dq_ref, dk_ref, dv_ref, dr_ref, h_ref, o_ref):
        h = h_ref[...]
        r = _dot_tn(dr_ref[...], h)
        parts = [(slice(n * FOX_WIDTH, (n + 1) * FOX_WIDTH), _dot_tn(ref[...], h)) for n, ref in enumerate((dq_ref, dk_ref, dv_ref))]
        parts += [(slice(off_ff, off_cq), r[0:HEADS]), (slice(off_cq, off_kr), r[REST_CQ:REST_KR]),
                  (slice(off_kr, IN_COLS), r[REST_KR + NOPE:REST_KR + NOPE + ROPE])]

        @pl.when(pl.program_id(0) == 0)
        def _():
            for rows, val in parts:
                o_ref[rows, :] = val

        @pl.when(pl.program_id(0) > 0)
        def _():
            for rows, val in parts:
                o_ref[rows, :] += val

    tok = lambda n: pl.BlockSpec((tk, n), lambda kk: (kk, 0))
    return pl.pallas_call(
        body, name="dw_in", grid=(t // tk,),
        out_shape=jax.ShapeDtypeStruct((IN_COLS, D_MODEL), F32),
        in_specs=[tok(FOX_WIDTH), tok(FOX_WIDTH), tok(FOX_WIDTH), tok(REST_COLS), tok(D_MODEL)],
        out_specs=pl.BlockSpec((IN_COLS, D_MODEL), lambda kk: (0, 0)),
        compiler_params=_cparams("arbitrary"),
    )(dfq, dfk, dfv, drest, h1)


def _attn_out_bwd(dx1, fox_o, mla_o, gf, gm, w_o):
    t = dx1.shape[0]
    tm = _row_tile(t)

    def body(dx_ref, f_ref, m_ref, gf_ref, gm_ref, w_ref, df_ref, dm_ref, dlf_ref, dlm_ref, dgf_ref, dgm_ref):
        @pl.when(pl.program_id(0) == 0)
        def _():
            dgf_ref[...] = jnp.zeros_like(dgf_ref)
            dgm_ref[...] = jnp.zeros_like(dgm_ref)
        dxb = dx_ref[...].astype(BF16)
        lane = lax.broadcasted_iota(jnp.int32, (8, LANES), 1)
        upper = lax.broadcasted_iota(jnp.int32, (8, LANES), 0) < 4
        pick = jnp.where(upper, (lane < HEAD_DIM).astype(F32), (lane >= HEAD_DIM).astype(F32)).astype(BF16)
        for o_ref, g_ref, lo_row, d_ref, dl_ref, dg_ref in ((f_ref, gf_ref, 0, df_ref, dlf_ref, dgf_ref),
                                                             (m_ref, gm_ref, FOX_WIDTH, dm_ref, dlm_ref, dgm_ref)):
            dn = _dot_nt(dxb, w_ref[lo_row:lo_row + FOX_WIDTH, :])
            o = o_ref[...]
            _, r = _rms(o, g_ref[...])
            d, dg = _rms_bwd(o, g_ref[...], r, dn)
            d_ref[...] = d
            dg_ref[...] += dg
            prod = d * o
            for pr in range(HEADS // 2):
                parts = _split3(prod[:, pr * LANES:(pr + 1) * LANES])
                both = (_dot_nt(pick, parts[0]) + _dot_nt(pick, parts[1])) + _dot_nt(pick, parts[2])
                dl_ref[2 * pr, 0] = both
                dl_ref[2 * pr + 1, 0] = pltpu.roll(both, 4, 0)

    row = lambda n: pl.BlockSpec((tm, n), lambda i: (i, 0))
    full = lambda a: pl.BlockSpec(a.shape, lambda i: (0,) * a.ndim)
    vec = pl.BlockSpec((1, FOX_WIDTH), lambda i: (0, 0))
    rows = pl.BlockSpec((HEADS, 1, 8, tm), lambda i: (0, i, 0, 0))
    o_shape = jax.ShapeDtypeStruct((t, FOX_WIDTH), F32)
    g_shape = jax.ShapeDtypeStruct((1, FOX_WIDTH), F32)
    r_shape = jax.ShapeDtypeStruct((HEADS, t // tm, 8, tm), F32)
    return pl.pallas_call(
        body, name="attn_out_bwd", grid=(t // tm,),
        out_shape=(o_shape, o_shape, r_shape, r_shape, g_shape, g_shape),
        in_specs=[row(D_MODEL), row(FOX_WIDTH), row(MLA_WIDTH), full(gf), full(gm), full(w_o)],
        out_specs=(row(FOX_WIDTH), row(MLA_WIDTH), rows, rows, vec, vec),
        compiler_params=_cparams("arbitrary"),
    )(dx1, fox_o, mla_o, gf, gm, w_o)


def _mla_prep_bwd(dq, dkv, dkr, dz, rest, gq, gkv, wq, wkv, cos, sin):
    t = rest.shape[0]
    tm = _row_tile(t)

    def body(dq_ref, dkv_ref, dkr_ref, dz_ref, r_ref, gq_ref, gkv_ref, wq_ref, wkv_ref, c_ref, s_ref,
             dr_ref, dqp_ref, dkvb_ref, dgq_ref, dgkv_ref):
        @pl.when(pl.program_id(0) == 0)
        def _():
            dgq_ref[...] = jnp.zeros_like(dgq_ref)
            dgkv_ref[...] = jnp.zeros_like(dgkv_ref)
        cos_, sin_ = c_ref[...], s_ref[...]
        dcq = jnp.zeros((tm, Q_RANK), F32)
        dckv = jnp.zeros((tm, KV_RANK), F32)
        for h in range(HEADS):
            dqp = _rope_bwd(dq_ref[h], cos_, sin_).astype(BF16)
            dqp_ref[:, h * LANES:(h + 1) * LANES] = dqp
            dcq = dcq + _dot_nt(dqp, wq_ref[h])
            dkvb = dkv_ref[h].astype(BF16)
            dkvb_ref[:, h * LANES:(h + 1) * LANES] = dkvb
            dckv = dckv + _dot_nt(dkvb, wkv_ref[h])
        dkrope = dkr_ref[0]
        for pr in range(1, HEADS // 2):
            dkrope = dkrope + dkr_ref[pr]
        cq = r_ref[:, REST_CQ:REST_CKV]
        _, rq = _rms(cq, gq_ref[...])
        d_cq, dgq = _rms_bwd(cq, gq_ref[...], rq, dcq)
        ckv = r_ref[:, REST_CKV:REST_KR]
        _, rkv = _rms(ckv, gkv_ref[...])
        d_ckv, dgkv = _rms_bwd(ckv, gkv_ref[...], rkv, dckv)
        dgq_ref[...] += dgq
        dgkv_ref[...] += dgkv
        dr_ref[:, 0:REST_CQ] = dz_ref[...].astype(BF16)
        dr_ref[:, REST_CQ:REST_CKV] = d_cq.astype(BF16)
        dr_ref[:, REST_CKV:REST_KR] = d_ckv.astype(BF16)
        dr_ref[:, REST_KR:REST_COLS] = _rope_bwd(dkrope, cos_, sin_).astype(BF16)

    row = lambda n: pl.BlockSpec((tm, n), lambda i: (i, 0))
    full = lambda a: pl.BlockSpec(a.shape, lambda i: (0,) * a.ndim)
    heads = pl.BlockSpec((HEADS, tm, LANES), lambda i: (0, i, 0))
    hshape = jax.ShapeDtypeStruct((t, HEADS * LANES), BF16)
    return pl.pallas_call(
        body, name="mla_prep_bwd", grid=(t // tm,),
        out_shape=(jax.ShapeDtypeStruct((t, REST_COLS), BF16), hshape, hshape,
                   jax.ShapeDtypeStruct((1, Q_RANK), F32), jax.ShapeDtypeStruct((1, KV_RANK), F32)),
        in_specs=[heads, heads, pl.BlockSpec((HEADS // 2, tm, LANES), lambda i: (0, i, 0)), row(LANES), row(REST_COLS),
                  full(gq), full(gkv), full(wq), full(wkv), row(LANES), row(LANES)],
        out_specs=(row(REST_COLS), row(HEADS * LANES), row(HEADS * LANES), pl.BlockSpec((1, Q_RANK), lambda i: (0, 0)),
                   pl.BlockSpec((1, KV_RANK), lambda i: (0, 0))),
        compiler_params=_cparams("arbitrary"),
    )(dq, dkv, dkr, dz, rest, gq, gkv, wq, wkv, cos, sin)


def _in_proj_bwd(x, g, dx1, dfq, dfk, dfv, drest, w_qkv, w_rest, comm=None):
    t = x.shape[0]
    tm = _row_tile(t)

    def main(ins, outs, scr):
        x_ref, g_ref, dx1_ref, dq_ref, dk_ref, dv_ref, dr_ref, wq_ref, wr_ref = ins
        dx_ref, dg_ref = outs

        @pl.when(pl.program_id(0) == 0)
        def _():
            dg_ref[...] = jnp.zeros_like(dg_ref)
        dh = _dot(dr_ref[...], wr_ref[...])
        for n, ref in enumerate((dq_ref, dk_ref, dv_ref)):
            dh = dh + _dot(ref[...], wq_ref[n * FOX_WIDTH:(n + 1) * FOX_WIDTH, :])
        xv = x_ref[...]
        _, r = _rms(xv, g_ref[...])
        dx, dg = _rms_bwd(xv, g_ref[...], r, dh)
        dx_ref[...] = dx1_ref[...] + dx
        dg_ref[...] += dg

    row = lambda n: pl.BlockSpec((tm, n), lambda i: (i, 0))
    full = lambda a: pl.BlockSpec(a.shape, lambda i: (0,) * a.ndim)
    vec = pl.BlockSpec((1, D_MODEL), lambda i: (0, 0))
    return _hosted_call(
        "in_proj_bwd", main, (t // tm,), [x, g, dx1, dfq, dfk, dfv, drest, w_qkv, w_rest],
        [row(D_MODEL), full(g), row(D_MODEL), row(FOX_WIDTH), row(FOX_WIDTH), row(FOX_WIDTH), row(REST_COLS),
         full(w_qkv), full(w_rest)],
        (jax.ShapeDtypeStruct((t, D_MODEL), F32), jax.ShapeDtypeStruct((1, D_MODEL), F32)), (row(D_MODEL), vec), [], comm)


def _pad_cols(a, n):
    return jnp.pad(a, ((0, 0),) * (a.ndim - 1) + ((0, n - a.shape[-1]),))


def kernel(x, positions, attn_norm_g, w_in, b_forget, q_norm_g, w_uq, kv_norm_g, w_ukv, fox_out_g, mla_out_g, w_o, mlp_norm_g, w_up, w_down, final_norm_g, loss_target, m_attn_norm_g, m_w_in, m_b_forget, m_q_norm_g, m_w_uq, m_kv_norm_g, m_w_ukv, m_fox_out_g, m_mla_out_g, m_w_o, m_mlp_norm_g, m_w_up, m_w_down, m_final_norm_g, v_attn_norm_g, v_w_in, v_b_forget, v_q_norm_g, v_w_uq, v_kv_norm_g, v_w_ukv, v_fox_out_g, v_mla_out_g, v_w_o, v_mlp_norm_g, v_w_up, v_w_down, v_final_norm_g):
    t = x.shape[1]
    tq = _row_tile(t)
    xs = x[0]
    target = loss_target[0]

    mid = [_pad_cols(w_uq[0], LANES).astype(BF16), w_ukv[0].astype(BF16)]
    late = [w_o[0].astype(BF16), w_up[0].astype(BF16), w_down[0].astype(BF16)]
    g_in, = _all_gather([jnp.transpose(w_in[0]).astype(BF16)])
    win = g_in.reshape(IN_COLS, D_MODEL)
    off_ff, off_cq, off_kr = 3 * FOX_WIDTH, 3 * FOX_WIDTH + HEADS, IN_COLS - ROPE
    zeros = lambda n: jnp.zeros((n, D_MODEL), BF16)
    w_qkv = win[:off_ff]
    w_rest = jnp.concatenate([
        win[off_ff:off_cq], zeros(REST_CQ - HEADS), win[off_cq:off_kr],
        zeros(NOPE), win[off_kr:], zeros(LANES - NOPE - ROPE)], axis=0)

    cos, sin = _rope_tables(positions.reshape(t, 1))
    (h1, fq, fk, fv, rest), (wq, wkv) = _in_proj(xs, attn_norm_g, w_qkv, w_rest, comm=_ag_to_all(mid))
    b128 = _pad_cols(b_forget, LANES)
    f2_rows, f2_rep = _forget_cumsum(rest, b128)
    f2_rows = f2_rows.reshape(HEADS, t // tq, 1, tq)
    (fox_o, fox_lse_rows), partly = _attn_fwd(True, fq, fk, fv, f2_rows, comm=_ag_direct(late))
    mq, mk, mkv, cqn, ckvn = _mla_prep(rest, q_norm_g, kv_norm_g, wq, wkv, cos, sin)
    (mla_o, mla_lse_rows), (g_o, g_up, g_down) = _attn_fwd(False, mq, mk, mkv, comm=_ag_forward(partly))
    wo = g_o.reshape(D_MODEL, D_MODEL)
    x1, mixed = _attn_out(xs, fox_o, mla_o, fox_out_g, mla_out_g, wo)
    wup = jnp.transpose(g_up, (1, 0, 2)).reshape(D_MODEL, D_FF)
    wdown = g_down.reshape(D_FF, D_MODEL)
    u, h2, dx2, dx2b, loss8, d_gfin = _mlp_fwd(x1, mlp_norm_g, wup, wdown, final_norm_g.reshape(1, D_MODEL), target)

    du, act, dx1, dx1b, d_gmlp = _mlp_bwd(dx2, u, x1, mlp_norm_g, wup, wdown)
    dw_down = _matmul_tn("dw_down", act, dx2b)
    dw_up = _matmul_tn("dw_up", h2, du, blocks=N_DEV)
    dfox_o, dmla_o, fox_delta_rows, mla_delta_rows, d_gfox, d_gmla = _attn_out_bwd(dx1, fox_o, mla_o, fox_out_g, mla_out_g, wo)
    dw_o = _matmul_tn("dw_o", mixed, dx1b)

    place = jnp.stack([lax.axis_index("c"), 2 * lax.axis_index("x") + lax.axis_index("y")]).astype(jnp.int32)
    names = ("w_in", "w_uq", "w_ukv", "w_o", "w_up", "w_down")
    grads_b = [dw_o.reshape(N_DEV, -1, D_MODEL), dw_up, dw_down.reshape(N_DEV, -1, D_MODEL)]
    (dfq, dfk, dfv, d_fq, d_fk), got_b = _attn_bwd(True, fq, fk, fv, dfox_o, fox_lse_rows, fox_delta_rows,
                                                   f2_rep, comm=_rs_to_sibling(grads_b))
    sums_b = [_rs_sibling_sum("rs_sibling_sum_" + nm, g, l, place) for nm, g, l in zip(names[3:], grads_b, got_b)]
    dz, d_b = _forget_bwd(rest, b128, d_fq, d_fk)
    (dmq, dmkv, dmkr), others_b = _attn_bwd(False, mq, mk, mkv, dmla_o, mla_lse_rows, mla_delta_rows,
                                            comm=_rs_to_chips([s[1] for s in sums_b]))
    drest, dqp, dkvb, d_gq, d_gkv = _mla_prep_bwd(dmq, dmkv, dmkr, dz, rest, q_norm_g, kv_norm_g, wq, wkv, cos, sin)
    dw_uq = _matmul_tn("dw_uq", cqn, dqp, blocks=HEADS)
    dw_ukv = _matmul_tn("dw_ukv", ckvn, dkvb, blocks=HEADS)
    dw_in = _dw_in(dfq, dfk, dfv, drest, h1)

    grads_a = [dw_in.reshape(N_DEV, IN_SHARD, D_MODEL), dw_uq, dw_ukv]
    got_a = _comm_call("rs_sibling_exchange", _rs_to_sibling(grads_a))
    sums_a = [_rs_sibling_sum("rs_sibling_sum_" + nm, g, l, place) for nm, g, l in zip(names[:3], grads_a, got_a)]
    (grad_x, d_gattn), others_a = _in_proj_bwd(xs, attn_norm_g, dx1, dfq, dfk, dfv, drest, w_qkv, w_rest,
                                               comm=_rs_to_chips([s[1] for s in sums_a]))
    sums, others = sums_a + sums_b, list(others_a) + list(others_b)
    sharded = (w_in, w_uq, w_ukv, w_o, w_up, w_down)
    moments_m = (m_w_in, m_w_uq, m_w_ukv, m_w_o, m_w_up, m_w_down)
    moments_v = (v_w_in, v_w_uq, v_w_ukv, v_w_o, v_w_up, v_w_down)
    g_in_t = _rs_final_sum("rs_final_sum_w_in", sums[0][0], others[0])
    big = [_adamw_given("adamw_w_in", jnp.transpose(g_in_t), w_in, m_w_in, v_w_in)]
    for a in range(1, len(names)):
        big.append(_adamw_sharded("adamw_" + names[a], sharded[a], moments_m[a], moments_v[a], sums[a][0], others[a]))
    big_g, big_d, big_m, big_v = [[b[k] for b in big] for k in range(4)]

    as_row = lambda a: a.reshape(1, -1)
    small_w = (attn_norm_g, b_forget, q_norm_g, kv_norm_g, fox_out_g, mla_out_g, mlp_norm_g, final_norm_g)
    small_m = (m_attn_norm_g, m_b_forget, m_q_norm_g, m_kv_norm_g, m_fox_out_g, m_mla_out_g, m_mlp_norm_g, m_final_norm_g)
    small_v = (v_attn_norm_g, v_b_forget, v_q_norm_g, v_kv_norm_g, v_fox_out_g, v_mla_out_g, v_mlp_norm_g, v_final_norm_g)
    total = _small_all_reduce([d_gattn, d_b, d_gq, d_gkv, d_gfox, d_gmla, d_gmlp, d_gfin], loss8)
    small = _adamw_small(total, [as_row(a) for a in small_w], [as_row(a) for a in small_m], [as_row(a) for a in small_v])
    loss = small[0].reshape(())
    s_g, s_d, s_m, s_v = [[small[1 + 4 * r + k].reshape(small_w[r].shape) for r in range(len(small_w))] for k in range(4)]

    def ordered(small_, bigs):
        ga, bf, gq_, gkv_, gfo, gml, gmlp_, gfin_ = small_
        bin_, buq, bukv, bo, bup, bdown = bigs
        return [ga, bin_, bf, gq_, buq, gkv_, bukv, gfo, gml, bo, gmlp_, bup, bdown, gfin_]

    return (loss, grad_x[None], *ordered(s_g, big_g), *ordered(s_d, big_d), *ordered(s_m, big_m), *ordered(s_v, big_v))
```
